```python
import math
import jax, jax.numpy as jnp
from jax import lax
import numpy as np

D_MODEL = 1024
BATCH = 2
SEQ = 16384
DEPTH = 4

CHUNK = 64
Q_BLOCK = 128
NORM_EPS = 1e-6
DA_HEADS = 4
DA_DK = 64
DA_DV = 2 * DA_DK
HG_HEADS = 4
HG_DK = 128
HG_DV = 128
ML_HEADS = 4
ML_DK = 128
ML_DV = 128
ML_CONV = 4
N_BRANCH = 3
BRANCH_W = 512
D_FF = 2816
N_EXPERTS = 8
TOP_K = 2
N_DENSE = (DEPTH + 1) // 2
N_MOE = DEPTH // 2

A_QK = DA_HEADS * DA_DK
COLS = (('a_q1', A_QK), ('a_q2', A_QK), ('a_k1', A_QK), ('a_k2', A_QK),
        ('a_v', DA_HEADS * DA_DV),
        ('b_q', HG_HEADS * HG_DK), ('b_f', HG_HEADS * HG_DK),
        ('b_i', HG_HEADS * HG_DV), ('b_g', HG_HEADS * HG_DV),
        ('c_qk', 2 * ML_HEADS * ML_DK), ('c_v', ML_HEADS * ML_DV),
        ('c_o', ML_HEADS * ML_DV), ('c_i', ML_HEADS), ('c_f', ML_HEADS),
        ('gate', N_BRANCH * D_MODEL))
COL_NAMES = tuple(n for n, _ in COLS)
COL_OFFSETS = tuple(int(v) for v in np.cumsum([s for _, s in COLS])[:-1])
D_IN = sum(s for _, s in COLS)

kernel_name = 'hybrid_diffattn_hgrn2_mlstm_moe_block'


def _rmsnorm(x, w):
    xf = x.astype(jnp.float32)
    y = xf * lax.rsqrt(jnp.mean(xf * xf, axis=-1, keepdims=True) + NORM_EPS)
    return (y * w.astype(jnp.float32)).astype(x.dtype)


def _modulate(xn, shift, scale):
    return xn * (1.0 + scale[:, None, :]) + shift[:, None, :]


def _swiglu(h, w1, w3, w2):
    return (jax.nn.silu(h @ w1) * (h @ w3)) @ w2


def _causal_conv(u, w, b):
    K = w.shape[0]
    S = u.shape[1]
    up = jnp.pad(u, ((0, 0), (K - 1, 0), (0, 0)))
    y = up[:, 0:S] * w[0] + b
    for j in range(1, K):
        y = y + up[:, j:j + S] * w[j]
    return y


def _diff_attention(q1, q2, k1, k2, v, lam, lam_init, qn_w, kn_w, subln_w):
    B, S, H, dk = q1.shape
    dv = v.shape[-1]
    f32 = jnp.float32
    q = _rmsnorm(jnp.stack([q1, q2], axis=2).astype(f32), qn_w) * (dk ** -0.5)
    k = _rmsnorm(jnp.stack([k1, k2], axis=2).astype(f32), kn_w)
    q = q.transpose(0, 3, 2, 1, 4)
    k = k.transpose(0, 3, 2, 1, 4)
    vf = v.astype(f32).transpose(0, 2, 1, 3)
    slopes = jnp.asarray(2.0 ** (-8.0 * np.arange(1, H + 1) / H), dtype=f32)
    kpos = jnp.arange(S)
    kchunk = kpos // CHUNK

    def block(j):
        start = j * Q_BLOCK
        qb = lax.dynamic_slice_in_dim(q, start, Q_BLOCK, axis=3)
        qpos = start + jnp.arange(Q_BLOCK)
        s = jnp.einsum('bhmqd,bhmkd->bhmqk', qb, k)
        dist = jnp.abs(qpos[:, None] - kpos[None, :]).astype(f32)
        mask = kchunk[None, :] <= (qpos // CHUNK)[:, None]
        s = jnp.where(mask, s - (slopes[:, None, None] * dist)[None, :, None], -jnp.inf)
        p = jax.nn.softmax(s, axis=-1)
        a = p[:, :, 0] - lam * p[:, :, 1]
        return jnp.einsum('bhqk,bhkd->bhqd', a, vf)

    o = lax.map(block, jnp.arange(S // Q_BLOCK))
    o = o.transpose(1, 0, 3, 2, 4).reshape(B, S, H, dv)
    o = _rmsnorm(o, subln_w) * (1.0 - lam_init)
    return o.reshape(B, S, H * dv).astype(v.dtype)


def _hgrn2(q, f_pre, i, g, lb, gnorm_w):
    B, S, H, dk = q.shape
    dv = i.shape[-1]
    n = S // CHUNK
    f32 = jnp.float32
    qf = jax.nn.silu(q.astype(f32))
    lb = lb.astype(f32).reshape(H, dk)
    logf = jnp.logaddexp(jnp.log(lb), jnp.log1p(-lb) + jax.nn.log_sigmoid(f_pre.astype(f32)))
    kf = -jnp.expm1(logf)
    vf = i.astype(f32)

    def to_chunks(t):
        return t.reshape(B, n, CHUNK, H, t.shape[-1]).transpose(1, 0, 3, 2, 4)

    causal = jnp.tril(jnp.ones((CHUNK, CHUNK), dtype=bool))

    def step(state, inp):
        qc, kc, vc, lfc = inp
        b = jnp.cumsum(lfc, axis=2)
        o_inter = jnp.einsum('bhtd,bhde->bhte', qc * jnp.exp(b), state)
        diff = b[:, :, :, None, :] - b[:, :, None, :, :]
        dec = jnp.exp(jnp.where(causal[:, :, None], diff, -jnp.inf))
        att = jnp.einsum('bhtd,bhsd,bhtsd->bhts', qc, kc, dec)
        o_intra = jnp.einsum('bhts,bhse->bhte', att, vc)
        b_last = b[:, :, -1:, :]
        state = (jnp.exp(b_last[:, :, 0, :])[..., None] * state
                 + jnp.einsum('bhsd,bhse->bhde', kc * jnp.exp(b_last - b), vc))
        return state, o_inter + o_intra

    s0 = jnp.zeros((B, H, dk, dv), f32)
    _, o = lax.scan(step, s0, (to_chunks(qf), to_chunks(kf), to_chunks(vf), to_chunks(logf)))
    o = o.transpose(1, 0, 3, 2, 4).reshape(B, S, H, dv)
    o = _rmsnorm(o, gnorm_w) * jax.nn.silu(g.astype(f32))
    return o.reshape(B, S, H * dv).astype(g.dtype)


def _mlstm(q, k, v, o_pre, i_pre, f_pre, norm_w):
    B, S, H, dk = q.shape
    dv = v.shape[-1]
    n = S // CHUNK
    f32 = jnp.float32
    qf = q.astype(f32)
    kf = k.astype(f32) * (dk ** -0.5)
    vf = v.astype(f32)
    ig = i_pre.astype(f32)
    lf = jax.nn.log_sigmoid(f_pre.astype(f32))

    def to_chunks(t):
        return t.reshape(B, n, CHUNK, H, t.shape[-1]).transpose(1, 0, 3, 2, 4)

    def gate_chunks(t):
        return t.reshape(B, n, CHUNK, H).transpose(1, 0, 3, 2)

    causal = jnp.tril(jnp.ones((CHUNK, CHUNK), dtype=bool))

    def step(carry, inp):
        C, nv, m = carry
        qc, kc, vc, igc, lfc = inp
        b = jnp.cumsum(lfc, axis=-1)
        log_d = jnp.where(causal, b[..., :, None] - b[..., None, :] + igc[..., None, :], -jnp.inf)
        log_inter = b + m[..., None]
        m_t = jnp.maximum(log_inter, jnp.max(log_d, axis=-1))
        w_intra = jnp.exp(log_d - m_t[..., None]) * jnp.einsum('bhtd,bhsd->bhts', qc, kc)
        w_inter = jnp.exp(log_inter - m_t)
        num = (w_inter[..., None] * jnp.einsum('bhtd,bhde->bhte', qc, C)
               + jnp.einsum('bhts,bhse->bhte', w_intra, vc))
        dot = w_inter * jnp.einsum('bhtd,bhd->bht', qc, nv) + jnp.sum(w_intra, axis=-1)
        h = num / jnp.maximum(jnp.abs(dot), jnp.exp(-m_t))[..., None]
        g = b[..., -1]
        log_w = g[..., None] - b + igc
        m_new = jnp.maximum(g + m, jnp.max(log_w, axis=-1))
        w_s = jnp.exp(log_w - m_new[..., None])
        decay = jnp.exp(g + m - m_new)
        C = decay[..., None, None] * C + jnp.einsum('bhs,bhsd,bhse->bhde', w_s, kc, vc)
        nv = decay[..., None] * nv + jnp.einsum('bhs,bhsd->bhd', w_s, kc)
        return (C, nv, m_new), h

    init = (jnp.zeros((B, H, dk, dv), f32), jnp.zeros((B, H, dk), f32), jnp.zeros((B, H), f32))
    _, h = lax.scan(step, init, (to_chunks(qf), to_chunks(kf), to_chunks(vf),
                                 gate_chunks(ig), gate_chunks(lf)))
    h = h.transpose(1, 0, 3, 2, 4).reshape(B, S, H, dv)
    h = _rmsnorm(h, norm_w) * jax.nn.sigmoid(o_pre.astype(f32))
    return h.reshape(B, S, H * dv).astype(v.dtype)


def _mixer(h, w_in, conv_w, conv_b, qn_w, kn_w, lq1, lk1, lq2, lk2, subln_w, lam_init,
           lb, gnorm_w, ib, fb, cnorm_w, w_branch, w_out):
    B, S, _ = h.shape
    f32 = jnp.float32
    parts = dict(zip(COL_NAMES, jnp.split(w_in, COL_OFFSETS, axis=1)))

    def proj(name):
        return h @ parts[name]

    def heads(t, nh):
        return t.reshape(B, S, nh, -1)

    lam = (jnp.exp(jnp.sum(lq1.astype(f32) * lk1.astype(f32)))
           - jnp.exp(jnp.sum(lq2.astype(f32) * lk2.astype(f32))) + lam_init)
    y_a = _diff_attention(heads(proj('a_q1'), DA_HEADS), heads(proj('a_q2'), DA_HEADS),
                          heads(proj('a_k1'), DA_HEADS), heads(proj('a_k2'), DA_HEADS),
                          heads(proj('a_v'), DA_HEADS), lam, lam_init, qn_w, kn_w, subln_w)
    y_b = _hgrn2(heads(proj('b_q'), HG_HEADS), heads(proj('b_f'), HG_HEADS),
                 heads(proj('b_i'), HG_HEADS), heads(proj('b_g'), HG_HEADS), lb, gnorm_w)
    qk = jax.nn.silu(_causal_conv(proj('c_qk'), conv_w, conv_b))
    c_q, c_k = jnp.split(qk, 2, axis=-1)
    y_c = _mlstm(heads(c_q, ML_HEADS), heads(c_k, ML_HEADS), heads(proj('c_v'), ML_HEADS),
                 heads(proj('c_o'), ML_HEADS), proj('c_i') + ib, proj('c_f') + fb, cnorm_w)
    gates = jax.nn.sigmoid(proj('gate')).reshape(B, S, N_BRANCH, D_MODEL)
    merged = (gates[:, :, 0] * (y_a @ w_branch[0])
              + gates[:, :, 1] * (y_b @ w_branch[1])
              + gates[:, :, 2] * (y_c @ w_branch[2]))
    return merged @ w_out


def _moe(h, router_w, router_b, w1, w3, w2):
    f32 = jnp.float32
    logits = (h @ router_w).astype(f32) + router_b.astype(f32)
    probs = jax.nn.softmax(logits, axis=-1)
    top_p, top_i = lax.top_k(probs, TOP_K)
    top_p = top_p / jnp.sum(top_p, axis=-1, keepdims=True)
    combine = jnp.sum(jax.nn.one_hot(top_i, N_EXPERTS, dtype=f32) * top_p[..., None], axis=-2)
    out = jnp.zeros_like(h)
    for e in range(N_EXPERTS):
        out = out + combine[..., e:e + 1].astype(h.dtype) * _swiglu(h, w1[e], w3[e], w2[e])
    return out


def setup_inputs(seed: int = 0) -> dict:
    key = jax.random.key(seed)
    ks = jax.random.split(key, 32)
    f32 = jnp.float32
    D = D_MODEL

    def nrm(k, shape, scale):
        return scale * jax.random.normal(k, shape, f32)

    def gain(k, shape):
        return 1.0 + 0.02 * jax.random.normal(k, shape, f32)

    return {
        'x': nrm(ks[0], (BATCH, SEQ, D), 1.0),
        'c': nrm(ks[1], (BATCH, D), 1.0),
        'w_mod': nrm(ks[2], (DEPTH, D, 6 * D), 0.5 * D ** -0.5),
        'b_mod': nrm(ks[3], (DEPTH, 6 * D), 0.02),
        'norm1_w': gain(ks[4], (DEPTH, D)),
        'norm2_w': gain(ks[5], (DEPTH, D)),
        'w_in': nrm(ks[6], (DEPTH, D, D_IN), D ** -0.5),
        'c_conv_w': nrm(ks[7], (DEPTH, ML_CONV, 2 * ML_HEADS * ML_DK), ML_CONV ** -0.5),
        'c_conv_b': nrm(ks[8], (DEPTH, 2 * ML_HEADS * ML_DK), 0.02),
        'a_qnorm_w': gain(ks[9], (DEPTH, DA_DK)),
        'a_knorm_w': gain(ks[10], (DEPTH, DA_DK)),
        'a_lambda_q1': nrm(ks[11], (DEPTH, DA_DK), 0.1),
        'a_lambda_k1': nrm(ks[12], (DEPTH, DA_DK), 0.1),
        'a_lambda_q2': nrm(ks[13], (DEPTH, DA_DK), 0.1),
        'a_lambda_k2': nrm(ks[14], (DEPTH, DA_DK), 0.1),
        'a_subln_w': gain(ks[15], (DEPTH, DA_DV)),
        'b_lb_logits': nrm(ks[16], (DEPTH, HG_HEADS * HG_DK), 1.0),
        'b_gnorm_w': gain(ks[17], (DEPTH, HG_DV)),
        'c_igate_b': nrm(ks[18], (DEPTH, ML_HEADS), 0.1),
        'c_fgate_b': jnp.linspace(3.0, 6.0, ML_HEADS, dtype=f32)[None, :] + nrm(ks[19], (DEPTH, ML_HEADS), 0.1),
        'c_norm_w': gain(ks[20], (DEPTH, ML_DV)),
        'w_branch': nrm(ks[21], (DEPTH, N_BRANCH, BRANCH_W, D), BRANCH_W ** -0.5),
        'w_out': nrm(ks[22], (DEPTH, D, D), D ** -0.5),
        'ffn_w1': nrm(ks[23], (N_DENSE, D, D_FF), D ** -0.5),
        'ffn_w3': nrm(ks[24], (N_DENSE, D, D_FF), D ** -0.5),
        'ffn_w2': nrm(ks[25], (N_DENSE, D_FF, D), D_FF ** -0.5),
        'moe_router_w': nrm(ks[26], (N_MOE, D, N_EXPERTS), D ** -0.5),
        'moe_router_b': nrm(ks[27], (N_MOE, N_EXPERTS), 0.01),
        'moe_w1': nrm(ks[28], (N_MOE, N_EXPERTS, D, D_FF), D ** -0.5),
        'moe_w3': nrm(ks[29], (N_MOE, N_EXPERTS, D, D_FF), D ** -0.5),
        'moe_w2': nrm(ks[30], (N_MOE, N_EXPERTS, D_FF, D), D_FF ** -0.5),
    }


def reference(x, c, w_mod, b_mod, norm1_w, norm2_w, w_in, c_conv_w, c_conv_b,
              a_qnorm_w, a_knorm_w, a_lambda_q1, a_lambda_k1, a_lambda_q2, a_lambda_k2,
              a_subln_w, b_lb_logits, b_gnorm_w, c_igate_b, c_fgate_b, c_norm_w,
              w_branch, w_out, ffn_w1, ffn_w3, ffn_w2,
              moe_router_w, moe_router_b, moe_w1, moe_w3, moe_w2):
    lb_all = jnp.cumsum(jax.nn.softmax(b_lb_logits.astype(jnp.float32), axis=0), axis=0)
    lb_all = lb_all - lb_all[:1]
    cond = jax.nn.silu(c)
    for l in range(DEPTH):
        mod = cond @ w_mod[l] + b_mod[l]
        sh1, sc1, g1, sh2, sc2, g2 = jnp.split(mod, 6, axis=-1)
        lam_init = 0.8 - 0.6 * math.exp(-0.3 * l)
        h = _modulate(_rmsnorm(x, norm1_w[l]), sh1, sc1)
        y = _mixer(h, w_in[l], c_conv_w[l], c_conv_b[l], a_qnorm_w[l], a_knorm_w[l],
                   a_lambda_q1[l], a_lambda_k1[l], a_lambda_q2[l], a_lambda_k2[l],
                   a_subln_w[l], lam_init, lb_all[l], b_gnorm_w[l],
                   c_igate_b[l], c_fgate_b[l], c_norm_w[l], w_branch[l], w_out[l])
        x = x + g1[:, None, :] * y
        h = _modulate(_rmsnorm(x, norm2_w[l]), sh2, sc2)
        if l % 2 == 0:
            i = l // 2
            y = _swiglu(h, ffn_w1[i], ffn_w3[i], ffn_w2[i])
        else:
            i = l // 2
            y = _moe(h, moe_router_w[i], moe_router_b[i], moe_w1[i], moe_w3[i], moe_w2[i])
        x = x + g2[:, None, :] * y
    return x
```

```python
import functools
import math

import numpy as np
import jax
import jax.numpy as jnp
from jax import lax
from jax.experimental import pallas as pl
from jax.experimental.pallas import tpu as pltpu

F32 = jnp.float32
BF16 = jnp.bfloat16
NORM_EPS = 1e-6

CHUNK = 64
DA_HEADS, DA_DK, DA_DV = 4, 64, 128
HG_HEADS, HG_D = 4, 128
ML_HEADS, ML_D, ML_CONV = 4, 128, 4
N_EXPERTS = 8

LANES = 128
VMEM_LIMIT = 56 * 1024 * 1024

ATT_TQ = 256
HG_CHUNK = 128
HG_SUB = 32
HG_SAFE_DECAY = 80.0
ML_CHUNK = 256


def _cparams(sem, vmem=VMEM_LIMIT):
    return pltpu.CompilerParams(dimension_semantics=sem, vmem_limit_bytes=vmem)


def _dot(a, b):
    return jnp.dot(a, b, preferred_element_type=F32)


def _dot_nt(a, b):
    return lax.dot_general(a, b, (((1,), (1,)), ((), ())), preferred_element_type=F32)


def _split3(x):
    hi = x.astype(BF16)
    r1 = x - hi.astype(F32)
    mid = r1.astype(BF16)
    lo = (r1 - mid.astype(F32)).astype(BF16)
    return hi, mid, lo


def _norm_mod(x, w, scale, shift):
    y = x * lax.rsqrt(jnp.mean(x * x, axis=-1, keepdims=True) + NORM_EPS) * w
    return y * (1.0 + scale) + shift


def _log_sigmoid(x):
    return jnp.minimum(x, 0.0) - jnp.log1p(jnp.exp(-jnp.abs(x)))


def _mod_kernel(c_ref, w_ref, b_ref, o_ref):
    cnd = c_ref[...]
    cnd = cnd * jax.nn.sigmoid(cnd)
    o_ref[0] = _dot(cnd.astype(BF16), w_ref[0].astype(BF16)) + b_ref[0]


def _mod_call(c8, w_mod, b_mod):
    depth, d, n = w_mod.shape
    tn = 1536 if n % 1536 == 0 else n
    return pl.pallas_call(
        _mod_kernel,
        grid=(depth, n // tn),
        in_specs=[pl.BlockSpec((8, d), lambda l, j: (0, 0)),
                  pl.BlockSpec((1, d, tn), lambda l, j: (l, 0, j)),
                  pl.BlockSpec((1, 1, tn), lambda l, j: (l, 0, j))],
        out_specs=pl.BlockSpec((1, 8, tn), lambda l, j: (l, 0, j)),
        out_shape=jax.ShapeDtypeStruct((depth, 8, n), F32),
        compiler_params=_cparams(("arbitrary", "arbitrary")),
        name="adaln_mod",
    )(c8, w_mod, b_mod)


def _normmod_kernel(x_ref, nw_ref, mod_ref, h_ref):
    m = mod_ref[0]
    h_ref[...] = _norm_mod(x_ref[...], nw_ref[...], m[1:2], m[0:1]).astype(BF16)


def _normmod_call(x, nw, mod, seq, tm):
    t, d = x.shape
    per_b = seq // tm
    return pl.pallas_call(
        _normmod_kernel,
        grid=(t // tm,),
        in_specs=[pl.BlockSpec((tm, d), lambda i: (i, 0)),
                  pl.BlockSpec((1, d), lambda i: (0, 0)),
                  pl.BlockSpec((1, 6, d), lambda i: (i // per_b, 0, 0))],
        out_specs=pl.BlockSpec((tm, d), lambda i: (i, 0)),
        out_shape=jax.ShapeDtypeStruct((t, d), BF16),
        compiler_params=_cparams(("arbitrary",)),
        name="prenorm_mod",
    )(x, nw, mod)


def _mm_kernel(x_ref, w_ref, b_ref, o_ref):
    o_ref[...] = (_dot(x_ref[...], w_ref[...]) + b_ref[...]).astype(o_ref.dtype)


def _matmul(x, w, bias, out_dtype, tm, tn):
    t, k = x.shape
    n = w.shape[1]
    return pl.pallas_call(
        _mm_kernel,
        grid=(n // tn, t // tm),
        in_specs=[pl.BlockSpec((tm, k), lambda j, i: (i, 0)),
                  pl.BlockSpec((k, tn), lambda j, i: (0, j)),
                  pl.BlockSpec((1, tn), lambda j, i: (0, j))],
        out_specs=pl.BlockSpec((tm, tn), lambda j, i: (i, j)),
        out_shape=jax.ShapeDtypeStruct((t, n), out_dtype),
        compiler_params=_cparams(("arbitrary", "arbitrary")),
        name="in_proj",
    )(x, w, bias)


def _attn_prep_kernel(a_ref, qw_ref, kw_ref, kn_ref, qt_ref, vt_ref):
    hw = DA_HEADS * 2 * DA_DK
    a = a_ref[...]
    r = lax.broadcasted_iota(jnp.int32, (hw, hw), 0) // DA_DK
    c = lax.broadcasted_iota(jnp.int32, (hw, hw), 1) // DA_DK
    group = jnp.where(r == c, 1.0, 0.0).astype(BF16)

    def qk_norm(z, w):
        ms = _dot((z * z).astype(BF16), group) * (1.0 / DA_DK)
        return z * lax.rsqrt(ms + NORM_EPS) * w

    qn = qk_norm(a[:, :hw].astype(F32), qw_ref[...])
    kn = qk_norm(a[:, hw:2 * hw].astype(F32), kw_ref[...])
    kn_ref[...] = kn.astype(BF16)
    qt_ref[0] = qn.T.astype(BF16)
    vt_ref[0] = a[:, 2 * hw:].astype(F32).T.astype(BF16)


def _attn_prep_call(a, qw, kw, batch, seq, ts):
    t = a.shape[0]
    hw = DA_HEADS * 2 * DA_DK
    hv = DA_HEADS * DA_DV
    per_b = seq // ts
    return pl.pallas_call(
        _attn_prep_kernel,
        grid=(t // ts,),
        in_specs=[pl.BlockSpec((ts, 2 * hw + hv), lambda i: (i, 0)),
                  pl.BlockSpec((1, hw), lambda i: (0, 0)),
                  pl.BlockSpec((1, hw), lambda i: (0, 0))],
        out_specs=[pl.BlockSpec((ts, hw), lambda i: (i, 0)),
                   pl.BlockSpec((1, hw, ts), lambda i: (i // per_b, 0, i % per_b)),
                   pl.BlockSpec((1, hv, ts), lambda i: (i // per_b, 0, i % per_b))],
        out_shape=[jax.ShapeDtypeStruct((t, hw), BF16),
                   jax.ShapeDtypeStruct((batch, hw, seq), BF16),
                   jax.ShapeDtypeStruct((batch, hv, seq), BF16)],
        compiler_params=_cparams(("arbitrary",)),
        name="attn_prep",
    )(a, qw, kw)


def _attn_kernel(sc_ref, qt_ref, k_ref, vt_ref, sw_ref, o_ref, m_ref, l_ref, acc_ref, *, tq):
    h = pl.program_id(1)
    i = pl.program_id(2)
    slope = sc_ref[h]
    lam = sc_ref[DA_HEADS]
    out_scale = sc_ref[DA_HEADS + 1]

    qt = qt_ref[0]
    row = lax.broadcasted_iota(jnp.int32, qt.shape, 0)
    zero = jnp.zeros_like(qt)
    qz = jnp.concatenate([jnp.where(row < DA_DK, qt, zero),
                          jnp.where(row >= DA_DK, qt, zero)], axis=1)

    m_ref[...] = jnp.full(m_ref.shape, -jnp.inf, F32)
    l_ref[...] = jnp.zeros(l_ref.shape, F32)
    acc_ref[...] = jnp.zeros(acc_ref.shape, F32)

    def update(s, shift, vblk):
        m_old = m_ref[...]
        m_new = jnp.maximum(m_old, jnp.max(s, axis=0, keepdims=True) + shift)
        p = jnp.exp(s - (m_new - shift))
        alpha = jnp.exp(m_old - m_new)
        l_ref[...] = alpha * l_ref[...] + jnp.sum(p, axis=0, keepdims=True)
        acc_ref[...] = alpha * acc_ref[...] + _dot(vblk, p.astype(BF16))
        m_ref[...] = m_new

    key_off = slope * lax.broadcasted_iota(jnp.int32, (tq, 2 * tq), 0).astype(F32)

    def body(j, carry):
        start = pl.multiple_of(j * tq, tq)
        kblk = k_ref[pl.ds(start, tq), :]
        s = _dot(kblk, qz) + key_off
        shift = slope * ((j - i) * tq).astype(F32)
        update(s, shift, vt_ref[0, :, pl.ds(start, tq)])
        return carry

    lax.fori_loop(0, i, body, 0)

    start = pl.multiple_of(i * tq, tq)
    kblk = k_ref[pl.ds(start, tq), :]
    kk = lax.broadcasted_iota(jnp.int32, (tq, 2 * tq), 0)
    cc = lax.broadcasted_iota(jnp.int32, (tq, 2 * tq), 1)
    qq = jnp.where(cc >= tq, cc - tq, cc)
    bias = slope * (qq - jnp.abs(qq - kk)).astype(F32)
    visible = (kk // CHUNK) <= (qq // CHUNK)
    s = jnp.where(visible, _dot(kblk, qz) + bias, -jnp.inf)
    update(s, jnp.zeros((), F32), vt_ref[0, :, pl.ds(start, tq)])

    o2 = acc_ref[...] / l_ref[...]
    o = (o2[:, :tq] - lam * o2[:, tq:]).T
    o = o * lax.rsqrt(jnp.mean(o * o, axis=-1, keepdims=True) + NORM_EPS) * sw_ref[...]
    o_ref[...] = (o * out_scale).astype(o_ref.dtype)


def _attn_call(scal, qt, kn, vt, sw, batch, seq, tq):
    t = kn.shape[0]
    nq = seq // tq
    return pl.pallas_call(
        functools.partial(_attn_kernel, tq=tq),
        grid=(batch, DA_HEADS, nq),
        in_specs=[pl.BlockSpec(memory_space=pltpu.SMEM),
                  pl.BlockSpec((1, 2 * DA_DK, tq), lambda b, h, i: (b, h, i)),
                  pl.BlockSpec((seq, 2 * DA_DK), lambda b, h, i: (b, h)),
                  pl.BlockSpec((1, DA_DV, seq), lambda b, h, i: (b, h, 0)),
                  pl.BlockSpec((1, DA_DV), lambda b, h, i: (0, 0))],
        out_specs=pl.BlockSpec((tq, DA_DV), lambda b, h, i: (b * nq + i, h)),
        out_shape=jax.ShapeDtypeStruct((t, DA_HEADS * DA_DV), BF16),
        scratch_shapes=[pltpu.VMEM((1, 2 * tq), F32),
                        pltpu.VMEM((1, 2 * tq), F32),
                        pltpu.VMEM((DA_DV, 2 * tq), F32)],
        compiler_params=_cparams(("arbitrary", "arbitrary", "arbitrary")),
        name="diff_attn",
    )(scal, qt, kn, vt, sw)


def _hgrn_kernel(q_ref, f_ref, i_ref, g_ref, lb_ref, gw_ref, o_ref,
                 st_ref, b_ref, qs_ref, ks_ref, oi_ref, *, n_chunks):
    c = HG_CHUNK
    sub = HG_SUB

    @pl.when(pl.program_id(2) == 0)
    def _():
        st_ref[...] = jnp.zeros(st_ref.shape, F32)

    log_lb = lb_ref[0, 0:1, :]
    log_1m_lb = lb_ref[0, 1:2, :]
    gw = gw_ref[...]
    rr = lax.broadcasted_iota(jnp.int32, (c, c), 0)
    cc = lax.broadcasted_iota(jnp.int32, (c, c), 1)
    causal = cc <= rr
    tril = jnp.where(causal, 1.0, 0.0).astype(BF16)

    def chunk(n, carry):
        r0 = pl.multiple_of(n * c, c)
        q = q_ref[pl.ds(r0, c), :]
        q = q * jax.nn.sigmoid(q)
        a = log_1m_lb + _log_sigmoid(f_ref[pl.ds(r0, c), :])
        logf = jnp.maximum(log_lb, a) + jnp.log1p(jnp.exp(-jnp.abs(log_lb - a)))
        k = 1.0 - jnp.exp(logf)
        v = i_ref[pl.ds(r0, c), :]
        hi, mid, lo = _split3(logf)
        b = _dot(tril, hi) + _dot(tril, mid) + _dot(tril, lo)
        b_last = b[c - 1:c, :]
        st = st_ref[...]
        o_inter = _dot_nt((q * jnp.exp(b)).astype(BF16), st.astype(BF16))
        k_hat = k * jnp.exp(b_last - b)
        st_ref[...] = st * jnp.exp(b_last) + _dot(v.T.astype(BF16), k_hat.astype(BF16))

        decay = jnp.zeros((), F32)
        betas = []
        for blk in range(c // sub):
            beta = jnp.zeros((1, HG_D), F32) if blk == 0 else b[blk * sub - 1:blk * sub, :]
            betas.append(beta)
            b_end = b[(blk + 1) * sub - 1:(blk + 1) * sub, :]
            decay = jnp.maximum(decay, jnp.max(beta - b_end))

        @pl.when(decay < HG_SAFE_DECAY)
        def _():
            rows = []
            for blk in range(c // sub):
                beta = betas[blk]
                q_t = q[blk * sub:(blk + 1) * sub, :] * jnp.exp(b[blk * sub:(blk + 1) * sub, :] - beta)
                k_t = k * jnp.exp(jnp.minimum(beta - b, HG_SAFE_DECAY))
                rows.append(_dot_nt(q_t.astype(BF16), k_t.astype(BF16)))
            att = jnp.where(causal, jnp.concatenate(rows, axis=0), 0.0)
            oi_ref[...] = _dot(att.astype(BF16), v.astype(BF16))

        @pl.when(decay >= HG_SAFE_DECAY)
        def _():
            b_ref[...] = b
            qs_ref[...] = q
            ks_ref[...] = k
            ridx = lax.broadcasted_iota(jnp.int32, (c, HG_D), 0)

            def row(t, carry2):
                bt = b_ref[pl.ds(t, 1), :]
                e = jnp.exp(jnp.where(ridx <= t, bt - b_ref[...], -jnp.inf))
                w = jnp.sum(qs_ref[pl.ds(t, 1), :] * ks_ref[...] * e, axis=1, keepdims=True)
                oi_ref[pl.ds(t, 1), :] = jnp.sum(w * v, axis=0, keepdims=True)
                return carry2

            lax.fori_loop(0, c, row, 0)

        o = o_inter + oi_ref[...]
        o = o * lax.rsqrt(jnp.mean(o * o, axis=-1, keepdims=True) + NORM_EPS) * gw
        g = g_ref[pl.ds(r0, c), :]
        o_ref[pl.ds(r0, c), :] = (o * (g * jax.nn.sigmoid(g))).astype(o_ref.dtype)
        return carry

    lax.fori_loop(0, n_chunks, chunk, 0)


def _hgrn_call(bproj, lbs, gw, batch, seq, ts):
    t = bproj.shape[0]
    d = HG_D
    nh = HG_HEADS
    ns = seq // ts
    spec = lambda off: pl.BlockSpec((ts, d), lambda b, h, i: (b * ns + i, off + h))
    return pl.pallas_call(
        functools.partial(_hgrn_kernel, n_chunks=ts // HG_CHUNK),
        grid=(batch, nh, ns),
        in_specs=[spec(0), spec(nh), spec(2 * nh), spec(3 * nh),
                  pl.BlockSpec((1, 2, d), lambda b, h, i: (h, 0, 0)),
                  pl.BlockSpec((1, d), lambda b, h, i: (0, 0))],
        out_specs=pl.BlockSpec((ts, d), lambda b, h, i: (b * ns + i, h)),
        out_shape=jax.ShapeDtypeStruct((t, nh * d), BF16),
        scratch_shapes=[pltpu.VMEM((d, d), F32),
                        pltpu.VMEM((HG_CHUNK, d), F32),
                        pltpu.VMEM((HG_CHUNK, d), F32),
                        pltpu.VMEM((HG_CHUNK, d), F32),
                        pltpu.VMEM((HG_CHUNK, d), F32)],
        compiler_params=_cparams(("arbitrary", "arbitrary", "arbitrary")),
        name="hgrn2",
    )(bproj, bproj, bproj, bproj, lbs, gw)


def _mlstm_kernel(uq_ref, uk_ref, v_ref, op_ref, gi_ref, gf_ref, cwq_ref, cwk_ref,
                  cbq_ref, cbk_ref, nw_ref, o_ref,
                  xq_ref, xk_ref, c_ref, m_ref, bs_ref):
    L = ML_CHUNK
    d = ML_D
    i = pl.program_id(2)
    rr = lax.broadcasted_iota(jnp.int32, (L, L), 0)
    cc = lax.broadcasted_iota(jnp.int32, (L, L), 1)

    @pl.when(i == 0)
    def _():
        c_ref[...] = jnp.zeros(c_ref.shape, F32)
        m_ref[...] = jnp.zeros(m_ref.shape, F32)
        xq_ref[0:8, :] = jnp.zeros((8, d), F32)
        xk_ref[0:8, :] = jnp.zeros((8, d), F32)
        lf = _log_sigmoid(gf_ref[0, 0])
        upper = jnp.where(rr <= cc, 1.0, 0.0).astype(BF16)
        hi, mid, lo = _split3(lf)
        bs_ref[...] = _dot(hi, upper) + _dot(mid, upper) + _dot(lo, upper)

    def conv_silu(u_ref, x_ref, w_ref, b_ref):
        x_ref[8:8 + L, :] = u_ref[...]
        y = b_ref[...] + w_ref[ML_CONV - 1:ML_CONV, :] * x_ref[8:8 + L, :]
        for j in range(ML_CONV - 1):
            y = y + w_ref[j:j + 1, :] * x_ref[5 + j:5 + j + L, :]
        x_ref[0:8, :] = x_ref[L:L + 8, :]
        return y * jax.nn.sigmoid(y)

    q = conv_silu(uq_ref, xq_ref, cwq_ref, cbq_ref)
    k = conv_silu(uk_ref, xk_ref, cwk_ref, cbk_ref) * (d ** -0.5)
    kt = k.T
    lane = lax.broadcasted_iota(jnp.int32, (L, d), 1)
    v_aug = jnp.concatenate([v_ref[...], jnp.where(lane == 0, 1.0, 0.0)], axis=1).astype(BF16)

    b_row = bs_ref[pl.ds(i, 1), :]
    ig_row = gi_ref[0, 0, pl.ds(i, 1), :]
    m_prev = m_ref[...]
    g = b_row[:, L - 1:L]
    b_col = jnp.sum(jnp.where(rr == cc, b_row, 0.0), axis=1, keepdims=True)
    log_d = jnp.where(cc <= rr, b_col + (ig_row - b_row), -jnp.inf)
    log_inter = b_col + m_prev
    m_t = jnp.maximum(log_inter, jnp.max(log_d, axis=1, keepdims=True))
    w_intra = jnp.exp(log_d - m_t) * _dot(q.astype(BF16), kt.astype(BF16))
    w_inter = jnp.exp(log_inter - m_t)
    c_aug = c_ref[...]
    tot = (w_inter * _dot(q.astype(BF16), c_aug.astype(BF16))
           + _dot(w_intra.astype(BF16), v_aug))
    denom = jnp.maximum(jnp.abs(tot[:, d:d + 1]), jnp.exp(-m_t))
    hout = tot[:, :d] / denom
    hout = hout * lax.rsqrt(jnp.mean(hout * hout, axis=-1, keepdims=True) + NORM_EPS) * nw_ref[...]
    o_ref[...] = (hout * jax.nn.sigmoid(op_ref[...])).astype(o_ref.dtype)

    log_w = g - b_row + ig_row
    m_new = jnp.maximum(g + m_prev, jnp.max(log_w, axis=1, keepdims=True))
    w_s = jnp.exp(log_w - m_new)
    c_ref[...] = jnp.exp(g + m_prev - m_new) * c_aug + _dot((kt * w_s).astype(BF16), v_aug)
    m_ref[...] = m_new


def _mlstm_call(cproj, gi, gf, conv_w, conv_b, nw, batch, seq):
    t = cproj.shape[0]
    d = ML_D
    nh = ML_HEADS
    L = ML_CHUNK
    nc = seq // L
    spec = lambda off: pl.BlockSpec((L, d), lambda b, h, i: (b * nc + i, off + h))
    gspec = pl.BlockSpec((1, 1, nc, L), lambda b, h, i: (b, h, 0, 0))
    return pl.pallas_call(
        _mlstm_kernel,
        grid=(batch, nh, nc),
        in_specs=[spec(0), spec(nh), spec(2 * nh), spec(3 * nh), gspec, gspec,
                  pl.BlockSpec((ML_CONV, d), lambda b, h, i: (0, h)),
                  pl.BlockSpec((ML_CONV, d), lambda b, h, i: (0, nh + h)),
                  pl.BlockSpec((1, d), lambda b, h, i: (0, h)),
                  pl.BlockSpec((1, d), lambda b, h, i: (0, nh + h)),
                  pl.BlockSpec((1, d), lambda b, h, i: (0, 0))],
        out_specs=pl.BlockSpec((L, d), lambda b, h, i: (b * nc + i, h)),
        out_shape=jax.ShapeDtypeStruct((t, nh * d), BF16),
        scratch_shapes=[pltpu.VMEM((L + 8, d), F32),
                        pltpu.VMEM((L + 8, d), F32),
                        pltpu.VMEM((d, 2 * d), F32),
                        pltpu.VMEM((1, 1), F32),
                        pltpu.VMEM((nc, L), F32)],
        compiler_params=_cparams(("arbitrary", "arbitrary", "arbitrary")),
        name="mlstm",
    )(cproj, cproj, cproj, cproj, gi, gf, conv_w, conv_w, conv_b, conv_b, nw)


def _top2_combine(logits):
    lane = lax.broadcasted_iota(jnp.int32, logits.shape, 1)
    lg = jnp.where(lane < N_EXPERTS, logits, -jnp.inf)
    ex = jnp.exp(lg - jnp.max(lg, axis=1, keepdims=True))
    probs = ex / jnp.sum(ex, axis=1, keepdims=True)
    p1 = jnp.max(probs, axis=1, keepdims=True)
    i1 = jnp.min(jnp.where(probs == p1, lane, LANES), axis=1, keepdims=True)
    rest = jnp.where(lane == i1, -1.0, probs)
    p2 = jnp.max(rest, axis=1, keepdims=True)
    i2 = jnp.min(jnp.where(rest == p2, lane, LANES), axis=1, keepdims=True)
    comb = jnp.where(lane == i1, p1, 0.0) + jnp.where(lane == i2, p2, 0.0)
    return comb / (p1 + p2)


def _merge_kernel(*refs, route):
    if route:
        (ya_ref, yb_ref, yc_ref, gp_ref, wb_ref, wo_ref, x_ref, mod_ref, nw_ref,
         rw_ref, rb_ref, xo_ref, h_ref, cmb_ref) = refs
    else:
        (ya_ref, yb_ref, yc_ref, gp_ref, wb_ref, wo_ref, x_ref, mod_ref, nw_ref,
         xo_ref, h_ref) = refs
    d = x_ref.shape[1]
    merged = None
    for n, y_ref in enumerate((ya_ref, yb_ref, yc_ref)):
        gate = jax.nn.sigmoid(gp_ref[:, n * d:(n + 1) * d].astype(F32))
        term = gate * _dot(y_ref[...], wb_ref[n])
        merged = term if merged is None else merged + term
    m = mod_ref[0]
    xn = x_ref[...] + m[2:3] * _dot(merged.astype(BF16), wo_ref[...])
    xo_ref[...] = xn
    h2 = _norm_mod(xn, nw_ref[...], m[4:5], m[3:4])
    h_ref[...] = h2.astype(BF16)
    if route:
        h_hi, h_mid, _ = _split3(h2)
        r_hi, r_mid, _ = _split3(rw_ref[...])
        logits = _dot(h_hi, r_hi) + _dot(h_mid, r_hi) + _dot(h_hi, r_mid) + rb_ref[...]
        cmb_ref[...] = _top2_combine(logits)


def _merge_call(ya, yb, yc, gp, wb, wo, x, mod, nw, router, seq, tm):
    t, d = x.shape
    bw = ya.shape[1]
    per_b = seq // tm
    route = router is not None
    tok = lambda w: pl.BlockSpec((tm, w), lambda i: (i, 0))
    const2 = lambda s: pl.BlockSpec(s, lambda i: (0, 0))
    in_specs = [tok(bw), tok(bw), tok(bw), tok(3 * d),
                pl.BlockSpec((3, bw, d), lambda i: (0, 0, 0)), const2((d, d)), tok(d),
                pl.BlockSpec((1, 6, d), lambda i: (i // per_b, 0, 0)), const2((1, d))]
    out_specs = [tok(d), tok(d)]
    out_shape = [jax.ShapeDtypeStruct((t, d), F32), jax.ShapeDtypeStruct((t, d), BF16)]
    args = [ya, yb, yc, gp, wb, wo, x, mod, nw]
    if route:
        in_specs += [const2((d, LANES)), const2((1, LANES))]
        out_specs.append(tok(LANES))
        out_shape.append(jax.ShapeDtypeStruct((t, LANES), F32))
        args += list(router)
    return pl.pallas_call(
        functools.partial(_merge_kernel, route=route),
        grid=(t // tm,),
        in_specs=in_specs, out_specs=out_specs, out_shape=out_shape,
        compiler_params=_cparams(("arbitrary",)),
        name="merge_out",
    )(*args)


def _finish(acc, x_ref, mod_ref, xo_ref, nxt):
    xn = x_ref[...] + mod_ref[0][5:6] * acc
    xo_ref[...] = xn
    if nxt is not None:
        nw_ref, modn_ref, hn_ref = nxt
        mn = modn_ref[0]
        hn_ref[...] = _norm_mod(xn, nw_ref[...], mn[1:2], mn[0:1]).astype(BF16)


def _swiglu_partial(h, w1, w3, w2):
    a = _dot(h, w1)
    act = a * jax.nn.sigmoid(a) * _dot(h, w3)
    return act, w2


def _ffn_kernel(*refs, has_next):
    if has_next:
        h_ref, w1_ref, w3_ref, w2_ref, x_ref, mod_ref, nw_ref, modn_ref, xo_ref, hn_ref, acc_ref = refs
        nxt = (nw_ref, modn_ref, hn_ref)
    else:
        h_ref, w1_ref, w3_ref, w2_ref, x_ref, mod_ref, xo_ref, acc_ref = refs
        nxt = None
    f = pl.program_id(1)

    @pl.when(f == 0)
    def _():
        acc_ref[...] = jnp.zeros(acc_ref.shape, F32)

    h = h_ref[...]
    a = _dot(h, w1_ref[...])
    act = a * jax.nn.sigmoid(a) * _dot(h, w3_ref[...])
    acc_ref[...] += _dot(act.astype(BF16), w2_ref[...])

    @pl.when(f == pl.num_programs(1) - 1)
    def _():
        _finish(acc_ref[...], x_ref, mod_ref, xo_ref, nxt)


def _ffn_call(h, w1, w3, w2, x, mod, nxt, seq, tm, tf):
    t, d = x.shape
    ff = w1.shape[1]
    per_b = seq // tm
    tok = lambda: pl.BlockSpec((tm, d), lambda i, f: (i, 0))
    modspec = lambda: pl.BlockSpec((1, 6, d), lambda i, f: (i // per_b, 0, 0))
    in_specs = [tok(), pl.BlockSpec((d, tf), lambda i, f: (0, f)),
                pl.BlockSpec((d, tf), lambda i, f: (0, f)),
                pl.BlockSpec((tf, d), lambda i, f: (f, 0)), tok(), modspec()]
    out_specs = [tok()]
    out_shape = [jax.ShapeDtypeStruct((t, d), F32)]
    args = [h, w1, w3, w2, x, mod]
    if nxt is not None:
        in_specs += [pl.BlockSpec((1, d), lambda i, f: (0, 0)), modspec()]
        out_specs.append(tok())
        out_shape.append(jax.ShapeDtypeStruct((t, d), BF16))
        args += list(nxt)
    return pl.pallas_call(
        functools.partial(_ffn_kernel, has_next=nxt is not None),
        grid=(t // tm, ff // tf),
        in_specs=in_specs, out_specs=out_specs, out_shape=out_shape,
        scratch_shapes=[pltpu.VMEM((tm, d), F32)],
        compiler_params=_cparams(("arbitrary", "arbitrary")),
        name="ffn_swiglu",
    )(*args)


def _moe_kernel(*refs, has_next):
    if has_next:
        (h_ref, cmb_ref, w1_ref, w3_ref, w2_ref, x_ref, mod_ref, nw_ref, modn_ref,
         xo_ref, hn_ref, acc_ref) = refs
        nxt = (nw_ref, modn_ref, hn_ref)
    else:
        h_ref, cmb_ref, w1_ref, w3_ref, w2_ref, x_ref, mod_ref, xo_ref, acc_ref = refs
        nxt = None
    e = pl.program_id(1)
    f = pl.program_id(2)

    @pl.when((e == 0) & (f == 0))
    def _():
        acc_ref[...] = jnp.zeros(acc_ref.shape, F32)

    h = h_ref[...]
    cmb = cmb_ref[...]
    lane = lax.broadcasted_iota(jnp.int32, cmb.shape, 1)
    w_e = jnp.sum(jnp.where(lane == e, cmb, 0.0), axis=1, keepdims=True)
    a = _dot(h, w1_ref[0])
    act = a * jax.nn.sigmoid(a) * _dot(h, w3_ref[0]) * w_e
    acc_ref[...] += _dot(act.astype(BF16), w2_ref[0])

    @pl.when((e == pl.num_programs(1) - 1) & (f == pl.num_programs(2) - 1))
    def _():
        _finish(acc_ref[...], x_ref, mod_ref, xo_ref, nxt)


def _moe_call(h, cmb, w1, w3, w2, x, mod, nxt, seq, tm, tf):
    t, d = x.shape
    ne, _, ff = w1.shape
    per_b = seq // tm
    tok = lambda w=None: pl.BlockSpec((tm, d if w is None else w), lambda i, e, f: (i, 0))
    modspec = lambda: pl.BlockSpec((1, 6, d), lambda i, e, f: (i // per_b, 0, 0))
    in_specs = [tok(), tok(LANES),
                pl.BlockSpec((1, d, tf), lambda i, e, f: (e, 0, f)),
                pl.BlockSpec((1, d, tf), lambda i, e, f: (e, 0, f)),
                pl.BlockSpec((1, tf, d), lambda i, e, f: (e, f, 0)), tok(), modspec()]
    out_specs = [tok()]
    out_shape = [jax.ShapeDtypeStruct((t, d), F32)]
    args = [h, cmb, w1, w3, w2, x, mod]
    if nxt is not None:
        in_specs += [pl.BlockSpec((1, d), lambda i, e, f: (0, 0)), modspec()]
        out_specs.append(tok())
        out_shape.append(jax.ShapeDtypeStruct((t, d), BF16))
        args += list(nxt)
    return pl.pallas_call(
        functools.partial(_moe_kernel, has_next=nxt is not None),
        grid=(t // tm, ne, ff // tf),
        in_specs=in_specs, out_specs=out_specs, out_shape=out_shape,
        scratch_shapes=[pltpu.VMEM((tm, d), F32)],
        compiler_params=_cparams(("arbitrary", "arbitrary", "arbitrary")),
        name="moe_swiglu",
    )(*args)


def _attn_col_perm():
    hq = DA_HEADS * DA_DK
    idx = []
    for base in (0, 2 * hq):
        for h in range(DA_HEADS):
            idx += list(range(base + h * DA_DK, base + (h + 1) * DA_DK))
            idx += list(range(base + hq + h * DA_DK, base + hq + (h + 1) * DA_DK))
    idx += list(range(4 * hq, 4 * hq + DA_HEADS * DA_DV))
    return np.asarray(idx, np.int32)


def _pick(n, pref):
    for c in pref:
        if n % c == 0:
            return c
    return n


def kernel(x, c, w_mod, b_mod, norm1_w, norm2_w, w_in, c_conv_w, c_conv_b, a_qnorm_w, a_knorm_w,
           a_lambda_q1, a_lambda_k1, a_lambda_q2, a_lambda_k2, a_subln_w, b_lb_logits, b_gnorm_w,
           c_igate_b, c_fgate_b, c_norm_w, w_branch, w_out, ffn_w1, ffn_w3, ffn_w2,
           moe_router_w, moe_router_b, moe_w1, moe_w3, moe_w2):
    batch, seq, d = x.shape
    depth = w_in.shape[0]
    t = batch * seq
    n_a = 4 * DA_HEADS * DA_DK + DA_HEADS * DA_DV
    n_b = 4 * HG_HEADS * HG_D
    n_c = 4 * ML_HEADS * ML_D + 2 * ML_HEADS
    n_c_pad = -(-n_c // LANES) * LANES
    tm = _pick(seq, (1024, 512, 256))
    tq = _pick(seq, (ATT_TQ,))
    ts_h = _pick(seq, (1024, 512, 256, 128))

    c8 = jnp.zeros((8, d), F32).at[:batch].set(c)
    mod = _mod_call(c8, w_mod, b_mod.reshape(depth, 1, 6 * d))[:, :batch].reshape(depth, batch, 6, d)

    lb_all = jnp.cumsum(jax.nn.softmax(b_lb_logits.astype(F32), axis=0), axis=0)
    lb_all = lb_all - lb_all[:1]
    slopes = jnp.asarray(2.0 ** (-8.0 * np.arange(1, DA_HEADS + 1) / DA_HEADS), F32)

    xf = x.reshape(t, d)
    h = _normmod_call(xf, norm1_w[0].reshape(1, d), mod[0], seq, tm)
    perm_a = _attn_col_perm()
    for l in range(depth):
        lam_init = 0.8 - 0.6 * math.exp(-0.3 * l)
        wl = w_in[l]
        w_a = wl[:, :n_a][:, perm_a].astype(BF16)
        w_b = wl[:, n_a:n_a + n_b].astype(BF16)
        w_c = jnp.pad(wl[:, n_a + n_b:n_a + n_b + n_c], ((0, 0), (0, n_c_pad - n_c))).astype(BF16)
        w_g = wl[:, n_a + n_b + n_c:].astype(BF16)
        bias_c = jnp.zeros((1, n_c_pad), F32)
        bias_c = bias_c.at[0, n_c - 2 * ML_HEADS:n_c - ML_HEADS].set(c_igate_b[l])
        bias_c = bias_c.at[0, n_c - ML_HEADS:n_c].set(c_fgate_b[l])
        pa = _matmul(h, w_a, jnp.zeros((1, n_a), F32), BF16, tm, 512)
        pb = _matmul(h, w_b, jnp.zeros((1, n_b), F32), F32, tm, 512)
        pc = _matmul(h, w_c, bias_c, F32, 512, n_c_pad)
        pg = _matmul(h, w_g, jnp.zeros((1, 3 * d), F32), BF16, tm, 1024)

        qw = (jnp.tile(a_qnorm_w[l], 2 * DA_HEADS) * DA_DK ** -0.5).reshape(1, -1)
        kw = jnp.tile(a_knorm_w[l], 2 * DA_HEADS).reshape(1, -1)
        kn, qt, vt = _attn_prep_call(pa, qw, kw, batch, seq, _pick(seq, (512, 256)))
        lam = (jnp.exp(jnp.sum(a_lambda_q1[l] * a_lambda_k1[l]))
               - jnp.exp(jnp.sum(a_lambda_q2[l] * a_lambda_k2[l])) + lam_init)
        scal = jnp.concatenate([slopes, jnp.stack([lam, jnp.asarray(1.0 - lam_init, F32)])]).astype(F32)
        y_a = _attn_call(scal, qt, kn, vt, a_subln_w[l].reshape(1, -1), batch, seq, tq)

        lb = lb_all[l].reshape(HG_HEADS, 1, HG_D)
        lbs = jnp.concatenate([jnp.log(lb), jnp.log1p(-lb)], axis=1)
        y_b = _hgrn_call(pb, lbs, b_gnorm_w[l].reshape(1, -1), batch, seq, ts_h)

        gates = pc[:, n_c - 2 * ML_HEADS:n_c].reshape(batch, seq, 2, ML_HEADS)
        gates = gates.transpose(2, 0, 3, 1).reshape(2, batch, ML_HEADS, seq // ML_CHUNK, ML_CHUNK)
        y_c = _mlstm_call(pc, gates[0], gates[1], c_conv_w[l], c_conv_b[l].reshape(1, -1),
                          c_norm_w[l].reshape(1, -1), batch, seq)

        dense = l % 2 == 0
        router = None
        if not dense:
            rw = jnp.pad(moe_router_w[l // 2], ((0, 0), (0, LANES - N_EXPERTS)))
            rb = jnp.pad(moe_router_b[l // 2], (0, LANES - N_EXPERTS)).reshape(1, LANES)
            router = (rw, rb)
        outs = _merge_call(y_a, y_b, y_c, pg, w_branch[l].astype(BF16), w_out[l].astype(BF16), xf,
                           mod[l], norm2_w[l].reshape(1, d), router, seq, 512)
        xf, h2 = outs[0], outs[1]
        nxt = None if l == depth - 1 else (norm1_w[l + 1].reshape(1, d), mod[l + 1])
        if dense:
            res = _ffn_call(h2, ffn_w1[l // 2].astype(BF16), ffn_w3[l // 2].astype(BF16),
                            ffn_w2[l // 2].astype(BF16), xf, mod[l], nxt, seq, tm, 256)
        else:
            res = _moe_call(h2, outs[2], moe_w1[l // 2].astype(BF16), moe_w3[l // 2].astype(BF16),
                            moe_w2[l // 2].astype(BF16), xf, mod[l], nxt, seq, tm, 256)
        xf = res[0]
        if nxt is not None:
            h = res[1]
    return xf.reshape(batch, seq, d)
```

```python
import functools
import math

import numpy as np
import jax
import jax.numpy as jnp
from jax import lax
from jax.experimental import pallas as pl
from jax.experimental.pallas import tpu as pltpu

F32 = jnp.float32
BF16 = jnp.bfloat16
NORM_EPS = 1e-6

CHUNK = 64
DA_HEADS, DA_DK, DA_DV = 4, 64, 128
HG_HEADS, HG_D = 4, 128
ML_HEADS, ML_D, ML_CONV = 4, 128, 4
N_EXPERTS = 8

LANES = 128
VMEM_LIMIT = 56 * 1024 * 1024

ATT_TQ = 256
ATT_HEADS_PER_STEP = 2
ATT_PAD_ROWS = 16
LOG2E = math.log2(math.e)
ATT_BOUND_MARGIN = 1.01
ATT_BOUND_EPS = 1e-3
ATT_BOUND_MAX = 40.0
HG_CHUNK = 128
HG_SUB = 32
HG_SAFE_DECAY = 80.0
ML_CHUNK = 256


def _cparams(sem, vmem=VMEM_LIMIT):
    return pltpu.CompilerParams(dimension_semantics=sem, vmem_limit_bytes=vmem)


def _dot(a, b):
    return jnp.dot(a, b, preferred_element_type=F32)


def _dot_nt(a, b):
    return lax.dot_general(a, b, (((1,), (1,)), ((), ())), preferred_element_type=F32)


def _split3(x):
    hi = x.astype(BF16)
    r1 = x - hi.astype(F32)
    mid = r1.astype(BF16)
    lo = (r1 - mid.astype(F32)).astype(BF16)
    return hi, mid, lo


def _norm_mod(x, w, scale, shift):
    y = x * lax.rsqrt(jnp.mean(x * x, axis=-1, keepdims=True) + NORM_EPS) * w
    return y * (1.0 + scale) + shift


def _log_sigmoid(x):
    return jnp.minimum(x, 0.0) - jnp.log1p(jnp.exp(-jnp.abs(x)))


def _mod_kernel(c_ref, w_ref, b_ref, o_ref):
    cnd = c_ref[...]
    cnd = cnd * jax.nn.sigmoid(cnd)
    o_ref[0] = _dot(cnd.astype(BF16), w_ref[0].astype(BF16)) + b_ref[0]


def _mod_call(c8, w_mod, b_mod):
    depth, d, n = w_mod.shape
    tn = 1536 if n % 1536 == 0 else n
    return pl.pallas_call(
        _mod_kernel,
        grid=(depth, n // tn),
        in_specs=[pl.BlockSpec((8, d), lambda l, j: (0, 0)),
                  pl.BlockSpec((1, d, tn), lambda l, j: (l, 0, j)),
                  pl.BlockSpec((1, 1, tn), lambda l, j: (l, 0, j))],
        out_specs=pl.BlockSpec((1, 8, tn), lambda l, j: (l, 0, j)),
        out_shape=jax.ShapeDtypeStruct((depth, 8, n), F32),
        compiler_params=_cparams(("arbitrary", "arbitrary")),
        name="adaln_mod",
    )(c8, w_mod, b_mod)


def _normmod_kernel(x_ref, nw_ref, mod_ref, h_ref):
    m = mod_ref[0]
    h_ref[...] = _norm_mod(x_ref[...], nw_ref[...], m[1:2], m[0:1]).astype(BF16)


def _normmod_call(x, nw, mod, seq, tm):
    t, d = x.shape
    per_b = seq // tm
    return pl.pallas_call(
        _normmod_kernel,
        grid=(t // tm,),
        in_specs=[pl.BlockSpec((tm, d), lambda i: (i, 0)),
                  pl.BlockSpec((1, d), lambda i: (0, 0)),
                  pl.BlockSpec((1, 6, d), lambda i: (i // per_b, 0, 0))],
        out_specs=pl.BlockSpec((tm, d), lambda i: (i, 0)),
        out_shape=jax.ShapeDtypeStruct((t, d), BF16),
        compiler_params=_cparams(("arbitrary",)),
        name="prenorm_mod",
    )(x, nw, mod)


def _mm_kernel(x_ref, w_ref, b_ref, o_ref):
    o_ref[...] = (_dot(x_ref[...], w_ref[...]) + b_ref[...]).astype(o_ref.dtype)


def _matmul(x, w, bias, out_dtype, tm, tn):
    t, k = x.shape
    n = w.shape[1]
    return pl.pallas_call(
        _mm_kernel,
        grid=(n // tn, t // tm),
        in_specs=[pl.BlockSpec((tm, k), lambda j, i: (i, 0)),
                  pl.BlockSpec((k, tn), lambda j, i: (0, j)),
                  pl.BlockSpec((1, tn), lambda j, i: (0, j))],
        out_specs=pl.BlockSpec((tm, tn), lambda j, i: (i, j)),
        out_shape=jax.ShapeDtypeStruct((t, n), out_dtype),
        compiler_params=_cparams(("arbitrary", "arbitrary")),
        name="in_proj",
    )(x, w, bias)


def _attn_prep_kernel(a_ref, qw_ref, kw_ref, kn_ref, qt_ref, vt_ref):
    hw = DA_HEADS * 2 * DA_DK
    a = a_ref[...]
    r = lax.broadcasted_iota(jnp.int32, (hw, hw), 0) // DA_DK
    c = lax.broadcasted_iota(jnp.int32, (hw, hw), 1) // DA_DK
    group = jnp.where(r == c, 1.0, 0.0).astype(BF16)

    def qk_norm(z, w):
        ms = _dot((z * z).astype(BF16), group) * (1.0 / DA_DK)
        return z * lax.rsqrt(ms + NORM_EPS) * w

    qn = qk_norm(a[:, :hw].astype(F32), qw_ref[...])
    kn = qk_norm(a[:, hw:2 * hw].astype(F32), kw_ref[...])
    kn_ref[...] = kn.astype(BF16)
    qt_ref[0] = qn.T.astype(BF16)
    vt_ref[0] = a[:, 2 * hw:].astype(F32).T.astype(BF16)


def _attn_prep_call(a, qw, kw, batch, seq, ts):
    t = a.shape[0]
    hw = DA_HEADS * 2 * DA_DK
    hv = DA_HEADS * DA_DV
    per_b = seq // ts
    return pl.pallas_call(
        _attn_prep_kernel,
        grid=(t // ts,),
        in_specs=[pl.BlockSpec((ts, 2 * hw + hv), lambda i: (i, 0)),
                  pl.BlockSpec((1, hw), lambda i: (0, 0)),
                  pl.BlockSpec((1, hw), lambda i: (0, 0))],
        out_specs=[pl.BlockSpec((ts, hw), lambda i: (i, 0)),
                   pl.BlockSpec((1, hw, ts), lambda i: (i // per_b, 0, i % per_b)),
                   pl.BlockSpec((1, hv, ts), lambda i: (i // per_b, 0, i % per_b))],
        out_shape=[jax.ShapeDtypeStruct((t, hw), BF16),
                   jax.ShapeDtypeStruct((batch, hw, seq), BF16),
                   jax.ShapeDtypeStruct((batch, hv, seq), BF16)],
        compiler_params=_cparams(("arbitrary",)),
        name="attn_prep",
    )(a, qw, kw)


def _attn_kernel(sc_ref, qt_ref, k_ref, vt_ref, sw_ref, o_ref,
                 qz_ref, m_ref, acc_ref, vs_ref, kmax_ref, s_ref, *, tq, seq):
    hp = ATT_HEADS_PER_STEP
    dv = DA_DV
    g = pl.program_id(1)
    i = pl.program_id(2)
    lam = sc_ref[DA_HEADS]
    out_scale = sc_ref[DA_HEADS + 1]
    slopes = [sc_ref[hp * g + hh] for hh in range(hp)]
    row16 = lax.broadcasted_iota(jnp.int32, (ATT_PAD_ROWS, tq), 0)
    ones_rows = jnp.where(row16 == 0, 1.0, 0.0).astype(BF16)

    half = lax.broadcasted_iota(jnp.int32, (1, 2 * DA_DK), 1) < DA_DK

    @pl.when(i == 0)
    def _():
        pos = lax.broadcasted_iota(jnp.int32, (1, seq), 1) & (tq - 1)
        rel = (pos - (tq - 1)).astype(F32)
        sub16 = lax.broadcasted_iota(jnp.int32, (ATT_PAD_ROWS, seq), 0)
        gr = lax.broadcasted_iota(jnp.int32, (2 * DA_DK, 2 * DA_DK), 0) // DA_DK
        gc = lax.broadcasted_iota(jnp.int32, (2 * DA_DK, 2 * DA_DK), 1) // DA_DK
        group = jnp.where(gr == gc, 1.0, 0.0).astype(BF16)
        for hh in range(hp):
            w = jnp.exp2(slopes[hh] * rel)
            vs_ref[hh, 0:dv, :] = (vt_ref[0, hh * dv:(hh + 1) * dv, :].astype(F32) * w).astype(BF16)
            vs_ref[hh, dv:dv + ATT_PAD_ROWS, :] = jnp.where(sub16 == 0, w, 0.0).astype(BF16)

            def knorm(n, best):
                kc = k_ref[pl.ds(pl.multiple_of(n * tq, tq), tq),
                           hh * 2 * DA_DK:(hh + 1) * 2 * DA_DK].astype(F32)
                return jnp.maximum(best, jnp.max(_dot((kc * kc).astype(BF16), group), axis=0, keepdims=True))

            k2 = lax.fori_loop(0, seq // tq, knorm, jnp.zeros((1, 2 * DA_DK), F32))
            kmax_ref[2 * hh] = jnp.max(jnp.where(half, k2, 0.0), axis=1, keepdims=True)
            kmax_ref[2 * hh + 1] = jnp.max(jnp.where(half, 0.0, k2), axis=1, keepdims=True)

    row = lax.broadcasted_iota(jnp.int32, (2 * DA_DK, tq), 0)
    qq_row = lax.broadcasted_iota(jnp.int32, (1, tq), 1).astype(F32)
    bound_max = jnp.zeros((), F32)
    for hh in range(hp):
        qt = qt_ref[0, hh * 2 * DA_DK:(hh + 1) * 2 * DA_DK, :]
        zero = jnp.zeros_like(qt)
        q1 = jnp.where(row < DA_DK, qt, zero)
        q2 = jnp.where(row >= DA_DK, qt, zero)
        qz_ref[hh] = jnp.concatenate([q1, q2], axis=1)
        qn = jnp.concatenate(
            [jnp.sum(jnp.square(q1.astype(F32)), axis=0, keepdims=True) * kmax_ref[2 * hh],
             jnp.sum(jnp.square(q2.astype(F32)), axis=0, keepdims=True) * kmax_ref[2 * hh + 1]], axis=1)
        bound = jnp.sqrt(qn) * ATT_BOUND_MARGIN + ATT_BOUND_EPS
        bound_max = jnp.maximum(bound_max, jnp.max(bound))
        m_ref[hh] = bound + slopes[hh] * jnp.concatenate([qq_row, qq_row], axis=1)
    acc_ref[...] = jnp.zeros(acc_ref.shape, F32)

    kk = lax.broadcasted_iota(jnp.int32, (tq, 2 * tq), 0)
    cc = lax.broadcasted_iota(jnp.int32, (tq, 2 * tq), 1)
    qq = jnp.where(cc >= tq, cc - tq, cc)
    dist = (qq - jnp.abs(qq - kk)).astype(F32)
    visible = (kk // CHUNK) <= (qq // CHUNK)

    def scores(hh, blk):
        start = pl.multiple_of(blk * tq, tq)
        return _dot(k_ref[pl.ds(start, tq), hh * 2 * DA_DK:(hh + 1) * 2 * DA_DK], qz_ref[hh])

    def diag_values(hh):
        start = pl.multiple_of(i * tq, tq)
        return jnp.concatenate([vt_ref[0, hh * dv:(hh + 1) * dv, pl.ds(start, tq)], ones_rows], axis=0)

    @pl.when(bound_max <= ATT_BOUND_MAX)
    def _():
        for hh in range(hp):
            s_ref[hh] = scores(hh, 0)

        def body(j, carry):
            start = pl.multiple_of(j * tq, tq)
            last_key = ((j + 1 - i) * tq - 1).astype(F32)
            for hh in range(hp):
                p = jnp.exp2(s_ref[hh] - (m_ref[hh] - slopes[hh] * last_key)).astype(BF16)
                s_ref[hh] = scores(hh, j + 1)
                acc_ref[hh] += _dot(vs_ref[hh, :, pl.ds(start, tq)], p)
            return carry

        lax.fori_loop(0, i, body, 0)
        for hh in range(hp):
            s = jnp.where(visible, s_ref[hh] + slopes[hh] * dist, -jnp.inf)
            acc_ref[hh] += _dot(diag_values(hh), jnp.exp2(s - m_ref[hh]).astype(BF16))

    @pl.when(bound_max > ATT_BOUND_MAX)
    def _():
        m_ref[...] = jnp.full(m_ref.shape, -jnp.inf, F32)

        def update(hh, s, bound_shift, v_aug):
            m_old = m_ref[hh]
            m_new = jnp.maximum(m_old, jnp.max(s, axis=0, keepdims=True) + bound_shift)
            p = jnp.exp2(s - (m_new - bound_shift))
            acc_ref[hh] = jnp.exp2(m_old - m_new) * acc_ref[hh] + _dot(v_aug, p.astype(BF16))
            m_ref[hh] = m_new

        def body(j, carry):
            start = pl.multiple_of(j * tq, tq)
            last_key = ((j + 1 - i) * tq - 1).astype(F32)
            for hh in range(hp):
                update(hh, scores(hh, j), slopes[hh] * last_key, vs_ref[hh, :, pl.ds(start, tq)])
            return carry

        lax.fori_loop(0, i, body, 0)
        for hh in range(hp):
            s = jnp.where(visible, scores(hh, i) + slopes[hh] * dist, -jnp.inf)
            update(hh, s, jnp.zeros((), F32), diag_values(hh))

    outs = []
    for hh in range(hp):
        acc = acc_ref[hh]
        o2 = acc[:dv, :] * (1.0 / acc[dv:dv + 1, :])
        o = (o2[:, :tq] - lam * o2[:, tq:]).T
        o = o * lax.rsqrt(jnp.mean(o * o, axis=-1, keepdims=True) + NORM_EPS) * sw_ref[...]
        outs.append(o * out_scale)
    o_ref[...] = jnp.concatenate(outs, axis=1).astype(o_ref.dtype)


def _attn_call(scal, qt, kn, vt, sw, batch, seq, tq):
    t = kn.shape[0]
    nq = seq // tq
    hp = ATT_HEADS_PER_STEP
    return pl.pallas_call(
        functools.partial(_attn_kernel, tq=tq, seq=seq),
        grid=(batch, DA_HEADS // hp, nq),
        in_specs=[pl.BlockSpec(memory_space=pltpu.SMEM),
                  pl.BlockSpec((1, hp * 2 * DA_DK, tq), lambda b, g, i: (b, g, i)),
                  pl.BlockSpec((seq, hp * 2 * DA_DK), lambda b, g, i: (b, g)),
                  pl.BlockSpec((1, hp * DA_DV, seq), lambda b, g, i: (b, g, 0)),
                  pl.BlockSpec((1, DA_DV), lambda b, g, i: (0, 0))],
        out_specs=pl.BlockSpec((tq, hp * DA_DV), lambda b, g, i: (b * nq + i, g)),
        out_shape=jax.ShapeDtypeStruct((t, DA_HEADS * DA_DV), BF16),
        scratch_shapes=[pltpu.VMEM((hp, 2 * DA_DK, 2 * tq), BF16),
                        pltpu.VMEM((hp, 1, 2 * tq), F32),
                        pltpu.VMEM((hp, DA_DV + ATT_PAD_ROWS, 2 * tq), F32),
                        pltpu.VMEM((hp, DA_DV + ATT_PAD_ROWS, seq), BF16),
                        pltpu.VMEM((2 * hp, 1, 1), F32),
                        pltpu.VMEM((hp, tq, 2 * tq), F32)],
        compiler_params=_cparams(("arbitrary", "arbitrary", "arbitrary")),
        name="diff_attn",
    )(scal, qt, kn, vt, sw)


def _hgrn_kernel(q_ref, f_ref, i_ref, g_ref, lb_ref, gw_ref, o_ref,
                 st_ref, b_ref, qs_ref, ks_ref, oi_ref, *, n_chunks):
    c = HG_CHUNK
    sub = HG_SUB

    @pl.when(pl.program_id(2) == 0)
    def _():
        st_ref[...] = jnp.zeros(st_ref.shape, F32)

    log_lb = lb_ref[0, 0:1, :]
    log_1m_lb = lb_ref[0, 1:2, :]
    gw = gw_ref[...]
    rr = lax.broadcasted_iota(jnp.int32, (c, c), 0)
    cc = lax.broadcasted_iota(jnp.int32, (c, c), 1)
    causal = cc <= rr
    tril = jnp.where(causal, 1.0, 0.0).astype(BF16)

    def chunk(n, carry):
        r0 = pl.multiple_of(n * c, c)
        q = q_ref[pl.ds(r0, c), :]
        q = q * jax.nn.sigmoid(q)
        a = log_1m_lb + _log_sigmoid(f_ref[pl.ds(r0, c), :])
        logf = jnp.maximum(log_lb, a) + jnp.log1p(jnp.exp(-jnp.abs(log_lb - a)))
        k = 1.0 - jnp.exp(logf)
        v = i_ref[pl.ds(r0, c), :]
        hi, mid, lo = _split3(logf)
        b = _dot(tril, hi) + _dot(tril, mid) + _dot(tril, lo)
        b_last = b[c - 1:c, :]
        st = st_ref[...]
        o_inter = _dot_nt((q * jnp.exp(b)).astype(BF16), st.astype(BF16))
        k_hat = k * jnp.exp(b_last - b)
        st_ref[...] = st * jnp.exp(b_last) + _dot(v.T.astype(BF16), k_hat.astype(BF16))

        decay = jnp.zeros((), F32)
        betas = []
        for blk in range(c // sub):
            beta = jnp.zeros((1, HG_D), F32) if blk == 0 else b[blk * sub - 1:blk * sub, :]
            betas.append(beta)
            b_end = b[(blk + 1) * sub - 1:(blk + 1) * sub, :]
            decay = jnp.maximum(decay, jnp.max(beta - b_end))

        @pl.when(decay < HG_SAFE_DECAY)
        def _():
            rows = []
            for blk in range(c // sub):
                beta = betas[blk]
                q_t = q[blk * sub:(blk + 1) * sub, :] * jnp.exp(b[blk * sub:(blk + 1) * sub, :] - beta)
                k_t = k * jnp.exp(jnp.minimum(beta - b, HG_SAFE_DECAY))
                rows.append(_dot_nt(q_t.astype(BF16), k_t.astype(BF16)))
            att = jnp.where(causal, jnp.concatenate(rows, axis=0), 0.0)
            oi_ref[...] = _dot(att.astype(BF16), v.astype(BF16))

        @pl.when(decay >= HG_SAFE_DECAY)
        def _():
            b_ref[...] = b
            qs_ref[...] = q
            ks_ref[...] = k
            ridx = lax.broadcasted_iota(jnp.int32, (c, HG_D), 0)

            def row(t, carry2):
                bt = b_ref[pl.ds(t, 1), :]
                e = jnp.exp(jnp.where(ridx <= t, bt - b_ref[...], -jnp.inf))
                w = jnp.sum(qs_ref[pl.ds(t, 1), :] * ks_ref[...] * e, axis=1, keepdims=True)
                oi_ref[pl.ds(t, 1), :] = jnp.sum(w * v, axis=0, keepdims=True)
                return carry2

            lax.fori_loop(0, c, row, 0)

        o = o_inter + oi_ref[...]
        o = o * lax.rsqrt(jnp.mean(o * o, axis=-1, keepdims=True) + NORM_EPS) * gw
        g = g_ref[pl.ds(r0, c), :]
        o_ref[pl.ds(r0, c), :] = (o * (g * jax.nn.sigmoid(g))).astype(o_ref.dtype)
        return carry

    lax.fori_loop(0, n_chunks, chunk, 0)


def _hgrn_call(bproj, lbs, gw, batch, seq, ts):
    t = bproj.shape[0]
    d = HG_D
    nh = HG_HEADS
    ns = seq // ts
    spec = lambda off: pl.BlockSpec((ts, d), lambda b, h, i: (b * ns + i, off + h))
    return pl.pallas_call(
        functools.partial(_hgrn_kernel, n_chunks=ts // HG_CHUNK),
        grid=(batch, nh, ns),
        in_specs=[spec(0), spec(nh), spec(2 * nh), spec(3 * nh),
                  pl.BlockSpec((1, 2, d), lambda b, h, i: (h, 0, 0)),
                  pl.BlockSpec((1, d), lambda b, h, i: (0, 0))],
        out_specs=pl.BlockSpec((ts, d), lambda b, h, i: (b * ns + i, h)),
        out_shape=jax.ShapeDtypeStruct((t, nh * d), BF16),
        scratch_shapes=[pltpu.VMEM((d, d), F32),
                        pltpu.VMEM((HG_CHUNK, d), F32),
                        pltpu.VMEM((HG_CHUNK, d), F32),
                        pltpu.VMEM((HG_CHUNK, d), F32),
                        pltpu.VMEM((HG_CHUNK, d), F32)],
        compiler_params=_cparams(("arbitrary", "arbitrary", "arbitrary")),
        name="hgrn2",
    )(bproj, bproj, bproj, bproj, lbs, gw)


def _mlstm_kernel(uq_ref, uk_ref, v_ref, op_ref, gi_ref, gf_ref, cwq_ref, cwk_ref,
                  cbq_ref, cbk_ref, nw_ref, o_ref,
                  xq_ref, xk_ref, c_ref, m_ref, bs_ref):
    L = ML_CHUNK
    d = ML_D
    i = pl.program_id(2)
    rr = lax.broadcasted_iota(jnp.int32, (L, L), 0)
    cc = lax.broadcasted_iota(jnp.int32, (L, L), 1)

    @pl.when(i == 0)
    def _():
        c_ref[...] = jnp.zeros(c_ref.shape, F32)
        m_ref[...] = jnp.zeros(m_ref.shape, F32)
        xq_ref[0:8, :] = jnp.zeros((8, d), F32)
        xk_ref[0:8, :] = jnp.zeros((8, d), F32)
        lf = _log_sigmoid(gf_ref[0, 0])
        upper = jnp.where(rr <= cc, 1.0, 0.0).astype(BF16)
        hi, mid, lo = _split3(lf)
        bs_ref[...] = _dot(hi, upper) + _dot(mid, upper) + _dot(lo, upper)

    def conv_silu(u_ref, x_ref, w_ref, b_ref):
        x_ref[8:8 + L, :] = u_ref[...]
        y = b_ref[...] + w_ref[ML_CONV - 1:ML_CONV, :] * x_ref[8:8 + L, :]
        for j in range(ML_CONV - 1):
            y = y + w_ref[j:j + 1, :] * x_ref[5 + j:5 + j + L, :]
        x_ref[0:8, :] = x_ref[L:L + 8, :]
        return y * jax.nn.sigmoid(y)

    q = conv_silu(uq_ref, xq_ref, cwq_ref, cbq_ref)
    k = conv_silu(uk_ref, xk_ref, cwk_ref, cbk_ref) * (d ** -0.5)
    kt = k.T
    lane = lax.broadcasted_iota(jnp.int32, (L, d), 1)
    v_aug = jnp.concatenate([v_ref[...], jnp.where(lane == 0, 1.0, 0.0)], axis=1).astype(BF16)

    b_row = bs_ref[pl.ds(i, 1), :]
    ig_row = gi_ref[0, 0, pl.ds(i, 1), :]
    m_prev = m_ref[...]
    g = b_row[:, L - 1:L]
    b_col = jnp.sum(jnp.where(rr == cc, b_row, 0.0), axis=1, keepdims=True)
    log_d = jnp.where(cc <= rr, b_col + (ig_row - b_row), -jnp.inf)
    log_inter = b_col + m_prev
    m_t = jnp.maximum(log_inter, jnp.max(log_d, axis=1, keepdims=True))
    w_intra = jnp.exp(log_d - m_t) * _dot(q.astype(BF16), kt.astype(BF16))
    w_inter = jnp.exp(log_inter - m_t)
    c_aug = c_ref[...]
    tot = (w_inter * _dot(q.astype(BF16), c_aug.astype(BF16))
           + _dot(w_intra.astype(BF16), v_aug))
    denom = jnp.maximum(jnp.abs(tot[:, d:d + 1]), jnp.exp(-m_t))
    hout = tot[:, :d] / denom
    hout = hout * lax.rsqrt(jnp.mean(hout * hout, axis=-1, keepdims=True) + NORM_EPS) * nw_ref[...]
    o_ref[...] = (hout * jax.nn.sigmoid(op_ref[...])).astype(o_ref.dtype)

    log_w = g - b_row + ig_row
    m_new = jnp.maximum(g + m_prev, jnp.max(log_w, axis=1, keepdims=True))
    w_s = jnp.exp(log_w - m_new)
    c_ref[...] = jnp.exp(g + m_prev - m_new) * c_aug + _dot((kt * w_s).astype(BF16), v_aug)
    m_ref[...] = m_new


def _mlstm_call(cproj, gi, gf, conv_w, conv_b, nw, batch, seq):
    t = cproj.shape[0]
    d = ML_D
    nh = ML_HEADS
    L = ML_CHUNK
    nc = seq // L
    spec = lambda off: pl.BlockSpec((L, d), lambda b, h, i: (b * nc + i, off + h))
    gspec = pl.BlockSpec((1, 1, nc, L), lambda b, h, i: (b, h, 0, 0))
    return pl.pallas_call(
        _mlstm_kernel,
        grid=(batch, nh, nc),
        in_specs=[spec(0), spec(nh), spec(2 * nh), spec(3 * nh), gspec, gspec,
                  pl.BlockSpec((ML_CONV, d), lambda b, h, i: (0, h)),
                  pl.BlockSpec((ML_CONV, d), lambda b, h, i: (0, nh + h)),
                  pl.BlockSpec((1, d), lambda b, h, i: (0, h)),
                  pl.BlockSpec((1, d), lambda b, h, i: (0, nh + h)),
                  pl.BlockSpec((1, d), lambda b, h, i: (0, 0))],
        out_specs=pl.BlockSpec((L, d), lambda b, h, i: (b * nc + i, h)),
        out_shape=jax.ShapeDtypeStruct((t, nh * d), BF16),
        scratch_shapes=[pltpu.VMEM((L + 8, d), F32),
                        pltpu.VMEM((L + 8, d), F32),
                        pltpu.VMEM((d, 2 * d), F32),
                        pltpu.VMEM((1, 1), F32),
                        pltpu.VMEM((nc, L), F32)],
        compiler_params=_cparams(("arbitrary", "arbitrary", "arbitrary")),
        name="mlstm",
    )(cproj, cproj, cproj, cproj, gi, gf, conv_w, conv_w, conv_b, conv_b, nw)


def _top2_combine(logits):
    lane = lax.broadcasted_iota(jnp.int32, logits.shape, 1)
    lg = jnp.where(lane < N_EXPERTS, logits, -jnp.inf)
    ex = jnp.exp(lg - jnp.max(lg, axis=1, keepdims=True))
    probs = ex / jnp.sum(ex, axis=1, keepdims=True)
    p1 = jnp.max(probs, axis=1, keepdims=True)
    i1 = jnp.min(jnp.where(probs == p1, lane, LANES), axis=1, keepdims=True)
    rest = jnp.where(lane == i1, -1.0, probs)
    p2 = jnp.max(rest, axis=1, keepdims=True)
    i2 = jnp.min(jnp.where(rest == p2, lane, LANES), axis=1, keepdims=True)
    comb = jnp.where(lane == i1, p1, 0.0) + jnp.where(lane == i2, p2, 0.0)
    return comb / (p1 + p2)


def _merge_kernel(*refs, route):
    if route:
        (ya_ref, yb_ref, yc_ref, gp_ref, wb_ref, wo_ref, x_ref, mod_ref, nw_ref,
         rw_ref, rb_ref, xo_ref, h_ref, cmb_ref) = refs
    else:
        (ya_ref, yb_ref, yc_ref, gp_ref, wb_ref, wo_ref, x_ref, mod_ref, nw_ref,
         xo_ref, h_ref) = refs
    d = x_ref.shape[1]
    merged = None
    for n, y_ref in enumerate((ya_ref, yb_ref, yc_ref)):
        gate = jax.nn.sigmoid(gp_ref[:, n * d:(n + 1) * d].astype(F32))
        term = gate * _dot(y_ref[...], wb_ref[n])
        merged = term if merged is None else merged + term
    m = mod_ref[0]
    xn = x_ref[...] + m[2:3] * _dot(merged.astype(BF16), wo_ref[...])
    xo_ref[...] = xn
    h2 = _norm_mod(xn, nw_ref[...], m[4:5], m[3:4])
    h_ref[...] = h2.astype(BF16)
    if route:
        h_hi, h_mid, _ = _split3(h2)
        r_hi, r_mid, _ = _split3(rw_ref[...])
        logits = _dot(h_hi, r_hi) + _dot(h_mid, r_hi) + _dot(h_hi, r_mid) + rb_ref[...]
        cmb_ref[...] = _top2_combine(logits)


def _merge_call(ya, yb, yc, gp, wb, wo, x, mod, nw, router, seq, tm):
    t, d = x.shape
    bw = ya.shape[1]
    per_b = seq // tm
    route = router is not None
    tok = lambda w: pl.BlockSpec((tm, w), lambda i: (i, 0))
    const2 = lambda s: pl.BlockSpec(s, lambda i: (0, 0))
    in_specs = [tok(bw), tok(bw), tok(bw), tok(3 * d),
                pl.BlockSpec((3, bw, d), lambda i: (0, 0, 0)), const2((d, d)), tok(d),
                pl.BlockSpec((1, 6, d), lambda i: (i // per_b, 0, 0)), const2((1, d))]
    out_specs = [tok(d), tok(d)]
    out_shape = [jax.ShapeDtypeStruct((t, d), F32), jax.ShapeDtypeStruct((t, d), BF16)]
    args = [ya, yb, yc, gp, wb, wo, x, mod, nw]
    if route:
        in_specs += [const2((d, LANES)), const2((1, LANES))]
        out_specs.append(tok(LANES))
        out_shape.append(jax.ShapeDtypeStruct((t, LANES), F32))
        args += list(router)
    return pl.pallas_call(
        functools.partial(_merge_kernel, route=route),
        grid=(t // tm,),
        in_specs=in_specs, out_specs=out_specs, out_shape=out_shape,
        compiler_params=_cparams(("arbitrary",)),
        name="merge_out",
    )(*args)


def _finish(acc, x_ref, mod_ref, xo_ref, nxt):
    xn = x_ref[...] + mod_ref[0][5:6] * acc
    xo_ref[...] = xn
    if nxt is not None:
        nw_ref, modn_ref, hn_ref = nxt
        mn = modn_ref[0]
        hn_ref[...] = _norm_mod(xn, nw_ref[...], mn[1:2], mn[0:1]).astype(BF16)


def _swiglu_partial(h, w1, w3, w2):
    a = _dot(h, w1)
    act = a * jax.nn.sigmoid(a) * _dot(h, w3)
    return act, w2


def _ffn_kernel(*refs, has_next):
    if has_next:
        h_ref, w1_ref, w3_ref, w2_ref, x_ref, mod_ref, nw_ref, modn_ref, xo_ref, hn_ref, acc_ref = refs
        nxt = (nw_ref, modn_ref, hn_ref)
    else:
        h_ref, w1_ref, w3_ref, w2_ref, x_ref, mod_ref, xo_ref, acc_ref = refs
        nxt = None
    f = pl.program_id(1)

    @pl.when(f == 0)
    def _():
        acc_ref[...] = jnp.zeros(acc_ref.shape, F32)

    h = h_ref[...]
    a = _dot(h, w1_ref[...])
    act = a * jax.nn.sigmoid(a) * _dot(h, w3_ref[...])
    acc_ref[...] += _dot(act.astype(BF16), w2_ref[...])

    @pl.when(f == pl.num_programs(1) - 1)
    def _():
        _finish(acc_ref[...], x_ref, mod_ref, xo_ref, nxt)


def _ffn_call(h, w1, w3, w2, x, mod, nxt, seq, tm, tf):
    t, d = x.shape
    ff = w1.shape[1]
    per_b = seq // tm
    tok = lambda: pl.BlockSpec((tm, d), lambda i, f: (i, 0))
    modspec = lambda: pl.BlockSpec((1, 6, d), lambda i, f: (i // per_b, 0, 0))
    in_specs = [tok(), pl.BlockSpec((d, tf), lambda i, f: (0, f)),
                pl.BlockSpec((d, tf), lambda i, f: (0, f)),
                pl.BlockSpec((tf, d), lambda i, f: (f, 0)), tok(), modspec()]
    out_specs = [tok()]
    out_shape = [jax.ShapeDtypeStruct((t, d), F32)]
    args = [h, w1, w3, w2, x, mod]
    if nxt is not None:
        in_specs += [pl.BlockSpec((1, d), lambda i, f: (0, 0)), modspec()]
        out_specs.append(tok())
        out_shape.append(jax.ShapeDtypeStruct((t, d), BF16))
        args += list(nxt)
    return pl.pallas_call(
        functools.partial(_ffn_kernel, has_next=nxt is not None),
        grid=(t // tm, ff // tf),
        in_specs=in_specs, out_specs=out_specs, out_shape=out_shape,
        scratch_shapes=[pltpu.VMEM((tm, d), F32)],
        compiler_params=_cparams(("arbitrary", "arbitrary")),
        name="ffn_swiglu",
    )(*args)


def _moe_kernel(*refs, has_next):
    if has_next:
        (h_ref, cmb_ref, w1_ref, w3_ref, w2_ref, x_ref, mod_ref, nw_ref, modn_ref,
         xo_ref, hn_ref, acc_ref) = refs
        nxt = (nw_ref, modn_ref, hn_ref)
    else:
        h_ref, cmb_ref, w1_ref, w3_ref, w2_ref, x_ref, mod_ref, xo_ref, acc_ref = refs
        nxt = None
    e = pl.program_id(1)
    f = pl.program_id(2)

    @pl.when((e == 0) & (f == 0))
    def _():
        acc_ref[...] = jnp.zeros(acc_ref.shape, F32)

    h = h_ref[...]
    cmb = cmb_ref[...]
    lane = lax.broadcasted_iota(jnp.int32, cmb.shape, 1)
    w_e = jnp.sum(jnp.where(lane == e, cmb, 0.0), axis=1, keepdims=True)
    a = _dot(h, w1_ref[0])
    act = a * jax.nn.sigmoid(a) * _dot(h, w3_ref[0]) * w_e
    acc_ref[...] += _dot(act.astype(BF16), w2_ref[0])

    @pl.when((e == pl.num_programs(1) - 1) & (f == pl.num_programs(2) - 1))
    def _():
        _finish(acc_ref[...], x_ref, mod_ref, xo_ref, nxt)


def _moe_call(h, cmb, w1, w3, w2, x, mod, nxt, seq, tm, tf):
    t, d = x.shape
    ne, _, ff = w1.shape
    per_b = seq // tm
    tok = lambda w=None: pl.BlockSpec((tm, d if w is None else w), lambda i, e, f: (i, 0))
    modspec = lambda: pl.BlockSpec((1, 6, d), lambda i, e, f: (i // per_b, 0, 0))
    in_specs = [tok(), tok(LANES),
                pl.BlockSpec((1, d, tf), lambda i, e, f: (e, 0, f)),
                pl.BlockSpec((1, d, tf), lambda i, e, f: (e, 0, f)),
                pl.BlockSpec((1, tf, d), lambda i, e, f: (e, f, 0)), tok(), modspec()]
    out_specs = [tok()]
    out_shape = [jax.ShapeDtypeStruct((t, d), F32)]
    args = [h, cmb, w1, w3, w2, x, mod]
    if nxt is not None:
        in_specs += [pl.BlockSpec((1, d), lambda i, e, f: (0, 0)), modspec()]
        out_specs.append(tok())
        out_shape.append(jax.ShapeDtypeStruct((t, d), BF16))
        args += list(nxt)
    return pl.pallas_call(
        functools.partial(_moe_kernel, has_next=nxt is not None),
        grid=(t // tm, ne, ff // tf),
        in_specs=in_specs, out_specs=out_specs, out_shape=out_shape,
        scratch_shapes=[pltpu.VMEM((tm, d), F32)],
        compiler_params=_cparams(("arbitrary", "arbitrary", "arbitrary")),
        name="moe_swiglu",
    )(*args)


def _attn_col_perm():
    hq = DA_HEADS * DA_DK
    idx = []
    for base in (0, 2 * hq):
        for h in range(DA_HEADS):
            idx += list(range(base + h * DA_DK, base + (h + 1) * DA_DK))
            idx += list(range(base + hq + h * DA_DK, base + hq + (h + 1) * DA_DK))
    idx += list(range(4 * hq, 4 * hq + DA_HEADS * DA_DV))
    return np.asarray(idx, np.int32)


def _pick(n, pref):
    for c in pref:
        if n % c == 0:
            return c
    return n


def kernel(x, c, w_mod, b_mod, norm1_w, norm2_w, w_in, c_conv_w, c_conv_b, a_qnorm_w, a_knorm_w,
           a_lambda_q1, a_lambda_k1, a_lambda_q2, a_lambda_k2, a_subln_w, b_lb_logits, b_gnorm_w,
           c_igate_b, c_fgate_b, c_norm_w, w_branch, w_out, ffn_w1, ffn_w3, ffn_w2,
           moe_router_w, moe_router_b, moe_w1, moe_w3, moe_w2):
    batch, seq, d = x.shape
    depth = w_in.shape[0]
    t = batch * seq
    n_a = 4 * DA_HEADS * DA_DK + DA_HEADS * DA_DV
    n_b = 4 * HG_HEADS * HG_D
    n_c = 4 * ML_HEADS * ML_D + 2 * ML_HEADS
    n_c_pad = -(-n_c // LANES) * LANES
    tm = _pick(seq, (1024, 512, 256))
    tq = _pick(seq, (ATT_TQ,))
    ts_h = _pick(seq, (1024, 512, 256, 128))

    c8 = jnp.zeros((8, d), F32).at[:batch].set(c)
    mod = _mod_call(c8, w_mod, b_mod.reshape(depth, 1, 6 * d))[:, :batch].reshape(depth, batch, 6, d)

    lb_all = jnp.cumsum(jax.nn.softmax(b_lb_logits.astype(F32), axis=0), axis=0)
    lb_all = lb_all - lb_all[:1]
    slopes = jnp.asarray(LOG2E * 2.0 ** (-8.0 * np.arange(1, DA_HEADS + 1) / DA_HEADS), F32)

    xf = x.reshape(t, d)
    h = _normmod_call(xf, norm1_w[0].reshape(1, d), mod[0], seq, tm)
    perm_a = _attn_col_perm()
    for l in range(depth):
        lam_init = 0.8 - 0.6 * math.exp(-0.3 * l)
        wl = w_in[l]
        w_a = wl[:, :n_a][:, perm_a].astype(BF16)
        w_b = wl[:, n_a:n_a + n_b].astype(BF16)
        w_c = jnp.pad(wl[:, n_a + n_b:n_a + n_b + n_c], ((0, 0), (0, n_c_pad - n_c))).astype(BF16)
        w_g = wl[:, n_a + n_b + n_c:].astype(BF16)
        bias_c = jnp.zeros((1, n_c_pad), F32)
        bias_c = bias_c.at[0, n_c - 2 * ML_HEADS:n_c - ML_HEADS].set(c_igate_b[l])
        bias_c = bias_c.at[0, n_c - ML_HEADS:n_c].set(c_fgate_b[l])
        pa = _matmul(h, w_a, jnp.zeros((1, n_a), F32), BF16, tm, 512)
        pb = _matmul(h, w_b, jnp.zeros((1, n_b), F32), F32, tm, 512)
        pc = _matmul(h, w_c, bias_c, F32, 512, n_c_pad)
        pg = _matmul(h, w_g, jnp.zeros((1, 3 * d), F32), BF16, tm, 1024)

        qw = (jnp.tile(a_qnorm_w[l], 2 * DA_HEADS) * (DA_DK ** -0.5 * LOG2E)).reshape(1, -1)
        kw = jnp.tile(a_knorm_w[l], 2 * DA_HEADS).reshape(1, -1)
        kn, qt, vt = _attn_prep_call(pa, qw, kw, batch, seq, _pick(seq, (512, 256)))
        lam = (jnp.exp(jnp.sum(a_lambda_q1[l] * a_lambda_k1[l]))
               - jnp.exp(jnp.sum(a_lambda_q2[l] * a_lambda_k2[l])) + lam_init)
        scal = jnp.concatenate([slopes, jnp.stack([lam, jnp.asarray(1.0 - lam_init, F32)])]).astype(F32)
        y_a = _attn_call(scal, qt, kn, vt, a_subln_w[l].reshape(1, -1), batch, seq, tq)

        lb = lb_all[l].reshape(HG_HEADS, 1, HG_D)
        lbs = jnp.concatenate([jnp.log(lb), jnp.log1p(-lb)], axis=1)
        y_b = _hgrn_call(pb, lbs, b_gnorm_w[l].reshape(1, -1), batch, seq, ts_h)

        gates = pc[:, n_c - 2 * ML_HEADS:n_c].reshape(batch, seq, 2, ML_HEADS)
        gates = gates.transpose(2, 0, 3, 1).reshape(2, batch, ML_HEADS, seq // ML_CHUNK, ML_CHUNK)
        y_c = _mlstm_call(pc, gates[0], gates[1], c_conv_w[l], c_conv_b[l].reshape(1, -1),
                          c_norm_w[l].reshape(1, -1), batch, seq)

        dense = l % 2 == 0
        router = None
        if not dense:
            rw = jnp.pad(moe_router_w[l // 2], ((0, 0), (0, LANES - N_EXPERTS)))
            rb = jnp.pad(moe_router_b[l // 2], (0, LANES - N_EXPERTS)).reshape(1, LANES)
            router = (rw, rb)
        outs = _merge_call(y_a, y_b, y_c, pg, w_branch[l].astype(BF16), w_out[l].astype(BF16), xf,
                           mod[l], norm2_w[l].reshape(1, d), router, seq, 512)
        xf, h2 = outs[0], outs[1]
        nxt = None if l == depth - 1 else (norm1_w[l + 1].reshape(1, d), mod[l + 1])
        if dense:
            res = _ffn_call(h2, ffn_w1[l // 2].astype(BF16), ffn_w3[l // 2].astype(BF16),
                            ffn_w2[l // 2].astype(BF16), xf, mod[l], nxt, seq, tm, 256)
        else:
            res = _moe_call(h2, outs[2], moe_w1[l // 2].astype(BF16), moe_w3[l // 2].astype(BF16),
                            moe_w2[l // 2].astype(BF16), xf, mod[l], nxt, seq, tm, 256)
        xf = res[0]
        if nxt is not None:
            h = res[1]
    return xf.reshape(batch, seq, d)
```

```python
import functools
import math

import numpy as np
import jax
import jax.numpy as jnp
from jax import lax
from jax.experimental import pallas as pl
from jax.experimental.pallas import tpu as pltpu

F32 = jnp.float32
BF16 = jnp.bfloat16
NORM_EPS = 1e-6

CHUNK = 64
DA_HEADS, DA_DK, DA_DV = 4, 64, 128
HG_HEADS, HG_D = 4, 128
ML_HEADS, ML_D, ML_CONV = 4, 128, 4
N_EXPERTS = 8

LANES = 128
VMEM_LIMIT = 56 * 1024 * 1024

ATT_TQ = 256
ATT_HEADS_PER_STEP = 2
ATT_PAD_ROWS = 16
LOG2E = math.log2(math.e)
ATT_BOUND_MARGIN = 1.01
ATT_BOUND_EPS = 1e-3
ATT_BOUND_MAX = 40.0
HG_CHUNK = 128
HG_SUB = 32
HG_SAFE_DECAY = 80.0
ML_CHUNK = 256


def _cparams(sem, vmem=VMEM_LIMIT):
    return pltpu.CompilerParams(dimension_semantics=sem, vmem_limit_bytes=vmem)


def _dot(a, b):
    return jnp.dot(a, b, preferred_element_type=F32)


def _dot_nt(a, b):
    return lax.dot_general(a, b, (((1,), (1,)), ((), ())), preferred_element_type=F32)


def _split3(x):
    hi = x.astype(BF16)
    r1 = x - hi.astype(F32)
    mid = r1.astype(BF16)
    lo = (r1 - mid.astype(F32)).astype(BF16)
    return hi, mid, lo


def _norm_mod(x, w, scale, shift):
    y = x * lax.rsqrt(jnp.mean(x * x, axis=-1, keepdims=True) + NORM_EPS) * w
    return y * (1.0 + scale) + shift


def _log_sigmoid(x):
    return jnp.minimum(x, 0.0) - jnp.log1p(jnp.exp(-jnp.abs(x)))


def _mod_kernel(c_ref, w_ref, b_ref, o_ref):
    cnd = c_ref[...]
    cnd = cnd * jax.nn.sigmoid(cnd)
    o_ref[0] = _dot(cnd.astype(BF16), w_ref[0].astype(BF16)) + b_ref[0]


def _mod_call(c8, w_mod, b_mod):
    depth, d, n = w_mod.shape
    tn = 1536 if n % 1536 == 0 else n
    return pl.pallas_call(
        _mod_kernel,
        grid=(depth, n // tn),
        in_specs=[pl.BlockSpec((8, d), lambda l, j: (0, 0)),
                  pl.BlockSpec((1, d, tn), lambda l, j: (l, 0, j)),
                  pl.BlockSpec((1, 1, tn), lambda l, j: (l, 0, j))],
        out_specs=pl.BlockSpec((1, 8, tn), lambda l, j: (l, 0, j)),
        out_shape=jax.ShapeDtypeStruct((depth, 8, n), F32),
        compiler_params=_cparams(("arbitrary", "arbitrary")),
        name="adaln_mod",
    )(c8, w_mod, b_mod)


def _normmod_kernel(x_ref, nw_ref, mod_ref, h_ref):
    m = mod_ref[0]
    h_ref[...] = _norm_mod(x_ref[...], nw_ref[...], m[1:2], m[0:1]).astype(BF16)


def _normmod_call(x, nw, mod, seq, tm):
    t, d = x.shape
    per_b = seq // tm
    return pl.pallas_call(
        _normmod_kernel,
        grid=(t // tm,),
        in_specs=[pl.BlockSpec((tm, d), lambda i: (i, 0)),
                  pl.BlockSpec((1, d), lambda i: (0, 0)),
                  pl.BlockSpec((1, 6, d), lambda i: (i // per_b, 0, 0))],
        out_specs=pl.BlockSpec((tm, d), lambda i: (i, 0)),
        out_shape=jax.ShapeDtypeStruct((t, d), BF16),
        compiler_params=_cparams(("arbitrary",)),
        name="prenorm_mod",
    )(x, nw, mod)


def _mm_kernel(x_ref, w_ref, b_ref, o_ref):
    o_ref[...] = (_dot(x_ref[...], w_ref[...]) + b_ref[...]).astype(o_ref.dtype)


def _matmul(x, w, bias, out_dtype, tm, tn):
    t, k = x.shape
    n = w.shape[1]
    return pl.pallas_call(
        _mm_kernel,
        grid=(n // tn, t // tm),
        in_specs=[pl.BlockSpec((tm, k), lambda j, i: (i, 0)),
                  pl.BlockSpec((k, tn), lambda j, i: (0, j)),
                  pl.BlockSpec((1, tn), lambda j, i: (0, j))],
        out_specs=pl.BlockSpec((tm, tn), lambda j, i: (i, j)),
        out_shape=jax.ShapeDtypeStruct((t, n), out_dtype),
        compiler_params=_cparams(("arbitrary", "arbitrary")),
        name="in_proj",
    )(x, w, bias)


def _attn_prep_kernel(a_ref, qw_ref, kw_ref, kn_ref, qt_ref, vt_ref):
    hw = DA_HEADS * 2 * DA_DK
    a = a_ref[...]
    r = lax.broadcasted_iota(jnp.int32, (hw, hw), 0) // DA_DK
    c = lax.broadcasted_iota(jnp.int32, (hw, hw), 1) // DA_DK
    group = jnp.where(r == c, 1.0, 0.0).astype(BF16)

    def qk_norm(z, w):
        ms = _dot((z * z).astype(BF16), group) * (1.0 / DA_DK)
        return z * lax.rsqrt(ms + NORM_EPS) * w

    qn = qk_norm(a[:, :hw].astype(F32), qw_ref[...])
    kn = qk_norm(a[:, hw:2 * hw].astype(F32), kw_ref[...])
    kn_ref[...] = kn.astype(BF16)
    qt_ref[0] = qn.T.astype(BF16)
    vt_ref[0] = a[:, 2 * hw:].astype(F32).T.astype(BF16)


def _attn_prep_call(a, qw, kw, batch, seq, ts):
    t = a.shape[0]
    hw = DA_HEADS * 2 * DA_DK
    hv = DA_HEADS * DA_DV
    per_b = seq // ts
    return pl.pallas_call(
        _attn_prep_kernel,
        grid=(t // ts,),
        in_specs=[pl.BlockSpec((ts, 2 * hw + hv), lambda i: (i, 0)),
                  pl.BlockSpec((1, hw), lambda i: (0, 0)),
                  pl.BlockSpec((1, hw), lambda i: (0, 0))],
        out_specs=[pl.BlockSpec((ts, hw), lambda i: (i, 0)),
                   pl.BlockSpec((1, hw, ts), lambda i: (i // per_b, 0, i % per_b)),
                   pl.BlockSpec((1, hv, ts), lambda i: (i // per_b, 0, i % per_b))],
        out_shape=[jax.ShapeDtypeStruct((t, hw), BF16),
                   jax.ShapeDtypeStruct((batch, hw, seq), BF16),
                   jax.ShapeDtypeStruct((batch, hv, seq), BF16)],
        compiler_params=_cparams(("arbitrary",)),
        name="attn_prep",
    )(a, qw, kw)


def _attn_kernel(sc_ref, qt_ref, k_ref, vt_ref, sw_ref, o_ref,
                 qz_ref, m_ref, acc_ref, vs_ref, kmax_ref, s_ref, s2_ref, *, tq, seq):
    hp = ATT_HEADS_PER_STEP
    dv = DA_DV
    g = pl.program_id(1)
    i = pl.program_id(2)
    lam = sc_ref[DA_HEADS]
    out_scale = sc_ref[DA_HEADS + 1]
    slopes = [sc_ref[hp * g + hh] for hh in range(hp)]
    row16 = lax.broadcasted_iota(jnp.int32, (ATT_PAD_ROWS, tq), 0)
    ones_rows = jnp.where(row16 == 0, 1.0, 0.0).astype(BF16)

    half = lax.broadcasted_iota(jnp.int32, (1, 2 * DA_DK), 1) < DA_DK

    @pl.when(i == 0)
    def _():
        pos = lax.broadcasted_iota(jnp.int32, (1, seq), 1) & (tq - 1)
        rel = (pos - (tq - 1)).astype(F32)
        sub16 = lax.broadcasted_iota(jnp.int32, (ATT_PAD_ROWS, seq), 0)
        gr = lax.broadcasted_iota(jnp.int32, (2 * DA_DK, 2 * DA_DK), 0) // DA_DK
        gc = lax.broadcasted_iota(jnp.int32, (2 * DA_DK, 2 * DA_DK), 1) // DA_DK
        group = jnp.where(gr == gc, 1.0, 0.0).astype(BF16)
        for hh in range(hp):
            w = jnp.exp2(slopes[hh] * rel)
            vs_ref[hh, 0:dv, :] = (vt_ref[0, hh * dv:(hh + 1) * dv, :].astype(F32) * w).astype(BF16)
            vs_ref[hh, dv:dv + ATT_PAD_ROWS, :] = jnp.where(sub16 == 0, w, 0.0).astype(BF16)

            def knorm(n, best):
                kc = k_ref[pl.ds(pl.multiple_of(n * tq, tq), tq),
                           hh * 2 * DA_DK:(hh + 1) * 2 * DA_DK].astype(F32)
                return jnp.maximum(best, jnp.max(_dot((kc * kc).astype(BF16), group), axis=0, keepdims=True))

            k2 = lax.fori_loop(0, seq // tq, knorm, jnp.zeros((1, 2 * DA_DK), F32))
            kmax_ref[2 * hh] = jnp.max(jnp.where(half, k2, 0.0), axis=1, keepdims=True)
            kmax_ref[2 * hh + 1] = jnp.max(jnp.where(half, 0.0, k2), axis=1, keepdims=True)

    row = lax.broadcasted_iota(jnp.int32, (2 * DA_DK, tq), 0)
    qq_row = lax.broadcasted_iota(jnp.int32, (1, tq), 1).astype(F32)
    bound_max = jnp.zeros((), F32)
    for hh in range(hp):
        qt = qt_ref[0, hh * 2 * DA_DK:(hh + 1) * 2 * DA_DK, :]
        zero = jnp.zeros_like(qt)
        q1 = jnp.where(row < DA_DK, qt, zero)
        q2 = jnp.where(row >= DA_DK, qt, zero)
        qz_ref[hh] = jnp.concatenate([q1, q2], axis=1)
        qn = jnp.concatenate(
            [jnp.sum(jnp.square(q1.astype(F32)), axis=0, keepdims=True) * kmax_ref[2 * hh],
             jnp.sum(jnp.square(q2.astype(F32)), axis=0, keepdims=True) * kmax_ref[2 * hh + 1]], axis=1)
        bound = jnp.sqrt(qn) * ATT_BOUND_MARGIN + ATT_BOUND_EPS
        bound_max = jnp.maximum(bound_max, jnp.max(bound))
        m_ref[hh] = bound + slopes[hh] * jnp.concatenate([qq_row, qq_row], axis=1)
    acc_ref[...] = jnp.zeros(acc_ref.shape, F32)

    kk = lax.broadcasted_iota(jnp.int32, (tq, 2 * tq), 0)
    cc = lax.broadcasted_iota(jnp.int32, (tq, 2 * tq), 1)
    qq = jnp.where(cc >= tq, cc - tq, cc)
    dist = (qq - jnp.abs(qq - kk)).astype(F32)
    visible = (kk // CHUNK) <= (qq // CHUNK)

    def scores(hh, blk):
        start = pl.multiple_of(blk * tq, tq)
        return _dot(k_ref[pl.ds(start, tq), hh * 2 * DA_DK:(hh + 1) * 2 * DA_DK], qz_ref[hh])

    def diag_values(hh):
        start = pl.multiple_of(i * tq, tq)
        return jnp.concatenate([vt_ref[0, hh * dv:(hh + 1) * dv, pl.ds(start, tq)], ones_rows], axis=0)

    @pl.when(bound_max <= ATT_BOUND_MAX)
    def _():
        for hh in range(hp):
            s_ref[hh] = scores(hh, 0)
            s2_ref[hh] = scores(hh, jnp.minimum(1, i))

        def probs(hh, blk, s):
            last_key = ((blk + 1 - i) * tq - 1).astype(F32)
            return jnp.exp2(s - (m_ref[hh] - slopes[hh] * last_key)).astype(BF16)

        def body(jj, carry):
            blk = 2 * jj
            start = pl.multiple_of(blk * tq, 2 * tq)
            for hh in range(hp):
                p = jnp.concatenate([probs(hh, blk, s_ref[hh]), probs(hh, blk + 1, s2_ref[hh])], axis=0)
                s_ref[hh] = scores(hh, jnp.minimum(blk + 2, i))
                s2_ref[hh] = scores(hh, jnp.minimum(blk + 3, i))
                acc_ref[hh] += _dot(vs_ref[hh, :, pl.ds(start, 2 * tq)], p)
            return carry

        lax.fori_loop(0, i // 2, body, 0)

        @pl.when(i % 2 == 1)
        def _():
            start = pl.multiple_of((i - 1) * tq, tq)
            for hh in range(hp):
                acc_ref[hh] += _dot(vs_ref[hh, :, pl.ds(start, tq)], probs(hh, i - 1, s_ref[hh]))
                s_ref[hh] = s2_ref[hh]

        for hh in range(hp):
            s = jnp.where(visible, s_ref[hh] + slopes[hh] * dist, -jnp.inf)
            acc_ref[hh] += _dot(diag_values(hh), jnp.exp2(s - m_ref[hh]).astype(BF16))

    @pl.when(bound_max > ATT_BOUND_MAX)
    def _():
        m_ref[...] = jnp.full(m_ref.shape, -jnp.inf, F32)

        def update(hh, s, bound_shift, v_aug):
            m_old = m_ref[hh]
            m_new = jnp.maximum(m_old, jnp.max(s, axis=0, keepdims=True) + bound_shift)
            p = jnp.exp2(s - (m_new - bound_shift))
            acc_ref[hh] = jnp.exp2(m_old - m_new) * acc_ref[hh] + _dot(v_aug, p.astype(BF16))
            m_ref[hh] = m_new

        def body(j, carry):
            start = pl.multiple_of(j * tq, tq)
            last_key = ((j + 1 - i) * tq - 1).astype(F32)
            for hh in range(hp):
                update(hh, scores(hh, j), slopes[hh] * last_key, vs_ref[hh, :, pl.ds(start, tq)])
            return carry

        lax.fori_loop(0, i, body, 0)
        for hh in range(hp):
            s = jnp.where(visible, scores(hh, i) + slopes[hh] * dist, -jnp.inf)
            update(hh, s, jnp.zeros((), F32), diag_values(hh))

    outs = []
    for hh in range(hp):
        acc = acc_ref[hh]
        o2 = acc[:dv, :] * (1.0 / acc[dv:dv + 1, :])
        o = (o2[:, :tq] - lam * o2[:, tq:]).T
        o = o * lax.rsqrt(jnp.mean(o * o, axis=-1, keepdims=True) + NORM_EPS) * sw_ref[...]
        outs.append(o * out_scale)
    o_ref[...] = jnp.concatenate(outs, axis=1).astype(o_ref.dtype)


def _attn_call(scal, qt, kn, vt, sw, batch, seq, tq):
    t = kn.shape[0]
    nq = seq // tq
    hp = ATT_HEADS_PER_STEP
    return pl.pallas_call(
        functools.partial(_attn_kernel, tq=tq, seq=seq),
        grid=(batch, DA_HEADS // hp, nq),
        in_specs=[pl.BlockSpec(memory_space=pltpu.SMEM),
                  pl.BlockSpec((1, hp * 2 * DA_DK, tq), lambda b, g, i: (b, g, i)),
                  pl.BlockSpec((seq, hp * 2 * DA_DK), lambda b, g, i: (b, g)),
                  pl.BlockSpec((1, hp * DA_DV, seq), lambda b, g, i: (b, g, 0)),
                  pl.BlockSpec((1, DA_DV), lambda b, g, i: (0, 0))],
        out_specs=pl.BlockSpec((tq, hp * DA_DV), lambda b, g, i: (b * nq + i, g)),
        out_shape=jax.ShapeDtypeStruct((t, DA_HEADS * DA_DV), BF16),
        scratch_shapes=[pltpu.VMEM((hp, 2 * DA_DK, 2 * tq), BF16),
                        pltpu.VMEM((hp, 1, 2 * tq), F32),
                        pltpu.VMEM((hp, DA_DV + ATT_PAD_ROWS, 2 * tq), F32),
                        pltpu.VMEM((hp, DA_DV + ATT_PAD_ROWS, seq), BF16),
                        pltpu.VMEM((2 * hp, 1, 1), F32),
                        pltpu.VMEM((hp, tq, 2 * tq), F32),
                        pltpu.VMEM((hp, tq, 2 * tq), F32)],
        compiler_params=_cparams(("arbitrary", "arbitrary", "arbitrary")),
        name="diff_attn",
    )(scal, qt, kn, vt, sw)


def _hgrn_kernel(q_ref, f_ref, i_ref, g_ref, lb_ref, gw_ref, o_ref,
                 st_ref, b_ref, qs_ref, ks_ref, oi_ref, *, n_chunks):
    c = HG_CHUNK
    sub = HG_SUB
    d = HG_D
    nh = HG_HEADS

    @pl.when(pl.program_id(1) == 0)
    def _():
        st_ref[...] = jnp.zeros(st_ref.shape, F32)

    gw = gw_ref[...]
    rr = lax.broadcasted_iota(jnp.int32, (c, c), 0)
    cc = lax.broadcasted_iota(jnp.int32, (c, c), 1)
    causal = cc <= rr
    tril = jnp.where(causal, 1.0, 0.0).astype(BF16)

    def chunk(n, carry):
        r0 = pl.multiple_of(n * c, c)
        heads = []
        decay = jnp.zeros((), F32)
        for hh in range(nh):
            cols = slice(hh * d, (hh + 1) * d)
            log_lb = lb_ref[hh, 0:1, :]
            q = q_ref[pl.ds(r0, c), cols]
            q = q * jax.nn.sigmoid(q)
            a = lb_ref[hh, 1:2, :] + _log_sigmoid(f_ref[pl.ds(r0, c), cols])
            logf = jnp.maximum(log_lb, a) + jnp.log1p(jnp.exp(-jnp.abs(log_lb - a)))
            k = 1.0 - jnp.exp(logf)
            v = i_ref[pl.ds(r0, c), cols]
            hi, mid, lo = _split3(logf)
            b = _dot(tril, hi) + _dot(tril, mid) + _dot(tril, lo)
            b_last = b[c - 1:c, :]
            st = st_ref[hh]
            o_inter = _dot_nt((q * jnp.exp(b)).astype(BF16), st.astype(BF16))
            k_hat = k * jnp.exp(b_last - b)
            st_ref[hh] = st * jnp.exp(b_last) + _dot(v.T.astype(BF16), k_hat.astype(BF16))
            betas = []
            for blk in range(c // sub):
                beta = jnp.zeros((1, d), F32) if blk == 0 else b[blk * sub - 1:blk * sub, :]
                betas.append(beta)
                b_end = b[(blk + 1) * sub - 1:(blk + 1) * sub, :]
                decay = jnp.maximum(decay, jnp.max(beta - b_end))
            heads.append((q, k, v, b, betas, o_inter))

        @pl.when(decay < HG_SAFE_DECAY)
        def _():
            for hh, (q, k, v, b, betas, _) in enumerate(heads):
                rows = []
                for blk in range(c // sub):
                    beta = betas[blk]
                    q_t = q[blk * sub:(blk + 1) * sub, :] * jnp.exp(b[blk * sub:(blk + 1) * sub, :] - beta)
                    k_t = k * jnp.exp(jnp.minimum(beta - b, HG_SAFE_DECAY))
                    rows.append(_dot_nt(q_t.astype(BF16), k_t.astype(BF16)))
                att = jnp.where(causal, jnp.concatenate(rows, axis=0), 0.0)
                oi_ref[hh] = _dot(att.astype(BF16), v.astype(BF16))

        @pl.when(decay >= HG_SAFE_DECAY)
        def _():
            ridx = lax.broadcasted_iota(jnp.int32, (c, d), 0)
            for hh, (q, k, v, b, _, _) in enumerate(heads):
                b_ref[...] = b
                qs_ref[...] = q
                ks_ref[...] = k

                def row(t, carry2):
                    bt = b_ref[pl.ds(t, 1), :]
                    e = jnp.exp(jnp.where(ridx <= t, bt - b_ref[...], -jnp.inf))
                    w = jnp.sum(qs_ref[pl.ds(t, 1), :] * ks_ref[...] * e, axis=1, keepdims=True)
                    oi_ref[hh, pl.ds(t, 1), :] = jnp.sum(w * v, axis=0, keepdims=True)
                    return carry2

                lax.fori_loop(0, c, row, 0)

        for hh in range(nh):
            cols = slice(hh * d, (hh + 1) * d)
            o = heads[hh][5] + oi_ref[hh]
            o = o * lax.rsqrt(jnp.mean(o * o, axis=-1, keepdims=True) + NORM_EPS) * gw
            g = g_ref[pl.ds(r0, c), cols]
            o_ref[pl.ds(r0, c), cols] = (o * (g * jax.nn.sigmoid(g))).astype(o_ref.dtype)
        return carry

    lax.fori_loop(0, n_chunks, chunk, 0)


def _hgrn_call(bproj, lbs, gw, batch, seq, ts):
    t = bproj.shape[0]
    d = HG_D
    nh = HG_HEADS
    ns = seq // ts
    spec = lambda off: pl.BlockSpec((ts, nh * d), lambda b, i: (b * ns + i, off))
    return pl.pallas_call(
        functools.partial(_hgrn_kernel, n_chunks=ts // HG_CHUNK),
        grid=(batch, ns),
        in_specs=[spec(0), spec(1), spec(2), spec(3),
                  pl.BlockSpec((nh, 2, d), lambda b, i: (0, 0, 0)),
                  pl.BlockSpec((1, d), lambda b, i: (0, 0))],
        out_specs=pl.BlockSpec((ts, nh * d), lambda b, i: (b * ns + i, 0)),
        out_shape=jax.ShapeDtypeStruct((t, nh * d), BF16),
        scratch_shapes=[pltpu.VMEM((nh, d, d), F32),
                        pltpu.VMEM((HG_CHUNK, d), F32),
                        pltpu.VMEM((HG_CHUNK, d), F32),
                        pltpu.VMEM((HG_CHUNK, d), F32),
                        pltpu.VMEM((nh, HG_CHUNK, d), F32)],
        compiler_params=_cparams(("arbitrary", "arbitrary")),
        name="hgrn2",
    )(bproj, bproj, bproj, bproj, lbs, gw)


def _mlstm_kernel(uq_ref, uk_ref, v_ref, op_ref, gi_ref, gf_ref, cwq_ref, cwk_ref,
                  cbq_ref, cbk_ref, nw_ref, o_ref,
                  xq_ref, xk_ref, c_ref, m_ref, bs_ref):
    L = ML_CHUNK
    d = ML_D
    nh = ML_HEADS
    i = pl.program_id(1)
    rr = lax.broadcasted_iota(jnp.int32, (L, L), 0)
    cc = lax.broadcasted_iota(jnp.int32, (L, L), 1)

    @pl.when(i == 0)
    def _():
        c_ref[...] = jnp.zeros(c_ref.shape, F32)
        m_ref[...] = jnp.zeros(m_ref.shape, F32)
        xq_ref[0:8, :] = jnp.zeros((8, nh * d), F32)
        xk_ref[0:8, :] = jnp.zeros((8, nh * d), F32)
        upper = jnp.where(rr <= cc, 1.0, 0.0).astype(BF16)
        for hh in range(nh):
            hi, mid, lo = _split3(_log_sigmoid(gf_ref[0, hh]))
            bs_ref[hh] = _dot(hi, upper) + _dot(mid, upper) + _dot(lo, upper)

    def conv_silu(u_ref, x_ref, w_ref, b_ref):
        x_ref[8:8 + L, :] = u_ref[...]
        y = b_ref[...] + w_ref[ML_CONV - 1:ML_CONV, :] * x_ref[8:8 + L, :]
        for j in range(ML_CONV - 1):
            y = y + w_ref[j:j + 1, :] * x_ref[5 + j:5 + j + L, :]
        x_ref[0:8, :] = x_ref[L:L + 8, :]
        return y * jax.nn.sigmoid(y)

    q_all = conv_silu(uq_ref, xq_ref, cwq_ref, cbq_ref)
    k_all = conv_silu(uk_ref, xk_ref, cwk_ref, cbk_ref) * (d ** -0.5)
    lane = lax.broadcasted_iota(jnp.int32, (L, d), 1)
    ones_col = jnp.where(lane == 0, 1.0, 0.0)
    for hh in range(nh):
        cols = slice(hh * d, (hh + 1) * d)
        q = q_all[:, cols].astype(BF16)
        kt = k_all[:, cols].T
        v_aug = jnp.concatenate([v_ref[:, cols], ones_col], axis=1).astype(BF16)

        b_row = bs_ref[hh, pl.ds(i, 1), :]
        ig_row = gi_ref[0, hh, pl.ds(i, 1), :]
        m_prev = m_ref[hh]
        g = b_row[:, L - 1:L]
        b_col = jnp.sum(jnp.where(rr == cc, b_row, 0.0), axis=1, keepdims=True)
        log_d = jnp.where(cc <= rr, b_col + (ig_row - b_row), -jnp.inf)
        log_inter = b_col + m_prev
        m_t = jnp.maximum(log_inter, jnp.max(log_d, axis=1, keepdims=True))
        w_intra = jnp.exp(log_d - m_t) * _dot(q, kt.astype(BF16))
        w_inter = jnp.exp(log_inter - m_t)
        c_aug = c_ref[hh]
        tot = w_inter * _dot(q, c_aug.astype(BF16)) + _dot(w_intra.astype(BF16), v_aug)
        denom = jnp.maximum(jnp.abs(tot[:, d:d + 1]), jnp.exp(-m_t))
        hout = tot[:, :d] / denom
        hout = hout * lax.rsqrt(jnp.mean(hout * hout, axis=-1, keepdims=True) + NORM_EPS) * nw_ref[...]
        o_ref[:, cols] = (hout * jax.nn.sigmoid(op_ref[:, cols])).astype(o_ref.dtype)

        log_w = g - b_row + ig_row
        m_new = jnp.maximum(g + m_prev, jnp.max(log_w, axis=1, keepdims=True))
        w_s = jnp.exp(log_w - m_new)
        c_ref[hh] = jnp.exp(g + m_prev - m_new) * c_aug + _dot((kt * w_s).astype(BF16), v_aug)
        m_ref[hh] = m_new


def _mlstm_call(cproj, gi, gf, conv_w, conv_b, nw, batch, seq):
    t = cproj.shape[0]
    d = ML_D
    nh = ML_HEADS
    L = ML_CHUNK
    nc = seq // L
    spec = lambda off: pl.BlockSpec((L, nh * d), lambda b, i: (b * nc + i, off))
    gspec = pl.BlockSpec((1, nh, nc, L), lambda b, i: (b, 0, 0, 0))
    return pl.pallas_call(
        _mlstm_kernel,
        grid=(batch, nc),
        in_specs=[spec(0), spec(1), spec(2), spec(3), gspec, gspec,
                  pl.BlockSpec((ML_CONV, nh * d), lambda b, i: (0, 0)),
                  pl.BlockSpec((ML_CONV, nh * d), lambda b, i: (0, 1)),
                  pl.BlockSpec((1, nh * d), lambda b, i: (0, 0)),
                  pl.BlockSpec((1, nh * d), lambda b, i: (0, 1)),
                  pl.BlockSpec((1, d), lambda b, i: (0, 0))],
        out_specs=pl.BlockSpec((L, nh * d), lambda b, i: (b * nc + i, 0)),
        out_shape=jax.ShapeDtypeStruct((t, nh * d), BF16),
        scratch_shapes=[pltpu.VMEM((L + 8, nh * d), F32),
                        pltpu.VMEM((L + 8, nh * d), F32),
                        pltpu.VMEM((nh, d, 2 * d), F32),
                        pltpu.VMEM((nh, 1, 1), F32),
                        pltpu.VMEM((nh, nc, L), F32)],
        compiler_params=_cparams(("arbitrary", "arbitrary")),
        name="mlstm",
    )(cproj, cproj, cproj, cproj, gi, gf, conv_w, conv_w, conv_b, conv_b, nw)


def _top2_combine(logits):
    lane = lax.broadcasted_iota(jnp.int32, logits.shape, 1)
    lg = jnp.where(lane < N_EXPERTS, logits, -jnp.inf)
    ex = jnp.exp(lg - jnp.max(lg, axis=1, keepdims=True))
    probs = ex / jnp.sum(ex, axis=1, keepdims=True)
    p1 = jnp.max(probs, axis=1, keepdims=True)
    i1 = jnp.min(jnp.where(probs == p1, lane, LANES), axis=1, keepdims=True)
    rest = jnp.where(lane == i1, -1.0, probs)
    p2 = jnp.max(rest, axis=1, keepdims=True)
    i2 = jnp.min(jnp.where(rest == p2, lane, LANES), axis=1, keepdims=True)
    comb = jnp.where(lane == i1, p1, 0.0) + jnp.where(lane == i2, p2, 0.0)
    return comb / (p1 + p2)


def _merge_kernel(*refs, route):
    if route:
        (ya_ref, yb_ref, yc_ref, gp_ref, wb_ref, wo_ref, x_ref, mod_ref, nw_ref,
         rw_ref, rb_ref, xo_ref, h_ref, cmb_ref) = refs
    else:
        (ya_ref, yb_ref, yc_ref, gp_ref, wb_ref, wo_ref, x_ref, mod_ref, nw_ref,
         xo_ref, h_ref) = refs
    d = x_ref.shape[1]
    merged = None
    for n, y_ref in enumerate((ya_ref, yb_ref, yc_ref)):
        gate = jax.nn.sigmoid(gp_ref[:, n * d:(n + 1) * d].astype(F32))
        term = gate * _dot(y_ref[...], wb_ref[n])
        merged = term if merged is None else merged + term
    m = mod_ref[0]
    xn = x_ref[...] + m[2:3] * _dot(merged.astype(BF16), wo_ref[...])
    xo_ref[...] = xn
    h2 = _norm_mod(xn, nw_ref[...], m[4:5], m[3:4])
    h_ref[...] = h2.astype(BF16)
    if route:
        h_hi, h_mid, _ = _split3(h2)
        r_hi, r_mid, _ = _split3(rw_ref[...])
        logits = _dot(h_hi, r_hi) + _dot(h_mid, r_hi) + _dot(h_hi, r_mid) + rb_ref[...]
        cmb_ref[...] = _top2_combine(logits)


def _merge_call(ya, yb, yc, gp, wb, wo, x, mod, nw, router, seq, tm):
    t, d = x.shape
    bw = ya.shape[1]
    per_b = seq // tm
    route = router is not None
    tok = lambda w: pl.BlockSpec((tm, w), lambda i: (i, 0))
    const2 = lambda s: pl.BlockSpec(s, lambda i: (0, 0))
    in_specs = [tok(bw), tok(bw), tok(bw), tok(3 * d),
                pl.BlockSpec((3, bw, d), lambda i: (0, 0, 0)), const2((d, d)), tok(d),
                pl.BlockSpec((1, 6, d), lambda i: (i // per_b, 0, 0)), const2((1, d))]
    out_specs = [tok(d), tok(d)]
    out_shape = [jax.ShapeDtypeStruct((t, d), F32), jax.ShapeDtypeStruct((t, d), BF16)]
    args = [ya, yb, yc, gp, wb, wo, x, mod, nw]
    if route:
        in_specs += [const2((d, LANES)), const2((1, LANES))]
        out_specs.append(tok(LANES))
        out_shape.append(jax.ShapeDtypeStruct((t, LANES), F32))
        args += list(router)
    return pl.pallas_call(
        functools.partial(_merge_kernel, route=route),
        grid=(t // tm,),
        in_specs=in_specs, out_specs=out_specs, out_shape=out_shape,
        compiler_params=_cparams(("arbitrary",)),
        name="merge_out",
    )(*args)


def _finish(acc, x_ref, mod_ref, xo_ref, nxt):
    xn = x_ref[...] + mod_ref[0][5:6] * acc
    xo_ref[...] = xn
    if nxt is not None:
        nw_ref, modn_ref, hn_ref = nxt
        mn = modn_ref[0]
        hn_ref[...] = _norm_mod(xn, nw_ref[...], mn[1:2], mn[0:1]).astype(BF16)


def _swiglu_partial(h, w1, w3, w2):
    a = _dot(h, w1)
    act = a * jax.nn.sigmoid(a) * _dot(h, w3)
    return act, w2


def _ffn_kernel(*refs, has_next):
    if has_next:
        h_ref, w1_ref, w3_ref, w2_ref, x_ref, mod_ref, nw_ref, modn_ref, xo_ref, hn_ref, acc_ref = refs
        nxt = (nw_ref, modn_ref, hn_ref)
    else:
        h_ref, w1_ref, w3_ref, w2_ref, x_ref, mod_ref, xo_ref, acc_ref = refs
        nxt = None
    f = pl.program_id(1)

    @pl.when(f == 0)
    def _():
        acc_ref[...] = jnp.zeros(acc_ref.shape, F32)

    h = h_ref[...]
    a = _dot(h, w1_ref[...])
    act = a * jax.nn.sigmoid(a) * _dot(h, w3_ref[...])
    acc_ref[...] += _dot(act.astype(BF16), w2_ref[...])

    @pl.when(f == pl.num_programs(1) - 1)
    def _():
        _finish(acc_ref[...], x_ref, mod_ref, xo_ref, nxt)


def _ffn_call(h, w1, w3, w2, x, mod, nxt, seq, tm, tf):
    t, d = x.shape
    ff = w1.shape[1]
    per_b = seq // tm
    tok = lambda: pl.BlockSpec((tm, d), lambda i, f: (i, 0))
    modspec = lambda: pl.BlockSpec((1, 6, d), lambda i, f: (i // per_b, 0, 0))
    in_specs = [tok(), pl.BlockSpec((d, tf), lambda i, f: (0, f)),
                pl.BlockSpec((d, tf), lambda i, f: (0, f)),
                pl.BlockSpec((tf, d), lambda i, f: (f, 0)), tok(), modspec()]
    out_specs = [tok()]
    out_shape = [jax.ShapeDtypeStruct((t, d), F32)]
    args = [h, w1, w3, w2, x, mod]
    if nxt is not None:
        in_specs += [pl.BlockSpec((1, d), lambda i, f: (0, 0)), modspec()]
        out_specs.append(tok())
        out_shape.append(jax.ShapeDtypeStruct((t, d), BF16))
        args += list(nxt)
    return pl.pallas_call(
        functools.partial(_ffn_kernel, has_next=nxt is not None),
        grid=(t // tm, ff // tf),
        in_specs=in_specs, out_specs=out_specs, out_shape=out_shape,
        scratch_shapes=[pltpu.VMEM((tm, d), F32)],
        compiler_params=_cparams(("arbitrary", "arbitrary")),
        name="ffn_swiglu",
    )(*args)


def _moe_kernel(*refs, has_next):
    if has_next:
        (h_ref, cmb_ref, w1_ref, w3_ref, w2_ref, x_ref, mod_ref, nw_ref, modn_ref,
         xo_ref, hn_ref, acc_ref) = refs
        nxt = (nw_ref, modn_ref, hn_ref)
    else:
        h_ref, cmb_ref, w1_ref, w3_ref, w2_ref, x_ref, mod_ref, xo_ref, acc_ref = refs
        nxt = None
    e = pl.program_id(1)
    f = pl.program_id(2)

    @pl.when((e == 0) & (f == 0))
    def _():
        acc_ref[...] = jnp.zeros(acc_ref.shape, F32)

    h = h_ref[...]
    cmb = cmb_ref[...]
    lane = lax.broadcasted_iota(jnp.int32, cmb.shape, 1)
    w_e = jnp.sum(jnp.where(lane == e, cmb, 0.0), axis=1, keepdims=True)
    a = _dot(h, w1_ref[0])
    act = a * jax.nn.sigmoid(a) * _dot(h, w3_ref[0]) * w_e
    acc_ref[...] += _dot(act.astype(BF16), w2_ref[0])

    @pl.when((e == pl.num_programs(1) - 1) & (f == pl.num_programs(2) - 1))
    def _():
        _finish(acc_ref[...], x_ref, mod_ref, xo_ref, nxt)


def _moe_call(h, cmb, w1, w3, w2, x, mod, nxt, seq, tm, tf):
    t, d = x.shape
    ne, _, ff = w1.shape
    per_b = seq // tm
    tok = lambda w=None: pl.BlockSpec((tm, d if w is None else w), lambda i, e, f: (i, 0))
    modspec = lambda: pl.BlockSpec((1, 6, d), lambda i, e, f: (i // per_b, 0, 0))
    in_specs = [tok(), tok(LANES),
                pl.BlockSpec((1, d, tf), lambda i, e, f: (e, 0, f)),
                pl.BlockSpec((1, d, tf), lambda i, e, f: (e, 0, f)),
                pl.BlockSpec((1, tf, d), lambda i, e, f: (e, f, 0)), tok(), modspec()]
    out_specs = [tok()]
    out_shape = [jax.ShapeDtypeStruct((t, d), F32)]
    args = [h, cmb, w1, w3, w2, x, mod]
    if nxt is not None:
        in_specs += [pl.BlockSpec((1, d), lambda i, e, f: (0, 0)), modspec()]
        out_specs.append(tok())
        out_shape.append(jax.ShapeDtypeStruct((t, d), BF16))
        args += list(nxt)
    return pl.pallas_call(
        functools.partial(_moe_kernel, has_next=nxt is not None),
        grid=(t // tm, ne, ff // tf),
        in_specs=in_specs, out_specs=out_specs, out_shape=out_shape,
        scratch_shapes=[pltpu.VMEM((tm, d), F32)],
        compiler_params=_cparams(("arbitrary", "arbitrary", "arbitrary")),
        name="moe_swiglu",
    )(*args)


def _attn_col_perm():
    hq = DA_HEADS * DA_DK
    idx = []
    for base in (0, 2 * hq):
        for h in range(DA_HEADS):
            idx += list(range(base + h * DA_DK, base + (h + 1) * DA_DK))
            idx += list(range(base + hq + h * DA_DK, base + hq + (h + 1) * DA_DK))
    idx += list(range(4 * hq, 4 * hq + DA_HEADS * DA_DV))
    return np.asarray(idx, np.int32)


def _pick(n, pref):
    for c in pref:
        if n % c == 0:
            return c
    return n


def kernel(x, c, w_mod, b_mod, norm1_w, norm2_w, w_in, c_conv_w, c_conv_b, a_qnorm_w, a_knorm_w,
           a_lambda_q1, a_lambda_k1, a_lambda_q2, a_lambda_k2, a_subln_w, b_lb_logits, b_gnorm_w,
           c_igate_b, c_fgate_b, c_norm_w, w_branch, w_out, ffn_w1, ffn_w3, ffn_w2,
           moe_router_w, moe_router_b, moe_w1, moe_w3, moe_w2):
    batch, seq, d = x.shape
    depth = w_in.shape[0]
    t = batch * seq
    n_a = 4 * DA_HEADS * DA_DK + DA_HEADS * DA_DV
    n_b = 4 * HG_HEADS * HG_D
    n_c = 4 * ML_HEADS * ML_D + 2 * ML_HEADS
    n_c_pad = -(-n_c // LANES) * LANES
    tm = _pick(seq, (1024, 512, 256))
    tq = _pick(seq, (ATT_TQ,))
    ts_h = _pick(seq, (1024, 512, 256, 128))

    c8 = jnp.zeros((8, d), F32).at[:batch].set(c)
    mod = _mod_call(c8, w_mod, b_mod.reshape(depth, 1, 6 * d))[:, :batch].reshape(depth, batch, 6, d)

    lb_all = jnp.cumsum(jax.nn.softmax(b_lb_logits.astype(F32), axis=0), axis=0)
    lb_all = lb_all - lb_all[:1]
    slopes = jnp.asarray(LOG2E * 2.0 ** (-8.0 * np.arange(1, DA_HEADS + 1) / DA_HEADS), F32)

    xf = x.reshape(t, d)
    h = _normmod_call(xf, norm1_w[0].reshape(1, d), mod[0], seq, tm)
    perm_a = _attn_col_perm()
    for l in range(depth):
        lam_init = 0.8 - 0.6 * math.exp(-0.3 * l)
        wl = w_in[l]
        w_a = wl[:, :n_a][:, perm_a].astype(BF16)
        w_b = wl[:, n_a:n_a + n_b].astype(BF16)
        w_c = jnp.pad(wl[:, n_a + n_b:n_a + n_b + n_c], ((0, 0), (0, n_c_pad - n_c))).astype(BF16)
        w_g = wl[:, n_a + n_b + n_c:].astype(BF16)
        bias_c = jnp.zeros((1, n_c_pad), F32)
        bias_c = bias_c.at[0, n_c - 2 * ML_HEADS:n_c - ML_HEADS].set(c_igate_b[l])
        bias_c = bias_c.at[0, n_c - ML_HEADS:n_c].set(c_fgate_b[l])
        pa = _matmul(h, w_a, jnp.zeros((1, n_a), F32), BF16, tm, 512)
        pb = _matmul(h, w_b, jnp.zeros((1, n_b), F32), F32, tm, 512)
        pc = _matmul(h, w_c, bias_c, F32, 512, n_c_pad)
        pg = _matmul(h, w_g, jnp.zeros((1, 3 * d), F32), BF16, tm, 1024)

        qw = (jnp.tile(a_qnorm_w[l], 2 * DA_HEADS) * (DA_DK ** -0.5 * LOG2E)).reshape(1, -1)
        kw = jnp.tile(a_knorm_w[l], 2 * DA_HEADS).reshape(1, -1)
        kn, qt, vt = _attn_prep_call(pa, qw, kw, batch, seq, _pick(seq, (512, 256)))
        lam = (jnp.exp(jnp.sum(a_lambda_q1[l] * a_lambda_k1[l]))
               - jnp.exp(jnp.sum(a_lambda_q2[l] * a_lambda_k2[l])) + lam_init)
        scal = jnp.concatenate([slopes, jnp.stack([lam, jnp.asarray(1.0 - lam_init, F32)])]).astype(F32)
        y_a = _attn_call(scal, qt, kn, vt, a_subln_w[l].reshape(1, -1), batch, seq, tq)

        lb = lb_all[l].reshape(HG_HEADS, 1, HG_D)
        lbs = jnp.concatenate([jnp.log(lb), jnp.log1p(-lb)], axis=1)
        y_b = _hgrn_call(pb, lbs, b_gnorm_w[l].reshape(1, -1), batch, seq, ts_h)

        gates = pc[:, n_c - 2 * ML_HEADS:n_c].reshape(batch, seq, 2, ML_HEADS)
        gates = gates.transpose(2, 0, 3, 1).reshape(2, batch, ML_HEADS, seq // ML_CHUNK, ML_CHUNK)
        y_c = _mlstm_call(pc, gates[0], gates[1], c_conv_w[l], c_conv_b[l].reshape(1, -1),
                          c_norm_w[l].reshape(1, -1), batch, seq)

        dense = l % 2 == 0
        router = None
        if not dense:
            rw = jnp.pad(moe_router_w[l // 2], ((0, 0), (0, LANES - N_EXPERTS)))
            rb = jnp.pad(moe_router_b[l // 2], (0, LANES - N_EXPERTS)).reshape(1, LANES)
            router = (rw, rb)
        outs = _merge_call(y_a, y_b, y_c, pg, w_branch[l].astype(BF16), w_out[l].astype(BF16), xf,
                           mod[l], norm2_w[l].reshape(1, d), router, seq, 512)
        xf, h2 = outs[0], outs[1]
        nxt = None if l == depth - 1 else (norm1_w[l + 1].reshape(1, d), mod[l + 1])
        if dense:
            res = _ffn_call(h2, ffn_w1[l // 2].astype(BF16), ffn_w3[l // 2].astype(BF16),
                            ffn_w2[l // 2].astype(BF16), xf, mod[l], nxt, seq, tm, 256)
        else:
            res = _moe_call(h2, outs[2], moe_w1[l // 2].astype(BF16), moe_w3[l // 2].astype(BF16),
                            moe_w2[l // 2].astype(BF16), xf, mod[l], nxt, seq, tm, 256)
        xf = res[0]
        if nxt is not None:
            h = res[1]
    return xf.reshape(batch, seq, d)
```

```python
import functools
import math

import numpy as np
import jax
import jax.numpy as jnp
from jax import lax
from jax.experimental import pallas as pl
from jax.experimental.pallas import tpu as pltpu

F32 = jnp.float32
BF16 = jnp.bfloat16
NORM_EPS = 1e-6

CHUNK = 64
DA_HEADS, DA_DK, DA_DV = 4, 64, 128
HG_HEADS, HG_D = 4, 128
ML_HEADS, ML_D, ML_CONV = 4, 128, 4
N_EXPERTS = 8

LANES = 128
VMEM_LIMIT = 56 * 1024 * 1024

ATT_TQ = 256
ATT_HEADS_PER_STEP = 2
ATT_PAD_ROWS = 16
LOG2E = math.log2(math.e)
ATT_BOUND_MARGIN = 1.01
ATT_BOUND_EPS = 1e-3
ATT_BOUND_MAX = 40.0
HG_CHUNK = 128
HG_SUB = 32
HG_SAFE_DECAY = 80.0
ML_CHUNK = 256
MOE_CAP = 320
MOE_TF = 1408


def _cparams(sem, vmem=VMEM_LIMIT):
    return pltpu.CompilerParams(dimension_semantics=sem, vmem_limit_bytes=vmem)


def _dot(a, b):
    return jnp.dot(a, b, preferred_element_type=F32)


def _dot_nt(a, b):
    return lax.dot_general(a, b, (((1,), (1,)), ((), ())), preferred_element_type=F32)


def _split3(x):
    hi = x.astype(BF16)
    r1 = x - hi.astype(F32)
    mid = r1.astype(BF16)
    lo = (r1 - mid.astype(F32)).astype(BF16)
    return hi, mid, lo


def _norm_mod(x, w, scale, shift):
    y = x * lax.rsqrt(jnp.mean(x * x, axis=-1, keepdims=True) + NORM_EPS) * w
    return y * (1.0 + scale) + shift


def _log_sigmoid(x):
    return jnp.minimum(x, 0.0) - jnp.log1p(jnp.exp(-jnp.abs(x)))


def _mod_kernel(c_ref, w_ref, b_ref, o_ref):
    cnd = c_ref[...]
    cnd = cnd * jax.nn.sigmoid(cnd)
    o_ref[0] = _dot(cnd.astype(BF16), w_ref[0].astype(BF16)) + b_ref[0]


def _mod_call(c8, w_mod, b_mod):
    depth, d, n = w_mod.shape
    tn = 1536 if n % 1536 == 0 else n
    return pl.pallas_call(
        _mod_kernel,
        grid=(depth, n // tn),
        in_specs=[pl.BlockSpec((8, d), lambda l, j: (0, 0)),
                  pl.BlockSpec((1, d, tn), lambda l, j: (l, 0, j)),
                  pl.BlockSpec((1, 1, tn), lambda l, j: (l, 0, j))],
        out_specs=pl.BlockSpec((1, 8, tn), lambda l, j: (l, 0, j)),
        out_shape=jax.ShapeDtypeStruct((depth, 8, n), F32),
        compiler_params=_cparams(("arbitrary", "arbitrary")),
        name="adaln_mod",
    )(c8, w_mod, b_mod)


def _normmod_kernel(x_ref, nw_ref, mod_ref, h_ref):
    m = mod_ref[0]
    h_ref[...] = _norm_mod(x_ref[...], nw_ref[...], m[1:2], m[0:1]).astype(BF16)


def _normmod_call(x, nw, mod, seq, tm):
    t, d = x.shape
    per_b = seq // tm
    return pl.pallas_call(
        _normmod_kernel,
        grid=(t // tm,),
        in_specs=[pl.BlockSpec((tm, d), lambda i: (i, 0)),
                  pl.BlockSpec((1, d), lambda i: (0, 0)),
                  pl.BlockSpec((1, 6, d), lambda i: (i // per_b, 0, 0))],
        out_specs=pl.BlockSpec((tm, d), lambda i: (i, 0)),
        out_shape=jax.ShapeDtypeStruct((t, d), BF16),
        compiler_params=_cparams(("arbitrary",)),
        name="prenorm_mod",
    )(x, nw, mod)


def _mm_kernel(x_ref, w_ref, b_ref, o_ref):
    o_ref[...] = (_dot(x_ref[...], w_ref[...]) + b_ref[...]).astype(o_ref.dtype)


def _matmul(x, w, bias, out_dtype, tm, tn):
    t, k = x.shape
    n = w.shape[1]
    return pl.pallas_call(
        _mm_kernel,
        grid=(n // tn, t // tm),
        in_specs=[pl.BlockSpec((tm, k), lambda j, i: (i, 0)),
                  pl.BlockSpec((k, tn), lambda j, i: (0, j)),
                  pl.BlockSpec((1, tn), lambda j, i: (0, j))],
        out_specs=pl.BlockSpec((tm, tn), lambda j, i: (i, j)),
        out_shape=jax.ShapeDtypeStruct((t, n), out_dtype),
        compiler_params=_cparams(("arbitrary", "arbitrary")),
        name="in_proj",
    )(x, w, bias)


def _attn_prep_kernel(a_ref, qw_ref, kw_ref, kn_ref, qt_ref, vt_ref):
    hw = DA_HEADS * 2 * DA_DK
    a = a_ref[...]
    r = lax.broadcasted_iota(jnp.int32, (hw, hw), 0) // DA_DK
    c = lax.broadcasted_iota(jnp.int32, (hw, hw), 1) // DA_DK
    group = jnp.where(r == c, 1.0, 0.0).astype(BF16)

    def qk_norm(z, w):
        ms = _dot((z * z).astype(BF16), group) * (1.0 / DA_DK)
        return z * lax.rsqrt(ms + NORM_EPS) * w

    qn = qk_norm(a[:, :hw].astype(F32), qw_ref[...])
    kn = qk_norm(a[:, hw:2 * hw].astype(F32), kw_ref[...])
    kn_ref[...] = kn.astype(BF16)
    qt_ref[0] = qn.T.astype(BF16)
    vt_ref[0] = a[:, 2 * hw:].astype(F32).T.astype(BF16)


def _attn_prep_call(a, qw, kw, batch, seq, ts):
    t = a.shape[0]
    hw = DA_HEADS * 2 * DA_DK
    hv = DA_HEADS * DA_DV
    per_b = seq // ts
    return pl.pallas_call(
        _attn_prep_kernel,
        grid=(t // ts,),
        in_specs=[pl.BlockSpec((ts, 2 * hw + hv), lambda i: (i, 0)),
                  pl.BlockSpec((1, hw), lambda i: (0, 0)),
                  pl.BlockSpec((1, hw), lambda i: (0, 0))],
        out_specs=[pl.BlockSpec((ts, hw), lambda i: (i, 0)),
                   pl.BlockSpec((1, hw, ts), lambda i: (i // per_b, 0, i % per_b)),
                   pl.BlockSpec((1, hv, ts), lambda i: (i // per_b, 0, i % per_b))],
        out_shape=[jax.ShapeDtypeStruct((t, hw), BF16),
                   jax.ShapeDtypeStruct((batch, hw, seq), BF16),
                   jax.ShapeDtypeStruct((batch, hv, seq), BF16)],
        compiler_params=_cparams(("arbitrary",)),
        name="attn_prep",
    )(a, qw, kw)


def _attn_kernel(sc_ref, qt_ref, k_ref, vt_ref, sw_ref, o_ref,
                 qz_ref, m_ref, acc_ref, vs_ref, kmax_ref, s_ref, s2_ref, *, tq, seq):
    hp = ATT_HEADS_PER_STEP
    dv = DA_DV
    g = pl.program_id(1)
    i = pl.program_id(2)
    lam = sc_ref[DA_HEADS]
    out_scale = sc_ref[DA_HEADS + 1]
    slopes = [sc_ref[hp * g + hh] for hh in range(hp)]
    row16 = lax.broadcasted_iota(jnp.int32, (ATT_PAD_ROWS, tq), 0)
    ones_rows = jnp.where(row16 == 0, 1.0, 0.0).astype(BF16)

    half = lax.broadcasted_iota(jnp.int32, (1, 2 * DA_DK), 1) < DA_DK

    @pl.when(i == 0)
    def _():
        pos = lax.broadcasted_iota(jnp.int32, (1, seq), 1) & (tq - 1)
        rel = (pos - (tq - 1)).astype(F32)
        sub16 = lax.broadcasted_iota(jnp.int32, (ATT_PAD_ROWS, seq), 0)
        gr = lax.broadcasted_iota(jnp.int32, (2 * DA_DK, 2 * DA_DK), 0) // DA_DK
        gc = lax.broadcasted_iota(jnp.int32, (2 * DA_DK, 2 * DA_DK), 1) // DA_DK
        group = jnp.where(gr == gc, 1.0, 0.0).astype(BF16)
        for hh in range(hp):
            w = jnp.exp2(slopes[hh] * rel)
            vs_ref[hh, 0:dv, :] = (vt_ref[0, hh * dv:(hh + 1) * dv, :].astype(F32) * w).astype(BF16)
            vs_ref[hh, dv:dv + ATT_PAD_ROWS, :] = jnp.where(sub16 == 0, w, 0.0).astype(BF16)

            def knorm(n, best):
                kc = k_ref[pl.ds(pl.multiple_of(n * tq, tq), tq),
                           hh * 2 * DA_DK:(hh + 1) * 2 * DA_DK].astype(F32)
                return jnp.maximum(best, jnp.max(_dot((kc * kc).astype(BF16), group), axis=0, keepdims=True))

            k2 = lax.fori_loop(0, seq // tq, knorm, jnp.zeros((1, 2 * DA_DK), F32))
            kmax_ref[2 * hh] = jnp.max(jnp.where(half, k2, 0.0), axis=1, keepdims=True)
            kmax_ref[2 * hh + 1] = jnp.max(jnp.where(half, 0.0, k2), axis=1, keepdims=True)

    row = lax.broadcasted_iota(jnp.int32, (2 * DA_DK, tq), 0)
    qq_row = lax.broadcasted_iota(jnp.int32, (1, tq), 1).astype(F32)
    bound_max = jnp.zeros((), F32)
    for hh in range(hp):
        qt = qt_ref[0, hh * 2 * DA_DK:(hh + 1) * 2 * DA_DK, :]
        zero = jnp.zeros_like(qt)
        q1 = jnp.where(row < DA_DK, qt, zero)
        q2 = jnp.where(row >= DA_DK, qt, zero)
        qz_ref[hh] = jnp.concatenate([q1, q2], axis=1)
        qn = jnp.concatenate(
            [jnp.sum(jnp.square(q1.astype(F32)), axis=0, keepdims=True) * kmax_ref[2 * hh],
             jnp.sum(jnp.square(q2.astype(F32)), axis=0, keepdims=True) * kmax_ref[2 * hh + 1]], axis=1)
        bound = jnp.sqrt(qn) * ATT_BOUND_MARGIN + ATT_BOUND_EPS
        bound_max = jnp.maximum(bound_max, jnp.max(bound))
        m_ref[hh] = bound + slopes[hh] * jnp.concatenate([qq_row, qq_row], axis=1)
    acc_ref[...] = jnp.zeros(acc_ref.shape, F32)

    kk = lax.broadcasted_iota(jnp.int32, (tq, 2 * tq), 0)
    cc = lax.broadcasted_iota(jnp.int32, (tq, 2 * tq), 1)
    qq = jnp.where(cc >= tq, cc - tq, cc)
    dist = (qq - jnp.abs(qq - kk)).astype(F32)
    visible = (kk // CHUNK) <= (qq // CHUNK)

    def scores(hh, blk):
        start = pl.multiple_of(blk * tq, tq)
        return _dot(k_ref[pl.ds(start, tq), hh * 2 * DA_DK:(hh + 1) * 2 * DA_DK], qz_ref[hh])

    def diag_values(hh):
        start = pl.multiple_of(i * tq, tq)
        return jnp.concatenate([vt_ref[0, hh * dv:(hh + 1) * dv, pl.ds(start, tq)], ones_rows], axis=0)

    @pl.when(bound_max <= ATT_BOUND_MAX)
    def _():
        for hh in range(hp):
            s_ref[hh] = scores(hh, 0)
            s2_ref[hh] = scores(hh, jnp.minimum(1, i))

        def probs(hh, blk, s):
            last_key = ((blk + 1 - i) * tq - 1).astype(F32)
            return jnp.exp2(s - (m_ref[hh] - slopes[hh] * last_key)).astype(BF16)

        def body(jj, carry):
            blk = 2 * jj
            start = pl.multiple_of(blk * tq, 2 * tq)
            for hh in range(hp):
                p = jnp.concatenate([probs(hh, blk, s_ref[hh]), probs(hh, blk + 1, s2_ref[hh])], axis=0)
                s_ref[hh] = scores(hh, jnp.minimum(blk + 2, i))
                s2_ref[hh] = scores(hh, jnp.minimum(blk + 3, i))
                acc_ref[hh] += _dot(vs_ref[hh, :, pl.ds(start, 2 * tq)], p)
            return carry

        lax.fori_loop(0, i // 2, body, 0)

        @pl.when(i % 2 == 1)
        def _():
            start = pl.multiple_of((i - 1) * tq, tq)
            for hh in range(hp):
                acc_ref[hh] += _dot(vs_ref[hh, :, pl.ds(start, tq)], probs(hh, i - 1, s_ref[hh]))
                s_ref[hh] = s2_ref[hh]

        for hh in range(hp):
            s = jnp.where(visible, s_ref[hh] + slopes[hh] * dist, -jnp.inf)
            acc_ref[hh] += _dot(diag_values(hh), jnp.exp2(s - m_ref[hh]).astype(BF16))

    @pl.when(bound_max > ATT_BOUND_MAX)
    def _():
        m_ref[...] = jnp.full(m_ref.shape, -jnp.inf, F32)

        def update(hh, s, bound_shift, v_aug):
            m_old = m_ref[hh]
            m_new = jnp.maximum(m_old, jnp.max(s, axis=0, keepdims=True) + bound_shift)
            p = jnp.exp2(s - (m_new - bound_shift))
            acc_ref[hh] = jnp.exp2(m_old - m_new) * acc_ref[hh] + _dot(v_aug, p.astype(BF16))
            m_ref[hh] = m_new

        def body(j, carry):
            start = pl.multiple_of(j * tq, tq)
            last_key = ((j + 1 - i) * tq - 1).astype(F32)
            for hh in range(hp):
                update(hh, scores(hh, j), slopes[hh] * last_key, vs_ref[hh, :, pl.ds(start, tq)])
            return carry

        lax.fori_loop(0, i, body, 0)
        for hh in range(hp):
            s = jnp.where(visible, scores(hh, i) + slopes[hh] * dist, -jnp.inf)
            update(hh, s, jnp.zeros((), F32), diag_values(hh))

    outs = []
    for hh in range(hp):
        acc = acc_ref[hh]
        o2 = acc[:dv, :] * (1.0 / acc[dv:dv + 1, :])
        o = (o2[:, :tq] - lam * o2[:, tq:]).T
        o = o * lax.rsqrt(jnp.mean(o * o, axis=-1, keepdims=True) + NORM_EPS) * sw_ref[...]
        outs.append(o * out_scale)
    o_ref[...] = jnp.concatenate(outs, axis=1).astype(o_ref.dtype)


def _attn_call(scal, qt, kn, vt, sw, batch, seq, tq):
    t = kn.shape[0]
    nq = seq // tq
    hp = ATT_HEADS_PER_STEP
    return pl.pallas_call(
        functools.partial(_attn_kernel, tq=tq, seq=seq),
        grid=(batch, DA_HEADS // hp, nq),
        in_specs=[pl.BlockSpec(memory_space=pltpu.SMEM),
                  pl.BlockSpec((1, hp * 2 * DA_DK, tq), lambda b, g, i: (b, g, i)),
                  pl.BlockSpec((seq, hp * 2 * DA_DK), lambda b, g, i: (b, g)),
                  pl.BlockSpec((1, hp * DA_DV, seq), lambda b, g, i: (b, g, 0)),
                  pl.BlockSpec((1, DA_DV), lambda b, g, i: (0, 0))],
        out_specs=pl.BlockSpec((tq, hp * DA_DV), lambda b, g, i: (b * nq + i, g)),
        out_shape=jax.ShapeDtypeStruct((t, DA_HEADS * DA_DV), BF16),
        scratch_shapes=[pltpu.VMEM((hp, 2 * DA_DK, 2 * tq), BF16),
                        pltpu.VMEM((hp, 1, 2 * tq), F32),
                        pltpu.VMEM((hp, DA_DV + ATT_PAD_ROWS, 2 * tq), F32),
                        pltpu.VMEM((hp, DA_DV + ATT_PAD_ROWS, seq), BF16),
                        pltpu.VMEM((2 * hp, 1, 1), F32),
                        pltpu.VMEM((hp, tq, 2 * tq), F32),
                        pltpu.VMEM((hp, tq, 2 * tq), F32)],
        compiler_params=_cparams(("arbitrary", "arbitrary", "arbitrary")),
        name="diff_attn",
    )(scal, qt, kn, vt, sw)


def _hgrn_kernel(q_ref, f_ref, i_ref, g_ref, lb_ref, gw_ref, o_ref,
                 st_ref, b_ref, qs_ref, ks_ref, oi_ref, *, n_chunks):
    c = HG_CHUNK
    sub = HG_SUB
    d = HG_D
    nh = HG_HEADS

    @pl.when(pl.program_id(1) == 0)
    def _():
        st_ref[...] = jnp.zeros(st_ref.shape, F32)

    gw = gw_ref[...]
    rr = lax.broadcasted_iota(jnp.int32, (c, c), 0)
    cc = lax.broadcasted_iota(jnp.int32, (c, c), 1)
    causal = cc <= rr
    tril = jnp.where(causal, 1.0, 0.0).astype(BF16)

    def chunk(n, carry):
        r0 = pl.multiple_of(n * c, c)
        heads = []
        decay = jnp.zeros((), F32)
        for hh in range(nh):
            cols = slice(hh * d, (hh + 1) * d)
            log_lb = lb_ref[hh, 0:1, :]
            q = q_ref[pl.ds(r0, c), cols]
            q = q * jax.nn.sigmoid(q)
            a = lb_ref[hh, 1:2, :] + _log_sigmoid(f_ref[pl.ds(r0, c), cols])
            logf = jnp.maximum(log_lb, a) + jnp.log1p(jnp.exp(-jnp.abs(log_lb - a)))
            k = 1.0 - jnp.exp(logf)
            v = i_ref[pl.ds(r0, c), cols]
            hi, mid, lo = _split3(logf)
            b = _dot(tril, hi) + _dot(tril, mid) + _dot(tril, lo)
            b_last = b[c - 1:c, :]
            st = st_ref[hh]
            o_inter = _dot_nt((q * jnp.exp(b)).astype(BF16), st.astype(BF16))
            k_hat = k * jnp.exp(b_last - b)
            st_ref[hh] = st * jnp.exp(b_last) + _dot(v.T.astype(BF16), k_hat.astype(BF16))
            betas = []
            for blk in range(c // sub):
                beta = jnp.zeros((1, d), F32) if blk == 0 else b[blk * sub - 1:blk * sub, :]
                betas.append(beta)
                b_end = b[(blk + 1) * sub - 1:(blk + 1) * sub, :]
                decay = jnp.maximum(decay, jnp.max(beta - b_end))
            heads.append((q, k, v, b, betas, o_inter))

        @pl.when(decay < HG_SAFE_DECAY)
        def _():
            for hh, (q, k, v, b, betas, _) in enumerate(heads):
                rows = []
                for blk in range(c // sub):
                    beta = betas[blk]
                    q_t = q[blk * sub:(blk + 1) * sub, :] * jnp.exp(b[blk * sub:(blk + 1) * sub, :] - beta)
                    k_t = k * jnp.exp(jnp.minimum(beta - b, HG_SAFE_DECAY))
                    rows.append(_dot_nt(q_t.astype(BF16), k_t.astype(BF16)))
                att = jnp.where(causal, jnp.concatenate(rows, axis=0), 0.0)
                oi_ref[hh] = _dot(att.astype(BF16), v.astype(BF16))

        @pl.when(decay >= HG_SAFE_DECAY)
        def _():
            ridx = lax.broadcasted_iota(jnp.int32, (c, d), 0)
            for hh, (q, k, v, b, _, _) in enumerate(heads):
                b_ref[...] = b
                qs_ref[...] = q
                ks_ref[...] = k

                def row(t, carry2):
                    bt = b_ref[pl.ds(t, 1), :]
                    e = jnp.exp(jnp.where(ridx <= t, bt - b_ref[...], -jnp.inf))
                    w = jnp.sum(qs_ref[pl.ds(t, 1), :] * ks_ref[...] * e, axis=1, keepdims=True)
                    oi_ref[hh, pl.ds(t, 1), :] = jnp.sum(w * v, axis=0, keepdims=True)
                    return carry2

                lax.fori_loop(0, c, row, 0)

        for hh in range(nh):
            cols = slice(hh * d, (hh + 1) * d)
            o = heads[hh][5] + oi_ref[hh]
            o = o * lax.rsqrt(jnp.mean(o * o, axis=-1, keepdims=True) + NORM_EPS) * gw
            g = g_ref[pl.ds(r0, c), cols]
            o_ref[pl.ds(r0, c), cols] = (o * (g * jax.nn.sigmoid(g))).astype(o_ref.dtype)
        return carry

    lax.fori_loop(0, n_chunks, chunk, 0)


def _hgrn_call(bproj, lbs, gw, batch, seq, ts):
    t = bproj.shape[0]
    d = HG_D
    nh = HG_HEADS
    ns = seq // ts
    spec = lambda off: pl.BlockSpec((ts, nh * d), lambda b, i: (b * ns + i, off))
    return pl.pallas_call(
        functools.partial(_hgrn_kernel, n_chunks=ts // HG_CHUNK),
        grid=(batch, ns),
        in_specs=[spec(0), spec(1), spec(2), spec(3),
                  pl.BlockSpec((nh, 2, d), lambda b, i: (0, 0, 0)),
                  pl.BlockSpec((1, d), lambda b, i: (0, 0))],
        out_specs=pl.BlockSpec((ts, nh * d), lambda b, i: (b * ns + i, 0)),
        out_shape=jax.ShapeDtypeStruct((t, nh * d), BF16),
        scratch_shapes=[pltpu.VMEM((nh, d, d), F32),
                        pltpu.VMEM((HG_CHUNK, d), F32),
                        pltpu.VMEM((HG_CHUNK, d), F32),
                        pltpu.VMEM((HG_CHUNK, d), F32),
                        pltpu.VMEM((nh, HG_CHUNK, d), F32)],
        compiler_params=_cparams(("arbitrary", "arbitrary")),
        name="hgrn2",
    )(bproj, bproj, bproj, bproj, lbs, gw)


def _mlstm_kernel(uq_ref, uk_ref, v_ref, op_ref, gi_ref, gf_ref, cwq_ref, cwk_ref,
                  cbq_ref, cbk_ref, nw_ref, o_ref,
                  xq_ref, xk_ref, c_ref, m_ref, bs_ref):
    L = ML_CHUNK
    d = ML_D
    nh = ML_HEADS
    i = pl.program_id(1)
    rr = lax.broadcasted_iota(jnp.int32, (L, L), 0)
    cc = lax.broadcasted_iota(jnp.int32, (L, L), 1)

    @pl.when(i == 0)
    def _():
        c_ref[...] = jnp.zeros(c_ref.shape, F32)
        m_ref[...] = jnp.zeros(m_ref.shape, F32)
        xq_ref[0:8, :] = jnp.zeros((8, nh * d), F32)
        xk_ref[0:8, :] = jnp.zeros((8, nh * d), F32)
        upper = jnp.where(rr <= cc, 1.0, 0.0).astype(BF16)
        for hh in range(nh):
            hi, mid, lo = _split3(_log_sigmoid(gf_ref[0, hh]))
            bs_ref[hh] = _dot(hi, upper) + _dot(mid, upper) + _dot(lo, upper)

    def conv_silu(u_ref, x_ref, w_ref, b_ref):
        x_ref[8:8 + L, :] = u_ref[...]
        y = b_ref[...] + w_ref[ML_CONV - 1:ML_CONV, :] * x_ref[8:8 + L, :]
        for j in range(ML_CONV - 1):
            y = y + w_ref[j:j + 1, :] * x_ref[5 + j:5 + j + L, :]
        x_ref[0:8, :] = x_ref[L:L + 8, :]
        return y * jax.nn.sigmoid(y)

    q_all = conv_silu(uq_ref, xq_ref, cwq_ref, cbq_ref)
    k_all = conv_silu(uk_ref, xk_ref, cwk_ref, cbk_ref) * (d ** -0.5)
    lane = lax.broadcasted_iota(jnp.int32, (L, d), 1)
    ones_col = jnp.where(lane == 0, 1.0, 0.0)
    for hh in range(nh):
        cols = slice(hh * d, (hh + 1) * d)
        q = q_all[:, cols].astype(BF16)
        kt = k_all[:, cols].T
        v_aug = jnp.concatenate([v_ref[:, cols], ones_col], axis=1).astype(BF16)

        b_row = bs_ref[hh, pl.ds(i, 1), :]
        ig_row = gi_ref[0, hh, pl.ds(i, 1), :]
        m_prev = m_ref[hh]
        g = b_row[:, L - 1:L]
        b_col = jnp.sum(jnp.where(rr == cc, b_row, 0.0), axis=1, keepdims=True)
        log_d = jnp.where(cc <= rr, b_col + (ig_row - b_row), -jnp.inf)
        log_inter = b_col + m_prev
        m_t = jnp.maximum(log_inter, jnp.max(log_d, axis=1, keepdims=True))
        w_intra = jnp.exp(log_d - m_t) * _dot(q, kt.astype(BF16))
        w_inter = jnp.exp(log_inter - m_t)
        c_aug = c_ref[hh]
        tot = w_inter * _dot(q, c_aug.astype(BF16)) + _dot(w_intra.astype(BF16), v_aug)
        denom = jnp.maximum(jnp.abs(tot[:, d:d + 1]), jnp.exp(-m_t))
        hout = tot[:, :d] / denom
        hout = hout * lax.rsqrt(jnp.mean(hout * hout, axis=-1, keepdims=True) + NORM_EPS) * nw_ref[...]
        o_ref[:, cols] = (hout * jax.nn.sigmoid(op_ref[:, cols])).astype(o_ref.dtype)

        log_w = g - b_row + ig_row
        m_new = jnp.maximum(g + m_prev, jnp.max(log_w, axis=1, keepdims=True))
        w_s = jnp.exp(log_w - m_new)
        c_ref[hh] = jnp.exp(g + m_prev - m_new) * c_aug + _dot((kt * w_s).astype(BF16), v_aug)
        m_ref[hh] = m_new


def _mlstm_call(cproj, gi, gf, conv_w, conv_b, nw, batch, seq):
    t = cproj.shape[0]
    d = ML_D
    nh = ML_HEADS
    L = ML_CHUNK
    nc = seq // L
    spec = lambda off: pl.BlockSpec((L, nh * d), lambda b, i: (b * nc + i, off))
    gspec = pl.BlockSpec((1, nh, nc, L), lambda b, i: (b, 0, 0, 0))
    return pl.pallas_call(
        _mlstm_kernel,
        grid=(batch, nc),
        in_specs=[spec(0), spec(1), spec(2), spec(3), gspec, gspec,
                  pl.BlockSpec((ML_CONV, nh * d), lambda b, i: (0, 0)),
                  pl.BlockSpec((ML_CONV, nh * d), lambda b, i: (0, 1)),
                  pl.BlockSpec((1, nh * d), lambda b, i: (0, 0)),
                  pl.BlockSpec((1, nh * d), lambda b, i: (0, 1)),
                  pl.BlockSpec((1, d), lambda b, i: (0, 0))],
        out_specs=pl.BlockSpec((L, nh * d), lambda b, i: (b * nc + i, 0)),
        out_shape=jax.ShapeDtypeStruct((t, nh * d), BF16),
        scratch_shapes=[pltpu.VMEM((L + 8, nh * d), F32),
                        pltpu.VMEM((L + 8, nh * d), F32),
                        pltpu.VMEM((nh, d, 2 * d), F32),
                        pltpu.VMEM((nh, 1, 1), F32),
                        pltpu.VMEM((nh, nc, L), F32)],
        compiler_params=_cparams(("arbitrary", "arbitrary")),
        name="mlstm",
    )(cproj, cproj, cproj, cproj, gi, gf, conv_w, conv_w, conv_b, conv_b, nw)


def _top2_combine(logits):
    lane = lax.broadcasted_iota(jnp.int32, logits.shape, 1)
    lg = jnp.where(lane < N_EXPERTS, logits, -jnp.inf)
    ex = jnp.exp(lg - jnp.max(lg, axis=1, keepdims=True))
    probs = ex / jnp.sum(ex, axis=1, keepdims=True)
    p1 = jnp.max(probs, axis=1, keepdims=True)
    i1 = jnp.min(jnp.where(probs == p1, lane, LANES), axis=1, keepdims=True)
    rest = jnp.where(lane == i1, -1.0, probs)
    p2 = jnp.max(rest, axis=1, keepdims=True)
    i2 = jnp.min(jnp.where(rest == p2, lane, LANES), axis=1, keepdims=True)
    comb = jnp.where(lane == i1, p1, 0.0) + jnp.where(lane == i2, p2, 0.0)
    return comb / (p1 + p2)


def _merge_kernel(*refs, route):
    if route:
        (ya_ref, yb_ref, yc_ref, gp_ref, wb_ref, wo_ref, x_ref, mod_ref, nw_ref,
         rw_ref, rb_ref, xo_ref, h_ref, cmb_ref) = refs
    else:
        (ya_ref, yb_ref, yc_ref, gp_ref, wb_ref, wo_ref, x_ref, mod_ref, nw_ref,
         xo_ref, h_ref) = refs
    d = x_ref.shape[1]
    merged = None
    for n, y_ref in enumerate((ya_ref, yb_ref, yc_ref)):
        gate = jax.nn.sigmoid(gp_ref[:, n * d:(n + 1) * d].astype(F32))
        term = gate * _dot(y_ref[...], wb_ref[n])
        merged = term if merged is None else merged + term
    m = mod_ref[0]
    xn = x_ref[...] + m[2:3] * _dot(merged.astype(BF16), wo_ref[...])
    xo_ref[...] = xn
    h2 = _norm_mod(xn, nw_ref[...], m[4:5], m[3:4])
    h_ref[...] = h2.astype(BF16)
    if route:
        h_hi, h_mid, _ = _split3(h2)
        r_hi, r_mid, _ = _split3(rw_ref[...])
        logits = _dot(h_hi, r_hi) + _dot(h_mid, r_hi) + _dot(h_hi, r_mid) + rb_ref[...]
        cmb_ref[...] = _top2_combine(logits)


def _merge_call(ya, yb, yc, gp, wb, wo, x, mod, nw, router, seq, tm):
    t, d = x.shape
    bw = ya.shape[1]
    per_b = seq // tm
    route = router is not None
    tok = lambda w: pl.BlockSpec((tm, w), lambda i: (i, 0))
    const2 = lambda s: pl.BlockSpec(s, lambda i: (0, 0))
    in_specs = [tok(bw), tok(bw), tok(bw), tok(3 * d),
                pl.BlockSpec((3, bw, d), lambda i: (0, 0, 0)), const2((d, d)), tok(d),
                pl.BlockSpec((1, 6, d), lambda i: (i // per_b, 0, 0)), const2((1, d))]
    out_specs = [tok(d), tok(d)]
    out_shape = [jax.ShapeDtypeStruct((t, d), F32), jax.ShapeDtypeStruct((t, d), BF16)]
    args = [ya, yb, yc, gp, wb, wo, x, mod, nw]
    if route:
        in_specs += [const2((d, LANES)), const2((1, LANES))]
        out_specs.append(tok(LANES))
        out_shape.append(jax.ShapeDtypeStruct((t, LANES), F32))
        args += list(router)
    return pl.pallas_call(
        functools.partial(_merge_kernel, route=route),
        grid=(t // tm,),
        in_specs=in_specs, out_specs=out_specs, out_shape=out_shape,
        compiler_params=_cparams(("arbitrary",)),
        name="merge_out",
    )(*args)


def _finish(acc, x_ref, mod_ref, xo_ref, nxt):
    xn = x_ref[...] + mod_ref[0][5:6] * acc
    xo_ref[...] = xn
    if nxt is not None:
        nw_ref, modn_ref, hn_ref = nxt
        mn = modn_ref[0]
        hn_ref[...] = _norm_mod(xn, nw_ref[...], mn[1:2], mn[0:1]).astype(BF16)


def _ffn_kernel(*refs, has_next):
    if has_next:
        h_ref, w1_ref, w3_ref, w2_ref, x_ref, mod_ref, nw_ref, modn_ref, xo_ref, hn_ref, acc_ref = refs
        nxt = (nw_ref, modn_ref, hn_ref)
    else:
        h_ref, w1_ref, w3_ref, w2_ref, x_ref, mod_ref, xo_ref, acc_ref = refs
        nxt = None
    f = pl.program_id(1)

    @pl.when(f == 0)
    def _():
        acc_ref[...] = jnp.zeros(acc_ref.shape, F32)

    h = h_ref[...]
    a = _dot(h, w1_ref[...])
    act = a * jax.nn.sigmoid(a) * _dot(h, w3_ref[...])
    acc_ref[...] += _dot(act.astype(BF16), w2_ref[...])

    @pl.when(f == pl.num_programs(1) - 1)
    def _():
        _finish(acc_ref[...], x_ref, mod_ref, xo_ref, nxt)


def _ffn_call(h, w1, w3, w2, x, mod, nxt, seq, tm, tf):
    t, d = x.shape
    ff = w1.shape[1]
    per_b = seq // tm
    tok = lambda: pl.BlockSpec((tm, d), lambda i, f: (i, 0))
    modspec = lambda: pl.BlockSpec((1, 6, d), lambda i, f: (i // per_b, 0, 0))
    in_specs = [tok(), pl.BlockSpec((d, tf), lambda i, f: (0, f)),
                pl.BlockSpec((d, tf), lambda i, f: (0, f)),
                pl.BlockSpec((tf, d), lambda i, f: (f, 0)), tok(), modspec()]
    out_specs = [tok()]
    out_shape = [jax.ShapeDtypeStruct((t, d), F32)]
    args = [h, w1, w3, w2, x, mod]
    if nxt is not None:
        in_specs += [pl.BlockSpec((1, d), lambda i, f: (0, 0)), modspec()]
        out_specs.append(tok())
        out_shape.append(jax.ShapeDtypeStruct((t, d), BF16))
        args += list(nxt)
    return pl.pallas_call(
        functools.partial(_ffn_kernel, has_next=nxt is not None),
        grid=(t // tm, ff // tf),
        in_specs=in_specs, out_specs=out_specs, out_shape=out_shape,
        scratch_shapes=[pltpu.VMEM((tm, d), F32)],
        compiler_params=_cparams(("arbitrary", "arbitrary")),
        name="ffn_swiglu",
    )(*args)


def _moe_kernel(h_ref, cmb_ref, w1_ref, w3_ref, w2_ref, y_ref,
                acc_ref, xg_ref, ya_ref, rk_ref, rkt_ref, cnt_ref):
    e = pl.program_id(1)
    f = pl.program_id(2)
    tm = h_ref.shape[0]
    cap = MOE_CAP
    cap_pad = -(-cap // LANES) * LANES

    @pl.when((e == 0) & (f == 0))
    def _():
        r = lax.broadcasted_iota(jnp.int32, (tm, tm), 0)
        c = lax.broadcasted_iota(jnp.int32, (tm, tm), 1)
        before = jnp.where(c < r, 1.0, 0.0).astype(BF16)
        sel = cmb_ref[...] > 0.0
        rank = _dot(before, jnp.where(sel, 1.0, 0.0).astype(BF16))
        rk = jnp.where(sel, rank, -1.0)
        rk_ref[...] = rk
        rkt_ref[...] = rk.T
        cnt_ref[...] = jnp.sum(jnp.where(sel, 1.0, 0.0), axis=0, keepdims=True)
        acc_ref[...] = jnp.zeros(acc_ref.shape, F32)

    lane1 = lax.broadcasted_iota(jnp.int32, (1, LANES), 1)
    n_e = jnp.sum(jnp.where(lane1 == e, cnt_ref[...], 0.0)).astype(jnp.int32)
    n_blocks = (n_e + cap - 1) // cap

    @pl.when(f == 0)
    def _():
        rank_row = rkt_ref[pl.ds(e, 1), :]

        def gather(sb, carry):
            r0 = pl.multiple_of(sb * cap, cap)
            slot = (sb * cap + lax.broadcasted_iota(jnp.int32, (cap, tm), 0)).astype(F32)
            onehot = jnp.where(rank_row == slot, 1.0, 0.0).astype(BF16)
            xg_ref[pl.ds(r0, cap), :] = _dot(onehot, h_ref[...]).astype(BF16)
            return carry

        lax.fori_loop(0, n_blocks, gather, 0)

    def expert(sb, carry):
        r0 = pl.multiple_of(sb * cap, cap)
        xs = xg_ref[pl.ds(r0, cap), :]
        a = _dot(xs, w1_ref[0])
        act = a * jax.nn.sigmoid(a) * _dot(xs, w3_ref[0])
        part = _dot(act.astype(BF16), w2_ref[0])

        @pl.when(f == 0)
        def _():
            ya_ref[pl.ds(r0, cap), :] = part

        @pl.when(f > 0)
        def _():
            ya_ref[pl.ds(r0, cap), :] += part

        return carry

    lax.fori_loop(0, n_blocks, expert, 0)

    @pl.when(f == pl.num_programs(2) - 1)
    def _():
        lane = lax.broadcasted_iota(jnp.int32, (tm, LANES), 1)
        rank_col = jnp.sum(jnp.where(lane == e, rk_ref[...], 0.0), axis=1, keepdims=True)
        w_col = jnp.sum(jnp.where(lane == e, cmb_ref[...], 0.0), axis=1, keepdims=True)
        col = lax.broadcasted_iota(jnp.int32, (tm, cap_pad), 1)

        def scatter(sb, carry):
            r0 = pl.multiple_of(sb * cap, cap)
            ys = ya_ref[pl.ds(r0, cap), :].astype(BF16)
            ys = jnp.concatenate([ys, jnp.zeros((cap_pad - cap, ys.shape[1]), BF16)], axis=0)
            hit = (rank_col == (sb * cap + col).astype(F32)) & (col < cap)
            acc_ref[...] += w_col * _dot(jnp.where(hit, 1.0, 0.0).astype(BF16), ys)
            return carry

        lax.fori_loop(0, n_blocks, scatter, 0)

    @pl.when((e == pl.num_programs(1) - 1) & (f == pl.num_programs(2) - 1))
    def _():
        y_ref[...] = acc_ref[...].astype(y_ref.dtype)


def _moe_call(h, cmb, w1, w3, w2, tm, tf):
    t, d = h.shape
    ne, _, ff = w1.shape
    rows = -(-tm // MOE_CAP) * MOE_CAP
    return pl.pallas_call(
        _moe_kernel,
        grid=(t // tm, ne, ff // tf),
        in_specs=[pl.BlockSpec((tm, d), lambda i, e, f: (i, 0)),
                  pl.BlockSpec((tm, LANES), lambda i, e, f: (i, 0)),
                  pl.BlockSpec((1, d, tf), lambda i, e, f: (e, 0, f)),
                  pl.BlockSpec((1, d, tf), lambda i, e, f: (e, 0, f)),
                  pl.BlockSpec((1, tf, d), lambda i, e, f: (e, f, 0))],
        out_specs=pl.BlockSpec((tm, d), lambda i, e, f: (i, 0)),
        out_shape=jax.ShapeDtypeStruct((t, d), BF16),
        scratch_shapes=[pltpu.VMEM((tm, d), F32),
                        pltpu.VMEM((rows, d), BF16),
                        pltpu.VMEM((rows, d), F32),
                        pltpu.VMEM((tm, LANES), F32),
                        pltpu.VMEM((LANES, tm), F32),
                        pltpu.VMEM((1, LANES), F32)],
        compiler_params=_cparams(("arbitrary", "arbitrary", "arbitrary")),
        name="moe_top2",
    )(h, cmb, w1, w3, w2)


def _resid_kernel(*refs, has_next):
    if has_next:
        y_ref, x_ref, mod_ref, nw_ref, modn_ref, xo_ref, hn_ref = refs
        nxt = (nw_ref, modn_ref, hn_ref)
    else:
        y_ref, x_ref, mod_ref, xo_ref = refs
        nxt = None
    _finish(y_ref[...].astype(F32), x_ref, mod_ref, xo_ref, nxt)


def _resid_call(y, x, mod, nxt, seq, tm):
    t, d = x.shape
    per_b = seq // tm
    tok = lambda: pl.BlockSpec((tm, d), lambda i: (i, 0))
    modspec = lambda: pl.BlockSpec((1, 6, d), lambda i: (i // per_b, 0, 0))
    in_specs = [tok(), tok(), modspec()]
    out_specs = [tok()]
    out_shape = [jax.ShapeDtypeStruct((t, d), F32)]
    args = [y, x, mod]
    if nxt is not None:
        in_specs += [pl.BlockSpec((1, d), lambda i: (0, 0)), modspec()]
        out_specs.append(tok())
        out_shape.append(jax.ShapeDtypeStruct((t, d), BF16))
        args += list(nxt)
    return pl.pallas_call(
        functools.partial(_resid_kernel, has_next=nxt is not None),
        grid=(t // tm,),
        in_specs=in_specs, out_specs=out_specs, out_shape=out_shape,
        compiler_params=_cparams(("arbitrary",)),
        name="moe_residual",
    )(*args)


def _attn_col_perm():
    hq = DA_HEADS * DA_DK
    idx = []
    for base in (0, 2 * hq):
        for h in range(DA_HEADS):
            idx += list(range(base + h * DA_DK, base + (h + 1) * DA_DK))
            idx += list(range(base + hq + h * DA_DK, base + hq + (h + 1) * DA_DK))
    idx += list(range(4 * hq, 4 * hq + DA_HEADS * DA_DV))
    return np.asarray(idx, np.int32)


def _pick(n, pref):
    for c in pref:
        if n % c == 0:
            return c
    return n


def kernel(x, c, w_mod, b_mod, norm1_w, norm2_w, w_in, c_conv_w, c_conv_b, a_qnorm_w, a_knorm_w,
           a_lambda_q1, a_lambda_k1, a_lambda_q2, a_lambda_k2, a_subln_w, b_lb_logits, b_gnorm_w,
           c_igate_b, c_fgate_b, c_norm_w, w_branch, w_out, ffn_w1, ffn_w3, ffn_w2,
           moe_router_w, moe_router_b, moe_w1, moe_w3, moe_w2):
    batch, seq, d = x.shape
    depth = w_in.shape[0]
    t = batch * seq
    n_a = 4 * DA_HEADS * DA_DK + DA_HEADS * DA_DV
    n_b = 4 * HG_HEADS * HG_D
    n_c = 4 * ML_HEADS * ML_D + 2 * ML_HEADS
    n_c_pad = -(-n_c // LANES) * LANES
    tm = _pick(seq, (1024, 512, 256))
    tq = _pick(seq, (ATT_TQ,))
    ts_h = _pick(seq, (1024, 512, 256, 128))

    c8 = jnp.zeros((8, d), F32).at[:batch].set(c)
    mod = _mod_call(c8, w_mod, b_mod.reshape(depth, 1, 6 * d))[:, :batch].reshape(depth, batch, 6, d)

    lb_all = jnp.cumsum(jax.nn.softmax(b_lb_logits.astype(F32), axis=0), axis=0)
    lb_all = lb_all - lb_all[:1]
    slopes = jnp.asarray(LOG2E * 2.0 ** (-8.0 * np.arange(1, DA_HEADS + 1) / DA_HEADS), F32)

    xf = x.reshape(t, d)
    h = _normmod_call(xf, norm1_w[0].reshape(1, d), mod[0], seq, tm)
    perm_a = _attn_col_perm()
    for l in range(depth):
        lam_init = 0.8 - 0.6 * math.exp(-0.3 * l)
        wl = w_in[l]
        w_a = wl[:, :n_a][:, perm_a].astype(BF16)
        w_b = wl[:, n_a:n_a + n_b].astype(BF16)
        w_c = jnp.pad(wl[:, n_a + n_b:n_a + n_b + n_c], ((0, 0), (0, n_c_pad - n_c))).astype(BF16)
        w_g = wl[:, n_a + n_b + n_c:].astype(BF16)
        bias_c = jnp.zeros((1, n_c_pad), F32)
        bias_c = bias_c.at[0, n_c - 2 * ML_HEADS:n_c - ML_HEADS].set(c_igate_b[l])
        bias_c = bias_c.at[0, n_c - ML_HEADS:n_c].set(c_fgate_b[l])
        pa = _matmul(h, w_a, jnp.zeros((1, n_a), F32), BF16, tm, 512)
        pb = _matmul(h, w_b, jnp.zeros((1, n_b), F32), F32, tm, 512)
        pc = _matmul(h, w_c, bias_c, F32, 512, n_c_pad)
        pg = _matmul(h, w_g, jnp.zeros((1, 3 * d), F32), BF16, tm, 1024)

        qw = (jnp.tile(a_qnorm_w[l], 2 * DA_HEADS) * (DA_DK ** -0.5 * LOG2E)).reshape(1, -1)
        kw = jnp.tile(a_knorm_w[l], 2 * DA_HEADS).reshape(1, -1)
        kn, qt, vt = _attn_prep_call(pa, qw, kw, batch, seq, _pick(seq, (512, 256)))
        lam = (jnp.exp(jnp.sum(a_lambda_q1[l] * a_lambda_k1[l]))
               - jnp.exp(jnp.sum(a_lambda_q2[l] * a_lambda_k2[l])) + lam_init)
        scal = jnp.concatenate([slopes, jnp.stack([lam, jnp.asarray(1.0 - lam_init, F32)])]).astype(F32)
        y_a = _attn_call(scal, qt, kn, vt, a_subln_w[l].reshape(1, -1), batch, seq, tq)

        lb = lb_all[l].reshape(HG_HEADS, 1, HG_D)
        lbs = jnp.concatenate([jnp.log(lb), jnp.log1p(-lb)], axis=1)
        y_b = _hgrn_call(pb, lbs, b_gnorm_w[l].reshape(1, -1), batch, seq, ts_h)

        gates = pc[:, n_c - 2 * ML_HEADS:n_c].reshape(batch, seq, 2, ML_HEADS)
        gates = gates.transpose(2, 0, 3, 1).reshape(2, batch, ML_HEADS, seq // ML_CHUNK, ML_CHUNK)
        y_c = _mlstm_call(pc, gates[0], gates[1], c_conv_w[l], c_conv_b[l].reshape(1, -1),
                          c_norm_w[l].reshape(1, -1), batch, seq)

        dense = l % 2 == 0
        router = None
        if not dense:
            rw = jnp.pad(moe_router_w[l // 2], ((0, 0), (0, LANES - N_EXPERTS)))
            rb = jnp.pad(moe_router_b[l // 2], (0, LANES - N_EXPERTS)).reshape(1, LANES)
            router = (rw, rb)
        outs = _merge_call(y_a, y_b, y_c, pg, w_branch[l].astype(BF16), w_out[l].astype(BF16), xf,
                           mod[l], norm2_w[l].reshape(1, d), router, seq, 512)
        xf, h2 = outs[0], outs[1]
        nxt = None if l == depth - 1 else (norm1_w[l + 1].reshape(1, d), mod[l + 1])
        if dense:
            res = _ffn_call(h2, ffn_w1[l // 2].astype(BF16), ffn_w3[l // 2].astype(BF16),
                            ffn_w2[l // 2].astype(BF16), xf, mod[l], nxt, seq, tm, 256)
        else:
            y = _moe_call(h2, outs[2], moe_w1[l // 2].astype(BF16), moe_w3[l // 2].astype(BF16),
                          moe_w2[l // 2].astype(BF16), tm, _pick(moe_w1.shape[-1], (MOE_TF,)))
            res = _resid_call(y, xf, mod[l], nxt, seq, tm)
        xf = res[0]
        if nxt is not None:
            h = res[1]
    return xf.reshape(batch, seq, d)
```

```python
import functools
import math

import numpy as np
import jax
import jax.numpy as jnp
from jax import lax
from jax.experimental import pallas as pl
from jax.experimental.pallas import tpu as pltpu

F32 = jnp.float32
BF16 = jnp.bfloat16
NORM_EPS = 1e-6

CHUNK = 64
DA_HEADS, DA_DK, DA_DV = 4, 64, 128
HG_HEADS, HG_D = 4, 128
ML_HEADS, ML_D, ML_CONV = 4, 128, 4
N_EXPERTS = 8

LANES = 128
VMEM_LIMIT = 56 * 1024 * 1024

ATT_TQ = 256
ATT_HEADS_PER_STEP = 2
ATT_PAD_ROWS = 16
LOG2E = math.log2(math.e)
ATT_BOUND_MARGIN = 1.01
ATT_BOUND_EPS = 1e-3
ATT_BOUND_MAX = 40.0
ATT_ZERO_EXP = 160.0
HG_CHUNK = 128
HG_SUB = 32
HG_SAFE_DECAY = 80.0
ML_CHUNK = 256
MOE_CAP = 288
MOE_TF = 1408


def _cparams(sem, vmem=VMEM_LIMIT):
    return pltpu.CompilerParams(dimension_semantics=sem, vmem_limit_bytes=vmem)


def _dot(a, b):
    return jnp.dot(a, b, preferred_element_type=F32)


def _dot_nt(a, b):
    return lax.dot_general(a, b, (((1,), (1,)), ((), ())), preferred_element_type=F32)


def _split3(x):
    hi = x.astype(BF16)
    r1 = x - hi.astype(F32)
    mid = r1.astype(BF16)
    lo = (r1 - mid.astype(F32)).astype(BF16)
    return hi, mid, lo


def _norm_mod(x, w, scale, shift):
    y = x * lax.rsqrt(jnp.mean(x * x, axis=-1, keepdims=True) + NORM_EPS) * w
    return y * (1.0 + scale) + shift


def _log_sigmoid(x):
    return jnp.minimum(x, 0.0) - jnp.log1p(jnp.exp(-jnp.abs(x)))


def _mod_kernel(c_ref, w_ref, b_ref, o_ref):
    cnd = c_ref[...]
    cnd = cnd * jax.nn.sigmoid(cnd)
    o_ref[0] = _dot(cnd.astype(BF16), w_ref[0].astype(BF16)) + b_ref[0]


def _mod_call(c8, w_mod, b_mod):
    depth, d, n = w_mod.shape
    tn = 1536 if n % 1536 == 0 else n
    return pl.pallas_call(
        _mod_kernel,
        grid=(depth, n // tn),
        in_specs=[pl.BlockSpec((8, d), lambda l, j: (0, 0)),
                  pl.BlockSpec((1, d, tn), lambda l, j: (l, 0, j)),
                  pl.BlockSpec((1, 1, tn), lambda l, j: (l, 0, j))],
        out_specs=pl.BlockSpec((1, 8, tn), lambda l, j: (l, 0, j)),
        out_shape=jax.ShapeDtypeStruct((depth, 8, n), F32),
        compiler_params=_cparams(("arbitrary", "arbitrary")),
        name="adaln_mod",
    )(c8, w_mod, b_mod)


def _normmod_kernel(x_ref, nw_ref, mod_ref, h_ref):
    m = mod_ref[0]
    h_ref[...] = _norm_mod(x_ref[...], nw_ref[...], m[1:2], m[0:1]).astype(BF16)


def _normmod_call(x, nw, mod, seq, tm):
    t, d = x.shape
    per_b = seq // tm
    return pl.pallas_call(
        _normmod_kernel,
        grid=(t // tm,),
        in_specs=[pl.BlockSpec((tm, d), lambda i: (i, 0)),
                  pl.BlockSpec((1, d), lambda i: (0, 0)),
                  pl.BlockSpec((1, 6, d), lambda i: (i // per_b, 0, 0))],
        out_specs=pl.BlockSpec((tm, d), lambda i: (i, 0)),
        out_shape=jax.ShapeDtypeStruct((t, d), BF16),
        compiler_params=_cparams(("arbitrary",)),
        name="prenorm_mod",
    )(x, nw, mod)


def _mm_kernel(x_ref, w_ref, b_ref, o_ref):
    o_ref[...] = (_dot(x_ref[...], w_ref[...]) + b_ref[...]).astype(o_ref.dtype)


def _matmul(x, w, bias, out_dtype, tm, tn):
    t, k = x.shape
    n = w.shape[1]
    return pl.pallas_call(
        _mm_kernel,
        grid=(n // tn, t // tm),
        in_specs=[pl.BlockSpec((tm, k), lambda j, i: (i, 0)),
                  pl.BlockSpec((k, tn), lambda j, i: (0, j)),
                  pl.BlockSpec((1, tn), lambda j, i: (0, j))],
        out_specs=pl.BlockSpec((tm, tn), lambda j, i: (i, j)),
        out_shape=jax.ShapeDtypeStruct((t, n), out_dtype),
        compiler_params=_cparams(("arbitrary", "arbitrary")),
        name="in_proj",
    )(x, w, bias)


def _attn_prep_kernel(a_ref, qw_ref, kw_ref, kn_ref, qt_ref, vt_ref):
    hw = DA_HEADS * 2 * DA_DK
    a = a_ref[...]
    r = lax.broadcasted_iota(jnp.int32, (hw, hw), 0) // DA_DK
    c = lax.broadcasted_iota(jnp.int32, (hw, hw), 1) // DA_DK
    group = jnp.where(r == c, 1.0, 0.0).astype(BF16)

    def qk_norm(z, w):
        ms = _dot((z * z).astype(BF16), group) * (1.0 / DA_DK)
        return z * lax.rsqrt(ms + NORM_EPS) * w

    qn = qk_norm(a[:, :hw].astype(F32), qw_ref[...])
    kn = qk_norm(a[:, hw:2 * hw].astype(F32), kw_ref[...])
    kn_ref[...] = kn.astype(BF16)
    qt_ref[0] = qn.T.astype(BF16)
    vt_ref[0] = a[:, 2 * hw:].astype(F32).T.astype(BF16)


def _attn_prep_call(a, qw, kw, batch, seq, ts):
    t = a.shape[0]
    hw = DA_HEADS * 2 * DA_DK
    hv = DA_HEADS * DA_DV
    per_b = seq // ts
    return pl.pallas_call(
        _attn_prep_kernel,
        grid=(t // ts,),
        in_specs=[pl.BlockSpec((ts, 2 * hw + hv), lambda i: (i, 0)),
                  pl.BlockSpec((1, hw), lambda i: (0, 0)),
                  pl.BlockSpec((1, hw), lambda i: (0, 0))],
        out_specs=[pl.BlockSpec((ts, hw), lambda i: (i, 0)),
                   pl.BlockSpec((1, hw, ts), lambda i: (i // per_b, 0, i % per_b)),
                   pl.BlockSpec((1, hv, ts), lambda i: (i // per_b, 0, i % per_b))],
        out_shape=[jax.ShapeDtypeStruct((t, hw), BF16),
                   jax.ShapeDtypeStruct((batch, hw, seq), BF16),
                   jax.ShapeDtypeStruct((batch, hv, seq), BF16)],
        compiler_params=_cparams(("arbitrary",)),
        name="attn_prep",
    )(a, qw, kw)


def _attn_slopes():
    return [LOG2E * 2.0 ** (-8.0 * (h + 1) / DA_HEADS) for h in range(DA_HEADS)]


def _attn_window(group, tq):
    hp = ATT_HEADS_PER_STEP
    blocks = [int((ATT_ZERO_EXP / s - 1.0) // tq) + 1 for s in _attn_slopes()[group * hp:(group + 1) * hp]]
    return max(blocks)


def _attn_kernel(sc_ref, qt_ref, k_ref, vt_ref, sw_ref, o_ref,
                 qz_ref, m_ref, acc_ref, vs_ref, kmax_ref, s_ref, s2_ref, *, tq, seq):
    hp = ATT_HEADS_PER_STEP
    dv = DA_DV
    g = pl.program_id(1)
    i = pl.program_id(2)
    lam = sc_ref[DA_HEADS]
    out_scale = sc_ref[DA_HEADS + 1]
    slopes = [sc_ref[hp * g + hh] for hh in range(hp)]
    row16 = lax.broadcasted_iota(jnp.int32, (ATT_PAD_ROWS, tq), 0)
    ones_rows = jnp.where(row16 == 0, 1.0, 0.0).astype(BF16)

    half = lax.broadcasted_iota(jnp.int32, (1, 2 * DA_DK), 1) < DA_DK

    @pl.when(i == 0)
    def _():
        pos = lax.broadcasted_iota(jnp.int32, (1, seq), 1) & (tq - 1)
        rel = (pos - (tq - 1)).astype(F32)
        sub16 = lax.broadcasted_iota(jnp.int32, (ATT_PAD_ROWS, seq), 0)
        gr = lax.broadcasted_iota(jnp.int32, (2 * DA_DK, 2 * DA_DK), 0) // DA_DK
        gc = lax.broadcasted_iota(jnp.int32, (2 * DA_DK, 2 * DA_DK), 1) // DA_DK
        group = jnp.where(gr == gc, 1.0, 0.0).astype(BF16)
        for hh in range(hp):
            w = jnp.exp2(slopes[hh] * rel)
            vs_ref[hh, 0:dv, :] = (vt_ref[0, hh * dv:(hh + 1) * dv, :].astype(F32) * w).astype(BF16)
            vs_ref[hh, dv:dv + ATT_PAD_ROWS, :] = jnp.where(sub16 == 0, w, 0.0).astype(BF16)

            def knorm(n, best):
                kc = k_ref[pl.ds(pl.multiple_of(n * tq, tq), tq),
                           hh * 2 * DA_DK:(hh + 1) * 2 * DA_DK].astype(F32)
                return jnp.maximum(best, jnp.max(_dot((kc * kc).astype(BF16), group), axis=0, keepdims=True))

            k2 = lax.fori_loop(0, seq // tq, knorm, jnp.zeros((1, 2 * DA_DK), F32))
            kmax_ref[2 * hh] = jnp.max(jnp.where(half, k2, 0.0), axis=1, keepdims=True)
            kmax_ref[2 * hh + 1] = jnp.max(jnp.where(half, 0.0, k2), axis=1, keepdims=True)

    row = lax.broadcasted_iota(jnp.int32, (2 * DA_DK, tq), 0)
    qq_row = lax.broadcasted_iota(jnp.int32, (1, tq), 1).astype(F32)
    bound_max = jnp.zeros((), F32)
    for hh in range(hp):
        qt = qt_ref[0, hh * 2 * DA_DK:(hh + 1) * 2 * DA_DK, :]
        zero = jnp.zeros_like(qt)
        q1 = jnp.where(row < DA_DK, qt, zero)
        q2 = jnp.where(row >= DA_DK, qt, zero)
        qz_ref[hh] = jnp.concatenate([q1, q2], axis=1)
        qn = jnp.concatenate(
            [jnp.sum(jnp.square(q1.astype(F32)), axis=0, keepdims=True) * kmax_ref[2 * hh],
             jnp.sum(jnp.square(q2.astype(F32)), axis=0, keepdims=True) * kmax_ref[2 * hh + 1]], axis=1)
        bound = jnp.sqrt(qn) * ATT_BOUND_MARGIN + ATT_BOUND_EPS
        bound_max = jnp.maximum(bound_max, jnp.max(bound))
        m_ref[hh] = bound + slopes[hh] * jnp.concatenate([qq_row, qq_row], axis=1)
    acc_ref[...] = jnp.zeros(acc_ref.shape, F32)

    kk = lax.broadcasted_iota(jnp.int32, (tq, 2 * tq), 0)
    cc = lax.broadcasted_iota(jnp.int32, (tq, 2 * tq), 1)
    qq = jnp.where(cc >= tq, cc - tq, cc)
    dist = (qq - jnp.abs(qq - kk)).astype(F32)
    visible = (kk // CHUNK) <= (qq // CHUNK)

    def scores(hh, blk):
        start = pl.multiple_of(blk * tq, tq)
        return _dot(k_ref[pl.ds(start, tq), hh * 2 * DA_DK:(hh + 1) * 2 * DA_DK], qz_ref[hh])

    def diag_values(hh):
        start = pl.multiple_of(i * tq, tq)
        return jnp.concatenate([vt_ref[0, hh * dv:(hh + 1) * dv, pl.ds(start, tq)], ones_rows], axis=0)

    @pl.when(bound_max <= ATT_BOUND_MAX)
    def _():
        window = jnp.int32(_attn_window(0, tq))
        for grp in range(1, DA_HEADS // hp):
            window = jnp.where(g == grp, _attn_window(grp, tq), window)
        first = (jnp.maximum(i - window, 0) // 2) * 2
        for hh in range(hp):
            s_ref[hh] = scores(hh, first)
            s2_ref[hh] = scores(hh, jnp.minimum(first + 1, i))

        def probs(hh, blk, s):
            last_key = ((blk + 1 - i) * tq - 1).astype(F32)
            return jnp.exp2(s - (m_ref[hh] - slopes[hh] * last_key)).astype(BF16)

        def body(jj, carry):
            blk = 2 * jj
            start = pl.multiple_of(blk * tq, 2 * tq)
            for hh in range(hp):
                p = jnp.concatenate([probs(hh, blk, s_ref[hh]), probs(hh, blk + 1, s2_ref[hh])], axis=0)
                s_ref[hh] = scores(hh, jnp.minimum(blk + 2, i))
                s2_ref[hh] = scores(hh, jnp.minimum(blk + 3, i))
                acc_ref[hh] += _dot(vs_ref[hh, :, pl.ds(start, 2 * tq)], p)
            return carry

        lax.fori_loop(first // 2, i // 2, body, 0)

        @pl.when(i % 2 == 1)
        def _():
            start = pl.multiple_of((i - 1) * tq, tq)
            for hh in range(hp):
                acc_ref[hh] += _dot(vs_ref[hh, :, pl.ds(start, tq)], probs(hh, i - 1, s_ref[hh]))
                s_ref[hh] = s2_ref[hh]

        for hh in range(hp):
            s = jnp.where(visible, s_ref[hh] + slopes[hh] * dist, -jnp.inf)
            acc_ref[hh] += _dot(diag_values(hh), jnp.exp2(s - m_ref[hh]).astype(BF16))

    @pl.when(bound_max > ATT_BOUND_MAX)
    def _():
        m_ref[...] = jnp.full(m_ref.shape, -jnp.inf, F32)

        def update(hh, s, bound_shift, v_aug):
            m_old = m_ref[hh]
            m_new = jnp.maximum(m_old, jnp.max(s, axis=0, keepdims=True) + bound_shift)
            p = jnp.exp2(s - (m_new - bound_shift))
            acc_ref[hh] = jnp.exp2(m_old - m_new) * acc_ref[hh] + _dot(v_aug, p.astype(BF16))
            m_ref[hh] = m_new

        def body(j, carry):
            start = pl.multiple_of(j * tq, tq)
            last_key = ((j + 1 - i) * tq - 1).astype(F32)
            for hh in range(hp):
                update(hh, scores(hh, j), slopes[hh] * last_key, vs_ref[hh, :, pl.ds(start, tq)])
            return carry

        lax.fori_loop(0, i, body, 0)
        for hh in range(hp):
            s = jnp.where(visible, scores(hh, i) + slopes[hh] * dist, -jnp.inf)
            update(hh, s, jnp.zeros((), F32), diag_values(hh))

    outs = []
    for hh in range(hp):
        acc = acc_ref[hh]
        o2 = acc[:dv, :] * (1.0 / acc[dv:dv + 1, :])
        o = (o2[:, :tq] - lam * o2[:, tq:]).T
        o = o * lax.rsqrt(jnp.mean(o * o, axis=-1, keepdims=True) + NORM_EPS) * sw_ref[...]
        outs.append(o * out_scale)
    o_ref[...] = jnp.concatenate(outs, axis=1).astype(o_ref.dtype)


def _attn_call(scal, qt, kn, vt, sw, batch, seq, tq):
    t = kn.shape[0]
    nq = seq // tq
    hp = ATT_HEADS_PER_STEP
    return pl.pallas_call(
        functools.partial(_attn_kernel, tq=tq, seq=seq),
        grid=(batch, DA_HEADS // hp, nq),
        in_specs=[pl.BlockSpec(memory_space=pltpu.SMEM),
                  pl.BlockSpec((1, hp * 2 * DA_DK, tq), lambda b, g, i: (b, g, i)),
                  pl.BlockSpec((seq, hp * 2 * DA_DK), lambda b, g, i: (b, g)),
                  pl.BlockSpec((1, hp * DA_DV, seq), lambda b, g, i: (b, g, 0)),
                  pl.BlockSpec((1, DA_DV), lambda b, g, i: (0, 0))],
        out_specs=pl.BlockSpec((tq, hp * DA_DV), lambda b, g, i: (b * nq + i, g)),
        out_shape=jax.ShapeDtypeStruct((t, DA_HEADS * DA_DV), BF16),
        scratch_shapes=[pltpu.VMEM((hp, 2 * DA_DK, 2 * tq), BF16),
                        pltpu.VMEM((hp, 1, 2 * tq), F32),
                        pltpu.VMEM((hp, DA_DV + ATT_PAD_ROWS, 2 * tq), F32),
                        pltpu.VMEM((hp, DA_DV + ATT_PAD_ROWS, seq), BF16),
                        pltpu.VMEM((2 * hp, 1, 1), F32),
                        pltpu.VMEM((hp, tq, 2 * tq), F32),
                        pltpu.VMEM((hp, tq, 2 * tq), F32)],
        compiler_params=_cparams(("arbitrary", "arbitrary", "arbitrary")),
        name="diff_attn",
    )(scal, qt, kn, vt, sw)


def _hgrn_kernel(q_ref, f_ref, i_ref, g_ref, lb_ref, gw_ref, o_ref,
                 st_ref, b_ref, qs_ref, ks_ref, oi_ref, *, n_chunks):
    c = HG_CHUNK
    sub = HG_SUB
    d = HG_D
    nh = HG_HEADS

    @pl.when(pl.program_id(1) == 0)
    def _():
        st_ref[...] = jnp.zeros(st_ref.shape, F32)

    gw = gw_ref[...]
    rr = lax.broadcasted_iota(jnp.int32, (c, c), 0)
    cc = lax.broadcasted_iota(jnp.int32, (c, c), 1)
    causal = cc <= rr
    tril = jnp.where(causal, 1.0, 0.0).astype(BF16)

    def chunk(n, carry):
        r0 = pl.multiple_of(n * c, c)
        heads = []
        decay = jnp.zeros((), F32)
        for hh in range(nh):
            cols = slice(hh * d, (hh + 1) * d)
            log_lb = lb_ref[hh, 0:1, :]
            q = q_ref[pl.ds(r0, c), cols]
            q = q * jax.nn.sigmoid(q)
            a = lb_ref[hh, 1:2, :] + _log_sigmoid(f_ref[pl.ds(r0, c), cols])
            logf = jnp.maximum(log_lb, a) + jnp.log1p(jnp.exp(-jnp.abs(log_lb - a)))
            k = 1.0 - jnp.exp(logf)
            v = i_ref[pl.ds(r0, c), cols]
            hi, mid, lo = _split3(logf)
            b = _dot(tril, hi) + _dot(tril, mid) + _dot(tril, lo)
            b_last = b[c - 1:c, :]
            st = st_ref[hh]
            o_inter = _dot_nt((q * jnp.exp(b)).astype(BF16), st.astype(BF16))
            k_hat = k * jnp.exp(b_last - b)
            st_ref[hh] = st * jnp.exp(b_last) + _dot(v.T.astype(BF16), k_hat.astype(BF16))
            betas = []
            for blk in range(c // sub):
                beta = jnp.zeros((1, d), F32) if blk == 0 else b[blk * sub - 1:blk * sub, :]
                betas.append(beta)
                b_end = b[(blk + 1) * sub - 1:(blk + 1) * sub, :]
                decay = jnp.maximum(decay, jnp.max(beta - b_end))
            heads.append((q, k, v, b, betas, o_inter))

        @pl.when(decay < HG_SAFE_DECAY)
        def _():
            for hh, (q, k, v, b, betas, _) in enumerate(heads):
                rows = []
                for blk in range(c // sub):
                    beta = betas[blk]
                    q_t = q[blk * sub:(blk + 1) * sub, :] * jnp.exp(b[blk * sub:(blk + 1) * sub, :] - beta)
                    k_t = k * jnp.exp(jnp.minimum(beta - b, HG_SAFE_DECAY))
                    rows.append(_dot_nt(q_t.astype(BF16), k_t.astype(BF16)))
                att = jnp.where(causal, jnp.concatenate(rows, axis=0), 0.0)
                oi_ref[hh] = _dot(att.astype(BF16), v.astype(BF16))

        @pl.when(decay >= HG_SAFE_DECAY)
        def _():
            ridx = lax.broadcasted_iota(jnp.int32, (c, d), 0)
            for hh, (q, k, v, b, _, _) in enumerate(heads):
                b_ref[...] = b
                qs_ref[...] = q
                ks_ref[...] = k

                def row(t, carry2):
                    bt = b_ref[pl.ds(t, 1), :]
                    e = jnp.exp(jnp.where(ridx <= t, bt - b_ref[...], -jnp.inf))
                    w = jnp.sum(qs_ref[pl.ds(t, 1), :] * ks_ref[...] * e, axis=1, keepdims=True)
                    oi_ref[hh, pl.ds(t, 1), :] = jnp.sum(w * v, axis=0, keepdims=True)
                    return carry2

                lax.fori_loop(0, c, row, 0)

        for hh in range(nh):
            cols = slice(hh * d, (hh + 1) * d)
            o = heads[hh][5] + oi_ref[hh]
            o = o * lax.rsqrt(jnp.mean(o * o, axis=-1, keepdims=True) + NORM_EPS) * gw
            g = g_ref[pl.ds(r0, c), cols]
            o_ref[pl.ds(r0, c), cols] = (o * (g * jax.nn.sigmoid(g))).astype(o_ref.dtype)
        return carry

    lax.fori_loop(0, n_chunks, chunk, 0)


def _hgrn_call(bproj, lbs, gw, batch, seq, ts):
    t = bproj.shape[0]
    d = HG_D
    nh = HG_HEADS
    ns = seq // ts
    spec = lambda off: pl.BlockSpec((ts, nh * d), lambda b, i: (b * ns + i, off))
    return pl.pallas_call(
        functools.partial(_hgrn_kernel, n_chunks=ts // HG_CHUNK),
        grid=(batch, ns),
        in_specs=[spec(0), spec(1), spec(2), spec(3),
                  pl.BlockSpec((nh, 2, d), lambda b, i: (0, 0, 0)),
                  pl.BlockSpec((1, d), lambda b, i: (0, 0))],
        out_specs=pl.BlockSpec((ts, nh * d), lambda b, i: (b * ns + i, 0)),
        out_shape=jax.ShapeDtypeStruct((t, nh * d), BF16),
        scratch_shapes=[pltpu.VMEM((nh, d, d), F32),
                        pltpu.VMEM((HG_CHUNK, d), F32),
                        pltpu.VMEM((HG_CHUNK, d), F32),
                        pltpu.VMEM((HG_CHUNK, d), F32),
                        pltpu.VMEM((nh, HG_CHUNK, d), F32)],
        compiler_params=_cparams(("arbitrary", "arbitrary")),
        name="hgrn2",
    )(bproj, bproj, bproj, bproj, lbs, gw)


def _mlstm_kernel(uq_ref, uk_ref, v_ref, op_ref, gi_ref, gf_ref, cwq_ref, cwk_ref,
                  cbq_ref, cbk_ref, nw_ref, o_ref,
                  xq_ref, xk_ref, c_ref, m_ref, bs_ref):
    L = ML_CHUNK
    d = ML_D
    nh = ML_HEADS
    i = pl.program_id(1)
    rr = lax.broadcasted_iota(jnp.int32, (L, L), 0)
    cc = lax.broadcasted_iota(jnp.int32, (L, L), 1)

    @pl.when(i == 0)
    def _():
        c_ref[...] = jnp.zeros(c_ref.shape, F32)
        m_ref[...] = jnp.zeros(m_ref.shape, F32)
        xq_ref[0:8, :] = jnp.zeros((8, nh * d), F32)
        xk_ref[0:8, :] = jnp.zeros((8, nh * d), F32)
        upper = jnp.where(rr <= cc, 1.0, 0.0).astype(BF16)
        for hh in range(nh):
            hi, mid, lo = _split3(_log_sigmoid(gf_ref[0, hh]))
            bs_ref[hh] = _dot(hi, upper) + _dot(mid, upper) + _dot(lo, upper)

    def conv_silu(u_ref, x_ref, w_ref, b_ref):
        x_ref[8:8 + L, :] = u_ref[...]
        y = b_ref[...] + w_ref[ML_CONV - 1:ML_CONV, :] * x_ref[8:8 + L, :]
        for j in range(ML_CONV - 1):
            y = y + w_ref[j:j + 1, :] * x_ref[5 + j:5 + j + L, :]
        x_ref[0:8, :] = x_ref[L:L + 8, :]
        return y * jax.nn.sigmoid(y)

    q_all = conv_silu(uq_ref, xq_ref, cwq_ref, cbq_ref)
    k_all = conv_silu(uk_ref, xk_ref, cwk_ref, cbk_ref) * (d ** -0.5)
    lane = lax.broadcasted_iota(jnp.int32, (L, d), 1)
    ones_col = jnp.where(lane == 0, 1.0, 0.0)
    for hh in range(nh):
        cols = slice(hh * d, (hh + 1) * d)
        q = q_all[:, cols].astype(BF16)
        kt = k_all[:, cols].T
        v_aug = jnp.concatenate([v_ref[:, cols], ones_col], axis=1).astype(BF16)

        b_row = bs_ref[hh, pl.ds(i, 1), :]
        ig_row = gi_ref[0, hh, pl.ds(i, 1), :]
        m_prev = m_ref[hh]
        g = b_row[:, L - 1:L]
        b_col = jnp.sum(jnp.where(rr == cc, b_row, 0.0), axis=1, keepdims=True)
        log_d = jnp.where(cc <= rr, b_col + (ig_row - b_row), -jnp.inf)
        log_inter = b_col + m_prev
        m_t = jnp.maximum(log_inter, jnp.max(log_d, axis=1, keepdims=True))
        w_intra = jnp.exp(log_d - m_t) * _dot(q, kt.astype(BF16))
        w_inter = jnp.exp(log_inter - m_t)
        c_aug = c_ref[hh]
        tot = w_inter * _dot(q, c_aug.astype(BF16)) + _dot(w_intra.astype(BF16), v_aug)
        denom = jnp.maximum(jnp.abs(tot[:, d:d + 1]), jnp.exp(-m_t))
        hout = tot[:, :d] / denom
        hout = hout * lax.rsqrt(jnp.mean(hout * hout, axis=-1, keepdims=True) + NORM_EPS) * nw_ref[...]
        o_ref[:, cols] = (hout * jax.nn.sigmoid(op_ref[:, cols])).astype(o_ref.dtype)

        log_w = g - b_row + ig_row
        m_new = jnp.maximum(g + m_prev, jnp.max(log_w, axis=1, keepdims=True))
        w_s = jnp.exp(log_w - m_new)
        c_ref[hh] = jnp.exp(g + m_prev - m_new) * c_aug + _dot((kt * w_s).astype(BF16), v_aug)
        m_ref[hh] = m_new


def _mlstm_call(cproj, gi, gf, conv_w, conv_b, nw, batch, seq):
    t = cproj.shape[0]
    d = ML_D
    nh = ML_HEADS
    L = ML_CHUNK
    nc = seq // L
    spec = lambda off: pl.BlockSpec((L, nh * d), lambda b, i: (b * nc + i, off))
    gspec = pl.BlockSpec((1, nh, nc, L), lambda b, i: (b, 0, 0, 0))
    return pl.pallas_call(
        _mlstm_kernel,
        grid=(batch, nc),
        in_specs=[spec(0), spec(1), spec(2), spec(3), gspec, gspec,
                  pl.BlockSpec((ML_CONV, nh * d), lambda b, i: (0, 0)),
                  pl.BlockSpec((ML_CONV, nh * d), lambda b, i: (0, 1)),
                  pl.BlockSpec((1, nh * d), lambda b, i: (0, 0)),
                  pl.BlockSpec((1, nh * d), lambda b, i: (0, 1)),
                  pl.BlockSpec((1, d), lambda b, i: (0, 0))],
        out_specs=pl.BlockSpec((L, nh * d), lambda b, i: (b * nc + i, 0)),
        out_shape=jax.ShapeDtypeStruct((t, nh * d), BF16),
        scratch_shapes=[pltpu.VMEM((L + 8, nh * d), F32),
                        pltpu.VMEM((L + 8, nh * d), F32),
                        pltpu.VMEM((nh, d, 2 * d), F32),
                        pltpu.VMEM((nh, 1, 1), F32),
                        pltpu.VMEM((nh, nc, L), F32)],
        compiler_params=_cparams(("arbitrary", "arbitrary")),
        name="mlstm",
    )(cproj, cproj, cproj, cproj, gi, gf, conv_w, conv_w, conv_b, conv_b, nw)


def _top2_combine(logits):
    lane = lax.broadcasted_iota(jnp.int32, logits.shape, 1)
    lg = jnp.where(lane < N_EXPERTS, logits, -jnp.inf)
    ex = jnp.exp(lg - jnp.max(lg, axis=1, keepdims=True))
    probs = ex / jnp.sum(ex, axis=1, keepdims=True)
    p1 = jnp.max(probs, axis=1, keepdims=True)
    i1 = jnp.min(jnp.where(probs == p1, lane, LANES), axis=1, keepdims=True)
    rest = jnp.where(lane == i1, -1.0, probs)
    p2 = jnp.max(rest, axis=1, keepdims=True)
    i2 = jnp.min(jnp.where(rest == p2, lane, LANES), axis=1, keepdims=True)
    comb = jnp.where(lane == i1, p1, 0.0) + jnp.where(lane == i2, p2, 0.0)
    return comb / (p1 + p2)


def _merge_kernel(*refs, route):
    if route:
        (ya_ref, yb_ref, yc_ref, gp_ref, wb_ref, wo_ref, x_ref, mod_ref, nw_ref,
         rw_ref, rb_ref, xo_ref, h_ref, cmb_ref) = refs
    else:
        (ya_ref, yb_ref, yc_ref, gp_ref, wb_ref, wo_ref, x_ref, mod_ref, nw_ref,
         xo_ref, h_ref) = refs
    d = x_ref.shape[1]
    merged = None
    for n, y_ref in enumerate((ya_ref, yb_ref, yc_ref)):
        gate = jax.nn.sigmoid(gp_ref[:, n * d:(n + 1) * d].astype(F32))
        term = gate * _dot(y_ref[...], wb_ref[n])
        merged = term if merged is None else merged + term
    m = mod_ref[0]
    xn = x_ref[...] + m[2:3] * _dot(merged.astype(BF16), wo_ref[...])
    xo_ref[...] = xn
    h2 = _norm_mod(xn, nw_ref[...], m[4:5], m[3:4])
    h_ref[...] = h2.astype(BF16)
    if route:
        h_hi, h_mid, _ = _split3(h2)
        r_hi, r_mid, _ = _split3(rw_ref[...])
        logits = _dot(h_hi, r_hi) + _dot(h_mid, r_hi) + _dot(h_hi, r_mid) + rb_ref[...]
        cmb_ref[...] = _top2_combine(logits)


def _merge_call(ya, yb, yc, gp, wb, wo, x, mod, nw, router, seq, tm):
    t, d = x.shape
    bw = ya.shape[1]
    per_b = seq // tm
    route = router is not None
    tok = lambda w: pl.BlockSpec((tm, w), lambda i: (i, 0))
    const2 = lambda s: pl.BlockSpec(s, lambda i: (0, 0))
    in_specs = [tok(bw), tok(bw), tok(bw), tok(3 * d),
                pl.BlockSpec((3, bw, d), lambda i: (0, 0, 0)), const2((d, d)), tok(d),
                pl.BlockSpec((1, 6, d), lambda i: (i // per_b, 0, 0)), const2((1, d))]
    out_specs = [tok(d), tok(d)]
    out_shape = [jax.ShapeDtypeStruct((t, d), F32), jax.ShapeDtypeStruct((t, d), BF16)]
    args = [ya, yb, yc, gp, wb, wo, x, mod, nw]
    if route:
        in_specs += [const2((d, LANES)), const2((1, LANES))]
        out_specs.append(tok(LANES))
        out_shape.append(jax.ShapeDtypeStruct((t, LANES), F32))
        args += list(router)
    return pl.pallas_call(
        functools.partial(_merge_kernel, route=route),
        grid=(t // tm,),
        in_specs=in_specs, out_specs=out_specs, out_shape=out_shape,
        compiler_params=_cparams(("arbitrary",)),
        name="merge_out",
    )(*args)


def _finish(acc, x_ref, mod_ref, xo_ref, nxt):
    xn = x_ref[...] + mod_ref[0][5:6] * acc
    xo_ref[...] = xn
    if nxt is not None:
        nw_ref, modn_ref, hn_ref = nxt
        mn = modn_ref[0]
        hn_ref[...] = _norm_mod(xn, nw_ref[...], mn[1:2], mn[0:1]).astype(BF16)


def _ffn_kernel(*refs, has_next):
    if has_next:
        h_ref, w1_ref, w3_ref, w2_ref, x_ref, mod_ref, nw_ref, modn_ref, xo_ref, hn_ref, acc_ref = refs
        nxt = (nw_ref, modn_ref, hn_ref)
    else:
        h_ref, w1_ref, w3_ref, w2_ref, x_ref, mod_ref, xo_ref, acc_ref = refs
        nxt = None
    f = pl.program_id(1)

    @pl.when(f == 0)
    def _():
        acc_ref[...] = jnp.zeros(acc_ref.shape, F32)

    h = h_ref[...]
    a = _dot(h, w1_ref[...])
    act = a * jax.nn.sigmoid(a) * _dot(h, w3_ref[...])
    acc_ref[...] += _dot(act.astype(BF16), w2_ref[...])

    @pl.when(f == pl.num_programs(1) - 1)
    def _():
        _finish(acc_ref[...], x_ref, mod_ref, xo_ref, nxt)


def _ffn_call(h, w1, w3, w2, x, mod, nxt, seq, tm, tf):
    t, d = x.shape
    ff = w1.shape[1]
    per_b = seq // tm
    tok = lambda: pl.BlockSpec((tm, d), lambda i, f: (i, 0))
    modspec = lambda: pl.BlockSpec((1, 6, d), lambda i, f: (i // per_b, 0, 0))
    in_specs = [tok(), pl.BlockSpec((d, tf), lambda i, f: (0, f)),
                pl.BlockSpec((d, tf), lambda i, f: (0, f)),
                pl.BlockSpec((tf, d), lambda i, f: (f, 0)), tok(), modspec()]
    out_specs = [tok()]
    out_shape = [jax.ShapeDtypeStruct((t, d), F32)]
    args = [h, w1, w3, w2, x, mod]
    if nxt is not None:
        in_specs += [pl.BlockSpec((1, d), lambda i, f: (0, 0)), modspec()]
        out_specs.append(tok())
        out_shape.append(jax.ShapeDtypeStruct((t, d), BF16))
        args += list(nxt)
    return pl.pallas_call(
        functools.partial(_ffn_kernel, has_next=nxt is not None),
        grid=(t // tm, ff // tf),
        in_specs=in_specs, out_specs=out_specs, out_shape=out_shape,
        scratch_shapes=[pltpu.VMEM((tm, d), F32)],
        compiler_params=_cparams(("arbitrary", "arbitrary")),
        name="ffn_swiglu",
    )(*args)


def _moe_kernel(h_ref, cmb_ref, w1_ref, w3_ref, w2_ref, y_ref,
                acc_ref, xg_ref, ya_ref, rk_ref, rkt_ref, cnt_ref):
    e = pl.program_id(1)
    f = pl.program_id(2)
    tm = h_ref.shape[0]
    cap = MOE_CAP
    cap_pad = -(-cap // LANES) * LANES

    @pl.when((e == 0) & (f == 0))
    def _():
        r = lax.broadcasted_iota(jnp.int32, (tm, tm), 0)
        c = lax.broadcasted_iota(jnp.int32, (tm, tm), 1)
        before = jnp.where(c < r, 1.0, 0.0).astype(BF16)
        sel = cmb_ref[...] > 0.0
        rank = _dot(before, jnp.where(sel, 1.0, 0.0).astype(BF16))
        rk = jnp.where(sel, rank, -1.0)
        rk_ref[...] = rk
        rkt_ref[...] = rk.T
        cnt_ref[...] = jnp.sum(jnp.where(sel, 1.0, 0.0), axis=0, keepdims=True)
        acc_ref[...] = jnp.zeros(acc_ref.shape, F32)

    lane1 = lax.broadcasted_iota(jnp.int32, (1, LANES), 1)
    n_e = jnp.sum(jnp.where(lane1 == e, cnt_ref[...], 0.0)).astype(jnp.int32)
    n_blocks = (n_e + cap - 1) // cap

    @pl.when(f == 0)
    def _():
        rank_row = rkt_ref[pl.ds(e, 1), :]

        def gather(sb, carry):
            r0 = pl.multiple_of(sb * cap, cap)
            slot = (sb * cap + lax.broadcasted_iota(jnp.int32, (cap, tm), 0)).astype(F32)
            onehot = jnp.where(rank_row == slot, 1.0, 0.0).astype(BF16)
            xg_ref[pl.ds(r0, cap), :] = _dot(onehot, h_ref[...]).astype(BF16)
            return carry

        lax.fori_loop(0, n_blocks, gather, 0)

    def expert(sb, carry):
        r0 = pl.multiple_of(sb * cap, cap)
        xs = xg_ref[pl.ds(r0, cap), :]
        a = _dot(xs, w1_ref[0])
        act = a * jax.nn.sigmoid(a) * _dot(xs, w3_ref[0])
        part = _dot(act.astype(BF16), w2_ref[0])

        @pl.when(f == 0)
        def _():
            ya_ref[pl.ds(r0, cap), :] = part

        @pl.when(f > 0)
        def _():
            ya_ref[pl.ds(r0, cap), :] += part

        return carry

    lax.fori_loop(0, n_blocks, expert, 0)

    @pl.when(f == pl.num_programs(2) - 1)
    def _():
        lane = lax.broadcasted_iota(jnp.int32, (tm, LANES), 1)
        rank_col = jnp.sum(jnp.where(lane == e, rk_ref[...], 0.0), axis=1, keepdims=True)
        w_col = jnp.sum(jnp.where(lane == e, cmb_ref[...], 0.0), axis=1, keepdims=True)
        col = lax.broadcasted_iota(jnp.int32, (tm, cap_pad), 1)

        def scatter(sb, carry):
            r0 = pl.multiple_of(sb * cap, cap)
            ys = ya_ref[pl.ds(r0, cap), :].astype(BF16)
            ys = jnp.concatenate([ys, jnp.zeros((cap_pad - cap, ys.shape[1]), BF16)], axis=0)
            hit = (rank_col == (sb * cap + col).astype(F32)) & (col < cap)
            acc_ref[...] += w_col * _dot(jnp.where(hit, 1.0, 0.0).astype(BF16), ys)
            return carry

        lax.fori_loop(0, n_blocks, scatter, 0)

    @pl.when((e == pl.num_programs(1) - 1) & (f == pl.num_programs(2) - 1))
    def _():
        y_ref[...] = acc_ref[...].astype(y_ref.dtype)


def _moe_call(h, cmb, w1, w3, w2, tm, tf):
    t, d = h.shape
    ne, _, ff = w1.shape
    rows = -(-tm // MOE_CAP) * MOE_CAP
    return pl.pallas_call(
        _moe_kernel,
        grid=(t // tm, ne, ff // tf),
        in_specs=[pl.BlockSpec((tm, d), lambda i, e, f: (i, 0)),
                  pl.BlockSpec((tm, LANES), lambda i, e, f: (i, 0)),
                  pl.BlockSpec((1, d, tf), lambda i, e, f: (e, 0, f)),
                  pl.BlockSpec((1, d, tf), lambda i, e, f: (e, 0, f)),
                  pl.BlockSpec((1, tf, d), lambda i, e, f: (e, f, 0))],
        out_specs=pl.BlockSpec((tm, d), lambda i, e, f: (i, 0)),
        out_shape=jax.ShapeDtypeStruct((t, d), BF16),
        scratch_shapes=[pltpu.VMEM((tm, d), F32),
                        pltpu.VMEM((rows, d), BF16),
                        pltpu.VMEM((rows, d), F32),
                        pltpu.VMEM((tm, LANES), F32),
                        pltpu.VMEM((LANES, tm), F32),
                        pltpu.VMEM((1, LANES), F32)],
        compiler_params=_cparams(("arbitrary", "arbitrary", "arbitrary")),
        name="moe_top2",
    )(h, cmb, w1, w3, w2)


def _resid_kernel(*refs, has_next):
    if has_next:
        y_ref, x_ref, mod_ref, nw_ref, modn_ref, xo_ref, hn_ref = refs
        nxt = (nw_ref, modn_ref, hn_ref)
    else:
        y_ref, x_ref, mod_ref, xo_ref = refs
        nxt = None
    _finish(y_ref[...].astype(F32), x_ref, mod_ref, xo_ref, nxt)


def _resid_call(y, x, mod, nxt, seq, tm):
    t, d = x.shape
    per_b = seq // tm
    tok = lambda: pl.BlockSpec((tm, d), lambda i: (i, 0))
    modspec = lambda: pl.BlockSpec((1, 6, d), lambda i: (i // per_b, 0, 0))
    in_specs = [tok(), tok(), modspec()]
    out_specs = [tok()]
    out_shape = [jax.ShapeDtypeStruct((t, d), F32)]
    args = [y, x, mod]
    if nxt is not None:
        in_specs += [pl.BlockSpec((1, d), lambda i: (0, 0)), modspec()]
        out_specs.append(tok())
        out_shape.append(jax.ShapeDtypeStruct((t, d), BF16))
        args += list(nxt)
    return pl.pallas_call(
        functools.partial(_resid_kernel, has_next=nxt is not None),
        grid=(t // tm,),
        in_specs=in_specs, out_specs=out_specs, out_shape=out_shape,
        compiler_params=_cparams(("arbitrary",)),
        name="moe_residual",
    )(*args)


def _attn_col_perm():
    hq = DA_HEADS * DA_DK
    idx = []
    for base in (0, 2 * hq):
        for h in range(DA_HEADS):
            idx += list(range(base + h * DA_DK, base + (h + 1) * DA_DK))
            idx += list(range(base + hq + h * DA_DK, base + hq + (h + 1) * DA_DK))
    idx += list(range(4 * hq, 4 * hq + DA_HEADS * DA_DV))
    return np.asarray(idx, np.int32)


def _pick(n, pref):
    for c in pref:
        if n % c == 0:
            return c
    return n


def kernel(x, c, w_mod, b_mod, norm1_w, norm2_w, w_in, c_conv_w, c_conv_b, a_qnorm_w, a_knorm_w,
           a_lambda_q1, a_lambda_k1, a_lambda_q2, a_lambda_k2, a_subln_w, b_lb_logits, b_gnorm_w,
           c_igate_b, c_fgate_b, c_norm_w, w_branch, w_out, ffn_w1, ffn_w3, ffn_w2,
           moe_router_w, moe_router_b, moe_w1, moe_w3, moe_w2):
    batch, seq, d = x.shape
    depth = w_in.shape[0]
    t = batch * seq
    n_a = 4 * DA_HEADS * DA_DK + DA_HEADS * DA_DV
    n_b = 4 * HG_HEADS * HG_D
    n_c = 4 * ML_HEADS * ML_D + 2 * ML_HEADS
    n_c_pad = -(-n_c // LANES) * LANES
    tm = _pick(seq, (1024, 512, 256))
    tq = _pick(seq, (ATT_TQ,))
    ts_h = _pick(seq, (1024, 512, 256, 128))

    c8 = jnp.zeros((8, d), F32).at[:batch].set(c)
    mod = _mod_call(c8, w_mod, b_mod.reshape(depth, 1, 6 * d))[:, :batch].reshape(depth, batch, 6, d)

    lb_all = jnp.cumsum(jax.nn.softmax(b_lb_logits.astype(F32), axis=0), axis=0)
    lb_all = lb_all - lb_all[:1]
    slopes = jnp.asarray(_attn_slopes(), F32)

    xf = x.reshape(t, d)
    h = _normmod_call(xf, norm1_w[0].reshape(1, d), mod[0], seq, tm)
    perm_a = _attn_col_perm()
    for l in range(depth):
        lam_init = 0.8 - 0.6 * math.exp(-0.3 * l)
        wl = w_in[l]
        w_a = wl[:, :n_a][:, perm_a].astype(BF16)
        w_b = wl[:, n_a:n_a + n_b].astype(BF16)
        w_c = jnp.pad(wl[:, n_a + n_b:n_a + n_b + n_c], ((0, 0), (0, n_c_pad - n_c))).astype(BF16)
        w_g = wl[:, n_a + n_b + n_c:].astype(BF16)
        bias_c = jnp.zeros((1, n_c_pad), F32)
        bias_c = bias_c.at[0, n_c - 2 * ML_HEADS:n_c - ML_HEADS].set(c_igate_b[l])
        bias_c = bias_c.at[0, n_c - ML_HEADS:n_c].set(c_fgate_b[l])
        pa = _matmul(h, w_a, jnp.zeros((1, n_a), F32), BF16, tm, 512)
        pb = _matmul(h, w_b, jnp.zeros((1, n_b), F32), F32, tm, 512)
        pc = _matmul(h, w_c, bias_c, F32, 512, n_c_pad)
        pg = _matmul(h, w_g, jnp.zeros((1, 3 * d), F32), BF16, tm, 1024)

        qw = (jnp.tile(a_qnorm_w[l], 2 * DA_HEADS) * (DA_DK ** -0.5 * LOG2E)).reshape(1, -1)
        kw = jnp.tile(a_knorm_w[l], 2 * DA_HEADS).reshape(1, -1)
        kn, qt, vt = _attn_prep_call(pa, qw, kw, batch, seq, _pick(seq, (512, 256)))
        lam = (jnp.exp(jnp.sum(a_lambda_q1[l] * a_lambda_k1[l]))
               - jnp.exp(jnp.sum(a_lambda_q2[l] * a_lambda_k2[l])) + lam_init)
        scal = jnp.concatenate([slopes, jnp.stack([lam, jnp.asarray(1.0 - lam_init, F32)])]).astype(F32)
        y_a = _attn_call(scal, qt, kn, vt, a_subln_w[l].reshape(1, -1), batch, seq, tq)

        lb = lb_all[l].reshape(HG_HEADS, 1, HG_D)
        lbs = jnp.concatenate([jnp.log(lb), jnp.log1p(-lb)], axis=1)
        y_b = _hgrn_call(pb, lbs, b_gnorm_w[l].reshape(1, -1), batch, seq, ts_h)

        gates = pc[:, n_c - 2 * ML_HEADS:n_c].reshape(batch, seq, 2, ML_HEADS)
        gates = gates.transpose(2, 0, 3, 1).reshape(2, batch, ML_HEADS, seq // ML_CHUNK, ML_CHUNK)
        y_c = _mlstm_call(pc, gates[0], gates[1], c_conv_w[l], c_conv_b[l].reshape(1, -1),
                          c_norm_w[l].reshape(1, -1), batch, seq)

        dense = l % 2 == 0
        router = None
        if not dense:
            rw = jnp.pad(moe_router_w[l // 2], ((0, 0), (0, LANES - N_EXPERTS)))
            rb = jnp.pad(moe_router_b[l // 2], (0, LANES - N_EXPERTS)).reshape(1, LANES)
            router = (rw, rb)
        outs = _merge_call(y_a, y_b, y_c, pg, w_branch[l].astype(BF16), w_out[l].astype(BF16), xf,
                           mod[l], norm2_w[l].reshape(1, d), router, seq, 512)
        xf, h2 = outs[0], outs[1]
        nxt = None if l == depth - 1 else (norm1_w[l + 1].reshape(1, d), mod[l + 1])
        if dense:
            res = _ffn_call(h2, ffn_w1[l // 2].astype(BF16), ffn_w3[l // 2].astype(BF16),
                            ffn_w2[l // 2].astype(BF16), xf, mod[l], nxt, seq, tm, 256)
        else:
            y = _moe_call(h2, outs[2], moe_w1[l // 2].astype(BF16), moe_w3[l // 2].astype(BF16),
                          moe_w2[l // 2].astype(BF16), tm, _pick(moe_w1.shape[-1], (MOE_TF,)))
            res = _resid_call(y, xf, mod[l], nxt, seq, tm)
        xf = res[0]
        if nxt is not None:
            h = res[1]
    return xf.reshape(batch, seq, d)
```

```python
import functools
import math

import numpy as np
import jax
import jax.numpy as jnp
from jax import lax
from jax.experimental import pallas as pl
from jax.experimental.pallas import tpu as pltpu

F32 = jnp.float32
BF16 = jnp.bfloat16
NORM_EPS = 1e-6

CHUNK = 64
DA_HEADS, DA_DK, DA_DV = 4, 64, 128
HG_HEADS, HG_D = 4, 128
ML_HEADS, ML_D, ML_CONV = 4, 128, 4
N_EXPERTS = 8

LANES = 128
VMEM_LIMIT = 56 * 1024 * 1024

ATT_TQ = 256
ATT_HEADS_PER_STEP = 2
ATT_PAD_ROWS = 16
LOG2E = math.log2(math.e)
ATT_BOUND_MARGIN = 1.01
ATT_BOUND_EPS = 1e-3
ATT_BOUND_MAX = 40.0
ATT_ZERO_EXP = 160.0
HG_CHUNK = 128
HG_SUB = 32
HG_SAFE_DECAY = 80.0
ML_CHUNK = 256
MOE_CAPS = (256, 320, 384)
MOE_TF = 1408


def _cparams(sem, vmem=VMEM_LIMIT):
    return pltpu.CompilerParams(dimension_semantics=sem, vmem_limit_bytes=vmem)


def _dot(a, b):
    return jnp.dot(a, b, preferred_element_type=F32)


def _dot_nt(a, b):
    return lax.dot_general(a, b, (((1,), (1,)), ((), ())), preferred_element_type=F32)


def _split3(x):
    hi = x.astype(BF16)
    r1 = x - hi.astype(F32)
    mid = r1.astype(BF16)
    lo = (r1 - mid.astype(F32)).astype(BF16)
    return hi, mid, lo


def _norm_mod(x, w, scale, shift):
    y = x * lax.rsqrt(jnp.mean(x * x, axis=-1, keepdims=True) + NORM_EPS) * w
    return y * (1.0 + scale) + shift


def _log_sigmoid(x):
    return jnp.minimum(x, 0.0) - jnp.log1p(jnp.exp(-jnp.abs(x)))


def _mod_kernel(c_ref, w_ref, b_ref, o_ref):
    cnd = c_ref[...]
    cnd = cnd * jax.nn.sigmoid(cnd)
    o_ref[0] = _dot(cnd.astype(BF16), w_ref[0].astype(BF16)) + b_ref[0]


def _mod_call(c8, w_mod, b_mod):
    depth, d, n = w_mod.shape
    tn = 1536 if n % 1536 == 0 else n
    return pl.pallas_call(
        _mod_kernel,
        grid=(depth, n // tn),
        in_specs=[pl.BlockSpec((8, d), lambda l, j: (0, 0)),
                  pl.BlockSpec((1, d, tn), lambda l, j: (l, 0, j)),
                  pl.BlockSpec((1, 1, tn), lambda l, j: (l, 0, j))],
        out_specs=pl.BlockSpec((1, 8, tn), lambda l, j: (l, 0, j)),
        out_shape=jax.ShapeDtypeStruct((depth, 8, n), F32),
        compiler_params=_cparams(("arbitrary", "arbitrary")),
        name="adaln_mod",
    )(c8, w_mod, b_mod)


def _normmod_kernel(x_ref, nw_ref, mod_ref, h_ref):
    m = mod_ref[0]
    h_ref[...] = _norm_mod(x_ref[...], nw_ref[...], m[1:2], m[0:1]).astype(BF16)


def _normmod_call(x, nw, mod, seq, tm):
    t, d = x.shape
    per_b = seq // tm
    return pl.pallas_call(
        _normmod_kernel,
        grid=(t // tm,),
        in_specs=[pl.BlockSpec((tm, d), lambda i: (i, 0)),
                  pl.BlockSpec((1, d), lambda i: (0, 0)),
                  pl.BlockSpec((1, 6, d), lambda i: (i // per_b, 0, 0))],
        out_specs=pl.BlockSpec((tm, d), lambda i: (i, 0)),
        out_shape=jax.ShapeDtypeStruct((t, d), BF16),
        compiler_params=_cparams(("arbitrary",)),
        name="prenorm_mod",
    )(x, nw, mod)


def _mm_kernel(x_ref, w_ref, b_ref, o_ref):
    o_ref[...] = (_dot(x_ref[...], w_ref[...]) + b_ref[...]).astype(o_ref.dtype)


def _matmul(x, w, bias, out_dtype, tm, tn):
    t, k = x.shape
    n = w.shape[1]
    return pl.pallas_call(
        _mm_kernel,
        grid=(n // tn, t // tm),
        in_specs=[pl.BlockSpec((tm, k), lambda j, i: (i, 0)),
                  pl.BlockSpec((k, tn), lambda j, i: (0, j)),
                  pl.BlockSpec((1, tn), lambda j, i: (0, j))],
        out_specs=pl.BlockSpec((tm, tn), lambda j, i: (i, j)),
        out_shape=jax.ShapeDtypeStruct((t, n), out_dtype),
        compiler_params=_cparams(("arbitrary", "arbitrary")),
        name="in_proj",
    )(x, w, bias)


def _attn_prep_kernel(a_ref, qw_ref, kw_ref, kn_ref, qt_ref, vt_ref):
    hw = DA_HEADS * 2 * DA_DK
    a = a_ref[...]
    r = lax.broadcasted_iota(jnp.int32, (hw, hw), 0) // DA_DK
    c = lax.broadcasted_iota(jnp.int32, (hw, hw), 1) // DA_DK
    group = jnp.where(r == c, 1.0, 0.0).astype(BF16)

    def qk_norm(z, w):
        ms = _dot((z * z).astype(BF16), group) * (1.0 / DA_DK)
        return z * lax.rsqrt(ms + NORM_EPS) * w

    qn = qk_norm(a[:, :hw].astype(F32), qw_ref[...])
    kn = qk_norm(a[:, hw:2 * hw].astype(F32), kw_ref[...])
    kn_ref[...] = kn.astype(BF16)
    qt_ref[0] = qn.T.astype(BF16)
    vt_ref[0] = a[:, 2 * hw:].astype(F32).T.astype(BF16)


def _attn_prep_call(a, qw, kw, batch, seq, ts):
    t = a.shape[0]
    hw = DA_HEADS * 2 * DA_DK
    hv = DA_HEADS * DA_DV
    per_b = seq // ts
    return pl.pallas_call(
        _attn_prep_kernel,
        grid=(t // ts,),
        in_specs=[pl.BlockSpec((ts, 2 * hw + hv), lambda i: (i, 0)),
                  pl.BlockSpec((1, hw), lambda i: (0, 0)),
                  pl.BlockSpec((1, hw), lambda i: (0, 0))],
        out_specs=[pl.BlockSpec((ts, hw), lambda i: (i, 0)),
                   pl.BlockSpec((1, hw, ts), lambda i: (i // per_b, 0, i % per_b)),
                   pl.BlockSpec((1, hv, ts), lambda i: (i // per_b, 0, i % per_b))],
        out_shape=[jax.ShapeDtypeStruct((t, hw), BF16),
                   jax.ShapeDtypeStruct((batch, hw, seq), BF16),
                   jax.ShapeDtypeStruct((batch, hv, seq), BF16)],
        compiler_params=_cparams(("arbitrary",)),
        name="attn_prep",
    )(a, qw, kw)


def _attn_slopes():
    return [LOG2E * 2.0 ** (-8.0 * (h + 1) / DA_HEADS) for h in range(DA_HEADS)]


def _attn_window(group, tq):
    hp = ATT_HEADS_PER_STEP
    blocks = [int((ATT_ZERO_EXP / s - 1.0) // tq) + 1 for s in _attn_slopes()[group * hp:(group + 1) * hp]]
    return max(blocks)


def _attn_kernel(sc_ref, qt_ref, k_ref, vt_ref, sw_ref, o_ref,
                 qz_ref, m_ref, acc_ref, vs_ref, kmax_ref, s_ref, s2_ref, *, tq, seq):
    hp = ATT_HEADS_PER_STEP
    dv = DA_DV
    g = pl.program_id(1)
    i = pl.program_id(2)
    lam = sc_ref[DA_HEADS]
    out_scale = sc_ref[DA_HEADS + 1]
    slopes = [sc_ref[hp * g + hh] for hh in range(hp)]
    row16 = lax.broadcasted_iota(jnp.int32, (ATT_PAD_ROWS, tq), 0)
    ones_rows = jnp.where(row16 == 0, 1.0, 0.0).astype(BF16)

    half = lax.broadcasted_iota(jnp.int32, (1, 2 * DA_DK), 1) < DA_DK

    @pl.when(i == 0)
    def _():
        pos = lax.broadcasted_iota(jnp.int32, (1, seq), 1) & (tq - 1)
        rel = (pos - (tq - 1)).astype(F32)
        sub16 = lax.broadcasted_iota(jnp.int32, (ATT_PAD_ROWS, seq), 0)
        gr = lax.broadcasted_iota(jnp.int32, (2 * DA_DK, 2 * DA_DK), 0) // DA_DK
        gc = lax.broadcasted_iota(jnp.int32, (2 * DA_DK, 2 * DA_DK), 1) // DA_DK
        group = jnp.where(gr == gc, 1.0, 0.0).astype(BF16)
        for hh in range(hp):
            w = jnp.exp2(slopes[hh] * rel)
            vs_ref[hh, 0:dv, :] = (vt_ref[0, hh * dv:(hh + 1) * dv, :].astype(F32) * w).astype(BF16)
            vs_ref[hh, dv:dv + ATT_PAD_ROWS, :] = jnp.where(sub16 == 0, w, 0.0).astype(BF16)

            def knorm(n, best):
                kc = k_ref[pl.ds(pl.multiple_of(n * tq, tq), tq),
                           hh * 2 * DA_DK:(hh + 1) * 2 * DA_DK].astype(F32)
                return jnp.maximum(best, jnp.max(_dot((kc * kc).astype(BF16), group), axis=0, keepdims=True))

            k2 = lax.fori_loop(0, seq // tq, knorm, jnp.zeros((1, 2 * DA_DK), F32))
            kmax_ref[2 * hh] = jnp.max(jnp.where(half, k2, 0.0), axis=1, keepdims=True)
            kmax_ref[2 * hh + 1] = jnp.max(jnp.where(half, 0.0, k2), axis=1, keepdims=True)

    row = lax.broadcasted_iota(jnp.int32, (2 * DA_DK, tq), 0)
    qq_row = lax.broadcasted_iota(jnp.int32, (1, tq), 1).astype(F32)
    bound_max = jnp.zeros((), F32)
    for hh in range(hp):
        qt = qt_ref[0, hh * 2 * DA_DK:(hh + 1) * 2 * DA_DK, :]
        zero = jnp.zeros_like(qt)
        q1 = jnp.where(row < DA_DK, qt, zero)
        q2 = jnp.where(row >= DA_DK, qt, zero)
        qz_ref[hh] = jnp.concatenate([q1, q2], axis=1)
        qn = jnp.concatenate(
            [jnp.sum(jnp.square(q1.astype(F32)), axis=0, keepdims=True) * kmax_ref[2 * hh],
             jnp.sum(jnp.square(q2.astype(F32)), axis=0, keepdims=True) * kmax_ref[2 * hh + 1]], axis=1)
        bound = jnp.sqrt(qn) * ATT_BOUND_MARGIN + ATT_BOUND_EPS
        bound_max = jnp.maximum(bound_max, jnp.max(bound))
        m_ref[hh] = bound + slopes[hh] * jnp.concatenate([qq_row, qq_row], axis=1)
    acc_ref[...] = jnp.zeros(acc_ref.shape, F32)

    kk = lax.broadcasted_iota(jnp.int32, (tq, 2 * tq), 0)
    cc = lax.broadcasted_iota(jnp.int32, (tq, 2 * tq), 1)
    qq = jnp.where(cc >= tq, cc - tq, cc)
    dist = (qq - jnp.abs(qq - kk)).astype(F32)
    visible = (kk // CHUNK) <= (qq // CHUNK)

    def scores(hh, blk):
        start = pl.multiple_of(blk * tq, tq)
        return _dot(k_ref[pl.ds(start, tq), hh * 2 * DA_DK:(hh + 1) * 2 * DA_DK], qz_ref[hh])

    def diag_values(hh):
        start = pl.multiple_of(i * tq, tq)
        return jnp.concatenate([vt_ref[0, hh * dv:(hh + 1) * dv, pl.ds(start, tq)], ones_rows], axis=0)

    @pl.when(bound_max <= ATT_BOUND_MAX)
    def _():
        window = jnp.int32(_attn_window(0, tq))
        for grp in range(1, DA_HEADS // hp):
            window = jnp.where(g == grp, _attn_window(grp, tq), window)
        first = (jnp.maximum(i - window, 0) // 2) * 2
        for hh in range(hp):
            s_ref[hh] = scores(hh, first)
            s2_ref[hh] = scores(hh, jnp.minimum(first + 1, i))

        def probs(hh, blk, s):
            last_key = ((blk + 1 - i) * tq - 1).astype(F32)
            return jnp.exp2(s - (m_ref[hh] - slopes[hh] * last_key)).astype(BF16)

        def body(jj, carry):
            blk = 2 * jj
            start = pl.multiple_of(blk * tq, 2 * tq)
            for hh in range(hp):
                p = jnp.concatenate([probs(hh, blk, s_ref[hh]), probs(hh, blk + 1, s2_ref[hh])], axis=0)
                s_ref[hh] = scores(hh, jnp.minimum(blk + 2, i))
                s2_ref[hh] = scores(hh, jnp.minimum(blk + 3, i))
                acc_ref[hh] += _dot(vs_ref[hh, :, pl.ds(start, 2 * tq)], p)
            return carry

        lax.fori_loop(first // 2, i // 2, body, 0)

        @pl.when(i % 2 == 1)
        def _():
            start = pl.multiple_of((i - 1) * tq, tq)
            for hh in range(hp):
                acc_ref[hh] += _dot(vs_ref[hh, :, pl.ds(start, tq)], probs(hh, i - 1, s_ref[hh]))
                s_ref[hh] = s2_ref[hh]

        for hh in range(hp):
            s = jnp.where(visible, s_ref[hh] + slopes[hh] * dist, -jnp.inf)
            acc_ref[hh] += _dot(diag_values(hh), jnp.exp2(s - m_ref[hh]).astype(BF16))

    @pl.when(bound_max > ATT_BOUND_MAX)
    def _():
        m_ref[...] = jnp.full(m_ref.shape, -jnp.inf, F32)

        def update(hh, s, bound_shift, v_aug):
            m_old = m_ref[hh]
            m_new = jnp.maximum(m_old, jnp.max(s, axis=0, keepdims=True) + bound_shift)
            p = jnp.exp2(s - (m_new - bound_shift))
            acc_ref[hh] = jnp.exp2(m_old - m_new) * acc_ref[hh] + _dot(v_aug, p.astype(BF16))
            m_ref[hh] = m_new

        def body(j, carry):
            start = pl.multiple_of(j * tq, tq)
            last_key = ((j + 1 - i) * tq - 1).astype(F32)
            for hh in range(hp):
                update(hh, scores(hh, j), slopes[hh] * last_key, vs_ref[hh, :, pl.ds(start, tq)])
            return carry

        lax.fori_loop(0, i, body, 0)
        for hh in range(hp):
            s = jnp.where(visible, scores(hh, i) + slopes[hh] * dist, -jnp.inf)
            update(hh, s, jnp.zeros((), F32), diag_values(hh))

    outs = []
    for hh in range(hp):
        acc = acc_ref[hh]
        o2 = acc[:dv, :] * (1.0 / acc[dv:dv + 1, :])
        o = (o2[:, :tq] - lam * o2[:, tq:]).T
        o = o * lax.rsqrt(jnp.mean(o * o, axis=-1, keepdims=True) + NORM_EPS) * sw_ref[...]
        outs.append(o * out_scale)
    o_ref[...] = jnp.concatenate(outs, axis=1).astype(o_ref.dtype)


def _attn_call(scal, qt, kn, vt, sw, batch, seq, tq):
    t = kn.shape[0]
    nq = seq // tq
    hp = ATT_HEADS_PER_STEP
    return pl.pallas_call(
        functools.partial(_attn_kernel, tq=tq, seq=seq),
        grid=(batch, DA_HEADS // hp, nq),
        in_specs=[pl.BlockSpec(memory_space=pltpu.SMEM),
                  pl.BlockSpec((1, hp * 2 * DA_DK, tq), lambda b, g, i: (b, g, i)),
                  pl.BlockSpec((seq, hp * 2 * DA_DK), lambda b, g, i: (b, g)),
                  pl.BlockSpec((1, hp * DA_DV, seq), lambda b, g, i: (b, g, 0)),
                  pl.BlockSpec((1, DA_DV), lambda b, g, i: (0, 0))],
        out_specs=pl.BlockSpec((tq, hp * DA_DV), lambda b, g, i: (b * nq + i, g)),
        out_shape=jax.ShapeDtypeStruct((t, DA_HEADS * DA_DV), BF16),
        scratch_shapes=[pltpu.VMEM((hp, 2 * DA_DK, 2 * tq), BF16),
                        pltpu.VMEM((hp, 1, 2 * tq), F32),
                        pltpu.VMEM((hp, DA_DV + ATT_PAD_ROWS, 2 * tq), F32),
                        pltpu.VMEM((hp, DA_DV + ATT_PAD_ROWS, seq), BF16),
                        pltpu.VMEM((2 * hp, 1, 1), F32),
                        pltpu.VMEM((hp, tq, 2 * tq), F32),
                        pltpu.VMEM((hp, tq, 2 * tq), F32)],
        compiler_params=_cparams(("arbitrary", "arbitrary", "arbitrary")),
        name="diff_attn",
    )(scal, qt, kn, vt, sw)


def _hgrn_kernel(q_ref, f_ref, i_ref, g_ref, lb_ref, gw_ref, o_ref,
                 st_ref, b_ref, qs_ref, ks_ref, oi_ref, *, n_chunks):
    c = HG_CHUNK
    sub = HG_SUB
    d = HG_D
    nh = HG_HEADS
    chains = [(bb, hh) for bb in range(q_ref.shape[0]) for hh in range(nh)]

    @pl.when(pl.program_id(0) == 0)
    def _():
        st_ref[...] = jnp.zeros(st_ref.shape, F32)

    gw = gw_ref[...]
    rr = lax.broadcasted_iota(jnp.int32, (c, c), 0)
    cc = lax.broadcasted_iota(jnp.int32, (c, c), 1)
    causal = cc <= rr
    tril = jnp.where(causal, 1.0, 0.0).astype(BF16)

    def chunk(n, carry):
        r0 = pl.multiple_of(n * c, c)
        heads = []
        decay = jnp.zeros((), F32)
        for ch, (bb, hh) in enumerate(chains):
            cols = slice(hh * d, (hh + 1) * d)
            log_lb = lb_ref[hh, 0:1, :]
            q = q_ref[bb, pl.ds(r0, c), cols]
            q = q * jax.nn.sigmoid(q)
            a = lb_ref[hh, 1:2, :] + _log_sigmoid(f_ref[bb, pl.ds(r0, c), cols])
            logf = jnp.maximum(log_lb, a) + jnp.log1p(jnp.exp(-jnp.abs(log_lb - a)))
            k = 1.0 - jnp.exp(logf)
            v = i_ref[bb, pl.ds(r0, c), cols]
            hi, mid, lo = _split3(logf)
            b = _dot(tril, hi) + _dot(tril, mid) + _dot(tril, lo)
            b_last = b[c - 1:c, :]
            st = st_ref[ch]
            o_inter = _dot_nt((q * jnp.exp(b)).astype(BF16), st.astype(BF16))
            k_hat = k * jnp.exp(b_last - b)
            st_ref[ch] = st * jnp.exp(b_last) + _dot(v.T.astype(BF16), k_hat.astype(BF16))
            betas = []
            for blk in range(c // sub):
                beta = jnp.zeros((1, d), F32) if blk == 0 else b[blk * sub - 1:blk * sub, :]
                betas.append(beta)
                b_end = b[(blk + 1) * sub - 1:(blk + 1) * sub, :]
                decay = jnp.maximum(decay, jnp.max(beta - b_end))
            heads.append((q, k, v, b, betas, o_inter))

        @pl.when(decay < HG_SAFE_DECAY)
        def _():
            for ch, (q, k, v, b, betas, _) in enumerate(heads):
                rows = []
                for blk in range(c // sub):
                    beta = betas[blk]
                    q_t = q[blk * sub:(blk + 1) * sub, :] * jnp.exp(b[blk * sub:(blk + 1) * sub, :] - beta)
                    k_t = k * jnp.exp(jnp.minimum(beta - b, HG_SAFE_DECAY))
                    rows.append(_dot_nt(q_t.astype(BF16), k_t.astype(BF16)))
                att = jnp.where(causal, jnp.concatenate(rows, axis=0), 0.0)
                oi_ref[ch] = _dot(att.astype(BF16), v.astype(BF16))

        @pl.when(decay >= HG_SAFE_DECAY)
        def _():
            ridx = lax.broadcasted_iota(jnp.int32, (c, d), 0)
            for ch, (q, k, v, b, _, _) in enumerate(heads):
                b_ref[...] = b
                qs_ref[...] = q
                ks_ref[...] = k

                def row(t, carry2):
                    bt = b_ref[pl.ds(t, 1), :]
                    e = jnp.exp(jnp.where(ridx <= t, bt - b_ref[...], -jnp.inf))
                    w = jnp.sum(qs_ref[pl.ds(t, 1), :] * ks_ref[...] * e, axis=1, keepdims=True)
                    oi_ref[ch, pl.ds(t, 1), :] = jnp.sum(w * v, axis=0, keepdims=True)
                    return carry2

                lax.fori_loop(0, c, row, 0)

        for ch, (bb, hh) in enumerate(chains):
            cols = slice(hh * d, (hh + 1) * d)
            o = heads[ch][5] + oi_ref[ch]
            o = o * lax.rsqrt(jnp.mean(o * o, axis=-1, keepdims=True) + NORM_EPS) * gw
            g = g_ref[bb, pl.ds(r0, c), cols]
            o_ref[bb, pl.ds(r0, c), cols] = (o * (g * jax.nn.sigmoid(g))).astype(o_ref.dtype)
        return carry

    lax.fori_loop(0, n_chunks, chunk, 0)


def _hgrn_call(bproj, lbs, gw, batch, seq, ts):
    t = bproj.shape[0]
    d = HG_D
    nh = HG_HEADS
    bp3 = bproj.reshape(batch, seq, bproj.shape[1])
    spec = lambda off: pl.BlockSpec((batch, ts, nh * d), lambda i: (0, i, off))
    out = pl.pallas_call(
        functools.partial(_hgrn_kernel, n_chunks=ts // HG_CHUNK),
        grid=(seq // ts,),
        in_specs=[spec(0), spec(1), spec(2), spec(3),
                  pl.BlockSpec((nh, 2, d), lambda i: (0, 0, 0)),
                  pl.BlockSpec((1, d), lambda i: (0, 0))],
        out_specs=pl.BlockSpec((batch, ts, nh * d), lambda i: (0, i, 0)),
        out_shape=jax.ShapeDtypeStruct((batch, seq, nh * d), BF16),
        scratch_shapes=[pltpu.VMEM((batch * nh, d, d), F32),
                        pltpu.VMEM((HG_CHUNK, d), F32),
                        pltpu.VMEM((HG_CHUNK, d), F32),
                        pltpu.VMEM((HG_CHUNK, d), F32),
                        pltpu.VMEM((batch * nh, HG_CHUNK, d), F32)],
        compiler_params=_cparams(("arbitrary",)),
        name="hgrn2",
    )(bp3, bp3, bp3, bp3, lbs, gw)
    return out.reshape(t, nh * d)


def _mlstm_kernel(uq_ref, uk_ref, v_ref, op_ref, gi_ref, gf_ref, cwq_ref, cwk_ref,
                  cbq_ref, cbk_ref, nw_ref, o_ref,
                  xq_ref, xk_ref, c_ref, m_ref, bs_ref):
    L = ML_CHUNK
    d = ML_D
    nh = ML_HEADS
    nb = uq_ref.shape[0]
    i = pl.program_id(0)
    rr = lax.broadcasted_iota(jnp.int32, (L, L), 0)
    cc = lax.broadcasted_iota(jnp.int32, (L, L), 1)

    @pl.when(i == 0)
    def _():
        c_ref[...] = jnp.zeros(c_ref.shape, F32)
        m_ref[...] = jnp.zeros(m_ref.shape, F32)
        xq_ref[:, 0:8, :] = jnp.zeros((nb, 8, nh * d), F32)
        xk_ref[:, 0:8, :] = jnp.zeros((nb, 8, nh * d), F32)
        upper = jnp.where(rr <= cc, 1.0, 0.0).astype(BF16)
        for ch in range(nb * nh):
            hi, mid, lo = _split3(_log_sigmoid(gf_ref[ch // nh, ch % nh]))
            bs_ref[ch] = _dot(hi, upper) + _dot(mid, upper) + _dot(lo, upper)

    def conv_silu(bb, u_ref, x_ref, w_ref, b_ref):
        x_ref[bb, 8:8 + L, :] = u_ref[bb]
        y = b_ref[...] + w_ref[ML_CONV - 1:ML_CONV, :] * x_ref[bb, 8:8 + L, :]
        for j in range(ML_CONV - 1):
            y = y + w_ref[j:j + 1, :] * x_ref[bb, 5 + j:5 + j + L, :]
        x_ref[bb, 0:8, :] = x_ref[bb, L:L + 8, :]
        return y * jax.nn.sigmoid(y)

    q_all = [conv_silu(bb, uq_ref, xq_ref, cwq_ref, cbq_ref) for bb in range(nb)]
    k_all = [conv_silu(bb, uk_ref, xk_ref, cwk_ref, cbk_ref) * (d ** -0.5) for bb in range(nb)]
    lane = lax.broadcasted_iota(jnp.int32, (L, d), 1)
    ones_col = jnp.where(lane == 0, 1.0, 0.0)
    for ch in range(nb * nh):
        bb, hh = ch // nh, ch % nh
        cols = slice(hh * d, (hh + 1) * d)
        q = q_all[bb][:, cols].astype(BF16)
        kt = k_all[bb][:, cols].T
        v_aug = jnp.concatenate([v_ref[bb, :, cols], ones_col], axis=1).astype(BF16)

        b_row = bs_ref[ch, pl.ds(i, 1), :]
        ig_row = gi_ref[bb, hh, pl.ds(i, 1), :]
        m_prev = m_ref[ch]
        g = b_row[:, L - 1:L]
        b_col = jnp.sum(jnp.where(rr == cc, b_row, 0.0), axis=1, keepdims=True)
        log_d = jnp.where(cc <= rr, b_col + (ig_row - b_row), -jnp.inf)
        log_inter = b_col + m_prev
        m_t = jnp.maximum(log_inter, jnp.max(log_d, axis=1, keepdims=True))
        w_intra = jnp.exp(log_d - m_t) * _dot(q, kt.astype(BF16))
        w_inter = jnp.exp(log_inter - m_t)
        c_aug = c_ref[ch]
        tot = w_inter * _dot(q, c_aug.astype(BF16)) + _dot(w_intra.astype(BF16), v_aug)
        denom = jnp.maximum(jnp.abs(tot[:, d:d + 1]), jnp.exp(-m_t))
        hout = tot[:, :d] / denom
        hout = hout * lax.rsqrt(jnp.mean(hout * hout, axis=-1, keepdims=True) + NORM_EPS) * nw_ref[...]
        o_ref[bb, :, cols] = (hout * jax.nn.sigmoid(op_ref[bb, :, cols])).astype(o_ref.dtype)

        log_w = g - b_row + ig_row
        m_new = jnp.maximum(g + m_prev, jnp.max(log_w, axis=1, keepdims=True))
        w_s = jnp.exp(log_w - m_new)
        c_ref[ch] = jnp.exp(g + m_prev - m_new) * c_aug + _dot((kt * w_s).astype(BF16), v_aug)
        m_ref[ch] = m_new


def _mlstm_call(cproj, gi, gf, conv_w, conv_b, nw, batch, seq):
    t = cproj.shape[0]
    d = ML_D
    nh = ML_HEADS
    L = ML_CHUNK
    nc = seq // L
    cp3 = cproj.reshape(batch, seq, cproj.shape[1])
    spec = lambda off: pl.BlockSpec((batch, L, nh * d), lambda i: (0, i, off))
    gspec = pl.BlockSpec((batch, nh, nc, L), lambda i: (0, 0, 0, 0))
    out = pl.pallas_call(
        _mlstm_kernel,
        grid=(nc,),
        in_specs=[spec(0), spec(1), spec(2), spec(3), gspec, gspec,
                  pl.BlockSpec((ML_CONV, nh * d), lambda i: (0, 0)),
                  pl.BlockSpec((ML_CONV, nh * d), lambda i: (0, 1)),
                  pl.BlockSpec((1, nh * d), lambda i: (0, 0)),
                  pl.BlockSpec((1, nh * d), lambda i: (0, 1)),
                  pl.BlockSpec((1, d), lambda i: (0, 0))],
        out_specs=pl.BlockSpec((batch, L, nh * d), lambda i: (0, i, 0)),
        out_shape=jax.ShapeDtypeStruct((batch, seq, nh * d), BF16),
        scratch_shapes=[pltpu.VMEM((batch, L + 8, nh * d), F32),
                        pltpu.VMEM((batch, L + 8, nh * d), F32),
                        pltpu.VMEM((batch * nh, d, 2 * d), F32),
                        pltpu.VMEM((batch * nh, 1, 1), F32),
                        pltpu.VMEM((batch * nh, nc, L), F32)],
        compiler_params=_cparams(("arbitrary",)),
        name="mlstm",
    )(cp3, cp3, cp3, cp3, gi, gf, conv_w, conv_w, conv_b, conv_b, nw)
    return out.reshape(t, nh * d)


def _top2_combine(logits):
    lane = lax.broadcasted_iota(jnp.int32, logits.shape, 1)
    lg = jnp.where(lane < N_EXPERTS, logits, -jnp.inf)
    ex = jnp.exp(lg - jnp.max(lg, axis=1, keepdims=True))
    probs = ex / jnp.sum(ex, axis=1, keepdims=True)
    p1 = jnp.max(probs, axis=1, keepdims=True)
    i1 = jnp.min(jnp.where(probs == p1, lane, LANES), axis=1, keepdims=True)
    rest = jnp.where(lane == i1, -1.0, probs)
    p2 = jnp.max(rest, axis=1, keepdims=True)
    i2 = jnp.min(jnp.where(rest == p2, lane, LANES), axis=1, keepdims=True)
    comb = jnp.where(lane == i1, p1, 0.0) + jnp.where(lane == i2, p2, 0.0)
    return comb / (p1 + p2)


def _merge_kernel(*refs, route):
    if route:
        (ya_ref, yb_ref, yc_ref, gp_ref, wb_ref, wo_ref, x_ref, mod_ref, nw_ref,
         rw_ref, rb_ref, xo_ref, h_ref, cmb_ref) = refs
    else:
        (ya_ref, yb_ref, yc_ref, gp_ref, wb_ref, wo_ref, x_ref, mod_ref, nw_ref,
         xo_ref, h_ref) = refs
    d = x_ref.shape[1]
    merged = None
    for n, y_ref in enumerate((ya_ref, yb_ref, yc_ref)):
        gate = jax.nn.sigmoid(gp_ref[:, n * d:(n + 1) * d].astype(F32))
        term = gate * _dot(y_ref[...], wb_ref[n])
        merged = term if merged is None else merged + term
    m = mod_ref[0]
    xn = x_ref[...] + m[2:3] * _dot(merged.astype(BF16), wo_ref[...])
    xo_ref[...] = xn
    h2 = _norm_mod(xn, nw_ref[...], m[4:5], m[3:4])
    h_ref[...] = h2.astype(BF16)
    if route:
        h_hi, h_mid, _ = _split3(h2)
        r_hi, r_mid, _ = _split3(rw_ref[...])
        logits = _dot(h_hi, r_hi) + _dot(h_mid, r_hi) + _dot(h_hi, r_mid) + rb_ref[...]
        cmb_ref[...] = _top2_combine(logits)


def _merge_call(ya, yb, yc, gp, wb, wo, x, mod, nw, router, seq, tm):
    t, d = x.shape
    bw = ya.shape[1]
    per_b = seq // tm
    route = router is not None
    tok = lambda w: pl.BlockSpec((tm, w), lambda i: (i, 0))
    const2 = lambda s: pl.BlockSpec(s, lambda i: (0, 0))
    in_specs = [tok(bw), tok(bw), tok(bw), tok(3 * d),
                pl.BlockSpec((3, bw, d), lambda i: (0, 0, 0)), const2((d, d)), tok(d),
                pl.BlockSpec((1, 6, d), lambda i: (i // per_b, 0, 0)), const2((1, d))]
    out_specs = [tok(d), tok(d)]
    out_shape = [jax.ShapeDtypeStruct((t, d), F32), jax.ShapeDtypeStruct((t, d), BF16)]
    args = [ya, yb, yc, gp, wb, wo, x, mod, nw]
    if route:
        in_specs += [const2((d, LANES)), const2((1, LANES))]
        out_specs.append(tok(LANES))
        out_shape.append(jax.ShapeDtypeStruct((t, LANES), F32))
        args += list(router)
    return pl.pallas_call(
        functools.partial(_merge_kernel, route=route),
        grid=(t // tm,),
        in_specs=in_specs, out_specs=out_specs, out_shape=out_shape,
        compiler_params=_cparams(("arbitrary",)),
        name="merge_out",
    )(*args)


def _finish(acc, x_ref, mod_ref, xo_ref, nxt):
    xn = x_ref[...] + mod_ref[0][5:6] * acc
    xo_ref[...] = xn
    if nxt is not None:
        nw_ref, modn_ref, hn_ref = nxt
        mn = modn_ref[0]
        hn_ref[...] = _norm_mod(xn, nw_ref[...], mn[1:2], mn[0:1]).astype(BF16)


def _ffn_kernel(*refs, has_next):
    if has_next:
        h_ref, w1_ref, w3_ref, w2_ref, x_ref, mod_ref, nw_ref, modn_ref, xo_ref, hn_ref, acc_ref = refs
        nxt = (nw_ref, modn_ref, hn_ref)
    else:
        h_ref, w1_ref, w3_ref, w2_ref, x_ref, mod_ref, xo_ref, acc_ref = refs
        nxt = None
    f = pl.program_id(1)

    @pl.when(f == 0)
    def _():
        acc_ref[...] = jnp.zeros(acc_ref.shape, F32)

    h = h_ref[...]
    a = _dot(h, w1_ref[...])
    act = a * jax.nn.sigmoid(a) * _dot(h, w3_ref[...])
    acc_ref[...] += _dot(act.astype(BF16), w2_ref[...])

    @pl.when(f == pl.num_programs(1) - 1)
    def _():
        _finish(acc_ref[...], x_ref, mod_ref, xo_ref, nxt)


def _ffn_call(h, w1, w3, w2, x, mod, nxt, seq, tm, tf):
    t, d = x.shape
    ff = w1.shape[1]
    per_b = seq // tm
    tok = lambda: pl.BlockSpec((tm, d), lambda i, f: (i, 0))
    modspec = lambda: pl.BlockSpec((1, 6, d), lambda i, f: (i // per_b, 0, 0))
    in_specs = [tok(), pl.BlockSpec((d, tf), lambda i, f: (0, f)),
                pl.BlockSpec((d, tf), lambda i, f: (0, f)),
                pl.BlockSpec((tf, d), lambda i, f: (f, 0)), tok(), modspec()]
    out_specs = [tok()]
    out_shape = [jax.ShapeDtypeStruct((t, d), F32)]
    args = [h, w1, w3, w2, x, mod]
    if nxt is not None:
        in_specs += [pl.BlockSpec((1, d), lambda i, f: (0, 0)), modspec()]
        out_specs.append(tok())
        out_shape.append(jax.ShapeDtypeStruct((t, d), BF16))
        args += list(nxt)
    return pl.pallas_call(
        functools.partial(_ffn_kernel, has_next=nxt is not None),
        grid=(t // tm, ff // tf),
        in_specs=in_specs, out_specs=out_specs, out_shape=out_shape,
        scratch_shapes=[pltpu.VMEM((tm, d), F32)],
        compiler_params=_cparams(("arbitrary", "arbitrary")),
        name="ffn_swiglu",
    )(*args)


def _moe_kernel(h_ref, cmb_ref, w1_ref, w3_ref, w2_ref, y_ref,
                acc_ref, xg_ref, ya_ref, rk_ref, rkt_ref, cnt_ref):
    e = pl.program_id(1)
    f = pl.program_id(2)
    tm = h_ref.shape[0]

    @pl.when((e == 0) & (f == 0))
    def _():
        r = lax.broadcasted_iota(jnp.int32, (tm, tm), 0)
        c = lax.broadcasted_iota(jnp.int32, (tm, tm), 1)
        before = jnp.where(c < r, 1.0, 0.0).astype(BF16)
        sel = cmb_ref[...] > 0.0
        rank = _dot(before, jnp.where(sel, 1.0, 0.0).astype(BF16))
        rk = jnp.where(sel, rank, -1.0)
        rk_ref[...] = rk
        rkt_ref[...] = rk.T
        cnt_ref[...] = jnp.sum(jnp.where(sel, 1.0, 0.0), axis=0, keepdims=True)
        acc_ref[...] = jnp.zeros(acc_ref.shape, F32)

    lane1 = lax.broadcasted_iota(jnp.int32, (1, LANES), 1)
    n_e = jnp.sum(jnp.where(lane1 == e, cnt_ref[...], 0.0)).astype(jnp.int32)

    def loop(n_blocks, body):
        if isinstance(n_blocks, int):
            for sb in range(n_blocks):
                body(sb, 0)
        else:
            lax.fori_loop(0, n_blocks, body, 0)

    def run(cap, n_blocks):
        cap_pad = -(-cap // LANES) * LANES

        def row0(sb):
            return sb * cap if isinstance(sb, int) else pl.multiple_of(sb * cap, cap)

        @pl.when(f == 0)
        def _():
            rank_row = rkt_ref[pl.ds(e, 1), :]

            def gather(sb, carry):
                r0 = row0(sb)
                slot = (r0 + lax.broadcasted_iota(jnp.int32, (cap, tm), 0)).astype(F32)
                onehot = jnp.where(rank_row == slot, 1.0, 0.0).astype(BF16)
                xg_ref[pl.ds(r0, cap), :] = _dot(onehot, h_ref[...]).astype(BF16)
                return carry

            loop(n_blocks, gather)

        def expert(sb, carry):
            r0 = row0(sb)
            xs = xg_ref[pl.ds(r0, cap), :]
            a = _dot(xs, w1_ref[0])
            act = a * jax.nn.sigmoid(a) * _dot(xs, w3_ref[0])
            part = _dot(act.astype(BF16), w2_ref[0])

            @pl.when(f == 0)
            def _():
                ya_ref[pl.ds(r0, cap), :] = part

            @pl.when(f > 0)
            def _():
                ya_ref[pl.ds(r0, cap), :] += part

            return carry

        loop(n_blocks, expert)

        @pl.when(f == pl.num_programs(2) - 1)
        def _():
            lane = lax.broadcasted_iota(jnp.int32, (tm, LANES), 1)
            rank_col = jnp.sum(jnp.where(lane == e, rk_ref[...], 0.0), axis=1, keepdims=True)
            w_col = jnp.sum(jnp.where(lane == e, cmb_ref[...], 0.0), axis=1, keepdims=True)
            col = lax.broadcasted_iota(jnp.int32, (tm, cap_pad), 1)

            def scatter(sb, carry):
                r0 = row0(sb)
                ys = ya_ref[pl.ds(r0, cap), :].astype(BF16)
                if cap_pad > cap:
                    ys = jnp.concatenate([ys, jnp.zeros((cap_pad - cap, ys.shape[1]), BF16)], axis=0)
                hit = (rank_col == (r0 + col).astype(F32)) & (col < cap)
                acc_ref[...] += w_col * _dot(jnp.where(hit, 1.0, 0.0).astype(BF16), ys)
                return carry

            loop(n_blocks, scatter)

    for idx, cap in enumerate(MOE_CAPS):
        lower = MOE_CAPS[idx - 1] if idx else 0
        if idx == len(MOE_CAPS) - 1:
            pl.when(n_e > lower)(functools.partial(run, cap, (n_e + cap - 1) // cap))
        else:
            pl.when((n_e > lower) & (n_e <= cap))(functools.partial(run, cap, 1))

    @pl.when((e == pl.num_programs(1) - 1) & (f == pl.num_programs(2) - 1))
    def _():
        y_ref[...] = acc_ref[...].astype(y_ref.dtype)


def _moe_call(h, cmb, w1, w3, w2, tm, tf):
    t, d = h.shape
    ne, _, ff = w1.shape
    rows = -(-tm // MOE_CAPS[-1]) * MOE_CAPS[-1]
    return pl.pallas_call(
        _moe_kernel,
        grid=(t // tm, ne, ff // tf),
        in_specs=[pl.BlockSpec((tm, d), lambda i, e, f: (i, 0)),
                  pl.BlockSpec((tm, LANES), lambda i, e, f: (i, 0)),
                  pl.BlockSpec((1, d, tf), lambda i, e, f: (e, 0, f)),
                  pl.BlockSpec((1, d, tf), lambda i, e, f: (e, 0, f)),
                  pl.BlockSpec((1, tf, d), lambda i, e, f: (e, f, 0))],
        out_specs=pl.BlockSpec((tm, d), lambda i, e, f: (i, 0)),
        out_shape=jax.ShapeDtypeStruct((t, d), BF16),
        scratch_shapes=[pltpu.VMEM((tm, d), F32),
                        pltpu.VMEM((rows, d), BF16),
                        pltpu.VMEM((rows, d), F32),
                        pltpu.VMEM((tm, LANES), F32),
                        pltpu.VMEM((LANES, tm), F32),
                        pltpu.VMEM((1, LANES), F32)],
        compiler_params=_cparams(("arbitrary", "arbitrary", "arbitrary")),
        name="moe_top2",
    )(h, cmb, w1, w3, w2)


def _resid_kernel(*refs, has_next):
    if has_next:
        y_ref, x_ref, mod_ref, nw_ref, modn_ref, xo_ref, hn_ref = refs
        nxt = (nw_ref, modn_ref, hn_ref)
    else:
        y_ref, x_ref, mod_ref, xo_ref = refs
        nxt = None
    _finish(y_ref[...].astype(F32), x_ref, mod_ref, xo_ref, nxt)


def _resid_call(y, x, mod, nxt, seq, tm):
    t, d = x.shape
    per_b = seq // tm
    tok = lambda: pl.BlockSpec((tm, d), lambda i: (i, 0))
    modspec = lambda: pl.BlockSpec((1, 6, d), lambda i: (i // per_b, 0, 0))
    in_specs = [tok(), tok(), modspec()]
    out_specs = [tok()]
    out_shape = [jax.ShapeDtypeStruct((t, d), F32)]
    args = [y, x, mod]
    if nxt is not None:
        in_specs += [pl.BlockSpec((1, d), lambda i: (0, 0)), modspec()]
        out_specs.append(tok())
        out_shape.append(jax.ShapeDtypeStruct((t, d), BF16))
        args += list(nxt)
    return pl.pallas_call(
        functools.partial(_resid_kernel, has_next=nxt is not None),
        grid=(t // tm,),
        in_specs=in_specs, out_specs=out_specs, out_shape=out_shape,
        compiler_params=_cparams(("arbitrary",)),
        name="moe_residual",
    )(*args)


def _attn_col_perm():
    hq = DA_HEADS * DA_DK
    idx = []
    for base in (0, 2 * hq):
        for h in range(DA_HEADS):
            idx += list(range(base + h * DA_DK, base + (h + 1) * DA_DK))
            idx += list(range(base + hq + h * DA_DK, base + hq + (h + 1) * DA_DK))
    idx += list(range(4 * hq, 4 * hq + DA_HEADS * DA_DV))
    return np.asarray(idx, np.int32)


def _pick(n, pref):
    for c in pref:
        if n % c == 0:
            return c
    return n


def kernel(x, c, w_mod, b_mod, norm1_w, norm2_w, w_in, c_conv_w, c_conv_b, a_qnorm_w, a_knorm_w,
           a_lambda_q1, a_lambda_k1, a_lambda_q2, a_lambda_k2, a_subln_w, b_lb_logits, b_gnorm_w,
           c_igate_b, c_fgate_b, c_norm_w, w_branch, w_out, ffn_w1, ffn_w3, ffn_w2,
           moe_router_w, moe_router_b, moe_w1, moe_w3, moe_w2):
    batch, seq, d = x.shape
    depth = w_in.shape[0]
    t = batch * seq
    n_a = 4 * DA_HEADS * DA_DK + DA_HEADS * DA_DV
    n_b = 4 * HG_HEADS * HG_D
    n_c = 4 * ML_HEADS * ML_D + 2 * ML_HEADS
    n_c_pad = -(-n_c // LANES) * LANES
    tm = _pick(seq, (1024, 512, 256))
    tq = _pick(seq, (ATT_TQ,))
    ts_h = _pick(seq, (512, 256, 128))

    c8 = jnp.zeros((8, d), F32).at[:batch].set(c)
    mod = _mod_call(c8, w_mod, b_mod.reshape(depth, 1, 6 * d))[:, :batch].reshape(depth, batch, 6, d)

    lb_all = jnp.cumsum(jax.nn.softmax(b_lb_logits.astype(F32), axis=0), axis=0)
    lb_all = lb_all - lb_all[:1]
    slopes = jnp.asarray(_attn_slopes(), F32)

    xf = x.reshape(t, d)
    h = _normmod_call(xf, norm1_w[0].reshape(1, d), mod[0], seq, tm)
    perm_a = _attn_col_perm()
    for l in range(depth):
        lam_init = 0.8 - 0.6 * math.exp(-0.3 * l)
        wl = w_in[l]
        w_a = wl[:, :n_a][:, perm_a].astype(BF16)
        w_b = wl[:, n_a:n_a + n_b].astype(BF16)
        w_c = jnp.pad(wl[:, n_a + n_b:n_a + n_b + n_c], ((0, 0), (0, n_c_pad - n_c))).astype(BF16)
        w_g = wl[:, n_a + n_b + n_c:].astype(BF16)
        bias_c = jnp.zeros((1, n_c_pad), F32)
        bias_c = bias_c.at[0, n_c - 2 * ML_HEADS:n_c - ML_HEADS].set(c_igate_b[l])
        bias_c = bias_c.at[0, n_c - ML_HEADS:n_c].set(c_fgate_b[l])
        pa = _matmul(h, w_a, jnp.zeros((1, n_a), F32), BF16, tm, n_a)
        pb = _matmul(h, w_b, jnp.zeros((1, n_b), F32), F32, tm, n_b // 2)
        pc = _matmul(h, w_c, bias_c, F32, tm, n_c_pad)
        pg = _matmul(h, w_g, jnp.zeros((1, 3 * d), F32), BF16, tm, 3 * d // 2)

        qw = (jnp.tile(a_qnorm_w[l], 2 * DA_HEADS) * (DA_DK ** -0.5 * LOG2E)).reshape(1, -1)
        kw = jnp.tile(a_knorm_w[l], 2 * DA_HEADS).reshape(1, -1)
        kn, qt, vt = _attn_prep_call(pa, qw, kw, batch, seq, _pick(seq, (512, 256)))
        lam = (jnp.exp(jnp.sum(a_lambda_q1[l] * a_lambda_k1[l]))
               - jnp.exp(jnp.sum(a_lambda_q2[l] * a_lambda_k2[l])) + lam_init)
        scal = jnp.concatenate([slopes, jnp.stack([lam, jnp.asarray(1.0 - lam_init, F32)])]).astype(F32)
        y_a = _attn_call(scal, qt, kn, vt, a_subln_w[l].reshape(1, -1), batch, seq, tq)

        lb = lb_all[l].reshape(HG_HEADS, 1, HG_D)
        lbs = jnp.concatenate([jnp.log(lb), jnp.log1p(-lb)], axis=1)
        y_b = _hgrn_call(pb, lbs, b_gnorm_w[l].reshape(1, -1), batch, seq, ts_h)

        gates = pc[:, n_c - 2 * ML_HEADS:n_c].reshape(batch, seq, 2, ML_HEADS)
        gates = gates.transpose(2, 0, 3, 1).reshape(2, batch, ML_HEADS, seq // ML_CHUNK, ML_CHUNK)
        y_c = _mlstm_call(pc, gates[0], gates[1], c_conv_w[l], c_conv_b[l].reshape(1, -1),
                          c_norm_w[l].reshape(1, -1), batch, seq)

        dense = l % 2 == 0
        router = None
        if not dense:
            rw = jnp.pad(moe_router_w[l // 2], ((0, 0), (0, LANES - N_EXPERTS)))
            rb = jnp.pad(moe_router_b[l // 2], (0, LANES - N_EXPERTS)).reshape(1, LANES)
            router = (rw, rb)
        outs = _merge_call(y_a, y_b, y_c, pg, w_branch[l].astype(BF16), w_out[l].astype(BF16), xf,
                           mod[l], norm2_w[l].reshape(1, d), router, seq, 512)
        xf, h2 = outs[0], outs[1]
        nxt = None if l == depth - 1 else (norm1_w[l + 1].reshape(1, d), mod[l + 1])
        if dense:
            res = _ffn_call(h2, ffn_w1[l // 2].astype(BF16), ffn_w3[l // 2].astype(BF16),
                            ffn_w2[l // 2].astype(BF16), xf, mod[l], nxt, seq, tm, 256)
        else:
            y = _moe_call(h2, outs[2], moe_w1[l // 2].astype(BF16), moe_w3[l // 2].astype(BF16),
                          moe_w2[l // 2].astype(BF16), tm, _pick(moe_w1.shape[-1], (MOE_TF,)))
            res = _resid_call(y, xf, mod[l], nxt, seq, tm)
        xf = res[0]
        if nxt is not None:
            h = res[1]
    return xf.reshape(batch, seq, d)
```

```python
import functools
import math

import numpy as np
import jax
import jax.numpy as jnp
from jax import lax
from jax.experimental import pallas as pl
from jax.experimental.pallas import tpu as pltpu

F32 = jnp.float32
BF16 = jnp.bfloat16
NORM_EPS = 1e-6

CHUNK = 64
DA_HEADS, DA_DK, DA_DV = 4, 64, 128
HG_HEADS, HG_D = 4, 128
ML_HEADS, ML_D, ML_CONV = 4, 128, 4
N_EXPERTS = 8

LANES = 128
VMEM_LIMIT = 56 * 1024 * 1024

ATT_TQ = 256
ATT_HEADS_PER_STEP = 2
ATT_PAD_ROWS = 16
LOG2E = math.log2(math.e)
ATT_BOUND_MARGIN = 1.01
ATT_BOUND_EPS = 1e-3
ATT_BOUND_MAX = 40.0
ATT_ZERO_EXP = 160.0
HG_CHUNK = 128
HG_SUB = 32
HG_SAFE_DECAY = 80.0
ML_CHUNK = 256
MOE_CAPS = (256, 320, 384)
MOE_TF = 1408


def _cparams(sem, vmem=VMEM_LIMIT):
    return pltpu.CompilerParams(dimension_semantics=sem, vmem_limit_bytes=vmem)


def _dot(a, b):
    return jnp.dot(a, b, preferred_element_type=F32)


def _dot_nt(a, b):
    return lax.dot_general(a, b, (((1,), (1,)), ((), ())), preferred_element_type=F32)


def _split3(x):
    hi = x.astype(BF16)
    r1 = x - hi.astype(F32)
    mid = r1.astype(BF16)
    lo = (r1 - mid.astype(F32)).astype(BF16)
    return hi, mid, lo


def _norm_mod(x, w, scale, shift):
    y = x * lax.rsqrt(jnp.mean(x * x, axis=-1, keepdims=True) + NORM_EPS) * w
    return y * (1.0 + scale) + shift


def _sigmoid(x):
    return 0.5 * jnp.tanh(0.5 * x) + 0.5


def _log1p_exp_neg(z):
    return jnp.log(1.0 + jnp.exp(-z))


def _log_sigmoid(x):
    return jnp.minimum(x, 0.0) - _log1p_exp_neg(jnp.abs(x))


def _mod_kernel(c_ref, w_ref, b_ref, o_ref):
    cnd = c_ref[...]
    cnd = cnd * _sigmoid(cnd)
    o_ref[0] = _dot(cnd.astype(BF16), w_ref[0].astype(BF16)) + b_ref[0]


def _mod_call(c8, w_mod, b_mod):
    depth, d, n = w_mod.shape
    tn = 1536 if n % 1536 == 0 else n
    return pl.pallas_call(
        _mod_kernel,
        grid=(depth, n // tn),
        in_specs=[pl.BlockSpec((8, d), lambda l, j: (0, 0)),
                  pl.BlockSpec((1, d, tn), lambda l, j: (l, 0, j)),
                  pl.BlockSpec((1, 1, tn), lambda l, j: (l, 0, j))],
        out_specs=pl.BlockSpec((1, 8, tn), lambda l, j: (l, 0, j)),
        out_shape=jax.ShapeDtypeStruct((depth, 8, n), F32),
        compiler_params=_cparams(("arbitrary", "arbitrary")),
        name="adaln_mod",
    )(c8, w_mod, b_mod)


def _normmod_kernel(x_ref, nw_ref, mod_ref, h_ref):
    m = mod_ref[0]
    h_ref[...] = _norm_mod(x_ref[...], nw_ref[...], m[1:2], m[0:1]).astype(BF16)


def _normmod_call(x, nw, mod, seq, tm):
    t, d = x.shape
    per_b = seq // tm
    return pl.pallas_call(
        _normmod_kernel,
        grid=(t // tm,),
        in_specs=[pl.BlockSpec((tm, d), lambda i: (i, 0)),
                  pl.BlockSpec((1, d), lambda i: (0, 0)),
                  pl.BlockSpec((1, 6, d), lambda i: (i // per_b, 0, 0))],
        out_specs=pl.BlockSpec((tm, d), lambda i: (i, 0)),
        out_shape=jax.ShapeDtypeStruct((t, d), BF16),
        compiler_params=_cparams(("arbitrary",)),
        name="prenorm_mod",
    )(x, nw, mod)


def _mm_kernel(x_ref, w_ref, b_ref, o_ref):
    o_ref[...] = (_dot(x_ref[...], w_ref[...]) + b_ref[...]).astype(o_ref.dtype)


def _matmul(x, w, bias, out_dtype, tm, tn):
    t, k = x.shape
    n = w.shape[1]
    return pl.pallas_call(
        _mm_kernel,
        grid=(n // tn, t // tm),
        in_specs=[pl.BlockSpec((tm, k), lambda j, i: (i, 0)),
                  pl.BlockSpec((k, tn), lambda j, i: (0, j)),
                  pl.BlockSpec((1, tn), lambda j, i: (0, j))],
        out_specs=pl.BlockSpec((tm, tn), lambda j, i: (i, j)),
        out_shape=jax.ShapeDtypeStruct((t, n), out_dtype),
        compiler_params=_cparams(("arbitrary", "arbitrary")),
        name="in_proj",
    )(x, w, bias)


def _attn_prep_kernel(a_ref, qw_ref, kw_ref, kn_ref, qt_ref, vt_ref):
    hw = DA_HEADS * 2 * DA_DK
    a = a_ref[...]
    r = lax.broadcasted_iota(jnp.int32, (hw, hw), 0) // DA_DK
    c = lax.broadcasted_iota(jnp.int32, (hw, hw), 1) // DA_DK
    group = jnp.where(r == c, 1.0, 0.0).astype(BF16)

    def qk_norm(z, w):
        ms = _dot((z * z).astype(BF16), group) * (1.0 / DA_DK)
        return z * lax.rsqrt(ms + NORM_EPS) * w

    qn = qk_norm(a[:, :hw].astype(F32), qw_ref[...])
    kn = qk_norm(a[:, hw:2 * hw].astype(F32), kw_ref[...])
    kn_ref[...] = kn.astype(BF16)
    qt_ref[0] = qn.T.astype(BF16)
    vt_ref[0] = a[:, 2 * hw:].astype(F32).T.astype(BF16)


def _attn_prep_call(a, qw, kw, batch, seq, ts):
    t = a.shape[0]
    hw = DA_HEADS * 2 * DA_DK
    hv = DA_HEADS * DA_DV
    per_b = seq // ts
    return pl.pallas_call(
        _attn_prep_kernel,
        grid=(t // ts,),
        in_specs=[pl.BlockSpec((ts, 2 * hw + hv), lambda i: (i, 0)),
                  pl.BlockSpec((1, hw), lambda i: (0, 0)),
                  pl.BlockSpec((1, hw), lambda i: (0, 0))],
        out_specs=[pl.BlockSpec((ts, hw), lambda i: (i, 0)),
                   pl.BlockSpec((1, hw, ts), lambda i: (i // per_b, 0, i % per_b)),
                   pl.BlockSpec((1, hv, ts), lambda i: (i // per_b, 0, i % per_b))],
        out_shape=[jax.ShapeDtypeStruct((t, hw), BF16),
                   jax.ShapeDtypeStruct((batch, hw, seq), BF16),
                   jax.ShapeDtypeStruct((batch, hv, seq), BF16)],
        compiler_params=_cparams(("arbitrary",)),
        name="attn_prep",
    )(a, qw, kw)


def _attn_slopes():
    return [LOG2E * 2.0 ** (-8.0 * (h + 1) / DA_HEADS) for h in range(DA_HEADS)]


def _attn_window(head, tq):
    return int((ATT_ZERO_EXP / _attn_slopes()[head] - 1.0) // tq) + 1


def _attn_kernel(sc_ref, qt_ref, k_ref, vt_ref, sw_ref, o_ref,
                 qz_ref, m_ref, acc_ref, vs_ref, kmax_ref, s_ref, s2_ref, *, tq, seq):
    hp = ATT_HEADS_PER_STEP
    dv = DA_DV
    g = pl.program_id(1)
    i = pl.program_id(2)
    lam = sc_ref[DA_HEADS]
    out_scale = sc_ref[DA_HEADS + 1]
    slopes = [sc_ref[hp * g + hh] for hh in range(hp)]
    row16 = lax.broadcasted_iota(jnp.int32, (ATT_PAD_ROWS, tq), 0)
    ones_rows = jnp.where(row16 == 0, 1.0, 0.0).astype(BF16)

    half = lax.broadcasted_iota(jnp.int32, (1, 2 * DA_DK), 1) < DA_DK

    @pl.when(i == 0)
    def _():
        pos = lax.broadcasted_iota(jnp.int32, (1, seq), 1) & (tq - 1)
        rel = (pos - (tq - 1)).astype(F32)
        sub16 = lax.broadcasted_iota(jnp.int32, (ATT_PAD_ROWS, seq), 0)
        gr = lax.broadcasted_iota(jnp.int32, (2 * DA_DK, 2 * DA_DK), 0) // DA_DK
        gc = lax.broadcasted_iota(jnp.int32, (2 * DA_DK, 2 * DA_DK), 1) // DA_DK
        group = jnp.where(gr == gc, 1.0, 0.0).astype(BF16)
        for hh in range(hp):
            w = jnp.exp2(slopes[hh] * rel)
            vs_ref[hh, 0:dv, :] = (vt_ref[0, hh * dv:(hh + 1) * dv, :].astype(F32) * w).astype(BF16)
            vs_ref[hh, dv:dv + ATT_PAD_ROWS, :] = jnp.where(sub16 == 0, w, 0.0).astype(BF16)

            def knorm(n, best):
                kc = k_ref[pl.ds(pl.multiple_of(n * tq, tq), tq),
                           hh * 2 * DA_DK:(hh + 1) * 2 * DA_DK].astype(F32)
                return jnp.maximum(best, jnp.max(_dot((kc * kc).astype(BF16), group), axis=0, keepdims=True))

            k2 = lax.fori_loop(0, seq // tq, knorm, jnp.zeros((1, 2 * DA_DK), F32))
            kmax_ref[2 * hh] = jnp.max(jnp.where(half, k2, 0.0), axis=1, keepdims=True)
            kmax_ref[2 * hh + 1] = jnp.max(jnp.where(half, 0.0, k2), axis=1, keepdims=True)

    row = lax.broadcasted_iota(jnp.int32, (2 * DA_DK, tq), 0)
    qq_row = lax.broadcasted_iota(jnp.int32, (1, tq), 1).astype(F32)
    bound_max = jnp.zeros((), F32)
    for hh in range(hp):
        qt = qt_ref[0, hh * 2 * DA_DK:(hh + 1) * 2 * DA_DK, :]
        zero = jnp.zeros_like(qt)
        q1 = jnp.where(row < DA_DK, qt, zero)
        q2 = jnp.where(row >= DA_DK, qt, zero)
        qz_ref[hh] = jnp.concatenate([q1, q2], axis=1)
        qn = jnp.concatenate(
            [jnp.sum(jnp.square(q1.astype(F32)), axis=0, keepdims=True) * kmax_ref[2 * hh],
             jnp.sum(jnp.square(q2.astype(F32)), axis=0, keepdims=True) * kmax_ref[2 * hh + 1]], axis=1)
        bound = jnp.sqrt(qn) * ATT_BOUND_MARGIN + ATT_BOUND_EPS
        bound_max = jnp.maximum(bound_max, jnp.max(bound))
        m_ref[hh] = bound + slopes[hh] * jnp.concatenate([qq_row, qq_row], axis=1)
    acc_ref[...] = jnp.zeros(acc_ref.shape, F32)

    kk = lax.broadcasted_iota(jnp.int32, (tq, 2 * tq), 0)
    cc = lax.broadcasted_iota(jnp.int32, (tq, 2 * tq), 1)
    qq = jnp.where(cc >= tq, cc - tq, cc)
    dist = (qq - jnp.abs(qq - kk)).astype(F32)
    visible = (kk // CHUNK) <= (qq // CHUNK)

    def scores(hh, blk):
        start = pl.multiple_of(blk * tq, tq)
        return _dot(k_ref[pl.ds(start, tq), hh * 2 * DA_DK:(hh + 1) * 2 * DA_DK], qz_ref[hh])

    def diag_values(hh):
        start = pl.multiple_of(i * tq, tq)
        return jnp.concatenate([vt_ref[0, hh * dv:(hh + 1) * dv, pl.ds(start, tq)], ones_rows], axis=0)

    @pl.when(bound_max <= ATT_BOUND_MAX)
    def _():
        firsts = []
        for hh in range(hp):
            window = jnp.int32(_attn_window(hh, tq))
            for grp in range(1, DA_HEADS // hp):
                window = jnp.where(g == grp, _attn_window(grp * hp + hh, tq), window)
            firsts.append((jnp.maximum(i - window, 0) // 2) * 2)
            s_ref[hh] = scores(hh, firsts[hh])
            s2_ref[hh] = scores(hh, jnp.minimum(firsts[hh] + 1, i))

        def probs(hh, blk, s):
            last_key = ((blk + 1 - i) * tq - 1).astype(F32)
            return jnp.exp2(s - (m_ref[hh] - slopes[hh] * last_key)).astype(BF16)

        def body(heads, jj, carry):
            blk = 2 * jj
            start = pl.multiple_of(blk * tq, 2 * tq)
            for hh in heads:
                p = jnp.concatenate([probs(hh, blk, s_ref[hh]), probs(hh, blk + 1, s2_ref[hh])], axis=0)
                s_ref[hh] = scores(hh, jnp.minimum(blk + 2, i))
                s2_ref[hh] = scores(hh, jnp.minimum(blk + 3, i))
                acc_ref[hh] += _dot(vs_ref[hh, :, pl.ds(start, 2 * tq)], p)
            return carry

        for hh in range(hp - 1, 0, -1):
            lax.fori_loop(firsts[hh] // 2, firsts[hh - 1] // 2,
                          functools.partial(body, tuple(range(hh, hp))), 0)
        lax.fori_loop(firsts[0] // 2, i // 2, functools.partial(body, tuple(range(hp))), 0)

        @pl.when(i % 2 == 1)
        def _():
            start = pl.multiple_of((i - 1) * tq, tq)
            for hh in range(hp):
                acc_ref[hh] += _dot(vs_ref[hh, :, pl.ds(start, tq)], probs(hh, i - 1, s_ref[hh]))
                s_ref[hh] = s2_ref[hh]

        for hh in range(hp):
            s = jnp.where(visible, s_ref[hh] + slopes[hh] * dist, -jnp.inf)
            acc_ref[hh] += _dot(diag_values(hh), jnp.exp2(s - m_ref[hh]).astype(BF16))

    @pl.when(bound_max > ATT_BOUND_MAX)
    def _():
        m_ref[...] = jnp.full(m_ref.shape, -jnp.inf, F32)

        def update(hh, s, bound_shift, v_aug):
            m_old = m_ref[hh]
            m_new = jnp.maximum(m_old, jnp.max(s, axis=0, keepdims=True) + bound_shift)
            p = jnp.exp2(s - (m_new - bound_shift))
            acc_ref[hh] = jnp.exp2(m_old - m_new) * acc_ref[hh] + _dot(v_aug, p.astype(BF16))
            m_ref[hh] = m_new

        def body(j, carry):
            start = pl.multiple_of(j * tq, tq)
            last_key = ((j + 1 - i) * tq - 1).astype(F32)
            for hh in range(hp):
                update(hh, scores(hh, j), slopes[hh] * last_key, vs_ref[hh, :, pl.ds(start, tq)])
            return carry

        lax.fori_loop(0, i, body, 0)
        for hh in range(hp):
            s = jnp.where(visible, scores(hh, i) + slopes[hh] * dist, -jnp.inf)
            update(hh, s, jnp.zeros((), F32), diag_values(hh))

    outs = []
    for hh in range(hp):
        acc = acc_ref[hh]
        o2 = acc[:dv, :] * (1.0 / acc[dv:dv + 1, :])
        o = (o2[:, :tq] - lam * o2[:, tq:]).T
        o = o * lax.rsqrt(jnp.mean(o * o, axis=-1, keepdims=True) + NORM_EPS) * sw_ref[...]
        outs.append(o * out_scale)
    o_ref[...] = jnp.concatenate(outs, axis=1).astype(o_ref.dtype)


def _attn_call(scal, qt, kn, vt, sw, batch, seq, tq):
    t = kn.shape[0]
    nq = seq // tq
    hp = ATT_HEADS_PER_STEP
    return pl.pallas_call(
        functools.partial(_attn_kernel, tq=tq, seq=seq),
        grid=(batch, DA_HEADS // hp, nq),
        in_specs=[pl.BlockSpec(memory_space=pltpu.SMEM),
                  pl.BlockSpec((1, hp * 2 * DA_DK, tq), lambda b, g, i: (b, g, i)),
                  pl.BlockSpec((seq, hp * 2 * DA_DK), lambda b, g, i: (b, g)),
                  pl.BlockSpec((1, hp * DA_DV, seq), lambda b, g, i: (b, g, 0)),
                  pl.BlockSpec((1, DA_DV), lambda b, g, i: (0, 0))],
        out_specs=pl.BlockSpec((tq, hp * DA_DV), lambda b, g, i: (b * nq + i, g)),
        out_shape=jax.ShapeDtypeStruct((t, DA_HEADS * DA_DV), BF16),
        scratch_shapes=[pltpu.VMEM((hp, 2 * DA_DK, 2 * tq), BF16),
                        pltpu.VMEM((hp, 1, 2 * tq), F32),
                        pltpu.VMEM((hp, DA_DV + ATT_PAD_ROWS, 2 * tq), F32),
                        pltpu.VMEM((hp, DA_DV + ATT_PAD_ROWS, seq), BF16),
                        pltpu.VMEM((2 * hp, 1, 1), F32),
                        pltpu.VMEM((hp, tq, 2 * tq), F32),
                        pltpu.VMEM((hp, tq, 2 * tq), F32)],
        compiler_params=_cparams(("arbitrary", "arbitrary", "arbitrary")),
        name="diff_attn",
    )(scal, qt, kn, vt, sw)


def _hgrn_kernel(q_ref, f_ref, i_ref, g_ref, lb_ref, gw_ref, o_ref,
                 st_ref, b_ref, qs_ref, ks_ref, oi_ref, *, n_chunks):
    c = HG_CHUNK
    sub = HG_SUB
    d = HG_D
    nh = HG_HEADS
    chains = [(bb, hh) for bb in range(q_ref.shape[0]) for hh in range(nh)]

    @pl.when(pl.program_id(0) == 0)
    def _():
        st_ref[...] = jnp.zeros(st_ref.shape, F32)

    gw = gw_ref[...]
    rr = lax.broadcasted_iota(jnp.int32, (c, c), 0)
    cc = lax.broadcasted_iota(jnp.int32, (c, c), 1)
    causal = cc <= rr
    tril = jnp.where(causal, 1.0, 0.0).astype(BF16)

    def chunk(n, carry):
        r0 = pl.multiple_of(n * c, c)
        heads = []
        decay = jnp.zeros((), F32)
        for ch, (bb, hh) in enumerate(chains):
            cols = slice(hh * d, (hh + 1) * d)
            log_lb = lb_ref[hh, 0:1, :]
            q = q_ref[bb, pl.ds(r0, c), cols]
            q = q * _sigmoid(q)
            a = lb_ref[hh, 1:2, :] + _log_sigmoid(f_ref[bb, pl.ds(r0, c), cols])
            logf = jnp.maximum(log_lb, a) + _log1p_exp_neg(jnp.abs(log_lb - a))
            k = 1.0 - jnp.exp(logf)
            v = i_ref[bb, pl.ds(r0, c), cols]
            hi, mid, lo = _split3(logf)
            b = _dot(tril, hi) + _dot(tril, mid) + _dot(tril, lo)
            b_last = b[c - 1:c, :]
            st = st_ref[ch]
            o_inter = _dot_nt((q * jnp.exp(b)).astype(BF16), st.astype(BF16))
            k_hat = k * jnp.exp(b_last - b)
            st_ref[ch] = st * jnp.exp(b_last) + _dot(v.T.astype(BF16), k_hat.astype(BF16))
            betas = []
            for blk in range(c // sub):
                beta = jnp.zeros((1, d), F32) if blk == 0 else b[blk * sub - 1:blk * sub, :]
                betas.append(beta)
                b_end = b[(blk + 1) * sub - 1:(blk + 1) * sub, :]
                decay = jnp.maximum(decay, jnp.max(beta - b_end))
            heads.append((q, k, v, b, betas, o_inter))

        @pl.when(decay < HG_SAFE_DECAY)
        def _():
            for ch, (q, k, v, b, betas, _) in enumerate(heads):
                rows = []
                for blk in range(c // sub):
                    beta = betas[blk]
                    q_t = q[blk * sub:(blk + 1) * sub, :] * jnp.exp(b[blk * sub:(blk + 1) * sub, :] - beta)
                    k_t = k * jnp.exp(jnp.minimum(beta - b, HG_SAFE_DECAY))
                    rows.append(_dot_nt(q_t.astype(BF16), k_t.astype(BF16)))
                att = jnp.where(causal, jnp.concatenate(rows, axis=0), 0.0)
                oi_ref[ch] = _dot(att.astype(BF16), v.astype(BF16))

        @pl.when(decay >= HG_SAFE_DECAY)
        def _():
            ridx = lax.broadcasted_iota(jnp.int32, (c, d), 0)
            for ch, (q, k, v, b, _, _) in enumerate(heads):
                b_ref[...] = b
                qs_ref[...] = q
                ks_ref[...] = k

                def row(t, carry2):
                    bt = b_ref[pl.ds(t, 1), :]
                    e = jnp.exp(jnp.where(ridx <= t, bt - b_ref[...], -jnp.inf))
                    w = jnp.sum(qs_ref[pl.ds(t, 1), :] * ks_ref[...] * e, axis=1, keepdims=True)
                    oi_ref[ch, pl.ds(t, 1), :] = jnp.sum(w * v, axis=0, keepdims=True)
                    return carry2

                lax.fori_loop(0, c, row, 0)

        for ch, (bb, hh) in enumerate(chains):
            cols = slice(hh * d, (hh + 1) * d)
            o = heads[ch][5] + oi_ref[ch]
            o = o * lax.rsqrt(jnp.mean(o * o, axis=-1, keepdims=True) + NORM_EPS) * gw
            g = g_ref[bb, pl.ds(r0, c), cols]
            o_ref[bb, pl.ds(r0, c), cols] = (o * (g * _sigmoid(g))).astype(o_ref.dtype)
        return carry

    lax.fori_loop(0, n_chunks, chunk, 0)


def _hgrn_call(bproj, lbs, gw, batch, seq, ts):
    t = bproj.shape[0]
    d = HG_D
    nh = HG_HEADS
    bp3 = bproj.reshape(batch, seq, bproj.shape[1])
    spec = lambda off: pl.BlockSpec((batch, ts, nh * d), lambda i: (0, i, off))
    out = pl.pallas_call(
        functools.partial(_hgrn_kernel, n_chunks=ts // HG_CHUNK),
        grid=(seq // ts,),
        in_specs=[spec(0), spec(1), spec(2), spec(3),
                  pl.BlockSpec((nh, 2, d), lambda i: (0, 0, 0)),
                  pl.BlockSpec((1, d), lambda i: (0, 0))],
        out_specs=pl.BlockSpec((batch, ts, nh * d), lambda i: (0, i, 0)),
        out_shape=jax.ShapeDtypeStruct((batch, seq, nh * d), BF16),
        scratch_shapes=[pltpu.VMEM((batch * nh, d, d), F32),
                        pltpu.VMEM((HG_CHUNK, d), F32),
                        pltpu.VMEM((HG_CHUNK, d), F32),
                        pltpu.VMEM((HG_CHUNK, d), F32),
                        pltpu.VMEM((batch * nh, HG_CHUNK, d), F32)],
        compiler_params=_cparams(("arbitrary",)),
        name="hgrn2",
    )(bp3, bp3, bp3, bp3, lbs, gw)
    return out.reshape(t, nh * d)


def _mlstm_kernel(uq_ref, uk_ref, v_ref, op_ref, gi_ref, gf_ref, cwq_ref, cwk_ref,
                  cbq_ref, cbk_ref, nw_ref, o_ref,
                  xq_ref, xk_ref, c_ref, m_ref, bs_ref):
    L = ML_CHUNK
    d = ML_D
    nh = ML_HEADS
    nb = uq_ref.shape[0]
    i = pl.program_id(0)
    rr = lax.broadcasted_iota(jnp.int32, (L, L), 0)
    cc = lax.broadcasted_iota(jnp.int32, (L, L), 1)

    @pl.when(i == 0)
    def _():
        c_ref[...] = jnp.zeros(c_ref.shape, F32)
        m_ref[...] = jnp.zeros(m_ref.shape, F32)
        xq_ref[:, 0:8, :] = jnp.zeros((nb, 8, nh * d), F32)
        xk_ref[:, 0:8, :] = jnp.zeros((nb, 8, nh * d), F32)
        upper = jnp.where(rr <= cc, 1.0, 0.0).astype(BF16)
        for ch in range(nb * nh):
            hi, mid, lo = _split3(_log_sigmoid(gf_ref[ch // nh, ch % nh]))
            bs_ref[ch] = _dot(hi, upper) + _dot(mid, upper) + _dot(lo, upper)

    def conv_silu(bb, u_ref, x_ref, w_ref, b_ref):
        x_ref[bb, 8:8 + L, :] = u_ref[bb]
        y = b_ref[...] + w_ref[ML_CONV - 1:ML_CONV, :] * x_ref[bb, 8:8 + L, :]
        for j in range(ML_CONV - 1):
            y = y + w_ref[j:j + 1, :] * x_ref[bb, 5 + j:5 + j + L, :]
        x_ref[bb, 0:8, :] = x_ref[bb, L:L + 8, :]
        return y * _sigmoid(y)

    q_all = [conv_silu(bb, uq_ref, xq_ref, cwq_ref, cbq_ref) for bb in range(nb)]
    k_all = [conv_silu(bb, uk_ref, xk_ref, cwk_ref, cbk_ref) * (d ** -0.5) for bb in range(nb)]
    lane = lax.broadcasted_iota(jnp.int32, (L, d), 1)
    ones_col = jnp.where(lane == 0, 1.0, 0.0)
    for ch in range(nb * nh):
        bb, hh = ch // nh, ch % nh
        cols = slice(hh * d, (hh + 1) * d)
        q = q_all[bb][:, cols].astype(BF16)
        kt = k_all[bb][:, cols].T
        v_aug = jnp.concatenate([v_ref[bb, :, cols], ones_col], axis=1).astype(BF16)

        b_row = bs_ref[ch, pl.ds(i, 1), :]
        ig_row = gi_ref[bb, hh, pl.ds(i, 1), :]
        m_prev = m_ref[ch]
        g = b_row[:, L - 1:L]
        b_col = jnp.sum(jnp.where(rr == cc, b_row, 0.0), axis=1, keepdims=True)
        log_d = jnp.where(cc <= rr, b_col + (ig_row - b_row), -jnp.inf)
        log_inter = b_col + m_prev
        m_t = jnp.maximum(log_inter, jnp.max(log_d, axis=1, keepdims=True))
        w_intra = jnp.exp(log_d - m_t) * _dot(q, kt.astype(BF16))
        w_inter = jnp.exp(log_inter - m_t)
        c_aug = c_ref[ch]
        tot = w_inter * _dot(q, c_aug.astype(BF16)) + _dot(w_intra.astype(BF16), v_aug)
        denom = jnp.maximum(jnp.abs(tot[:, d:d + 1]), jnp.exp(-m_t))
        hout = tot[:, :d] / denom
        hout = hout * lax.rsqrt(jnp.mean(hout * hout, axis=-1, keepdims=True) + NORM_EPS) * nw_ref[...]
        o_ref[bb, :, cols] = (hout * _sigmoid(op_ref[bb, :, cols])).astype(o_ref.dtype)

        log_w = g - b_row + ig_row
        m_new = jnp.maximum(g + m_prev, jnp.max(log_w, axis=1, keepdims=True))
        w_s = jnp.exp(log_w - m_new)
        c_ref[ch] = jnp.exp(g + m_prev - m_new) * c_aug + _dot((kt * w_s).astype(BF16), v_aug)
        m_ref[ch] = m_new


def _mlstm_call(cproj, gi, gf, conv_w, conv_b, nw, batch, seq):
    t = cproj.shape[0]
    d = ML_D
    nh = ML_HEADS
    L = ML_CHUNK
    nc = seq // L
    cp3 = cproj.reshape(batch, seq, cproj.shape[1])
    spec = lambda off: pl.BlockSpec((batch, L, nh * d), lambda i: (0, i, off))
    gspec = pl.BlockSpec((batch, nh, nc, L), lambda i: (0, 0, 0, 0))
    out = pl.pallas_call(
        _mlstm_kernel,
        grid=(nc,),
        in_specs=[spec(0), spec(1), spec(2), spec(3), gspec, gspec,
                  pl.BlockSpec((ML_CONV, nh * d), lambda i: (0, 0)),
                  pl.BlockSpec((ML_CONV, nh * d), lambda i: (0, 1)),
                  pl.BlockSpec((1, nh * d), lambda i: (0, 0)),
                  pl.BlockSpec((1, nh * d), lambda i: (0, 1)),
                  pl.BlockSpec((1, d), lambda i: (0, 0))],
        out_specs=pl.BlockSpec((batch, L, nh * d), lambda i: (0, i, 0)),
        out_shape=jax.ShapeDtypeStruct((batch, seq, nh * d), BF16),
        scratch_shapes=[pltpu.VMEM((batch, L + 8, nh * d), F32),
                        pltpu.VMEM((batch, L + 8, nh * d), F32),
                        pltpu.VMEM((batch * nh, d, 2 * d), F32),
                        pltpu.VMEM((batch * nh, 1, 1), F32),
                        pltpu.VMEM((batch * nh, nc, L), F32)],
        compiler_params=_cparams(("arbitrary",)),
        name="mlstm",
    )(cp3, cp3, cp3, cp3, gi, gf, conv_w, conv_w, conv_b, conv_b, nw)
    return out.reshape(t, nh * d)


def _top2_combine(logits):
    lane = lax.broadcasted_iota(jnp.int32, logits.shape, 1)
    lg = jnp.where(lane < N_EXPERTS, logits, -jnp.inf)
    ex = jnp.exp(lg - jnp.max(lg, axis=1, keepdims=True))
    probs = ex / jnp.sum(ex, axis=1, keepdims=True)
    p1 = jnp.max(probs, axis=1, keepdims=True)
    i1 = jnp.min(jnp.where(probs == p1, lane, LANES), axis=1, keepdims=True)
    rest = jnp.where(lane == i1, -1.0, probs)
    p2 = jnp.max(rest, axis=1, keepdims=True)
    i2 = jnp.min(jnp.where(rest == p2, lane, LANES), axis=1, keepdims=True)
    comb = jnp.where(lane == i1, p1, 0.0) + jnp.where(lane == i2, p2, 0.0)
    return comb / (p1 + p2)


def _merge_kernel(*refs, route):
    if route:
        (ya_ref, yb_ref, yc_ref, gp_ref, wb_ref, wo_ref, x_ref, mod_ref, nw_ref,
         rw_ref, rb_ref, xo_ref, h_ref, cmb_ref) = refs
    else:
        (ya_ref, yb_ref, yc_ref, gp_ref, wb_ref, wo_ref, x_ref, mod_ref, nw_ref,
         xo_ref, h_ref) = refs
    d = x_ref.shape[1]
    merged = None
    for n, y_ref in enumerate((ya_ref, yb_ref, yc_ref)):
        gate = _sigmoid(gp_ref[:, n * d:(n + 1) * d].astype(F32))
        term = gate * _dot(y_ref[...], wb_ref[n])
        merged = term if merged is None else merged + term
    m = mod_ref[0]
    xn = x_ref[...] + m[2:3] * _dot(merged.astype(BF16), wo_ref[...])
    xo_ref[...] = xn
    h2 = _norm_mod(xn, nw_ref[...], m[4:5], m[3:4])
    h_ref[...] = h2.astype(BF16)
    if route:
        h_hi, h_mid, _ = _split3(h2)
        r_hi, r_mid, _ = _split3(rw_ref[...])
        logits = _dot(h_hi, r_hi) + _dot(h_mid, r_hi) + _dot(h_hi, r_mid) + rb_ref[...]
        cmb_ref[...] = _top2_combine(logits)


def _merge_call(ya, yb, yc, gp, wb, wo, x, mod, nw, router, seq, tm):
    t, d = x.shape
    bw = ya.shape[1]
    per_b = seq // tm
    route = router is not None
    tok = lambda w: pl.BlockSpec((tm, w), lambda i: (i, 0))
    const2 = lambda s: pl.BlockSpec(s, lambda i: (0, 0))
    in_specs = [tok(bw), tok(bw), tok(bw), tok(3 * d),
                pl.BlockSpec((3, bw, d), lambda i: (0, 0, 0)), const2((d, d)), tok(d),
                pl.BlockSpec((1, 6, d), lambda i: (i // per_b, 0, 0)), const2((1, d))]
    out_specs = [tok(d), tok(d)]
    out_shape = [jax.ShapeDtypeStruct((t, d), F32), jax.ShapeDtypeStruct((t, d), BF16)]
    args = [ya, yb, yc, gp, wb, wo, x, mod, nw]
    if route:
        in_specs += [const2((d, LANES)), const2((1, LANES))]
        out_specs.append(tok(LANES))
        out_shape.append(jax.ShapeDtypeStruct((t, LANES), F32))
        args += list(router)
    return pl.pallas_call(
        functools.partial(_merge_kernel, route=route),
        grid=(t // tm,),
        in_specs=in_specs, out_specs=out_specs, out_shape=out_shape,
        compiler_params=_cparams(("arbitrary",)),
        name="merge_out",
    )(*args)


def _finish(acc, x_ref, mod_ref, xo_ref, nxt):
    xn = x_ref[...] + mod_ref[0][5:6] * acc
    xo_ref[...] = xn
    if nxt is not None:
        nw_ref, modn_ref, hn_ref = nxt
        mn = modn_ref[0]
        hn_ref[...] = _norm_mod(xn, nw_ref[...], mn[1:2], mn[0:1]).astype(BF16)


def _ffn_kernel(*refs, has_next):
    if has_next:
        h_ref, w1_ref, w3_ref, w2_ref, x_ref, mod_ref, nw_ref, modn_ref, xo_ref, hn_ref, acc_ref = refs
        nxt = (nw_ref, modn_ref, hn_ref)
    else:
        h_ref, w1_ref, w3_ref, w2_ref, x_ref, mod_ref, xo_ref, acc_ref = refs
        nxt = None
    f = pl.program_id(1)

    @pl.when(f == 0)
    def _():
        acc_ref[...] = jnp.zeros(acc_ref.shape, F32)

    h = h_ref[...]
    a = _dot(h, w1_ref[...])
    act = a * _sigmoid(a) * _dot(h, w3_ref[...])
    acc_ref[...] += _dot(act.astype(BF16), w2_ref[...])

    @pl.when(f == pl.num_programs(1) - 1)
    def _():
        _finish(acc_ref[...], x_ref, mod_ref, xo_ref, nxt)


def _ffn_call(h, w1, w3, w2, x, mod, nxt, seq, tm, tf):
    t, d = x.shape
    ff = w1.shape[1]
    per_b = seq // tm
    tok = lambda: pl.BlockSpec((tm, d), lambda i, f: (i, 0))
    modspec = lambda: pl.BlockSpec((1, 6, d), lambda i, f: (i // per_b, 0, 0))
    in_specs = [tok(), pl.BlockSpec((d, tf), lambda i, f: (0, f)),
                pl.BlockSpec((d, tf), lambda i, f: (0, f)),
                pl.BlockSpec((tf, d), lambda i, f: (f, 0)), tok(), modspec()]
    out_specs = [tok()]
    out_shape = [jax.ShapeDtypeStruct((t, d), F32)]
    args = [h, w1, w3, w2, x, mod]
    if nxt is not None:
        in_specs += [pl.BlockSpec((1, d), lambda i, f: (0, 0)), modspec()]
        out_specs.append(tok())
        out_shape.append(jax.ShapeDtypeStruct((t, d), BF16))
        args += list(nxt)
    return pl.pallas_call(
        functools.partial(_ffn_kernel, has_next=nxt is not None),
        grid=(t // tm, ff // tf),
        in_specs=in_specs, out_specs=out_specs, out_shape=out_shape,
        scratch_shapes=[pltpu.VMEM((tm, d), F32)],
        compiler_params=_cparams(("arbitrary", "arbitrary")),
        name="ffn_swiglu",
    )(*args)


def _moe_kernel(h_ref, cmb_ref, w1_ref, w3_ref, w2_ref, y_ref,
                acc_ref, xg_ref, ya_ref, rk_ref, rkt_ref, cnt_ref):
    e = pl.program_id(1)
    f = pl.program_id(2)
    tm = h_ref.shape[0]

    @pl.when((e == 0) & (f == 0))
    def _():
        r = lax.broadcasted_iota(jnp.int32, (tm, tm), 0)
        c = lax.broadcasted_iota(jnp.int32, (tm, tm), 1)
        before = jnp.where(c < r, 1.0, 0.0).astype(BF16)
        sel = cmb_ref[...] > 0.0
        rank = _dot(before, jnp.where(sel, 1.0, 0.0).astype(BF16))
        rk = jnp.where(sel, rank, -1.0)
        rk_ref[...] = rk
        rkt_ref[...] = rk.T
        cnt_ref[...] = jnp.sum(jnp.where(sel, 1.0, 0.0), axis=0, keepdims=True)
        acc_ref[...] = jnp.zeros(acc_ref.shape, F32)

    lane1 = lax.broadcasted_iota(jnp.int32, (1, LANES), 1)
    n_e = jnp.sum(jnp.where(lane1 == e, cnt_ref[...], 0.0)).astype(jnp.int32)

    def loop(n_blocks, body):
        if isinstance(n_blocks, int):
            for sb in range(n_blocks):
                body(sb, 0)
        else:
            lax.fori_loop(0, n_blocks, body, 0)

    def run(cap, n_blocks):
        cap_pad = -(-cap // LANES) * LANES

        def row0(sb):
            return sb * cap if isinstance(sb, int) else pl.multiple_of(sb * cap, cap)

        @pl.when(f == 0)
        def _():
            rank_row = rkt_ref[pl.ds(e, 1), :]

            def gather(sb, carry):
                r0 = row0(sb)
                slot = (r0 + lax.broadcasted_iota(jnp.int32, (cap, tm), 0)).astype(F32)
                onehot = jnp.where(rank_row == slot, 1.0, 0.0).astype(BF16)
                xg_ref[pl.ds(r0, cap), :] = _dot(onehot, h_ref[...]).astype(BF16)
                return carry

            loop(n_blocks, gather)

        def expert(sb, carry):
            r0 = row0(sb)
            xs = xg_ref[pl.ds(r0, cap), :]
            a = _dot(xs, w1_ref[0])
            act = a * _sigmoid(a) * _dot(xs, w3_ref[0])
            part = _dot(act.astype(BF16), w2_ref[0])

            @pl.when(f == 0)
            def _():
                ya_ref[pl.ds(r0, cap), :] = part

            @pl.when(f > 0)
            def _():
                ya_ref[pl.ds(r0, cap), :] += part

            return carry

        loop(n_blocks, expert)

        @pl.when(f == pl.num_programs(2) - 1)
        def _():
            lane = lax.broadcasted_iota(jnp.int32, (tm, LANES), 1)
            rank_col = jnp.sum(jnp.where(lane == e, rk_ref[...], 0.0), axis=1, keepdims=True)
            w_col = jnp.sum(jnp.where(lane == e, cmb_ref[...], 0.0), axis=1, keepdims=True)
            col = lax.broadcasted_iota(jnp.int32, (tm, cap_pad), 1)

            def scatter(sb, carry):
                r0 = row0(sb)
                ys = ya_ref[pl.ds(r0, cap), :].astype(BF16)
                if cap_pad > cap:
                    ys = jnp.concatenate([ys, jnp.zeros((cap_pad - cap, ys.shape[1]), BF16)], axis=0)
                hit = (rank_col == (r0 + col).astype(F32)) & (col < cap)
                acc_ref[...] += w_col * _dot(jnp.where(hit, 1.0, 0.0).astype(BF16), ys)
                return carry

            loop(n_blocks, scatter)

    for idx, cap in enumerate(MOE_CAPS):
        lower = MOE_CAPS[idx - 1] if idx else 0
        if idx == len(MOE_CAPS) - 1:
            pl.when(n_e > lower)(functools.partial(run, cap, (n_e + cap - 1) // cap))
        else:
            pl.when((n_e > lower) & (n_e <= cap))(functools.partial(run, cap, 1))

    @pl.when((e == pl.num_programs(1) - 1) & (f == pl.num_programs(2) - 1))
    def _():
        y_ref[...] = acc_ref[...].astype(y_ref.dtype)


def _moe_call(h, cmb, w1, w3, w2, tm, tf):
    t, d = h.shape
    ne, _, ff = w1.shape
    rows = -(-tm // MOE_CAPS[-1]) * MOE_CAPS[-1]
    return pl.pallas_call(
        _moe_kernel,
        grid=(t // tm, ne, ff // tf),
        in_specs=[pl.BlockSpec((tm, d), lambda i, e, f: (i, 0)),
                  pl.BlockSpec((tm, LANES), lambda i, e, f: (i, 0)),
                  pl.BlockSpec((1, d, tf), lambda i, e, f: (e, 0, f)),
                  pl.BlockSpec((1, d, tf), lambda i, e, f: (e, 0, f)),
                  pl.BlockSpec((1, tf, d), lambda i, e, f: (e, f, 0))],
        out_specs=pl.BlockSpec((tm, d), lambda i, e, f: (i, 0)),
        out_shape=jax.ShapeDtypeStruct((t, d), BF16),
        scratch_shapes=[pltpu.VMEM((tm, d), F32),
                        pltpu.VMEM((rows, d), BF16),
                        pltpu.VMEM((rows, d), F32),
                        pltpu.VMEM((tm, LANES), F32),
                        pltpu.VMEM((LANES, tm), F32),
                        pltpu.VMEM((1, LANES), F32)],
        compiler_params=_cparams(("arbitrary", "arbitrary", "arbitrary")),
        name="moe_top2",
    )(h, cmb, w1, w3, w2)


def _resid_kernel(*refs, has_next):
    if has_next:
        y_ref, x_ref, mod_ref, nw_ref, modn_ref, xo_ref, hn_ref = refs
        nxt = (nw_ref, modn_ref, hn_ref)
    else:
        y_ref, x_ref, mod_ref, xo_ref = refs
        nxt = None
    _finish(y_ref[...].astype(F32), x_ref, mod_ref, xo_ref, nxt)


def _resid_call(y, x, mod, nxt, seq, tm):
    t, d = x.shape
    per_b = seq // tm
    tok = lambda: pl.BlockSpec((tm, d), lambda i: (i, 0))
    modspec = lambda: pl.BlockSpec((1, 6, d), lambda i: (i // per_b, 0, 0))
    in_specs = [tok(), tok(), modspec()]
    out_specs = [tok()]
    out_shape = [jax.ShapeDtypeStruct((t, d), F32)]
    args = [y, x, mod]
    if nxt is not None:
        in_specs += [pl.BlockSpec((1, d), lambda i: (0, 0)), modspec()]
        out_specs.append(tok())
        out_shape.append(jax.ShapeDtypeStruct((t, d), BF16))
        args += list(nxt)
    return pl.pallas_call(
        functools.partial(_resid_kernel, has_next=nxt is not None),
        grid=(t // tm,),
        in_specs=in_specs, out_specs=out_specs, out_shape=out_shape,
        compiler_params=_cparams(("arbitrary",)),
        name="moe_residual",
    )(*args)


def _attn_col_perm():
    hq = DA_HEADS * DA_DK
    idx = []
    for base in (0, 2 * hq):
        for h in range(DA_HEADS):
            idx += list(range(base + h * DA_DK, base + (h + 1) * DA_DK))
            idx += list(range(base + hq + h * DA_DK, base + hq + (h + 1) * DA_DK))
    idx += list(range(4 * hq, 4 * hq + DA_HEADS * DA_DV))
    return np.asarray(idx, np.int32)


def _pick(n, pref):
    for c in pref:
        if n % c == 0:
            return c
    return n


def kernel(x, c, w_mod, b_mod, norm1_w, norm2_w, w_in, c_conv_w, c_conv_b, a_qnorm_w, a_knorm_w,
           a_lambda_q1, a_lambda_k1, a_lambda_q2, a_lambda_k2, a_subln_w, b_lb_logits, b_gnorm_w,
           c_igate_b, c_fgate_b, c_norm_w, w_branch, w_out, ffn_w1, ffn_w3, ffn_w2,
           moe_router_w, moe_router_b, moe_w1, moe_w3, moe_w2):
    batch, seq, d = x.shape
    depth = w_in.shape[0]
    t = batch * seq
    n_a = 4 * DA_HEADS * DA_DK + DA_HEADS * DA_DV
    n_b = 4 * HG_HEADS * HG_D
    n_c = 4 * ML_HEADS * ML_D + 2 * ML_HEADS
    n_c_pad = -(-n_c // LANES) * LANES
    tm = _pick(seq, (1024, 512, 256))
    tq = _pick(seq, (ATT_TQ,))
    ts_h = _pick(seq, (512, 256, 128))

    c8 = jnp.zeros((8, d), F32).at[:batch].set(c)
    mod = _mod_call(c8, w_mod, b_mod.reshape(depth, 1, 6 * d))[:, :batch].reshape(depth, batch, 6, d)

    lb_all = jnp.cumsum(jax.nn.softmax(b_lb_logits.astype(F32), axis=0), axis=0)
    lb_all = lb_all - lb_all[:1]
    slopes = jnp.asarray(_attn_slopes(), F32)

    xf = x.reshape(t, d)
    h = _normmod_call(xf, norm1_w[0].reshape(1, d), mod[0], seq, tm)
    perm_a = _attn_col_perm()
    for l in range(depth):
        lam_init = 0.8 - 0.6 * math.exp(-0.3 * l)
        wl = w_in[l]
        w_a = wl[:, :n_a][:, perm_a].astype(BF16)
        w_b = wl[:, n_a:n_a + n_b].astype(BF16)
        w_c = jnp.pad(wl[:, n_a + n_b:n_a + n_b + n_c], ((0, 0), (0, n_c_pad - n_c))).astype(BF16)
        w_g = wl[:, n_a + n_b + n_c:].astype(BF16)
        bias_c = jnp.zeros((1, n_c_pad), F32)
        bias_c = bias_c.at[0, n_c - 2 * ML_HEADS:n_c - ML_HEADS].set(c_igate_b[l])
        bias_c = bias_c.at[0, n_c - ML_HEADS:n_c].set(c_fgate_b[l])
        pa = _matmul(h, w_a, jnp.zeros((1, n_a), F32), BF16, tm, n_a)
        pb = _matmul(h, w_b, jnp.zeros((1, n_b), F32), F32, tm, n_b // 2)
        pc = _matmul(h, w_c, bias_c, F32, tm, n_c_pad)
        pg = _matmul(h, w_g, jnp.zeros((1, 3 * d), F32), BF16, tm, 3 * d // 2)

        qw = (jnp.tile(a_qnorm_w[l], 2 * DA_HEADS) * (DA_DK ** -0.5 * LOG2E)).reshape(1, -1)
        kw = jnp.tile(a_knorm_w[l], 2 * DA_HEADS).reshape(1, -1)
        kn, qt, vt = _attn_prep_call(pa, qw, kw, batch, seq, _pick(seq, (512, 256)))
        lam = (jnp.exp(jnp.sum(a_lambda_q1[l] * a_lambda_k1[l]))
               - jnp.exp(jnp.sum(a_lambda_q2[l] * a_lambda_k2[l])) + lam_init)
        scal = jnp.concatenate([slopes, jnp.stack([lam, jnp.asarray(1.0 - lam_init, F32)])]).astype(F32)
        y_a = _attn_call(scal, qt, kn, vt, a_subln_w[l].reshape(1, -1), batch, seq, tq)

        lb = lb_all[l].reshape(HG_HEADS, 1, HG_D)
        lbs = jnp.concatenate([jnp.log(lb), jnp.log1p(-lb)], axis=1)
        y_b = _hgrn_call(pb, lbs, b_gnorm_w[l].reshape(1, -1), batch, seq, ts_h)

        gates = pc[:, n_c - 2 * ML_HEADS:n_c].reshape(batch, seq, 2, ML_HEADS)
        gates = gates.transpose(2, 0, 3, 1).reshape(2, batch, ML_HEADS, seq // ML_CHUNK, ML_CHUNK)
        y_c = _mlstm_call(pc, gates[0], gates[1], c_conv_w[l], c_conv_b[l].reshape(1, -1),
                          c_norm_w[l].reshape(1, -1), batch, seq)

        dense = l % 2 == 0
        router = None
        if not dense:
            rw = jnp.pad(moe_router_w[l // 2], ((0, 0), (0, LANES - N_EXPERTS)))
            rb = jnp.pad(moe_router_b[l // 2], (0, LANES - N_EXPERTS)).reshape(1, LANES)
            router = (rw, rb)
        outs = _merge_call(y_a, y_b, y_c, pg, w_branch[l].astype(BF16), w_out[l].astype(BF16), xf,
                           mod[l], norm2_w[l].reshape(1, d), router, seq, 512)
        xf, h2 = outs[0], outs[1]
        nxt = None if l == depth - 1 else (norm1_w[l + 1].reshape(1, d), mod[l + 1])
        if dense:
            res = _ffn_call(h2, ffn_w1[l // 2].astype(BF16), ffn_w3[l // 2].astype(BF16),
                            ffn_w2[l // 2].astype(BF16), xf, mod[l], nxt, seq, tm, 256)
        else:
            y = _moe_call(h2, outs[2], moe_w1[l // 2].astype(BF16), moe_w3[l // 2].astype(BF16),
                          moe_w2[l // 2].astype(BF16), tm, _pick(moe_w1.shape[-1], (MOE_TF,)))
            res = _resid_call(y, xf, mod[l], nxt, seq, tm)
        xf = res[0]
        if nxt is not None:
            h = res[1]
    return xf.reshape(batch, seq, d)
```

```python
import functools
import math

import numpy as np
import jax
import jax.numpy as jnp
from jax import lax
from jax.experimental import pallas as pl
from jax.experimental.pallas import tpu as pltpu

F32 = jnp.float32
BF16 = jnp.bfloat16
NORM_EPS = 1e-6

CHUNK = 64
DA_HEADS, DA_DK, DA_DV = 4, 64, 128
HG_HEADS, HG_D = 4, 128
ML_HEADS, ML_D, ML_CONV = 4, 128, 4
N_EXPERTS = 8

LANES = 128
VMEM_LIMIT = 56 * 1024 * 1024

ATT_TQ = 256
ATT_HEADS_PER_STEP = 2
ATT_PAD_ROWS = 16
LOG2E = math.log2(math.e)
ATT_BOUND_MARGIN = 1.01
ATT_BOUND_EPS = 1e-3
ATT_BOUND_MAX = 40.0
ATT_ZERO_EXP = 160.0
HG_CHUNK = 128
HG_SUB = 32
HG_SAFE_DECAY = 80.0
ML_CHUNK = 256
MOE_CAPS = (256, 320, 384)
MOE_TF = 1408
FFN_TF = 256


def _cparams(sem, vmem=VMEM_LIMIT):
    return pltpu.CompilerParams(dimension_semantics=sem, vmem_limit_bytes=vmem)


def _dot(a, b):
    return jnp.dot(a, b, preferred_element_type=F32)


def _dot_nt(a, b):
    return lax.dot_general(a, b, (((1,), (1,)), ((), ())), preferred_element_type=F32)


def _split3(x):
    hi = x.astype(BF16)
    r1 = x - hi.astype(F32)
    mid = r1.astype(BF16)
    lo = (r1 - mid.astype(F32)).astype(BF16)
    return hi, mid, lo


def _norm_mod(x, w, scale, shift):
    y = x * lax.rsqrt(jnp.mean(x * x, axis=-1, keepdims=True) + NORM_EPS) * w
    return y * (1.0 + scale) + shift


def _sigmoid(x):
    return 0.5 * jnp.tanh(0.5 * x) + 0.5


def _log1p_exp_neg(z):
    return jnp.log(1.0 + jnp.exp(-z))


def _log_sigmoid(x):
    return jnp.minimum(x, 0.0) - _log1p_exp_neg(jnp.abs(x))


def _mod_kernel(c_ref, w_ref, b_ref, o_ref):
    cnd = c_ref[...]
    cnd = cnd * _sigmoid(cnd)
    o_ref[0] = _dot(cnd.astype(BF16), w_ref[0].astype(BF16)) + b_ref[0]


def _mod_call(c8, w_mod, b_mod):
    depth, d, n = w_mod.shape
    tn = 1536 if n % 1536 == 0 else n
    return pl.pallas_call(
        _mod_kernel,
        grid=(depth, n // tn),
        in_specs=[pl.BlockSpec((8, d), lambda l, j: (0, 0)),
                  pl.BlockSpec((1, d, tn), lambda l, j: (l, 0, j)),
                  pl.BlockSpec((1, 1, tn), lambda l, j: (l, 0, j))],
        out_specs=pl.BlockSpec((1, 8, tn), lambda l, j: (l, 0, j)),
        out_shape=jax.ShapeDtypeStruct((depth, 8, n), F32),
        compiler_params=_cparams(("arbitrary", "arbitrary")),
        name="adaln_mod",
    )(c8, w_mod, b_mod)


def _normmod_kernel(x_ref, nw_ref, mod_ref, h_ref):
    m = mod_ref[0]
    h_ref[...] = _norm_mod(x_ref[...], nw_ref[...], m[1:2], m[0:1]).astype(BF16)


def _normmod_call(x, nw, mod, seq, tm):
    t, d = x.shape
    per_b = seq // tm
    return pl.pallas_call(
        _normmod_kernel,
        grid=(t // tm,),
        in_specs=[pl.BlockSpec((tm, d), lambda i: (i, 0)),
                  pl.BlockSpec((1, d), lambda i: (0, 0)),
                  pl.BlockSpec((1, 6, d), lambda i: (i // per_b, 0, 0))],
        out_specs=pl.BlockSpec((tm, d), lambda i: (i, 0)),
        out_shape=jax.ShapeDtypeStruct((t, d), BF16),
        compiler_params=_cparams(("arbitrary",)),
        name="prenorm_mod",
    )(x, nw, mod)


def _mm_kernel(x_ref, w_ref, b_ref, o_ref):
    o_ref[...] = (_dot(x_ref[...], w_ref[...]) + b_ref[...]).astype(o_ref.dtype)


def _matmul(x, w, bias, out_dtype, tm, tn):
    t, k = x.shape
    n = w.shape[1]
    return pl.pallas_call(
        _mm_kernel,
        grid=(n // tn, t // tm),
        in_specs=[pl.BlockSpec((tm, k), lambda j, i: (i, 0)),
                  pl.BlockSpec((k, tn), lambda j, i: (0, j)),
                  pl.BlockSpec((1, tn), lambda j, i: (0, j))],
        out_specs=pl.BlockSpec((tm, tn), lambda j, i: (i, j)),
        out_shape=jax.ShapeDtypeStruct((t, n), out_dtype),
        compiler_params=_cparams(("arbitrary", "arbitrary")),
        name="in_proj",
    )(x, w, bias)


def _attn_prep_kernel(a_ref, qw_ref, kw_ref, kn_ref, qt_ref, vt_ref):
    hw = DA_HEADS * 2 * DA_DK
    a = a_ref[...]
    r = lax.broadcasted_iota(jnp.int32, (hw, hw), 0) // DA_DK
    c = lax.broadcasted_iota(jnp.int32, (hw, hw), 1) // DA_DK
    group = jnp.where(r == c, 1.0, 0.0).astype(BF16)

    def qk_norm(z, w):
        ms = _dot((z * z).astype(BF16), group) * (1.0 / DA_DK)
        return z * lax.rsqrt(ms + NORM_EPS) * w

    qn = qk_norm(a[:, :hw].astype(F32), qw_ref[...])
    kn = qk_norm(a[:, hw:2 * hw].astype(F32), kw_ref[...])
    kn_ref[...] = kn.astype(BF16)
    qt_ref[0] = qn.T.astype(BF16)
    vt_ref[0] = a[:, 2 * hw:].astype(F32).T.astype(BF16)


def _attn_prep_call(a, qw, kw, batch, seq, ts):
    t = a.shape[0]
    hw = DA_HEADS * 2 * DA_DK
    hv = DA_HEADS * DA_DV
    per_b = seq // ts
    return pl.pallas_call(
        _attn_prep_kernel,
        grid=(t // ts,),
        in_specs=[pl.BlockSpec((ts, 2 * hw + hv), lambda i: (i, 0)),
                  pl.BlockSpec((1, hw), lambda i: (0, 0)),
                  pl.BlockSpec((1, hw), lambda i: (0, 0))],
        out_specs=[pl.BlockSpec((ts, hw), lambda i: (i, 0)),
                   pl.BlockSpec((1, hw, ts), lambda i: (i // per_b, 0, i % per_b)),
                   pl.BlockSpec((1, hv, ts), lambda i: (i // per_b, 0, i % per_b))],
        out_shape=[jax.ShapeDtypeStruct((t, hw), BF16),
                   jax.ShapeDtypeStruct((batch, hw, seq), BF16),
                   jax.ShapeDtypeStruct((batch, hv, seq), BF16)],
        compiler_params=_cparams(("arbitrary",)),
        name="attn_prep",
    )(a, qw, kw)


def _attn_slopes():
    return [LOG2E * 2.0 ** (-8.0 * (h + 1) / DA_HEADS) for h in range(DA_HEADS)]


def _attn_window(head, tq):
    return int((ATT_ZERO_EXP / _attn_slopes()[head] - 1.0) // tq) + 1


def _attn_kernel(sc_ref, qt_ref, k_ref, vt_ref, sw_ref, o_ref,
                 qz_ref, m_ref, acc_ref, vs_ref, kmax_ref, s_ref, s2_ref, *, tq, seq):
    hp = ATT_HEADS_PER_STEP
    dv = DA_DV
    g = pl.program_id(1)
    i = pl.program_id(2)
    lam = sc_ref[DA_HEADS]
    out_scale = sc_ref[DA_HEADS + 1]
    slopes = [sc_ref[hp * g + hh] for hh in range(hp)]
    row16 = lax.broadcasted_iota(jnp.int32, (ATT_PAD_ROWS, tq), 0)
    ones_rows = jnp.where(row16 == 0, 1.0, 0.0).astype(BF16)

    half = lax.broadcasted_iota(jnp.int32, (1, 2 * DA_DK), 1) < DA_DK

    @pl.when(i == 0)
    def _():
        pos = lax.broadcasted_iota(jnp.int32, (1, seq), 1) & (tq - 1)
        rel = (pos - (tq - 1)).astype(F32)
        sub16 = lax.broadcasted_iota(jnp.int32, (ATT_PAD_ROWS, seq), 0)
        gr = lax.broadcasted_iota(jnp.int32, (2 * DA_DK, 2 * DA_DK), 0) // DA_DK
        gc = lax.broadcasted_iota(jnp.int32, (2 * DA_DK, 2 * DA_DK), 1) // DA_DK
        group = jnp.where(gr == gc, 1.0, 0.0).astype(BF16)
        for hh in range(hp):
            w = jnp.exp2(slopes[hh] * rel)
            vs_ref[hh, 0:dv, :] = (vt_ref[0, hh * dv:(hh + 1) * dv, :].astype(F32) * w).astype(BF16)
            vs_ref[hh, dv:dv + ATT_PAD_ROWS, :] = jnp.where(sub16 == 0, w, 0.0).astype(BF16)

            def knorm(n, best):
                kc = k_ref[pl.ds(pl.multiple_of(n * tq, tq), tq),
                           hh * 2 * DA_DK:(hh + 1) * 2 * DA_DK].astype(F32)
                return jnp.maximum(best, jnp.max(_dot((kc * kc).astype(BF16), group), axis=0, keepdims=True))

            k2 = lax.fori_loop(0, seq // tq, knorm, jnp.zeros((1, 2 * DA_DK), F32))
            kmax_ref[2 * hh] = jnp.max(jnp.where(half, k2, 0.0), axis=1, keepdims=True)
            kmax_ref[2 * hh + 1] = jnp.max(jnp.where(half, 0.0, k2), axis=1, keepdims=True)

    row = lax.broadcasted_iota(jnp.int32, (2 * DA_DK, tq), 0)
    qq_row = lax.broadcasted_iota(jnp.int32, (1, tq), 1).astype(F32)
    bound_max = jnp.zeros((), F32)
    firsts = []

    def scores(hh, blk):
        start = pl.multiple_of(blk * tq, tq)
        return _dot(k_ref[pl.ds(start, tq), hh * 2 * DA_DK:(hh + 1) * 2 * DA_DK], qz_ref[hh])

    for hh in range(hp):
        qt = qt_ref[0, hh * 2 * DA_DK:(hh + 1) * 2 * DA_DK, :]
        zero = jnp.zeros_like(qt)
        q1 = jnp.where(row < DA_DK, qt, zero)
        q2 = jnp.where(row >= DA_DK, qt, zero)
        qz_ref[hh] = jnp.concatenate([q1, q2], axis=1)
        window = jnp.int32(_attn_window(hh, tq))
        for grp in range(1, DA_HEADS // hp):
            window = jnp.where(g == grp, _attn_window(grp * hp + hh, tq), window)
        firsts.append((jnp.maximum(i - window, 0) // 2) * 2)
        s_ref[hh] = scores(hh, firsts[hh])
        s2_ref[hh] = scores(hh, jnp.minimum(firsts[hh] + 1, i))
        qn = jnp.concatenate(
            [jnp.sum(jnp.square(q1.astype(F32)), axis=0, keepdims=True) * kmax_ref[2 * hh],
             jnp.sum(jnp.square(q2.astype(F32)), axis=0, keepdims=True) * kmax_ref[2 * hh + 1]], axis=1)
        bound = jnp.sqrt(qn) * ATT_BOUND_MARGIN + ATT_BOUND_EPS
        bound_max = jnp.maximum(bound_max, jnp.max(bound))
        m_ref[hh] = bound + slopes[hh] * jnp.concatenate([qq_row, qq_row], axis=1)
    acc_ref[...] = jnp.zeros(acc_ref.shape, F32)

    kk = lax.broadcasted_iota(jnp.int32, (tq, 2 * tq), 0)
    cc = lax.broadcasted_iota(jnp.int32, (tq, 2 * tq), 1)
    qq = jnp.where(cc >= tq, cc - tq, cc)
    dist = (qq - jnp.abs(qq - kk)).astype(F32)
    visible = (kk // CHUNK) <= (qq // CHUNK)

    def diag_values(hh):
        start = pl.multiple_of(i * tq, tq)
        return jnp.concatenate([vt_ref[0, hh * dv:(hh + 1) * dv, pl.ds(start, tq)], ones_rows], axis=0)

    @pl.when(bound_max <= ATT_BOUND_MAX)
    def _():
        def probs(hh, blk, s):
            last_key = ((blk + 1 - i) * tq - 1).astype(F32)
            return jnp.exp2(s - (m_ref[hh] - slopes[hh] * last_key)).astype(BF16)

        def body(heads, jj, carry):
            blk = 2 * jj
            start = pl.multiple_of(blk * tq, 2 * tq)
            for hh in heads:
                p = jnp.concatenate([probs(hh, blk, s_ref[hh]), probs(hh, blk + 1, s2_ref[hh])], axis=0)
                s_ref[hh] = scores(hh, jnp.minimum(blk + 2, i))
                s2_ref[hh] = scores(hh, jnp.minimum(blk + 3, i))
                acc_ref[hh] += _dot(vs_ref[hh, :, pl.ds(start, 2 * tq)], p)
            return carry

        for hh in range(hp - 1, 0, -1):
            lax.fori_loop(firsts[hh] // 2, firsts[hh - 1] // 2,
                          functools.partial(body, tuple(range(hh, hp))), 0)
        lax.fori_loop(firsts[0] // 2, i // 2, functools.partial(body, tuple(range(hp))), 0)

        @pl.when(i % 2 == 1)
        def _():
            start = pl.multiple_of((i - 1) * tq, tq)
            for hh in range(hp):
                acc_ref[hh] += _dot(vs_ref[hh, :, pl.ds(start, tq)], probs(hh, i - 1, s_ref[hh]))
                s_ref[hh] = s2_ref[hh]

        for hh in range(hp):
            s = jnp.where(visible, s_ref[hh] + slopes[hh] * dist, -jnp.inf)
            acc_ref[hh] += _dot(diag_values(hh), jnp.exp2(s - m_ref[hh]).astype(BF16))

    @pl.when(bound_max > ATT_BOUND_MAX)
    def _():
        m_ref[...] = jnp.full(m_ref.shape, -jnp.inf, F32)

        def update(hh, s, bound_shift, v_aug):
            m_old = m_ref[hh]
            m_new = jnp.maximum(m_old, jnp.max(s, axis=0, keepdims=True) + bound_shift)
            p = jnp.exp2(s - (m_new - bound_shift))
            acc_ref[hh] = jnp.exp2(m_old - m_new) * acc_ref[hh] + _dot(v_aug, p.astype(BF16))
            m_ref[hh] = m_new

        def body(j, carry):
            start = pl.multiple_of(j * tq, tq)
            last_key = ((j + 1 - i) * tq - 1).astype(F32)
            for hh in range(hp):
                update(hh, scores(hh, j), slopes[hh] * last_key, vs_ref[hh, :, pl.ds(start, tq)])
            return carry

        lax.fori_loop(0, i, body, 0)
        for hh in range(hp):
            s = jnp.where(visible, scores(hh, i) + slopes[hh] * dist, -jnp.inf)
            update(hh, s, jnp.zeros((), F32), diag_values(hh))

    outs = []
    for hh in range(hp):
        acc = acc_ref[hh]
        o2 = acc[:dv, :] * (1.0 / acc[dv:dv + 1, :])
        o = (o2[:, :tq] - lam * o2[:, tq:]).T
        o = o * lax.rsqrt(jnp.mean(o * o, axis=-1, keepdims=True) + NORM_EPS) * sw_ref[...]
        outs.append(o * out_scale)
    o_ref[...] = jnp.concatenate(outs, axis=1).astype(o_ref.dtype)


def _attn_call(scal, qt, kn, vt, sw, batch, seq, tq):
    t = kn.shape[0]
    nq = seq // tq
    hp = ATT_HEADS_PER_STEP
    return pl.pallas_call(
        functools.partial(_attn_kernel, tq=tq, seq=seq),
        grid=(batch, DA_HEADS // hp, nq),
        in_specs=[pl.BlockSpec(memory_space=pltpu.SMEM),
                  pl.BlockSpec((1, hp * 2 * DA_DK, tq), lambda b, g, i: (b, g, i)),
                  pl.BlockSpec((seq, hp * 2 * DA_DK), lambda b, g, i: (b, g)),
                  pl.BlockSpec((1, hp * DA_DV, seq), lambda b, g, i: (b, g, 0)),
                  pl.BlockSpec((1, DA_DV), lambda b, g, i: (0, 0))],
        out_specs=pl.BlockSpec((tq, hp * DA_DV), lambda b, g, i: (b * nq + i, g)),
        out_shape=jax.ShapeDtypeStruct((t, DA_HEADS * DA_DV), BF16),
        scratch_shapes=[pltpu.VMEM((hp, 2 * DA_DK, 2 * tq), BF16),
                        pltpu.VMEM((hp, 1, 2 * tq), F32),
                        pltpu.VMEM((hp, DA_DV + ATT_PAD_ROWS, 2 * tq), F32),
                        pltpu.VMEM((hp, DA_DV + ATT_PAD_ROWS, seq), BF16),
                        pltpu.VMEM((2 * hp, 1, 1), F32),
                        pltpu.VMEM((hp, tq, 2 * tq), F32),
                        pltpu.VMEM((hp, tq, 2 * tq), F32)],
        compiler_params=_cparams(("arbitrary", "arbitrary", "arbitrary")),
        name="diff_attn",
    )(scal, qt, kn, vt, sw)


def _hgrn_kernel(q_ref, f_ref, i_ref, g_ref, lb_ref, gw_ref, o_ref,
                 st_ref, b_ref, qs_ref, ks_ref, oi_ref, *, n_chunks):
    c = HG_CHUNK
    sub = HG_SUB
    d = HG_D
    nh = HG_HEADS
    chains = [(bb, hh) for bb in range(q_ref.shape[0]) for hh in range(nh)]

    @pl.when(pl.program_id(0) == 0)
    def _():
        st_ref[...] = jnp.zeros(st_ref.shape, F32)

    gw = gw_ref[...]
    rr = lax.broadcasted_iota(jnp.int32, (c, c), 0)
    cc = lax.broadcasted_iota(jnp.int32, (c, c), 1)
    causal = cc <= rr
    tril = jnp.where(causal, 1.0, 0.0).astype(BF16)

    def chunk(n, carry):
        r0 = pl.multiple_of(n * c, c)
        gates = []
        for ch, (bb, hh) in enumerate(chains):
            cols = slice(hh * d, (hh + 1) * d)
            log_lb = lb_ref[hh, 0:1, :]
            q = q_ref[bb, pl.ds(r0, c), cols]
            q = q * _sigmoid(q)
            a = lb_ref[hh, 1:2, :] + _log_sigmoid(f_ref[bb, pl.ds(r0, c), cols])
            logf = jnp.maximum(log_lb, a) + _log1p_exp_neg(jnp.abs(log_lb - a))
            k = 1.0 - jnp.exp(logf)
            v = i_ref[bb, pl.ds(r0, c), cols]
            hi, mid, lo = _split3(logf)
            b = _dot(tril, hi) + _dot(tril, mid) + _dot(tril, lo)
            gates.append((q, k, v, b))

        heads = []
        decay = jnp.zeros((), F32)
        for ch, (q, k, v, b) in enumerate(gates):
            b_last = b[c - 1:c, :]
            st = st_ref[ch]
            o_inter = _dot_nt((q * jnp.exp(b)).astype(BF16), st.astype(BF16))
            k_hat = k * jnp.exp(b_last - b)
            st_ref[ch] = st * jnp.exp(b_last) + _dot(v.T.astype(BF16), k_hat.astype(BF16))
            betas = []
            for blk in range(c // sub):
                beta = jnp.zeros((1, d), F32) if blk == 0 else b[blk * sub - 1:blk * sub, :]
                betas.append(beta)
                b_end = b[(blk + 1) * sub - 1:(blk + 1) * sub, :]
                decay = jnp.maximum(decay, jnp.max(beta - b_end))
            heads.append((q, k, v, b, betas, o_inter))

        @pl.when(decay < HG_SAFE_DECAY)
        def _():
            atts = []
            for q, k, v, b, betas, _ in heads:
                rows = []
                for blk in range(c // sub):
                    beta = betas[blk]
                    q_t = q[blk * sub:(blk + 1) * sub, :] * jnp.exp(b[blk * sub:(blk + 1) * sub, :] - beta)
                    k_t = k * jnp.exp(jnp.minimum(beta - b, HG_SAFE_DECAY))
                    rows.append(_dot_nt(q_t.astype(BF16), k_t.astype(BF16)))
                atts.append(jnp.where(causal, jnp.concatenate(rows, axis=0), 0.0).astype(BF16))
            for ch, att in enumerate(atts):
                oi_ref[ch] = _dot(att, heads[ch][2].astype(BF16))

        @pl.when(decay >= HG_SAFE_DECAY)
        def _():
            ridx = lax.broadcasted_iota(jnp.int32, (c, d), 0)
            for ch, (q, k, v, b, _, _) in enumerate(heads):
                b_ref[...] = b
                qs_ref[...] = q
                ks_ref[...] = k

                def row(t, carry2):
                    bt = b_ref[pl.ds(t, 1), :]
                    e = jnp.exp(jnp.where(ridx <= t, bt - b_ref[...], -jnp.inf))
                    w = jnp.sum(qs_ref[pl.ds(t, 1), :] * ks_ref[...] * e, axis=1, keepdims=True)
                    oi_ref[ch, pl.ds(t, 1), :] = jnp.sum(w * v, axis=0, keepdims=True)
                    return carry2

                lax.fori_loop(0, c, row, 0)

        for ch, (bb, hh) in enumerate(chains):
            cols = slice(hh * d, (hh + 1) * d)
            o = heads[ch][5] + oi_ref[ch]
            o = o * lax.rsqrt(jnp.mean(o * o, axis=-1, keepdims=True) + NORM_EPS) * gw
            g = g_ref[bb, pl.ds(r0, c), cols]
            o_ref[bb, pl.ds(r0, c), cols] = (o * (g * _sigmoid(g))).astype(o_ref.dtype)
        return carry

    lax.fori_loop(0, n_chunks, chunk, 0)


def _hgrn_call(bproj, lbs, gw, batch, seq, ts):
    t = bproj.shape[0]
    d = HG_D
    nh = HG_HEADS
    bp3 = bproj.reshape(batch, seq, bproj.shape[1])
    spec = lambda off: pl.BlockSpec((batch, ts, nh * d), lambda i: (0, i, off))
    out = pl.pallas_call(
        functools.partial(_hgrn_kernel, n_chunks=ts // HG_CHUNK),
        grid=(seq // ts,),
        in_specs=[spec(0), spec(1), spec(2), spec(3),
                  pl.BlockSpec((nh, 2, d), lambda i: (0, 0, 0)),
                  pl.BlockSpec((1, d), lambda i: (0, 0))],
        out_specs=pl.BlockSpec((batch, ts, nh * d), lambda i: (0, i, 0)),
        out_shape=jax.ShapeDtypeStruct((batch, seq, nh * d), BF16),
        scratch_shapes=[pltpu.VMEM((batch * nh, d, d), F32),
                        pltpu.VMEM((HG_CHUNK, d), F32),
                        pltpu.VMEM((HG_CHUNK, d), F32),
                        pltpu.VMEM((HG_CHUNK, d), F32),
                        pltpu.VMEM((batch * nh, HG_CHUNK, d), F32)],
        compiler_params=_cparams(("arbitrary",)),
        name="hgrn2",
    )(bp3, bp3, bp3, bp3, lbs, gw)
    return out.reshape(t, nh * d)


def _mlstm_kernel(uq_ref, uk_ref, v_ref, op_ref, gi_ref, gf_ref, cwq_ref, cwk_ref,
                  cbq_ref, cbk_ref, nw_ref, o_ref,
                  xq_ref, xk_ref, c_ref, m_ref, bs_ref):
    L = ML_CHUNK
    d = ML_D
    nh = ML_HEADS
    nb = uq_ref.shape[0]
    i = pl.program_id(0)
    rr = lax.broadcasted_iota(jnp.int32, (L, L), 0)
    cc = lax.broadcasted_iota(jnp.int32, (L, L), 1)

    @pl.when(i == 0)
    def _():
        c_ref[...] = jnp.zeros(c_ref.shape, F32)
        m_ref[...] = jnp.zeros(m_ref.shape, F32)
        xq_ref[:, 0:8, :] = jnp.zeros((nb, 8, nh * d), F32)
        xk_ref[:, 0:8, :] = jnp.zeros((nb, 8, nh * d), F32)
        upper = jnp.where(rr <= cc, 1.0, 0.0).astype(BF16)
        for ch in range(nb * nh):
            hi, mid, lo = _split3(_log_sigmoid(gf_ref[ch // nh, ch % nh]))
            bs_ref[ch] = _dot(hi, upper) + _dot(mid, upper) + _dot(lo, upper)

    def conv_silu(bb, u_ref, x_ref, w_ref, b_ref):
        x_ref[bb, 8:8 + L, :] = u_ref[bb]
        y = b_ref[...] + w_ref[ML_CONV - 1:ML_CONV, :] * x_ref[bb, 8:8 + L, :]
        for j in range(ML_CONV - 1):
            y = y + w_ref[j:j + 1, :] * x_ref[bb, 5 + j:5 + j + L, :]
        x_ref[bb, 0:8, :] = x_ref[bb, L:L + 8, :]
        return y * _sigmoid(y)

    q_all = [conv_silu(bb, uq_ref, xq_ref, cwq_ref, cbq_ref) for bb in range(nb)]
    k_all = [conv_silu(bb, uk_ref, xk_ref, cwk_ref, cbk_ref) * (d ** -0.5) for bb in range(nb)]
    lane = lax.broadcasted_iota(jnp.int32, (L, d), 1)
    ones_col = jnp.where(lane == 0, 1.0, 0.0)
    stage1 = []
    for ch in range(nb * nh):
        bb, hh = ch // nh, ch % nh
        cols = slice(hh * d, (hh + 1) * d)
        q = q_all[bb][:, cols].astype(BF16)
        kt = k_all[bb][:, cols].T
        v_aug = jnp.concatenate([v_ref[bb, :, cols], ones_col], axis=1).astype(BF16)
        c_aug = c_ref[ch]
        qk = _dot(q, kt.astype(BF16))
        qc = _dot(q, c_aug.astype(BF16))

        b_row = bs_ref[ch, pl.ds(i, 1), :]
        ig_row = gi_ref[bb, hh, pl.ds(i, 1), :]
        m_prev = m_ref[ch]
        g = b_row[:, L - 1:L]
        log_w = g - b_row + ig_row
        m_new = jnp.maximum(g + m_prev, jnp.max(log_w, axis=1, keepdims=True))
        w_s = jnp.exp(log_w - m_new)
        c_ref[ch] = jnp.exp(g + m_prev - m_new) * c_aug + _dot((kt * w_s).astype(BF16), v_aug)
        m_ref[ch] = m_new
        stage1.append((qk, qc, v_aug, b_row, ig_row, m_prev))

    stage2 = []
    for qk, qc, v_aug, b_row, ig_row, m_prev in stage1:
        b_col = jnp.sum(jnp.where(rr == cc, b_row, 0.0), axis=1, keepdims=True)
        log_d = jnp.where(cc <= rr, b_col + (ig_row - b_row), -jnp.inf)
        log_inter = b_col + m_prev
        m_t = jnp.maximum(log_inter, jnp.max(log_d, axis=1, keepdims=True))
        w_intra = jnp.exp(log_d - m_t) * qk
        tot = jnp.exp(log_inter - m_t) * qc + _dot(w_intra.astype(BF16), v_aug)
        stage2.append((tot, m_t))

    for ch, (tot, m_t) in enumerate(stage2):
        bb, hh = ch // nh, ch % nh
        cols = slice(hh * d, (hh + 1) * d)
        denom = jnp.maximum(jnp.abs(tot[:, d:d + 1]), jnp.exp(-m_t))
        hout = tot[:, :d] / denom
        hout = hout * lax.rsqrt(jnp.mean(hout * hout, axis=-1, keepdims=True) + NORM_EPS) * nw_ref[...]
        o_ref[bb, :, cols] = (hout * _sigmoid(op_ref[bb, :, cols])).astype(o_ref.dtype)


def _mlstm_call(cproj, gi, gf, conv_w, conv_b, nw, batch, seq):
    t = cproj.shape[0]
    d = ML_D
    nh = ML_HEADS
    L = ML_CHUNK
    nc = seq // L
    cp3 = cproj.reshape(batch, seq, cproj.shape[1])
    spec = lambda off: pl.BlockSpec((batch, L, nh * d), lambda i: (0, i, off))
    gspec = pl.BlockSpec((batch, nh, nc, L), lambda i: (0, 0, 0, 0))
    out = pl.pallas_call(
        _mlstm_kernel,
        grid=(nc,),
        in_specs=[spec(0), spec(1), spec(2), spec(3), gspec, gspec,
                  pl.BlockSpec((ML_CONV, nh * d), lambda i: (0, 0)),
                  pl.BlockSpec((ML_CONV, nh * d), lambda i: (0, 1)),
                  pl.BlockSpec((1, nh * d), lambda i: (0, 0)),
                  pl.BlockSpec((1, nh * d), lambda i: (0, 1)),
                  pl.BlockSpec((1, d), lambda i: (0, 0))],
        out_specs=pl.BlockSpec((batch, L, nh * d), lambda i: (0, i, 0)),
        out_shape=jax.ShapeDtypeStruct((batch, seq, nh * d), BF16),
        scratch_shapes=[pltpu.VMEM((batch, L + 8, nh * d), F32),
                        pltpu.VMEM((batch, L + 8, nh * d), F32),
                        pltpu.VMEM((batch * nh, d, 2 * d), F32),
                        pltpu.VMEM((batch * nh, 1, 1), F32),
                        pltpu.VMEM((batch * nh, nc, L), F32)],
        compiler_params=_cparams(("arbitrary",)),
        name="mlstm",
    )(cp3, cp3, cp3, cp3, gi, gf, conv_w, conv_w, conv_b, conv_b, nw)
    return out.reshape(t, nh * d)


def _top2_combine(logits):
    lane = lax.broadcasted_iota(jnp.int32, logits.shape, 1)
    lg = jnp.where(lane < N_EXPERTS, logits, -jnp.inf)
    ex = jnp.exp(lg - jnp.max(lg, axis=1, keepdims=True))
    probs = ex / jnp.sum(ex, axis=1, keepdims=True)
    p1 = jnp.max(probs, axis=1, keepdims=True)
    i1 = jnp.min(jnp.where(probs == p1, lane, LANES), axis=1, keepdims=True)
    rest = jnp.where(lane == i1, -1.0, probs)
    p2 = jnp.max(rest, axis=1, keepdims=True)
    i2 = jnp.min(jnp.where(rest == p2, lane, LANES), axis=1, keepdims=True)
    comb = jnp.where(lane == i1, p1, 0.0) + jnp.where(lane == i2, p2, 0.0)
    return comb / (p1 + p2)


def _merge_kernel(*refs, route):
    if route:
        (ya_ref, yb_ref, yc_ref, gp_ref, wb_ref, wo_ref, x_ref, mod_ref, nw_ref,
         rw_ref, rb_ref, xo_ref, h_ref, cmb_ref) = refs
    else:
        (ya_ref, yb_ref, yc_ref, gp_ref, wb_ref, wo_ref, x_ref, mod_ref, nw_ref,
         xo_ref, h_ref) = refs
    d = x_ref.shape[1]
    merged = None
    for n, y_ref in enumerate((ya_ref, yb_ref, yc_ref)):
        gate = _sigmoid(gp_ref[:, n * d:(n + 1) * d].astype(F32))
        term = gate * _dot(y_ref[...], wb_ref[n])
        merged = term if merged is None else merged + term
    m = mod_ref[0]
    xn = x_ref[...] + m[2:3] * _dot(merged.astype(BF16), wo_ref[...])
    xo_ref[...] = xn
    h2 = _norm_mod(xn, nw_ref[...], m[4:5], m[3:4])
    h_ref[...] = h2.astype(BF16)
    if route:
        h_hi, h_mid, _ = _split3(h2)
        r_hi, r_mid, _ = _split3(rw_ref[...])
        logits = _dot(h_hi, r_hi) + _dot(h_mid, r_hi) + _dot(h_hi, r_mid) + rb_ref[...]
        cmb_ref[...] = _top2_combine(logits)


def _merge_call(ya, yb, yc, gp, wb, wo, x, mod, nw, router, seq, tm):
    t, d = x.shape
    bw = ya.shape[1]
    per_b = seq // tm
    route = router is not None
    tok = lambda w: pl.BlockSpec((tm, w), lambda i: (i, 0))
    const2 = lambda s: pl.BlockSpec(s, lambda i: (0, 0))
    in_specs = [tok(bw), tok(bw), tok(bw), tok(3 * d),
                pl.BlockSpec((3, bw, d), lambda i: (0, 0, 0)), const2((d, d)), tok(d),
                pl.BlockSpec((1, 6, d), lambda i: (i // per_b, 0, 0)), const2((1, d))]
    out_specs = [tok(d), tok(d)]
    out_shape = [jax.ShapeDtypeStruct((t, d), F32), jax.ShapeDtypeStruct((t, d), BF16)]
    args = [ya, yb, yc, gp, wb, wo, x, mod, nw]
    if route:
        in_specs += [const2((d, LANES)), const2((1, LANES))]
        out_specs.append(tok(LANES))
        out_shape.append(jax.ShapeDtypeStruct((t, LANES), F32))
        args += list(router)
    return pl.pallas_call(
        functools.partial(_merge_kernel, route=route),
        grid=(t // tm,),
        in_specs=in_specs, out_specs=out_specs, out_shape=out_shape,
        compiler_params=_cparams(("arbitrary",)),
        name="merge_out",
    )(*args)


def _finish(acc, x_ref, mod_ref, xo_ref, nxt):
    xn = x_ref[...] + mod_ref[0][5:6] * acc
    xo_ref[...] = xn
    if nxt is not None:
        nw_ref, modn_ref, hn_ref = nxt
        mn = modn_ref[0]
        hn_ref[...] = _norm_mod(xn, nw_ref[...], mn[1:2], mn[0:1]).astype(BF16)


def _ffn_kernel(*refs, has_next):
    if has_next:
        h_ref, w1_ref, w3_ref, w2_ref, x_ref, mod_ref, nw_ref, modn_ref, xo_ref, hn_ref, acc_ref = refs
        nxt = (nw_ref, modn_ref, hn_ref)
    else:
        h_ref, w1_ref, w3_ref, w2_ref, x_ref, mod_ref, xo_ref, acc_ref = refs
        nxt = None
    f = pl.program_id(1)

    @pl.when(f == 0)
    def _():
        acc_ref[...] = jnp.zeros(acc_ref.shape, F32)

    h = h_ref[...]
    a = _dot(h, w1_ref[...])
    act = a * _sigmoid(a) * _dot(h, w3_ref[...])
    acc_ref[...] += _dot(act.astype(BF16), w2_ref[...])

    @pl.when(f == pl.num_programs(1) - 1)
    def _():
        _finish(acc_ref[...], x_ref, mod_ref, xo_ref, nxt)


def _ffn_call(h, w1, w3, w2, x, mod, nxt, seq, tm, tf):
    t, d = x.shape
    ff = w1.shape[1]
    per_b = seq // tm
    tok = lambda: pl.BlockSpec((tm, d), lambda i, f: (i, 0))
    modspec = lambda: pl.BlockSpec((1, 6, d), lambda i, f: (i // per_b, 0, 0))
    in_specs = [tok(), pl.BlockSpec((d, tf), lambda i, f: (0, f)),
                pl.BlockSpec((d, tf), lambda i, f: (0, f)),
                pl.BlockSpec((tf, d), lambda i, f: (f, 0)), tok(), modspec()]
    out_specs = [tok()]
    out_shape = [jax.ShapeDtypeStruct((t, d), F32)]
    args = [h, w1, w3, w2, x, mod]
    if nxt is not None:
        in_specs += [pl.BlockSpec((1, d), lambda i, f: (0, 0)), modspec()]
        out_specs.append(tok())
        out_shape.append(jax.ShapeDtypeStruct((t, d), BF16))
        args += list(nxt)
    return pl.pallas_call(
        functools.partial(_ffn_kernel, has_next=nxt is not None),
        grid=(t // tm, ff // tf),
        in_specs=in_specs, out_specs=out_specs, out_shape=out_shape,
        scratch_shapes=[pltpu.VMEM((tm, d), F32)],
        compiler_params=_cparams(("arbitrary", "arbitrary")),
        name="ffn_swiglu",
    )(*args)


def _moe_kernel(h_ref, cmb_ref, w1_ref, w3_ref, w2_ref, y_ref,
                acc_ref, xg_ref, ya_ref, rk_ref, rkt_ref, cnt_ref):
    e = pl.program_id(1)
    f = pl.program_id(2)
    tm = h_ref.shape[0]

    @pl.when((e == 0) & (f == 0))
    def _():
        r = lax.broadcasted_iota(jnp.int32, (tm, tm), 0)
        c = lax.broadcasted_iota(jnp.int32, (tm, tm), 1)
        before = jnp.where(c < r, 1.0, 0.0).astype(BF16)
        sel = cmb_ref[...] > 0.0
        rank = _dot(before, jnp.where(sel, 1.0, 0.0).astype(BF16))
        rk = jnp.where(sel, rank, -1.0)
        rk_ref[...] = rk
        rkt_ref[...] = rk.T
        cnt_ref[...] = jnp.sum(jnp.where(sel, 1.0, 0.0), axis=0, keepdims=True)
        acc_ref[...] = jnp.zeros(acc_ref.shape, F32)

    lane1 = lax.broadcasted_iota(jnp.int32, (1, LANES), 1)
    n_e = jnp.sum(jnp.where(lane1 == e, cnt_ref[...], 0.0)).astype(jnp.int32)

    def loop(n_blocks, body):
        if isinstance(n_blocks, int):
            for sb in range(n_blocks):
                body(sb, 0)
        else:
            lax.fori_loop(0, n_blocks, body, 0)

    def run(cap, n_blocks):
        cap_pad = -(-cap // LANES) * LANES

        def row0(sb):
            return sb * cap if isinstance(sb, int) else pl.multiple_of(sb * cap, cap)

        @pl.when(f == 0)
        def _():
            rank_row = rkt_ref[pl.ds(e, 1), :]

            def gather(sb, carry):
                r0 = row0(sb)
                slot = (r0 + lax.broadcasted_iota(jnp.int32, (cap, tm), 0)).astype(F32)
                onehot = jnp.where(rank_row == slot, 1.0, 0.0).astype(BF16)
                xg_ref[pl.ds(r0, cap), :] = _dot(onehot, h_ref[...]).astype(BF16)
                return carry

            loop(n_blocks, gather)

        def expert(sb, carry):
            r0 = row0(sb)
            xs = xg_ref[pl.ds(r0, cap), :]
            a = _dot(xs, w1_ref[0])
            act = a * _sigmoid(a) * _dot(xs, w3_ref[0])
            part = _dot(act.astype(BF16), w2_ref[0])

            @pl.when(f == 0)
            def _():
                ya_ref[pl.ds(r0, cap), :] = part

            @pl.when(f > 0)
            def _():
                ya_ref[pl.ds(r0, cap), :] += part

            return carry

        loop(n_blocks, expert)

        @pl.when(f == pl.num_programs(2) - 1)
        def _():
            lane = lax.broadcasted_iota(jnp.int32, (tm, LANES), 1)
            rank_col = jnp.sum(jnp.where(lane == e, rk_ref[...], 0.0), axis=1, keepdims=True)
            w_col = jnp.sum(jnp.where(lane == e, cmb_ref[...], 0.0), axis=1, keepdims=True)
            col = lax.broadcasted_iota(jnp.int32, (tm, cap_pad), 1)

            def scatter(sb, carry):
                r0 = row0(sb)
                ys = ya_ref[pl.ds(r0, cap), :].astype(BF16)
                if cap_pad > cap:
                    ys = jnp.concatenate([ys, jnp.zeros((cap_pad - cap, ys.shape[1]), BF16)], axis=0)
                hit = (rank_col == (r0 + col).astype(F32)) & (col < cap)
                acc_ref[...] += w_col * _dot(jnp.where(hit, 1.0, 0.0).astype(BF16), ys)
                return carry

            loop(n_blocks, scatter)

    for idx, cap in enumerate(MOE_CAPS):
        lower = MOE_CAPS[idx - 1] if idx else 0
        if idx == len(MOE_CAPS) - 1:
            pl.when(n_e > lower)(functools.partial(run, cap, (n_e + cap - 1) // cap))
        else:
            pl.when((n_e > lower) & (n_e <= cap))(functools.partial(run, cap, 1))

    @pl.when((e == pl.num_programs(1) - 1) & (f == pl.num_programs(2) - 1))
    def _():
        y_ref[...] = acc_ref[...].astype(y_ref.dtype)


def _moe_call(h, cmb, w1, w3, w2, tm, tf):
    t, d = h.shape
    ne, _, ff = w1.shape
    rows = -(-tm // MOE_CAPS[-1]) * MOE_CAPS[-1]
    return pl.pallas_call(
        _moe_kernel,
        grid=(t // tm, ne, ff // tf),
        in_specs=[pl.BlockSpec((tm, d), lambda i, e, f: (i, 0)),
                  pl.BlockSpec((tm, LANES), lambda i, e, f: (i, 0)),
                  pl.BlockSpec((1, d, tf), lambda i, e, f: (e, 0, f)),
                  pl.BlockSpec((1, d, tf), lambda i, e, f: (e, 0, f)),
                  pl.BlockSpec((1, tf, d), lambda i, e, f: (e, f, 0))],
        out_specs=pl.BlockSpec((tm, d), lambda i, e, f: (i, 0)),
        out_shape=jax.ShapeDtypeStruct((t, d), BF16),
        scratch_shapes=[pltpu.VMEM((tm, d), F32),
                        pltpu.VMEM((rows, d), BF16),
                        pltpu.VMEM((rows, d), F32),
                        pltpu.VMEM((tm, LANES), F32),
                        pltpu.VMEM((LANES, tm), F32),
                        pltpu.VMEM((1, LANES), F32)],
        compiler_params=_cparams(("arbitrary", "arbitrary", "arbitrary")),
        name="moe_top2",
    )(h, cmb, w1, w3, w2)


def _resid_kernel(*refs, has_next):
    if has_next:
        y_ref, x_ref, mod_ref, nw_ref, modn_ref, xo_ref, hn_ref = refs
        nxt = (nw_ref, modn_ref, hn_ref)
    else:
        y_ref, x_ref, mod_ref, xo_ref = refs
        nxt = None
    _finish(y_ref[...].astype(F32), x_ref, mod_ref, xo_ref, nxt)


def _resid_call(y, x, mod, nxt, seq, tm):
    t, d = x.shape
    per_b = seq // tm
    tok = lambda: pl.BlockSpec((tm, d), lambda i: (i, 0))
    modspec = lambda: pl.BlockSpec((1, 6, d), lambda i: (i // per_b, 0, 0))
    in_specs = [tok(), tok(), modspec()]
    out_specs = [tok()]
    out_shape = [jax.ShapeDtypeStruct((t, d), F32)]
    args = [y, x, mod]
    if nxt is not None:
        in_specs += [pl.BlockSpec((1, d), lambda i: (0, 0)), modspec()]
        out_specs.append(tok())
        out_shape.append(jax.ShapeDtypeStruct((t, d), BF16))
        args += list(nxt)
    return pl.pallas_call(
        functools.partial(_resid_kernel, has_next=nxt is not None),
        grid=(t // tm,),
        in_specs=in_specs, out_specs=out_specs, out_shape=out_shape,
        compiler_params=_cparams(("arbitrary",)),
        name="moe_residual",
    )(*args)


def _attn_col_perm():
    hq = DA_HEADS * DA_DK
    idx = []
    for base in (0, 2 * hq):
        for h in range(DA_HEADS):
            idx += list(range(base + h * DA_DK, base + (h + 1) * DA_DK))
            idx += list(range(base + hq + h * DA_DK, base + hq + (h + 1) * DA_DK))
    idx += list(range(4 * hq, 4 * hq + DA_HEADS * DA_DV))
    return np.asarray(idx, np.int32)


def _pick(n, pref):
    for c in pref:
        if n % c == 0:
            return c
    return n


def kernel(x, c, w_mod, b_mod, norm1_w, norm2_w, w_in, c_conv_w, c_conv_b, a_qnorm_w, a_knorm_w,
           a_lambda_q1, a_lambda_k1, a_lambda_q2, a_lambda_k2, a_subln_w, b_lb_logits, b_gnorm_w,
           c_igate_b, c_fgate_b, c_norm_w, w_branch, w_out, ffn_w1, ffn_w3, ffn_w2,
           moe_router_w, moe_router_b, moe_w1, moe_w3, moe_w2):
    batch, seq, d = x.shape
    depth = w_in.shape[0]
    t = batch * seq
    n_a = 4 * DA_HEADS * DA_DK + DA_HEADS * DA_DV
    n_b = 4 * HG_HEADS * HG_D
    n_c = 4 * ML_HEADS * ML_D + 2 * ML_HEADS
    n_c_pad = -(-n_c // LANES) * LANES
    tm = _pick(seq, (1024, 512, 256))
    tq = _pick(seq, (ATT_TQ,))
    ts_h = _pick(seq, (512, 256, 128))

    c8 = jnp.zeros((8, d), F32).at[:batch].set(c)
    mod = _mod_call(c8, w_mod, b_mod.reshape(depth, 1, 6 * d))[:, :batch].reshape(depth, batch, 6, d)

    lb_all = jnp.cumsum(jax.nn.softmax(b_lb_logits.astype(F32), axis=0), axis=0)
    lb_all = lb_all - lb_all[:1]
    slopes = jnp.asarray(_attn_slopes(), F32)

    xf = x.reshape(t, d)
    h = _normmod_call(xf, norm1_w[0].reshape(1, d), mod[0], seq, tm)
    perm_a = _attn_col_perm()
    for l in range(depth):
        lam_init = 0.8 - 0.6 * math.exp(-0.3 * l)
        wl = w_in[l]
        w_a = wl[:, :n_a][:, perm_a].astype(BF16)
        w_b = wl[:, n_a:n_a + n_b].astype(BF16)
        w_c = jnp.pad(wl[:, n_a + n_b:n_a + n_b + n_c], ((0, 0), (0, n_c_pad - n_c))).astype(BF16)
        w_g = wl[:, n_a + n_b + n_c:].astype(BF16)
        bias_c = jnp.zeros((1, n_c_pad), F32)
        bias_c = bias_c.at[0, n_c - 2 * ML_HEADS:n_c - ML_HEADS].set(c_igate_b[l])
        bias_c = bias_c.at[0, n_c - ML_HEADS:n_c].set(c_fgate_b[l])
        pa = _matmul(h, w_a, jnp.zeros((1, n_a), F32), BF16, tm, n_a)
        pb = _matmul(h, w_b, jnp.zeros((1, n_b), F32), F32, tm, n_b // 2)
        pc = _matmul(h, w_c, bias_c, F32, tm, n_c_pad)
        pg = _matmul(h, w_g, jnp.zeros((1, 3 * d), F32), BF16, tm, 3 * d // 2)

        qw = (jnp.tile(a_qnorm_w[l], 2 * DA_HEADS) * (DA_DK ** -0.5 * LOG2E)).reshape(1, -1)
        kw = jnp.tile(a_knorm_w[l], 2 * DA_HEADS).reshape(1, -1)
        kn, qt, vt = _attn_prep_call(pa, qw, kw, batch, seq, _pick(seq, (512, 256)))
        lam = (jnp.exp(jnp.sum(a_lambda_q1[l] * a_lambda_k1[l]))
               - jnp.exp(jnp.sum(a_lambda_q2[l] * a_lambda_k2[l])) + lam_init)
        scal = jnp.concatenate([slopes, jnp.stack([lam, jnp.asarray(1.0 - lam_init, F32)])]).astype(F32)
        y_a = _attn_call(scal, qt, kn, vt, a_subln_w[l].reshape(1, -1), batch, seq, tq)

        lb = lb_all[l].reshape(HG_HEADS, 1, HG_D)
        lbs = jnp.concatenate([jnp.log(lb), jnp.log1p(-lb)], axis=1)
        y_b = _hgrn_call(pb, lbs, b_gnorm_w[l].reshape(1, -1), batch, seq, ts_h)

        gates = pc[:, n_c - 2 * ML_HEADS:n_c].reshape(batch, seq, 2, ML_HEADS)
        gates = gates.transpose(2, 0, 3, 1).reshape(2, batch, ML_HEADS, seq // ML_CHUNK, ML_CHUNK)
        y_c = _mlstm_call(pc, gates[0], gates[1], c_conv_w[l], c_conv_b[l].reshape(1, -1),
                          c_norm_w[l].reshape(1, -1), batch, seq)

        dense = l % 2 == 0
        router = None
        if not dense:
            rw = jnp.pad(moe_router_w[l // 2], ((0, 0), (0, LANES - N_EXPERTS)))
            rb = jnp.pad(moe_router_b[l // 2], (0, LANES - N_EXPERTS)).reshape(1, LANES)
            router = (rw, rb)
        outs = _merge_call(y_a, y_b, y_c, pg, w_branch[l].astype(BF16), w_out[l].astype(BF16), xf,
                           mod[l], norm2_w[l].reshape(1, d), router, seq, 512)
        xf, h2 = outs[0], outs[1]
        nxt = None if l == depth - 1 else (norm1_w[l + 1].reshape(1, d), mod[l + 1])
        if dense:
            res = _ffn_call(h2, ffn_w1[l // 2].astype(BF16), ffn_w3[l // 2].astype(BF16),
                            ffn_w2[l // 2].astype(BF16), xf, mod[l], nxt, seq, tm, FFN_TF)
        else:
            y = _moe_call(h2, outs[2], moe_w1[l // 2].astype(BF16), moe_w3[l // 2].astype(BF16),
                          moe_w2[l // 2].astype(BF16), tm, _pick(moe_w1.shape[-1], (MOE_TF,)))
            res = _resid_call(y, xf, mod[l], nxt, seq, tm)
        xf = res[0]
        if nxt is not None:
            h = res[1]
    return xf.reshape(batch, seq, d)
```

```python
import functools
import math

import numpy as np
import jax
import jax.numpy as jnp
from jax import lax
from jax.experimental import pallas as pl
from jax.experimental.pallas import tpu as pltpu

F32 = jnp.float32
BF16 = jnp.bfloat16
NORM_EPS = 1e-6

CHUNK = 64
DA_HEADS, DA_DK, DA_DV = 4, 64, 128
HG_HEADS, HG_D = 4, 128
ML_HEADS, ML_D, ML_CONV = 4, 128, 4
N_EXPERTS = 8

LANES = 128
VMEM_LIMIT = 56 * 1024 * 1024

ATT_TQ = 256
ATT_HEADS_PER_STEP = 2
ATT_PAD_ROWS = 16
LOG2E = math.log2(math.e)
ATT_BOUND_MARGIN = 1.01
ATT_BOUND_EPS = 1e-3
ATT_BOUND_MAX = 40.0
ATT_ZERO_EXP = 160.0
HG_CHUNK = 128
HG_SUB = 32
HG_SAFE_DECAY = 80.0
ML_CHUNK = 256
MOE_CAPS = (256, 320, 384)
MOE_TF = 1408
FFN_TF = 256


def _cparams(sem, vmem=VMEM_LIMIT):
    return pltpu.CompilerParams(dimension_semantics=sem, vmem_limit_bytes=vmem)


def _dot(a, b):
    return jnp.dot(a, b, preferred_element_type=F32)


def _dot_nt(a, b):
    return lax.dot_general(a, b, (((1,), (1,)), ((), ())), preferred_element_type=F32)


def _split3(x):
    hi = x.astype(BF16)
    r1 = x - hi.astype(F32)
    mid = r1.astype(BF16)
    lo = (r1 - mid.astype(F32)).astype(BF16)
    return hi, mid, lo


def _norm_mod(x, w, scale, shift):
    y = x * lax.rsqrt(jnp.mean(x * x, axis=-1, keepdims=True) + NORM_EPS) * w
    return y * (1.0 + scale) + shift


def _sigmoid(x):
    return 0.5 * jnp.tanh(0.5 * x) + 0.5


def _log1p_exp_neg(z):
    return jnp.log(1.0 + jnp.exp(-z))


def _log_sigmoid(x):
    return jnp.minimum(x, 0.0) - _log1p_exp_neg(jnp.abs(x))


def _mod_kernel(c_ref, w_ref, b_ref, o_ref):
    cnd = c_ref[...]
    cnd = cnd * _sigmoid(cnd)
    o_ref[0] = _dot(cnd.astype(BF16), w_ref[0].astype(BF16)) + b_ref[0]


def _mod_call(c8, w_mod, b_mod):
    depth, d, n = w_mod.shape
    tn = 1536 if n % 1536 == 0 else n
    return pl.pallas_call(
        _mod_kernel,
        grid=(depth, n // tn),
        in_specs=[pl.BlockSpec((8, d), lambda l, j: (0, 0)),
                  pl.BlockSpec((1, d, tn), lambda l, j: (l, 0, j)),
                  pl.BlockSpec((1, 1, tn), lambda l, j: (l, 0, j))],
        out_specs=pl.BlockSpec((1, 8, tn), lambda l, j: (l, 0, j)),
        out_shape=jax.ShapeDtypeStruct((depth, 8, n), F32),
        compiler_params=_cparams(("arbitrary", "arbitrary")),
        name="adaln_mod",
    )(c8, w_mod, b_mod)


def _normmod_kernel(x_ref, nw_ref, mod_ref, h_ref):
    m = mod_ref[0]
    h_ref[...] = _norm_mod(x_ref[...], nw_ref[...], m[1:2], m[0:1]).astype(BF16)


def _normmod_call(x, nw, mod, seq, tm):
    t, d = x.shape
    per_b = seq // tm
    return pl.pallas_call(
        _normmod_kernel,
        grid=(t // tm,),
        in_specs=[pl.BlockSpec((tm, d), lambda i: (i, 0)),
                  pl.BlockSpec((1, d), lambda i: (0, 0)),
                  pl.BlockSpec((1, 6, d), lambda i: (i // per_b, 0, 0))],
        out_specs=pl.BlockSpec((tm, d), lambda i: (i, 0)),
        out_shape=jax.ShapeDtypeStruct((t, d), BF16),
        compiler_params=_cparams(("arbitrary",)),
        name="prenorm_mod",
    )(x, nw, mod)


def _mm_kernel(x_ref, w_ref, b_ref, o_ref):
    o_ref[...] = (_dot(x_ref[...], w_ref[...]) + b_ref[...]).astype(o_ref.dtype)


def _matmul(x, w, bias, out_dtype, tm, tn):
    t, k = x.shape
    n = w.shape[1]
    return pl.pallas_call(
        _mm_kernel,
        grid=(n // tn, t // tm),
        in_specs=[pl.BlockSpec((tm, k), lambda j, i: (i, 0)),
                  pl.BlockSpec((k, tn), lambda j, i: (0, j)),
                  pl.BlockSpec((1, tn), lambda j, i: (0, j))],
        out_specs=pl.BlockSpec((tm, tn), lambda j, i: (i, j)),
        out_shape=jax.ShapeDtypeStruct((t, n), out_dtype),
        compiler_params=_cparams(("arbitrary", "arbitrary")),
        name="in_proj",
    )(x, w, bias)


def _attn_prep_kernel(x_ref, w_ref, qw_ref, kw_ref, kn_ref, qt_ref, vt_ref):
    hw = DA_HEADS * 2 * DA_DK
    a = _dot(x_ref[...], w_ref[...])
    r = lax.broadcasted_iota(jnp.int32, (hw, hw), 0) // DA_DK
    c = lax.broadcasted_iota(jnp.int32, (hw, hw), 1) // DA_DK
    group = jnp.where(r == c, 1.0, 0.0).astype(BF16)

    def qk_norm(z, w):
        ms = _dot((z * z).astype(BF16), group) * (1.0 / DA_DK)
        return z * lax.rsqrt(ms + NORM_EPS) * w

    qn = qk_norm(a[:, :hw], qw_ref[...])
    kn = qk_norm(a[:, hw:2 * hw], kw_ref[...])
    kn_ref[...] = kn.astype(BF16)
    qt_ref[0] = qn.T.astype(BF16)
    vt_ref[0] = a[:, 2 * hw:].T.astype(BF16)


def _attn_prep_call(x, w, qw, kw, batch, seq, ts):
    t, d = x.shape
    hw = DA_HEADS * 2 * DA_DK
    hv = DA_HEADS * DA_DV
    per_b = seq // ts
    return pl.pallas_call(
        _attn_prep_kernel,
        grid=(t // ts,),
        in_specs=[pl.BlockSpec((ts, d), lambda i: (i, 0)),
                  pl.BlockSpec((d, 2 * hw + hv), lambda i: (0, 0)),
                  pl.BlockSpec((1, hw), lambda i: (0, 0)),
                  pl.BlockSpec((1, hw), lambda i: (0, 0))],
        out_specs=[pl.BlockSpec((ts, hw), lambda i: (i, 0)),
                   pl.BlockSpec((1, hw, ts), lambda i: (i // per_b, 0, i % per_b)),
                   pl.BlockSpec((1, hv, ts), lambda i: (i // per_b, 0, i % per_b))],
        out_shape=[jax.ShapeDtypeStruct((t, hw), BF16),
                   jax.ShapeDtypeStruct((batch, hw, seq), BF16),
                   jax.ShapeDtypeStruct((batch, hv, seq), BF16)],
        compiler_params=_cparams(("arbitrary",)),
        name="attn_prep",
    )(x, w, qw, kw)


def _attn_slopes():
    return [LOG2E * 2.0 ** (-8.0 * (h + 1) / DA_HEADS) for h in range(DA_HEADS)]


def _attn_window(head, tq):
    return int((ATT_ZERO_EXP / _attn_slopes()[head] - 1.0) // tq) + 1


def _attn_kernel(sc_ref, qt_ref, k_ref, vt_ref, sw_ref, o_ref,
                 qz_ref, m_ref, acc_ref, vs_ref, kmax_ref, s_ref, s2_ref, *, tq, seq):
    hp = ATT_HEADS_PER_STEP
    dv = DA_DV
    g = pl.program_id(1)
    i = pl.program_id(2)
    lam = sc_ref[DA_HEADS]
    out_scale = sc_ref[DA_HEADS + 1]
    slopes = [sc_ref[hp * g + hh] for hh in range(hp)]
    row16 = lax.broadcasted_iota(jnp.int32, (ATT_PAD_ROWS, tq), 0)
    ones_rows = jnp.where(row16 == 0, 1.0, 0.0).astype(BF16)

    half = lax.broadcasted_iota(jnp.int32, (1, 2 * DA_DK), 1) < DA_DK

    @pl.when(i == 0)
    def _():
        pos = lax.broadcasted_iota(jnp.int32, (1, seq), 1) & (tq - 1)
        rel = (pos - (tq - 1)).astype(F32)
        sub16 = lax.broadcasted_iota(jnp.int32, (ATT_PAD_ROWS, seq), 0)
        gr = lax.broadcasted_iota(jnp.int32, (2 * DA_DK, 2 * DA_DK), 0) // DA_DK
        gc = lax.broadcasted_iota(jnp.int32, (2 * DA_DK, 2 * DA_DK), 1) // DA_DK
        group = jnp.where(gr == gc, 1.0, 0.0).astype(BF16)
        for hh in range(hp):
            w = jnp.exp2(slopes[hh] * rel)
            vs_ref[hh, 0:dv, :] = (vt_ref[0, hh * dv:(hh + 1) * dv, :].astype(F32) * w).astype(BF16)
            vs_ref[hh, dv:dv + ATT_PAD_ROWS, :] = jnp.where(sub16 == 0, w, 0.0).astype(BF16)

            def knorm(n, best):
                kc = k_ref[pl.ds(pl.multiple_of(n * tq, tq), tq),
                           hh * 2 * DA_DK:(hh + 1) * 2 * DA_DK].astype(F32)
                return jnp.maximum(best, jnp.max(_dot((kc * kc).astype(BF16), group), axis=0, keepdims=True))

            k2 = lax.fori_loop(0, seq // tq, knorm, jnp.zeros((1, 2 * DA_DK), F32))
            kmax_ref[2 * hh] = jnp.max(jnp.where(half, k2, 0.0), axis=1, keepdims=True)
            kmax_ref[2 * hh + 1] = jnp.max(jnp.where(half, 0.0, k2), axis=1, keepdims=True)

    row = lax.broadcasted_iota(jnp.int32, (2 * DA_DK, tq), 0)
    qq_row = lax.broadcasted_iota(jnp.int32, (1, tq), 1).astype(F32)
    bound_max = jnp.zeros((), F32)
    firsts = []

    def scores(hh, blk):
        start = pl.multiple_of(blk * tq, tq)
        return _dot(k_ref[pl.ds(start, tq), hh * 2 * DA_DK:(hh + 1) * 2 * DA_DK], qz_ref[hh])

    for hh in range(hp):
        qt = qt_ref[0, hh * 2 * DA_DK:(hh + 1) * 2 * DA_DK, :]
        zero = jnp.zeros_like(qt)
        q1 = jnp.where(row < DA_DK, qt, zero)
        q2 = jnp.where(row >= DA_DK, qt, zero)
        qz_ref[hh] = jnp.concatenate([q1, q2], axis=1)
        window = jnp.int32(_attn_window(hh, tq))
        for grp in range(1, DA_HEADS // hp):
            window = jnp.where(g == grp, _attn_window(grp * hp + hh, tq), window)
        firsts.append((jnp.maximum(i - window, 0) // 2) * 2)
        s_ref[hh] = scores(hh, firsts[hh])
        s2_ref[hh] = scores(hh, jnp.minimum(firsts[hh] + 1, i))
        qn = jnp.concatenate(
            [jnp.sum(jnp.square(q1.astype(F32)), axis=0, keepdims=True) * kmax_ref[2 * hh],
             jnp.sum(jnp.square(q2.astype(F32)), axis=0, keepdims=True) * kmax_ref[2 * hh + 1]], axis=1)
        bound = jnp.sqrt(qn) * ATT_BOUND_MARGIN + ATT_BOUND_EPS
        bound_max = jnp.maximum(bound_max, jnp.max(bound))
        m_ref[hh] = bound + slopes[hh] * jnp.concatenate([qq_row, qq_row], axis=1)
    acc_ref[...] = jnp.zeros(acc_ref.shape, F32)

    kk = lax.broadcasted_iota(jnp.int32, (tq, 2 * tq), 0)
    cc = lax.broadcasted_iota(jnp.int32, (tq, 2 * tq), 1)
    qq = jnp.where(cc >= tq, cc - tq, cc)
    dist = (qq - jnp.abs(qq - kk)).astype(F32)
    visible = (kk // CHUNK) <= (qq // CHUNK)

    def diag_values(hh):
        start = pl.multiple_of(i * tq, tq)
        return jnp.concatenate([vt_ref[0, hh * dv:(hh + 1) * dv, pl.ds(start, tq)], ones_rows], axis=0)

    @pl.when(bound_max <= ATT_BOUND_MAX)
    def _():
        def probs(hh, blk, s):
            last_key = ((blk + 1 - i) * tq - 1).astype(F32)
            return jnp.exp2(s - (m_ref[hh] - slopes[hh] * last_key)).astype(BF16)

        def body(heads, jj, carry):
            blk = 2 * jj
            start = pl.multiple_of(blk * tq, 2 * tq)
            for hh in heads:
                p = jnp.concatenate([probs(hh, blk, s_ref[hh]), probs(hh, blk + 1, s2_ref[hh])], axis=0)
                s_ref[hh] = scores(hh, jnp.minimum(blk + 2, i))
                s2_ref[hh] = scores(hh, jnp.minimum(blk + 3, i))
                acc_ref[hh] += _dot(vs_ref[hh, :, pl.ds(start, 2 * tq)], p)
            return carry

        for hh in range(hp - 1, 0, -1):
            lax.fori_loop(firsts[hh] // 2, firsts[hh - 1] // 2,
                          functools.partial(body, tuple(range(hh, hp))), 0)
        lax.fori_loop(firsts[0] // 2, i // 2, functools.partial(body, tuple(range(hp))), 0)

        @pl.when(i % 2 == 1)
        def _():
            start = pl.multiple_of((i - 1) * tq, tq)
            for hh in range(hp):
                acc_ref[hh] += _dot(vs_ref[hh, :, pl.ds(start, tq)], probs(hh, i - 1, s_ref[hh]))
                s_ref[hh] = s2_ref[hh]

        for hh in range(hp):
            s = jnp.where(visible, s_ref[hh] + slopes[hh] * dist, -jnp.inf)
            acc_ref[hh] += _dot(diag_values(hh), jnp.exp2(s - m_ref[hh]).astype(BF16))

    @pl.when(bound_max > ATT_BOUND_MAX)
    def _():
        m_ref[...] = jnp.full(m_ref.shape, -jnp.inf, F32)

        def update(hh, s, bound_shift, v_aug):
            m_old = m_ref[hh]
            m_new = jnp.maximum(m_old, jnp.max(s, axis=0, keepdims=True) + bound_shift)
            p = jnp.exp2(s - (m_new - bound_shift))
            acc_ref[hh] = jnp.exp2(m_old - m_new) * acc_ref[hh] + _dot(v_aug, p.astype(BF16))
            m_ref[hh] = m_new

        def body(j, carry):
            start = pl.multiple_of(j * tq, tq)
            last_key = ((j + 1 - i) * tq - 1).astype(F32)
            for hh in range(hp):
                update(hh, scores(hh, j), slopes[hh] * last_key, vs_ref[hh, :, pl.ds(start, tq)])
            return carry

        lax.fori_loop(0, i, body, 0)
        for hh in range(hp):
            s = jnp.where(visible, scores(hh, i) + slopes[hh] * dist, -jnp.inf)
            update(hh, s, jnp.zeros((), F32), diag_values(hh))

    outs = []
    for hh in range(hp):
        acc = acc_ref[hh]
        o2 = acc[:dv, :] * (1.0 / acc[dv:dv + 1, :])
        o = (o2[:, :tq] - lam * o2[:, tq:]).T
        o = o * lax.rsqrt(jnp.mean(o * o, axis=-1, keepdims=True) + NORM_EPS) * sw_ref[...]
        outs.append(o * out_scale)
    o_ref[...] = jnp.concatenate(outs, axis=1).astype(o_ref.dtype)


def _attn_call(scal, qt, kn, vt, sw, batch, seq, tq):
    t = kn.shape[0]
    nq = seq // tq
    hp = ATT_HEADS_PER_STEP
    return pl.pallas_call(
        functools.partial(_attn_kernel, tq=tq, seq=seq),
        grid=(batch, DA_HEADS // hp, nq),
        in_specs=[pl.BlockSpec(memory_space=pltpu.SMEM),
                  pl.BlockSpec((1, hp * 2 * DA_DK, tq), lambda b, g, i: (b, g, i)),
                  pl.BlockSpec((seq, hp * 2 * DA_DK), lambda b, g, i: (b, g)),
                  pl.BlockSpec((1, hp * DA_DV, seq), lambda b, g, i: (b, g, 0)),
                  pl.BlockSpec((1, DA_DV), lambda b, g, i: (0, 0))],
        out_specs=pl.BlockSpec((tq, hp * DA_DV), lambda b, g, i: (b * nq + i, g)),
        out_shape=jax.ShapeDtypeStruct((t, DA_HEADS * DA_DV), BF16),
        scratch_shapes=[pltpu.VMEM((hp, 2 * DA_DK, 2 * tq), BF16),
                        pltpu.VMEM((hp, 1, 2 * tq), F32),
                        pltpu.VMEM((hp, DA_DV + ATT_PAD_ROWS, 2 * tq), F32),
                        pltpu.VMEM((hp, DA_DV + ATT_PAD_ROWS, seq), BF16),
                        pltpu.VMEM((2 * hp, 1, 1), F32),
                        pltpu.VMEM((hp, tq, 2 * tq), F32),
                        pltpu.VMEM((hp, tq, 2 * tq), F32)],
        compiler_params=_cparams(("arbitrary", "arbitrary", "arbitrary")),
        name="diff_attn",
    )(scal, qt, kn, vt, sw)


def _hgrn_kernel(q_ref, f_ref, i_ref, g_ref, lb_ref, gw_ref, o_ref,
                 st_ref, b_ref, qs_ref, ks_ref, oi_ref, *, n_chunks):
    c = HG_CHUNK
    sub = HG_SUB
    d = HG_D
    nh = HG_HEADS
    chains = [(bb, hh) for bb in range(q_ref.shape[0]) for hh in range(nh)]

    @pl.when(pl.program_id(0) == 0)
    def _():
        st_ref[...] = jnp.zeros(st_ref.shape, F32)

    gw = gw_ref[...]
    rr = lax.broadcasted_iota(jnp.int32, (c, c), 0)
    cc = lax.broadcasted_iota(jnp.int32, (c, c), 1)
    causal = cc <= rr
    tril = jnp.where(causal, 1.0, 0.0).astype(BF16)

    def chunk(n, carry):
        r0 = pl.multiple_of(n * c, c)
        gates = []
        for ch, (bb, hh) in enumerate(chains):
            cols = slice(hh * d, (hh + 1) * d)
            log_lb = lb_ref[hh, 0:1, :]
            q = q_ref[bb, pl.ds(r0, c), cols]
            q = q * _sigmoid(q)
            a = lb_ref[hh, 1:2, :] + _log_sigmoid(f_ref[bb, pl.ds(r0, c), cols])
            logf = jnp.maximum(log_lb, a) + _log1p_exp_neg(jnp.abs(log_lb - a))
            k = 1.0 - jnp.exp(logf)
            v = i_ref[bb, pl.ds(r0, c), cols]
            hi, mid, lo = _split3(logf)
            b = _dot(tril, hi) + _dot(tril, mid) + _dot(tril, lo)
            gates.append((q, k, v, b))

        heads = []
        decay = jnp.zeros((), F32)
        for ch, (q, k, v, b) in enumerate(gates):
            b_last = b[c - 1:c, :]
            st = st_ref[ch]
            o_inter = _dot_nt((q * jnp.exp(b)).astype(BF16), st.astype(BF16))
            k_hat = k * jnp.exp(b_last - b)
            st_ref[ch] = st * jnp.exp(b_last) + _dot(v.T.astype(BF16), k_hat.astype(BF16))
            betas = []
            for blk in range(c // sub):
                beta = jnp.zeros((1, d), F32) if blk == 0 else b[blk * sub - 1:blk * sub, :]
                betas.append(beta)
                b_end = b[(blk + 1) * sub - 1:(blk + 1) * sub, :]
                decay = jnp.maximum(decay, jnp.max(beta - b_end))
            heads.append((q, k, v, b, betas, o_inter))

        @pl.when(decay < HG_SAFE_DECAY)
        def _():
            atts = []
            for q, k, v, b, betas, _ in heads:
                rows = []
                for blk in range(c // sub):
                    beta = betas[blk]
                    q_t = q[blk * sub:(blk + 1) * sub, :] * jnp.exp(b[blk * sub:(blk + 1) * sub, :] - beta)
                    k_t = k * jnp.exp(jnp.minimum(beta - b, HG_SAFE_DECAY))
                    rows.append(_dot_nt(q_t.astype(BF16), k_t.astype(BF16)))
                atts.append(jnp.where(causal, jnp.concatenate(rows, axis=0), 0.0).astype(BF16))
            for ch, att in enumerate(atts):
                oi_ref[ch] = _dot(att, heads[ch][2].astype(BF16))

        @pl.when(decay >= HG_SAFE_DECAY)
        def _():
            ridx = lax.broadcasted_iota(jnp.int32, (c, d), 0)
            for ch, (q, k, v, b, _, _) in enumerate(heads):
                b_ref[...] = b
                qs_ref[...] = q
                ks_ref[...] = k

                def row(t, carry2):
                    bt = b_ref[pl.ds(t, 1), :]
                    e = jnp.exp(jnp.where(ridx <= t, bt - b_ref[...], -jnp.inf))
                    w = jnp.sum(qs_ref[pl.ds(t, 1), :] * ks_ref[...] * e, axis=1, keepdims=True)
                    oi_ref[ch, pl.ds(t, 1), :] = jnp.sum(w * v, axis=0, keepdims=True)
                    return carry2

                lax.fori_loop(0, c, row, 0)

        for ch, (bb, hh) in enumerate(chains):
            cols = slice(hh * d, (hh + 1) * d)
            o = heads[ch][5] + oi_ref[ch]
            o = o * lax.rsqrt(jnp.mean(o * o, axis=-1, keepdims=True) + NORM_EPS) * gw
            g = g_ref[bb, pl.ds(r0, c), cols]
            o_ref[bb, pl.ds(r0, c), cols] = (o * (g * _sigmoid(g))).astype(o_ref.dtype)
        return carry

    lax.fori_loop(0, n_chunks, chunk, 0)


def _hgrn_call(bproj, lbs, gw, batch, seq, ts):
    t = bproj.shape[0]
    d = HG_D
    nh = HG_HEADS
    bp3 = bproj.reshape(batch, seq, bproj.shape[1])
    spec = lambda off: pl.BlockSpec((batch, ts, nh * d), lambda i: (0, i, off))
    out = pl.pallas_call(
        functools.partial(_hgrn_kernel, n_chunks=ts // HG_CHUNK),
        grid=(seq // ts,),
        in_specs=[spec(0), spec(1), spec(2), spec(3),
                  pl.BlockSpec((nh, 2, d), lambda i: (0, 0, 0)),
                  pl.BlockSpec((1, d), lambda i: (0, 0))],
        out_specs=pl.BlockSpec((batch, ts, nh * d), lambda i: (0, i, 0)),
        out_shape=jax.ShapeDtypeStruct((batch, seq, nh * d), BF16),
        scratch_shapes=[pltpu.VMEM((batch * nh, d, d), F32),
                        pltpu.VMEM((HG_CHUNK, d), F32),
                        pltpu.VMEM((HG_CHUNK, d), F32),
                        pltpu.VMEM((HG_CHUNK, d), F32),
                        pltpu.VMEM((batch * nh, HG_CHUNK, d), F32)],
        compiler_params=_cparams(("arbitrary",)),
        name="hgrn2",
    )(bp3, bp3, bp3, bp3, lbs, gw)
    return out.reshape(t, nh * d)


def _mlstm_kernel(uq_ref, uk_ref, v_ref, op_ref, gi_ref, gf_ref, cwq_ref, cwk_ref,
                  cbq_ref, cbk_ref, nw_ref, o_ref,
                  xq_ref, xk_ref, c_ref, m_ref, bs_ref):
    L = ML_CHUNK
    d = ML_D
    nh = ML_HEADS
    nb = uq_ref.shape[0]
    i = pl.program_id(0)
    rr = lax.broadcasted_iota(jnp.int32, (L, L), 0)
    cc = lax.broadcasted_iota(jnp.int32, (L, L), 1)

    @pl.when(i == 0)
    def _():
        c_ref[...] = jnp.zeros(c_ref.shape, F32)
        m_ref[...] = jnp.zeros(m_ref.shape, F32)
        xq_ref[:, 0:8, :] = jnp.zeros((nb, 8, nh * d), F32)
        xk_ref[:, 0:8, :] = jnp.zeros((nb, 8, nh * d), F32)
        upper = jnp.where(rr <= cc, 1.0, 0.0).astype(BF16)
        for ch in range(nb * nh):
            hi, mid, lo = _split3(_log_sigmoid(gf_ref[ch // nh, ch % nh]))
            bs_ref[ch] = _dot(hi, upper) + _dot(mid, upper) + _dot(lo, upper)

    def conv_silu(bb, u_ref, x_ref, w_ref, b_ref):
        x_ref[bb, 8:8 + L, :] = u_ref[bb]
        y = b_ref[...] + w_ref[ML_CONV - 1:ML_CONV, :] * x_ref[bb, 8:8 + L, :]
        for j in range(ML_CONV - 1):
            y = y + w_ref[j:j + 1, :] * x_ref[bb, 5 + j:5 + j + L, :]
        x_ref[bb, 0:8, :] = x_ref[bb, L:L + 8, :]
        return y * _sigmoid(y)

    q_all = [conv_silu(bb, uq_ref, xq_ref, cwq_ref, cbq_ref) for bb in range(nb)]
    k_all = [conv_silu(bb, uk_ref, xk_ref, cwk_ref, cbk_ref) * (d ** -0.5) for bb in range(nb)]
    lane = lax.broadcasted_iota(jnp.int32, (L, d), 1)
    ones_col = jnp.where(lane == 0, 1.0, 0.0)
    stage1 = []
    for ch in range(nb * nh):
        bb, hh = ch // nh, ch % nh
        cols = slice(hh * d, (hh + 1) * d)
        q = q_all[bb][:, cols].astype(BF16)
        kt = k_all[bb][:, cols].T
        v_aug = jnp.concatenate([v_ref[bb, :, cols], ones_col], axis=1).astype(BF16)
        c_aug = c_ref[ch]
        qk = _dot(q, kt.astype(BF16))
        qc = _dot(q, c_aug.astype(BF16))

        b_row = bs_ref[ch, pl.ds(i, 1), :]
        ig_row = gi_ref[bb, hh, pl.ds(i, 1), :]
        m_prev = m_ref[ch]
        g = b_row[:, L - 1:L]
        log_w = g - b_row + ig_row
        m_new = jnp.maximum(g + m_prev, jnp.max(log_w, axis=1, keepdims=True))
        w_s = jnp.exp(log_w - m_new)
        c_ref[ch] = jnp.exp(g + m_prev - m_new) * c_aug + _dot((kt * w_s).astype(BF16), v_aug)
        m_ref[ch] = m_new
        stage1.append((qk, qc, v_aug, b_row, ig_row, m_prev))

    stage2 = []
    for qk, qc, v_aug, b_row, ig_row, m_prev in stage1:
        b_col = jnp.sum(jnp.where(rr == cc, b_row, 0.0), axis=1, keepdims=True)
        log_d = jnp.where(cc <= rr, b_col + (ig_row - b_row), -jnp.inf)
        log_inter = b_col + m_prev
        m_t = jnp.maximum(log_inter, jnp.max(log_d, axis=1, keepdims=True))
        w_intra = jnp.exp(log_d - m_t) * qk
        tot = jnp.exp(log_inter - m_t) * qc + _dot(w_intra.astype(BF16), v_aug)
        stage2.append((tot, m_t))

    for ch, (tot, m_t) in enumerate(stage2):
        bb, hh = ch // nh, ch % nh
        cols = slice(hh * d, (hh + 1) * d)
        denom = jnp.maximum(jnp.abs(tot[:, d:d + 1]), jnp.exp(-m_t))
        hout = tot[:, :d] / denom
        hout = hout * lax.rsqrt(jnp.mean(hout * hout, axis=-1, keepdims=True) + NORM_EPS) * nw_ref[...]
        o_ref[bb, :, cols] = (hout * _sigmoid(op_ref[bb, :, cols])).astype(o_ref.dtype)


def _mlstm_call(cproj, gi, gf, conv_w, conv_b, nw, batch, seq):
    t = cproj.shape[0]
    d = ML_D
    nh = ML_HEADS
    L = ML_CHUNK
    nc = seq // L
    cp3 = cproj.reshape(batch, seq, cproj.shape[1])
    spec = lambda off: pl.BlockSpec((batch, L, nh * d), lambda i: (0, i, off))
    gspec = pl.BlockSpec((batch, nh, nc, L), lambda i: (0, 0, 0, 0))
    out = pl.pallas_call(
        _mlstm_kernel,
        grid=(nc,),
        in_specs=[spec(0), spec(1), spec(2), spec(3), gspec, gspec,
                  pl.BlockSpec((ML_CONV, nh * d), lambda i: (0, 0)),
                  pl.BlockSpec((ML_CONV, nh * d), lambda i: (0, 1)),
                  pl.BlockSpec((1, nh * d), lambda i: (0, 0)),
                  pl.BlockSpec((1, nh * d), lambda i: (0, 1)),
                  pl.BlockSpec((1, d), lambda i: (0, 0))],
        out_specs=pl.BlockSpec((batch, L, nh * d), lambda i: (0, i, 0)),
        out_shape=jax.ShapeDtypeStruct((batch, seq, nh * d), BF16),
        scratch_shapes=[pltpu.VMEM((batch, L + 8, nh * d), F32),
                        pltpu.VMEM((batch, L + 8, nh * d), F32),
                        pltpu.VMEM((batch * nh, d, 2 * d), F32),
                        pltpu.VMEM((batch * nh, 1, 1), F32),
                        pltpu.VMEM((batch * nh, nc, L), F32)],
        compiler_params=_cparams(("arbitrary",)),
        name="mlstm",
    )(cp3, cp3, cp3, cp3, gi, gf, conv_w, conv_w, conv_b, conv_b, nw)
    return out.reshape(t, nh * d)


def _top2_combine(logits):
    lane = lax.broadcasted_iota(jnp.int32, logits.shape, 1)
    lg = jnp.where(lane < N_EXPERTS, logits, -jnp.inf)
    ex = jnp.exp(lg - jnp.max(lg, axis=1, keepdims=True))
    probs = ex / jnp.sum(ex, axis=1, keepdims=True)
    p1 = jnp.max(probs, axis=1, keepdims=True)
    i1 = jnp.min(jnp.where(probs == p1, lane, LANES), axis=1, keepdims=True)
    rest = jnp.where(lane == i1, -1.0, probs)
    p2 = jnp.max(rest, axis=1, keepdims=True)
    i2 = jnp.min(jnp.where(rest == p2, lane, LANES), axis=1, keepdims=True)
    comb = jnp.where(lane == i1, p1, 0.0) + jnp.where(lane == i2, p2, 0.0)
    return comb / (p1 + p2)


def _merge_kernel(*refs, route):
    if route:
        (ya_ref, yb_ref, yc_ref, gp_ref, wb_ref, wo_ref, x_ref, mod_ref, nw_ref,
         rw_ref, rb_ref, xo_ref, h_ref, cmb_ref) = refs
    else:
        (ya_ref, yb_ref, yc_ref, gp_ref, wb_ref, wo_ref, x_ref, mod_ref, nw_ref,
         xo_ref, h_ref) = refs
    d = x_ref.shape[1]
    merged = None
    for n, y_ref in enumerate((ya_ref, yb_ref, yc_ref)):
        gate = _sigmoid(gp_ref[:, n * d:(n + 1) * d].astype(F32))
        term = gate * _dot(y_ref[...], wb_ref[n])
        merged = term if merged is None else merged + term
    m = mod_ref[0]
    xn = x_ref[...] + m[2:3] * _dot(merged.astype(BF16), wo_ref[...])
    xo_ref[...] = xn
    h2 = _norm_mod(xn, nw_ref[...], m[4:5], m[3:4])
    h_ref[...] = h2.astype(BF16)
    if route:
        h_hi, h_mid, _ = _split3(h2)
        r_hi, r_mid, _ = _split3(rw_ref[...])
        logits = _dot(h_hi, r_hi) + _dot(h_mid, r_hi) + _dot(h_hi, r_mid) + rb_ref[...]
        cmb_ref[...] = _top2_combine(logits)


def _merge_call(ya, yb, yc, gp, wb, wo, x, mod, nw, router, seq, tm):
    t, d = x.shape
    bw = ya.shape[1]
    per_b = seq // tm
    route = router is not None
    tok = lambda w: pl.BlockSpec((tm, w), lambda i: (i, 0))
    const2 = lambda s: pl.BlockSpec(s, lambda i: (0, 0))
    in_specs = [tok(bw), tok(bw), tok(bw), tok(3 * d),
                pl.BlockSpec((3, bw, d), lambda i: (0, 0, 0)), const2((d, d)), tok(d),
                pl.BlockSpec((1, 6, d), lambda i: (i // per_b, 0, 0)), const2((1, d))]
    out_specs = [tok(d), tok(d)]
    out_shape = [jax.ShapeDtypeStruct((t, d), F32), jax.ShapeDtypeStruct((t, d), BF16)]
    args = [ya, yb, yc, gp, wb, wo, x, mod, nw]
    if route:
        in_specs += [const2((d, LANES)), const2((1, LANES))]
        out_specs.append(tok(LANES))
        out_shape.append(jax.ShapeDtypeStruct((t, LANES), F32))
        args += list(router)
    return pl.pallas_call(
        functools.partial(_merge_kernel, route=route),
        grid=(t // tm,),
        in_specs=in_specs, out_specs=out_specs, out_shape=out_shape,
        compiler_params=_cparams(("arbitrary",)),
        name="merge_out",
    )(*args)


def _finish(acc, x_ref, mod_ref, xo_ref, nxt):
    xn = x_ref[...] + mod_ref[0][5:6] * acc
    xo_ref[...] = xn
    if nxt is not None:
        nw_ref, modn_ref, hn_ref = nxt
        mn = modn_ref[0]
        hn_ref[...] = _norm_mod(xn, nw_ref[...], mn[1:2], mn[0:1]).astype(BF16)


def _ffn_kernel(*refs, has_next):
    if has_next:
        h_ref, w1_ref, w3_ref, w2_ref, x_ref, mod_ref, nw_ref, modn_ref, xo_ref, hn_ref, acc_ref = refs
        nxt = (nw_ref, modn_ref, hn_ref)
    else:
        h_ref, w1_ref, w3_ref, w2_ref, x_ref, mod_ref, xo_ref, acc_ref = refs
        nxt = None
    f = pl.program_id(1)

    @pl.when(f == 0)
    def _():
        acc_ref[...] = jnp.zeros(acc_ref.shape, F32)

    h = h_ref[...]
    a = _dot(h, w1_ref[...])
    act = a * _sigmoid(a) * _dot(h, w3_ref[...])
    acc_ref[...] += _dot(act.astype(BF16), w2_ref[...])

    @pl.when(f == pl.num_programs(1) - 1)
    def _():
        _finish(acc_ref[...], x_ref, mod_ref, xo_ref, nxt)


def _ffn_call(h, w1, w3, w2, x, mod, nxt, seq, tm, tf):
    t, d = x.shape
    ff = w1.shape[1]
    per_b = seq // tm
    tok = lambda: pl.BlockSpec((tm, d), lambda i, f: (i, 0))
    modspec = lambda: pl.BlockSpec((1, 6, d), lambda i, f: (i // per_b, 0, 0))
    in_specs = [tok(), pl.BlockSpec((d, tf), lambda i, f: (0, f)),
                pl.BlockSpec((d, tf), lambda i, f: (0, f)),
                pl.BlockSpec((tf, d), lambda i, f: (f, 0)), tok(), modspec()]
    out_specs = [tok()]
    out_shape = [jax.ShapeDtypeStruct((t, d), F32)]
    args = [h, w1, w3, w2, x, mod]
    if nxt is not None:
        in_specs += [pl.BlockSpec((1, d), lambda i, f: (0, 0)), modspec()]
        out_specs.append(tok())
        out_shape.append(jax.ShapeDtypeStruct((t, d), BF16))
        args += list(nxt)
    return pl.pallas_call(
        functools.partial(_ffn_kernel, has_next=nxt is not None),
        grid=(t // tm, ff // tf),
        in_specs=in_specs, out_specs=out_specs, out_shape=out_shape,
        scratch_shapes=[pltpu.VMEM((tm, d), F32)],
        compiler_params=_cparams(("arbitrary", "arbitrary")),
        name="ffn_swiglu",
    )(*args)


def _moe_kernel(h_ref, cmb_ref, w13_ref, w2_ref, y_ref,
                acc_ref, xg_ref, ya_ref, rk_ref, rkt_ref, cnt_ref):
    e = pl.program_id(1)
    f = pl.program_id(2)
    tm = h_ref.shape[0]

    @pl.when((e == 0) & (f == 0))
    def _():
        r = lax.broadcasted_iota(jnp.int32, (tm, tm), 0)
        c = lax.broadcasted_iota(jnp.int32, (tm, tm), 1)
        before = jnp.where(c < r, 1.0, 0.0).astype(BF16)
        sel = cmb_ref[...] > 0.0
        rank = _dot(before, jnp.where(sel, 1.0, 0.0).astype(BF16))
        rk = jnp.where(sel, rank, -1.0)
        rk_ref[...] = rk
        rkt_ref[...] = rk.T
        cnt_ref[...] = jnp.sum(jnp.where(sel, 1.0, 0.0), axis=0, keepdims=True)
        acc_ref[...] = jnp.zeros(acc_ref.shape, F32)

    lane1 = lax.broadcasted_iota(jnp.int32, (1, LANES), 1)
    n_e = jnp.sum(jnp.where(lane1 == e, cnt_ref[...], 0.0)).astype(jnp.int32)

    def loop(n_blocks, body):
        if isinstance(n_blocks, int):
            for sb in range(n_blocks):
                body(sb, 0)
        else:
            lax.fori_loop(0, n_blocks, body, 0)

    def run(cap, n_blocks):
        cap_pad = -(-cap // LANES) * LANES

        def row0(sb):
            return sb * cap if isinstance(sb, int) else pl.multiple_of(sb * cap, cap)

        @pl.when(f == 0)
        def _():
            rank_row = rkt_ref[pl.ds(e, 1), :]

            def gather(sb, carry):
                r0 = row0(sb)
                slot = (r0 + lax.broadcasted_iota(jnp.int32, (cap, tm), 0)).astype(F32)
                onehot = jnp.where(rank_row == slot, 1.0, 0.0).astype(BF16)
                xg_ref[pl.ds(r0, cap), :] = _dot(onehot, h_ref[...]).astype(BF16)
                return carry

            loop(n_blocks, gather)

        def expert(sb, carry):
            r0 = row0(sb)
            xs = xg_ref[pl.ds(r0, cap), :]
            ab = _dot(xs, w13_ref[0, 0])
            tf = w2_ref.shape[1]
            a = ab[:, :tf]
            act = a * _sigmoid(a) * ab[:, tf:]
            part = _dot(act.astype(BF16), w2_ref[0])

            @pl.when(f == 0)
            def _():
                ya_ref[pl.ds(r0, cap), :] = part

            @pl.when(f > 0)
            def _():
                ya_ref[pl.ds(r0, cap), :] += part

            return carry

        loop(n_blocks, expert)

        @pl.when(f == pl.num_programs(2) - 1)
        def _():
            lane = lax.broadcasted_iota(jnp.int32, (tm, LANES), 1)
            rank_col = jnp.sum(jnp.where(lane == e, rk_ref[...], 0.0), axis=1, keepdims=True)
            w_col = jnp.sum(jnp.where(lane == e, cmb_ref[...], 0.0), axis=1, keepdims=True)
            col = lax.broadcasted_iota(jnp.int32, (tm, cap_pad), 1)

            def scatter(sb, carry):
                r0 = row0(sb)
                ys = ya_ref[pl.ds(r0, cap), :].astype(BF16)
                if cap_pad > cap:
                    ys = jnp.concatenate([ys, jnp.zeros((cap_pad - cap, ys.shape[1]), BF16)], axis=0)
                hit = (rank_col == (r0 + col).astype(F32)) & (col < cap)
                acc_ref[...] += w_col * _dot(jnp.where(hit, 1.0, 0.0).astype(BF16), ys)
                return carry

            loop(n_blocks, scatter)

    for idx, cap in enumerate(MOE_CAPS):
        lower = MOE_CAPS[idx - 1] if idx else 0
        if idx == len(MOE_CAPS) - 1:
            pl.when(n_e > lower)(functools.partial(run, cap, (n_e + cap - 1) // cap))
        else:
            pl.when((n_e > lower) & (n_e <= cap))(functools.partial(run, cap, 1))

    @pl.when((e == pl.num_programs(1) - 1) & (f == pl.num_programs(2) - 1))
    def _():
        y_ref[...] = acc_ref[...].astype(y_ref.dtype)


def _moe_pack_w13(w1, w3):
    ne, d, ff = w1.shape
    tf = _pick(ff, (MOE_TF,))
    split = lambda w: w.reshape(ne, d, ff // tf, tf).transpose(0, 2, 1, 3)
    return jnp.concatenate([split(w1), split(w3)], axis=-1).astype(BF16)


def _moe_call(h, cmb, w13, w2, tm):
    t, d = h.shape
    ne, ff, _ = w2.shape
    tf = w13.shape[-1] // 2
    rows = -(-tm // MOE_CAPS[-1]) * MOE_CAPS[-1]
    return pl.pallas_call(
        _moe_kernel,
        grid=(t // tm, ne, ff // tf),
        in_specs=[pl.BlockSpec((tm, d), lambda i, e, f: (i, 0)),
                  pl.BlockSpec((tm, LANES), lambda i, e, f: (i, 0)),
                  pl.BlockSpec((1, 1, d, 2 * tf), lambda i, e, f: (e, f, 0, 0)),
                  pl.BlockSpec((1, tf, d), lambda i, e, f: (e, f, 0))],
        out_specs=pl.BlockSpec((tm, d), lambda i, e, f: (i, 0)),
        out_shape=jax.ShapeDtypeStruct((t, d), BF16),
        scratch_shapes=[pltpu.VMEM((tm, d), F32),
                        pltpu.VMEM((rows, d), BF16),
                        pltpu.VMEM((rows, d), F32),
                        pltpu.VMEM((tm, LANES), F32),
                        pltpu.VMEM((LANES, tm), F32),
                        pltpu.VMEM((1, LANES), F32)],
        compiler_params=_cparams(("arbitrary", "arbitrary", "arbitrary")),
        name="moe_top2",
    )(h, cmb, w13, w2)


def _resid_kernel(*refs, has_next):
    if has_next:
        y_ref, x_ref, mod_ref, nw_ref, modn_ref, xo_ref, hn_ref = refs
        nxt = (nw_ref, modn_ref, hn_ref)
    else:
        y_ref, x_ref, mod_ref, xo_ref = refs
        nxt = None
    _finish(y_ref[...].astype(F32), x_ref, mod_ref, xo_ref, nxt)


def _resid_call(y, x, mod, nxt, seq, tm):
    t, d = x.shape
    per_b = seq // tm
    tok = lambda: pl.BlockSpec((tm, d), lambda i: (i, 0))
    modspec = lambda: pl.BlockSpec((1, 6, d), lambda i: (i // per_b, 0, 0))
    in_specs = [tok(), tok(), modspec()]
    out_specs = [tok()]
    out_shape = [jax.ShapeDtypeStruct((t, d), F32)]
    args = [y, x, mod]
    if nxt is not None:
        in_specs += [pl.BlockSpec((1, d), lambda i: (0, 0)), modspec()]
        out_specs.append(tok())
        out_shape.append(jax.ShapeDtypeStruct((t, d), BF16))
        args += list(nxt)
    return pl.pallas_call(
        functools.partial(_resid_kernel, has_next=nxt is not None),
        grid=(t // tm,),
        in_specs=in_specs, out_specs=out_specs, out_shape=out_shape,
        compiler_params=_cparams(("arbitrary",)),
        name="moe_residual",
    )(*args)


def _attn_col_perm():
    hq = DA_HEADS * DA_DK
    idx = []
    for base in (0, 2 * hq):
        for h in range(DA_HEADS):
            idx += list(range(base + h * DA_DK, base + (h + 1) * DA_DK))
            idx += list(range(base + hq + h * DA_DK, base + hq + (h + 1) * DA_DK))
    idx += list(range(4 * hq, 4 * hq + DA_HEADS * DA_DV))
    return np.asarray(idx, np.int32)


def _pick(n, pref):
    for c in pref:
        if n % c == 0:
            return c
    return n


def kernel(x, c, w_mod, b_mod, norm1_w, norm2_w, w_in, c_conv_w, c_conv_b, a_qnorm_w, a_knorm_w,
           a_lambda_q1, a_lambda_k1, a_lambda_q2, a_lambda_k2, a_subln_w, b_lb_logits, b_gnorm_w,
           c_igate_b, c_fgate_b, c_norm_w, w_branch, w_out, ffn_w1, ffn_w3, ffn_w2,
           moe_router_w, moe_router_b, moe_w1, moe_w3, moe_w2):
    batch, seq, d = x.shape
    depth = w_in.shape[0]
    t = batch * seq
    n_a = 4 * DA_HEADS * DA_DK + DA_HEADS * DA_DV
    n_b = 4 * HG_HEADS * HG_D
    n_c = 4 * ML_HEADS * ML_D + 2 * ML_HEADS
    n_c_pad = -(-n_c // LANES) * LANES
    tm = _pick(seq, (1024, 512, 256))
    tq = _pick(seq, (ATT_TQ,))
    ts_h = _pick(seq, (512, 256, 128))

    c8 = jnp.zeros((8, d), F32).at[:batch].set(c)
    mod = _mod_call(c8, w_mod, b_mod.reshape(depth, 1, 6 * d))[:, :batch].reshape(depth, batch, 6, d)

    lb_all = jnp.cumsum(jax.nn.softmax(b_lb_logits.astype(F32), axis=0), axis=0)
    lb_all = lb_all - lb_all[:1]
    slopes = jnp.asarray(_attn_slopes(), F32)

    xf = x.reshape(t, d)
    h = _normmod_call(xf, norm1_w[0].reshape(1, d), mod[0], seq, tm)
    perm_a = _attn_col_perm()
    for l in range(depth):
        lam_init = 0.8 - 0.6 * math.exp(-0.3 * l)
        wl = w_in[l]
        w_a = wl[:, :n_a][:, perm_a].astype(BF16)
        w_b = wl[:, n_a:n_a + n_b].astype(BF16)
        w_c = jnp.pad(wl[:, n_a + n_b:n_a + n_b + n_c], ((0, 0), (0, n_c_pad - n_c))).astype(BF16)
        w_g = wl[:, n_a + n_b + n_c:].astype(BF16)
        bias_c = jnp.zeros((1, n_c_pad), F32)
        bias_c = bias_c.at[0, n_c - 2 * ML_HEADS:n_c - ML_HEADS].set(c_igate_b[l])
        bias_c = bias_c.at[0, n_c - ML_HEADS:n_c].set(c_fgate_b[l])
        pb = _matmul(h, w_b, jnp.zeros((1, n_b), F32), F32, tm, n_b // 2)
        pc = _matmul(h, w_c, bias_c, F32, tm, n_c_pad)
        pg = _matmul(h, w_g, jnp.zeros((1, 3 * d), F32), BF16, tm, 3 * d // 2)

        qw = (jnp.tile(a_qnorm_w[l], 2 * DA_HEADS) * (DA_DK ** -0.5 * LOG2E)).reshape(1, -1)
        kw = jnp.tile(a_knorm_w[l], 2 * DA_HEADS).reshape(1, -1)
        kn, qt, vt = _attn_prep_call(h, w_a, qw, kw, batch, seq, _pick(seq, (512, 256)))
        lam = (jnp.exp(jnp.sum(a_lambda_q1[l] * a_lambda_k1[l]))
               - jnp.exp(jnp.sum(a_lambda_q2[l] * a_lambda_k2[l])) + lam_init)
        scal = jnp.concatenate([slopes, jnp.stack([lam, jnp.asarray(1.0 - lam_init, F32)])]).astype(F32)
        y_a = _attn_call(scal, qt, kn, vt, a_subln_w[l].reshape(1, -1), batch, seq, tq)

        lb = lb_all[l].reshape(HG_HEADS, 1, HG_D)
        lbs = jnp.concatenate([jnp.log(lb), jnp.log1p(-lb)], axis=1)
        y_b = _hgrn_call(pb, lbs, b_gnorm_w[l].reshape(1, -1), batch, seq, ts_h)

        gates = pc[:, n_c - 2 * ML_HEADS:n_c].reshape(batch, seq, 2, ML_HEADS)
        gates = gates.transpose(2, 0, 3, 1).reshape(2, batch, ML_HEADS, seq // ML_CHUNK, ML_CHUNK)
        y_c = _mlstm_call(pc, gates[0], gates[1], c_conv_w[l], c_conv_b[l].reshape(1, -1),
                          c_norm_w[l].reshape(1, -1), batch, seq)

        dense = l % 2 == 0
        router = None
        if not dense:
            rw = jnp.pad(moe_router_w[l // 2], ((0, 0), (0, LANES - N_EXPERTS)))
            rb = jnp.pad(moe_router_b[l // 2], (0, LANES - N_EXPERTS)).reshape(1, LANES)
            router = (rw, rb)
        outs = _merge_call(y_a, y_b, y_c, pg, w_branch[l].astype(BF16), w_out[l].astype(BF16), xf,
                           mod[l], norm2_w[l].reshape(1, d), router, seq, 512)
        xf, h2 = outs[0], outs[1]
        nxt = None if l == depth - 1 else (norm1_w[l + 1].reshape(1, d), mod[l + 1])
        if dense:
            res = _ffn_call(h2, ffn_w1[l // 2].astype(BF16), ffn_w3[l // 2].astype(BF16),
                            ffn_w2[l // 2].astype(BF16), xf, mod[l], nxt, seq, tm, FFN_TF)
        else:
            y = _moe_call(h2, outs[2], _moe_pack_w13(moe_w1[l // 2], moe_w3[l // 2]),
                          moe_w2[l // 2].astype(BF16), tm)
            res = _resid_call(y, xf, mod[l], nxt, seq, tm)
        xf = res[0]
        if nxt is not None:
            h = res[1]
    return xf.reshape(batch, seq, d)
```

```python
import functools
import math

import numpy as np
import jax
import jax.numpy as jnp
from jax import lax
from jax.experimental import pallas as pl
from jax.experimental.pallas import tpu as pltpu

F32 = jnp.float32
BF16 = jnp.bfloat16
NORM_EPS = 1e-6

CHUNK = 64
DA_HEADS, DA_DK, DA_DV = 4, 64, 128
HG_HEADS, HG_D = 4, 128
ML_HEADS, ML_D, ML_CONV = 4, 128, 4
N_EXPERTS = 8

LANES = 128
VMEM_LIMIT = 56 * 1024 * 1024

ATT_TQ = 256
ATT_HEADS_PER_STEP = 2
ATT_PAD_ROWS = 16
LOG2E = math.log2(math.e)
ATT_BOUND_MARGIN = 1.01
ATT_BOUND_EPS = 1e-3
ATT_BOUND_MAX = 40.0
ATT_ZERO_EXP = 160.0
HG_CHUNK = 128
HG_SUB = 32
HG_SAFE_DECAY = 80.0
ML_CHUNK = 256
MOE_CAPS = (256, 320, 384)
MOE_TF = 1408
FFN_TF = 256


def _cparams(sem, vmem=VMEM_LIMIT):
    return pltpu.CompilerParams(dimension_semantics=sem, vmem_limit_bytes=vmem)


def _dot(a, b):
    return jnp.dot(a, b, preferred_element_type=F32)


def _dot_nt(a, b):
    return lax.dot_general(a, b, (((1,), (1,)), ((), ())), preferred_element_type=F32)


def _split3(x):
    hi = x.astype(BF16)
    r1 = x - hi.astype(F32)
    mid = r1.astype(BF16)
    lo = (r1 - mid.astype(F32)).astype(BF16)
    return hi, mid, lo


def _norm_mod(x, w, scale, shift):
    y = x * lax.rsqrt(jnp.mean(x * x, axis=-1, keepdims=True) + NORM_EPS) * w
    return y * (1.0 + scale) + shift


def _sigmoid(x):
    return 0.5 * jnp.tanh(0.5 * x) + 0.5


def _log1p_exp_neg(z):
    return jnp.log(1.0 + jnp.exp(-z))


def _log_sigmoid(x):
    return jnp.minimum(x, 0.0) - _log1p_exp_neg(jnp.abs(x))


def _mod_kernel(c_ref, w_ref, b_ref, o_ref):
    cnd = c_ref[...]
    cnd = cnd * _sigmoid(cnd)
    o_ref[0] = _dot(cnd.astype(BF16), w_ref[0].astype(BF16)) + b_ref[0]


def _mod_call(c8, w_mod, b_mod):
    depth, d, n = w_mod.shape
    tn = 1536 if n % 1536 == 0 else n
    return pl.pallas_call(
        _mod_kernel,
        grid=(depth, n // tn),
        in_specs=[pl.BlockSpec((8, d), lambda l, j: (0, 0)),
                  pl.BlockSpec((1, d, tn), lambda l, j: (l, 0, j)),
                  pl.BlockSpec((1, 1, tn), lambda l, j: (l, 0, j))],
        out_specs=pl.BlockSpec((1, 8, tn), lambda l, j: (l, 0, j)),
        out_shape=jax.ShapeDtypeStruct((depth, 8, n), F32),
        compiler_params=_cparams(("arbitrary", "arbitrary")),
        name="adaln_mod",
    )(c8, w_mod, b_mod)


def _normmod_kernel(x_ref, nw_ref, mod_ref, h_ref):
    m = mod_ref[0]
    h_ref[...] = _norm_mod(x_ref[...], nw_ref[...], m[1:2], m[0:1]).astype(BF16)


def _normmod_call(x, nw, mod, seq, tm):
    t, d = x.shape
    per_b = seq // tm
    return pl.pallas_call(
        _normmod_kernel,
        grid=(t // tm,),
        in_specs=[pl.BlockSpec((tm, d), lambda i: (i, 0)),
                  pl.BlockSpec((1, d), lambda i: (0, 0)),
                  pl.BlockSpec((1, 6, d), lambda i: (i // per_b, 0, 0))],
        out_specs=pl.BlockSpec((tm, d), lambda i: (i, 0)),
        out_shape=jax.ShapeDtypeStruct((t, d), BF16),
        compiler_params=_cparams(("arbitrary",)),
        name="prenorm_mod",
    )(x, nw, mod)


def _mm_kernel(x_ref, w_ref, b_ref, o_ref):
    o_ref[...] = (_dot(x_ref[...], w_ref[...]) + b_ref[...]).astype(o_ref.dtype)


def _matmul(x, w, bias, out_dtype, tm, tn):
    t, k = x.shape
    n = w.shape[1]
    return pl.pallas_call(
        _mm_kernel,
        grid=(n // tn, t // tm),
        in_specs=[pl.BlockSpec((tm, k), lambda j, i: (i, 0)),
                  pl.BlockSpec((k, tn), lambda j, i: (0, j)),
                  pl.BlockSpec((1, tn), lambda j, i: (0, j))],
        out_specs=pl.BlockSpec((tm, tn), lambda j, i: (i, j)),
        out_shape=jax.ShapeDtypeStruct((t, n), out_dtype),
        compiler_params=_cparams(("arbitrary", "arbitrary")),
        name="in_proj",
    )(x, w, bias)


def _attn_prep_kernel(x_ref, w_ref, qw_ref, kw_ref, kn_ref, qt_ref, vt_ref):
    hw = DA_HEADS * 2 * DA_DK
    a = _dot(x_ref[...], w_ref[...])
    r = lax.broadcasted_iota(jnp.int32, (hw, hw), 0) // DA_DK
    c = lax.broadcasted_iota(jnp.int32, (hw, hw), 1) // DA_DK
    group = jnp.where(r == c, 1.0, 0.0).astype(BF16)

    def qk_norm(z, w):
        ms = _dot((z * z).astype(BF16), group) * (1.0 / DA_DK)
        return z * lax.rsqrt(ms + NORM_EPS) * w

    qn = qk_norm(a[:, :hw], qw_ref[...])
    kn = qk_norm(a[:, hw:2 * hw], kw_ref[...])
    kn_ref[...] = kn.astype(BF16)
    qt_ref[0] = qn.T.astype(BF16)
    vt_ref[0] = a[:, 2 * hw:].T.astype(BF16)


def _attn_prep_call(x, w, qw, kw, batch, seq, ts):
    t, d = x.shape
    hw = DA_HEADS * 2 * DA_DK
    hv = DA_HEADS * DA_DV
    per_b = seq // ts
    return pl.pallas_call(
        _attn_prep_kernel,
        grid=(t // ts,),
        in_specs=[pl.BlockSpec((ts, d), lambda i: (i, 0)),
                  pl.BlockSpec((d, 2 * hw + hv), lambda i: (0, 0)),
                  pl.BlockSpec((1, hw), lambda i: (0, 0)),
                  pl.BlockSpec((1, hw), lambda i: (0, 0))],
        out_specs=[pl.BlockSpec((ts, hw), lambda i: (i, 0)),
                   pl.BlockSpec((1, hw, ts), lambda i: (i // per_b, 0, i % per_b)),
                   pl.BlockSpec((1, hv, ts), lambda i: (i // per_b, 0, i % per_b))],
        out_shape=[jax.ShapeDtypeStruct((t, hw), BF16),
                   jax.ShapeDtypeStruct((batch, hw, seq), BF16),
                   jax.ShapeDtypeStruct((batch, hv, seq), BF16)],
        compiler_params=_cparams(("arbitrary",)),
        name="attn_prep",
    )(x, w, qw, kw)


def _attn_slopes():
    return [LOG2E * 2.0 ** (-8.0 * (h + 1) / DA_HEADS) for h in range(DA_HEADS)]


def _attn_window(head, tq):
    return int((ATT_ZERO_EXP / _attn_slopes()[head] - 1.0) // tq) + 1


def _attn_kernel(sc_ref, qt_ref, k_ref, vt_ref, sw_ref, o_ref,
                 qz_ref, m_ref, acc_ref, vs_ref, kmax_ref, s_ref, s2_ref, *, tq, seq):
    hp = ATT_HEADS_PER_STEP
    dv = DA_DV
    g = pl.program_id(1)
    i = pl.program_id(2)
    lam = sc_ref[DA_HEADS]
    out_scale = sc_ref[DA_HEADS + 1]
    slopes = [sc_ref[hp * g + hh] for hh in range(hp)]
    row16 = lax.broadcasted_iota(jnp.int32, (ATT_PAD_ROWS, tq), 0)
    ones_rows = jnp.where(row16 == 0, 1.0, 0.0).astype(BF16)

    half = lax.broadcasted_iota(jnp.int32, (1, 2 * DA_DK), 1) < DA_DK

    @pl.when(i == 0)
    def _():
        pos = lax.broadcasted_iota(jnp.int32, (1, seq), 1) & (tq - 1)
        rel = (pos - (tq - 1)).astype(F32)
        sub16 = lax.broadcasted_iota(jnp.int32, (ATT_PAD_ROWS, seq), 0)
        gr = lax.broadcasted_iota(jnp.int32, (2 * DA_DK, 2 * DA_DK), 0) // DA_DK
        gc = lax.broadcasted_iota(jnp.int32, (2 * DA_DK, 2 * DA_DK), 1) // DA_DK
        group = jnp.where(gr == gc, 1.0, 0.0).astype(BF16)
        for hh in range(hp):
            w = jnp.exp2(slopes[hh] * rel)
            vs_ref[hh, 0:dv, :] = (vt_ref[0, hh * dv:(hh + 1) * dv, :].astype(F32) * w).astype(BF16)
            vs_ref[hh, dv:dv + ATT_PAD_ROWS, :] = jnp.where(sub16 == 0, w, 0.0).astype(BF16)

            def knorm(n, best):
                kc = k_ref[pl.ds(pl.multiple_of(n * tq, tq), tq),
                           hh * 2 * DA_DK:(hh + 1) * 2 * DA_DK].astype(F32)
                return jnp.maximum(best, jnp.max(_dot((kc * kc).astype(BF16), group), axis=0, keepdims=True))

            k2 = lax.fori_loop(0, seq // tq, knorm, jnp.zeros((1, 2 * DA_DK), F32))
            kmax_ref[2 * hh] = jnp.max(jnp.where(half, k2, 0.0), axis=1, keepdims=True)
            kmax_ref[2 * hh + 1] = jnp.max(jnp.where(half, 0.0, k2), axis=1, keepdims=True)

    row = lax.broadcasted_iota(jnp.int32, (2 * DA_DK, tq), 0)
    qq_row = lax.broadcasted_iota(jnp.int32, (1, tq), 1).astype(F32)
    bound_max = jnp.zeros((), F32)
    firsts = []

    def scores(hh, blk):
        start = pl.multiple_of(blk * tq, tq)
        return _dot(k_ref[pl.ds(start, tq), hh * 2 * DA_DK:(hh + 1) * 2 * DA_DK], qz_ref[hh])

    for hh in range(hp):
        qt = qt_ref[0, hh * 2 * DA_DK:(hh + 1) * 2 * DA_DK, :]
        zero = jnp.zeros_like(qt)
        q1 = jnp.where(row < DA_DK, qt, zero)
        q2 = jnp.where(row >= DA_DK, qt, zero)
        qz_ref[hh] = jnp.concatenate([q1, q2], axis=1)
        window = jnp.int32(_attn_window(hh, tq))
        for grp in range(1, DA_HEADS // hp):
            window = jnp.where(g == grp, _attn_window(grp * hp + hh, tq), window)
        firsts.append((jnp.maximum(i - window, 0) // 2) * 2)
        s_ref[hh] = scores(hh, firsts[hh])
        s2_ref[hh] = scores(hh, jnp.minimum(firsts[hh] + 1, i))
        qn = jnp.concatenate(
            [jnp.sum(jnp.square(q1.astype(F32)), axis=0, keepdims=True) * kmax_ref[2 * hh],
             jnp.sum(jnp.square(q2.astype(F32)), axis=0, keepdims=True) * kmax_ref[2 * hh + 1]], axis=1)
        bound = jnp.sqrt(qn) * ATT_BOUND_MARGIN + ATT_BOUND_EPS
        bound_max = jnp.maximum(bound_max, jnp.max(bound))
        m_ref[hh] = bound + slopes[hh] * jnp.concatenate([qq_row, qq_row], axis=1)
    acc_ref[...] = jnp.zeros(acc_ref.shape, F32)

    kk = lax.broadcasted_iota(jnp.int32, (tq, 2 * tq), 0)
    cc = lax.broadcasted_iota(jnp.int32, (tq, 2 * tq), 1)
    qq = jnp.where(cc >= tq, cc - tq, cc)
    dist = (qq - jnp.abs(qq - kk)).astype(F32)
    visible = (kk // CHUNK) <= (qq // CHUNK)

    def diag_values(hh):
        start = pl.multiple_of(i * tq, tq)
        return jnp.concatenate([vt_ref[0, hh * dv:(hh + 1) * dv, pl.ds(start, tq)], ones_rows], axis=0)

    @pl.when(bound_max <= ATT_BOUND_MAX)
    def _():
        def probs(hh, blk, s):
            last_key = ((blk + 1 - i) * tq - 1).astype(F32)
            return jnp.exp2(s - (m_ref[hh] - slopes[hh] * last_key)).astype(BF16)

        def body(heads, jj, carry):
            blk = 2 * jj
            start = pl.multiple_of(blk * tq, 2 * tq)
            for hh in heads:
                p = jnp.concatenate([probs(hh, blk, s_ref[hh]), probs(hh, blk + 1, s2_ref[hh])], axis=0)
                s_ref[hh] = scores(hh, jnp.minimum(blk + 2, i))
                s2_ref[hh] = scores(hh, jnp.minimum(blk + 3, i))
                acc_ref[hh] += _dot(vs_ref[hh, :, pl.ds(start, 2 * tq)], p)
            return carry

        for hh in range(hp - 1, 0, -1):
            lax.fori_loop(firsts[hh] // 2, firsts[hh - 1] // 2,
                          functools.partial(body, tuple(range(hh, hp))), 0)
        lax.fori_loop(firsts[0] // 2, i // 2, functools.partial(body, tuple(range(hp))), 0)

        @pl.when(i % 2 == 1)
        def _():
            start = pl.multiple_of((i - 1) * tq, tq)
            for hh in range(hp):
                acc_ref[hh] += _dot(vs_ref[hh, :, pl.ds(start, tq)], probs(hh, i - 1, s_ref[hh]))
                s_ref[hh] = s2_ref[hh]

        for hh in range(hp):
            s = jnp.where(visible, s_ref[hh] + slopes[hh] * dist, -jnp.inf)
            acc_ref[hh] += _dot(diag_values(hh), jnp.exp2(s - m_ref[hh]).astype(BF16))

    @pl.when(bound_max > ATT_BOUND_MAX)
    def _():
        m_ref[...] = jnp.full(m_ref.shape, -jnp.inf, F32)

        def update(hh, s, bound_shift, v_aug):
            m_old = m_ref[hh]
            m_new = jnp.maximum(m_old, jnp.max(s, axis=0, keepdims=True) + bound_shift)
            p = jnp.exp2(s - (m_new - bound_shift))
            acc_ref[hh] = jnp.exp2(m_old - m_new) * acc_ref[hh] + _dot(v_aug, p.astype(BF16))
            m_ref[hh] = m_new

        def body(j, carry):
            start = pl.multiple_of(j * tq, tq)
            last_key = ((j + 1 - i) * tq - 1).astype(F32)
            for hh in range(hp):
                update(hh, scores(hh, j), slopes[hh] * last_key, vs_ref[hh, :, pl.ds(start, tq)])
            return carry

        lax.fori_loop(0, i, body, 0)
        for hh in range(hp):
            s = jnp.where(visible, scores(hh, i) + slopes[hh] * dist, -jnp.inf)
            update(hh, s, jnp.zeros((), F32), diag_values(hh))

    outs = []
    for hh in range(hp):
        acc = acc_ref[hh]
        o2 = acc[:dv, :] * (1.0 / acc[dv:dv + 1, :])
        o = (o2[:, :tq] - lam * o2[:, tq:]).T
        o = o * lax.rsqrt(jnp.mean(o * o, axis=-1, keepdims=True) + NORM_EPS) * sw_ref[...]
        outs.append(o * out_scale)
    o_ref[...] = jnp.concatenate(outs, axis=1).astype(o_ref.dtype)


def _attn_call(scal, qt, kn, vt, sw, batch, seq, tq):
    t = kn.shape[0]
    nq = seq // tq
    hp = ATT_HEADS_PER_STEP
    return pl.pallas_call(
        functools.partial(_attn_kernel, tq=tq, seq=seq),
        grid=(batch, DA_HEADS // hp, nq),
        in_specs=[pl.BlockSpec(memory_space=pltpu.SMEM),
                  pl.BlockSpec((1, hp * 2 * DA_DK, tq), lambda b, g, i: (b, g, i)),
                  pl.BlockSpec((seq, hp * 2 * DA_DK), lambda b, g, i: (b, g)),
                  pl.BlockSpec((1, hp * DA_DV, seq), lambda b, g, i: (b, g, 0)),
                  pl.BlockSpec((1, DA_DV), lambda b, g, i: (0, 0))],
        out_specs=pl.BlockSpec((tq, hp * DA_DV), lambda b, g, i: (b * nq + i, g)),
        out_shape=jax.ShapeDtypeStruct((t, DA_HEADS * DA_DV), BF16),
        scratch_shapes=[pltpu.VMEM((hp, 2 * DA_DK, 2 * tq), BF16),
                        pltpu.VMEM((hp, 1, 2 * tq), F32),
                        pltpu.VMEM((hp, DA_DV + ATT_PAD_ROWS, 2 * tq), F32),
                        pltpu.VMEM((hp, DA_DV + ATT_PAD_ROWS, seq), BF16),
                        pltpu.VMEM((2 * hp, 1, 1), F32),
                        pltpu.VMEM((hp, tq, 2 * tq), F32),
                        pltpu.VMEM((hp, tq, 2 * tq), F32)],
        compiler_params=_cparams(("arbitrary", "arbitrary", "arbitrary")),
        name="diff_attn",
    )(scal, qt, kn, vt, sw)


def _hgrn_kernel(q_ref, f_ref, i_ref, g_ref, lb_ref, gw_ref, o_ref,
                 st_ref, b_ref, qs_ref, ks_ref, oi_ref, *, n_chunks):
    c = HG_CHUNK
    sub = HG_SUB
    d = HG_D
    nh = HG_HEADS
    chains = [(bb, hh) for bb in range(q_ref.shape[0]) for hh in range(nh)]

    @pl.when(pl.program_id(0) == 0)
    def _():
        st_ref[...] = jnp.zeros(st_ref.shape, F32)

    gw = gw_ref[...]
    rr = lax.broadcasted_iota(jnp.int32, (c, c), 0)
    cc = lax.broadcasted_iota(jnp.int32, (c, c), 1)
    causal = cc <= rr
    tril = jnp.where(causal, 1.0, 0.0).astype(BF16)

    def chunk(n, carry):
        r0 = pl.multiple_of(n * c, c)
        gates = []
        for ch, (bb, hh) in enumerate(chains):
            cols = slice(hh * d, (hh + 1) * d)
            log_lb = lb_ref[hh, 0:1, :]
            q = q_ref[bb, pl.ds(r0, c), cols]
            q = q * _sigmoid(q)
            a = lb_ref[hh, 1:2, :] + _log_sigmoid(f_ref[bb, pl.ds(r0, c), cols])
            logf = jnp.maximum(log_lb, a) + _log1p_exp_neg(jnp.abs(log_lb - a))
            k = 1.0 - jnp.exp(logf)
            v = i_ref[bb, pl.ds(r0, c), cols]
            hi, mid, lo = _split3(logf)
            b = _dot(tril, hi) + _dot(tril, mid) + _dot(tril, lo)
            gates.append((q, k, v, b))

        heads = []
        decay = jnp.zeros((), F32)
        for ch, (q, k, v, b) in enumerate(gates):
            b_last = b[c - 1:c, :]
            st = st_ref[ch]
            o_inter = _dot_nt((q * jnp.exp(b)).astype(BF16), st.astype(BF16))
            k_hat = k * jnp.exp(b_last - b)
            st_ref[ch] = st * jnp.exp(b_last) + _dot(v.T.astype(BF16), k_hat.astype(BF16))
            betas = []
            for blk in range(c // sub):
                beta = jnp.zeros((1, d), F32) if blk == 0 else b[blk * sub - 1:blk * sub, :]
                betas.append(beta)
                b_end = b[(blk + 1) * sub - 1:(blk + 1) * sub, :]
                decay = jnp.maximum(decay, jnp.max(beta - b_end))
            heads.append((q, k, v, b, betas, o_inter))

        @pl.when(decay < HG_SAFE_DECAY)
        def _():
            atts = []
            for q, k, v, b, betas, _ in heads:
                rows = []
                for blk in range(c // sub):
                    beta = betas[blk]
                    q_t = q[blk * sub:(blk + 1) * sub, :] * jnp.exp(b[blk * sub:(blk + 1) * sub, :] - beta)
                    k_t = k * jnp.exp(jnp.minimum(beta - b, HG_SAFE_DECAY))
                    rows.append(_dot_nt(q_t.astype(BF16), k_t.astype(BF16)))
                atts.append(jnp.where(causal, jnp.concatenate(rows, axis=0), 0.0).astype(BF16))
            for ch, att in enumerate(atts):
                oi_ref[ch] = _dot(att, heads[ch][2].astype(BF16))

        @pl.when(decay >= HG_SAFE_DECAY)
        def _():
            ridx = lax.broadcasted_iota(jnp.int32, (c, d), 0)
            for ch, (q, k, v, b, _, _) in enumerate(heads):
                b_ref[...] = b
                qs_ref[...] = q
                ks_ref[...] = k

                def row(t, carry2):
                    bt = b_ref[pl.ds(t, 1), :]
                    e = jnp.exp(jnp.where(ridx <= t, bt - b_ref[...], -jnp.inf))
                    w = jnp.sum(qs_ref[pl.ds(t, 1), :] * ks_ref[...] * e, axis=1, keepdims=True)
                    oi_ref[ch, pl.ds(t, 1), :] = jnp.sum(w * v, axis=0, keepdims=True)
                    return carry2

                lax.fori_loop(0, c, row, 0)

        for ch, (bb, hh) in enumerate(chains):
            cols = slice(hh * d, (hh + 1) * d)
            o = heads[ch][5] + oi_ref[ch]
            o = o * lax.rsqrt(jnp.mean(o * o, axis=-1, keepdims=True) + NORM_EPS) * gw
            g = g_ref[bb, pl.ds(r0, c), cols]
            o_ref[bb, pl.ds(r0, c), cols] = (o * (g * _sigmoid(g))).astype(o_ref.dtype)
        return carry

    lax.fori_loop(0, n_chunks, chunk, 0)


def _hgrn_call(bproj, lbs, gw, batch, seq, ts):
    t = bproj.shape[0]
    d = HG_D
    nh = HG_HEADS
    bp3 = bproj.reshape(batch, seq, bproj.shape[1])
    spec = lambda off: pl.BlockSpec((batch, ts, nh * d), lambda i: (0, i, off))
    out = pl.pallas_call(
        functools.partial(_hgrn_kernel, n_chunks=ts // HG_CHUNK),
        grid=(seq // ts,),
        in_specs=[spec(0), spec(1), spec(2), spec(3),
                  pl.BlockSpec((nh, 2, d), lambda i: (0, 0, 0)),
                  pl.BlockSpec((1, d), lambda i: (0, 0))],
        out_specs=pl.BlockSpec((batch, ts, nh * d), lambda i: (0, i, 0)),
        out_shape=jax.ShapeDtypeStruct((batch, seq, nh * d), BF16),
        scratch_shapes=[pltpu.VMEM((batch * nh, d, d), F32),
                        pltpu.VMEM((HG_CHUNK, d), F32),
                        pltpu.VMEM((HG_CHUNK, d), F32),
                        pltpu.VMEM((HG_CHUNK, d), F32),
                        pltpu.VMEM((batch * nh, HG_CHUNK, d), F32)],
        compiler_params=_cparams(("arbitrary",)),
        name="hgrn2",
    )(bp3, bp3, bp3, bp3, lbs, gw)
    return out.reshape(t, nh * d)


def _mlstm_kernel(uq_ref, uk_ref, v_ref, op_ref, gi_ref, gf_ref, cwq_ref, cwk_ref,
                  cbq_ref, cbk_ref, nw_ref, o_ref,
                  xq_ref, xk_ref, c_ref, m_ref, bs_ref):
    L = ML_CHUNK
    d = ML_D
    nh = ML_HEADS
    nb = uq_ref.shape[0]
    i = pl.program_id(0)
    rr = lax.broadcasted_iota(jnp.int32, (L, L), 0)
    cc = lax.broadcasted_iota(jnp.int32, (L, L), 1)

    @pl.when(i == 0)
    def _():
        c_ref[...] = jnp.zeros(c_ref.shape, F32)
        m_ref[...] = jnp.zeros(m_ref.shape, F32)
        xq_ref[:, 0:8, :] = jnp.zeros((nb, 8, nh * d), F32)
        xk_ref[:, 0:8, :] = jnp.zeros((nb, 8, nh * d), F32)
        upper = jnp.where(rr <= cc, 1.0, 0.0).astype(BF16)
        for ch in range(nb * nh):
            hi, mid, lo = _split3(_log_sigmoid(gf_ref[ch // nh, ch % nh]))
            bs_ref[ch] = _dot(hi, upper) + _dot(mid, upper) + _dot(lo, upper)

    def conv_silu(bb, u_ref, x_ref, w_ref, b_ref):
        x_ref[bb, 8:8 + L, :] = u_ref[bb]
        y = b_ref[...] + w_ref[ML_CONV - 1:ML_CONV, :] * x_ref[bb, 8:8 + L, :]
        for j in range(ML_CONV - 1):
            y = y + w_ref[j:j + 1, :] * x_ref[bb, 5 + j:5 + j + L, :]
        x_ref[bb, 0:8, :] = x_ref[bb, L:L + 8, :]
        return y * _sigmoid(y)

    q_all = [conv_silu(bb, uq_ref, xq_ref, cwq_ref, cbq_ref) for bb in range(nb)]
    k_all = [conv_silu(bb, uk_ref, xk_ref, cwk_ref, cbk_ref) * (d ** -0.5) for bb in range(nb)]
    lane = lax.broadcasted_iota(jnp.int32, (L, d), 1)
    ones_col = jnp.where(lane == 0, 1.0, 0.0)
    stage1 = []
    for ch in range(nb * nh):
        bb, hh = ch // nh, ch % nh
        cols = slice(hh * d, (hh + 1) * d)
        q = q_all[bb][:, cols].astype(BF16)
        kt = k_all[bb][:, cols].T
        v_aug = jnp.concatenate([v_ref[bb, :, cols], ones_col], axis=1).astype(BF16)
        c_aug = c_ref[ch]
        qk = _dot(q, kt.astype(BF16))
        qc = _dot(q, c_aug.astype(BF16))

        b_row = bs_ref[ch, pl.ds(i, 1), :]
        ig_row = gi_ref[bb, hh, pl.ds(i, 1), :]
        m_prev = m_ref[ch]
        g = b_row[:, L - 1:L]
        log_w = g - b_row + ig_row
        m_new = jnp.maximum(g + m_prev, jnp.max(log_w, axis=1, keepdims=True))
        w_s = jnp.exp(log_w - m_new)
        c_ref[ch] = jnp.exp(g + m_prev - m_new) * c_aug + _dot((kt * w_s).astype(BF16), v_aug)
        m_ref[ch] = m_new
        stage1.append((qk, qc, v_aug, b_row, ig_row, m_prev))

    stage2 = []
    for qk, qc, v_aug, b_row, ig_row, m_prev in stage1:
        b_col = jnp.sum(jnp.where(rr == cc, b_row, 0.0), axis=1, keepdims=True)
        log_d = jnp.where(cc <= rr, b_col + (ig_row - b_row), -jnp.inf)
        log_inter = b_col + m_prev
        m_t = jnp.maximum(log_inter, jnp.max(log_d, axis=1, keepdims=True))
        w_intra = jnp.exp(log_d - m_t) * qk
        tot = jnp.exp(log_inter - m_t) * qc + _dot(w_intra.astype(BF16), v_aug)
        stage2.append((tot, m_t))

    for ch, (tot, m_t) in enumerate(stage2):
        bb, hh = ch // nh, ch % nh
        cols = slice(hh * d, (hh + 1) * d)
        denom = jnp.maximum(jnp.abs(tot[:, d:d + 1]), jnp.exp(-m_t))
        hout = tot[:, :d] / denom
        hout = hout * lax.rsqrt(jnp.mean(hout * hout, axis=-1, keepdims=True) + NORM_EPS) * nw_ref[...]
        o_ref[bb, :, cols] = (hout * _sigmoid(op_ref[bb, :, cols])).astype(o_ref.dtype)


def _mlstm_call(cproj, gi, gf, conv_w, conv_b, nw, batch, seq):
    t = cproj.shape[0]
    d = ML_D
    nh = ML_HEADS
    L = ML_CHUNK
    nc = seq // L
    cp3 = cproj.reshape(batch, seq, cproj.shape[1])
    spec = lambda off: pl.BlockSpec((batch, L, nh * d), lambda i: (0, i, off))
    gspec = pl.BlockSpec((batch, nh, nc, L), lambda i: (0, 0, 0, 0))
    out = pl.pallas_call(
        _mlstm_kernel,
        grid=(nc,),
        in_specs=[spec(0), spec(1), spec(2), spec(3), gspec, gspec,
                  pl.BlockSpec((ML_CONV, nh * d), lambda i: (0, 0)),
                  pl.BlockSpec((ML_CONV, nh * d), lambda i: (0, 1)),
                  pl.BlockSpec((1, nh * d), lambda i: (0, 0)),
                  pl.BlockSpec((1, nh * d), lambda i: (0, 1)),
                  pl.BlockSpec((1, d), lambda i: (0, 0))],
        out_specs=pl.BlockSpec((batch, L, nh * d), lambda i: (0, i, 0)),
        out_shape=jax.ShapeDtypeStruct((batch, seq, nh * d), BF16),
        scratch_shapes=[pltpu.VMEM((batch, L + 8, nh * d), F32),
                        pltpu.VMEM((batch, L + 8, nh * d), F32),
                        pltpu.VMEM((batch * nh, d, 2 * d), F32),
                        pltpu.VMEM((batch * nh, 1, 1), F32),
                        pltpu.VMEM((batch * nh, nc, L), F32)],
        compiler_params=_cparams(("arbitrary",)),
        name="mlstm",
    )(cp3, cp3, cp3, cp3, gi, gf, conv_w, conv_w, conv_b, conv_b, nw)
    return out.reshape(t, nh * d)


def _top2_combine(logits):
    lane = lax.broadcasted_iota(jnp.int32, logits.shape, 1)
    lg = jnp.where(lane < N_EXPERTS, logits, -jnp.inf)
    ex = jnp.exp(lg - jnp.max(lg, axis=1, keepdims=True))
    probs = ex / jnp.sum(ex, axis=1, keepdims=True)
    p1 = jnp.max(probs, axis=1, keepdims=True)
    i1 = jnp.min(jnp.where(probs == p1, lane, LANES), axis=1, keepdims=True)
    rest = jnp.where(lane == i1, -1.0, probs)
    p2 = jnp.max(rest, axis=1, keepdims=True)
    i2 = jnp.min(jnp.where(rest == p2, lane, LANES), axis=1, keepdims=True)
    comb = jnp.where(lane == i1, p1, 0.0) + jnp.where(lane == i2, p2, 0.0)
    return comb / (p1 + p2)


def _merge_kernel(*refs, route):
    if route:
        (ya_ref, yb_ref, yc_ref, gp_ref, wb_ref, wo_ref, x_ref, mod_ref, nw_ref,
         rw_ref, rb_ref, xo_ref, h_ref, cmb_ref) = refs
    else:
        (ya_ref, yb_ref, yc_ref, gp_ref, wb_ref, wo_ref, x_ref, mod_ref, nw_ref,
         xo_ref, h_ref) = refs
    d = x_ref.shape[1]
    merged = None
    for n, y_ref in enumerate((ya_ref, yb_ref, yc_ref)):
        gate = _sigmoid(gp_ref[:, n * d:(n + 1) * d].astype(F32))
        term = gate * _dot(y_ref[...], wb_ref[n])
        merged = term if merged is None else merged + term
    m = mod_ref[0]
    xn = x_ref[...] + m[2:3] * _dot(merged.astype(BF16), wo_ref[...])
    xo_ref[...] = xn
    h2 = _norm_mod(xn, nw_ref[...], m[4:5], m[3:4])
    h_ref[...] = h2.astype(BF16)
    if route:
        h_hi, h_mid, _ = _split3(h2)
        r_hi, r_mid, _ = _split3(rw_ref[...])
        logits = _dot(h_hi, r_hi) + _dot(h_mid, r_hi) + _dot(h_hi, r_mid) + rb_ref[...]
        cmb_ref[...] = _top2_combine(logits)


def _merge_call(ya, yb, yc, gp, wb, wo, x, mod, nw, router, seq, tm):
    t, d = x.shape
    bw = ya.shape[1]
    per_b = seq // tm
    route = router is not None
    tok = lambda w: pl.BlockSpec((tm, w), lambda i: (i, 0))
    const2 = lambda s: pl.BlockSpec(s, lambda i: (0, 0))
    in_specs = [tok(bw), tok(bw), tok(bw), tok(3 * d),
                pl.BlockSpec((3, bw, d), lambda i: (0, 0, 0)), const2((d, d)), tok(d),
                pl.BlockSpec((1, 6, d), lambda i: (i // per_b, 0, 0)), const2((1, d))]
    out_specs = [tok(d), tok(d)]
    out_shape = [jax.ShapeDtypeStruct((t, d), F32), jax.ShapeDtypeStruct((t, d), BF16)]
    args = [ya, yb, yc, gp, wb, wo, x, mod, nw]
    if route:
        in_specs += [const2((d, LANES)), const2((1, LANES))]
        out_specs.append(tok(LANES))
        out_shape.append(jax.ShapeDtypeStruct((t, LANES), F32))
        args += list(router)
    return pl.pallas_call(
        functools.partial(_merge_kernel, route=route),
        grid=(t // tm,),
        in_specs=in_specs, out_specs=out_specs, out_shape=out_shape,
        compiler_params=_cparams(("arbitrary",)),
        name="merge_out",
    )(*args)


def _finish(acc, x_ref, mod_ref, xo_ref, nxt):
    xn = x_ref[...] + mod_ref[0][5:6] * acc
    xo_ref[...] = xn
    if nxt is not None:
        nw_ref, modn_ref, hn_ref = nxt
        mn = modn_ref[0]
        hn_ref[...] = _norm_mod(xn, nw_ref[...], mn[1:2], mn[0:1]).astype(BF16)


def _ffn_kernel(*refs, has_next):
    if has_next:
        h_ref, w1_ref, w3_ref, w2_ref, x_ref, mod_ref, nw_ref, modn_ref, xo_ref, hn_ref, acc_ref = refs
        nxt = (nw_ref, modn_ref, hn_ref)
    else:
        h_ref, w1_ref, w3_ref, w2_ref, x_ref, mod_ref, xo_ref, acc_ref = refs
        nxt = None
    f = pl.program_id(1)

    @pl.when(f == 0)
    def _():
        acc_ref[...] = jnp.zeros(acc_ref.shape, F32)

    h = h_ref[...]
    a = _dot(h, w1_ref[...])
    act = a * _sigmoid(a) * _dot(h, w3_ref[...])
    acc_ref[...] += _dot(act.astype(BF16), w2_ref[...])

    @pl.when(f == pl.num_programs(1) - 1)
    def _():
        _finish(acc_ref[...], x_ref, mod_ref, xo_ref, nxt)


def _ffn_call(h, w1, w3, w2, x, mod, nxt, seq, tm, tf):
    t, d = x.shape
    ff = w1.shape[1]
    per_b = seq // tm
    tok = lambda: pl.BlockSpec((tm, d), lambda i, f: (i, 0))
    modspec = lambda: pl.BlockSpec((1, 6, d), lambda i, f: (i // per_b, 0, 0))
    in_specs = [tok(), pl.BlockSpec((d, tf), lambda i, f: (0, f)),
                pl.BlockSpec((d, tf), lambda i, f: (0, f)),
                pl.BlockSpec((tf, d), lambda i, f: (f, 0)), tok(), modspec()]
    out_specs = [tok()]
    out_shape = [jax.ShapeDtypeStruct((t, d), F32)]
    args = [h, w1, w3, w2, x, mod]
    if nxt is not None:
        in_specs += [pl.BlockSpec((1, d), lambda i, f: (0, 0)), modspec()]
        out_specs.append(tok())
        out_shape.append(jax.ShapeDtypeStruct((t, d), BF16))
        args += list(nxt)
    return pl.pallas_call(
        functools.partial(_ffn_kernel, has_next=nxt is not None),
        grid=(t // tm, ff // tf),
        in_specs=in_specs, out_specs=out_specs, out_shape=out_shape,
        scratch_shapes=[pltpu.VMEM((tm, d), F32)],
        compiler_params=_cparams(("arbitrary", "arbitrary")),
        name="ffn_swiglu",
    )(*args)


def _moe_kernel(h_ref, cmb_ref, w13_ref, w2_ref, y_ref,
                acc_ref, xg_ref, ya_ref, rk_ref, rkt_ref, cnt_ref):
    e = pl.program_id(1)
    f = pl.program_id(2)
    tm = h_ref.shape[0]

    @pl.when((e == 0) & (f == 0))
    def _():
        r = lax.broadcasted_iota(jnp.int32, (tm, tm), 0)
        c = lax.broadcasted_iota(jnp.int32, (tm, tm), 1)
        before = jnp.where(c < r, 1.0, 0.0).astype(BF16)
        sel = cmb_ref[...] > 0.0
        rank = _dot(before, jnp.where(sel, 1.0, 0.0).astype(BF16))
        rk = jnp.where(sel, rank, -1.0)
        rk_ref[...] = rk
        rkt_ref[...] = rk.T
        cnt_ref[...] = jnp.sum(jnp.where(sel, 1.0, 0.0), axis=0, keepdims=True)
        acc_ref[...] = jnp.zeros(acc_ref.shape, F32)

    lane1 = lax.broadcasted_iota(jnp.int32, (1, LANES), 1)
    n_e = jnp.sum(jnp.where(lane1 == e, cnt_ref[...], 0.0)).astype(jnp.int32)

    def loop(n_blocks, body):
        if isinstance(n_blocks, int):
            for sb in range(n_blocks):
                body(sb, 0)
        else:
            lax.fori_loop(0, n_blocks, body, 0)

    def run(cap, n_blocks):
        cap_pad = -(-cap // LANES) * LANES

        def row0(sb):
            return sb * cap if isinstance(sb, int) else pl.multiple_of(sb * cap, cap)

        @pl.when(f == 0)
        def _():
            rank_row = rkt_ref[pl.ds(e, 1), :]

            def gather(sb, carry):
                r0 = row0(sb)
                slot = (r0 + lax.broadcasted_iota(jnp.int32, (cap, tm), 0)).astype(F32)
                onehot = jnp.where(rank_row == slot, 1.0, 0.0).astype(BF16)
                xg_ref[pl.ds(r0, cap), :] = _dot(onehot, h_ref[...]).astype(BF16)
                return carry

            loop(n_blocks, gather)

        def expert(sb, carry):
            r0 = row0(sb)
            xs = xg_ref[pl.ds(r0, cap), :]
            ab = _dot(xs, w13_ref[0])
            tf = w2_ref.shape[1]
            a = ab[:, :tf]
            act = a * _sigmoid(a) * ab[:, tf:]
            part = _dot(act.astype(BF16), w2_ref[0])

            @pl.when(f == 0)
            def _():
                ya_ref[pl.ds(r0, cap), :] = part

            @pl.when(f > 0)
            def _():
                ya_ref[pl.ds(r0, cap), :] += part

            return carry

        loop(n_blocks, expert)

        @pl.when(f == pl.num_programs(2) - 1)
        def _():
            lane = lax.broadcasted_iota(jnp.int32, (tm, LANES), 1)
            rank_col = jnp.sum(jnp.where(lane == e, rk_ref[...], 0.0), axis=1, keepdims=True)
            w_col = jnp.sum(jnp.where(lane == e, cmb_ref[...], 0.0), axis=1, keepdims=True)
            col = lax.broadcasted_iota(jnp.int32, (tm, cap_pad), 1)

            def scatter(sb, carry):
                r0 = row0(sb)
                ys = ya_ref[pl.ds(r0, cap), :].astype(BF16)
                if cap_pad > cap:
                    ys = jnp.concatenate([ys, jnp.zeros((cap_pad - cap, ys.shape[1]), BF16)], axis=0)
                hit = (rank_col == (r0 + col).astype(F32)) & (col < cap)
                acc_ref[...] += w_col * _dot(jnp.where(hit, 1.0, 0.0).astype(BF16), ys)
                return carry

            loop(n_blocks, scatter)

    for idx, cap in enumerate(MOE_CAPS):
        lower = MOE_CAPS[idx - 1] if idx else 0
        if idx == len(MOE_CAPS) - 1:
            pl.when(n_e > lower)(functools.partial(run, cap, (n_e + cap - 1) // cap))
        else:
            pl.when((n_e > lower) & (n_e <= cap))(functools.partial(run, cap, 1))

    @pl.when((e == pl.num_programs(1) - 1) & (f == pl.num_programs(2) - 1))
    def _():
        y_ref[...] = acc_ref[...].astype(y_ref.dtype)


def _moe_pack_w13(w1, w3):
    ff = w1.shape[-1]
    tf = _pick(ff, (MOE_TF,))
    parts = []
    for f in range(ff // tf):
        parts += [w1[:, :, f * tf:(f + 1) * tf].astype(BF16), w3[:, :, f * tf:(f + 1) * tf].astype(BF16)]
    return jnp.concatenate(parts, axis=-1)


def _moe_call(h, cmb, w13, w2, tm):
    t, d = h.shape
    ne, ff, _ = w2.shape
    tf = _pick(ff, (MOE_TF,))
    rows = -(-tm // MOE_CAPS[-1]) * MOE_CAPS[-1]
    return pl.pallas_call(
        _moe_kernel,
        grid=(t // tm, ne, ff // tf),
        in_specs=[pl.BlockSpec((tm, d), lambda i, e, f: (i, 0)),
                  pl.BlockSpec((tm, LANES), lambda i, e, f: (i, 0)),
                  pl.BlockSpec((1, d, 2 * tf), lambda i, e, f: (e, 0, f)),
                  pl.BlockSpec((1, tf, d), lambda i, e, f: (e, f, 0))],
        out_specs=pl.BlockSpec((tm, d), lambda i, e, f: (i, 0)),
        out_shape=jax.ShapeDtypeStruct((t, d), BF16),
        scratch_shapes=[pltpu.VMEM((tm, d), F32),
                        pltpu.VMEM((rows, d), BF16),
                        pltpu.VMEM((rows, d), F32),
                        pltpu.VMEM((tm, LANES), F32),
                        pltpu.VMEM((LANES, tm), F32),
                        pltpu.VMEM((1, LANES), F32)],
        compiler_params=_cparams(("arbitrary", "arbitrary", "arbitrary")),
        name="moe_top2",
    )(h, cmb, w13, w2)


def _resid_kernel(*refs, has_next):
    if has_next:
        y_ref, x_ref, mod_ref, nw_ref, modn_ref, xo_ref, hn_ref = refs
        nxt = (nw_ref, modn_ref, hn_ref)
    else:
        y_ref, x_ref, mod_ref, xo_ref = refs
        nxt = None
    _finish(y_ref[...].astype(F32), x_ref, mod_ref, xo_ref, nxt)


def _resid_call(y, x, mod, nxt, seq, tm):
    t, d = x.shape
    per_b = seq // tm
    tok = lambda: pl.BlockSpec((tm, d), lambda i: (i, 0))
    modspec = lambda: pl.BlockSpec((1, 6, d), lambda i: (i // per_b, 0, 0))
    in_specs = [tok(), tok(), modspec()]
    out_specs = [tok()]
    out_shape = [jax.ShapeDtypeStruct((t, d), F32)]
    args = [y, x, mod]
    if nxt is not None:
        in_specs += [pl.BlockSpec((1, d), lambda i: (0, 0)), modspec()]
        out_specs.append(tok())
        out_shape.append(jax.ShapeDtypeStruct((t, d), BF16))
        args += list(nxt)
    return pl.pallas_call(
        functools.partial(_resid_kernel, has_next=nxt is not None),
        grid=(t // tm,),
        in_specs=in_specs, out_specs=out_specs, out_shape=out_shape,
        compiler_params=_cparams(("arbitrary",)),
        name="moe_residual",
    )(*args)


def _attn_col_perm():
    hq = DA_HEADS * DA_DK
    idx = []
    for base in (0, 2 * hq):
        for h in range(DA_HEADS):
            idx += list(range(base + h * DA_DK, base + (h + 1) * DA_DK))
            idx += list(range(base + hq + h * DA_DK, base + hq + (h + 1) * DA_DK))
    idx += list(range(4 * hq, 4 * hq + DA_HEADS * DA_DV))
    return np.asarray(idx, np.int32)


def _pick(n, pref):
    for c in pref:
        if n % c == 0:
            return c
    return n


def kernel(x, c, w_mod, b_mod, norm1_w, norm2_w, w_in, c_conv_w, c_conv_b, a_qnorm_w, a_knorm_w,
           a_lambda_q1, a_lambda_k1, a_lambda_q2, a_lambda_k2, a_subln_w, b_lb_logits, b_gnorm_w,
           c_igate_b, c_fgate_b, c_norm_w, w_branch, w_out, ffn_w1, ffn_w3, ffn_w2,
           moe_router_w, moe_router_b, moe_w1, moe_w3, moe_w2):
    batch, seq, d = x.shape
    depth = w_in.shape[0]
    t = batch * seq
    n_a = 4 * DA_HEADS * DA_DK + DA_HEADS * DA_DV
    n_b = 4 * HG_HEADS * HG_D
    n_c = 4 * ML_HEADS * ML_D + 2 * ML_HEADS
    n_c_pad = -(-n_c // LANES) * LANES
    tm = _pick(seq, (1024, 512, 256))
    tq = _pick(seq, (ATT_TQ,))
    ts_h = _pick(seq, (512, 256, 128))

    c8 = jnp.zeros((8, d), F32).at[:batch].set(c)
    mod = _mod_call(c8, w_mod, b_mod.reshape(depth, 1, 6 * d))[:, :batch].reshape(depth, batch, 6, d)

    lb_all = jnp.cumsum(jax.nn.softmax(b_lb_logits.astype(F32), axis=0), axis=0)
    lb_all = lb_all - lb_all[:1]
    slopes = jnp.asarray(_attn_slopes(), F32)

    xf = x.reshape(t, d)
    h = _normmod_call(xf, norm1_w[0].reshape(1, d), mod[0], seq, tm)
    perm_a = _attn_col_perm()
    for l in range(depth):
        lam_init = 0.8 - 0.6 * math.exp(-0.3 * l)
        wl = w_in[l]
        w_a = wl[:, :n_a][:, perm_a].astype(BF16)
        w_b = wl[:, n_a:n_a + n_b].astype(BF16)
        w_c = jnp.pad(wl[:, n_a + n_b:n_a + n_b + n_c], ((0, 0), (0, n_c_pad - n_c))).astype(BF16)
        w_g = wl[:, n_a + n_b + n_c:].astype(BF16)
        bias_c = jnp.zeros((1, n_c_pad), F32)
        bias_c = bias_c.at[0, n_c - 2 * ML_HEADS:n_c - ML_HEADS].set(c_igate_b[l])
        bias_c = bias_c.at[0, n_c - ML_HEADS:n_c].set(c_fgate_b[l])
        pb = _matmul(h, w_b, jnp.zeros((1, n_b), F32), F32, tm, n_b // 2)
        pc = _matmul(h, w_c, bias_c, F32, tm, n_c_pad)
        pg = _matmul(h, w_g, jnp.zeros((1, 3 * d), F32), BF16, tm, 3 * d // 2)

        qw = (jnp.tile(a_qnorm_w[l], 2 * DA_HEADS) * (DA_DK ** -0.5 * LOG2E)).reshape(1, -1)
        kw = jnp.tile(a_knorm_w[l], 2 * DA_HEADS).reshape(1, -1)
        kn, qt, vt = _attn_prep_call(h, w_a, qw, kw, batch, seq, _pick(seq, (512, 256)))
        lam = (jnp.exp(jnp.sum(a_lambda_q1[l] * a_lambda_k1[l]))
               - jnp.exp(jnp.sum(a_lambda_q2[l] * a_lambda_k2[l])) + lam_init)
        scal = jnp.concatenate([slopes, jnp.stack([lam, jnp.asarray(1.0 - lam_init, F32)])]).astype(F32)
        y_a = _attn_call(scal, qt, kn, vt, a_subln_w[l].reshape(1, -1), batch, seq, tq)

        lb = lb_all[l].reshape(HG_HEADS, 1, HG_D)
        lbs = jnp.concatenate([jnp.log(lb), jnp.log1p(-lb)], axis=1)
        y_b = _hgrn_call(pb, lbs, b_gnorm_w[l].reshape(1, -1), batch, seq, ts_h)

        gates = pc[:, n_c - 2 * ML_HEADS:n_c].reshape(batch, seq, 2, ML_HEADS)
        gates = gates.transpose(2, 0, 3, 1).reshape(2, batch, ML_HEADS, seq // ML_CHUNK, ML_CHUNK)
        y_c = _mlstm_call(pc, gates[0], gates[1], c_conv_w[l], c_conv_b[l].reshape(1, -1),
                          c_norm_w[l].reshape(1, -1), batch, seq)

        dense = l % 2 == 0
        router = None
        if not dense:
            rw = jnp.pad(moe_router_w[l // 2], ((0, 0), (0, LANES - N_EXPERTS)))
            rb = jnp.pad(moe_router_b[l // 2], (0, LANES - N_EXPERTS)).reshape(1, LANES)
            router = (rw, rb)
        outs = _merge_call(y_a, y_b, y_c, pg, w_branch[l].astype(BF16), w_out[l].astype(BF16), xf,
                           mod[l], norm2_w[l].reshape(1, d), router, seq, 512)
        xf, h2 = outs[0], outs[1]
        nxt = None if l == depth - 1 else (norm1_w[l + 1].reshape(1, d), mod[l + 1])
        if dense:
            res = _ffn_call(h2, ffn_w1[l // 2].astype(BF16), ffn_w3[l // 2].astype(BF16),
                            ffn_w2[l // 2].astype(BF16), xf, mod[l], nxt, seq, tm, FFN_TF)
        else:
            y = _moe_call(h2, outs[2], _moe_pack_w13(moe_w1[l // 2], moe_w3[l // 2]),
                          moe_w2[l // 2].astype(BF16), tm)
            res = _resid_call(y, xf, mod[l], nxt, seq, tm)
        xf = res[0]
        if nxt is not None:
            h = res[1]
    return xf.reshape(batch, seq, d)
```

```python
import functools
import math

import numpy as np
import jax
import jax.numpy as jnp
from jax import lax
from jax.experimental import pallas as pl
from jax.experimental.pallas import tpu as pltpu

F32 = jnp.float32
BF16 = jnp.bfloat16
NORM_EPS = 1e-6

CHUNK = 64
DA_HEADS, DA_DK, DA_DV = 4, 64, 128
HG_HEADS, HG_D = 4, 128
ML_HEADS, ML_D, ML_CONV = 4, 128, 4
N_EXPERTS = 8

LANES = 128
VMEM_LIMIT = 56 * 1024 * 1024

ATT_TQ = 256
ATT_HEADS_PER_STEP = 2
ATT_PAD_ROWS = 16
LOG2E = math.log2(math.e)
ATT_BOUND_MARGIN = 1.01
ATT_BOUND_EPS = 1e-3
ATT_BOUND_MAX = 40.0
ATT_ZERO_EXP = 160.0
HG_CHUNK = 128
HG_SUB = 32
HG_SAFE_DECAY = 80.0
ML_CHUNK = 256
MOE_CAPS = (256, 320, 384)
MOE_TF = 1408
FFN_TF = 256


def _cparams(sem, vmem=VMEM_LIMIT):
    return pltpu.CompilerParams(dimension_semantics=sem, vmem_limit_bytes=vmem)


def _dot(a, b):
    return jnp.dot(a, b, preferred_element_type=F32)


def _dot_nt(a, b):
    return lax.dot_general(a, b, (((1,), (1,)), ((), ())), preferred_element_type=F32)


def _split3(x):
    hi = x.astype(BF16)
    r1 = x - hi.astype(F32)
    mid = r1.astype(BF16)
    lo = (r1 - mid.astype(F32)).astype(BF16)
    return hi, mid, lo


def _norm_mod(x, w, scale, shift):
    y = x * lax.rsqrt(jnp.mean(x * x, axis=-1, keepdims=True) + NORM_EPS) * w
    return y * (1.0 + scale) + shift


def _sigmoid(x):
    return 0.5 * jnp.tanh(0.5 * x) + 0.5


def _log1p_exp_neg(z):
    return jnp.log(1.0 + jnp.exp(-z))


def _log_sigmoid(x):
    return jnp.minimum(x, 0.0) - _log1p_exp_neg(jnp.abs(x))


def _mod_kernel(c_ref, w_ref, b_ref, o_ref):
    cnd = c_ref[...]
    cnd = cnd * _sigmoid(cnd)
    o_ref[0] = _dot(cnd.astype(BF16), w_ref[0].astype(BF16)) + b_ref[0]


def _mod_call(c8, w_mod, b_mod):
    depth, d, n = w_mod.shape
    tn = 1536 if n % 1536 == 0 else n
    return pl.pallas_call(
        _mod_kernel,
        grid=(depth, n // tn),
        in_specs=[pl.BlockSpec((8, d), lambda l, j: (0, 0)),
                  pl.BlockSpec((1, d, tn), lambda l, j: (l, 0, j)),
                  pl.BlockSpec((1, 1, tn), lambda l, j: (l, 0, j))],
        out_specs=pl.BlockSpec((1, 8, tn), lambda l, j: (l, 0, j)),
        out_shape=jax.ShapeDtypeStruct((depth, 8, n), F32),
        compiler_params=_cparams(("arbitrary", "arbitrary")),
        name="adaln_mod",
    )(c8, w_mod, b_mod)


def _normmod_kernel(x_ref, nw_ref, mod_ref, h_ref):
    m = mod_ref[0]
    h_ref[...] = _norm_mod(x_ref[...], nw_ref[...], m[1:2], m[0:1]).astype(BF16)


def _normmod_call(x, nw, mod, seq, tm):
    t, d = x.shape
    per_b = seq // tm
    return pl.pallas_call(
        _normmod_kernel,
        grid=(t // tm,),
        in_specs=[pl.BlockSpec((tm, d), lambda i: (i, 0)),
                  pl.BlockSpec((1, d), lambda i: (0, 0)),
                  pl.BlockSpec((1, 6, d), lambda i: (i // per_b, 0, 0))],
        out_specs=pl.BlockSpec((tm, d), lambda i: (i, 0)),
        out_shape=jax.ShapeDtypeStruct((t, d), BF16),
        compiler_params=_cparams(("arbitrary",)),
        name="prenorm_mod",
    )(x, nw, mod)


def _mm_kernel(x_ref, w_ref, b_ref, o_ref):
    o_ref[...] = (_dot(x_ref[...], w_ref[...]) + b_ref[...]).astype(o_ref.dtype)


def _matmul(x, w, bias, out_dtype, tm, tn):
    t, k = x.shape
    n = w.shape[1]
    return pl.pallas_call(
        _mm_kernel,
        grid=(n // tn, t // tm),
        in_specs=[pl.BlockSpec((tm, k), lambda j, i: (i, 0)),
                  pl.BlockSpec((k, tn), lambda j, i: (0, j)),
                  pl.BlockSpec((1, tn), lambda j, i: (0, j))],
        out_specs=pl.BlockSpec((tm, tn), lambda j, i: (i, j)),
        out_shape=jax.ShapeDtypeStruct((t, n), out_dtype),
        compiler_params=_cparams(("arbitrary", "arbitrary")),
        name="in_proj",
    )(x, w, bias)


def _attn_prep_kernel(x_ref, w_ref, qw_ref, kw_ref, kn_ref, qt_ref, vt_ref):
    hw = DA_HEADS * 2 * DA_DK
    a = _dot(x_ref[...], w_ref[...])
    r = lax.broadcasted_iota(jnp.int32, (hw, hw), 0) // DA_DK
    c = lax.broadcasted_iota(jnp.int32, (hw, hw), 1) // DA_DK
    group = jnp.where(r == c, 1.0, 0.0).astype(BF16)

    def qk_norm(z, w):
        ms = _dot((z * z).astype(BF16), group) * (1.0 / DA_DK)
        return z * lax.rsqrt(ms + NORM_EPS) * w

    qn = qk_norm(a[:, :hw], qw_ref[...])
    kn = qk_norm(a[:, hw:2 * hw], kw_ref[...])
    kn_ref[...] = kn.astype(BF16)
    qt_ref[0] = qn.T.astype(BF16)
    vt_ref[0] = a[:, 2 * hw:].T.astype(BF16)


def _attn_prep_call(x, w, qw, kw, batch, seq, ts):
    t, d = x.shape
    hw = DA_HEADS * 2 * DA_DK
    hv = DA_HEADS * DA_DV
    per_b = seq // ts
    return pl.pallas_call(
        _attn_prep_kernel,
        grid=(t // ts,),
        in_specs=[pl.BlockSpec((ts, d), lambda i: (i, 0)),
                  pl.BlockSpec((d, 2 * hw + hv), lambda i: (0, 0)),
                  pl.BlockSpec((1, hw), lambda i: (0, 0)),
                  pl.BlockSpec((1, hw), lambda i: (0, 0))],
        out_specs=[pl.BlockSpec((ts, hw), lambda i: (i, 0)),
                   pl.BlockSpec((1, hw, ts), lambda i: (i // per_b, 0, i % per_b)),
                   pl.BlockSpec((1, hv, ts), lambda i: (i // per_b, 0, i % per_b))],
        out_shape=[jax.ShapeDtypeStruct((t, hw), BF16),
                   jax.ShapeDtypeStruct((batch, hw, seq), BF16),
                   jax.ShapeDtypeStruct((batch, hv, seq), BF16)],
        compiler_params=_cparams(("arbitrary",)),
        name="attn_prep",
    )(x, w, qw, kw)


def _attn_slopes():
    return [LOG2E * 2.0 ** (-8.0 * (h + 1) / DA_HEADS) for h in range(DA_HEADS)]


def _attn_window(head, tq):
    return int((ATT_ZERO_EXP / _attn_slopes()[head] - 1.0) // tq) + 1


def _attn_kernel(sc_ref, qt_ref, k_ref, vt_ref, sw_ref, o_ref,
                 qz_ref, m_ref, acc_ref, vs_ref, kmax_ref, s_ref, s2_ref, *, tq, seq):
    hp = ATT_HEADS_PER_STEP
    dv = DA_DV
    g = pl.program_id(1)
    i = pl.program_id(2)
    lam = sc_ref[DA_HEADS]
    out_scale = sc_ref[DA_HEADS + 1]
    slopes = [sc_ref[hp * g + hh] for hh in range(hp)]
    row16 = lax.broadcasted_iota(jnp.int32, (ATT_PAD_ROWS, tq), 0)
    ones_rows = jnp.where(row16 == 0, 1.0, 0.0).astype(BF16)

    half = lax.broadcasted_iota(jnp.int32, (1, 2 * DA_DK), 1) < DA_DK

    @pl.when(i == 0)
    def _():
        pos = lax.broadcasted_iota(jnp.int32, (1, seq), 1) & (tq - 1)
        rel = (pos - (tq - 1)).astype(F32)
        sub16 = lax.broadcasted_iota(jnp.int32, (ATT_PAD_ROWS, seq), 0)
        gr = lax.broadcasted_iota(jnp.int32, (2 * DA_DK, 2 * DA_DK), 0) // DA_DK
        gc = lax.broadcasted_iota(jnp.int32, (2 * DA_DK, 2 * DA_DK), 1) // DA_DK
        group = jnp.where(gr == gc, 1.0, 0.0).astype(BF16)
        for hh in range(hp):
            w = jnp.exp2(slopes[hh] * rel)
            vs_ref[hh, 0:dv, :] = (vt_ref[0, hh * dv:(hh + 1) * dv, :].astype(F32) * w).astype(BF16)
            vs_ref[hh, dv:dv + ATT_PAD_ROWS, :] = jnp.where(sub16 == 0, w, 0.0).astype(BF16)

            def knorm(n, best):
                kc = k_ref[pl.ds(pl.multiple_of(n * tq, tq), tq),
                           hh * 2 * DA_DK:(hh + 1) * 2 * DA_DK].astype(F32)
                return jnp.maximum(best, jnp.max(_dot((kc * kc).astype(BF16), group), axis=0, keepdims=True))

            k2 = lax.fori_loop(0, seq // tq, knorm, jnp.zeros((1, 2 * DA_DK), F32))
            kmax_ref[2 * hh] = jnp.max(jnp.where(half, k2, 0.0), axis=1, keepdims=True)
            kmax_ref[2 * hh + 1] = jnp.max(jnp.where(half, 0.0, k2), axis=1, keepdims=True)

    row = lax.broadcasted_iota(jnp.int32, (2 * DA_DK, tq), 0)
    qq_row = lax.broadcasted_iota(jnp.int32, (1, tq), 1).astype(F32)
    bound_max = jnp.zeros((), F32)
    firsts = []

    def scores(hh, blk):
        start = pl.multiple_of(blk * tq, tq)
        return _dot(k_ref[pl.ds(start, tq), hh * 2 * DA_DK:(hh + 1) * 2 * DA_DK], qz_ref[hh])

    for hh in range(hp):
        qt = qt_ref[0, hh * 2 * DA_DK:(hh + 1) * 2 * DA_DK, :]
        zero = jnp.zeros_like(qt)
        q1 = jnp.where(row < DA_DK, qt, zero)
        q2 = jnp.where(row >= DA_DK, qt, zero)
        qz_ref[hh] = jnp.concatenate([q1, q2], axis=1)
        window = jnp.int32(_attn_window(hh, tq))
        for grp in range(1, DA_HEADS // hp):
            window = jnp.where(g == grp, _attn_window(grp * hp + hh, tq), window)
        firsts.append((jnp.maximum(i - window, 0) // 2) * 2)
        s_ref[hh] = scores(hh, firsts[hh])
        s2_ref[hh] = scores(hh, jnp.minimum(firsts[hh] + 1, i))
        qn = jnp.concatenate(
            [jnp.sum(jnp.square(q1.astype(F32)), axis=0, keepdims=True) * kmax_ref[2 * hh],
             jnp.sum(jnp.square(q2.astype(F32)), axis=0, keepdims=True) * kmax_ref[2 * hh + 1]], axis=1)
        bound = jnp.sqrt(qn) * ATT_BOUND_MARGIN + ATT_BOUND_EPS
        bound_max = jnp.maximum(bound_max, jnp.max(bound))
        m_ref[hh] = bound + slopes[hh] * jnp.concatenate([qq_row, qq_row], axis=1)
    acc_ref[...] = jnp.zeros(acc_ref.shape, F32)

    kk = lax.broadcasted_iota(jnp.int32, (tq, 2 * tq), 0)
    cc = lax.broadcasted_iota(jnp.int32, (tq, 2 * tq), 1)
    qq = jnp.where(cc >= tq, cc - tq, cc)
    dist = (qq - jnp.abs(qq - kk)).astype(F32)
    visible = (kk // CHUNK) <= (qq // CHUNK)

    def diag_values(hh):
        start = pl.multiple_of(i * tq, tq)
        return jnp.concatenate([vt_ref[0, hh * dv:(hh + 1) * dv, pl.ds(start, tq)], ones_rows], axis=0)

    @pl.when(bound_max <= ATT_BOUND_MAX)
    def _():
        def probs(hh, blk, s):
            last_key = ((blk + 1 - i) * tq - 1).astype(F32)
            return jnp.exp2(s - (m_ref[hh] - slopes[hh] * last_key)).astype(BF16)

        def body(heads, jj, carry):
            blk = 2 * jj
            start = pl.multiple_of(blk * tq, 2 * tq)
            for hh in heads:
                p = jnp.concatenate([probs(hh, blk, s_ref[hh]), probs(hh, blk + 1, s2_ref[hh])], axis=0)
                s_ref[hh] = scores(hh, jnp.minimum(blk + 2, i))
                s2_ref[hh] = scores(hh, jnp.minimum(blk + 3, i))
                acc_ref[hh] += _dot(vs_ref[hh, :, pl.ds(start, 2 * tq)], p)
            return carry

        for hh in range(hp - 1, 0, -1):
            lax.fori_loop(firsts[hh] // 2, firsts[hh - 1] // 2,
                          functools.partial(body, tuple(range(hh, hp))), 0)
        lax.fori_loop(firsts[0] // 2, i // 2, functools.partial(body, tuple(range(hp))), 0)

        @pl.when(i % 2 == 1)
        def _():
            start = pl.multiple_of((i - 1) * tq, tq)
            for hh in range(hp):
                acc_ref[hh] += _dot(vs_ref[hh, :, pl.ds(start, tq)], probs(hh, i - 1, s_ref[hh]))
                s_ref[hh] = s2_ref[hh]

        for hh in range(hp):
            s = jnp.where(visible, s_ref[hh] + slopes[hh] * dist, -jnp.inf)
            acc_ref[hh] += _dot(diag_values(hh), jnp.exp2(s - m_ref[hh]).astype(BF16))

    @pl.when(bound_max > ATT_BOUND_MAX)
    def _():
        m_ref[...] = jnp.full(m_ref.shape, -jnp.inf, F32)

        def update(hh, s, bound_shift, v_aug):
            m_old = m_ref[hh]
            m_new = jnp.maximum(m_old, jnp.max(s, axis=0, keepdims=True) + bound_shift)
            p = jnp.exp2(s - (m_new - bound_shift))
            acc_ref[hh] = jnp.exp2(m_old - m_new) * acc_ref[hh] + _dot(v_aug, p.astype(BF16))
            m_ref[hh] = m_new

        def body(j, carry):
            start = pl.multiple_of(j * tq, tq)
            last_key = ((j + 1 - i) * tq - 1).astype(F32)
            for hh in range(hp):
                update(hh, scores(hh, j), slopes[hh] * last_key, vs_ref[hh, :, pl.ds(start, tq)])
            return carry

        lax.fori_loop(0, i, body, 0)
        for hh in range(hp):
            s = jnp.where(visible, scores(hh, i) + slopes[hh] * dist, -jnp.inf)
            update(hh, s, jnp.zeros((), F32), diag_values(hh))

    outs = []
    for hh in range(hp):
        acc = acc_ref[hh]
        o2 = acc[:dv, :] * (1.0 / acc[dv:dv + 1, :])
        o = (o2[:, :tq] - lam * o2[:, tq:]).T
        o = o * lax.rsqrt(jnp.mean(o * o, axis=-1, keepdims=True) + NORM_EPS) * sw_ref[...]
        outs.append(o * out_scale)
    o_ref[...] = jnp.concatenate(outs, axis=1).astype(o_ref.dtype)


def _attn_call(scal, qt, kn, vt, sw, batch, seq, tq):
    t = kn.shape[0]
    nq = seq // tq
    hp = ATT_HEADS_PER_STEP
    return pl.pallas_call(
        functools.partial(_attn_kernel, tq=tq, seq=seq),
        grid=(batch, DA_HEADS // hp, nq),
        in_specs=[pl.BlockSpec(memory_space=pltpu.SMEM),
                  pl.BlockSpec((1, hp * 2 * DA_DK, tq), lambda b, g, i: (b, g, i)),
                  pl.BlockSpec((seq, hp * 2 * DA_DK), lambda b, g, i: (b, g)),
                  pl.BlockSpec((1, hp * DA_DV, seq), lambda b, g, i: (b, g, 0)),
                  pl.BlockSpec((1, DA_DV), lambda b, g, i: (0, 0))],
        out_specs=pl.BlockSpec((tq, hp * DA_DV), lambda b, g, i: (b * nq + i, g)),
        out_shape=jax.ShapeDtypeStruct((t, DA_HEADS * DA_DV), BF16),
        scratch_shapes=[pltpu.VMEM((hp, 2 * DA_DK, 2 * tq), BF16),
                        pltpu.VMEM((hp, 1, 2 * tq), F32),
                        pltpu.VMEM((hp, DA_DV + ATT_PAD_ROWS, 2 * tq), F32),
                        pltpu.VMEM((hp, DA_DV + ATT_PAD_ROWS, seq), BF16),
                        pltpu.VMEM((2 * hp, 1, 1), F32),
                        pltpu.VMEM((hp, tq, 2 * tq), F32),
                        pltpu.VMEM((hp, tq, 2 * tq), F32)],
        compiler_params=_cparams(("arbitrary", "arbitrary", "arbitrary")),
        name="diff_attn",
    )(scal, qt, kn, vt, sw)


def _hgrn_kernel(q_ref, f_ref, i_ref, g_ref, lb_ref, gw_ref, o_ref,
                 st_ref, b_ref, qs_ref, ks_ref, oi_ref, *, n_chunks):
    c = HG_CHUNK
    sub = HG_SUB
    d = HG_D
    nh = HG_HEADS
    chains = [(bb, hh) for bb in range(q_ref.shape[0]) for hh in range(nh)]

    @pl.when(pl.program_id(0) == 0)
    def _():
        st_ref[...] = jnp.zeros(st_ref.shape, F32)

    gw = gw_ref[...]
    rr = lax.broadcasted_iota(jnp.int32, (c, c), 0)
    cc = lax.broadcasted_iota(jnp.int32, (c, c), 1)
    causal = cc <= rr
    tril = jnp.where(causal, 1.0, 0.0).astype(BF16)

    def chunk(n, carry):
        r0 = pl.multiple_of(n * c, c)
        gates = []
        for ch, (bb, hh) in enumerate(chains):
            cols = slice(hh * d, (hh + 1) * d)
            log_lb = lb_ref[hh, 0:1, :]
            q = q_ref[bb, pl.ds(r0, c), cols]
            q = q * _sigmoid(q)
            a = lb_ref[hh, 1:2, :] + _log_sigmoid(f_ref[bb, pl.ds(r0, c), cols])
            logf = jnp.maximum(log_lb, a) + _log1p_exp_neg(jnp.abs(log_lb - a))
            k = 1.0 - jnp.exp(logf)
            v = i_ref[bb, pl.ds(r0, c), cols]
            hi, mid, lo = _split3(logf)
            b = _dot(tril, hi) + _dot(tril, mid) + _dot(tril, lo)
            gates.append((q, k, v, b))

        heads = []
        decay = jnp.zeros((), F32)
        for ch, (q, k, v, b) in enumerate(gates):
            b_last = b[c - 1:c, :]
            st = st_ref[ch]
            o_inter = _dot_nt((q * jnp.exp(b)).astype(BF16), st.astype(BF16))
            k_hat = k * jnp.exp(b_last - b)
            st_ref[ch] = st * jnp.exp(b_last) + _dot(v.T.astype(BF16), k_hat.astype(BF16))
            betas = []
            for blk in range(c // sub):
                beta = jnp.zeros((1, d), F32) if blk == 0 else b[blk * sub - 1:blk * sub, :]
                betas.append(beta)
                b_end = b[(blk + 1) * sub - 1:(blk + 1) * sub, :]
                decay = jnp.maximum(decay, jnp.max(beta - b_end))
            heads.append((q, k, v, b, betas, o_inter))

        @pl.when(decay < HG_SAFE_DECAY)
        def _():
            atts = []
            for q, k, v, b, betas, _ in heads:
                rows = []
                for blk in range(c // sub):
                    beta = betas[blk]
                    q_t = q[blk * sub:(blk + 1) * sub, :] * jnp.exp(b[blk * sub:(blk + 1) * sub, :] - beta)
                    k_t = k * jnp.exp(jnp.minimum(beta - b, HG_SAFE_DECAY))
                    rows.append(_dot_nt(q_t.astype(BF16), k_t.astype(BF16)))
                atts.append(jnp.where(causal, jnp.concatenate(rows, axis=0), 0.0).astype(BF16))
            for ch, att in enumerate(atts):
                oi_ref[ch] = _dot(att, heads[ch][2].astype(BF16))

        @pl.when(decay >= HG_SAFE_DECAY)
        def _():
            ridx = lax.broadcasted_iota(jnp.int32, (c, d), 0)
            for ch, (q, k, v, b, _, _) in enumerate(heads):
                b_ref[...] = b
                qs_ref[...] = q
                ks_ref[...] = k

                def row(t, carry2):
                    bt = b_ref[pl.ds(t, 1), :]
                    e = jnp.exp(jnp.where(ridx <= t, bt - b_ref[...], -jnp.inf))
                    w = jnp.sum(qs_ref[pl.ds(t, 1), :] * ks_ref[...] * e, axis=1, keepdims=True)
                    oi_ref[ch, pl.ds(t, 1), :] = jnp.sum(w * v, axis=0, keepdims=True)
                    return carry2

                lax.fori_loop(0, c, row, 0)

        for ch, (bb, hh) in enumerate(chains):
            cols = slice(hh * d, (hh + 1) * d)
            o = heads[ch][5] + oi_ref[ch]
            o = o * lax.rsqrt(jnp.mean(o * o, axis=-1, keepdims=True) + NORM_EPS) * gw
            g = g_ref[bb, pl.ds(r0, c), cols]
            o_ref[bb, pl.ds(r0, c), cols] = (o * (g * _sigmoid(g))).astype(o_ref.dtype)
        return carry

    lax.fori_loop(0, n_chunks, chunk, 0)


def _hgrn_call(bproj, lbs, gw, batch, seq, ts):
    t = bproj.shape[0]
    d = HG_D
    nh = HG_HEADS
    bp3 = bproj.reshape(batch, seq, bproj.shape[1])
    spec = lambda off: pl.BlockSpec((batch, ts, nh * d), lambda i: (0, i, off))
    out = pl.pallas_call(
        functools.partial(_hgrn_kernel, n_chunks=ts // HG_CHUNK),
        grid=(seq // ts,),
        in_specs=[spec(0), spec(1), spec(2), spec(3),
                  pl.BlockSpec((nh, 2, d), lambda i: (0, 0, 0)),
                  pl.BlockSpec((1, d), lambda i: (0, 0))],
        out_specs=pl.BlockSpec((batch, ts, nh * d), lambda i: (0, i, 0)),
        out_shape=jax.ShapeDtypeStruct((batch, seq, nh * d), BF16),
        scratch_shapes=[pltpu.VMEM((batch * nh, d, d), F32),
                        pltpu.VMEM((HG_CHUNK, d), F32),
                        pltpu.VMEM((HG_CHUNK, d), F32),
                        pltpu.VMEM((HG_CHUNK, d), F32),
                        pltpu.VMEM((batch * nh, HG_CHUNK, d), F32)],
        compiler_params=_cparams(("arbitrary",)),
        name="hgrn2",
    )(bp3, bp3, bp3, bp3, lbs, gw)
    return out.reshape(t, nh * d)


def _mlstm_kernel(uq_ref, uk_ref, v_ref, op_ref, gi_ref, gf_ref, cwq_ref, cwk_ref,
                  cbq_ref, cbk_ref, nw_ref, o_ref,
                  xq_ref, xk_ref, c_ref, m_ref, bs_ref):
    L = ML_CHUNK
    d = ML_D
    nh = ML_HEADS
    nb = uq_ref.shape[0]
    i = pl.program_id(0)
    rr = lax.broadcasted_iota(jnp.int32, (L, L), 0)
    cc = lax.broadcasted_iota(jnp.int32, (L, L), 1)

    @pl.when(i == 0)
    def _():
        c_ref[...] = jnp.zeros(c_ref.shape, F32)
        m_ref[...] = jnp.zeros(m_ref.shape, F32)
        xq_ref[:, 0:8, :] = jnp.zeros((nb, 8, nh * d), F32)
        xk_ref[:, 0:8, :] = jnp.zeros((nb, 8, nh * d), F32)
        upper = jnp.where(rr <= cc, 1.0, 0.0).astype(BF16)
        for ch in range(nb * nh):
            hi, mid, lo = _split3(_log_sigmoid(gf_ref[ch // nh, ch % nh]))
            bs_ref[ch] = _dot(hi, upper) + _dot(mid, upper) + _dot(lo, upper)

    def conv_silu(bb, u_ref, x_ref, w_ref, b_ref):
        x_ref[bb, 8:8 + L, :] = u_ref[bb]
        y = b_ref[...] + w_ref[ML_CONV - 1:ML_CONV, :] * x_ref[bb, 8:8 + L, :]
        for j in range(ML_CONV - 1):
            y = y + w_ref[j:j + 1, :] * x_ref[bb, 5 + j:5 + j + L, :]
        x_ref[bb, 0:8, :] = x_ref[bb, L:L + 8, :]
        return y * _sigmoid(y)

    q_all = [conv_silu(bb, uq_ref, xq_ref, cwq_ref, cbq_ref) for bb in range(nb)]
    k_all = [conv_silu(bb, uk_ref, xk_ref, cwk_ref, cbk_ref) * (d ** -0.5) for bb in range(nb)]
    lane = lax.broadcasted_iota(jnp.int32, (L, d), 1)
    ones_col = jnp.where(lane == 0, 1.0, 0.0)
    stage1 = []
    for ch in range(nb * nh):
        bb, hh = ch // nh, ch % nh
        cols = slice(hh * d, (hh + 1) * d)
        q = q_all[bb][:, cols].astype(BF16)
        kt = k_all[bb][:, cols].T
        v_aug = jnp.concatenate([v_ref[bb, :, cols], ones_col], axis=1).astype(BF16)
        c_aug = c_ref[ch]
        qk = _dot(q, kt.astype(BF16))
        qc = _dot(q, c_aug.astype(BF16))

        b_row = bs_ref[ch, pl.ds(i, 1), :]
        ig_row = gi_ref[bb, hh, pl.ds(i, 1), :]
        m_prev = m_ref[ch]
        g = b_row[:, L - 1:L]
        log_w = g - b_row + ig_row
        m_new = jnp.maximum(g + m_prev, jnp.max(log_w, axis=1, keepdims=True))
        w_s = jnp.exp(log_w - m_new)
        c_ref[ch] = jnp.exp(g + m_prev - m_new) * c_aug + _dot((kt * w_s).astype(BF16), v_aug)
        m_ref[ch] = m_new
        stage1.append((qk, qc, v_aug, b_row, ig_row, m_prev))

    stage2 = []
    for qk, qc, v_aug, b_row, ig_row, m_prev in stage1:
        b_col = jnp.sum(jnp.where(rr == cc, b_row, 0.0), axis=1, keepdims=True)
        log_d = jnp.where(cc <= rr, b_col + (ig_row - b_row), -jnp.inf)
        log_inter = b_col + m_prev
        m_t = jnp.maximum(log_inter, jnp.max(log_d, axis=1, keepdims=True))
        w_intra = jnp.exp(log_d - m_t) * qk
        tot = jnp.exp(log_inter - m_t) * qc + _dot(w_intra.astype(BF16), v_aug)
        stage2.append((tot, m_t))

    for ch, (tot, m_t) in enumerate(stage2):
        bb, hh = ch // nh, ch % nh
        cols = slice(hh * d, (hh + 1) * d)
        denom = jnp.maximum(jnp.abs(tot[:, d:d + 1]), jnp.exp(-m_t))
        hout = tot[:, :d] / denom
        hout = hout * lax.rsqrt(jnp.mean(hout * hout, axis=-1, keepdims=True) + NORM_EPS) * nw_ref[...]
        o_ref[bb, :, cols] = (hout * _sigmoid(op_ref[bb, :, cols])).astype(o_ref.dtype)


def _mlstm_call(cproj, gi, gf, conv_w, conv_b, nw, batch, seq):
    t = cproj.shape[0]
    d = ML_D
    nh = ML_HEADS
    L = ML_CHUNK
    nc = seq // L
    cp3 = cproj.reshape(batch, seq, cproj.shape[1])
    spec = lambda off: pl.BlockSpec((batch, L, nh * d), lambda i: (0, i, off))
    gspec = pl.BlockSpec((batch, nh, nc, L), lambda i: (0, 0, 0, 0))
    out = pl.pallas_call(
        _mlstm_kernel,
        grid=(nc,),
        in_specs=[spec(0), spec(1), spec(2), spec(3), gspec, gspec,
                  pl.BlockSpec((ML_CONV, nh * d), lambda i: (0, 0)),
                  pl.BlockSpec((ML_CONV, nh * d), lambda i: (0, 1)),
                  pl.BlockSpec((1, nh * d), lambda i: (0, 0)),
                  pl.BlockSpec((1, nh * d), lambda i: (0, 1)),
                  pl.BlockSpec((1, d), lambda i: (0, 0))],
        out_specs=pl.BlockSpec((batch, L, nh * d), lambda i: (0, i, 0)),
        out_shape=jax.ShapeDtypeStruct((batch, seq, nh * d), BF16),
        scratch_shapes=[pltpu.VMEM((batch, L + 8, nh * d), F32),
                        pltpu.VMEM((batch, L + 8, nh * d), F32),
                        pltpu.VMEM((batch * nh, d, 2 * d), F32),
                        pltpu.VMEM((batch * nh, 1, 1), F32),
                        pltpu.VMEM((batch * nh, nc, L), F32)],
        compiler_params=_cparams(("arbitrary",)),
        name="mlstm",
    )(cp3, cp3, cp3, cp3, gi, gf, conv_w, conv_w, conv_b, conv_b, nw)
    return out.reshape(t, nh * d)


def _top2_combine(logits):
    lane = lax.broadcasted_iota(jnp.int32, logits.shape, 1)
    lg = jnp.where(lane < N_EXPERTS, logits, -jnp.inf)
    ex = jnp.exp(lg - jnp.max(lg, axis=1, keepdims=True))
    probs = ex / jnp.sum(ex, axis=1, keepdims=True)
    p1 = jnp.max(probs, axis=1, keepdims=True)
    i1 = jnp.min(jnp.where(probs == p1, lane, LANES), axis=1, keepdims=True)
    rest = jnp.where(lane == i1, -1.0, probs)
    p2 = jnp.max(rest, axis=1, keepdims=True)
    i2 = jnp.min(jnp.where(rest == p2, lane, LANES), axis=1, keepdims=True)
    comb = jnp.where(lane == i1, p1, 0.0) + jnp.where(lane == i2, p2, 0.0)
    return comb / (p1 + p2)


def _merge_kernel(*refs, route):
    if route:
        (ya_ref, yb_ref, yc_ref, gp_ref, wb_ref, wo_ref, x_ref, mod_ref, nw_ref,
         rw_ref, rb_ref, xo_ref, h_ref, cmb_ref) = refs
    else:
        (ya_ref, yb_ref, yc_ref, gp_ref, wb_ref, wo_ref, x_ref, mod_ref, nw_ref,
         xo_ref, h_ref) = refs
    d = x_ref.shape[1]
    merged = None
    for n, y_ref in enumerate((ya_ref, yb_ref, yc_ref)):
        gate = _sigmoid(gp_ref[:, n * d:(n + 1) * d].astype(F32))
        term = gate * _dot(y_ref[...], wb_ref[n])
        merged = term if merged is None else merged + term
    m = mod_ref[0]
    xn = x_ref[...] + m[2:3] * _dot(merged.astype(BF16), wo_ref[...])
    xo_ref[...] = xn
    h2 = _norm_mod(xn, nw_ref[...], m[4:5], m[3:4])
    h_ref[...] = h2.astype(BF16)
    if route:
        h_hi, h_mid, _ = _split3(h2)
        r_hi, r_mid, _ = _split3(rw_ref[...])
        logits = _dot(h_hi, r_hi) + _dot(h_mid, r_hi) + _dot(h_hi, r_mid) + rb_ref[...]
        cmb_ref[...] = _top2_combine(logits)


def _merge_call(ya, yb, yc, gp, wb, wo, x, mod, nw, router, seq, tm):
    t, d = x.shape
    bw = ya.shape[1]
    per_b = seq // tm
    route = router is not None
    tok = lambda w: pl.BlockSpec((tm, w), lambda i: (i, 0))
    const2 = lambda s: pl.BlockSpec(s, lambda i: (0, 0))
    in_specs = [tok(bw), tok(bw), tok(bw), tok(3 * d),
                pl.BlockSpec((3, bw, d), lambda i: (0, 0, 0)), const2((d, d)), tok(d),
                pl.BlockSpec((1, 6, d), lambda i: (i // per_b, 0, 0)), const2((1, d))]
    out_specs = [tok(d), tok(d)]
    out_shape = [jax.ShapeDtypeStruct((t, d), F32), jax.ShapeDtypeStruct((t, d), BF16)]
    args = [ya, yb, yc, gp, wb, wo, x, mod, nw]
    if route:
        in_specs += [const2((d, LANES)), const2((1, LANES))]
        out_specs.append(tok(LANES))
        out_shape.append(jax.ShapeDtypeStruct((t, LANES), F32))
        args += list(router)
    return pl.pallas_call(
        functools.partial(_merge_kernel, route=route),
        grid=(t // tm,),
        in_specs=in_specs, out_specs=out_specs, out_shape=out_shape,
        compiler_params=_cparams(("arbitrary",)),
        name="merge_out",
    )(*args)


def _finish(acc, x_ref, mod_ref, xo_ref, nxt):
    xn = x_ref[...] + mod_ref[0][5:6] * acc
    xo_ref[...] = xn
    if nxt is not None:
        nw_ref, modn_ref, hn_ref = nxt
        mn = modn_ref[0]
        hn_ref[...] = _norm_mod(xn, nw_ref[...], mn[1:2], mn[0:1]).astype(BF16)


def _ffn_kernel(*refs, has_next):
    if has_next:
        h_ref, w1_ref, w3_ref, w2_ref, x_ref, mod_ref, nw_ref, modn_ref, xo_ref, hn_ref, acc_ref = refs
        nxt = (nw_ref, modn_ref, hn_ref)
    else:
        h_ref, w1_ref, w3_ref, w2_ref, x_ref, mod_ref, xo_ref, acc_ref = refs
        nxt = None
    f = pl.program_id(1)

    @pl.when(f == 0)
    def _():
        acc_ref[...] = jnp.zeros(acc_ref.shape, F32)

    h = h_ref[...]
    a = _dot(h, w1_ref[...])
    act = a * _sigmoid(a) * _dot(h, w3_ref[...])
    acc_ref[...] += _dot(act.astype(BF16), w2_ref[...])

    @pl.when(f == pl.num_programs(1) - 1)
    def _():
        _finish(acc_ref[...], x_ref, mod_ref, xo_ref, nxt)


def _ffn_call(h, w1, w3, w2, x, mod, nxt, seq, tm, tf):
    t, d = x.shape
    ff = w1.shape[1]
    per_b = seq // tm
    tok = lambda: pl.BlockSpec((tm, d), lambda i, f: (i, 0))
    modspec = lambda: pl.BlockSpec((1, 6, d), lambda i, f: (i // per_b, 0, 0))
    in_specs = [tok(), pl.BlockSpec((d, tf), lambda i, f: (0, f)),
                pl.BlockSpec((d, tf), lambda i, f: (0, f)),
                pl.BlockSpec((tf, d), lambda i, f: (f, 0)), tok(), modspec()]
    out_specs = [tok()]
    out_shape = [jax.ShapeDtypeStruct((t, d), F32)]
    args = [h, w1, w3, w2, x, mod]
    if nxt is not None:
        in_specs += [pl.BlockSpec((1, d), lambda i, f: (0, 0)), modspec()]
        out_specs.append(tok())
        out_shape.append(jax.ShapeDtypeStruct((t, d), BF16))
        args += list(nxt)
    return pl.pallas_call(
        functools.partial(_ffn_kernel, has_next=nxt is not None),
        grid=(t // tm, ff // tf),
        in_specs=in_specs, out_specs=out_specs, out_shape=out_shape,
        scratch_shapes=[pltpu.VMEM((tm, d), F32)],
        compiler_params=_cparams(("arbitrary", "arbitrary")),
        name="ffn_swiglu",
    )(*args)


def _moe_kernel(h_ref, cmb_ref, w13_ref, w2_ref, y_ref,
                acc_ref, xg_ref, ya_ref, rk_ref, rkt_ref, cnt_ref):
    e = pl.program_id(1)
    f = pl.program_id(2)
    tm = h_ref.shape[0]

    @pl.when((e == 0) & (f == 0))
    def _():
        r = lax.broadcasted_iota(jnp.int32, (tm, tm), 0)
        c = lax.broadcasted_iota(jnp.int32, (tm, tm), 1)
        before = jnp.where(c < r, 1.0, 0.0).astype(BF16)
        sel = cmb_ref[...] > 0.0
        rank = _dot(before, jnp.where(sel, 1.0, 0.0).astype(BF16))
        rk = jnp.where(sel, rank, -1.0)
        rk_ref[...] = rk
        rkt_ref[...] = rk.T
        cnt_ref[...] = jnp.sum(jnp.where(sel, 1.0, 0.0), axis=0, keepdims=True)
        acc_ref[...] = jnp.zeros(acc_ref.shape, F32)

    lane1 = lax.broadcasted_iota(jnp.int32, (1, LANES), 1)
    n_e = jnp.sum(jnp.where(lane1 == e, cnt_ref[...], 0.0)).astype(jnp.int32)

    def loop(n_blocks, body):
        if isinstance(n_blocks, int):
            for sb in range(n_blocks):
                body(sb, 0)
        else:
            lax.fori_loop(0, n_blocks, body, 0)

    def run(cap, n_blocks):
        cap_pad = -(-cap // LANES) * LANES

        def row0(sb):
            return sb * cap if isinstance(sb, int) else pl.multiple_of(sb * cap, cap)

        @pl.when(f == 0)
        def _():
            rank_row = rkt_ref[pl.ds(e, 1), :]

            def gather(sb, carry):
                r0 = row0(sb)
                slot = (r0 + lax.broadcasted_iota(jnp.int32, (cap, tm), 0)).astype(F32)
                onehot = jnp.where(rank_row == slot, 1.0, 0.0).astype(BF16)
                xg_ref[pl.ds(r0, cap), :] = _dot(onehot, h_ref[...]).astype(BF16)
                return carry

            loop(n_blocks, gather)

        def expert(sb, carry):
            r0 = row0(sb)
            xs = xg_ref[pl.ds(r0, cap), :]
            ab = _dot(xs, w13_ref[0])
            tf = w2_ref.shape[1]
            a = ab[:, :tf]
            act = a * _sigmoid(a) * ab[:, tf:]
            part = _dot(act.astype(BF16), w2_ref[0])

            @pl.when(f == 0)
            def _():
                ya_ref[pl.ds(r0, cap), :] = part

            @pl.when(f > 0)
            def _():
                ya_ref[pl.ds(r0, cap), :] += part

            return carry

        loop(n_blocks, expert)

        @pl.when(f == pl.num_programs(2) - 1)
        def _():
            lane = lax.broadcasted_iota(jnp.int32, (tm, LANES), 1)
            rank_col = jnp.sum(jnp.where(lane == e, rk_ref[...], 0.0), axis=1, keepdims=True)
            w_col = jnp.sum(jnp.where(lane == e, cmb_ref[...], 0.0), axis=1, keepdims=True)
            col = lax.broadcasted_iota(jnp.int32, (tm, cap_pad), 1)

            def scatter(sb, carry):
                r0 = row0(sb)
                ys = ya_ref[pl.ds(r0, cap), :].astype(BF16)
                if cap_pad > cap:
                    ys = jnp.concatenate([ys, jnp.zeros((cap_pad - cap, ys.shape[1]), BF16)], axis=0)
                hit = (rank_col == (r0 + col).astype(F32)) & (col < cap)
                acc_ref[...] += w_col * _dot(jnp.where(hit, 1.0, 0.0).astype(BF16), ys)
                return carry

            loop(n_blocks, scatter)

    for idx, cap in enumerate(MOE_CAPS):
        lower = MOE_CAPS[idx - 1] if idx else 0
        if idx == len(MOE_CAPS) - 1:
            pl.when(n_e > lower)(functools.partial(run, cap, (n_e + cap - 1) // cap))
        else:
            pl.when((n_e > lower) & (n_e <= cap))(functools.partial(run, cap, 1))

    @pl.when((e == pl.num_programs(1) - 1) & (f == pl.num_programs(2) - 1))
    def _():
        y_ref[...] = acc_ref[...].astype(y_ref.dtype)


def _moe_pack_w13(w1, w3):
    ff = w1.shape[-1]
    tf = _pick(ff, (MOE_TF,))
    parts = []
    for f in range(ff // tf):
        parts += [w1[:, :, f * tf:(f + 1) * tf].astype(BF16), w3[:, :, f * tf:(f + 1) * tf].astype(BF16)]
    return jnp.concatenate(parts, axis=-1)


def _moe_call(h, cmb, w13, w2, tm):
    t, d = h.shape
    ne, ff, _ = w2.shape
    tf = _pick(ff, (MOE_TF,))
    rows = -(-tm // MOE_CAPS[-1]) * MOE_CAPS[-1]
    return pl.pallas_call(
        _moe_kernel,
        grid=(t // tm, ne, ff // tf),
        in_specs=[pl.BlockSpec((tm, d), lambda i, e, f: (i, 0)),
                  pl.BlockSpec((tm, LANES), lambda i, e, f: (i, 0)),
                  pl.BlockSpec((1, d, 2 * tf), lambda i, e, f: (e, 0, f)),
                  pl.BlockSpec((1, tf, d), lambda i, e, f: (e, f, 0))],
        out_specs=pl.BlockSpec((tm, d), lambda i, e, f: (i, 0)),
        out_shape=jax.ShapeDtypeStruct((t, d), BF16),
        scratch_shapes=[pltpu.VMEM((tm, d), F32),
                        pltpu.VMEM((rows, d), BF16),
                        pltpu.VMEM((rows, d), F32),
                        pltpu.VMEM((tm, LANES), F32),
                        pltpu.VMEM((LANES, tm), F32),
                        pltpu.VMEM((1, LANES), F32)],
        compiler_params=_cparams(("arbitrary", "arbitrary", "arbitrary")),
        name="moe_top2",
    )(h, cmb, w13, w2)


def _resid_kernel(*refs, has_next):
    if has_next:
        y_ref, x_ref, mod_ref, nw_ref, modn_ref, xo_ref, hn_ref = refs
        nxt = (nw_ref, modn_ref, hn_ref)
    else:
        y_ref, x_ref, mod_ref, xo_ref = refs
        nxt = None
    _finish(y_ref[...].astype(F32), x_ref, mod_ref, xo_ref, nxt)


def _resid_call(y, x, mod, nxt, seq, tm):
    t, d = x.shape
    per_b = seq // tm
    tok = lambda: pl.BlockSpec((tm, d), lambda i: (i, 0))
    modspec = lambda: pl.BlockSpec((1, 6, d), lambda i: (i // per_b, 0, 0))
    in_specs = [tok(), tok(), modspec()]
    out_specs = [tok()]
    out_shape = [jax.ShapeDtypeStruct((t, d), F32)]
    args = [y, x, mod]
    if nxt is not None:
        in_specs += [pl.BlockSpec((1, d), lambda i: (0, 0)), modspec()]
        out_specs.append(tok())
        out_shape.append(jax.ShapeDtypeStruct((t, d), BF16))
        args += list(nxt)
    return pl.pallas_call(
        functools.partial(_resid_kernel, has_next=nxt is not None),
        grid=(t // tm,),
        in_specs=in_specs, out_specs=out_specs, out_shape=out_shape,
        compiler_params=_cparams(("arbitrary",)),
        name="moe_residual",
    )(*args)


def _attn_col_perm():
    hq = DA_HEADS * DA_DK
    idx = []
    for base in (0, 2 * hq):
        for h in range(DA_HEADS):
            idx += list(range(base + h * DA_DK, base + (h + 1) * DA_DK))
            idx += list(range(base + hq + h * DA_DK, base + hq + (h + 1) * DA_DK))
    idx += list(range(4 * hq, 4 * hq + DA_HEADS * DA_DV))
    return np.asarray(idx, np.int32)


def _pick(n, pref):
    for c in pref:
        if n % c == 0:
            return c
    return n


def kernel(x, c, w_mod, b_mod, norm1_w, norm2_w, w_in, c_conv_w, c_conv_b, a_qnorm_w, a_knorm_w,
           a_lambda_q1, a_lambda_k1, a_lambda_q2, a_lambda_k2, a_subln_w, b_lb_logits, b_gnorm_w,
           c_igate_b, c_fgate_b, c_norm_w, w_branch, w_out, ffn_w1, ffn_w3, ffn_w2,
           moe_router_w, moe_router_b, moe_w1, moe_w3, moe_w2):
    batch, seq, d = x.shape
    depth = w_in.shape[0]
    t = batch * seq
    n_a = 4 * DA_HEADS * DA_DK + DA_HEADS * DA_DV
    n_b = 4 * HG_HEADS * HG_D
    n_c = 4 * ML_HEADS * ML_D + 2 * ML_HEADS
    n_c_pad = -(-n_c // LANES) * LANES
    tm = _pick(seq, (1024, 512, 256))
    tq = _pick(seq, (ATT_TQ,))
    ts_h = _pick(seq, (512, 256, 128))

    c8 = jnp.zeros((8, d), F32).at[:batch].set(c)
    mod = _mod_call(c8, w_mod, b_mod.reshape(depth, 1, 6 * d))[:, :batch].reshape(depth, batch, 6, d)

    lb_all = jnp.cumsum(jax.nn.softmax(b_lb_logits.astype(F32), axis=0), axis=0)
    lb_all = lb_all - lb_all[:1]
    slopes = jnp.asarray(_attn_slopes(), F32)

    xf = x.reshape(t, d)
    h = _normmod_call(xf, norm1_w[0].reshape(1, d), mod[0], seq, tm)
    perm_a = _attn_col_perm()
    w_a_all = w_in[:, :, :n_a][:, :, perm_a].astype(BF16)
    w_b_all = w_in[:, :, n_a:n_a + n_b].astype(BF16)
    w_c_all = jnp.pad(w_in[:, :, n_a + n_b:n_a + n_b + n_c].astype(BF16),
                      ((0, 0), (0, 0), (0, n_c_pad - n_c)))
    w_g_all = w_in[:, :, n_a + n_b + n_c:].astype(BF16)
    for l in range(depth):
        lam_init = 0.8 - 0.6 * math.exp(-0.3 * l)
        w_a, w_b, w_c, w_g = w_a_all[l], w_b_all[l], w_c_all[l], w_g_all[l]
        bias_c = jnp.zeros((1, n_c_pad), F32)
        bias_c = bias_c.at[0, n_c - 2 * ML_HEADS:n_c - ML_HEADS].set(c_igate_b[l])
        bias_c = bias_c.at[0, n_c - ML_HEADS:n_c].set(c_fgate_b[l])
        pb = _matmul(h, w_b, jnp.zeros((1, n_b), F32), F32, tm, n_b // 2)
        pc = _matmul(h, w_c, bias_c, F32, tm, n_c_pad)
        pg = _matmul(h, w_g, jnp.zeros((1, 3 * d), F32), BF16, tm, 3 * d // 2)

        qw = (jnp.tile(a_qnorm_w[l], 2 * DA_HEADS) * (DA_DK ** -0.5 * LOG2E)).reshape(1, -1)
        kw = jnp.tile(a_knorm_w[l], 2 * DA_HEADS).reshape(1, -1)
        kn, qt, vt = _attn_prep_call(h, w_a, qw, kw, batch, seq, _pick(seq, (512, 256)))
        lam = (jnp.exp(jnp.sum(a_lambda_q1[l] * a_lambda_k1[l]))
               - jnp.exp(jnp.sum(a_lambda_q2[l] * a_lambda_k2[l])) + lam_init)
        scal = jnp.concatenate([slopes, jnp.stack([lam, jnp.asarray(1.0 - lam_init, F32)])]).astype(F32)
        y_a = _attn_call(scal, qt, kn, vt, a_subln_w[l].reshape(1, -1), batch, seq, tq)

        lb = lb_all[l].reshape(HG_HEADS, 1, HG_D)
        lbs = jnp.concatenate([jnp.log(lb), jnp.log1p(-lb)], axis=1)
        y_b = _hgrn_call(pb, lbs, b_gnorm_w[l].reshape(1, -1), batch, seq, ts_h)

        gates = pc[:, n_c - 2 * ML_HEADS:n_c].reshape(batch, seq, 2, ML_HEADS)
        gates = gates.transpose(2, 0, 3, 1).reshape(2, batch, ML_HEADS, seq // ML_CHUNK, ML_CHUNK)
        y_c = _mlstm_call(pc, gates[0], gates[1], c_conv_w[l], c_conv_b[l].reshape(1, -1),
                          c_norm_w[l].reshape(1, -1), batch, seq)

        dense = l % 2 == 0
        router = None
        if not dense:
            rw = jnp.pad(moe_router_w[l // 2], ((0, 0), (0, LANES - N_EXPERTS)))
            rb = jnp.pad(moe_router_b[l // 2], (0, LANES - N_EXPERTS)).reshape(1, LANES)
            router = (rw, rb)
        outs = _merge_call(y_a, y_b, y_c, pg, w_branch[l].astype(BF16), w_out[l].astype(BF16), xf,
                           mod[l], norm2_w[l].reshape(1, d), router, seq, 512)
        xf, h2 = outs[0], outs[1]
        nxt = None if l == depth - 1 else (norm1_w[l + 1].reshape(1, d), mod[l + 1])
        if dense:
            res = _ffn_call(h2, ffn_w1[l // 2].astype(BF16), ffn_w3[l // 2].astype(BF16),
                            ffn_w2[l // 2].astype(BF16), xf, mod[l], nxt, seq, tm, FFN_TF)
        else:
            y = _moe_call(h2, outs[2], _moe_pack_w13(moe_w1[l // 2], moe_w3[l // 2]),
                          moe_w2[l // 2].astype(BF16), tm)
            res = _resid_call(y, xf, mod[l], nxt, seq, tm)
        xf = res[0]
        if nxt is not None:
            h = res[1]
    return xf.reshape(batch, seq, d)
```

```python
import functools
import math

import numpy as np
import jax
import jax.numpy as jnp
from jax import lax
from jax.experimental import pallas as pl
from jax.experimental.pallas import tpu as pltpu

F32 = jnp.float32
BF16 = jnp.bfloat16
NORM_EPS = 1e-6

CHUNK = 64
DA_HEADS, DA_DK, DA_DV = 4, 64, 128
HG_HEADS, HG_D = 4, 128
ML_HEADS, ML_D, ML_CONV = 4, 128, 4
N_EXPERTS = 8

LANES = 128
VMEM_LIMIT = 56 * 1024 * 1024

ATT_TQ = 256
ATT_HEADS_PER_STEP = 2
ATT_PAD_ROWS = 16
LOG2E = math.log2(math.e)
ATT_BOUND_MARGIN = 1.01
ATT_BOUND_EPS = 1e-3
ATT_BOUND_MAX = 40.0
ATT_ZERO_EXP = 160.0
HG_CHUNK = 128
HG_SUB = 32
HG_SAFE_DECAY = 80.0
ML_CHUNK = 256
MOE_CAPS = (256, 320, 384)
MOE_TF = 1408
FFN_TF = 256


def _cparams(sem, vmem=VMEM_LIMIT):
    return pltpu.CompilerParams(dimension_semantics=sem, vmem_limit_bytes=vmem)


def _dot(a, b):
    return jnp.dot(a, b, preferred_element_type=F32)


def _dot_nt(a, b):
    return lax.dot_general(a, b, (((1,), (1,)), ((), ())), preferred_element_type=F32)


def _split3(x):
    hi = x.astype(BF16)
    r1 = x - hi.astype(F32)
    mid = r1.astype(BF16)
    lo = (r1 - mid.astype(F32)).astype(BF16)
    return hi, mid, lo


def _norm_mod(x, w, scale, shift):
    y = x * lax.rsqrt(jnp.mean(x * x, axis=-1, keepdims=True) + NORM_EPS) * w
    return y * (1.0 + scale) + shift


def _sigmoid(x):
    return 0.5 * jnp.tanh(0.5 * x) + 0.5


def _log1p_exp_neg(z):
    return jnp.log(1.0 + jnp.exp(-z))


def _log_sigmoid(x):
    return jnp.minimum(x, 0.0) - _log1p_exp_neg(jnp.abs(x))


def _mod_kernel(c_ref, w_ref, b_ref, o_ref):
    cnd = c_ref[...]
    cnd = cnd * _sigmoid(cnd)
    o_ref[0] = _dot(cnd.astype(BF16), w_ref[0].astype(BF16)) + b_ref[0]


def _mod_call(c8, w_mod, b_mod):
    depth, d, n = w_mod.shape
    tn = 1536 if n % 1536 == 0 else n
    return pl.pallas_call(
        _mod_kernel,
        grid=(depth, n // tn),
        in_specs=[pl.BlockSpec((8, d), lambda l, j: (0, 0)),
                  pl.BlockSpec((1, d, tn), lambda l, j: (l, 0, j)),
                  pl.BlockSpec((1, 1, tn), lambda l, j: (l, 0, j))],
        out_specs=pl.BlockSpec((1, 8, tn), lambda l, j: (l, 0, j)),
        out_shape=jax.ShapeDtypeStruct((depth, 8, n), F32),
        compiler_params=_cparams(("arbitrary", "arbitrary")),
        name="adaln_mod",
    )(c8, w_mod, b_mod)


def _normmod_kernel(x_ref, nw_ref, mod_ref, h_ref):
    m = mod_ref[0]
    h_ref[...] = _norm_mod(x_ref[...], nw_ref[...], m[1:2], m[0:1]).astype(BF16)


def _normmod_call(x, nw, mod, seq, tm):
    t, d = x.shape
    per_b = seq // tm
    return pl.pallas_call(
        _normmod_kernel,
        grid=(t // tm,),
        in_specs=[pl.BlockSpec((tm, d), lambda i: (i, 0)),
                  pl.BlockSpec((1, d), lambda i: (0, 0)),
                  pl.BlockSpec((1, 6, d), lambda i: (i // per_b, 0, 0))],
        out_specs=pl.BlockSpec((tm, d), lambda i: (i, 0)),
        out_shape=jax.ShapeDtypeStruct((t, d), BF16),
        compiler_params=_cparams(("arbitrary",)),
        name="prenorm_mod",
    )(x, nw, mod)


def _mm_kernel(x_ref, w_ref, b_ref, o_ref):
    o_ref[...] = (_dot(x_ref[...], w_ref[...]) + b_ref[...]).astype(o_ref.dtype)


def _matmul(x, w, bias, out_dtype, tm, tn):
    t, k = x.shape
    n = w.shape[1]
    return pl.pallas_call(
        _mm_kernel,
        grid=(n // tn, t // tm),
        in_specs=[pl.BlockSpec((tm, k), lambda j, i: (i, 0)),
                  pl.BlockSpec((k, tn), lambda j, i: (0, j)),
                  pl.BlockSpec((1, tn), lambda j, i: (0, j))],
        out_specs=pl.BlockSpec((tm, tn), lambda j, i: (i, j)),
        out_shape=jax.ShapeDtypeStruct((t, n), out_dtype),
        compiler_params=_cparams(("arbitrary", "arbitrary")),
        name="in_proj",
    )(x, w, bias)


def _mm_tail_kernel(x_ref, w_ref, b_ref, o_ref, tail_ref):
    res = _dot(x_ref[...], w_ref[...]) + b_ref[...]
    main = o_ref.shape[1]
    o_ref[...] = res[:, :main]
    tail_ref[...] = res[:, main:]


def _matmul_tail(x, w, bias, tm, tail):
    t, k = x.shape
    n = w.shape[1]
    return pl.pallas_call(
        _mm_tail_kernel,
        grid=(t // tm,),
        in_specs=[pl.BlockSpec((tm, k), lambda i: (i, 0)),
                  pl.BlockSpec((k, n), lambda i: (0, 0)),
                  pl.BlockSpec((1, n), lambda i: (0, 0))],
        out_specs=[pl.BlockSpec((tm, n - tail), lambda i: (i, 0)),
                   pl.BlockSpec((tm, tail), lambda i: (i, 0))],
        out_shape=[jax.ShapeDtypeStruct((t, n - tail), F32),
                   jax.ShapeDtypeStruct((t, tail), F32)],
        compiler_params=_cparams(("arbitrary",)),
        name="in_proj_c",
    )(x, w, bias)


def _attn_prep_kernel(x_ref, w_ref, qw_ref, kw_ref, kn_ref, qt_ref, vt_ref):
    hw = DA_HEADS * 2 * DA_DK
    a = _dot(x_ref[...], w_ref[...])
    r = lax.broadcasted_iota(jnp.int32, (hw, hw), 0) // DA_DK
    c = lax.broadcasted_iota(jnp.int32, (hw, hw), 1) // DA_DK
    group = jnp.where(r == c, 1.0, 0.0).astype(BF16)

    def qk_norm(z, w):
        ms = _dot((z * z).astype(BF16), group) * (1.0 / DA_DK)
        return z * lax.rsqrt(ms + NORM_EPS) * w

    qn = qk_norm(a[:, :hw], qw_ref[...])
    kn = qk_norm(a[:, hw:2 * hw], kw_ref[...])
    kn_ref[...] = kn.astype(BF16)
    qt_ref[0] = qn.T.astype(BF16)
    vt_ref[0] = a[:, 2 * hw:].T.astype(BF16)


def _attn_prep_call(x, w, qw, kw, batch, seq, ts):
    t, d = x.shape
    hw = DA_HEADS * 2 * DA_DK
    hv = DA_HEADS * DA_DV
    per_b = seq // ts
    return pl.pallas_call(
        _attn_prep_kernel,
        grid=(t // ts,),
        in_specs=[pl.BlockSpec((ts, d), lambda i: (i, 0)),
                  pl.BlockSpec((d, 2 * hw + hv), lambda i: (0, 0)),
                  pl.BlockSpec((1, hw), lambda i: (0, 0)),
                  pl.BlockSpec((1, hw), lambda i: (0, 0))],
        out_specs=[pl.BlockSpec((ts, hw), lambda i: (i, 0)),
                   pl.BlockSpec((1, hw, ts), lambda i: (i // per_b, 0, i % per_b)),
                   pl.BlockSpec((1, hv, ts), lambda i: (i // per_b, 0, i % per_b))],
        out_shape=[jax.ShapeDtypeStruct((t, hw), BF16),
                   jax.ShapeDtypeStruct((batch, hw, seq), BF16),
                   jax.ShapeDtypeStruct((batch, hv, seq), BF16)],
        compiler_params=_cparams(("arbitrary",)),
        name="attn_prep",
    )(x, w, qw, kw)


def _attn_slopes():
    return [LOG2E * 2.0 ** (-8.0 * (h + 1) / DA_HEADS) for h in range(DA_HEADS)]


def _attn_window(head, tq):
    return int((ATT_ZERO_EXP / _attn_slopes()[head] - 1.0) // tq) + 1


def _attn_kernel(sc_ref, qt_ref, k_ref, vt_ref, sw_ref, o_ref,
                 qz_ref, m_ref, acc_ref, vs_ref, kmax_ref, s_ref, s2_ref, *, tq, seq):
    hp = ATT_HEADS_PER_STEP
    dv = DA_DV
    g = pl.program_id(1)
    i = pl.program_id(2)
    lam = sc_ref[DA_HEADS]
    out_scale = sc_ref[DA_HEADS + 1]
    slopes = [sc_ref[hp * g + hh] for hh in range(hp)]
    row16 = lax.broadcasted_iota(jnp.int32, (ATT_PAD_ROWS, tq), 0)
    ones_rows = jnp.where(row16 == 0, 1.0, 0.0).astype(BF16)

    half = lax.broadcasted_iota(jnp.int32, (1, 2 * DA_DK), 1) < DA_DK

    @pl.when(i == 0)
    def _():
        pos = lax.broadcasted_iota(jnp.int32, (1, seq), 1) & (tq - 1)
        rel = (pos - (tq - 1)).astype(F32)
        sub16 = lax.broadcasted_iota(jnp.int32, (ATT_PAD_ROWS, seq), 0)
        gr = lax.broadcasted_iota(jnp.int32, (2 * DA_DK, 2 * DA_DK), 0) // DA_DK
        gc = lax.broadcasted_iota(jnp.int32, (2 * DA_DK, 2 * DA_DK), 1) // DA_DK
        group = jnp.where(gr == gc, 1.0, 0.0).astype(BF16)
        for hh in range(hp):
            w = jnp.exp2(slopes[hh] * rel)
            vs_ref[hh, 0:dv, :] = (vt_ref[0, hh * dv:(hh + 1) * dv, :].astype(F32) * w).astype(BF16)
            vs_ref[hh, dv:dv + ATT_PAD_ROWS, :] = jnp.where(sub16 == 0, w, 0.0).astype(BF16)

            def knorm(n, best):
                kc = k_ref[pl.ds(pl.multiple_of(n * tq, tq), tq),
                           hh * 2 * DA_DK:(hh + 1) * 2 * DA_DK].astype(F32)
                return jnp.maximum(best, jnp.max(_dot((kc * kc).astype(BF16), group), axis=0, keepdims=True))

            k2 = lax.fori_loop(0, seq // tq, knorm, jnp.zeros((1, 2 * DA_DK), F32))
            kmax_ref[2 * hh] = jnp.max(jnp.where(half, k2, 0.0), axis=1, keepdims=True)
            kmax_ref[2 * hh + 1] = jnp.max(jnp.where(half, 0.0, k2), axis=1, keepdims=True)

    row = lax.broadcasted_iota(jnp.int32, (2 * DA_DK, tq), 0)
    qq_row = lax.broadcasted_iota(jnp.int32, (1, tq), 1).astype(F32)
    bound_max = jnp.zeros((), F32)
    firsts = []

    def scores(hh, blk):
        start = pl.multiple_of(blk * tq, tq)
        return _dot(k_ref[pl.ds(start, tq), hh * 2 * DA_DK:(hh + 1) * 2 * DA_DK], qz_ref[hh])

    for hh in range(hp):
        qt = qt_ref[0, hh * 2 * DA_DK:(hh + 1) * 2 * DA_DK, :]
        zero = jnp.zeros_like(qt)
        q1 = jnp.where(row < DA_DK, qt, zero)
        q2 = jnp.where(row >= DA_DK, qt, zero)
        qz_ref[hh] = jnp.concatenate([q1, q2], axis=1)
        window = jnp.int32(_attn_window(hh, tq))
        for grp in range(1, DA_HEADS // hp):
            window = jnp.where(g == grp, _attn_window(grp * hp + hh, tq), window)
        firsts.append((jnp.maximum(i - window, 0) // 2) * 2)
        s_ref[hh] = scores(hh, firsts[hh])
        s2_ref[hh] = scores(hh, jnp.minimum(firsts[hh] + 1, i))
        qn = jnp.concatenate(
            [jnp.sum(jnp.square(q1.astype(F32)), axis=0, keepdims=True) * kmax_ref[2 * hh],
             jnp.sum(jnp.square(q2.astype(F32)), axis=0, keepdims=True) * kmax_ref[2 * hh + 1]], axis=1)
        bound = jnp.sqrt(qn) * ATT_BOUND_MARGIN + ATT_BOUND_EPS
        bound_max = jnp.maximum(bound_max, jnp.max(bound))
        m_ref[hh] = bound + slopes[hh] * jnp.concatenate([qq_row, qq_row], axis=1)
    acc_ref[...] = jnp.zeros(acc_ref.shape, F32)

    kk = lax.broadcasted_iota(jnp.int32, (tq, 2 * tq), 0)
    cc = lax.broadcasted_iota(jnp.int32, (tq, 2 * tq), 1)
    qq = jnp.where(cc >= tq, cc - tq, cc)
    dist = (qq - jnp.abs(qq - kk)).astype(F32)
    visible = (kk // CHUNK) <= (qq // CHUNK)

    def diag_values(hh):
        start = pl.multiple_of(i * tq, tq)
        return jnp.concatenate([vt_ref[0, hh * dv:(hh + 1) * dv, pl.ds(start, tq)], ones_rows], axis=0)

    @pl.when(bound_max <= ATT_BOUND_MAX)
    def _():
        def probs(hh, blk, s):
            last_key = ((blk + 1 - i) * tq - 1).astype(F32)
            return jnp.exp2(s - (m_ref[hh] - slopes[hh] * last_key)).astype(BF16)

        def body(heads, jj, carry):
            blk = 2 * jj
            start = pl.multiple_of(blk * tq, 2 * tq)
            for hh in heads:
                p = jnp.concatenate([probs(hh, blk, s_ref[hh]), probs(hh, blk + 1, s2_ref[hh])], axis=0)
                s_ref[hh] = scores(hh, jnp.minimum(blk + 2, i))
                s2_ref[hh] = scores(hh, jnp.minimum(blk + 3, i))
                acc_ref[hh] += _dot(vs_ref[hh, :, pl.ds(start, 2 * tq)], p)
            return carry

        for hh in range(hp - 1, 0, -1):
            lax.fori_loop(firsts[hh] // 2, firsts[hh - 1] // 2,
                          functools.partial(body, tuple(range(hh, hp))), 0)
        lax.fori_loop(firsts[0] // 2, i // 2, functools.partial(body, tuple(range(hp))), 0)

        @pl.when(i % 2 == 1)
        def _():
            start = pl.multiple_of((i - 1) * tq, tq)
            for hh in range(hp):
                acc_ref[hh] += _dot(vs_ref[hh, :, pl.ds(start, tq)], probs(hh, i - 1, s_ref[hh]))
                s_ref[hh] = s2_ref[hh]

        for hh in range(hp):
            s = jnp.where(visible, s_ref[hh] + slopes[hh] * dist, -jnp.inf)
            acc_ref[hh] += _dot(diag_values(hh), jnp.exp2(s - m_ref[hh]).astype(BF16))

    @pl.when(bound_max > ATT_BOUND_MAX)
    def _():
        m_ref[...] = jnp.full(m_ref.shape, -jnp.inf, F32)

        def update(hh, s, bound_shift, v_aug):
            m_old = m_ref[hh]
            m_new = jnp.maximum(m_old, jnp.max(s, axis=0, keepdims=True) + bound_shift)
            p = jnp.exp2(s - (m_new - bound_shift))
            acc_ref[hh] = jnp.exp2(m_old - m_new) * acc_ref[hh] + _dot(v_aug, p.astype(BF16))
            m_ref[hh] = m_new

        def body(j, carry):
            start = pl.multiple_of(j * tq, tq)
            last_key = ((j + 1 - i) * tq - 1).astype(F32)
            for hh in range(hp):
                update(hh, scores(hh, j), slopes[hh] * last_key, vs_ref[hh, :, pl.ds(start, tq)])
            return carry

        lax.fori_loop(0, i, body, 0)
        for hh in range(hp):
            s = jnp.where(visible, scores(hh, i) + slopes[hh] * dist, -jnp.inf)
            update(hh, s, jnp.zeros((), F32), diag_values(hh))

    outs = []
    for hh in range(hp):
        acc = acc_ref[hh]
        o2 = acc[:dv, :] * (1.0 / acc[dv:dv + 1, :])
        o = (o2[:, :tq] - lam * o2[:, tq:]).T
        o = o * lax.rsqrt(jnp.mean(o * o, axis=-1, keepdims=True) + NORM_EPS) * sw_ref[...]
        outs.append(o * out_scale)
    o_ref[...] = jnp.concatenate(outs, axis=1).astype(o_ref.dtype)


def _attn_call(scal, qt, kn, vt, sw, batch, seq, tq):
    t = kn.shape[0]
    nq = seq // tq
    hp = ATT_HEADS_PER_STEP
    return pl.pallas_call(
        functools.partial(_attn_kernel, tq=tq, seq=seq),
        grid=(batch, DA_HEADS // hp, nq),
        in_specs=[pl.BlockSpec(memory_space=pltpu.SMEM),
                  pl.BlockSpec((1, hp * 2 * DA_DK, tq), lambda b, g, i: (b, g, i)),
                  pl.BlockSpec((seq, hp * 2 * DA_DK), lambda b, g, i: (b, g)),
                  pl.BlockSpec((1, hp * DA_DV, seq), lambda b, g, i: (b, g, 0)),
                  pl.BlockSpec((1, DA_DV), lambda b, g, i: (0, 0))],
        out_specs=pl.BlockSpec((tq, hp * DA_DV), lambda b, g, i: (b * nq + i, g)),
        out_shape=jax.ShapeDtypeStruct((t, DA_HEADS * DA_DV), BF16),
        scratch_shapes=[pltpu.VMEM((hp, 2 * DA_DK, 2 * tq), BF16),
                        pltpu.VMEM((hp, 1, 2 * tq), F32),
                        pltpu.VMEM((hp, DA_DV + ATT_PAD_ROWS, 2 * tq), F32),
                        pltpu.VMEM((hp, DA_DV + ATT_PAD_ROWS, seq), BF16),
                        pltpu.VMEM((2 * hp, 1, 1), F32),
                        pltpu.VMEM((hp, tq, 2 * tq), F32),
                        pltpu.VMEM((hp, tq, 2 * tq), F32)],
        compiler_params=_cparams(("arbitrary", "arbitrary", "arbitrary")),
        name="diff_attn",
    )(scal, qt, kn, vt, sw)


def _hgrn_kernel(q_ref, f_ref, i_ref, g_ref, lb_ref, gw_ref, o_ref,
                 st_ref, b_ref, qs_ref, ks_ref, oi_ref, *, n_chunks):
    c = HG_CHUNK
    sub = HG_SUB
    d = HG_D
    nh = HG_HEADS
    chains = [(bb, hh) for bb in range(q_ref.shape[0]) for hh in range(nh)]

    @pl.when(pl.program_id(0) == 0)
    def _():
        st_ref[...] = jnp.zeros(st_ref.shape, F32)

    gw = gw_ref[...]
    rr = lax.broadcasted_iota(jnp.int32, (c, c), 0)
    cc = lax.broadcasted_iota(jnp.int32, (c, c), 1)
    causal = cc <= rr
    tril = jnp.where(causal, 1.0, 0.0).astype(BF16)

    def chunk(n, carry):
        r0 = pl.multiple_of(n * c, c)
        gates = []
        for ch, (bb, hh) in enumerate(chains):
            cols = slice(hh * d, (hh + 1) * d)
            log_lb = lb_ref[hh, 0:1, :]
            q = q_ref[bb, pl.ds(r0, c), cols]
            q = q * _sigmoid(q)
            a = lb_ref[hh, 1:2, :] + _log_sigmoid(f_ref[bb, pl.ds(r0, c), cols])
            logf = jnp.maximum(log_lb, a) + _log1p_exp_neg(jnp.abs(log_lb - a))
            k = 1.0 - jnp.exp(logf)
            v = i_ref[bb, pl.ds(r0, c), cols]
            hi, mid, lo = _split3(logf)
            b = _dot(tril, hi) + _dot(tril, mid) + _dot(tril, lo)
            gates.append((q, k, v, b))

        heads = []
        decay = jnp.zeros((), F32)
        for ch, (q, k, v, b) in enumerate(gates):
            b_last = b[c - 1:c, :]
            st = st_ref[ch]
            o_inter = _dot_nt((q * jnp.exp(b)).astype(BF16), st.astype(BF16))
            k_hat = k * jnp.exp(b_last - b)
            st_ref[ch] = st * jnp.exp(b_last) + _dot(v.T.astype(BF16), k_hat.astype(BF16))
            betas = []
            for blk in range(c // sub):
                beta = jnp.zeros((1, d), F32) if blk == 0 else b[blk * sub - 1:blk * sub, :]
                betas.append(beta)
                b_end = b[(blk + 1) * sub - 1:(blk + 1) * sub, :]
                decay = jnp.maximum(decay, jnp.max(beta - b_end))
            heads.append((q, k, v, b, betas, o_inter))

        @pl.when(decay < HG_SAFE_DECAY)
        def _():
            atts = []
            for q, k, v, b, betas, _ in heads:
                rows = []
                for blk in range(c // sub):
                    beta = betas[blk]
                    q_t = q[blk * sub:(blk + 1) * sub, :] * jnp.exp(b[blk * sub:(blk + 1) * sub, :] - beta)
                    k_t = k * jnp.exp(jnp.minimum(beta - b, HG_SAFE_DECAY))
                    rows.append(_dot_nt(q_t.astype(BF16), k_t.astype(BF16)))
                atts.append(jnp.where(causal, jnp.concatenate(rows, axis=0), 0.0).astype(BF16))
            for ch, att in enumerate(atts):
                oi_ref[ch] = _dot(att, heads[ch][2].astype(BF16))

        @pl.when(decay >= HG_SAFE_DECAY)
        def _():
            ridx = lax.broadcasted_iota(jnp.int32, (c, d), 0)
            for ch, (q, k, v, b, _, _) in enumerate(heads):
                b_ref[...] = b
                qs_ref[...] = q
                ks_ref[...] = k

                def row(t, carry2):
                    bt = b_ref[pl.ds(t, 1), :]
                    e = jnp.exp(jnp.where(ridx <= t, bt - b_ref[...], -jnp.inf))
                    w = jnp.sum(qs_ref[pl.ds(t, 1), :] * ks_ref[...] * e, axis=1, keepdims=True)
                    oi_ref[ch, pl.ds(t, 1), :] = jnp.sum(w * v, axis=0, keepdims=True)
                    return carry2

                lax.fori_loop(0, c, row, 0)

        for ch, (bb, hh) in enumerate(chains):
            cols = slice(hh * d, (hh + 1) * d)
            o = heads[ch][5] + oi_ref[ch]
            o = o * lax.rsqrt(jnp.mean(o * o, axis=-1, keepdims=True) + NORM_EPS) * gw
            g = g_ref[bb, pl.ds(r0, c), cols]
            o_ref[bb, pl.ds(r0, c), cols] = (o * (g * _sigmoid(g))).astype(o_ref.dtype)
        return carry

    lax.fori_loop(0, n_chunks, chunk, 0)


def _hgrn_call(bproj, lbs, gw, batch, seq, ts):
    t = bproj.shape[0]
    d = HG_D
    nh = HG_HEADS
    bp3 = bproj.reshape(batch, seq, bproj.shape[1])
    spec = lambda off: pl.BlockSpec((batch, ts, nh * d), lambda i: (0, i, off))
    out = pl.pallas_call(
        functools.partial(_hgrn_kernel, n_chunks=ts // HG_CHUNK),
        grid=(seq // ts,),
        in_specs=[spec(0), spec(1), spec(2), spec(3),
                  pl.BlockSpec((nh, 2, d), lambda i: (0, 0, 0)),
                  pl.BlockSpec((1, d), lambda i: (0, 0))],
        out_specs=pl.BlockSpec((batch, ts, nh * d), lambda i: (0, i, 0)),
        out_shape=jax.ShapeDtypeStruct((batch, seq, nh * d), BF16),
        scratch_shapes=[pltpu.VMEM((batch * nh, d, d), F32),
                        pltpu.VMEM((HG_CHUNK, d), F32),
                        pltpu.VMEM((HG_CHUNK, d), F32),
                        pltpu.VMEM((HG_CHUNK, d), F32),
                        pltpu.VMEM((batch * nh, HG_CHUNK, d), F32)],
        compiler_params=_cparams(("arbitrary",)),
        name="hgrn2",
    )(bp3, bp3, bp3, bp3, lbs, gw)
    return out.reshape(t, nh * d)


def _mlstm_kernel(uq_ref, uk_ref, v_ref, op_ref, gi_ref, gf_ref, cwq_ref, cwk_ref,
                  cbq_ref, cbk_ref, nw_ref, o_ref,
                  xq_ref, xk_ref, c_ref, m_ref, bs_ref):
    L = ML_CHUNK
    d = ML_D
    nh = ML_HEADS
    nb = uq_ref.shape[0]
    i = pl.program_id(0)
    rr = lax.broadcasted_iota(jnp.int32, (L, L), 0)
    cc = lax.broadcasted_iota(jnp.int32, (L, L), 1)

    @pl.when(i == 0)
    def _():
        c_ref[...] = jnp.zeros(c_ref.shape, F32)
        m_ref[...] = jnp.zeros(m_ref.shape, F32)
        xq_ref[:, 0:8, :] = jnp.zeros((nb, 8, nh * d), F32)
        xk_ref[:, 0:8, :] = jnp.zeros((nb, 8, nh * d), F32)
        upper = jnp.where(rr <= cc, 1.0, 0.0).astype(BF16)
        for ch in range(nb * nh):
            hi, mid, lo = _split3(_log_sigmoid(gf_ref[ch // nh, ch % nh]))
            bs_ref[ch] = _dot(hi, upper) + _dot(mid, upper) + _dot(lo, upper)

    def conv_silu(bb, u_ref, x_ref, w_ref, b_ref):
        x_ref[bb, 8:8 + L, :] = u_ref[bb]
        y = b_ref[...] + w_ref[ML_CONV - 1:ML_CONV, :] * x_ref[bb, 8:8 + L, :]
        for j in range(ML_CONV - 1):
            y = y + w_ref[j:j + 1, :] * x_ref[bb, 5 + j:5 + j + L, :]
        x_ref[bb, 0:8, :] = x_ref[bb, L:L + 8, :]
        return y * _sigmoid(y)

    q_all = [conv_silu(bb, uq_ref, xq_ref, cwq_ref, cbq_ref) for bb in range(nb)]
    k_all = [conv_silu(bb, uk_ref, xk_ref, cwk_ref, cbk_ref) * (d ** -0.5) for bb in range(nb)]
    lane = lax.broadcasted_iota(jnp.int32, (L, d), 1)
    ones_col = jnp.where(lane == 0, 1.0, 0.0)
    stage1 = []
    for ch in range(nb * nh):
        bb, hh = ch // nh, ch % nh
        cols = slice(hh * d, (hh + 1) * d)
        q = q_all[bb][:, cols].astype(BF16)
        kt = k_all[bb][:, cols].T
        v_aug = jnp.concatenate([v_ref[bb, :, cols], ones_col], axis=1).astype(BF16)
        c_aug = c_ref[ch]
        qk = _dot(q, kt.astype(BF16))
        qc = _dot(q, c_aug.astype(BF16))

        b_row = bs_ref[ch, pl.ds(i, 1), :]
        ig_row = gi_ref[bb, hh, pl.ds(i, 1), :]
        m_prev = m_ref[ch]
        g = b_row[:, L - 1:L]
        log_w = g - b_row + ig_row
        m_new = jnp.maximum(g + m_prev, jnp.max(log_w, axis=1, keepdims=True))
        w_s = jnp.exp(log_w - m_new)
        c_ref[ch] = jnp.exp(g + m_prev - m_new) * c_aug + _dot((kt * w_s).astype(BF16), v_aug)
        m_ref[ch] = m_new
        stage1.append((qk, qc, v_aug, b_row, ig_row, m_prev))

    stage2 = []
    for qk, qc, v_aug, b_row, ig_row, m_prev in stage1:
        b_col = jnp.sum(jnp.where(rr == cc, b_row, 0.0), axis=1, keepdims=True)
        log_d = jnp.where(cc <= rr, b_col + (ig_row - b_row), -jnp.inf)
        log_inter = b_col + m_prev
        m_t = jnp.maximum(log_inter, jnp.max(log_d, axis=1, keepdims=True))
        w_intra = jnp.exp(log_d - m_t) * qk
        tot = jnp.exp(log_inter - m_t) * qc + _dot(w_intra.astype(BF16), v_aug)
        stage2.append((tot, m_t))

    for ch, (tot, m_t) in enumerate(stage2):
        bb, hh = ch // nh, ch % nh
        cols = slice(hh * d, (hh + 1) * d)
        denom = jnp.maximum(jnp.abs(tot[:, d:d + 1]), jnp.exp(-m_t))
        hout = tot[:, :d] / denom
        hout = hout * lax.rsqrt(jnp.mean(hout * hout, axis=-1, keepdims=True) + NORM_EPS) * nw_ref[...]
        o_ref[bb, :, cols] = (hout * _sigmoid(op_ref[bb, :, cols])).astype(o_ref.dtype)


def _mlstm_call(cproj, gi, gf, conv_w, conv_b, nw, batch, seq):
    t = cproj.shape[0]
    d = ML_D
    nh = ML_HEADS
    L = ML_CHUNK
    nc = seq // L
    cp3 = cproj.reshape(batch, seq, cproj.shape[1])
    spec = lambda off: pl.BlockSpec((batch, L, nh * d), lambda i: (0, i, off))
    gspec = pl.BlockSpec((batch, nh, nc, L), lambda i: (0, 0, 0, 0))
    out = pl.pallas_call(
        _mlstm_kernel,
        grid=(nc,),
        in_specs=[spec(0), spec(1), spec(2), spec(3), gspec, gspec,
                  pl.BlockSpec((ML_CONV, nh * d), lambda i: (0, 0)),
                  pl.BlockSpec((ML_CONV, nh * d), lambda i: (0, 1)),
                  pl.BlockSpec((1, nh * d), lambda i: (0, 0)),
                  pl.BlockSpec((1, nh * d), lambda i: (0, 1)),
                  pl.BlockSpec((1, d), lambda i: (0, 0))],
        out_specs=pl.BlockSpec((batch, L, nh * d), lambda i: (0, i, 0)),
        out_shape=jax.ShapeDtypeStruct((batch, seq, nh * d), BF16),
        scratch_shapes=[pltpu.VMEM((batch, L + 8, nh * d), F32),
                        pltpu.VMEM((batch, L + 8, nh * d), F32),
                        pltpu.VMEM((batch * nh, d, 2 * d), F32),
                        pltpu.VMEM((batch * nh, 1, 1), F32),
                        pltpu.VMEM((batch * nh, nc, L), F32)],
        compiler_params=_cparams(("arbitrary",)),
        name="mlstm",
    )(cp3, cp3, cp3, cp3, gi, gf, conv_w, conv_w, conv_b, conv_b, nw)
    return out.reshape(t, nh * d)


def _top2_combine(logits):
    lane = lax.broadcasted_iota(jnp.int32, logits.shape, 1)
    lg = jnp.where(lane < N_EXPERTS, logits, -jnp.inf)
    ex = jnp.exp(lg - jnp.max(lg, axis=1, keepdims=True))
    probs = ex / jnp.sum(ex, axis=1, keepdims=True)
    p1 = jnp.max(probs, axis=1, keepdims=True)
    i1 = jnp.min(jnp.where(probs == p1, lane, LANES), axis=1, keepdims=True)
    rest = jnp.where(lane == i1, -1.0, probs)
    p2 = jnp.max(rest, axis=1, keepdims=True)
    i2 = jnp.min(jnp.where(rest == p2, lane, LANES), axis=1, keepdims=True)
    comb = jnp.where(lane == i1, p1, 0.0) + jnp.where(lane == i2, p2, 0.0)
    return comb / (p1 + p2)


def _merge_kernel(*refs, route):
    if route:
        (ya_ref, yb_ref, yc_ref, gp_ref, wb_ref, wo_ref, x_ref, mod_ref, nw_ref,
         rw_ref, rb_ref, xo_ref, h_ref, cmb_ref) = refs
    else:
        (ya_ref, yb_ref, yc_ref, gp_ref, wb_ref, wo_ref, x_ref, mod_ref, nw_ref,
         xo_ref, h_ref) = refs
    d = x_ref.shape[1]
    merged = None
    for n, y_ref in enumerate((ya_ref, yb_ref, yc_ref)):
        gate = _sigmoid(gp_ref[:, n * d:(n + 1) * d].astype(F32))
        term = gate * _dot(y_ref[...], wb_ref[n])
        merged = term if merged is None else merged + term
    m = mod_ref[0]
    xn = x_ref[...] + m[2:3] * _dot(merged.astype(BF16), wo_ref[...])
    xo_ref[...] = xn
    h2 = _norm_mod(xn, nw_ref[...], m[4:5], m[3:4])
    h_ref[...] = h2.astype(BF16)
    if route:
        h_hi, h_mid, _ = _split3(h2)
        r_hi, r_mid, _ = _split3(rw_ref[...])
        logits = _dot(h_hi, r_hi) + _dot(h_mid, r_hi) + _dot(h_hi, r_mid) + rb_ref[...]
        cmb_ref[...] = _top2_combine(logits)


def _merge_call(ya, yb, yc, gp, wb, wo, x, mod, nw, router, seq, tm):
    t, d = x.shape
    bw = ya.shape[1]
    per_b = seq // tm
    route = router is not None
    tok = lambda w: pl.BlockSpec((tm, w), lambda i: (i, 0))
    const2 = lambda s: pl.BlockSpec(s, lambda i: (0, 0))
    in_specs = [tok(bw), tok(bw), tok(bw), tok(3 * d),
                pl.BlockSpec((3, bw, d), lambda i: (0, 0, 0)), const2((d, d)), tok(d),
                pl.BlockSpec((1, 6, d), lambda i: (i // per_b, 0, 0)), const2((1, d))]
    out_specs = [tok(d), tok(d)]
    out_shape = [jax.ShapeDtypeStruct((t, d), F32), jax.ShapeDtypeStruct((t, d), BF16)]
    args = [ya, yb, yc, gp, wb, wo, x, mod, nw]
    if route:
        in_specs += [const2((d, LANES)), const2((1, LANES))]
        out_specs.append(tok(LANES))
        out_shape.append(jax.ShapeDtypeStruct((t, LANES), F32))
        args += list(router)
    return pl.pallas_call(
        functools.partial(_merge_kernel, route=route),
        grid=(t // tm,),
        in_specs=in_specs, out_specs=out_specs, out_shape=out_shape,
        compiler_params=_cparams(("arbitrary",)),
        name="merge_out",
    )(*args)


def _finish(acc, x_ref, mod_ref, xo_ref, nxt):
    xn = x_ref[...] + mod_ref[0][5:6] * acc
    xo_ref[...] = xn
    if nxt is not None:
        nw_ref, modn_ref, hn_ref = nxt
        mn = modn_ref[0]
        hn_ref[...] = _norm_mod(xn, nw_ref[...], mn[1:2], mn[0:1]).astype(BF16)


def _ffn_kernel(*refs, has_next):
    if has_next:
        h_ref, w1_ref, w3_ref, w2_ref, x_ref, mod_ref, nw_ref, modn_ref, xo_ref, hn_ref, acc_ref = refs
        nxt = (nw_ref, modn_ref, hn_ref)
    else:
        h_ref, w1_ref, w3_ref, w2_ref, x_ref, mod_ref, xo_ref, acc_ref = refs
        nxt = None
    f = pl.program_id(1)

    @pl.when(f == 0)
    def _():
        acc_ref[...] = jnp.zeros(acc_ref.shape, F32)

    h = h_ref[...]
    a = _dot(h, w1_ref[...])
    act = a * _sigmoid(a) * _dot(h, w3_ref[...])
    acc_ref[...] += _dot(act.astype(BF16), w2_ref[...])

    @pl.when(f == pl.num_programs(1) - 1)
    def _():
        _finish(acc_ref[...], x_ref, mod_ref, xo_ref, nxt)


def _ffn_call(h, w1, w3, w2, x, mod, nxt, seq, tm, tf):
    t, d = x.shape
    ff = w1.shape[1]
    per_b = seq // tm
    tok = lambda: pl.BlockSpec((tm, d), lambda i, f: (i, 0))
    modspec = lambda: pl.BlockSpec((1, 6, d), lambda i, f: (i // per_b, 0, 0))
    in_specs = [tok(), pl.BlockSpec((d, tf), lambda i, f: (0, f)),
                pl.BlockSpec((d, tf), lambda i, f: (0, f)),
                pl.BlockSpec((tf, d), lambda i, f: (f, 0)), tok(), modspec()]
    out_specs = [tok()]
    out_shape = [jax.ShapeDtypeStruct((t, d), F32)]
    args = [h, w1, w3, w2, x, mod]
    if nxt is not None:
        in_specs += [pl.BlockSpec((1, d), lambda i, f: (0, 0)), modspec()]
        out_specs.append(tok())
        out_shape.append(jax.ShapeDtypeStruct((t, d), BF16))
        args += list(nxt)
    return pl.pallas_call(
        functools.partial(_ffn_kernel, has_next=nxt is not None),
        grid=(t // tm, ff // tf),
        in_specs=in_specs, out_specs=out_specs, out_shape=out_shape,
        scratch_shapes=[pltpu.VMEM((tm, d), F32)],
        compiler_params=_cparams(("arbitrary", "arbitrary")),
        name="ffn_swiglu",
    )(*args)


def _moe_kernel(h_ref, cmb_ref, w13_ref, w2_ref, y_ref,
                acc_ref, xg_ref, ya_ref, rk_ref, rkt_ref, cnt_ref):
    e = pl.program_id(1)
    f = pl.program_id(2)
    tm = h_ref.shape[0]

    @pl.when((e == 0) & (f == 0))
    def _():
        r = lax.broadcasted_iota(jnp.int32, (tm, tm), 0)
        c = lax.broadcasted_iota(jnp.int32, (tm, tm), 1)
        before = jnp.where(c < r, 1.0, 0.0).astype(BF16)
        sel = cmb_ref[...] > 0.0
        rank = _dot(before, jnp.where(sel, 1.0, 0.0).astype(BF16))
        rk = jnp.where(sel, rank, -1.0)
        rk_ref[...] = rk
        rkt_ref[...] = rk.T
        cnt_ref[...] = jnp.sum(jnp.where(sel, 1.0, 0.0), axis=0, keepdims=True)
        acc_ref[...] = jnp.zeros(acc_ref.shape, F32)

    lane1 = lax.broadcasted_iota(jnp.int32, (1, LANES), 1)
    n_e = jnp.sum(jnp.where(lane1 == e, cnt_ref[...], 0.0)).astype(jnp.int32)

    def loop(n_blocks, body):
        if isinstance(n_blocks, int):
            for sb in range(n_blocks):
                body(sb, 0)
        else:
            lax.fori_loop(0, n_blocks, body, 0)

    def run(cap, n_blocks):
        cap_pad = -(-cap // LANES) * LANES

        def row0(sb):
            return sb * cap if isinstance(sb, int) else pl.multiple_of(sb * cap, cap)

        @pl.when(f == 0)
        def _():
            rank_row = rkt_ref[pl.ds(e, 1), :]

            def gather(sb, carry):
                r0 = row0(sb)
                slot = (r0 + lax.broadcasted_iota(jnp.int32, (cap, tm), 0)).astype(F32)
                onehot = jnp.where(rank_row == slot, 1.0, 0.0).astype(BF16)
                xg_ref[pl.ds(r0, cap), :] = _dot(onehot, h_ref[...]).astype(BF16)
                return carry

            loop(n_blocks, gather)

        def expert(sb, carry):
            r0 = row0(sb)
            xs = xg_ref[pl.ds(r0, cap), :]
            ab = _dot(xs, w13_ref[0])
            tf = w2_ref.shape[1]
            a = ab[:, :tf]
            act = a * _sigmoid(a) * ab[:, tf:]
            part = _dot(act.astype(BF16), w2_ref[0])

            @pl.when(f == 0)
            def _():
                ya_ref[pl.ds(r0, cap), :] = part

            @pl.when(f > 0)
            def _():
                ya_ref[pl.ds(r0, cap), :] += part

            return carry

        loop(n_blocks, expert)

        @pl.when(f == pl.num_programs(2) - 1)
        def _():
            lane = lax.broadcasted_iota(jnp.int32, (tm, LANES), 1)
            rank_col = jnp.sum(jnp.where(lane == e, rk_ref[...], 0.0), axis=1, keepdims=True)
            w_col = jnp.sum(jnp.where(lane == e, cmb_ref[...], 0.0), axis=1, keepdims=True)
            col = lax.broadcasted_iota(jnp.int32, (tm, cap_pad), 1)

            def scatter(sb, carry):
                r0 = row0(sb)
                ys = ya_ref[pl.ds(r0, cap), :].astype(BF16)
                if cap_pad > cap:
                    ys = jnp.concatenate([ys, jnp.zeros((cap_pad - cap, ys.shape[1]), BF16)], axis=0)
                hit = (rank_col == (r0 + col).astype(F32)) & (col < cap)
                acc_ref[...] += w_col * _dot(jnp.where(hit, 1.0, 0.0).astype(BF16), ys)
                return carry

            loop(n_blocks, scatter)

    for idx, cap in enumerate(MOE_CAPS):
        lower = MOE_CAPS[idx - 1] if idx else 0
        if idx == len(MOE_CAPS) - 1:
            pl.when(n_e > lower)(functools.partial(run, cap, (n_e + cap - 1) // cap))
        else:
            pl.when((n_e > lower) & (n_e <= cap))(functools.partial(run, cap, 1))

    @pl.when((e == pl.num_programs(1) - 1) & (f == pl.num_programs(2) - 1))
    def _():
        y_ref[...] = acc_ref[...].astype(y_ref.dtype)


def _moe_pack_w13(w1, w3):
    ff = w1.shape[-1]
    tf = _pick(ff, (MOE_TF,))
    parts = []
    for f in range(ff // tf):
        parts += [w1[:, :, f * tf:(f + 1) * tf].astype(BF16), w3[:, :, f * tf:(f + 1) * tf].astype(BF16)]
    return jnp.concatenate(parts, axis=-1)


def _moe_call(h, cmb, w13, w2, tm):
    t, d = h.shape
    ne, ff, _ = w2.shape
    tf = _pick(ff, (MOE_TF,))
    rows = -(-tm // MOE_CAPS[-1]) * MOE_CAPS[-1]
    return pl.pallas_call(
        _moe_kernel,
        grid=(t // tm, ne, ff // tf),
        in_specs=[pl.BlockSpec((tm, d), lambda i, e, f: (i, 0)),
                  pl.BlockSpec((tm, LANES), lambda i, e, f: (i, 0)),
                  pl.BlockSpec((1, d, 2 * tf), lambda i, e, f: (e, 0, f)),
                  pl.BlockSpec((1, tf, d), lambda i, e, f: (e, f, 0))],
        out_specs=pl.BlockSpec((tm, d), lambda i, e, f: (i, 0)),
        out_shape=jax.ShapeDtypeStruct((t, d), BF16),
        scratch_shapes=[pltpu.VMEM((tm, d), F32),
                        pltpu.VMEM((rows, d), BF16),
                        pltpu.VMEM((rows, d), F32),
                        pltpu.VMEM((tm, LANES), F32),
                        pltpu.VMEM((LANES, tm), F32),
                        pltpu.VMEM((1, LANES), F32)],
        compiler_params=_cparams(("arbitrary", "arbitrary", "arbitrary")),
        name="moe_top2",
    )(h, cmb, w13, w2)


def _resid_kernel(*refs, has_next):
    if has_next:
        y_ref, x_ref, mod_ref, nw_ref, modn_ref, xo_ref, hn_ref = refs
        nxt = (nw_ref, modn_ref, hn_ref)
    else:
        y_ref, x_ref, mod_ref, xo_ref = refs
        nxt = None
    _finish(y_ref[...].astype(F32), x_ref, mod_ref, xo_ref, nxt)


def _resid_call(y, x, mod, nxt, seq, tm):
    t, d = x.shape
    per_b = seq // tm
    tok = lambda: pl.BlockSpec((tm, d), lambda i: (i, 0))
    modspec = lambda: pl.BlockSpec((1, 6, d), lambda i: (i // per_b, 0, 0))
    in_specs = [tok(), tok(), modspec()]
    out_specs = [tok()]
    out_shape = [jax.ShapeDtypeStruct((t, d), F32)]
    args = [y, x, mod]
    if nxt is not None:
        in_specs += [pl.BlockSpec((1, d), lambda i: (0, 0)), modspec()]
        out_specs.append(tok())
        out_shape.append(jax.ShapeDtypeStruct((t, d), BF16))
        args += list(nxt)
    return pl.pallas_call(
        functools.partial(_resid_kernel, has_next=nxt is not None),
        grid=(t // tm,),
        in_specs=in_specs, out_specs=out_specs, out_shape=out_shape,
        compiler_params=_cparams(("arbitrary",)),
        name="moe_residual",
    )(*args)


def _attn_col_perm():
    hq = DA_HEADS * DA_DK
    idx = []
    for base in (0, 2 * hq):
        for h in range(DA_HEADS):
            idx += list(range(base + h * DA_DK, base + (h + 1) * DA_DK))
            idx += list(range(base + hq + h * DA_DK, base + hq + (h + 1) * DA_DK))
    idx += list(range(4 * hq, 4 * hq + DA_HEADS * DA_DV))
    return np.asarray(idx, np.int32)


def _pick(n, pref):
    for c in pref:
        if n % c == 0:
            return c
    return n


def kernel(x, c, w_mod, b_mod, norm1_w, norm2_w, w_in, c_conv_w, c_conv_b, a_qnorm_w, a_knorm_w,
           a_lambda_q1, a_lambda_k1, a_lambda_q2, a_lambda_k2, a_subln_w, b_lb_logits, b_gnorm_w,
           c_igate_b, c_fgate_b, c_norm_w, w_branch, w_out, ffn_w1, ffn_w3, ffn_w2,
           moe_router_w, moe_router_b, moe_w1, moe_w3, moe_w2):
    batch, seq, d = x.shape
    depth = w_in.shape[0]
    t = batch * seq
    n_a = 4 * DA_HEADS * DA_DK + DA_HEADS * DA_DV
    n_b = 4 * HG_HEADS * HG_D
    n_c = 4 * ML_HEADS * ML_D + 2 * ML_HEADS
    n_c_pad = -(-n_c // LANES) * LANES
    tm = _pick(seq, (1024, 512, 256))
    tq = _pick(seq, (ATT_TQ,))
    ts_h = _pick(seq, (512, 256, 128))

    c8 = jnp.zeros((8, d), F32).at[:batch].set(c)
    mod = _mod_call(c8, w_mod, b_mod.reshape(depth, 1, 6 * d))[:, :batch].reshape(depth, batch, 6, d)

    lb_all = jnp.cumsum(jax.nn.softmax(b_lb_logits.astype(F32), axis=0), axis=0)
    lb_all = lb_all - lb_all[:1]
    slopes = jnp.asarray(_attn_slopes(), F32)

    xf = x.reshape(t, d)
    h = _normmod_call(xf, norm1_w[0].reshape(1, d), mod[0], seq, tm)
    perm_a = _attn_col_perm()
    w_a_all = w_in[:, :, :n_a][:, :, perm_a].astype(BF16)
    w_b_all = w_in[:, :, n_a:n_a + n_b].astype(BF16)
    w_c_all = jnp.pad(w_in[:, :, n_a + n_b:n_a + n_b + n_c].astype(BF16),
                      ((0, 0), (0, 0), (0, n_c_pad - n_c)))
    w_g_all = w_in[:, :, n_a + n_b + n_c:].astype(BF16)
    for l in range(depth):
        lam_init = 0.8 - 0.6 * math.exp(-0.3 * l)
        w_a, w_b, w_c, w_g = w_a_all[l], w_b_all[l], w_c_all[l], w_g_all[l]
        bias_c = jnp.zeros((1, n_c_pad), F32)
        bias_c = bias_c.at[0, n_c - 2 * ML_HEADS:n_c - ML_HEADS].set(c_igate_b[l])
        bias_c = bias_c.at[0, n_c - ML_HEADS:n_c].set(c_fgate_b[l])
        pb = _matmul(h, w_b, jnp.zeros((1, n_b), F32), F32, tm, n_b // 2)
        pc, pc_gates = _matmul_tail(h, w_c, bias_c, tm, n_c_pad - (n_c - 2 * ML_HEADS))
        pg = _matmul(h, w_g, jnp.zeros((1, 3 * d), F32), BF16, tm, 3 * d // 2)

        qw = (jnp.tile(a_qnorm_w[l], 2 * DA_HEADS) * (DA_DK ** -0.5 * LOG2E)).reshape(1, -1)
        kw = jnp.tile(a_knorm_w[l], 2 * DA_HEADS).reshape(1, -1)
        kn, qt, vt = _attn_prep_call(h, w_a, qw, kw, batch, seq, _pick(seq, (512, 256)))
        lam = (jnp.exp(jnp.sum(a_lambda_q1[l] * a_lambda_k1[l]))
               - jnp.exp(jnp.sum(a_lambda_q2[l] * a_lambda_k2[l])) + lam_init)
        scal = jnp.concatenate([slopes, jnp.stack([lam, jnp.asarray(1.0 - lam_init, F32)])]).astype(F32)
        y_a = _attn_call(scal, qt, kn, vt, a_subln_w[l].reshape(1, -1), batch, seq, tq)

        lb = lb_all[l].reshape(HG_HEADS, 1, HG_D)
        lbs = jnp.concatenate([jnp.log(lb), jnp.log1p(-lb)], axis=1)
        y_b = _hgrn_call(pb, lbs, b_gnorm_w[l].reshape(1, -1), batch, seq, ts_h)

        gates = pc_gates[:, :2 * ML_HEADS].reshape(batch, seq, 2, ML_HEADS)
        gates = gates.transpose(2, 0, 3, 1).reshape(2, batch, ML_HEADS, seq // ML_CHUNK, ML_CHUNK)
        y_c = _mlstm_call(pc, gates[0], gates[1], c_conv_w[l], c_conv_b[l].reshape(1, -1),
                          c_norm_w[l].reshape(1, -1), batch, seq)

        dense = l % 2 == 0
        router = None
        if not dense:
            rw = jnp.pad(moe_router_w[l // 2], ((0, 0), (0, LANES - N_EXPERTS)))
            rb = jnp.pad(moe_router_b[l // 2], (0, LANES - N_EXPERTS)).reshape(1, LANES)
            router = (rw, rb)
        outs = _merge_call(y_a, y_b, y_c, pg, w_branch[l].astype(BF16), w_out[l].astype(BF16), xf,
                           mod[l], norm2_w[l].reshape(1, d), router, seq, 512)
        xf, h2 = outs[0], outs[1]
        nxt = None if l == depth - 1 else (norm1_w[l + 1].reshape(1, d), mod[l + 1])
        if dense:
            res = _ffn_call(h2, ffn_w1[l // 2].astype(BF16), ffn_w3[l // 2].astype(BF16),
                            ffn_w2[l // 2].astype(BF16), xf, mod[l], nxt, seq, tm, FFN_TF)
        else:
            y = _moe_call(h2, outs[2], _moe_pack_w13(moe_w1[l // 2], moe_w3[l // 2]),
                          moe_w2[l // 2].astype(BF16), tm)
            res = _resid_call(y, xf, mod[l], nxt, seq, tm)
        xf = res[0]
        if nxt is not None:
            h = res[1]
    return xf.reshape(batch, seq, d)
```

```python
import functools
import math

import numpy as np
import jax
import jax.numpy as jnp
from jax import lax
from jax.experimental import pallas as pl
from jax.experimental.pallas import tpu as pltpu

F32 = jnp.float32
BF16 = jnp.bfloat16
NORM_EPS = 1e-6

CHUNK = 64
DA_HEADS, DA_DK, DA_DV = 4, 64, 128
HG_HEADS, HG_D = 4, 128
ML_HEADS, ML_D, ML_CONV = 4, 128, 4
N_EXPERTS = 8

LANES = 128
VMEM_LIMIT = 56 * 1024 * 1024

ATT_TQ = 256
ATT_HEADS_PER_STEP = 2
ATT_PAD_ROWS = 16
LOG2E = math.log2(math.e)
ATT_BOUND_MARGIN = 1.01
ATT_BOUND_EPS = 1e-3
ATT_BOUND_MAX = 40.0
ATT_ZERO_EXP = 160.0
HG_CHUNK = 128
HG_SUB = 32
HG_SAFE_DECAY = 80.0
ML_CHUNK = 256
MOE_CAPS = (256, 288, 320, 352, 384)
MOE_TF = 1408
FFN_TF = 256


def _cparams(sem, vmem=VMEM_LIMIT):
    return pltpu.CompilerParams(dimension_semantics=sem, vmem_limit_bytes=vmem)


def _dot(a, b):
    return jnp.dot(a, b, preferred_element_type=F32)


def _dot_nt(a, b):
    return lax.dot_general(a, b, (((1,), (1,)), ((), ())), preferred_element_type=F32)


def _split3(x):
    hi = x.astype(BF16)
    r1 = x - hi.astype(F32)
    mid = r1.astype(BF16)
    lo = (r1 - mid.astype(F32)).astype(BF16)
    return hi, mid, lo


def _norm_mod(x, w, scale, shift):
    y = x * lax.rsqrt(jnp.mean(x * x, axis=-1, keepdims=True) + NORM_EPS) * w
    return y * (1.0 + scale) + shift


def _sigmoid(x):
    return 0.5 * jnp.tanh(0.5 * x) + 0.5


def _log1p_exp_neg(z):
    return jnp.log(1.0 + jnp.exp(-z))


def _log_sigmoid(x):
    return jnp.minimum(x, 0.0) - _log1p_exp_neg(jnp.abs(x))


def _mod_kernel(c_ref, w_ref, b_ref, o_ref):
    cnd = c_ref[...]
    cnd = cnd * _sigmoid(cnd)
    o_ref[0] = _dot(cnd.astype(BF16), w_ref[0].astype(BF16)) + b_ref[0]


def _mod_call(c8, w_mod, b_mod):
    depth, d, n = w_mod.shape
    tn = 1536 if n % 1536 == 0 else n
    return pl.pallas_call(
        _mod_kernel,
        grid=(depth, n // tn),
        in_specs=[pl.BlockSpec((8, d), lambda l, j: (0, 0)),
                  pl.BlockSpec((1, d, tn), lambda l, j: (l, 0, j)),
                  pl.BlockSpec((1, 1, tn), lambda l, j: (l, 0, j))],
        out_specs=pl.BlockSpec((1, 8, tn), lambda l, j: (l, 0, j)),
        out_shape=jax.ShapeDtypeStruct((depth, 8, n), F32),
        compiler_params=_cparams(("arbitrary", "arbitrary")),
        name="adaln_mod",
    )(c8, w_mod, b_mod)


def _normmod_kernel(x_ref, nw_ref, mod_ref, h_ref):
    m = mod_ref[0]
    h_ref[...] = _norm_mod(x_ref[...], nw_ref[...], m[1:2], m[0:1]).astype(BF16)


def _normmod_call(x, nw, mod, seq, tm):
    t, d = x.shape
    per_b = seq // tm
    return pl.pallas_call(
        _normmod_kernel,
        grid=(t // tm,),
        in_specs=[pl.BlockSpec((tm, d), lambda i: (i, 0)),
                  pl.BlockSpec((1, d), lambda i: (0, 0)),
                  pl.BlockSpec((1, 6, d), lambda i: (i // per_b, 0, 0))],
        out_specs=pl.BlockSpec((tm, d), lambda i: (i, 0)),
        out_shape=jax.ShapeDtypeStruct((t, d), BF16),
        compiler_params=_cparams(("arbitrary",)),
        name="prenorm_mod",
    )(x, nw, mod)


def _mm_kernel(x_ref, w_ref, b_ref, o_ref):
    o_ref[...] = (_dot(x_ref[...], w_ref[...]) + b_ref[...]).astype(o_ref.dtype)


def _matmul(x, w, bias, out_dtype, tm, tn):
    t, k = x.shape
    n = w.shape[1]
    return pl.pallas_call(
        _mm_kernel,
        grid=(n // tn, t // tm),
        in_specs=[pl.BlockSpec((tm, k), lambda j, i: (i, 0)),
                  pl.BlockSpec((k, tn), lambda j, i: (0, j)),
                  pl.BlockSpec((1, tn), lambda j, i: (0, j))],
        out_specs=pl.BlockSpec((tm, tn), lambda j, i: (i, j)),
        out_shape=jax.ShapeDtypeStruct((t, n), out_dtype),
        compiler_params=_cparams(("arbitrary", "arbitrary")),
        name="in_proj",
    )(x, w, bias)


def _mm_tail_kernel(x_ref, w_ref, b_ref, o_ref, tail_ref):
    res = _dot(x_ref[...], w_ref[...]) + b_ref[...]
    main = o_ref.shape[1]
    o_ref[...] = res[:, :main]
    tail_ref[...] = res[:, main:]


def _matmul_tail(x, w, bias, tm, tail):
    t, k = x.shape
    n = w.shape[1]
    return pl.pallas_call(
        _mm_tail_kernel,
        grid=(t // tm,),
        in_specs=[pl.BlockSpec((tm, k), lambda i: (i, 0)),
                  pl.BlockSpec((k, n), lambda i: (0, 0)),
                  pl.BlockSpec((1, n), lambda i: (0, 0))],
        out_specs=[pl.BlockSpec((tm, n - tail), lambda i: (i, 0)),
                   pl.BlockSpec((tm, tail), lambda i: (i, 0))],
        out_shape=[jax.ShapeDtypeStruct((t, n - tail), F32),
                   jax.ShapeDtypeStruct((t, tail), F32)],
        compiler_params=_cparams(("arbitrary",)),
        name="in_proj_c",
    )(x, w, bias)


def _attn_prep_kernel(x_ref, w_ref, qw_ref, kw_ref, kn_ref, qt_ref, vt_ref):
    hw = DA_HEADS * 2 * DA_DK
    a = _dot(x_ref[...], w_ref[...])
    r = lax.broadcasted_iota(jnp.int32, (hw, hw), 0) // DA_DK
    c = lax.broadcasted_iota(jnp.int32, (hw, hw), 1) // DA_DK
    group = jnp.where(r == c, 1.0, 0.0).astype(BF16)

    def qk_norm(z, w):
        ms = _dot((z * z).astype(BF16), group) * (1.0 / DA_DK)
        return z * lax.rsqrt(ms + NORM_EPS) * w

    qn = qk_norm(a[:, :hw], qw_ref[...])
    kn = qk_norm(a[:, hw:2 * hw], kw_ref[...])
    kn_ref[...] = kn.astype(BF16)
    qt_ref[0] = qn.T.astype(BF16)
    vt_ref[0] = a[:, 2 * hw:].T.astype(BF16)


def _attn_prep_call(x, w, qw, kw, batch, seq, ts):
    t, d = x.shape
    hw = DA_HEADS * 2 * DA_DK
    hv = DA_HEADS * DA_DV
    per_b = seq // ts
    return pl.pallas_call(
        _attn_prep_kernel,
        grid=(t // ts,),
        in_specs=[pl.BlockSpec((ts, d), lambda i: (i, 0)),
                  pl.BlockSpec((d, 2 * hw + hv), lambda i: (0, 0)),
                  pl.BlockSpec((1, hw), lambda i: (0, 0)),
                  pl.BlockSpec((1, hw), lambda i: (0, 0))],
        out_specs=[pl.BlockSpec((ts, hw), lambda i: (i, 0)),
                   pl.BlockSpec((1, hw, ts), lambda i: (i // per_b, 0, i % per_b)),
                   pl.BlockSpec((1, hv, ts), lambda i: (i // per_b, 0, i % per_b))],
        out_shape=[jax.ShapeDtypeStruct((t, hw), BF16),
                   jax.ShapeDtypeStruct((batch, hw, seq), BF16),
                   jax.ShapeDtypeStruct((batch, hv, seq), BF16)],
        compiler_params=_cparams(("arbitrary",)),
        name="attn_prep",
    )(x, w, qw, kw)


def _attn_slopes():
    return [LOG2E * 2.0 ** (-8.0 * (h + 1) / DA_HEADS) for h in range(DA_HEADS)]


def _attn_window(head, tq):
    return int((ATT_ZERO_EXP / _attn_slopes()[head] - 1.0) // tq) + 1


def _attn_kernel(sc_ref, qt_ref, k_ref, vt_ref, sw_ref, o_ref,
                 qz_ref, m_ref, acc_ref, vs_ref, kmax_ref, s_ref, s2_ref, *, tq, seq):
    hp = ATT_HEADS_PER_STEP
    dv = DA_DV
    g = pl.program_id(1)
    i = pl.program_id(2)
    lam = sc_ref[DA_HEADS]
    out_scale = sc_ref[DA_HEADS + 1]
    slopes = [sc_ref[hp * g + hh] for hh in range(hp)]
    row16 = lax.broadcasted_iota(jnp.int32, (ATT_PAD_ROWS, tq), 0)
    ones_rows = jnp.where(row16 == 0, 1.0, 0.0).astype(BF16)

    half = lax.broadcasted_iota(jnp.int32, (1, 2 * DA_DK), 1) < DA_DK

    @pl.when(i == 0)
    def _():
        pos = lax.broadcasted_iota(jnp.int32, (1, seq), 1) & (tq - 1)
        rel = (pos - (tq - 1)).astype(F32)
        sub16 = lax.broadcasted_iota(jnp.int32, (ATT_PAD_ROWS, seq), 0)
        gr = lax.broadcasted_iota(jnp.int32, (2 * DA_DK, 2 * DA_DK), 0) // DA_DK
        gc = lax.broadcasted_iota(jnp.int32, (2 * DA_DK, 2 * DA_DK), 1) // DA_DK
        group = jnp.where(gr == gc, 1.0, 0.0).astype(BF16)
        for hh in range(hp):
            w = jnp.exp2(slopes[hh] * rel)
            vs_ref[hh, 0:dv, :] = (vt_ref[0, hh * dv:(hh + 1) * dv, :].astype(F32) * w).astype(BF16)
            vs_ref[hh, dv:dv + ATT_PAD_ROWS, :] = jnp.where(sub16 == 0, w, 0.0).astype(BF16)

            def knorm(n, best):
                kc = k_ref[pl.ds(pl.multiple_of(n * tq, tq), tq),
                           hh * 2 * DA_DK:(hh + 1) * 2 * DA_DK].astype(F32)
                return jnp.maximum(best, jnp.max(_dot((kc * kc).astype(BF16), group), axis=0, keepdims=True))

            k2 = lax.fori_loop(0, seq // tq, knorm, jnp.zeros((1, 2 * DA_DK), F32))
            kmax_ref[2 * hh] = jnp.max(jnp.where(half, k2, 0.0), axis=1, keepdims=True)
            kmax_ref[2 * hh + 1] = jnp.max(jnp.where(half, 0.0, k2), axis=1, keepdims=True)

    row = lax.broadcasted_iota(jnp.int32, (2 * DA_DK, tq), 0)
    qq_row = lax.broadcasted_iota(jnp.int32, (1, tq), 1).astype(F32)
    bound_max = jnp.zeros((), F32)
    firsts = []

    def scores(hh, blk):
        start = pl.multiple_of(blk * tq, tq)
        return _dot(k_ref[pl.ds(start, tq), hh * 2 * DA_DK:(hh + 1) * 2 * DA_DK], qz_ref[hh])

    for hh in range(hp):
        qt = qt_ref[0, hh * 2 * DA_DK:(hh + 1) * 2 * DA_DK, :]
        zero = jnp.zeros_like(qt)
        q1 = jnp.where(row < DA_DK, qt, zero)
        q2 = jnp.where(row >= DA_DK, qt, zero)
        qz_ref[hh] = jnp.concatenate([q1, q2], axis=1)
        window = jnp.int32(_attn_window(hh, tq))
        for grp in range(1, DA_HEADS // hp):
            window = jnp.where(g == grp, _attn_window(grp * hp + hh, tq), window)
        firsts.append((jnp.maximum(i - window, 0) // 2) * 2)
        s_ref[hh] = scores(hh, firsts[hh])
        s2_ref[hh] = scores(hh, jnp.minimum(firsts[hh] + 1, i))
        qn = jnp.concatenate(
            [jnp.sum(jnp.square(q1.astype(F32)), axis=0, keepdims=True) * kmax_ref[2 * hh],
             jnp.sum(jnp.square(q2.astype(F32)), axis=0, keepdims=True) * kmax_ref[2 * hh + 1]], axis=1)
        bound = jnp.sqrt(qn) * ATT_BOUND_MARGIN + ATT_BOUND_EPS
        bound_max = jnp.maximum(bound_max, jnp.max(bound))
        m_ref[hh] = bound + slopes[hh] * jnp.concatenate([qq_row, qq_row], axis=1)
    acc_ref[...] = jnp.zeros(acc_ref.shape, F32)

    kk = lax.broadcasted_iota(jnp.int32, (tq, 2 * tq), 0)
    cc = lax.broadcasted_iota(jnp.int32, (tq, 2 * tq), 1)
    qq = jnp.where(cc >= tq, cc - tq, cc)
    dist = (qq - jnp.abs(qq - kk)).astype(F32)
    visible = (kk // CHUNK) <= (qq // CHUNK)

    def diag_values(hh):
        start = pl.multiple_of(i * tq, tq)
        return jnp.concatenate([vt_ref[0, hh * dv:(hh + 1) * dv, pl.ds(start, tq)], ones_rows], axis=0)

    @pl.when(bound_max <= ATT_BOUND_MAX)
    def _():
        def probs(hh, blk, s):
            last_key = ((blk + 1 - i) * tq - 1).astype(F32)
            return jnp.exp2(s - (m_ref[hh] - slopes[hh] * last_key)).astype(BF16)

        def body(heads, jj, carry):
            blk = 2 * jj
            start = pl.multiple_of(blk * tq, 2 * tq)
            for hh in heads:
                p = jnp.concatenate([probs(hh, blk, s_ref[hh]), probs(hh, blk + 1, s2_ref[hh])], axis=0)
                s_ref[hh] = scores(hh, jnp.minimum(blk + 2, i))
                s2_ref[hh] = scores(hh, jnp.minimum(blk + 3, i))
                acc_ref[hh] += _dot(vs_ref[hh, :, pl.ds(start, 2 * tq)], p)
            return carry

        for hh in range(hp - 1, 0, -1):
            lax.fori_loop(firsts[hh] // 2, firsts[hh - 1] // 2,
                          functools.partial(body, tuple(range(hh, hp))), 0)
        lax.fori_loop(firsts[0] // 2, i // 2, functools.partial(body, tuple(range(hp))), 0)

        @pl.when(i % 2 == 1)
        def _():
            start = pl.multiple_of((i - 1) * tq, tq)
            for hh in range(hp):
                acc_ref[hh] += _dot(vs_ref[hh, :, pl.ds(start, tq)], probs(hh, i - 1, s_ref[hh]))
                s_ref[hh] = s2_ref[hh]

        for hh in range(hp):
            s = jnp.where(visible, s_ref[hh] + slopes[hh] * dist, -jnp.inf)
            acc_ref[hh] += _dot(diag_values(hh), jnp.exp2(s - m_ref[hh]).astype(BF16))

    @pl.when(bound_max > ATT_BOUND_MAX)
    def _():
        m_ref[...] = jnp.full(m_ref.shape, -jnp.inf, F32)

        def update(hh, s, bound_shift, v_aug):
            m_old = m_ref[hh]
            m_new = jnp.maximum(m_old, jnp.max(s, axis=0, keepdims=True) + bound_shift)
            p = jnp.exp2(s - (m_new - bound_shift))
            acc_ref[hh] = jnp.exp2(m_old - m_new) * acc_ref[hh] + _dot(v_aug, p.astype(BF16))
            m_ref[hh] = m_new

        def body(j, carry):
            start = pl.multiple_of(j * tq, tq)
            last_key = ((j + 1 - i) * tq - 1).astype(F32)
            for hh in range(hp):
                update(hh, scores(hh, j), slopes[hh] * last_key, vs_ref[hh, :, pl.ds(start, tq)])
            return carry

        lax.fori_loop(0, i, body, 0)
        for hh in range(hp):
            s = jnp.where(visible, scores(hh, i) + slopes[hh] * dist, -jnp.inf)
            update(hh, s, jnp.zeros((), F32), diag_values(hh))

    outs = []
    for hh in range(hp):
        acc = acc_ref[hh]
        o2 = acc[:dv, :] * (1.0 / acc[dv:dv + 1, :])
        o = (o2[:, :tq] - lam * o2[:, tq:]).T
        o = o * lax.rsqrt(jnp.mean(o * o, axis=-1, keepdims=True) + NORM_EPS) * sw_ref[...]
        outs.append(o * out_scale)
    o_ref[...] = jnp.concatenate(outs, axis=1).astype(o_ref.dtype)


def _attn_call(scal, qt, kn, vt, sw, batch, seq, tq):
    t = kn.shape[0]
    nq = seq // tq
    hp = ATT_HEADS_PER_STEP
    return pl.pallas_call(
        functools.partial(_attn_kernel, tq=tq, seq=seq),
        grid=(batch, DA_HEADS // hp, nq),
        in_specs=[pl.BlockSpec(memory_space=pltpu.SMEM),
                  pl.BlockSpec((1, hp * 2 * DA_DK, tq), lambda b, g, i: (b, g, i)),
                  pl.BlockSpec((seq, hp * 2 * DA_DK), lambda b, g, i: (b, g)),
                  pl.BlockSpec((1, hp * DA_DV, seq), lambda b, g, i: (b, g, 0)),
                  pl.BlockSpec((1, DA_DV), lambda b, g, i: (0, 0))],
        out_specs=pl.BlockSpec((tq, hp * DA_DV), lambda b, g, i: (b * nq + i, g)),
        out_shape=jax.ShapeDtypeStruct((t, DA_HEADS * DA_DV), BF16),
        scratch_shapes=[pltpu.VMEM((hp, 2 * DA_DK, 2 * tq), BF16),
                        pltpu.VMEM((hp, 1, 2 * tq), F32),
                        pltpu.VMEM((hp, DA_DV + ATT_PAD_ROWS, 2 * tq), F32),
                        pltpu.VMEM((hp, DA_DV + ATT_PAD_ROWS, seq), BF16),
                        pltpu.VMEM((2 * hp, 1, 1), F32),
                        pltpu.VMEM((hp, tq, 2 * tq), F32),
                        pltpu.VMEM((hp, tq, 2 * tq), F32)],
        compiler_params=_cparams(("arbitrary", "arbitrary", "arbitrary")),
        name="diff_attn",
    )(scal, qt, kn, vt, sw)


def _hgrn_kernel(q_ref, f_ref, i_ref, g_ref, lb_ref, gw_ref, o_ref,
                 st_ref, b_ref, qs_ref, ks_ref, oi_ref, *, n_chunks):
    c = HG_CHUNK
    sub = HG_SUB
    d = HG_D
    nh = HG_HEADS
    chains = [(bb, hh) for bb in range(q_ref.shape[0]) for hh in range(nh)]

    @pl.when(pl.program_id(0) == 0)
    def _():
        st_ref[...] = jnp.zeros(st_ref.shape, F32)

    gw = gw_ref[...]
    rr = lax.broadcasted_iota(jnp.int32, (c, c), 0)
    cc = lax.broadcasted_iota(jnp.int32, (c, c), 1)
    causal = cc <= rr
    tril = jnp.where(causal, 1.0, 0.0).astype(BF16)

    def chunk(n, carry):
        r0 = pl.multiple_of(n * c, c)
        gates = []
        for ch, (bb, hh) in enumerate(chains):
            cols = slice(hh * d, (hh + 1) * d)
            log_lb = lb_ref[hh, 0:1, :]
            q = q_ref[bb, pl.ds(r0, c), cols]
            q = q * _sigmoid(q)
            a = lb_ref[hh, 1:2, :] + _log_sigmoid(f_ref[bb, pl.ds(r0, c), cols])
            logf = jnp.maximum(log_lb, a) + _log1p_exp_neg(jnp.abs(log_lb - a))
            k = 1.0 - jnp.exp(logf)
            v = i_ref[bb, pl.ds(r0, c), cols]
            hi, mid, lo = _split3(logf)
            b = _dot(tril, hi) + _dot(tril, mid) + _dot(tril, lo)
            gates.append((q, k, v, b))

        heads = []
        decay = jnp.zeros((), F32)
        for ch, (q, k, v, b) in enumerate(gates):
            b_last = b[c - 1:c, :]
            st = st_ref[ch]
            o_inter = _dot_nt((q * jnp.exp(b)).astype(BF16), st.astype(BF16))
            k_hat = k * jnp.exp(b_last - b)
            st_ref[ch] = st * jnp.exp(b_last) + _dot(v.T.astype(BF16), k_hat.astype(BF16))
            betas = []
            for blk in range(c // sub):
                beta = jnp.zeros((1, d), F32) if blk == 0 else b[blk * sub - 1:blk * sub, :]
                betas.append(beta)
                b_end = b[(blk + 1) * sub - 1:(blk + 1) * sub, :]
                decay = jnp.maximum(decay, jnp.max(beta - b_end))
            heads.append((q, k, v, b, betas, o_inter))

        @pl.when(decay < HG_SAFE_DECAY)
        def _():
            atts = []
            for q, k, v, b, betas, _ in heads:
                rows = []
                for blk in range(c // sub):
                    beta = betas[blk]
                    q_t = q[blk * sub:(blk + 1) * sub, :] * jnp.exp(b[blk * sub:(blk + 1) * sub, :] - beta)
                    k_t = k * jnp.exp(jnp.minimum(beta - b, HG_SAFE_DECAY))
                    rows.append(_dot_nt(q_t.astype(BF16), k_t.astype(BF16)))
                atts.append(jnp.where(causal, jnp.concatenate(rows, axis=0), 0.0).astype(BF16))
            for ch, att in enumerate(atts):
                oi_ref[ch] = _dot(att, heads[ch][2].astype(BF16))

        @pl.when(decay >= HG_SAFE_DECAY)
        def _():
            ridx = lax.broadcasted_iota(jnp.int32, (c, d), 0)
            for ch, (q, k, v, b, _, _) in enumerate(heads):
                b_ref[...] = b
                qs_ref[...] = q
                ks_ref[...] = k

                def row(t, carry2):
                    bt = b_ref[pl.ds(t, 1), :]
                    e = jnp.exp(jnp.where(ridx <= t, bt - b_ref[...], -jnp.inf))
                    w = jnp.sum(qs_ref[pl.ds(t, 1), :] * ks_ref[...] * e, axis=1, keepdims=True)
                    oi_ref[ch, pl.ds(t, 1), :] = jnp.sum(w * v, axis=0, keepdims=True)
                    return carry2

                lax.fori_loop(0, c, row, 0)

        for ch, (bb, hh) in enumerate(chains):
            cols = slice(hh * d, (hh + 1) * d)
            o = heads[ch][5] + oi_ref[ch]
            o = o * lax.rsqrt(jnp.mean(o * o, axis=-1, keepdims=True) + NORM_EPS) * gw
            g = g_ref[bb, pl.ds(r0, c), cols]
            o_ref[bb, pl.ds(r0, c), cols] = (o * (g * _sigmoid(g))).astype(o_ref.dtype)
        return carry

    lax.fori_loop(0, n_chunks, chunk, 0)


def _hgrn_call(bproj, lbs, gw, batch, seq, ts):
    t = bproj.shape[0]
    d = HG_D
    nh = HG_HEADS
    bp3 = bproj.reshape(batch, seq, bproj.shape[1])
    spec = lambda off: pl.BlockSpec((batch, ts, nh * d), lambda i: (0, i, off))
    out = pl.pallas_call(
        functools.partial(_hgrn_kernel, n_chunks=ts // HG_CHUNK),
        grid=(seq // ts,),
        in_specs=[spec(0), spec(1), spec(2), spec(3),
                  pl.BlockSpec((nh, 2, d), lambda i: (0, 0, 0)),
                  pl.BlockSpec((1, d), lambda i: (0, 0))],
        out_specs=pl.BlockSpec((batch, ts, nh * d), lambda i: (0, i, 0)),
        out_shape=jax.ShapeDtypeStruct((batch, seq, nh * d), BF16),
        scratch_shapes=[pltpu.VMEM((batch * nh, d, d), F32),
                        pltpu.VMEM((HG_CHUNK, d), F32),
                        pltpu.VMEM((HG_CHUNK, d), F32),
                        pltpu.VMEM((HG_CHUNK, d), F32),
                        pltpu.VMEM((batch * nh, HG_CHUNK, d), F32)],
        compiler_params=_cparams(("arbitrary",)),
        name="hgrn2",
    )(bp3, bp3, bp3, bp3, lbs, gw)
    return out.reshape(t, nh * d)


def _mlstm_kernel(uq_ref, uk_ref, v_ref, op_ref, gi_ref, gf_ref, cwq_ref, cwk_ref,
                  cbq_ref, cbk_ref, nw_ref, o_ref,
                  xq_ref, xk_ref, c_ref, m_ref, bs_ref):
    L = ML_CHUNK
    d = ML_D
    nh = ML_HEADS
    nb = uq_ref.shape[0]
    i = pl.program_id(0)
    rr = lax.broadcasted_iota(jnp.int32, (L, L), 0)
    cc = lax.broadcasted_iota(jnp.int32, (L, L), 1)

    @pl.when(i == 0)
    def _():
        c_ref[...] = jnp.zeros(c_ref.shape, F32)
        m_ref[...] = jnp.zeros(m_ref.shape, F32)
        xq_ref[:, 0:8, :] = jnp.zeros((nb, 8, nh * d), F32)
        xk_ref[:, 0:8, :] = jnp.zeros((nb, 8, nh * d), F32)
        upper = jnp.where(rr <= cc, 1.0, 0.0).astype(BF16)
        for ch in range(nb * nh):
            hi, mid, lo = _split3(_log_sigmoid(gf_ref[ch // nh, ch % nh]))
            bs_ref[ch] = _dot(hi, upper) + _dot(mid, upper) + _dot(lo, upper)

    def conv_silu(bb, u_ref, x_ref, w_ref, b_ref):
        x_ref[bb, 8:8 + L, :] = u_ref[bb]
        y = b_ref[...] + w_ref[ML_CONV - 1:ML_CONV, :] * x_ref[bb, 8:8 + L, :]
        for j in range(ML_CONV - 1):
            y = y + w_ref[j:j + 1, :] * x_ref[bb, 5 + j:5 + j + L, :]
        x_ref[bb, 0:8, :] = x_ref[bb, L:L + 8, :]
        return y * _sigmoid(y)

    q_all = [conv_silu(bb, uq_ref, xq_ref, cwq_ref, cbq_ref) for bb in range(nb)]
    k_all = [conv_silu(bb, uk_ref, xk_ref, cwk_ref, cbk_ref) * (d ** -0.5) for bb in range(nb)]
    lane = lax.broadcasted_iota(jnp.int32, (L, d), 1)
    ones_col = jnp.where(lane == 0, 1.0, 0.0)
    stage1 = []
    for ch in range(nb * nh):
        bb, hh = ch // nh, ch % nh
        cols = slice(hh * d, (hh + 1) * d)
        q = q_all[bb][:, cols].astype(BF16)
        kt = k_all[bb][:, cols].T
        v_aug = jnp.concatenate([v_ref[bb, :, cols], ones_col], axis=1).astype(BF16)
        c_aug = c_ref[ch]
        qk = _dot(q, kt.astype(BF16))
        qc = _dot(q, c_aug.astype(BF16))

        b_row = bs_ref[ch, pl.ds(i, 1), :]
        ig_row = gi_ref[bb, hh, pl.ds(i, 1), :]
        m_prev = m_ref[ch]
        g = b_row[:, L - 1:L]
        log_w = g - b_row + ig_row
        m_new = jnp.maximum(g + m_prev, jnp.max(log_w, axis=1, keepdims=True))
        w_s = jnp.exp(log_w - m_new)
        c_ref[ch] = jnp.exp(g + m_prev - m_new) * c_aug + _dot((kt * w_s).astype(BF16), v_aug)
        m_ref[ch] = m_new
        stage1.append((qk, qc, v_aug, b_row, ig_row, m_prev))

    stage2 = []
    for qk, qc, v_aug, b_row, ig_row, m_prev in stage1:
        b_col = jnp.sum(jnp.where(rr == cc, b_row, 0.0), axis=1, keepdims=True)
        log_d = jnp.where(cc <= rr, b_col + (ig_row - b_row), -jnp.inf)
        log_inter = b_col + m_prev
        m_t = jnp.maximum(log_inter, jnp.max(log_d, axis=1, keepdims=True))
        w_intra = jnp.exp(log_d - m_t) * qk
        tot = jnp.exp(log_inter - m_t) * qc + _dot(w_intra.astype(BF16), v_aug)
        stage2.append((tot, m_t))

    for ch, (tot, m_t) in enumerate(stage2):
        bb, hh = ch // nh, ch % nh
        cols = slice(hh * d, (hh + 1) * d)
        denom = jnp.maximum(jnp.abs(tot[:, d:d + 1]), jnp.exp(-m_t))
        hout = tot[:, :d] / denom
        hout = hout * lax.rsqrt(jnp.mean(hout * hout, axis=-1, keepdims=True) + NORM_EPS) * nw_ref[...]
        o_ref[bb, :, cols] = (hout * _sigmoid(op_ref[bb, :, cols])).astype(o_ref.dtype)


def _mlstm_call(cproj, gi, gf, conv_w, conv_b, nw, batch, seq):
    t = cproj.shape[0]
    d = ML_D
    nh = ML_HEADS
    L = ML_CHUNK
    nc = seq // L
    cp3 = cproj.reshape(batch, seq, cproj.shape[1])
    spec = lambda off: pl.BlockSpec((batch, L, nh * d), lambda i: (0, i, off))
    gspec = pl.BlockSpec((batch, nh, nc, L), lambda i: (0, 0, 0, 0))
    out = pl.pallas_call(
        _mlstm_kernel,
        grid=(nc,),
        in_specs=[spec(0), spec(1), spec(2), spec(3), gspec, gspec,
                  pl.BlockSpec((ML_CONV, nh * d), lambda i: (0, 0)),
                  pl.BlockSpec((ML_CONV, nh * d), lambda i: (0, 1)),
                  pl.BlockSpec((1, nh * d), lambda i: (0, 0)),
                  pl.BlockSpec((1, nh * d), lambda i: (0, 1)),
                  pl.BlockSpec((1, d), lambda i: (0, 0))],
        out_specs=pl.BlockSpec((batch, L, nh * d), lambda i: (0, i, 0)),
        out_shape=jax.ShapeDtypeStruct((batch, seq, nh * d), BF16),
        scratch_shapes=[pltpu.VMEM((batch, L + 8, nh * d), F32),
                        pltpu.VMEM((batch, L + 8, nh * d), F32),
                        pltpu.VMEM((batch * nh, d, 2 * d), F32),
                        pltpu.VMEM((batch * nh, 1, 1), F32),
                        pltpu.VMEM((batch * nh, nc, L), F32)],
        compiler_params=_cparams(("arbitrary",)),
        name="mlstm",
    )(cp3, cp3, cp3, cp3, gi, gf, conv_w, conv_w, conv_b, conv_b, nw)
    return out.reshape(t, nh * d)


def _top2_combine(logits):
    lane = lax.broadcasted_iota(jnp.int32, logits.shape, 1)
    lg = jnp.where(lane < N_EXPERTS, logits, -jnp.inf)
    ex = jnp.exp(lg - jnp.max(lg, axis=1, keepdims=True))
    probs = ex / jnp.sum(ex, axis=1, keepdims=True)
    p1 = jnp.max(probs, axis=1, keepdims=True)
    i1 = jnp.min(jnp.where(probs == p1, lane, LANES), axis=1, keepdims=True)
    rest = jnp.where(lane == i1, -1.0, probs)
    p2 = jnp.max(rest, axis=1, keepdims=True)
    i2 = jnp.min(jnp.where(rest == p2, lane, LANES), axis=1, keepdims=True)
    comb = jnp.where(lane == i1, p1, 0.0) + jnp.where(lane == i2, p2, 0.0)
    return comb / (p1 + p2)


def _merge_kernel(*refs, route):
    if route:
        (ya_ref, yb_ref, yc_ref, gp_ref, wb_ref, wo_ref, x_ref, mod_ref, nw_ref,
         rw_ref, rb_ref, xo_ref, h_ref, cmb_ref) = refs
    else:
        (ya_ref, yb_ref, yc_ref, gp_ref, wb_ref, wo_ref, x_ref, mod_ref, nw_ref,
         xo_ref, h_ref) = refs
    d = x_ref.shape[1]
    merged = None
    for n, y_ref in enumerate((ya_ref, yb_ref, yc_ref)):
        gate = _sigmoid(gp_ref[:, n * d:(n + 1) * d].astype(F32))
        term = gate * _dot(y_ref[...], wb_ref[n])
        merged = term if merged is None else merged + term
    m = mod_ref[0]
    xn = x_ref[...] + m[2:3] * _dot(merged.astype(BF16), wo_ref[...])
    xo_ref[...] = xn
    h2 = _norm_mod(xn, nw_ref[...], m[4:5], m[3:4])
    h_ref[...] = h2.astype(BF16)
    if route:
        h_hi, h_mid, _ = _split3(h2)
        r_hi, r_mid, _ = _split3(rw_ref[...])
        logits = _dot(h_hi, r_hi) + _dot(h_mid, r_hi) + _dot(h_hi, r_mid) + rb_ref[...]
        cmb_ref[...] = _top2_combine(logits)


def _merge_call(ya, yb, yc, gp, wb, wo, x, mod, nw, router, seq, tm):
    t, d = x.shape
    bw = ya.shape[1]
    per_b = seq // tm
    route = router is not None
    tok = lambda w: pl.BlockSpec((tm, w), lambda i: (i, 0))
    const2 = lambda s: pl.BlockSpec(s, lambda i: (0, 0))
    in_specs = [tok(bw), tok(bw), tok(bw), tok(3 * d),
                pl.BlockSpec((3, bw, d), lambda i: (0, 0, 0)), const2((d, d)), tok(d),
                pl.BlockSpec((1, 6, d), lambda i: (i // per_b, 0, 0)), const2((1, d))]
    out_specs = [tok(d), tok(d)]
    out_shape = [jax.ShapeDtypeStruct((t, d), F32), jax.ShapeDtypeStruct((t, d), BF16)]
    args = [ya, yb, yc, gp, wb, wo, x, mod, nw]
    if route:
        in_specs += [const2((d, LANES)), const2((1, LANES))]
        out_specs.append(tok(LANES))
        out_shape.append(jax.ShapeDtypeStruct((t, LANES), F32))
        args += list(router)
    return pl.pallas_call(
        functools.partial(_merge_kernel, route=route),
        grid=(t // tm,),
        in_specs=in_specs, out_specs=out_specs, out_shape=out_shape,
        compiler_params=_cparams(("arbitrary",)),
        name="merge_out",
    )(*args)


def _finish(acc, x_ref, mod_ref, xo_ref, nxt):
    xn = x_ref[...] + mod_ref[0][5:6] * acc
    xo_ref[...] = xn
    if nxt is not None:
        nw_ref, modn_ref, hn_ref = nxt
        mn = modn_ref[0]
        hn_ref[...] = _norm_mod(xn, nw_ref[...], mn[1:2], mn[0:1]).astype(BF16)


def _ffn_kernel(*refs, has_next):
    if has_next:
        h_ref, w1_ref, w3_ref, w2_ref, x_ref, mod_ref, nw_ref, modn_ref, xo_ref, hn_ref, acc_ref = refs
        nxt = (nw_ref, modn_ref, hn_ref)
    else:
        h_ref, w1_ref, w3_ref, w2_ref, x_ref, mod_ref, xo_ref, acc_ref = refs
        nxt = None
    f = pl.program_id(1)

    @pl.when(f == 0)
    def _():
        acc_ref[...] = jnp.zeros(acc_ref.shape, F32)

    h = h_ref[...]
    a = _dot(h, w1_ref[...])
    act = a * _sigmoid(a) * _dot(h, w3_ref[...])
    acc_ref[...] += _dot(act.astype(BF16), w2_ref[...])

    @pl.when(f == pl.num_programs(1) - 1)
    def _():
        _finish(acc_ref[...], x_ref, mod_ref, xo_ref, nxt)


def _ffn_call(h, w1, w3, w2, x, mod, nxt, seq, tm, tf):
    t, d = x.shape
    ff = w1.shape[1]
    per_b = seq // tm
    tok = lambda: pl.BlockSpec((tm, d), lambda i, f: (i, 0))
    modspec = lambda: pl.BlockSpec((1, 6, d), lambda i, f: (i // per_b, 0, 0))
    in_specs = [tok(), pl.BlockSpec((d, tf), lambda i, f: (0, f)),
                pl.BlockSpec((d, tf), lambda i, f: (0, f)),
                pl.BlockSpec((tf, d), lambda i, f: (f, 0)), tok(), modspec()]
    out_specs = [tok()]
    out_shape = [jax.ShapeDtypeStruct((t, d), F32)]
    args = [h, w1, w3, w2, x, mod]
    if nxt is not None:
        in_specs += [pl.BlockSpec((1, d), lambda i, f: (0, 0)), modspec()]
        out_specs.append(tok())
        out_shape.append(jax.ShapeDtypeStruct((t, d), BF16))
        args += list(nxt)
    return pl.pallas_call(
        functools.partial(_ffn_kernel, has_next=nxt is not None),
        grid=(t // tm, ff // tf),
        in_specs=in_specs, out_specs=out_specs, out_shape=out_shape,
        scratch_shapes=[pltpu.VMEM((tm, d), F32)],
        compiler_params=_cparams(("arbitrary", "arbitrary")),
        name="ffn_swiglu",
    )(*args)


def _moe_kernel(h_ref, cmb_ref, w13_ref, w2_ref, y_ref,
                acc_ref, xg_ref, ya_ref, rk_ref, rkt_ref, cnt_ref):
    e = pl.program_id(1)
    f = pl.program_id(2)
    tm = h_ref.shape[0]

    @pl.when((e == 0) & (f == 0))
    def _():
        r = lax.broadcasted_iota(jnp.int32, (tm, tm), 0)
        c = lax.broadcasted_iota(jnp.int32, (tm, tm), 1)
        before = jnp.where(c < r, 1.0, 0.0).astype(BF16)
        sel = cmb_ref[...] > 0.0
        rank = _dot(before, jnp.where(sel, 1.0, 0.0).astype(BF16))
        rk = jnp.where(sel, rank, -1.0)
        rk_ref[...] = rk
        rkt_ref[...] = rk.T
        cnt_ref[...] = jnp.sum(jnp.where(sel, 1.0, 0.0), axis=0, keepdims=True)
        acc_ref[...] = jnp.zeros(acc_ref.shape, F32)

    lane1 = lax.broadcasted_iota(jnp.int32, (1, LANES), 1)
    n_e = jnp.sum(jnp.where(lane1 == e, cnt_ref[...], 0.0)).astype(jnp.int32)

    def loop(n_blocks, body):
        if isinstance(n_blocks, int):
            for sb in range(n_blocks):
                body(sb, 0)
        else:
            lax.fori_loop(0, n_blocks, body, 0)

    def run(cap, n_blocks):
        cap_pad = -(-cap // LANES) * LANES

        def row0(sb):
            return sb * cap if isinstance(sb, int) else pl.multiple_of(sb * cap, cap)

        @pl.when(f == 0)
        def _():
            rank_row = rkt_ref[pl.ds(e, 1), :]

            def gather(sb, carry):
                r0 = row0(sb)
                slot = (r0 + lax.broadcasted_iota(jnp.int32, (cap, tm), 0)).astype(F32)
                onehot = jnp.where(rank_row == slot, 1.0, 0.0).astype(BF16)
                xg_ref[pl.ds(r0, cap), :] = _dot(onehot, h_ref[...]).astype(BF16)
                return carry

            loop(n_blocks, gather)

        def expert(sb, carry):
            r0 = row0(sb)
            xs = xg_ref[pl.ds(r0, cap), :]
            ab = _dot(xs, w13_ref[0])
            tf = w2_ref.shape[1]
            a = ab[:, :tf]
            act = a * _sigmoid(a) * ab[:, tf:]
            part = _dot(act.astype(BF16), w2_ref[0])

            @pl.when(f == 0)
            def _():
                ya_ref[pl.ds(r0, cap), :] = part

            @pl.when(f > 0)
            def _():
                ya_ref[pl.ds(r0, cap), :] += part

            return carry

        loop(n_blocks, expert)

        @pl.when(f == pl.num_programs(2) - 1)
        def _():
            lane = lax.broadcasted_iota(jnp.int32, (tm, LANES), 1)
            rank_col = jnp.sum(jnp.where(lane == e, rk_ref[...], 0.0), axis=1, keepdims=True)
            w_col = jnp.sum(jnp.where(lane == e, cmb_ref[...], 0.0), axis=1, keepdims=True)
            col = lax.broadcasted_iota(jnp.int32, (tm, cap_pad), 1)

            def scatter(sb, carry):
                r0 = row0(sb)
                ys = ya_ref[pl.ds(r0, cap), :].astype(BF16)
                if cap_pad > cap:
                    ys = jnp.concatenate([ys, jnp.zeros((cap_pad - cap, ys.shape[1]), BF16)], axis=0)
                hit = (rank_col == (r0 + col).astype(F32)) & (col < cap)
                acc_ref[...] += w_col * _dot(jnp.where(hit, 1.0, 0.0).astype(BF16), ys)
                return carry

            loop(n_blocks, scatter)

    for idx, cap in enumerate(MOE_CAPS):
        lower = MOE_CAPS[idx - 1] if idx else 0
        if idx == len(MOE_CAPS) - 1:
            pl.when(n_e > lower)(functools.partial(run, cap, (n_e + cap - 1) // cap))
        else:
            pl.when((n_e > lower) & (n_e <= cap))(functools.partial(run, cap, 1))

    @pl.when((e == pl.num_programs(1) - 1) & (f == pl.num_programs(2) - 1))
    def _():
        y_ref[...] = acc_ref[...].astype(y_ref.dtype)


def _moe_pack_w13(w1, w3):
    ff = w1.shape[-1]
    tf = _pick(ff, (MOE_TF,))
    parts = []
    for f in range(ff // tf):
        parts += [w1[:, :, f * tf:(f + 1) * tf].astype(BF16), w3[:, :, f * tf:(f + 1) * tf].astype(BF16)]
    return jnp.concatenate(parts, axis=-1)


def _moe_call(h, cmb, w13, w2, tm):
    t, d = h.shape
    ne, ff, _ = w2.shape
    tf = _pick(ff, (MOE_TF,))
    rows = -(-tm // MOE_CAPS[-1]) * MOE_CAPS[-1]
    return pl.pallas_call(
        _moe_kernel,
        grid=(t // tm, ne, ff // tf),
        in_specs=[pl.BlockSpec((tm, d), lambda i, e, f: (i, 0)),
                  pl.BlockSpec((tm, LANES), lambda i, e, f: (i, 0)),
                  pl.BlockSpec((1, d, 2 * tf), lambda i, e, f: (e, 0, f)),
                  pl.BlockSpec((1, tf, d), lambda i, e, f: (e, f, 0))],
        out_specs=pl.BlockSpec((tm, d), lambda i, e, f: (i, 0)),
        out_shape=jax.ShapeDtypeStruct((t, d), BF16),
        scratch_shapes=[pltpu.VMEM((tm, d), F32),
                        pltpu.VMEM((rows, d), BF16),
                        pltpu.VMEM((rows, d), F32),
                        pltpu.VMEM((tm, LANES), F32),
                        pltpu.VMEM((LANES, tm), F32),
                        pltpu.VMEM((1, LANES), F32)],
        compiler_params=_cparams(("arbitrary", "arbitrary", "arbitrary")),
        name="moe_top2",
    )(h, cmb, w13, w2)


def _resid_kernel(*refs, has_next):
    if has_next:
        y_ref, x_ref, mod_ref, nw_ref, modn_ref, xo_ref, hn_ref = refs
        nxt = (nw_ref, modn_ref, hn_ref)
    else:
        y_ref, x_ref, mod_ref, xo_ref = refs
        nxt = None
    _finish(y_ref[...].astype(F32), x_ref, mod_ref, xo_ref, nxt)


def _resid_call(y, x, mod, nxt, seq, tm):
    t, d = x.shape
    per_b = seq // tm
    tok = lambda: pl.BlockSpec((tm, d), lambda i: (i, 0))
    modspec = lambda: pl.BlockSpec((1, 6, d), lambda i: (i // per_b, 0, 0))
    in_specs = [tok(), tok(), modspec()]
    out_specs = [tok()]
    out_shape = [jax.ShapeDtypeStruct((t, d), F32)]
    args = [y, x, mod]
    if nxt is not None:
        in_specs += [pl.BlockSpec((1, d), lambda i: (0, 0)), modspec()]
        out_specs.append(tok())
        out_shape.append(jax.ShapeDtypeStruct((t, d), BF16))
        args += list(nxt)
    return pl.pallas_call(
        functools.partial(_resid_kernel, has_next=nxt is not None),
        grid=(t // tm,),
        in_specs=in_specs, out_specs=out_specs, out_shape=out_shape,
        compiler_params=_cparams(("arbitrary",)),
        name="moe_residual",
    )(*args)


def _attn_col_perm():
    hq = DA_HEADS * DA_DK
    idx = []
    for base in (0, 2 * hq):
        for h in range(DA_HEADS):
            idx += list(range(base + h * DA_DK, base + (h + 1) * DA_DK))
            idx += list(range(base + hq + h * DA_DK, base + hq + (h + 1) * DA_DK))
    idx += list(range(4 * hq, 4 * hq + DA_HEADS * DA_DV))
    return np.asarray(idx, np.int32)


def _pick(n, pref):
    for c in pref:
        if n % c == 0:
            return c
    return n


def kernel(x, c, w_mod, b_mod, norm1_w, norm2_w, w_in, c_conv_w, c_conv_b, a_qnorm_w, a_knorm_w,
           a_lambda_q1, a_lambda_k1, a_lambda_q2, a_lambda_k2, a_subln_w, b_lb_logits, b_gnorm_w,
           c_igate_b, c_fgate_b, c_norm_w, w_branch, w_out, ffn_w1, ffn_w3, ffn_w2,
           moe_router_w, moe_router_b, moe_w1, moe_w3, moe_w2):
    batch, seq, d = x.shape
    depth = w_in.shape[0]
    t = batch * seq
    n_a = 4 * DA_HEADS * DA_DK + DA_HEADS * DA_DV
    n_b = 4 * HG_HEADS * HG_D
    n_c = 4 * ML_HEADS * ML_D + 2 * ML_HEADS
    n_c_pad = -(-n_c // LANES) * LANES
    tm = _pick(seq, (1024, 512, 256))
    tq = _pick(seq, (ATT_TQ,))
    ts_h = _pick(seq, (512, 256, 128))

    c8 = jnp.zeros((8, d), F32).at[:batch].set(c)
    mod = _mod_call(c8, w_mod, b_mod.reshape(depth, 1, 6 * d))[:, :batch].reshape(depth, batch, 6, d)

    lb_all = jnp.cumsum(jax.nn.softmax(b_lb_logits.astype(F32), axis=0), axis=0)
    lb_all = lb_all - lb_all[:1]
    slopes = jnp.asarray(_attn_slopes(), F32)

    xf = x.reshape(t, d)
    h = _normmod_call(xf, norm1_w[0].reshape(1, d), mod[0], seq, tm)
    perm_a = _attn_col_perm()
    w_bf = lax.optimization_barrier(w_in.astype(BF16))
    w_a_all = w_bf[:, :, :n_a][:, :, perm_a]
    w_b_all = w_bf[:, :, n_a:n_a + n_b]
    w_c_all = jnp.pad(w_bf[:, :, n_a + n_b:n_a + n_b + n_c], ((0, 0), (0, 0), (0, n_c_pad - n_c)))
    w_g_all = w_bf[:, :, n_a + n_b + n_c:]
    for l in range(depth):
        lam_init = 0.8 - 0.6 * math.exp(-0.3 * l)
        w_a, w_b, w_c, w_g = w_a_all[l], w_b_all[l], w_c_all[l], w_g_all[l]
        bias_c = jnp.zeros((1, n_c_pad), F32)
        bias_c = bias_c.at[0, n_c - 2 * ML_HEADS:n_c - ML_HEADS].set(c_igate_b[l])
        bias_c = bias_c.at[0, n_c - ML_HEADS:n_c].set(c_fgate_b[l])
        pb = _matmul(h, w_b, jnp.zeros((1, n_b), F32), F32, tm, n_b // 2)
        pc, pc_gates = _matmul_tail(h, w_c, bias_c, tm, n_c_pad - (n_c - 2 * ML_HEADS))
        pg = _matmul(h, w_g, jnp.zeros((1, 3 * d), F32), BF16, tm, 3 * d // 2)

        qw = (jnp.tile(a_qnorm_w[l], 2 * DA_HEADS) * (DA_DK ** -0.5 * LOG2E)).reshape(1, -1)
        kw = jnp.tile(a_knorm_w[l], 2 * DA_HEADS).reshape(1, -1)
        kn, qt, vt = _attn_prep_call(h, w_a, qw, kw, batch, seq, _pick(seq, (512, 256)))
        lam = (jnp.exp(jnp.sum(a_lambda_q1[l] * a_lambda_k1[l]))
               - jnp.exp(jnp.sum(a_lambda_q2[l] * a_lambda_k2[l])) + lam_init)
        scal = jnp.concatenate([slopes, jnp.stack([lam, jnp.asarray(1.0 - lam_init, F32)])]).astype(F32)
        y_a = _attn_call(scal, qt, kn, vt, a_subln_w[l].reshape(1, -1), batch, seq, tq)

        lb = lb_all[l].reshape(HG_HEADS, 1, HG_D)
        lbs = jnp.concatenate([jnp.log(lb), jnp.log1p(-lb)], axis=1)
        y_b = _hgrn_call(pb, lbs, b_gnorm_w[l].reshape(1, -1), batch, seq, ts_h)

        gates = pc_gates[:, :2 * ML_HEADS].reshape(batch, seq, 2, ML_HEADS)
        gates = gates.transpose(2, 0, 3, 1).reshape(2, batch, ML_HEADS, seq // ML_CHUNK, ML_CHUNK)
        y_c = _mlstm_call(pc, gates[0], gates[1], c_conv_w[l], c_conv_b[l].reshape(1, -1),
                          c_norm_w[l].reshape(1, -1), batch, seq)

        dense = l % 2 == 0
        router = None
        if not dense:
            rw = jnp.pad(moe_router_w[l // 2], ((0, 0), (0, LANES - N_EXPERTS)))
            rb = jnp.pad(moe_router_b[l // 2], (0, LANES - N_EXPERTS)).reshape(1, LANES)
            router = (rw, rb)
        outs = _merge_call(y_a, y_b, y_c, pg, w_branch[l].astype(BF16), w_out[l].astype(BF16), xf,
                           mod[l], norm2_w[l].reshape(1, d), router, seq, 512)
        xf, h2 = outs[0], outs[1]
        nxt = None if l == depth - 1 else (norm1_w[l + 1].reshape(1, d), mod[l + 1])
        if dense:
            res = _ffn_call(h2, ffn_w1[l // 2].astype(BF16), ffn_w3[l // 2].astype(BF16),
                            ffn_w2[l // 2].astype(BF16), xf, mod[l], nxt, seq, tm, FFN_TF)
        else:
            y = _moe_call(h2, outs[2], _moe_pack_w13(moe_w1[l // 2], moe_w3[l // 2]),
                          moe_w2[l // 2].astype(BF16), tm)
            res = _resid_call(y, xf, mod[l], nxt, seq, tm)
        xf = res[0]
        if nxt is not None:
            h = res[1]
    return xf.reshape(batch, seq, d)
```

```python
import functools
import math

import numpy as np
import jax
import jax.numpy as jnp
from jax import lax
from jax.experimental import pallas as pl
from jax.experimental.pallas import tpu as pltpu

F32 = jnp.float32
BF16 = jnp.bfloat16
NORM_EPS = 1e-6

CHUNK = 64
DA_HEADS, DA_DK, DA_DV = 4, 64, 128
HG_HEADS, HG_D = 4, 128
ML_HEADS, ML_D, ML_CONV = 4, 128, 4
N_EXPERTS = 8

LANES = 128
VMEM_LIMIT = 56 * 1024 * 1024

ATT_TQ = 256
ATT_HEADS_PER_STEP = 2
ATT_PAD_ROWS = 16
LOG2E = math.log2(math.e)
ATT_BOUND_MARGIN = 1.01
ATT_BOUND_EPS = 1e-3
ATT_BOUND_MAX = 40.0
ATT_ZERO_EXP = 160.0
HG_CHUNK = 128
HG_SUB = 32
HG_SAFE_DECAY = 80.0
ML_CHUNK = 256
MOE_CAPS = (256, 288, 320, 352, 384)
MOE_TF = 1408
FFN_TF = 256


def _cparams(sem, vmem=VMEM_LIMIT):
    return pltpu.CompilerParams(dimension_semantics=sem, vmem_limit_bytes=vmem)


def _dot(a, b):
    return jnp.dot(a, b, preferred_element_type=F32)


def _dot_nt(a, b):
    return lax.dot_general(a, b, (((1,), (1,)), ((), ())), preferred_element_type=F32)


def _split3(x):
    hi = x.astype(BF16)
    r1 = x - hi.astype(F32)
    mid = r1.astype(BF16)
    lo = (r1 - mid.astype(F32)).astype(BF16)
    return hi, mid, lo


def _norm_mod(x, w, scale, shift):
    y = x * lax.rsqrt(jnp.mean(x * x, axis=-1, keepdims=True) + NORM_EPS) * w
    return y * (1.0 + scale) + shift


def _sigmoid(x):
    return 0.5 * jnp.tanh(0.5 * x) + 0.5


def _log1p_exp_neg(z):
    return jnp.log(1.0 + jnp.exp(-z))


def _log_sigmoid(x):
    return jnp.minimum(x, 0.0) - _log1p_exp_neg(jnp.abs(x))


def _mod_kernel(c_ref, w_ref, b_ref, o_ref):
    cnd = c_ref[...]
    cnd = cnd * _sigmoid(cnd)
    o_ref[0] = _dot(cnd.astype(BF16), w_ref[0].astype(BF16)) + b_ref[0]


def _mod_call(c8, w_mod, b_mod):
    depth, d, n = w_mod.shape
    tn = 1536 if n % 1536 == 0 else n
    return pl.pallas_call(
        _mod_kernel,
        grid=(depth, n // tn),
        in_specs=[pl.BlockSpec((8, d), lambda l, j: (0, 0)),
                  pl.BlockSpec((1, d, tn), lambda l, j: (l, 0, j)),
                  pl.BlockSpec((1, 1, tn), lambda l, j: (l, 0, j))],
        out_specs=pl.BlockSpec((1, 8, tn), lambda l, j: (l, 0, j)),
        out_shape=jax.ShapeDtypeStruct((depth, 8, n), F32),
        compiler_params=_cparams(("arbitrary", "arbitrary")),
        name="adaln_mod",
    )(c8, w_mod, b_mod)


def _normmod_kernel(x_ref, nw_ref, mod_ref, h_ref):
    m = mod_ref[0]
    h_ref[...] = _norm_mod(x_ref[...], nw_ref[...], m[1:2], m[0:1]).astype(BF16)


def _normmod_call(x, nw, mod, seq, tm):
    t, d = x.shape
    per_b = seq // tm
    return pl.pallas_call(
        _normmod_kernel,
        grid=(t // tm,),
        in_specs=[pl.BlockSpec((tm, d), lambda i: (i, 0)),
                  pl.BlockSpec((1, d), lambda i: (0, 0)),
                  pl.BlockSpec((1, 6, d), lambda i: (i // per_b, 0, 0))],
        out_specs=pl.BlockSpec((tm, d), lambda i: (i, 0)),
        out_shape=jax.ShapeDtypeStruct((t, d), BF16),
        compiler_params=_cparams(("arbitrary",)),
        name="prenorm_mod",
    )(x, nw, mod)


def _mm_kernel(x_ref, w_ref, b_ref, o_ref):
    o_ref[...] = (_dot(x_ref[...], w_ref[...]) + b_ref[...]).astype(o_ref.dtype)


def _matmul(x, w, bias, out_dtype, tm, tn):
    t, k = x.shape
    n = w.shape[1]
    return pl.pallas_call(
        _mm_kernel,
        grid=(n // tn, t // tm),
        in_specs=[pl.BlockSpec((tm, k), lambda j, i: (i, 0)),
                  pl.BlockSpec((k, tn), lambda j, i: (0, j)),
                  pl.BlockSpec((1, tn), lambda j, i: (0, j))],
        out_specs=pl.BlockSpec((tm, tn), lambda j, i: (i, j)),
        out_shape=jax.ShapeDtypeStruct((t, n), out_dtype),
        compiler_params=_cparams(("arbitrary", "arbitrary")),
        name="in_proj",
    )(x, w, bias)


def _mm_tail_kernel(x_ref, w_ref, b_ref, o_ref, tail_ref):
    res = _dot(x_ref[...], w_ref[...]) + b_ref[...]
    main = o_ref.shape[1]
    o_ref[...] = res[:, :main]
    tail_ref[...] = res[:, main:]


def _matmul_tail(x, w, bias, tm, tail):
    t, k = x.shape
    n = w.shape[1]
    return pl.pallas_call(
        _mm_tail_kernel,
        grid=(t // tm,),
        in_specs=[pl.BlockSpec((tm, k), lambda i: (i, 0)),
                  pl.BlockSpec((k, n), lambda i: (0, 0)),
                  pl.BlockSpec((1, n), lambda i: (0, 0))],
        out_specs=[pl.BlockSpec((tm, n - tail), lambda i: (i, 0)),
                   pl.BlockSpec((tm, tail), lambda i: (i, 0))],
        out_shape=[jax.ShapeDtypeStruct((t, n - tail), F32),
                   jax.ShapeDtypeStruct((t, tail), F32)],
        compiler_params=_cparams(("arbitrary",)),
        name="in_proj_c",
    )(x, w, bias)


def _attn_prep_kernel(x_ref, w_ref, qw_ref, kw_ref, kn_ref, qt_ref, vt_ref):
    hw = DA_HEADS * 2 * DA_DK
    a = _dot(x_ref[...], w_ref[...])
    r = lax.broadcasted_iota(jnp.int32, (hw, hw), 0) // DA_DK
    c = lax.broadcasted_iota(jnp.int32, (hw, hw), 1) // DA_DK
    group = jnp.where(r == c, 1.0, 0.0).astype(BF16)

    def qk_norm(z, w):
        ms = _dot((z * z).astype(BF16), group) * (1.0 / DA_DK)
        return z * lax.rsqrt(ms + NORM_EPS) * w

    qn = qk_norm(a[:, :hw], qw_ref[...])
    kn = qk_norm(a[:, hw:2 * hw], kw_ref[...])
    kn_ref[...] = kn.astype(BF16)
    qt_ref[0] = qn.T.astype(BF16)
    vt_ref[0] = a[:, 2 * hw:].T.astype(BF16)


def _attn_prep_call(x, w, qw, kw, batch, seq, ts):
    t, d = x.shape
    hw = DA_HEADS * 2 * DA_DK
    hv = DA_HEADS * DA_DV
    per_b = seq // ts
    return pl.pallas_call(
        _attn_prep_kernel,
        grid=(t // ts,),
        in_specs=[pl.BlockSpec((ts, d), lambda i: (i, 0)),
                  pl.BlockSpec((d, 2 * hw + hv), lambda i: (0, 0)),
                  pl.BlockSpec((1, hw), lambda i: (0, 0)),
                  pl.BlockSpec((1, hw), lambda i: (0, 0))],
        out_specs=[pl.BlockSpec((ts, hw), lambda i: (i, 0)),
                   pl.BlockSpec((1, hw, ts), lambda i: (i // per_b, 0, i % per_b)),
                   pl.BlockSpec((1, hv, ts), lambda i: (i // per_b, 0, i % per_b))],
        out_shape=[jax.ShapeDtypeStruct((t, hw), BF16),
                   jax.ShapeDtypeStruct((batch, hw, seq), BF16),
                   jax.ShapeDtypeStruct((batch, hv, seq), BF16)],
        compiler_params=_cparams(("arbitrary",)),
        name="attn_prep",
    )(x, w, qw, kw)


def _attn_slopes():
    return [LOG2E * 2.0 ** (-8.0 * (h + 1) / DA_HEADS) for h in range(DA_HEADS)]


def _attn_window(head, tq):
    return int((ATT_ZERO_EXP / _attn_slopes()[head] - 1.0) // tq) + 1


def _attn_kernel(sc_ref, qt_ref, k_ref, vt_ref, sw_ref, o_ref,
                 qz_ref, m_ref, acc_ref, vs_ref, kmax_ref, s_ref, s2_ref, *, tq, seq):
    hp = ATT_HEADS_PER_STEP
    dv = DA_DV
    g = pl.program_id(1)
    i = pl.program_id(2)
    lam = sc_ref[DA_HEADS]
    out_scale = sc_ref[DA_HEADS + 1]
    slopes = [sc_ref[hp * g + hh] for hh in range(hp)]
    row16 = lax.broadcasted_iota(jnp.int32, (ATT_PAD_ROWS, tq), 0)
    ones_rows = jnp.where(row16 == 0, 1.0, 0.0).astype(BF16)

    half = lax.broadcasted_iota(jnp.int32, (1, 2 * DA_DK), 1) < DA_DK

    @pl.when(i == 0)
    def _():
        pos = lax.broadcasted_iota(jnp.int32, (1, seq), 1) & (tq - 1)
        rel = (pos - (tq - 1)).astype(F32)
        sub16 = lax.broadcasted_iota(jnp.int32, (ATT_PAD_ROWS, seq), 0)
        gr = lax.broadcasted_iota(jnp.int32, (2 * DA_DK, 2 * DA_DK), 0) // DA_DK
        gc = lax.broadcasted_iota(jnp.int32, (2 * DA_DK, 2 * DA_DK), 1) // DA_DK
        group = jnp.where(gr == gc, 1.0, 0.0).astype(BF16)
        for hh in range(hp):
            w = jnp.exp2(slopes[hh] * rel)
            vs_ref[hh, 0:dv, :] = (vt_ref[0, hh * dv:(hh + 1) * dv, :].astype(F32) * w).astype(BF16)
            vs_ref[hh, dv:dv + ATT_PAD_ROWS, :] = jnp.where(sub16 == 0, w, 0.0).astype(BF16)

            def knorm(n, best):
                kc = k_ref[pl.ds(pl.multiple_of(n * tq, tq), tq),
                           hh * 2 * DA_DK:(hh + 1) * 2 * DA_DK].astype(F32)
                return jnp.maximum(best, jnp.max(_dot((kc * kc).astype(BF16), group), axis=0, keepdims=True))

            k2 = lax.fori_loop(0, seq // tq, knorm, jnp.zeros((1, 2 * DA_DK), F32))
            kmax_ref[2 * hh] = jnp.max(jnp.where(half, k2, 0.0), axis=1, keepdims=True)
            kmax_ref[2 * hh + 1] = jnp.max(jnp.where(half, 0.0, k2), axis=1, keepdims=True)

    row = lax.broadcasted_iota(jnp.int32, (2 * DA_DK, tq), 0)
    qq_row = lax.broadcasted_iota(jnp.int32, (1, tq), 1).astype(F32)
    bound_max = jnp.zeros((), F32)
    firsts = []

    def scores(hh, blk):
        start = pl.multiple_of(blk * tq, tq)
        return _dot(k_ref[pl.ds(start, tq), hh * 2 * DA_DK:(hh + 1) * 2 * DA_DK], qz_ref[hh])

    for hh in range(hp):
        qt = qt_ref[0, hh * 2 * DA_DK:(hh + 1) * 2 * DA_DK, :]
        zero = jnp.zeros_like(qt)
        q1 = jnp.where(row < DA_DK, qt, zero)
        q2 = jnp.where(row >= DA_DK, qt, zero)
        qz_ref[hh] = jnp.concatenate([q1, q2], axis=1)
        window = jnp.int32(_attn_window(hh, tq))
        for grp in range(1, DA_HEADS // hp):
            window = jnp.where(g == grp, _attn_window(grp * hp + hh, tq), window)
        firsts.append((jnp.maximum(i - window, 0) // 2) * 2)
        s_ref[hh] = scores(hh, firsts[hh])
        s2_ref[hh] = scores(hh, jnp.minimum(firsts[hh] + 1, i))
        qn = jnp.concatenate(
            [jnp.sum(jnp.square(q1.astype(F32)), axis=0, keepdims=True) * kmax_ref[2 * hh],
             jnp.sum(jnp.square(q2.astype(F32)), axis=0, keepdims=True) * kmax_ref[2 * hh + 1]], axis=1)
        bound = jnp.sqrt(qn) * ATT_BOUND_MARGIN + ATT_BOUND_EPS
        bound_max = jnp.maximum(bound_max, jnp.max(bound))
        m_ref[hh] = bound + slopes[hh] * jnp.concatenate([qq_row, qq_row], axis=1)
    acc_ref[...] = jnp.zeros(acc_ref.shape, F32)

    kk = lax.broadcasted_iota(jnp.int32, (tq, 2 * tq), 0)
    cc = lax.broadcasted_iota(jnp.int32, (tq, 2 * tq), 1)
    qq = jnp.where(cc >= tq, cc - tq, cc)
    dist = (qq - jnp.abs(qq - kk)).astype(F32)
    visible = (kk // CHUNK) <= (qq // CHUNK)

    def diag_values(hh):
        start = pl.multiple_of(i * tq, tq)
        return jnp.concatenate([vt_ref[0, hh * dv:(hh + 1) * dv, pl.ds(start, tq)], ones_rows], axis=0)

    @pl.when(bound_max <= ATT_BOUND_MAX)
    def _():
        def probs(hh, blk, s):
            last_key = ((blk + 1 - i) * tq - 1).astype(F32)
            return jnp.exp2(s - (m_ref[hh] - slopes[hh] * last_key)).astype(BF16)

        def body(heads, jj, carry):
            blk = 2 * jj
            start = pl.multiple_of(blk * tq, 2 * tq)
            for hh in heads:
                p = jnp.concatenate([probs(hh, blk, s_ref[hh]), probs(hh, blk + 1, s2_ref[hh])], axis=0)
                s_ref[hh] = scores(hh, jnp.minimum(blk + 2, i))
                s2_ref[hh] = scores(hh, jnp.minimum(blk + 3, i))
                acc_ref[hh] += _dot(vs_ref[hh, :, pl.ds(start, 2 * tq)], p)
            return carry

        for hh in range(hp - 1, 0, -1):
            lax.fori_loop(firsts[hh] // 2, firsts[hh - 1] // 2,
                          functools.partial(body, tuple(range(hh, hp))), 0)
        lax.fori_loop(firsts[0] // 2, i // 2, functools.partial(body, tuple(range(hp))), 0)

        @pl.when(i % 2 == 1)
        def _():
            start = pl.multiple_of((i - 1) * tq, tq)
            for hh in range(hp):
                acc_ref[hh] += _dot(vs_ref[hh, :, pl.ds(start, tq)], probs(hh, i - 1, s_ref[hh]))
                s_ref[hh] = s2_ref[hh]

        for hh in range(hp):
            s = jnp.where(visible, s_ref[hh] + slopes[hh] * dist, -jnp.inf)
            acc_ref[hh] += _dot(diag_values(hh), jnp.exp2(s - m_ref[hh]).astype(BF16))

    @pl.when(bound_max > ATT_BOUND_MAX)
    def _():
        m_ref[...] = jnp.full(m_ref.shape, -jnp.inf, F32)

        def update(hh, s, bound_shift, v_aug):
            m_old = m_ref[hh]
            m_new = jnp.maximum(m_old, jnp.max(s, axis=0, keepdims=True) + bound_shift)
            p = jnp.exp2(s - (m_new - bound_shift))
            acc_ref[hh] = jnp.exp2(m_old - m_new) * acc_ref[hh] + _dot(v_aug, p.astype(BF16))
            m_ref[hh] = m_new

        def body(j, carry):
            start = pl.multiple_of(j * tq, tq)
            last_key = ((j + 1 - i) * tq - 1).astype(F32)
            for hh in range(hp):
                update(hh, scores(hh, j), slopes[hh] * last_key, vs_ref[hh, :, pl.ds(start, tq)])
            return carry

        lax.fori_loop(0, i, body, 0)
        for hh in range(hp):
            s = jnp.where(visible, scores(hh, i) + slopes[hh] * dist, -jnp.inf)
            update(hh, s, jnp.zeros((), F32), diag_values(hh))

    outs = []
    for hh in range(hp):
        acc = acc_ref[hh]
        o2 = acc[:dv, :] * (1.0 / acc[dv:dv + 1, :])
        o = (o2[:, :tq] - lam * o2[:, tq:]).T
        o = o * lax.rsqrt(jnp.mean(o * o, axis=-1, keepdims=True) + NORM_EPS) * sw_ref[...]
        outs.append(o * out_scale)
    o_ref[...] = jnp.concatenate(outs, axis=1).astype(o_ref.dtype)


def _attn_call(scal, qt, kn, vt, sw, batch, seq, tq):
    t = kn.shape[0]
    nq = seq // tq
    hp = ATT_HEADS_PER_STEP
    return pl.pallas_call(
        functools.partial(_attn_kernel, tq=tq, seq=seq),
        grid=(batch, DA_HEADS // hp, nq),
        in_specs=[pl.BlockSpec(memory_space=pltpu.SMEM),
                  pl.BlockSpec((1, hp * 2 * DA_DK, tq), lambda b, g, i: (b, g, i)),
                  pl.BlockSpec((seq, hp * 2 * DA_DK), lambda b, g, i: (b, g)),
                  pl.BlockSpec((1, hp * DA_DV, seq), lambda b, g, i: (b, g, 0)),
                  pl.BlockSpec((1, DA_DV), lambda b, g, i: (0, 0))],
        out_specs=pl.BlockSpec((tq, hp * DA_DV), lambda b, g, i: (b * nq + i, g)),
        out_shape=jax.ShapeDtypeStruct((t, DA_HEADS * DA_DV), BF16),
        scratch_shapes=[pltpu.VMEM((hp, 2 * DA_DK, 2 * tq), BF16),
                        pltpu.VMEM((hp, 1, 2 * tq), F32),
                        pltpu.VMEM((hp, DA_DV + ATT_PAD_ROWS, 2 * tq), F32),
                        pltpu.VMEM((hp, DA_DV + ATT_PAD_ROWS, seq), BF16),
                        pltpu.VMEM((2 * hp, 1, 1), F32),
                        pltpu.VMEM((hp, tq, 2 * tq), F32),
                        pltpu.VMEM((hp, tq, 2 * tq), F32)],
        compiler_params=_cparams(("arbitrary", "arbitrary", "arbitrary")),
        name="diff_attn",
    )(scal, qt, kn, vt, sw)


def _hgrn_kernel(q_ref, f_ref, i_ref, g_ref, lb_ref, gw_ref, o_ref,
                 st_ref, b_ref, qs_ref, ks_ref, oi_ref, *, n_chunks):
    c = HG_CHUNK
    sub = HG_SUB
    d = HG_D
    nh = HG_HEADS
    chains = [(bb, hh) for bb in range(q_ref.shape[0]) for hh in range(nh)]

    @pl.when(pl.program_id(0) == 0)
    def _():
        st_ref[...] = jnp.zeros(st_ref.shape, F32)

    gw = gw_ref[...]
    rr = lax.broadcasted_iota(jnp.int32, (c, c), 0)
    cc = lax.broadcasted_iota(jnp.int32, (c, c), 1)
    causal = cc <= rr
    tril = jnp.where(causal, 1.0, 0.0).astype(BF16)

    def chunk(n, carry):
        r0 = pl.multiple_of(n * c, c)
        gates = []
        for ch, (bb, hh) in enumerate(chains):
            cols = slice(hh * d, (hh + 1) * d)
            log_lb = lb_ref[hh, 0:1, :]
            q = q_ref[bb, pl.ds(r0, c), cols]
            q = q * _sigmoid(q)
            a = lb_ref[hh, 1:2, :] + _log_sigmoid(f_ref[bb, pl.ds(r0, c), cols])
            logf = jnp.maximum(log_lb, a) + _log1p_exp_neg(jnp.abs(log_lb - a))
            k = 1.0 - jnp.exp(logf)
            v = i_ref[bb, pl.ds(r0, c), cols]
            hi, mid, lo = _split3(logf)
            b = _dot(tril, hi) + _dot(tril, mid) + _dot(tril, lo)
            gates.append((q, k, v, b))

        heads = []
        decay = jnp.zeros((), F32)
        for ch, (q, k, v, b) in enumerate(gates):
            b_last = b[c - 1:c, :]
            st = st_ref[ch]
            o_inter = _dot_nt((q * jnp.exp(b)).astype(BF16), st.astype(BF16))
            k_hat = k * jnp.exp(b_last - b)
            st_ref[ch] = st * jnp.exp(b_last) + _dot(v.T.astype(BF16), k_hat.astype(BF16))
            betas = []
            for blk in range(c // sub):
                beta = jnp.zeros((1, d), F32) if blk == 0 else b[blk * sub - 1:blk * sub, :]
                betas.append(beta)
                b_end = b[(blk + 1) * sub - 1:(blk + 1) * sub, :]
                decay = jnp.maximum(decay, jnp.max(beta - b_end))
            heads.append((q, k, v, b, betas, o_inter))

        @pl.when(decay < HG_SAFE_DECAY)
        def _():
            atts = []
            for q, k, v, b, betas, _ in heads:
                rows = []
                for blk in range(c // sub):
                    beta = betas[blk]
                    q_t = q[blk * sub:(blk + 1) * sub, :] * jnp.exp(b[blk * sub:(blk + 1) * sub, :] - beta)
                    k_t = k * jnp.exp(jnp.minimum(beta - b, HG_SAFE_DECAY))
                    rows.append(_dot_nt(q_t.astype(BF16), k_t.astype(BF16)))
                atts.append(jnp.where(causal, jnp.concatenate(rows, axis=0), 0.0).astype(BF16))
            for ch, att in enumerate(atts):
                oi_ref[ch] = _dot(att, heads[ch][2].astype(BF16))

        @pl.when(decay >= HG_SAFE_DECAY)
        def _():
            ridx = lax.broadcasted_iota(jnp.int32, (c, d), 0)
            for ch, (q, k, v, b, _, _) in enumerate(heads):
                b_ref[...] = b
                qs_ref[...] = q
                ks_ref[...] = k

                def row(t, carry2):
                    bt = b_ref[pl.ds(t, 1), :]
                    e = jnp.exp(jnp.where(ridx <= t, bt - b_ref[...], -jnp.inf))
                    w = jnp.sum(qs_ref[pl.ds(t, 1), :] * ks_ref[...] * e, axis=1, keepdims=True)
                    oi_ref[ch, pl.ds(t, 1), :] = jnp.sum(w * v, axis=0, keepdims=True)
                    return carry2

                lax.fori_loop(0, c, row, 0)

        for ch, (bb, hh) in enumerate(chains):
            cols = slice(hh * d, (hh + 1) * d)
            o = heads[ch][5] + oi_ref[ch]
            o = o * lax.rsqrt(jnp.mean(o * o, axis=-1, keepdims=True) + NORM_EPS) * gw
            g = g_ref[bb, pl.ds(r0, c), cols]
            o_ref[bb, pl.ds(r0, c), cols] = (o * (g * _sigmoid(g))).astype(o_ref.dtype)
        return carry

    lax.fori_loop(0, n_chunks, chunk, 0)


def _hgrn_call(bproj, lbs, gw, batch, seq, ts):
    t = bproj.shape[0]
    d = HG_D
    nh = HG_HEADS
    bp3 = bproj.reshape(batch, seq, bproj.shape[1])
    spec = lambda off: pl.BlockSpec((batch, ts, nh * d), lambda i: (0, i, off))
    out = pl.pallas_call(
        functools.partial(_hgrn_kernel, n_chunks=ts // HG_CHUNK),
        grid=(seq // ts,),
        in_specs=[spec(0), spec(1), spec(2), spec(3),
                  pl.BlockSpec((nh, 2, d), lambda i: (0, 0, 0)),
                  pl.BlockSpec((1, d), lambda i: (0, 0))],
        out_specs=pl.BlockSpec((batch, ts, nh * d), lambda i: (0, i, 0)),
        out_shape=jax.ShapeDtypeStruct((batch, seq, nh * d), BF16),
        scratch_shapes=[pltpu.VMEM((batch * nh, d, d), F32),
                        pltpu.VMEM((HG_CHUNK, d), F32),
                        pltpu.VMEM((HG_CHUNK, d), F32),
                        pltpu.VMEM((HG_CHUNK, d), F32),
                        pltpu.VMEM((batch * nh, HG_CHUNK, d), F32)],
        compiler_params=_cparams(("arbitrary",)),
        name="hgrn2",
    )(bp3, bp3, bp3, bp3, lbs, gw)
    return out.reshape(t, nh * d)


def _mlstm_kernel(uq_ref, uk_ref, v_ref, op_ref, gi_ref, gf_ref, cwq_ref, cwk_ref,
                  cbq_ref, cbk_ref, nw_ref, hg_ref, wg_ref, o_ref, pg_ref,
                  xq_ref, xk_ref, c_ref, m_ref, bs_ref):
    L = ML_CHUNK
    n_gate = wg_ref.shape[1]

    def gate_proj(part):
        cols = slice(part * n_gate // 3, (part + 1) * n_gate // 3)
        for bb in range(hg_ref.shape[0]):
            pg_ref[bb, :, cols] = _dot(hg_ref[bb], wg_ref[:, cols]).astype(pg_ref.dtype)

    d = ML_D
    nh = ML_HEADS
    nb = uq_ref.shape[0]
    i = pl.program_id(0)
    rr = lax.broadcasted_iota(jnp.int32, (L, L), 0)
    cc = lax.broadcasted_iota(jnp.int32, (L, L), 1)

    @pl.when(i == 0)
    def _():
        c_ref[...] = jnp.zeros(c_ref.shape, F32)
        m_ref[...] = jnp.zeros(m_ref.shape, F32)
        xq_ref[:, 0:8, :] = jnp.zeros((nb, 8, nh * d), F32)
        xk_ref[:, 0:8, :] = jnp.zeros((nb, 8, nh * d), F32)
        upper = jnp.where(rr <= cc, 1.0, 0.0).astype(BF16)
        for ch in range(nb * nh):
            hi, mid, lo = _split3(_log_sigmoid(gf_ref[ch // nh, ch % nh]))
            bs_ref[ch] = _dot(hi, upper) + _dot(mid, upper) + _dot(lo, upper)

    def conv_silu(bb, u_ref, x_ref, w_ref, b_ref):
        x_ref[bb, 8:8 + L, :] = u_ref[bb]
        y = b_ref[...] + w_ref[ML_CONV - 1:ML_CONV, :] * x_ref[bb, 8:8 + L, :]
        for j in range(ML_CONV - 1):
            y = y + w_ref[j:j + 1, :] * x_ref[bb, 5 + j:5 + j + L, :]
        x_ref[bb, 0:8, :] = x_ref[bb, L:L + 8, :]
        return y * _sigmoid(y)

    gate_proj(0)
    q_all = [conv_silu(bb, uq_ref, xq_ref, cwq_ref, cbq_ref) for bb in range(nb)]
    k_all = [conv_silu(bb, uk_ref, xk_ref, cwk_ref, cbk_ref) * (d ** -0.5) for bb in range(nb)]
    lane = lax.broadcasted_iota(jnp.int32, (L, d), 1)
    ones_col = jnp.where(lane == 0, 1.0, 0.0)
    stage1 = []
    for ch in range(nb * nh):
        bb, hh = ch // nh, ch % nh
        cols = slice(hh * d, (hh + 1) * d)
        q = q_all[bb][:, cols].astype(BF16)
        kt = k_all[bb][:, cols].T
        v_aug = jnp.concatenate([v_ref[bb, :, cols], ones_col], axis=1).astype(BF16)
        c_aug = c_ref[ch]
        qk = _dot(q, kt.astype(BF16))
        qc = _dot(q, c_aug.astype(BF16))

        b_row = bs_ref[ch, pl.ds(i, 1), :]
        ig_row = gi_ref[bb, hh, pl.ds(i, 1), :]
        m_prev = m_ref[ch]
        g = b_row[:, L - 1:L]
        log_w = g - b_row + ig_row
        m_new = jnp.maximum(g + m_prev, jnp.max(log_w, axis=1, keepdims=True))
        w_s = jnp.exp(log_w - m_new)
        c_ref[ch] = jnp.exp(g + m_prev - m_new) * c_aug + _dot((kt * w_s).astype(BF16), v_aug)
        m_ref[ch] = m_new
        stage1.append((qk, qc, v_aug, b_row, ig_row, m_prev))

    gate_proj(1)
    stage2 = []
    for qk, qc, v_aug, b_row, ig_row, m_prev in stage1:
        b_col = jnp.sum(jnp.where(rr == cc, b_row, 0.0), axis=1, keepdims=True)
        log_d = jnp.where(cc <= rr, b_col + (ig_row - b_row), -jnp.inf)
        log_inter = b_col + m_prev
        m_t = jnp.maximum(log_inter, jnp.max(log_d, axis=1, keepdims=True))
        w_intra = jnp.exp(log_d - m_t) * qk
        tot = jnp.exp(log_inter - m_t) * qc + _dot(w_intra.astype(BF16), v_aug)
        stage2.append((tot, m_t))

    gate_proj(2)
    for ch, (tot, m_t) in enumerate(stage2):
        bb, hh = ch // nh, ch % nh
        cols = slice(hh * d, (hh + 1) * d)
        denom = jnp.maximum(jnp.abs(tot[:, d:d + 1]), jnp.exp(-m_t))
        hout = tot[:, :d] / denom
        hout = hout * lax.rsqrt(jnp.mean(hout * hout, axis=-1, keepdims=True) + NORM_EPS) * nw_ref[...]
        o_ref[bb, :, cols] = (hout * _sigmoid(op_ref[bb, :, cols])).astype(o_ref.dtype)


def _mlstm_call(cproj, gi, gf, conv_w, conv_b, nw, h, w_gate, batch, seq):
    t = cproj.shape[0]
    d = ML_D
    nh = ML_HEADS
    L = ML_CHUNK
    nc = seq // L
    dm, n_gate = w_gate.shape
    cp3 = cproj.reshape(batch, seq, cproj.shape[1])
    spec = lambda off: pl.BlockSpec((batch, L, nh * d), lambda i: (0, i, off))
    gspec = pl.BlockSpec((batch, nh, nc, L), lambda i: (0, 0, 0, 0))
    out, pg = pl.pallas_call(
        _mlstm_kernel,
        grid=(nc,),
        in_specs=[spec(0), spec(1), spec(2), spec(3), gspec, gspec,
                  pl.BlockSpec((ML_CONV, nh * d), lambda i: (0, 0)),
                  pl.BlockSpec((ML_CONV, nh * d), lambda i: (0, 1)),
                  pl.BlockSpec((1, nh * d), lambda i: (0, 0)),
                  pl.BlockSpec((1, nh * d), lambda i: (0, 1)),
                  pl.BlockSpec((1, d), lambda i: (0, 0)),
                  pl.BlockSpec((batch, L, dm), lambda i: (0, i, 0)),
                  pl.BlockSpec((dm, n_gate), lambda i: (0, 0))],
        out_specs=[pl.BlockSpec((batch, L, nh * d), lambda i: (0, i, 0)),
                   pl.BlockSpec((batch, L, n_gate), lambda i: (0, i, 0))],
        out_shape=[jax.ShapeDtypeStruct((batch, seq, nh * d), BF16),
                   jax.ShapeDtypeStruct((batch, seq, n_gate), BF16)],
        scratch_shapes=[pltpu.VMEM((batch, L + 8, nh * d), F32),
                        pltpu.VMEM((batch, L + 8, nh * d), F32),
                        pltpu.VMEM((batch * nh, d, 2 * d), F32),
                        pltpu.VMEM((batch * nh, 1, 1), F32),
                        pltpu.VMEM((batch * nh, nc, L), F32)],
        compiler_params=_cparams(("arbitrary",)),
        name="mlstm",
    )(cp3, cp3, cp3, cp3, gi, gf, conv_w, conv_w, conv_b, conv_b, nw,
      h.reshape(batch, seq, dm), w_gate)
    return out.reshape(t, nh * d), pg.reshape(t, n_gate)


def _top2_combine(logits):
    lane = lax.broadcasted_iota(jnp.int32, logits.shape, 1)
    lg = jnp.where(lane < N_EXPERTS, logits, -jnp.inf)
    ex = jnp.exp(lg - jnp.max(lg, axis=1, keepdims=True))
    probs = ex / jnp.sum(ex, axis=1, keepdims=True)
    p1 = jnp.max(probs, axis=1, keepdims=True)
    i1 = jnp.min(jnp.where(probs == p1, lane, LANES), axis=1, keepdims=True)
    rest = jnp.where(lane == i1, -1.0, probs)
    p2 = jnp.max(rest, axis=1, keepdims=True)
    i2 = jnp.min(jnp.where(rest == p2, lane, LANES), axis=1, keepdims=True)
    comb = jnp.where(lane == i1, p1, 0.0) + jnp.where(lane == i2, p2, 0.0)
    return comb / (p1 + p2)


def _merge_kernel(*refs, route):
    if route:
        (ya_ref, yb_ref, yc_ref, gp_ref, wb_ref, wo_ref, x_ref, mod_ref, nw_ref,
         rw_ref, rb_ref, xo_ref, h_ref, cmb_ref) = refs
    else:
        (ya_ref, yb_ref, yc_ref, gp_ref, wb_ref, wo_ref, x_ref, mod_ref, nw_ref,
         xo_ref, h_ref) = refs
    d = x_ref.shape[1]
    merged = None
    for n, y_ref in enumerate((ya_ref, yb_ref, yc_ref)):
        gate = _sigmoid(gp_ref[:, n * d:(n + 1) * d].astype(F32))
        term = gate * _dot(y_ref[...], wb_ref[n])
        merged = term if merged is None else merged + term
    m = mod_ref[0]
    xn = x_ref[...] + m[2:3] * _dot(merged.astype(BF16), wo_ref[...])
    xo_ref[...] = xn
    h2 = _norm_mod(xn, nw_ref[...], m[4:5], m[3:4])
    h_ref[...] = h2.astype(BF16)
    if route:
        h_hi, h_mid, _ = _split3(h2)
        r_hi, r_mid, _ = _split3(rw_ref[...])
        logits = _dot(h_hi, r_hi) + _dot(h_mid, r_hi) + _dot(h_hi, r_mid) + rb_ref[...]
        cmb_ref[...] = _top2_combine(logits)


def _merge_call(ya, yb, yc, gp, wb, wo, x, mod, nw, router, seq, tm):
    t, d = x.shape
    bw = ya.shape[1]
    per_b = seq // tm
    route = router is not None
    tok = lambda w: pl.BlockSpec((tm, w), lambda i: (i, 0))
    const2 = lambda s: pl.BlockSpec(s, lambda i: (0, 0))
    in_specs = [tok(bw), tok(bw), tok(bw), tok(3 * d),
                pl.BlockSpec((3, bw, d), lambda i: (0, 0, 0)), const2((d, d)), tok(d),
                pl.BlockSpec((1, 6, d), lambda i: (i // per_b, 0, 0)), const2((1, d))]
    out_specs = [tok(d), tok(d)]
    out_shape = [jax.ShapeDtypeStruct((t, d), F32), jax.ShapeDtypeStruct((t, d), BF16)]
    args = [ya, yb, yc, gp, wb, wo, x, mod, nw]
    if route:
        in_specs += [const2((d, LANES)), const2((1, LANES))]
        out_specs.append(tok(LANES))
        out_shape.append(jax.ShapeDtypeStruct((t, LANES), F32))
        args += list(router)
    return pl.pallas_call(
        functools.partial(_merge_kernel, route=route),
        grid=(t // tm,),
        in_specs=in_specs, out_specs=out_specs, out_shape=out_shape,
        compiler_params=_cparams(("arbitrary",)),
        name="merge_out",
    )(*args)


def _finish(acc, x_ref, mod_ref, xo_ref, nxt):
    xn = x_ref[...] + mod_ref[0][5:6] * acc
    xo_ref[...] = xn
    if nxt is not None:
        nw_ref, modn_ref, hn_ref = nxt
        mn = modn_ref[0]
        hn_ref[...] = _norm_mod(xn, nw_ref[...], mn[1:2], mn[0:1]).astype(BF16)


def _ffn_kernel(*refs, has_next):
    if has_next:
        h_ref, w1_ref, w3_ref, w2_ref, x_ref, mod_ref, nw_ref, modn_ref, xo_ref, hn_ref, acc_ref = refs
        nxt = (nw_ref, modn_ref, hn_ref)
    else:
        h_ref, w1_ref, w3_ref, w2_ref, x_ref, mod_ref, xo_ref, acc_ref = refs
        nxt = None
    f = pl.program_id(1)

    @pl.when(f == 0)
    def _():
        acc_ref[...] = jnp.zeros(acc_ref.shape, F32)

    h = h_ref[...]
    a = _dot(h, w1_ref[...])
    act = a * _sigmoid(a) * _dot(h, w3_ref[...])
    acc_ref[...] += _dot(act.astype(BF16), w2_ref[...])

    @pl.when(f == pl.num_programs(1) - 1)
    def _():
        _finish(acc_ref[...], x_ref, mod_ref, xo_ref, nxt)


def _ffn_call(h, w1, w3, w2, x, mod, nxt, seq, tm, tf):
    t, d = x.shape
    ff = w1.shape[1]
    per_b = seq // tm
    tok = lambda: pl.BlockSpec((tm, d), lambda i, f: (i, 0))
    modspec = lambda: pl.BlockSpec((1, 6, d), lambda i, f: (i // per_b, 0, 0))
    in_specs = [tok(), pl.BlockSpec((d, tf), lambda i, f: (0, f)),
                pl.BlockSpec((d, tf), lambda i, f: (0, f)),
                pl.BlockSpec((tf, d), lambda i, f: (f, 0)), tok(), modspec()]
    out_specs = [tok()]
    out_shape = [jax.ShapeDtypeStruct((t, d), F32)]
    args = [h, w1, w3, w2, x, mod]
    if nxt is not None:
        in_specs += [pl.BlockSpec((1, d), lambda i, f: (0, 0)), modspec()]
        out_specs.append(tok())
        out_shape.append(jax.ShapeDtypeStruct((t, d), BF16))
        args += list(nxt)
    return pl.pallas_call(
        functools.partial(_ffn_kernel, has_next=nxt is not None),
        grid=(t // tm, ff // tf),
        in_specs=in_specs, out_specs=out_specs, out_shape=out_shape,
        scratch_shapes=[pltpu.VMEM((tm, d), F32)],
        compiler_params=_cparams(("arbitrary", "arbitrary")),
        name="ffn_swiglu",
    )(*args)


def _moe_kernel(h_ref, cmb_ref, w13_ref, w2_ref, y_ref,
                acc_ref, xg_ref, ya_ref, rk_ref, rkt_ref, cnt_ref):
    e = pl.program_id(1)
    f = pl.program_id(2)
    tm = h_ref.shape[0]

    @pl.when((e == 0) & (f == 0))
    def _():
        r = lax.broadcasted_iota(jnp.int32, (tm, tm), 0)
        c = lax.broadcasted_iota(jnp.int32, (tm, tm), 1)
        before = jnp.where(c < r, 1.0, 0.0).astype(BF16)
        sel = cmb_ref[...] > 0.0
        rank = _dot(before, jnp.where(sel, 1.0, 0.0).astype(BF16))
        rk = jnp.where(sel, rank, -1.0)
        rk_ref[...] = rk
        rkt_ref[...] = rk.T
        cnt_ref[...] = jnp.sum(jnp.where(sel, 1.0, 0.0), axis=0, keepdims=True)
        acc_ref[...] = jnp.zeros(acc_ref.shape, F32)

    lane1 = lax.broadcasted_iota(jnp.int32, (1, LANES), 1)
    n_e = jnp.sum(jnp.where(lane1 == e, cnt_ref[...], 0.0)).astype(jnp.int32)

    def loop(n_blocks, body):
        if isinstance(n_blocks, int):
            for sb in range(n_blocks):
                body(sb, 0)
        else:
            lax.fori_loop(0, n_blocks, body, 0)

    def run(cap, n_blocks):
        cap_pad = -(-cap // LANES) * LANES

        def row0(sb):
            return sb * cap if isinstance(sb, int) else pl.multiple_of(sb * cap, cap)

        @pl.when(f == 0)
        def _():
            rank_row = rkt_ref[pl.ds(e, 1), :]

            def gather(sb, carry):
                r0 = row0(sb)
                slot = (r0 + lax.broadcasted_iota(jnp.int32, (cap, tm), 0)).astype(F32)
                onehot = jnp.where(rank_row == slot, 1.0, 0.0).astype(BF16)
                xg_ref[pl.ds(r0, cap), :] = _dot(onehot, h_ref[...]).astype(BF16)
                return carry

            loop(n_blocks, gather)

        def expert(sb, carry):
            r0 = row0(sb)
            xs = xg_ref[pl.ds(r0, cap), :]
            ab = _dot(xs, w13_ref[0])
            tf = w2_ref.shape[1]
            a = ab[:, :tf]
            act = a * _sigmoid(a) * ab[:, tf:]
            part = _dot(act.astype(BF16), w2_ref[0])

            @pl.when(f == 0)
            def _():
                ya_ref[pl.ds(r0, cap), :] = part

            @pl.when(f > 0)
            def _():
                ya_ref[pl.ds(r0, cap), :] += part

            return carry

        loop(n_blocks, expert)

        @pl.when(f == pl.num_programs(2) - 1)
        def _():
            lane = lax.broadcasted_iota(jnp.int32, (tm, LANES), 1)
            rank_col = jnp.sum(jnp.where(lane == e, rk_ref[...], 0.0), axis=1, keepdims=True)
            w_col = jnp.sum(jnp.where(lane == e, cmb_ref[...], 0.0), axis=1, keepdims=True)
            col = lax.broadcasted_iota(jnp.int32, (tm, cap_pad), 1)

            def scatter(sb, carry):
                r0 = row0(sb)
                ys = ya_ref[pl.ds(r0, cap), :].astype(BF16)
                if cap_pad > cap:
                    ys = jnp.concatenate([ys, jnp.zeros((cap_pad - cap, ys.shape[1]), BF16)], axis=0)
                hit = (rank_col == (r0 + col).astype(F32)) & (col < cap)
                acc_ref[...] += w_col * _dot(jnp.where(hit, 1.0, 0.0).astype(BF16), ys)
                return carry

            loop(n_blocks, scatter)

    for idx, cap in enumerate(MOE_CAPS):
        lower = MOE_CAPS[idx - 1] if idx else 0
        if idx == len(MOE_CAPS) - 1:
            pl.when(n_e > lower)(functools.partial(run, cap, (n_e + cap - 1) // cap))
        else:
            pl.when((n_e > lower) & (n_e <= cap))(functools.partial(run, cap, 1))

    @pl.when((e == pl.num_programs(1) - 1) & (f == pl.num_programs(2) - 1))
    def _():
        y_ref[...] = acc_ref[...].astype(y_ref.dtype)


def _moe_pack_w13(w1, w3):
    ff = w1.shape[-1]
    tf = _pick(ff, (MOE_TF,))
    parts = []
    for f in range(ff // tf):
        parts += [w1[:, :, f * tf:(f + 1) * tf].astype(BF16), w3[:, :, f * tf:(f + 1) * tf].astype(BF16)]
    return jnp.concatenate(parts, axis=-1)


def _moe_call(h, cmb, w13, w2, tm):
    t, d = h.shape
    ne, ff, _ = w2.shape
    tf = _pick(ff, (MOE_TF,))
    rows = -(-tm // MOE_CAPS[-1]) * MOE_CAPS[-1]
    return pl.pallas_call(
        _moe_kernel,
        grid=(t // tm, ne, ff // tf),
        in_specs=[pl.BlockSpec((tm, d), lambda i, e, f: (i, 0)),
                  pl.BlockSpec((tm, LANES), lambda i, e, f: (i, 0)),
                  pl.BlockSpec((1, d, 2 * tf), lambda i, e, f: (e, 0, f)),
                  pl.BlockSpec((1, tf, d), lambda i, e, f: (e, f, 0))],
        out_specs=pl.BlockSpec((tm, d), lambda i, e, f: (i, 0)),
        out_shape=jax.ShapeDtypeStruct((t, d), BF16),
        scratch_shapes=[pltpu.VMEM((tm, d), F32),
                        pltpu.VMEM((rows, d), BF16),
                        pltpu.VMEM((rows, d), F32),
                        pltpu.VMEM((tm, LANES), F32),
                        pltpu.VMEM((LANES, tm), F32),
                        pltpu.VMEM((1, LANES), F32)],
        compiler_params=_cparams(("arbitrary", "arbitrary", "arbitrary")),
        name="moe_top2",
    )(h, cmb, w13, w2)


def _resid_kernel(*refs, has_next):
    if has_next:
        y_ref, x_ref, mod_ref, nw_ref, modn_ref, xo_ref, hn_ref = refs
        nxt = (nw_ref, modn_ref, hn_ref)
    else:
        y_ref, x_ref, mod_ref, xo_ref = refs
        nxt = None
    _finish(y_ref[...].astype(F32), x_ref, mod_ref, xo_ref, nxt)


def _resid_call(y, x, mod, nxt, seq, tm):
    t, d = x.shape
    per_b = seq // tm
    tok = lambda: pl.BlockSpec((tm, d), lambda i: (i, 0))
    modspec = lambda: pl.BlockSpec((1, 6, d), lambda i: (i // per_b, 0, 0))
    in_specs = [tok(), tok(), modspec()]
    out_specs = [tok()]
    out_shape = [jax.ShapeDtypeStruct((t, d), F32)]
    args = [y, x, mod]
    if nxt is not None:
        in_specs += [pl.BlockSpec((1, d), lambda i: (0, 0)), modspec()]
        out_specs.append(tok())
        out_shape.append(jax.ShapeDtypeStruct((t, d), BF16))
        args += list(nxt)
    return pl.pallas_call(
        functools.partial(_resid_kernel, has_next=nxt is not None),
        grid=(t // tm,),
        in_specs=in_specs, out_specs=out_specs, out_shape=out_shape,
        compiler_params=_cparams(("arbitrary",)),
        name="moe_residual",
    )(*args)


def _attn_col_perm():
    hq = DA_HEADS * DA_DK
    idx = []
    for base in (0, 2 * hq):
        for h in range(DA_HEADS):
            idx += list(range(base + h * DA_DK, base + (h + 1) * DA_DK))
            idx += list(range(base + hq + h * DA_DK, base + hq + (h + 1) * DA_DK))
    idx += list(range(4 * hq, 4 * hq + DA_HEADS * DA_DV))
    return np.asarray(idx, np.int32)


def _pick(n, pref):
    for c in pref:
        if n % c == 0:
            return c
    return n


def kernel(x, c, w_mod, b_mod, norm1_w, norm2_w, w_in, c_conv_w, c_conv_b, a_qnorm_w, a_knorm_w,
           a_lambda_q1, a_lambda_k1, a_lambda_q2, a_lambda_k2, a_subln_w, b_lb_logits, b_gnorm_w,
           c_igate_b, c_fgate_b, c_norm_w, w_branch, w_out, ffn_w1, ffn_w3, ffn_w2,
           moe_router_w, moe_router_b, moe_w1, moe_w3, moe_w2):
    batch, seq, d = x.shape
    depth = w_in.shape[0]
    t = batch * seq
    n_a = 4 * DA_HEADS * DA_DK + DA_HEADS * DA_DV
    n_b = 4 * HG_HEADS * HG_D
    n_c = 4 * ML_HEADS * ML_D + 2 * ML_HEADS
    n_c_pad = -(-n_c // LANES) * LANES
    tm = _pick(seq, (1024, 512, 256))
    tq = _pick(seq, (ATT_TQ,))
    ts_h = _pick(seq, (512, 256, 128))

    c8 = jnp.zeros((8, d), F32).at[:batch].set(c)
    mod = _mod_call(c8, w_mod, b_mod.reshape(depth, 1, 6 * d))[:, :batch].reshape(depth, batch, 6, d)

    lb_all = jnp.cumsum(jax.nn.softmax(b_lb_logits.astype(F32), axis=0), axis=0)
    lb_all = lb_all - lb_all[:1]
    slopes = jnp.asarray(_attn_slopes(), F32)

    xf = x.reshape(t, d)
    h = _normmod_call(xf, norm1_w[0].reshape(1, d), mod[0], seq, tm)
    perm_a = _attn_col_perm()
    w_bf = lax.optimization_barrier(w_in.astype(BF16))
    w_a_all = w_bf[:, :, :n_a][:, :, perm_a]
    w_b_all = w_bf[:, :, n_a:n_a + n_b]
    w_c_all = jnp.pad(w_bf[:, :, n_a + n_b:n_a + n_b + n_c], ((0, 0), (0, 0), (0, n_c_pad - n_c)))
    w_g_all = w_bf[:, :, n_a + n_b + n_c:]
    for l in range(depth):
        lam_init = 0.8 - 0.6 * math.exp(-0.3 * l)
        w_a, w_b, w_c, w_g = w_a_all[l], w_b_all[l], w_c_all[l], w_g_all[l]
        bias_c = jnp.zeros((1, n_c_pad), F32)
        bias_c = bias_c.at[0, n_c - 2 * ML_HEADS:n_c - ML_HEADS].set(c_igate_b[l])
        bias_c = bias_c.at[0, n_c - ML_HEADS:n_c].set(c_fgate_b[l])
        pb = _matmul(h, w_b, jnp.zeros((1, n_b), F32), F32, tm, n_b // 2)
        pc, pc_gates = _matmul_tail(h, w_c, bias_c, tm, n_c_pad - (n_c - 2 * ML_HEADS))

        qw = (jnp.tile(a_qnorm_w[l], 2 * DA_HEADS) * (DA_DK ** -0.5 * LOG2E)).reshape(1, -1)
        kw = jnp.tile(a_knorm_w[l], 2 * DA_HEADS).reshape(1, -1)
        kn, qt, vt = _attn_prep_call(h, w_a, qw, kw, batch, seq, _pick(seq, (512, 256)))
        lam = (jnp.exp(jnp.sum(a_lambda_q1[l] * a_lambda_k1[l]))
               - jnp.exp(jnp.sum(a_lambda_q2[l] * a_lambda_k2[l])) + lam_init)
        scal = jnp.concatenate([slopes, jnp.stack([lam, jnp.asarray(1.0 - lam_init, F32)])]).astype(F32)
        y_a = _attn_call(scal, qt, kn, vt, a_subln_w[l].reshape(1, -1), batch, seq, tq)

        lb = lb_all[l].reshape(HG_HEADS, 1, HG_D)
        lbs = jnp.concatenate([jnp.log(lb), jnp.log1p(-lb)], axis=1)
        y_b = _hgrn_call(pb, lbs, b_gnorm_w[l].reshape(1, -1), batch, seq, ts_h)

        gates = pc_gates[:, :2 * ML_HEADS].reshape(batch, seq, 2, ML_HEADS)
        gates = gates.transpose(2, 0, 3, 1).reshape(2, batch, ML_HEADS, seq // ML_CHUNK, ML_CHUNK)
        y_c, pg = _mlstm_call(pc, gates[0], gates[1], c_conv_w[l], c_conv_b[l].reshape(1, -1),
                              c_norm_w[l].reshape(1, -1), h, w_g, batch, seq)

        dense = l % 2 == 0
        router = None
        if not dense:
            rw = jnp.pad(moe_router_w[l // 2], ((0, 0), (0, LANES - N_EXPERTS)))
            rb = jnp.pad(moe_router_b[l // 2], (0, LANES - N_EXPERTS)).reshape(1, LANES)
            router = (rw, rb)
        outs = _merge_call(y_a, y_b, y_c, pg, w_branch[l].astype(BF16), w_out[l].astype(BF16), xf,
                           mod[l], norm2_w[l].reshape(1, d), router, seq, 512)
        xf, h2 = outs[0], outs[1]
        nxt = None if l == depth - 1 else (norm1_w[l + 1].reshape(1, d), mod[l + 1])
        if dense:
            res = _ffn_call(h2, ffn_w1[l // 2].astype(BF16), ffn_w3[l // 2].astype(BF16),
                            ffn_w2[l // 2].astype(BF16), xf, mod[l], nxt, seq, tm, FFN_TF)
        else:
            y = _moe_call(h2, outs[2], _moe_pack_w13(moe_w1[l // 2], moe_w3[l // 2]),
                          moe_w2[l // 2].astype(BF16), tm)
            res = _resid_call(y, xf, mod[l], nxt, seq, tm)
        xf = res[0]
        if nxt is not None:
            h = res[1]
    return xf.reshape(batch, seq, d)
```

```python
import functools
import math

import numpy as np
import jax
import jax.numpy as jnp
from jax import lax
from jax.experimental import pallas as pl
from jax.experimental.pallas import tpu as pltpu

F32 = jnp.float32
BF16 = jnp.bfloat16
NORM_EPS = 1e-6

CHUNK = 64
DA_HEADS, DA_DK, DA_DV = 4, 64, 128
HG_HEADS, HG_D = 4, 128
ML_HEADS, ML_D, ML_CONV = 4, 128, 4
N_EXPERTS = 8

LANES = 128
VMEM_LIMIT = 56 * 1024 * 1024

ATT_TQ = 256
ATT_HEADS_PER_STEP = 2
ATT_PAD_ROWS = 16
LOG2E = math.log2(math.e)
ATT_BOUND_MARGIN = 1.01
ATT_BOUND_EPS = 1e-3
ATT_BOUND_MAX = 40.0
ATT_ZERO_EXP = 160.0
HG_CHUNK = 128
HG_SUB = 32
HG_SAFE_DECAY = 80.0
ML_CHUNK = 256
MOE_CAPS = (256, 288, 320, 352, 384)
MOE_TF = 1408
FFN_TF = 256


def _cparams(sem, vmem=VMEM_LIMIT):
    return pltpu.CompilerParams(dimension_semantics=sem, vmem_limit_bytes=vmem)


def _dot(a, b):
    return jnp.dot(a, b, preferred_element_type=F32)


def _dot_nt(a, b):
    return lax.dot_general(a, b, (((1,), (1,)), ((), ())), preferred_element_type=F32)


def _split3(x):
    hi = x.astype(BF16)
    r1 = x - hi.astype(F32)
    mid = r1.astype(BF16)
    lo = (r1 - mid.astype(F32)).astype(BF16)
    return hi, mid, lo


def _norm_mod(x, w, scale, shift):
    y = x * lax.rsqrt(jnp.mean(x * x, axis=-1, keepdims=True) + NORM_EPS) * w
    return y * (1.0 + scale) + shift


def _sigmoid(x):
    return 0.5 * jnp.tanh(0.5 * x) + 0.5


def _log1p_exp_neg(z):
    return jnp.log(1.0 + jnp.exp(-z))


def _log_sigmoid(x):
    return jnp.minimum(x, 0.0) - _log1p_exp_neg(jnp.abs(x))


def _mod_kernel(c_ref, w_ref, b_ref, o_ref):
    cnd = c_ref[...]
    cnd = cnd * _sigmoid(cnd)
    o_ref[0] = _dot(cnd.astype(BF16), w_ref[0].astype(BF16)) + b_ref[0]


def _mod_call(c8, w_mod, b_mod):
    depth, d, n = w_mod.shape
    tn = 1536 if n % 1536 == 0 else n
    return pl.pallas_call(
        _mod_kernel,
        grid=(depth, n // tn),
        in_specs=[pl.BlockSpec((8, d), lambda l, j: (0, 0)),
                  pl.BlockSpec((1, d, tn), lambda l, j: (l, 0, j)),
                  pl.BlockSpec((1, 1, tn), lambda l, j: (l, 0, j))],
        out_specs=pl.BlockSpec((1, 8, tn), lambda l, j: (l, 0, j)),
        out_shape=jax.ShapeDtypeStruct((depth, 8, n), F32),
        compiler_params=_cparams(("arbitrary", "arbitrary")),
        name="adaln_mod",
    )(c8, w_mod, b_mod)


def _normmod_kernel(x_ref, nw_ref, mod_ref, h_ref):
    m = mod_ref[0]
    h_ref[...] = _norm_mod(x_ref[...], nw_ref[...], m[1:2], m[0:1]).astype(BF16)


def _normmod_call(x, nw, mod, seq, tm):
    t, d = x.shape
    per_b = seq // tm
    return pl.pallas_call(
        _normmod_kernel,
        grid=(t // tm,),
        in_specs=[pl.BlockSpec((tm, d), lambda i: (i, 0)),
                  pl.BlockSpec((1, d), lambda i: (0, 0)),
                  pl.BlockSpec((1, 6, d), lambda i: (i // per_b, 0, 0))],
        out_specs=pl.BlockSpec((tm, d), lambda i: (i, 0)),
        out_shape=jax.ShapeDtypeStruct((t, d), BF16),
        compiler_params=_cparams(("arbitrary",)),
        name="prenorm_mod",
    )(x, nw, mod)


def _mm_kernel(x_ref, w_ref, b_ref, o_ref):
    o_ref[...] = (_dot(x_ref[...], w_ref[...]) + b_ref[...]).astype(o_ref.dtype)


def _matmul(x, w, bias, out_dtype, tm, tn):
    t, k = x.shape
    n = w.shape[1]
    return pl.pallas_call(
        _mm_kernel,
        grid=(n // tn, t // tm),
        in_specs=[pl.BlockSpec((tm, k), lambda j, i: (i, 0)),
                  pl.BlockSpec((k, tn), lambda j, i: (0, j)),
                  pl.BlockSpec((1, tn), lambda j, i: (0, j))],
        out_specs=pl.BlockSpec((tm, tn), lambda j, i: (i, j)),
        out_shape=jax.ShapeDtypeStruct((t, n), out_dtype),
        compiler_params=_cparams(("arbitrary", "arbitrary")),
        name="in_proj",
    )(x, w, bias)


def _mm_tail_kernel(x_ref, w_ref, b_ref, o_ref, tail_ref):
    res = _dot(x_ref[...], w_ref[...]) + b_ref[...]
    main = o_ref.shape[1]
    o_ref[...] = res[:, :main]
    tail_ref[...] = res[:, main:]


def _matmul_tail(x, w, bias, tm, tail):
    t, k = x.shape
    n = w.shape[1]
    return pl.pallas_call(
        _mm_tail_kernel,
        grid=(t // tm,),
        in_specs=[pl.BlockSpec((tm, k), lambda i: (i, 0)),
                  pl.BlockSpec((k, n), lambda i: (0, 0)),
                  pl.BlockSpec((1, n), lambda i: (0, 0))],
        out_specs=[pl.BlockSpec((tm, n - tail), lambda i: (i, 0)),
                   pl.BlockSpec((tm, tail), lambda i: (i, 0))],
        out_shape=[jax.ShapeDtypeStruct((t, n - tail), F32),
                   jax.ShapeDtypeStruct((t, tail), F32)],
        compiler_params=_cparams(("arbitrary",)),
        name="in_proj_c",
    )(x, w, bias)


def _attn_prep_kernel(x_ref, w_ref, qw_ref, kw_ref, kn_ref, qt_ref, vt_ref):
    hw = DA_HEADS * 2 * DA_DK
    a = _dot(x_ref[...], w_ref[...])
    r = lax.broadcasted_iota(jnp.int32, (hw, hw), 0) // DA_DK
    c = lax.broadcasted_iota(jnp.int32, (hw, hw), 1) // DA_DK
    group = jnp.where(r == c, 1.0, 0.0).astype(BF16)

    def qk_norm(z, w):
        ms = _dot((z * z).astype(BF16), group) * (1.0 / DA_DK)
        return z * lax.rsqrt(ms + NORM_EPS) * w

    qn = qk_norm(a[:, :hw], qw_ref[...])
    kn = qk_norm(a[:, hw:2 * hw], kw_ref[...])
    kn_ref[...] = kn.astype(BF16)
    qt_ref[0] = qn.T.astype(BF16)
    vt_ref[0] = a[:, 2 * hw:].T.astype(BF16)


def _attn_prep_call(x, w, qw, kw, batch, seq, ts):
    t, d = x.shape
    hw = DA_HEADS * 2 * DA_DK
    hv = DA_HEADS * DA_DV
    per_b = seq // ts
    return pl.pallas_call(
        _attn_prep_kernel,
        grid=(t // ts,),
        in_specs=[pl.BlockSpec((ts, d), lambda i: (i, 0)),
                  pl.BlockSpec((d, 2 * hw + hv), lambda i: (0, 0)),
                  pl.BlockSpec((1, hw), lambda i: (0, 0)),
                  pl.BlockSpec((1, hw), lambda i: (0, 0))],
        out_specs=[pl.BlockSpec((ts, hw), lambda i: (i, 0)),
                   pl.BlockSpec((1, hw, ts), lambda i: (i // per_b, 0, i % per_b)),
                   pl.BlockSpec((1, hv, ts), lambda i: (i // per_b, 0, i % per_b))],
        out_shape=[jax.ShapeDtypeStruct((t, hw), BF16),
                   jax.ShapeDtypeStruct((batch, hw, seq), BF16),
                   jax.ShapeDtypeStruct((batch, hv, seq), BF16)],
        compiler_params=_cparams(("arbitrary",)),
        name="attn_prep",
    )(x, w, qw, kw)


def _attn_slopes():
    return [LOG2E * 2.0 ** (-8.0 * (h + 1) / DA_HEADS) for h in range(DA_HEADS)]


def _attn_window(head, tq):
    return int((ATT_ZERO_EXP / _attn_slopes()[head] - 1.0) // tq) + 1


def _attn_kernel(sc_ref, qt_ref, k_ref, vt_ref, sw_ref, o_ref,
                 qz_ref, m_ref, acc_ref, vs_ref, kmax_ref, s_ref, s2_ref, *, tq, seq):
    hp = ATT_HEADS_PER_STEP
    dv = DA_DV
    g = pl.program_id(1)
    i = pl.program_id(2)
    lam = sc_ref[DA_HEADS]
    out_scale = sc_ref[DA_HEADS + 1]
    slopes = [sc_ref[hp * g + hh] for hh in range(hp)]
    row16 = lax.broadcasted_iota(jnp.int32, (ATT_PAD_ROWS, tq), 0)
    ones_rows = jnp.where(row16 == 0, 1.0, 0.0).astype(BF16)

    half = lax.broadcasted_iota(jnp.int32, (1, 2 * DA_DK), 1) < DA_DK

    @pl.when(i == 0)
    def _():
        pos = lax.broadcasted_iota(jnp.int32, (1, seq), 1) & (tq - 1)
        rel = (pos - (tq - 1)).astype(F32)
        sub16 = lax.broadcasted_iota(jnp.int32, (ATT_PAD_ROWS, seq), 0)
        gr = lax.broadcasted_iota(jnp.int32, (2 * DA_DK, 2 * DA_DK), 0) // DA_DK
        gc = lax.broadcasted_iota(jnp.int32, (2 * DA_DK, 2 * DA_DK), 1) // DA_DK
        group = jnp.where(gr == gc, 1.0, 0.0).astype(BF16)
        for hh in range(hp):
            w = jnp.exp2(slopes[hh] * rel)
            vs_ref[hh, 0:dv, :] = (vt_ref[0, hh * dv:(hh + 1) * dv, :].astype(F32) * w).astype(BF16)
            vs_ref[hh, dv:dv + ATT_PAD_ROWS, :] = jnp.where(sub16 == 0, w, 0.0).astype(BF16)

            def knorm(n, best):
                kc = k_ref[pl.ds(pl.multiple_of(n * tq, tq), tq),
                           hh * 2 * DA_DK:(hh + 1) * 2 * DA_DK].astype(F32)
                return jnp.maximum(best, jnp.max(_dot((kc * kc).astype(BF16), group), axis=0, keepdims=True))

            k2 = lax.fori_loop(0, seq // tq, knorm, jnp.zeros((1, 2 * DA_DK), F32))
            kmax_ref[2 * hh] = jnp.max(jnp.where(half, k2, 0.0), axis=1, keepdims=True)
            kmax_ref[2 * hh + 1] = jnp.max(jnp.where(half, 0.0, k2), axis=1, keepdims=True)

    row = lax.broadcasted_iota(jnp.int32, (2 * DA_DK, tq), 0)
    qq_row = lax.broadcasted_iota(jnp.int32, (1, tq), 1).astype(F32)
    bound_max = jnp.zeros((), F32)
    firsts = []

    def scores(hh, blk):
        start = pl.multiple_of(blk * tq, tq)
        return _dot(k_ref[pl.ds(start, tq), hh * 2 * DA_DK:(hh + 1) * 2 * DA_DK], qz_ref[hh])

    for hh in range(hp):
        qt = qt_ref[0, hh * 2 * DA_DK:(hh + 1) * 2 * DA_DK, :]
        zero = jnp.zeros_like(qt)
        q1 = jnp.where(row < DA_DK, qt, zero)
        q2 = jnp.where(row >= DA_DK, qt, zero)
        qz_ref[hh] = jnp.concatenate([q1, q2], axis=1)
        window = jnp.int32(_attn_window(hh, tq))
        for grp in range(1, DA_HEADS // hp):
            window = jnp.where(g == grp, _attn_window(grp * hp + hh, tq), window)
        firsts.append((jnp.maximum(i - window, 0) // 2) * 2)
        s_ref[hh] = scores(hh, firsts[hh])
        s2_ref[hh] = scores(hh, jnp.minimum(firsts[hh] + 1, i))
        qn = jnp.concatenate(
            [jnp.sum(jnp.square(q1.astype(F32)), axis=0, keepdims=True) * kmax_ref[2 * hh],
             jnp.sum(jnp.square(q2.astype(F32)), axis=0, keepdims=True) * kmax_ref[2 * hh + 1]], axis=1)
        bound = jnp.sqrt(qn) * ATT_BOUND_MARGIN + ATT_BOUND_EPS
        bound_max = jnp.maximum(bound_max, jnp.max(bound))
        m_ref[hh] = bound + slopes[hh] * jnp.concatenate([qq_row, qq_row], axis=1)
    acc_ref[...] = jnp.zeros(acc_ref.shape, F32)

    kk = lax.broadcasted_iota(jnp.int32, (tq, 2 * tq), 0)
    cc = lax.broadcasted_iota(jnp.int32, (tq, 2 * tq), 1)
    qq = jnp.where(cc >= tq, cc - tq, cc)
    dist = (qq - jnp.abs(qq - kk)).astype(F32)
    visible = (kk // CHUNK) <= (qq // CHUNK)

    def diag_values(hh):
        start = pl.multiple_of(i * tq, tq)
        return jnp.concatenate([vt_ref[0, hh * dv:(hh + 1) * dv, pl.ds(start, tq)], ones_rows], axis=0)

    @pl.when(bound_max <= ATT_BOUND_MAX)
    def _():
        def probs(hh, blk, s):
            last_key = ((blk + 1 - i) * tq - 1).astype(F32)
            return jnp.exp2(s - (m_ref[hh] - slopes[hh] * last_key)).astype(BF16)

        def body(heads, jj, carry):
            blk = 2 * jj
            start = pl.multiple_of(blk * tq, 2 * tq)
            for hh in heads:
                p = jnp.concatenate([probs(hh, blk, s_ref[hh]), probs(hh, blk + 1, s2_ref[hh])], axis=0)
                s_ref[hh] = scores(hh, jnp.minimum(blk + 2, i))
                s2_ref[hh] = scores(hh, jnp.minimum(blk + 3, i))
                acc_ref[hh] += _dot(vs_ref[hh, :, pl.ds(start, 2 * tq)], p)
            return carry

        for hh in range(hp - 1, 0, -1):
            lax.fori_loop(firsts[hh] // 2, firsts[hh - 1] // 2,
                          functools.partial(body, tuple(range(hh, hp))), 0)
        lax.fori_loop(firsts[0] // 2, i // 2, functools.partial(body, tuple(range(hp))), 0)

        @pl.when(i % 2 == 1)
        def _():
            start = pl.multiple_of((i - 1) * tq, tq)
            for hh in range(hp):
                acc_ref[hh] += _dot(vs_ref[hh, :, pl.ds(start, tq)], probs(hh, i - 1, s_ref[hh]))
                s_ref[hh] = s2_ref[hh]

        for hh in range(hp):
            s = jnp.where(visible, s_ref[hh] + slopes[hh] * dist, -jnp.inf)
            acc_ref[hh] += _dot(diag_values(hh), jnp.exp2(s - m_ref[hh]).astype(BF16))

    @pl.when(bound_max > ATT_BOUND_MAX)
    def _():
        m_ref[...] = jnp.full(m_ref.shape, -jnp.inf, F32)

        def update(hh, s, bound_shift, v_aug):
            m_old = m_ref[hh]
            m_new = jnp.maximum(m_old, jnp.max(s, axis=0, keepdims=True) + bound_shift)
            p = jnp.exp2(s - (m_new - bound_shift))
            acc_ref[hh] = jnp.exp2(m_old - m_new) * acc_ref[hh] + _dot(v_aug, p.astype(BF16))
            m_ref[hh] = m_new

        def body(j, carry):
            start = pl.multiple_of(j * tq, tq)
            last_key = ((j + 1 - i) * tq - 1).astype(F32)
            for hh in range(hp):
                update(hh, scores(hh, j), slopes[hh] * last_key, vs_ref[hh, :, pl.ds(start, tq)])
            return carry

        lax.fori_loop(0, i, body, 0)
        for hh in range(hp):
            s = jnp.where(visible, scores(hh, i) + slopes[hh] * dist, -jnp.inf)
            update(hh, s, jnp.zeros((), F32), diag_values(hh))

    outs = []
    for hh in range(hp):
        acc = acc_ref[hh]
        o2 = acc[:dv, :] * (1.0 / acc[dv:dv + 1, :])
        o = (o2[:, :tq] - lam * o2[:, tq:]).T
        o = o * lax.rsqrt(jnp.mean(o * o, axis=-1, keepdims=True) + NORM_EPS) * sw_ref[...]
        outs.append(o * out_scale)
    o_ref[...] = jnp.concatenate(outs, axis=1).astype(o_ref.dtype)


def _attn_call(scal, qt, kn, vt, sw, batch, seq, tq):
    t = kn.shape[0]
    nq = seq // tq
    hp = ATT_HEADS_PER_STEP
    return pl.pallas_call(
        functools.partial(_attn_kernel, tq=tq, seq=seq),
        grid=(batch, DA_HEADS // hp, nq),
        in_specs=[pl.BlockSpec(memory_space=pltpu.SMEM),
                  pl.BlockSpec((1, hp * 2 * DA_DK, tq), lambda b, g, i: (b, g, i)),
                  pl.BlockSpec((seq, hp * 2 * DA_DK), lambda b, g, i: (b, g)),
                  pl.BlockSpec((1, hp * DA_DV, seq), lambda b, g, i: (b, g, 0)),
                  pl.BlockSpec((1, DA_DV), lambda b, g, i: (0, 0))],
        out_specs=pl.BlockSpec((tq, hp * DA_DV), lambda b, g, i: (b * nq + i, g)),
        out_shape=jax.ShapeDtypeStruct((t, DA_HEADS * DA_DV), BF16),
        scratch_shapes=[pltpu.VMEM((hp, 2 * DA_DK, 2 * tq), BF16),
                        pltpu.VMEM((hp, 1, 2 * tq), F32),
                        pltpu.VMEM((hp, DA_DV + ATT_PAD_ROWS, 2 * tq), F32),
                        pltpu.VMEM((hp, DA_DV + ATT_PAD_ROWS, seq), BF16),
                        pltpu.VMEM((2 * hp, 1, 1), F32),
                        pltpu.VMEM((hp, tq, 2 * tq), F32),
                        pltpu.VMEM((hp, tq, 2 * tq), F32)],
        compiler_params=_cparams(("arbitrary", "arbitrary", "arbitrary")),
        name="diff_attn",
    )(scal, qt, kn, vt, sw)


def _hgrn_kernel(q_ref, f_ref, i_ref, g_ref, lb_ref, gw_ref, hp_ref, wp_ref, bp_ref,
                 o_ref, pm_ref, pt_ref,
                 st_ref, b_ref, qs_ref, ks_ref, oi_ref, *, n_chunks):
    c = HG_CHUNK
    sub = HG_SUB
    d = HG_D
    nh = HG_HEADS
    nb = q_ref.shape[0]
    chains = [(bb, hh) for bb in range(nb) for hh in range(nh)]
    n_main = pm_ref.shape[2]
    n_proj = wp_ref.shape[1]
    proj_split = (n_proj // 2) // LANES * LANES

    def host_proj(r0, lo, hi):
        rows = jnp.concatenate([hp_ref[bb, pl.ds(r0, c), :] for bb in range(nb)], axis=0)
        res = _dot(rows, wp_ref[:, lo:hi]) + bp_ref[:, lo:hi]
        for bb in range(nb):
            blk = res[bb * c:(bb + 1) * c, :]
            if hi <= n_main:
                pm_ref[bb, pl.ds(r0, c), lo:hi] = blk
            else:
                pm_ref[bb, pl.ds(r0, c), lo:n_main] = blk[:, :n_main - lo]
                pt_ref[bb, pl.ds(r0, c), :] = blk[:, n_main - lo:]

    @pl.when(pl.program_id(0) == 0)
    def _():
        st_ref[...] = jnp.zeros(st_ref.shape, F32)

    gw = gw_ref[...]
    rr = lax.broadcasted_iota(jnp.int32, (c, c), 0)
    cc = lax.broadcasted_iota(jnp.int32, (c, c), 1)
    causal = cc <= rr
    tril = jnp.where(causal, 1.0, 0.0).astype(BF16)

    def chunk(n, carry):
        r0 = pl.multiple_of(n * c, c)
        host_proj(r0, 0, proj_split)
        gates = []
        for ch, (bb, hh) in enumerate(chains):
            cols = slice(hh * d, (hh + 1) * d)
            log_lb = lb_ref[hh, 0:1, :]
            q = q_ref[bb, pl.ds(r0, c), cols]
            q = q * _sigmoid(q)
            a = lb_ref[hh, 1:2, :] + _log_sigmoid(f_ref[bb, pl.ds(r0, c), cols])
            logf = jnp.maximum(log_lb, a) + _log1p_exp_neg(jnp.abs(log_lb - a))
            k = 1.0 - jnp.exp(logf)
            v = i_ref[bb, pl.ds(r0, c), cols]
            hi, mid, lo = _split3(logf)
            b = _dot(tril, hi) + _dot(tril, mid) + _dot(tril, lo)
            gates.append((q, k, v, b))
        host_proj(r0, proj_split, n_proj)

        heads = []
        decay = jnp.zeros((), F32)
        for ch, (q, k, v, b) in enumerate(gates):
            b_last = b[c - 1:c, :]
            st = st_ref[ch]
            o_inter = _dot_nt((q * jnp.exp(b)).astype(BF16), st.astype(BF16))
            k_hat = k * jnp.exp(b_last - b)
            st_ref[ch] = st * jnp.exp(b_last) + _dot(v.T.astype(BF16), k_hat.astype(BF16))
            betas = []
            for blk in range(c // sub):
                beta = jnp.zeros((1, d), F32) if blk == 0 else b[blk * sub - 1:blk * sub, :]
                betas.append(beta)
                b_end = b[(blk + 1) * sub - 1:(blk + 1) * sub, :]
                decay = jnp.maximum(decay, jnp.max(beta - b_end))
            heads.append((q, k, v, b, betas, o_inter))

        @pl.when(decay < HG_SAFE_DECAY)
        def _():
            atts = []
            for q, k, v, b, betas, _ in heads:
                rows = []
                for blk in range(c // sub):
                    beta = betas[blk]
                    q_t = q[blk * sub:(blk + 1) * sub, :] * jnp.exp(b[blk * sub:(blk + 1) * sub, :] - beta)
                    k_t = k * jnp.exp(jnp.minimum(beta - b, HG_SAFE_DECAY))
                    rows.append(_dot_nt(q_t.astype(BF16), k_t.astype(BF16)))
                atts.append(jnp.where(causal, jnp.concatenate(rows, axis=0), 0.0).astype(BF16))
            for ch, att in enumerate(atts):
                oi_ref[ch] = _dot(att, heads[ch][2].astype(BF16))

        @pl.when(decay >= HG_SAFE_DECAY)
        def _():
            ridx = lax.broadcasted_iota(jnp.int32, (c, d), 0)
            for ch, (q, k, v, b, _, _) in enumerate(heads):
                b_ref[...] = b
                qs_ref[...] = q
                ks_ref[...] = k

                def row(t, carry2):
                    bt = b_ref[pl.ds(t, 1), :]
                    e = jnp.exp(jnp.where(ridx <= t, bt - b_ref[...], -jnp.inf))
                    w = jnp.sum(qs_ref[pl.ds(t, 1), :] * ks_ref[...] * e, axis=1, keepdims=True)
                    oi_ref[ch, pl.ds(t, 1), :] = jnp.sum(w * v, axis=0, keepdims=True)
                    return carry2

                lax.fori_loop(0, c, row, 0)

        for ch, (bb, hh) in enumerate(chains):
            cols = slice(hh * d, (hh + 1) * d)
            o = heads[ch][5] + oi_ref[ch]
            o = o * lax.rsqrt(jnp.mean(o * o, axis=-1, keepdims=True) + NORM_EPS) * gw
            g = g_ref[bb, pl.ds(r0, c), cols]
            o_ref[bb, pl.ds(r0, c), cols] = (o * (g * _sigmoid(g))).astype(o_ref.dtype)
        return carry

    lax.fori_loop(0, n_chunks, chunk, 0)


def _hgrn_call(bproj, lbs, gw, h, w_proj, b_proj, tail, batch, seq, ts):
    t = bproj.shape[0]
    d = HG_D
    nh = HG_HEADS
    dm, n_proj = w_proj.shape
    bp3 = bproj.reshape(batch, seq, bproj.shape[1])
    spec = lambda off: pl.BlockSpec((batch, ts, nh * d), lambda i: (0, i, off))
    out, pm, pt = pl.pallas_call(
        functools.partial(_hgrn_kernel, n_chunks=ts // HG_CHUNK),
        grid=(seq // ts,),
        in_specs=[spec(0), spec(1), spec(2), spec(3),
                  pl.BlockSpec((nh, 2, d), lambda i: (0, 0, 0)),
                  pl.BlockSpec((1, d), lambda i: (0, 0)),
                  pl.BlockSpec((batch, ts, dm), lambda i: (0, i, 0)),
                  pl.BlockSpec((dm, n_proj), lambda i: (0, 0)),
                  pl.BlockSpec((1, n_proj), lambda i: (0, 0))],
        out_specs=[pl.BlockSpec((batch, ts, nh * d), lambda i: (0, i, 0)),
                   pl.BlockSpec((batch, ts, n_proj - tail), lambda i: (0, i, 0)),
                   pl.BlockSpec((batch, ts, tail), lambda i: (0, i, 0))],
        out_shape=[jax.ShapeDtypeStruct((batch, seq, nh * d), BF16),
                   jax.ShapeDtypeStruct((batch, seq, n_proj - tail), F32),
                   jax.ShapeDtypeStruct((batch, seq, tail), F32)],
        scratch_shapes=[pltpu.VMEM((batch * nh, d, d), F32),
                        pltpu.VMEM((HG_CHUNK, d), F32),
                        pltpu.VMEM((HG_CHUNK, d), F32),
                        pltpu.VMEM((HG_CHUNK, d), F32),
                        pltpu.VMEM((batch * nh, HG_CHUNK, d), F32)],
        compiler_params=_cparams(("arbitrary",)),
        name="hgrn2",
    )(bp3, bp3, bp3, bp3, lbs, gw, h.reshape(batch, seq, dm), w_proj, b_proj)
    return out.reshape(t, nh * d), pm.reshape(t, n_proj - tail), pt.reshape(t, tail)


def _mlstm_kernel(uq_ref, uk_ref, v_ref, op_ref, gi_ref, gf_ref, cwq_ref, cwk_ref,
                  cbq_ref, cbk_ref, nw_ref, hg_ref, wg_ref, o_ref, pg_ref,
                  xq_ref, xk_ref, c_ref, m_ref, bs_ref):
    L = ML_CHUNK
    n_gate = wg_ref.shape[1]

    def gate_proj(part):
        cols = slice(part * n_gate // 3, (part + 1) * n_gate // 3)
        for bb in range(hg_ref.shape[0]):
            pg_ref[bb, :, cols] = _dot(hg_ref[bb], wg_ref[:, cols]).astype(pg_ref.dtype)

    d = ML_D
    nh = ML_HEADS
    nb = uq_ref.shape[0]
    i = pl.program_id(0)
    rr = lax.broadcasted_iota(jnp.int32, (L, L), 0)
    cc = lax.broadcasted_iota(jnp.int32, (L, L), 1)

    @pl.when(i == 0)
    def _():
        c_ref[...] = jnp.zeros(c_ref.shape, F32)
        m_ref[...] = jnp.zeros(m_ref.shape, F32)
        xq_ref[:, 0:8, :] = jnp.zeros((nb, 8, nh * d), F32)
        xk_ref[:, 0:8, :] = jnp.zeros((nb, 8, nh * d), F32)
        upper = jnp.where(rr <= cc, 1.0, 0.0).astype(BF16)
        for ch in range(nb * nh):
            hi, mid, lo = _split3(_log_sigmoid(gf_ref[ch // nh, ch % nh]))
            bs_ref[ch] = _dot(hi, upper) + _dot(mid, upper) + _dot(lo, upper)

    def conv_silu(bb, u_ref, x_ref, w_ref, b_ref):
        x_ref[bb, 8:8 + L, :] = u_ref[bb]
        y = b_ref[...] + w_ref[ML_CONV - 1:ML_CONV, :] * x_ref[bb, 8:8 + L, :]
        for j in range(ML_CONV - 1):
            y = y + w_ref[j:j + 1, :] * x_ref[bb, 5 + j:5 + j + L, :]
        x_ref[bb, 0:8, :] = x_ref[bb, L:L + 8, :]
        return y * _sigmoid(y)

    gate_proj(0)
    q_all = [conv_silu(bb, uq_ref, xq_ref, cwq_ref, cbq_ref) for bb in range(nb)]
    k_all = [conv_silu(bb, uk_ref, xk_ref, cwk_ref, cbk_ref) * (d ** -0.5) for bb in range(nb)]
    lane = lax.broadcasted_iota(jnp.int32, (L, d), 1)
    ones_col = jnp.where(lane == 0, 1.0, 0.0)
    stage1 = []
    for ch in range(nb * nh):
        bb, hh = ch // nh, ch % nh
        cols = slice(hh * d, (hh + 1) * d)
        q = q_all[bb][:, cols].astype(BF16)
        kt = k_all[bb][:, cols].T
        v_aug = jnp.concatenate([v_ref[bb, :, cols], ones_col], axis=1).astype(BF16)
        c_aug = c_ref[ch]
        qk = _dot(q, kt.astype(BF16))
        qc = _dot(q, c_aug.astype(BF16))

        b_row = bs_ref[ch, pl.ds(i, 1), :]
        ig_row = gi_ref[bb, hh, pl.ds(i, 1), :]
        m_prev = m_ref[ch]
        g = b_row[:, L - 1:L]
        log_w = g - b_row + ig_row
        m_new = jnp.maximum(g + m_prev, jnp.max(log_w, axis=1, keepdims=True))
        w_s = jnp.exp(log_w - m_new)
        c_ref[ch] = jnp.exp(g + m_prev - m_new) * c_aug + _dot((kt * w_s).astype(BF16), v_aug)
        m_ref[ch] = m_new
        stage1.append((qk, qc, v_aug, b_row, ig_row, m_prev))

    gate_proj(1)
    stage2 = []
    for qk, qc, v_aug, b_row, ig_row, m_prev in stage1:
        b_col = jnp.sum(jnp.where(rr == cc, b_row, 0.0), axis=1, keepdims=True)
        log_d = jnp.where(cc <= rr, b_col + (ig_row - b_row), -jnp.inf)
        log_inter = b_col + m_prev
        m_t = jnp.maximum(log_inter, jnp.max(log_d, axis=1, keepdims=True))
        w_intra = jnp.exp(log_d - m_t) * qk
        tot = jnp.exp(log_inter - m_t) * qc + _dot(w_intra.astype(BF16), v_aug)
        stage2.append((tot, m_t))

    gate_proj(2)
    for ch, (tot, m_t) in enumerate(stage2):
        bb, hh = ch // nh, ch % nh
        cols = slice(hh * d, (hh + 1) * d)
        denom = jnp.maximum(jnp.abs(tot[:, d:d + 1]), jnp.exp(-m_t))
        hout = tot[:, :d] / denom
        hout = hout * lax.rsqrt(jnp.mean(hout * hout, axis=-1, keepdims=True) + NORM_EPS) * nw_ref[...]
        o_ref[bb, :, cols] = (hout * _sigmoid(op_ref[bb, :, cols])).astype(o_ref.dtype)


def _mlstm_call(cproj, gi, gf, conv_w, conv_b, nw, h, w_gate, batch, seq):
    t = cproj.shape[0]
    d = ML_D
    nh = ML_HEADS
    L = ML_CHUNK
    nc = seq // L
    dm, n_gate = w_gate.shape
    cp3 = cproj.reshape(batch, seq, cproj.shape[1])
    spec = lambda off: pl.BlockSpec((batch, L, nh * d), lambda i: (0, i, off))
    gspec = pl.BlockSpec((batch, nh, nc, L), lambda i: (0, 0, 0, 0))
    out, pg = pl.pallas_call(
        _mlstm_kernel,
        grid=(nc,),
        in_specs=[spec(0), spec(1), spec(2), spec(3), gspec, gspec,
                  pl.BlockSpec((ML_CONV, nh * d), lambda i: (0, 0)),
                  pl.BlockSpec((ML_CONV, nh * d), lambda i: (0, 1)),
                  pl.BlockSpec((1, nh * d), lambda i: (0, 0)),
                  pl.BlockSpec((1, nh * d), lambda i: (0, 1)),
                  pl.BlockSpec((1, d), lambda i: (0, 0)),
                  pl.BlockSpec((batch, L, dm), lambda i: (0, i, 0)),
                  pl.BlockSpec((dm, n_gate), lambda i: (0, 0))],
        out_specs=[pl.BlockSpec((batch, L, nh * d), lambda i: (0, i, 0)),
                   pl.BlockSpec((batch, L, n_gate), lambda i: (0, i, 0))],
        out_shape=[jax.ShapeDtypeStruct((batch, seq, nh * d), BF16),
                   jax.ShapeDtypeStruct((batch, seq, n_gate), BF16)],
        scratch_shapes=[pltpu.VMEM((batch, L + 8, nh * d), F32),
                        pltpu.VMEM((batch, L + 8, nh * d), F32),
                        pltpu.VMEM((batch * nh, d, 2 * d), F32),
                        pltpu.VMEM((batch * nh, 1, 1), F32),
                        pltpu.VMEM((batch * nh, nc, L), F32)],
        compiler_params=_cparams(("arbitrary",)),
        name="mlstm",
    )(cp3, cp3, cp3, cp3, gi, gf, conv_w, conv_w, conv_b, conv_b, nw,
      h.reshape(batch, seq, dm), w_gate)
    return out.reshape(t, nh * d), pg.reshape(t, n_gate)


def _top2_combine(logits):
    lane = lax.broadcasted_iota(jnp.int32, logits.shape, 1)
    lg = jnp.where(lane < N_EXPERTS, logits, -jnp.inf)
    ex = jnp.exp(lg - jnp.max(lg, axis=1, keepdims=True))
    probs = ex / jnp.sum(ex, axis=1, keepdims=True)
    p1 = jnp.max(probs, axis=1, keepdims=True)
    i1 = jnp.min(jnp.where(probs == p1, lane, LANES), axis=1, keepdims=True)
    rest = jnp.where(lane == i1, -1.0, probs)
    p2 = jnp.max(rest, axis=1, keepdims=True)
    i2 = jnp.min(jnp.where(rest == p2, lane, LANES), axis=1, keepdims=True)
    comb = jnp.where(lane == i1, p1, 0.0) + jnp.where(lane == i2, p2, 0.0)
    return comb / (p1 + p2)


def _merge_kernel(*refs, route):
    if route:
        (ya_ref, yb_ref, yc_ref, gp_ref, wb_ref, wo_ref, x_ref, mod_ref, nw_ref,
         rw_ref, rb_ref, xo_ref, h_ref, cmb_ref) = refs
    else:
        (ya_ref, yb_ref, yc_ref, gp_ref, wb_ref, wo_ref, x_ref, mod_ref, nw_ref,
         xo_ref, h_ref) = refs
    d = x_ref.shape[1]
    merged = None
    for n, y_ref in enumerate((ya_ref, yb_ref, yc_ref)):
        gate = _sigmoid(gp_ref[:, n * d:(n + 1) * d].astype(F32))
        term = gate * _dot(y_ref[...], wb_ref[n])
        merged = term if merged is None else merged + term
    m = mod_ref[0]
    xn = x_ref[...] + m[2:3] * _dot(merged.astype(BF16), wo_ref[...])
    xo_ref[...] = xn
    h2 = _norm_mod(xn, nw_ref[...], m[4:5], m[3:4])
    h_ref[...] = h2.astype(BF16)
    if route:
        h_hi, h_mid, _ = _split3(h2)
        r_hi, r_mid, _ = _split3(rw_ref[...])
        logits = _dot(h_hi, r_hi) + _dot(h_mid, r_hi) + _dot(h_hi, r_mid) + rb_ref[...]
        cmb_ref[...] = _top2_combine(logits)


def _merge_call(ya, yb, yc, gp, wb, wo, x, mod, nw, router, seq, tm):
    t, d = x.shape
    bw = ya.shape[1]
    per_b = seq // tm
    route = router is not None
    tok = lambda w: pl.BlockSpec((tm, w), lambda i: (i, 0))
    const2 = lambda s: pl.BlockSpec(s, lambda i: (0, 0))
    in_specs = [tok(bw), tok(bw), tok(bw), tok(3 * d),
                pl.BlockSpec((3, bw, d), lambda i: (0, 0, 0)), const2((d, d)), tok(d),
                pl.BlockSpec((1, 6, d), lambda i: (i // per_b, 0, 0)), const2((1, d))]
    out_specs = [tok(d), tok(d)]
    out_shape = [jax.ShapeDtypeStruct((t, d), F32), jax.ShapeDtypeStruct((t, d), BF16)]
    args = [ya, yb, yc, gp, wb, wo, x, mod, nw]
    if route:
        in_specs += [const2((d, LANES)), const2((1, LANES))]
        out_specs.append(tok(LANES))
        out_shape.append(jax.ShapeDtypeStruct((t, LANES), F32))
        args += list(router)
    return pl.pallas_call(
        functools.partial(_merge_kernel, route=route),
        grid=(t // tm,),
        in_specs=in_specs, out_specs=out_specs, out_shape=out_shape,
        compiler_params=_cparams(("arbitrary",)),
        name="merge_out",
    )(*args)


def _finish(acc, x_ref, mod_ref, xo_ref, nxt):
    xn = x_ref[...] + mod_ref[0][5:6] * acc
    xo_ref[...] = xn
    if nxt is not None:
        nw_ref, modn_ref, hn_ref = nxt
        mn = modn_ref[0]
        hn_ref[...] = _norm_mod(xn, nw_ref[...], mn[1:2], mn[0:1]).astype(BF16)


def _ffn_kernel(*refs, has_next):
    if has_next:
        h_ref, w1_ref, w3_ref, w2_ref, x_ref, mod_ref, nw_ref, modn_ref, xo_ref, hn_ref, acc_ref = refs
        nxt = (nw_ref, modn_ref, hn_ref)
    else:
        h_ref, w1_ref, w3_ref, w2_ref, x_ref, mod_ref, xo_ref, acc_ref = refs
        nxt = None
    f = pl.program_id(1)

    @pl.when(f == 0)
    def _():
        acc_ref[...] = jnp.zeros(acc_ref.shape, F32)

    h = h_ref[...]
    a = _dot(h, w1_ref[...])
    act = a * _sigmoid(a) * _dot(h, w3_ref[...])
    acc_ref[...] += _dot(act.astype(BF16), w2_ref[...])

    @pl.when(f == pl.num_programs(1) - 1)
    def _():
        _finish(acc_ref[...], x_ref, mod_ref, xo_ref, nxt)


def _ffn_call(h, w1, w3, w2, x, mod, nxt, seq, tm, tf):
    t, d = x.shape
    ff = w1.shape[1]
    per_b = seq // tm
    tok = lambda: pl.BlockSpec((tm, d), lambda i, f: (i, 0))
    modspec = lambda: pl.BlockSpec((1, 6, d), lambda i, f: (i // per_b, 0, 0))
    in_specs = [tok(), pl.BlockSpec((d, tf), lambda i, f: (0, f)),
                pl.BlockSpec((d, tf), lambda i, f: (0, f)),
                pl.BlockSpec((tf, d), lambda i, f: (f, 0)), tok(), modspec()]
    out_specs = [tok()]
    out_shape = [jax.ShapeDtypeStruct((t, d), F32)]
    args = [h, w1, w3, w2, x, mod]
    if nxt is not None:
        in_specs += [pl.BlockSpec((1, d), lambda i, f: (0, 0)), modspec()]
        out_specs.append(tok())
        out_shape.append(jax.ShapeDtypeStruct((t, d), BF16))
        args += list(nxt)
    return pl.pallas_call(
        functools.partial(_ffn_kernel, has_next=nxt is not None),
        grid=(t // tm, ff // tf),
        in_specs=in_specs, out_specs=out_specs, out_shape=out_shape,
        scratch_shapes=[pltpu.VMEM((tm, d), F32)],
        compiler_params=_cparams(("arbitrary", "arbitrary")),
        name="ffn_swiglu",
    )(*args)


def _moe_kernel(h_ref, cmb_ref, w13_ref, w2_ref, y_ref,
                acc_ref, xg_ref, ya_ref, rk_ref, rkt_ref, cnt_ref):
    e = pl.program_id(1)
    f = pl.program_id(2)
    tm = h_ref.shape[0]

    @pl.when((e == 0) & (f == 0))
    def _():
        r = lax.broadcasted_iota(jnp.int32, (tm, tm), 0)
        c = lax.broadcasted_iota(jnp.int32, (tm, tm), 1)
        before = jnp.where(c < r, 1.0, 0.0).astype(BF16)
        sel = cmb_ref[...] > 0.0
        rank = _dot(before, jnp.where(sel, 1.0, 0.0).astype(BF16))
        rk = jnp.where(sel, rank, -1.0)
        rk_ref[...] = rk
        rkt_ref[...] = rk.T
        cnt_ref[...] = jnp.sum(jnp.where(sel, 1.0, 0.0), axis=0, keepdims=True)
        acc_ref[...] = jnp.zeros(acc_ref.shape, F32)

    lane1 = lax.broadcasted_iota(jnp.int32, (1, LANES), 1)
    n_e = jnp.sum(jnp.where(lane1 == e, cnt_ref[...], 0.0)).astype(jnp.int32)

    def loop(n_blocks, body):
        if isinstance(n_blocks, int):
            for sb in range(n_blocks):
                body(sb, 0)
        else:
            lax.fori_loop(0, n_blocks, body, 0)

    def run(cap, n_blocks):
        cap_pad = -(-cap // LANES) * LANES

        def row0(sb):
            return sb * cap if isinstance(sb, int) else pl.multiple_of(sb * cap, cap)

        @pl.when(f == 0)
        def _():
            rank_row = rkt_ref[pl.ds(e, 1), :]

            def gather(sb, carry):
                r0 = row0(sb)
                slot = (r0 + lax.broadcasted_iota(jnp.int32, (cap, tm), 0)).astype(F32)
                onehot = jnp.where(rank_row == slot, 1.0, 0.0).astype(BF16)
                xg_ref[pl.ds(r0, cap), :] = _dot(onehot, h_ref[...]).astype(BF16)
                return carry

            loop(n_blocks, gather)

        def expert(sb, carry):
            r0 = row0(sb)
            xs = xg_ref[pl.ds(r0, cap), :]
            ab = _dot(xs, w13_ref[0])
            tf = w2_ref.shape[1]
            a = ab[:, :tf]
            act = a * _sigmoid(a) * ab[:, tf:]
            part = _dot(act.astype(BF16), w2_ref[0])

            @pl.when(f == 0)
            def _():
                ya_ref[pl.ds(r0, cap), :] = part

            @pl.when(f > 0)
            def _():
                ya_ref[pl.ds(r0, cap), :] += part

            return carry

        loop(n_blocks, expert)

        @pl.when(f == pl.num_programs(2) - 1)
        def _():
            lane = lax.broadcasted_iota(jnp.int32, (tm, LANES), 1)
            rank_col = jnp.sum(jnp.where(lane == e, rk_ref[...], 0.0), axis=1, keepdims=True)
            w_col = jnp.sum(jnp.where(lane == e, cmb_ref[...], 0.0), axis=1, keepdims=True)
            col = lax.broadcasted_iota(jnp.int32, (tm, cap_pad), 1)

            def scatter(sb, carry):
                r0 = row0(sb)
                ys = ya_ref[pl.ds(r0, cap), :].astype(BF16)
                if cap_pad > cap:
                    ys = jnp.concatenate([ys, jnp.zeros((cap_pad - cap, ys.shape[1]), BF16)], axis=0)
                hit = (rank_col == (r0 + col).astype(F32)) & (col < cap)
                acc_ref[...] += w_col * _dot(jnp.where(hit, 1.0, 0.0).astype(BF16), ys)
                return carry

            loop(n_blocks, scatter)

    for idx, cap in enumerate(MOE_CAPS):
        lower = MOE_CAPS[idx - 1] if idx else 0
        if idx == len(MOE_CAPS) - 1:
            pl.when(n_e > lower)(functools.partial(run, cap, (n_e + cap - 1) // cap))
        else:
            pl.when((n_e > lower) & (n_e <= cap))(functools.partial(run, cap, 1))

    @pl.when((e == pl.num_programs(1) - 1) & (f == pl.num_programs(2) - 1))
    def _():
        y_ref[...] = acc_ref[...].astype(y_ref.dtype)


def _moe_pack_w13(w1, w3):
    ff = w1.shape[-1]
    tf = _pick(ff, (MOE_TF,))
    parts = []
    for f in range(ff // tf):
        parts += [w1[:, :, f * tf:(f + 1) * tf].astype(BF16), w3[:, :, f * tf:(f + 1) * tf].astype(BF16)]
    return jnp.concatenate(parts, axis=-1)


def _moe_call(h, cmb, w13, w2, tm):
    t, d = h.shape
    ne, ff, _ = w2.shape
    tf = _pick(ff, (MOE_TF,))
    rows = -(-tm // MOE_CAPS[-1]) * MOE_CAPS[-1]
    return pl.pallas_call(
        _moe_kernel,
        grid=(t // tm, ne, ff // tf),
        in_specs=[pl.BlockSpec((tm, d), lambda i, e, f: (i, 0)),
                  pl.BlockSpec((tm, LANES), lambda i, e, f: (i, 0)),
                  pl.BlockSpec((1, d, 2 * tf), lambda i, e, f: (e, 0, f)),
                  pl.BlockSpec((1, tf, d), lambda i, e, f: (e, f, 0))],
        out_specs=pl.BlockSpec((tm, d), lambda i, e, f: (i, 0)),
        out_shape=jax.ShapeDtypeStruct((t, d), BF16),
        scratch_shapes=[pltpu.VMEM((tm, d), F32),
                        pltpu.VMEM((rows, d), BF16),
                        pltpu.VMEM((rows, d), F32),
                        pltpu.VMEM((tm, LANES), F32),
                        pltpu.VMEM((LANES, tm), F32),
                        pltpu.VMEM((1, LANES), F32)],
        compiler_params=_cparams(("arbitrary", "arbitrary", "arbitrary")),
        name="moe_top2",
    )(h, cmb, w13, w2)


def _resid_kernel(*refs, has_next):
    if has_next:
        y_ref, x_ref, mod_ref, nw_ref, modn_ref, xo_ref, hn_ref = refs
        nxt = (nw_ref, modn_ref, hn_ref)
    else:
        y_ref, x_ref, mod_ref, xo_ref = refs
        nxt = None
    _finish(y_ref[...].astype(F32), x_ref, mod_ref, xo_ref, nxt)


def _resid_call(y, x, mod, nxt, seq, tm):
    t, d = x.shape
    per_b = seq // tm
    tok = lambda: pl.BlockSpec((tm, d), lambda i: (i, 0))
    modspec = lambda: pl.BlockSpec((1, 6, d), lambda i: (i // per_b, 0, 0))
    in_specs = [tok(), tok(), modspec()]
    out_specs = [tok()]
    out_shape = [jax.ShapeDtypeStruct((t, d), F32)]
    args = [y, x, mod]
    if nxt is not None:
        in_specs += [pl.BlockSpec((1, d), lambda i: (0, 0)), modspec()]
        out_specs.append(tok())
        out_shape.append(jax.ShapeDtypeStruct((t, d), BF16))
        args += list(nxt)
    return pl.pallas_call(
        functools.partial(_resid_kernel, has_next=nxt is not None),
        grid=(t // tm,),
        in_specs=in_specs, out_specs=out_specs, out_shape=out_shape,
        compiler_params=_cparams(("arbitrary",)),
        name="moe_residual",
    )(*args)


def _attn_col_perm():
    hq = DA_HEADS * DA_DK
    idx = []
    for base in (0, 2 * hq):
        for h in range(DA_HEADS):
            idx += list(range(base + h * DA_DK, base + (h + 1) * DA_DK))
            idx += list(range(base + hq + h * DA_DK, base + hq + (h + 1) * DA_DK))
    idx += list(range(4 * hq, 4 * hq + DA_HEADS * DA_DV))
    return np.asarray(idx, np.int32)


def _pick(n, pref):
    for c in pref:
        if n % c == 0:
            return c
    return n


def kernel(x, c, w_mod, b_mod, norm1_w, norm2_w, w_in, c_conv_w, c_conv_b, a_qnorm_w, a_knorm_w,
           a_lambda_q1, a_lambda_k1, a_lambda_q2, a_lambda_k2, a_subln_w, b_lb_logits, b_gnorm_w,
           c_igate_b, c_fgate_b, c_norm_w, w_branch, w_out, ffn_w1, ffn_w3, ffn_w2,
           moe_router_w, moe_router_b, moe_w1, moe_w3, moe_w2):
    batch, seq, d = x.shape
    depth = w_in.shape[0]
    t = batch * seq
    n_a = 4 * DA_HEADS * DA_DK + DA_HEADS * DA_DV
    n_b = 4 * HG_HEADS * HG_D
    n_c = 4 * ML_HEADS * ML_D + 2 * ML_HEADS
    n_c_pad = -(-n_c // LANES) * LANES
    tm = _pick(seq, (1024, 512, 256))
    tq = _pick(seq, (ATT_TQ,))
    ts_h = _pick(seq, (256, 128))

    c8 = jnp.zeros((8, d), F32).at[:batch].set(c)
    mod = _mod_call(c8, w_mod, b_mod.reshape(depth, 1, 6 * d))[:, :batch].reshape(depth, batch, 6, d)

    lb_all = jnp.cumsum(jax.nn.softmax(b_lb_logits.astype(F32), axis=0), axis=0)
    lb_all = lb_all - lb_all[:1]
    slopes = jnp.asarray(_attn_slopes(), F32)

    xf = x.reshape(t, d)
    h = _normmod_call(xf, norm1_w[0].reshape(1, d), mod[0], seq, tm)
    perm_a = _attn_col_perm()
    w_bf = lax.optimization_barrier(w_in.astype(BF16))
    w_a_all = w_bf[:, :, :n_a][:, :, perm_a]
    w_b_all = w_bf[:, :, n_a:n_a + n_b]
    w_c_all = jnp.pad(w_bf[:, :, n_a + n_b:n_a + n_b + n_c], ((0, 0), (0, 0), (0, n_c_pad - n_c)))
    w_g_all = w_bf[:, :, n_a + n_b + n_c:]
    for l in range(depth):
        lam_init = 0.8 - 0.6 * math.exp(-0.3 * l)
        w_a, w_b, w_c, w_g = w_a_all[l], w_b_all[l], w_c_all[l], w_g_all[l]
        bias_c = jnp.zeros((1, n_c_pad), F32)
        bias_c = bias_c.at[0, n_c - 2 * ML_HEADS:n_c - ML_HEADS].set(c_igate_b[l])
        bias_c = bias_c.at[0, n_c - ML_HEADS:n_c].set(c_fgate_b[l])
        pb = _matmul(h, w_b, jnp.zeros((1, n_b), F32), F32, tm, n_b // 2)

        qw = (jnp.tile(a_qnorm_w[l], 2 * DA_HEADS) * (DA_DK ** -0.5 * LOG2E)).reshape(1, -1)
        kw = jnp.tile(a_knorm_w[l], 2 * DA_HEADS).reshape(1, -1)
        kn, qt, vt = _attn_prep_call(h, w_a, qw, kw, batch, seq, _pick(seq, (512, 256)))
        lam = (jnp.exp(jnp.sum(a_lambda_q1[l] * a_lambda_k1[l]))
               - jnp.exp(jnp.sum(a_lambda_q2[l] * a_lambda_k2[l])) + lam_init)
        scal = jnp.concatenate([slopes, jnp.stack([lam, jnp.asarray(1.0 - lam_init, F32)])]).astype(F32)
        y_a = _attn_call(scal, qt, kn, vt, a_subln_w[l].reshape(1, -1), batch, seq, tq)

        lb = lb_all[l].reshape(HG_HEADS, 1, HG_D)
        lbs = jnp.concatenate([jnp.log(lb), jnp.log1p(-lb)], axis=1)
        y_b, pc, pc_gates = _hgrn_call(pb, lbs, b_gnorm_w[l].reshape(1, -1), h, w_c, bias_c,
                                       n_c_pad - (n_c - 2 * ML_HEADS), batch, seq, ts_h)

        gates = pc_gates[:, :2 * ML_HEADS].reshape(batch, seq, 2, ML_HEADS)
        gates = gates.transpose(2, 0, 3, 1).reshape(2, batch, ML_HEADS, seq // ML_CHUNK, ML_CHUNK)
        y_c, pg = _mlstm_call(pc, gates[0], gates[1], c_conv_w[l], c_conv_b[l].reshape(1, -1),
                              c_norm_w[l].reshape(1, -1), h, w_g, batch, seq)

        dense = l % 2 == 0
        router = None
        if not dense:
            rw = jnp.pad(moe_router_w[l // 2], ((0, 0), (0, LANES - N_EXPERTS)))
            rb = jnp.pad(moe_router_b[l // 2], (0, LANES - N_EXPERTS)).reshape(1, LANES)
            router = (rw, rb)
        outs = _merge_call(y_a, y_b, y_c, pg, w_branch[l].astype(BF16), w_out[l].astype(BF16), xf,
                           mod[l], norm2_w[l].reshape(1, d), router, seq, 512)
        xf, h2 = outs[0], outs[1]
        nxt = None if l == depth - 1 else (norm1_w[l + 1].reshape(1, d), mod[l + 1])
        if dense:
            res = _ffn_call(h2, ffn_w1[l // 2].astype(BF16), ffn_w3[l // 2].astype(BF16),
                            ffn_w2[l // 2].astype(BF16), xf, mod[l], nxt, seq, tm, FFN_TF)
        else:
            y = _moe_call(h2, outs[2], _moe_pack_w13(moe_w1[l // 2], moe_w3[l // 2]),
                          moe_w2[l // 2].astype(BF16), tm)
            res = _resid_call(y, xf, mod[l], nxt, seq, tm)
        xf = res[0]
        if nxt is not None:
            h = res[1]
    return xf.reshape(batch, seq, d)
```

```python
import functools
import math

import numpy as np
import jax
import jax.numpy as jnp
from jax import lax
from jax.experimental import pallas as pl
from jax.experimental.pallas import tpu as pltpu

F32 = jnp.float32
BF16 = jnp.bfloat16
NORM_EPS = 1e-6

CHUNK = 64
DA_HEADS, DA_DK, DA_DV = 4, 64, 128
HG_HEADS, HG_D = 4, 128
ML_HEADS, ML_D, ML_CONV = 4, 128, 4
N_EXPERTS = 8

LANES = 128
VMEM_LIMIT = 56 * 1024 * 1024

ATT_TQ = 256
ATT_HEADS_PER_STEP = 2
ATT_PAD_ROWS = 16
LOG2E = math.log2(math.e)
ATT_BOUND_MARGIN = 1.01
ATT_BOUND_EPS = 1e-3
ATT_BOUND_MAX = 40.0
ATT_ZERO_EXP = 160.0
HG_CHUNK = 128
HG_SUB = 32
HG_SAFE_DECAY = 80.0
ML_CHUNK = 256
MOE_CAPS = (256, 288, 320, 352, 384)
MOE_TF = 1408
FFN_TF = 256


def _cparams(sem, vmem=VMEM_LIMIT):
    return pltpu.CompilerParams(dimension_semantics=sem, vmem_limit_bytes=vmem)


def _dot(a, b):
    return jnp.dot(a, b, preferred_element_type=F32)


def _dot_nt(a, b):
    return lax.dot_general(a, b, (((1,), (1,)), ((), ())), preferred_element_type=F32)


def _split3(x):
    hi = x.astype(BF16)
    r1 = x - hi.astype(F32)
    mid = r1.astype(BF16)
    lo = (r1 - mid.astype(F32)).astype(BF16)
    return hi, mid, lo


def _norm_mod(x, w, scale, shift):
    y = x * lax.rsqrt(jnp.mean(x * x, axis=-1, keepdims=True) + NORM_EPS) * w
    return y * (1.0 + scale) + shift


def _sigmoid(x):
    return 0.5 * jnp.tanh(0.5 * x) + 0.5


def _log1p_exp_neg(z):
    return jnp.log(1.0 + jnp.exp(-z))


def _log_sigmoid(x):
    return jnp.minimum(x, 0.0) - _log1p_exp_neg(jnp.abs(x))


def _mod_kernel(c_ref, w_ref, b_ref, o_ref):
    cnd = c_ref[...]
    cnd = cnd * _sigmoid(cnd)
    o_ref[0] = _dot(cnd.astype(BF16), w_ref[0].astype(BF16)) + b_ref[0]


def _mod_call(c8, w_mod, b_mod):
    depth, d, n = w_mod.shape
    tn = 1536 if n % 1536 == 0 else n
    return pl.pallas_call(
        _mod_kernel,
        grid=(depth, n // tn),
        in_specs=[pl.BlockSpec((8, d), lambda l, j: (0, 0)),
                  pl.BlockSpec((1, d, tn), lambda l, j: (l, 0, j)),
                  pl.BlockSpec((1, 1, tn), lambda l, j: (l, 0, j))],
        out_specs=pl.BlockSpec((1, 8, tn), lambda l, j: (l, 0, j)),
        out_shape=jax.ShapeDtypeStruct((depth, 8, n), F32),
        compiler_params=_cparams(("arbitrary", "arbitrary")),
        name="adaln_mod",
    )(c8, w_mod, b_mod)


def _normmod_kernel(x_ref, nw_ref, mod_ref, h_ref):
    m = mod_ref[0]
    h_ref[...] = _norm_mod(x_ref[...], nw_ref[...], m[1:2], m[0:1]).astype(BF16)


def _normmod_call(x, nw, mod, seq, tm):
    t, d = x.shape
    per_b = seq // tm
    return pl.pallas_call(
        _normmod_kernel,
        grid=(t // tm,),
        in_specs=[pl.BlockSpec((tm, d), lambda i: (i, 0)),
                  pl.BlockSpec((1, d), lambda i: (0, 0)),
                  pl.BlockSpec((1, 6, d), lambda i: (i // per_b, 0, 0))],
        out_specs=pl.BlockSpec((tm, d), lambda i: (i, 0)),
        out_shape=jax.ShapeDtypeStruct((t, d), BF16),
        compiler_params=_cparams(("arbitrary",)),
        name="prenorm_mod",
    )(x, nw, mod)


def _mm_kernel(x_ref, w_ref, b_ref, o_ref):
    o_ref[...] = (_dot(x_ref[...], w_ref[...]) + b_ref[...]).astype(o_ref.dtype)


def _matmul(x, w, bias, out_dtype, tm, tn):
    t, k = x.shape
    n = w.shape[1]
    return pl.pallas_call(
        _mm_kernel,
        grid=(n // tn, t // tm),
        in_specs=[pl.BlockSpec((tm, k), lambda j, i: (i, 0)),
                  pl.BlockSpec((k, tn), lambda j, i: (0, j)),
                  pl.BlockSpec((1, tn), lambda j, i: (0, j))],
        out_specs=pl.BlockSpec((tm, tn), lambda j, i: (i, j)),
        out_shape=jax.ShapeDtypeStruct((t, n), out_dtype),
        compiler_params=_cparams(("arbitrary", "arbitrary")),
        name="in_proj",
    )(x, w, bias)


def _mm_tail_kernel(x_ref, w_ref, b_ref, o_ref, tail_ref):
    res = _dot(x_ref[...], w_ref[...]) + b_ref[...]
    main = o_ref.shape[1]
    o_ref[...] = res[:, :main]
    tail_ref[...] = res[:, main:]


def _matmul_tail(x, w, bias, tm, tail):
    t, k = x.shape
    n = w.shape[1]
    return pl.pallas_call(
        _mm_tail_kernel,
        grid=(t // tm,),
        in_specs=[pl.BlockSpec((tm, k), lambda i: (i, 0)),
                  pl.BlockSpec((k, n), lambda i: (0, 0)),
                  pl.BlockSpec((1, n), lambda i: (0, 0))],
        out_specs=[pl.BlockSpec((tm, n - tail), lambda i: (i, 0)),
                   pl.BlockSpec((tm, tail), lambda i: (i, 0))],
        out_shape=[jax.ShapeDtypeStruct((t, n - tail), F32),
                   jax.ShapeDtypeStruct((t, tail), F32)],
        compiler_params=_cparams(("arbitrary",)),
        name="in_proj_c",
    )(x, w, bias)


def _attn_prep_kernel(x_ref, w_ref, qw_ref, kw_ref, kn_ref, qt_ref, vt_ref):
    hw = DA_HEADS * 2 * DA_DK
    a = _dot(x_ref[...], w_ref[...])
    r = lax.broadcasted_iota(jnp.int32, (hw, hw), 0) // DA_DK
    c = lax.broadcasted_iota(jnp.int32, (hw, hw), 1) // DA_DK
    group = jnp.where(r == c, 1.0, 0.0).astype(BF16)

    def qk_norm(z, w):
        ms = _dot((z * z).astype(BF16), group) * (1.0 / DA_DK)
        return z * lax.rsqrt(ms + NORM_EPS) * w

    qn = qk_norm(a[:, :hw], qw_ref[...])
    kn = qk_norm(a[:, hw:2 * hw], kw_ref[...])
    kn_ref[...] = kn.astype(BF16)
    qt_ref[0] = qn.T.astype(BF16)
    vt_ref[0] = a[:, 2 * hw:].T.astype(BF16)


def _attn_prep_call(x, w, qw, kw, batch, seq, ts):
    t, d = x.shape
    hw = DA_HEADS * 2 * DA_DK
    hv = DA_HEADS * DA_DV
    per_b = seq // ts
    return pl.pallas_call(
        _attn_prep_kernel,
        grid=(t // ts,),
        in_specs=[pl.BlockSpec((ts, d), lambda i: (i, 0)),
                  pl.BlockSpec((d, 2 * hw + hv), lambda i: (0, 0)),
                  pl.BlockSpec((1, hw), lambda i: (0, 0)),
                  pl.BlockSpec((1, hw), lambda i: (0, 0))],
        out_specs=[pl.BlockSpec((ts, hw), lambda i: (i, 0)),
                   pl.BlockSpec((1, hw, ts), lambda i: (i // per_b, 0, i % per_b)),
                   pl.BlockSpec((1, hv, ts), lambda i: (i // per_b, 0, i % per_b))],
        out_shape=[jax.ShapeDtypeStruct((t, hw), BF16),
                   jax.ShapeDtypeStruct((batch, hw, seq), BF16),
                   jax.ShapeDtypeStruct((batch, hv, seq), BF16)],
        compiler_params=_cparams(("arbitrary",)),
        name="attn_prep",
    )(x, w, qw, kw)


def _attn_slopes():
    return [LOG2E * 2.0 ** (-8.0 * (h + 1) / DA_HEADS) for h in range(DA_HEADS)]


def _attn_window(head, tq):
    return int((ATT_ZERO_EXP / _attn_slopes()[head] - 1.0) // tq) + 1


def _attn_kernel(sc_ref, qt_ref, k_ref, vt_ref, sw_ref, o_ref,
                 qz_ref, m_ref, acc_ref, vs_ref, kmax_ref, s_ref, s2_ref, *, tq, seq):
    hp = ATT_HEADS_PER_STEP
    dv = DA_DV
    g = pl.program_id(1)
    i = pl.program_id(2)
    lam = sc_ref[DA_HEADS]
    out_scale = sc_ref[DA_HEADS + 1]
    slopes = [sc_ref[hp * g + hh] for hh in range(hp)]
    row16 = lax.broadcasted_iota(jnp.int32, (ATT_PAD_ROWS, tq), 0)
    ones_rows = jnp.where(row16 == 0, 1.0, 0.0).astype(BF16)

    half = lax.broadcasted_iota(jnp.int32, (1, 2 * DA_DK), 1) < DA_DK

    @pl.when(i == 0)
    def _():
        pos = lax.broadcasted_iota(jnp.int32, (1, seq), 1) & (tq - 1)
        rel = (pos - (tq - 1)).astype(F32)
        sub16 = lax.broadcasted_iota(jnp.int32, (ATT_PAD_ROWS, seq), 0)
        gr = lax.broadcasted_iota(jnp.int32, (2 * DA_DK, 2 * DA_DK), 0) // DA_DK
        gc = lax.broadcasted_iota(jnp.int32, (2 * DA_DK, 2 * DA_DK), 1) // DA_DK
        group = jnp.where(gr == gc, 1.0, 0.0).astype(BF16)
        for hh in range(hp):
            w = jnp.exp2(slopes[hh] * rel)
            vs_ref[hh, 0:dv, :] = (vt_ref[0, hh * dv:(hh + 1) * dv, :].astype(F32) * w).astype(BF16)
            vs_ref[hh, dv:dv + ATT_PAD_ROWS, :] = jnp.where(sub16 == 0, w, 0.0).astype(BF16)

            def knorm(n, best):
                kc = k_ref[pl.ds(pl.multiple_of(n * tq, tq), tq),
                           hh * 2 * DA_DK:(hh + 1) * 2 * DA_DK].astype(F32)
                return jnp.maximum(best, jnp.max(_dot((kc * kc).astype(BF16), group), axis=0, keepdims=True))

            k2 = lax.fori_loop(0, seq // tq, knorm, jnp.zeros((1, 2 * DA_DK), F32))
            kmax_ref[2 * hh] = jnp.max(jnp.where(half, k2, 0.0), axis=1, keepdims=True)
            kmax_ref[2 * hh + 1] = jnp.max(jnp.where(half, 0.0, k2), axis=1, keepdims=True)

    row = lax.broadcasted_iota(jnp.int32, (2 * DA_DK, tq), 0)
    qq_row = lax.broadcasted_iota(jnp.int32, (1, tq), 1).astype(F32)
    bound_max = jnp.zeros((), F32)
    firsts = []

    def scores(hh, blk):
        start = pl.multiple_of(blk * tq, tq)
        return _dot(k_ref[pl.ds(start, tq), hh * 2 * DA_DK:(hh + 1) * 2 * DA_DK], qz_ref[hh])

    for hh in range(hp):
        qt = qt_ref[0, hh * 2 * DA_DK:(hh + 1) * 2 * DA_DK, :]
        zero = jnp.zeros_like(qt)
        q1 = jnp.where(row < DA_DK, qt, zero)
        q2 = jnp.where(row >= DA_DK, qt, zero)
        qz_ref[hh] = jnp.concatenate([q1, q2], axis=1)
        window = jnp.int32(_attn_window(hh, tq))
        for grp in range(1, DA_HEADS // hp):
            window = jnp.where(g == grp, _attn_window(grp * hp + hh, tq), window)
        firsts.append((jnp.maximum(i - window, 0) // 2) * 2)
        s_ref[hh] = scores(hh, firsts[hh])
        s2_ref[hh] = scores(hh, jnp.minimum(firsts[hh] + 1, i))
        qn = jnp.concatenate(
            [jnp.sum(jnp.square(q1.astype(F32)), axis=0, keepdims=True) * kmax_ref[2 * hh],
             jnp.sum(jnp.square(q2.astype(F32)), axis=0, keepdims=True) * kmax_ref[2 * hh + 1]], axis=1)
        bound = jnp.sqrt(qn) * ATT_BOUND_MARGIN + ATT_BOUND_EPS
        bound_max = jnp.maximum(bound_max, jnp.max(bound))
        m_ref[hh] = bound + slopes[hh] * jnp.concatenate([qq_row, qq_row], axis=1)
    acc_ref[...] = jnp.zeros(acc_ref.shape, F32)

    kk = lax.broadcasted_iota(jnp.int32, (tq, 2 * tq), 0)
    cc = lax.broadcasted_iota(jnp.int32, (tq, 2 * tq), 1)
    qq = jnp.where(cc >= tq, cc - tq, cc)
    dist = (qq - jnp.abs(qq - kk)).astype(F32)
    visible = (kk // CHUNK) <= (qq // CHUNK)

    def diag_values(hh):
        start = pl.multiple_of(i * tq, tq)
        return jnp.concatenate([vt_ref[0, hh * dv:(hh + 1) * dv, pl.ds(start, tq)], ones_rows], axis=0)

    @pl.when(bound_max <= ATT_BOUND_MAX)
    def _():
        def probs(hh, blk, s):
            last_key = ((blk + 1 - i) * tq - 1).astype(F32)
            return jnp.exp2(s - (m_ref[hh] - slopes[hh] * last_key)).astype(BF16)

        def body(heads, jj, carry):
            blk = 2 * jj
            start = pl.multiple_of(blk * tq, 2 * tq)
            for hh in heads:
                p = jnp.concatenate([probs(hh, blk, s_ref[hh]), probs(hh, blk + 1, s2_ref[hh])], axis=0)
                s_ref[hh] = scores(hh, jnp.minimum(blk + 2, i))
                s2_ref[hh] = scores(hh, jnp.minimum(blk + 3, i))
                acc_ref[hh] += _dot(vs_ref[hh, :, pl.ds(start, 2 * tq)], p)
            return carry

        for hh in range(hp - 1, 0, -1):
            lax.fori_loop(firsts[hh] // 2, firsts[hh - 1] // 2,
                          functools.partial(body, tuple(range(hh, hp))), 0)
        lax.fori_loop(firsts[0] // 2, i // 2, functools.partial(body, tuple(range(hp))), 0)

        @pl.when(i % 2 == 1)
        def _():
            start = pl.multiple_of((i - 1) * tq, tq)
            for hh in range(hp):
                acc_ref[hh] += _dot(vs_ref[hh, :, pl.ds(start, tq)], probs(hh, i - 1, s_ref[hh]))
                s_ref[hh] = s2_ref[hh]

        for hh in range(hp):
            s = jnp.where(visible, s_ref[hh] + slopes[hh] * dist, -jnp.inf)
            acc_ref[hh] += _dot(diag_values(hh), jnp.exp2(s - m_ref[hh]).astype(BF16))

    @pl.when(bound_max > ATT_BOUND_MAX)
    def _():
        m_ref[...] = jnp.full(m_ref.shape, -jnp.inf, F32)

        def update(hh, s, bound_shift, v_aug):
            m_old = m_ref[hh]
            m_new = jnp.maximum(m_old, jnp.max(s, axis=0, keepdims=True) + bound_shift)
            p = jnp.exp2(s - (m_new - bound_shift))
            acc_ref[hh] = jnp.exp2(m_old - m_new) * acc_ref[hh] + _dot(v_aug, p.astype(BF16))
            m_ref[hh] = m_new

        def body(j, carry):
            start = pl.multiple_of(j * tq, tq)
            last_key = ((j + 1 - i) * tq - 1).astype(F32)
            for hh in range(hp):
                update(hh, scores(hh, j), slopes[hh] * last_key, vs_ref[hh, :, pl.ds(start, tq)])
            return carry

        lax.fori_loop(0, i, body, 0)
        for hh in range(hp):
            s = jnp.where(visible, scores(hh, i) + slopes[hh] * dist, -jnp.inf)
            update(hh, s, jnp.zeros((), F32), diag_values(hh))

    outs = []
    for hh in range(hp):
        acc = acc_ref[hh]
        o2 = acc[:dv, :] * (1.0 / acc[dv:dv + 1, :])
        o = (o2[:, :tq] - lam * o2[:, tq:]).T
        o = o * lax.rsqrt(jnp.mean(o * o, axis=-1, keepdims=True) + NORM_EPS) * sw_ref[...]
        outs.append(o * out_scale)
    o_ref[...] = jnp.concatenate(outs, axis=1).astype(o_ref.dtype)


def _attn_call(scal, qt, kn, vt, sw, batch, seq, tq):
    t = kn.shape[0]
    nq = seq // tq
    hp = ATT_HEADS_PER_STEP
    return pl.pallas_call(
        functools.partial(_attn_kernel, tq=tq, seq=seq),
        grid=(batch, DA_HEADS // hp, nq),
        in_specs=[pl.BlockSpec(memory_space=pltpu.SMEM),
                  pl.BlockSpec((1, hp * 2 * DA_DK, tq), lambda b, g, i: (b, g, i)),
                  pl.BlockSpec((seq, hp * 2 * DA_DK), lambda b, g, i: (b, g)),
                  pl.BlockSpec((1, hp * DA_DV, seq), lambda b, g, i: (b, g, 0)),
                  pl.BlockSpec((1, DA_DV), lambda b, g, i: (0, 0))],
        out_specs=pl.BlockSpec((tq, hp * DA_DV), lambda b, g, i: (b * nq + i, g)),
        out_shape=jax.ShapeDtypeStruct((t, DA_HEADS * DA_DV), BF16),
        scratch_shapes=[pltpu.VMEM((hp, 2 * DA_DK, 2 * tq), BF16),
                        pltpu.VMEM((hp, 1, 2 * tq), F32),
                        pltpu.VMEM((hp, DA_DV + ATT_PAD_ROWS, 2 * tq), F32),
                        pltpu.VMEM((hp, DA_DV + ATT_PAD_ROWS, seq), BF16),
                        pltpu.VMEM((2 * hp, 1, 1), F32),
                        pltpu.VMEM((hp, tq, 2 * tq), F32),
                        pltpu.VMEM((hp, tq, 2 * tq), F32)],
        compiler_params=_cparams(("arbitrary", "arbitrary", "arbitrary")),
        name="diff_attn",
    )(scal, qt, kn, vt, sw)


def _hgrn_kernel(q_ref, f_ref, i_ref, g_ref, lb_ref, gw_ref, o_ref,
                 st_ref, b_ref, qs_ref, ks_ref, oi_ref, *, n_chunks):
    c = HG_CHUNK
    sub = HG_SUB
    d = HG_D
    nh = HG_HEADS
    chains = [(bb, hh) for bb in range(q_ref.shape[0]) for hh in range(nh)]

    @pl.when(pl.program_id(0) == 0)
    def _():
        st_ref[...] = jnp.zeros(st_ref.shape, F32)

    gw = gw_ref[...]
    rr = lax.broadcasted_iota(jnp.int32, (c, c), 0)
    cc = lax.broadcasted_iota(jnp.int32, (c, c), 1)
    causal = cc <= rr
    tril = jnp.where(causal, 1.0, 0.0).astype(BF16)

    def chunk(n, carry):
        r0 = pl.multiple_of(n * c, c)
        gates = []
        for ch, (bb, hh) in enumerate(chains):
            cols = slice(hh * d, (hh + 1) * d)
            log_lb = lb_ref[hh, 0:1, :]
            q = q_ref[bb, pl.ds(r0, c), cols]
            q = q * _sigmoid(q)
            a = lb_ref[hh, 1:2, :] + _log_sigmoid(f_ref[bb, pl.ds(r0, c), cols])
            logf = jnp.maximum(log_lb, a) + _log1p_exp_neg(jnp.abs(log_lb - a))
            k = 1.0 - jnp.exp(logf)
            v = i_ref[bb, pl.ds(r0, c), cols]
            hi, mid, lo = _split3(logf)
            b = _dot(tril, hi) + _dot(tril, mid) + _dot(tril, lo)
            gates.append((q, k, v, b))

        heads = []
        decay = jnp.zeros((), F32)
        for ch, (q, k, v, b) in enumerate(gates):
            b_last = b[c - 1:c, :]
            st = st_ref[ch]
            o_inter = _dot_nt((q * jnp.exp(b)).astype(BF16), st.astype(BF16))
            k_hat = k * jnp.exp(b_last - b)
            st_ref[ch] = st * jnp.exp(b_last) + _dot(v.T.astype(BF16), k_hat.astype(BF16))
            betas = []
            for blk in range(c // sub):
                beta = jnp.zeros((1, d), F32) if blk == 0 else b[blk * sub - 1:blk * sub, :]
                betas.append(beta)
                b_end = b[(blk + 1) * sub - 1:(blk + 1) * sub, :]
                decay = jnp.maximum(decay, jnp.max(beta - b_end))
            heads.append((q, k, v, b, betas, o_inter))

        @pl.when(decay < HG_SAFE_DECAY)
        def _():
            atts = []
            for q, k, v, b, betas, _ in heads:
                rows = []
                for blk in range(c // sub):
                    beta = betas[blk]
                    q_t = q[blk * sub:(blk + 1) * sub, :] * jnp.exp(b[blk * sub:(blk + 1) * sub, :] - beta)
                    k_t = k * jnp.exp(jnp.minimum(beta - b, HG_SAFE_DECAY))
                    rows.append(_dot_nt(q_t.astype(BF16), k_t.astype(BF16)))
                atts.append(jnp.where(causal, jnp.concatenate(rows, axis=0), 0.0).astype(BF16))
            for ch, att in enumerate(atts):
                oi_ref[ch] = _dot(att, heads[ch][2].astype(BF16))

        @pl.when(decay >= HG_SAFE_DECAY)
        def _():
            ridx = lax.broadcasted_iota(jnp.int32, (c, d), 0)
            for ch, (q, k, v, b, _, _) in enumerate(heads):
                b_ref[...] = b
                qs_ref[...] = q
                ks_ref[...] = k

                def row(t, carry2):
                    bt = b_ref[pl.ds(t, 1), :]
                    e = jnp.exp(jnp.where(ridx <= t, bt - b_ref[...], -jnp.inf))
                    w = jnp.sum(qs_ref[pl.ds(t, 1), :] * ks_ref[...] * e, axis=1, keepdims=True)
                    oi_ref[ch, pl.ds(t, 1), :] = jnp.sum(w * v, axis=0, keepdims=True)
                    return carry2

                lax.fori_loop(0, c, row, 0)

        for ch, (bb, hh) in enumerate(chains):
            cols = slice(hh * d, (hh + 1) * d)
            o = heads[ch][5] + oi_ref[ch]
            o = o * lax.rsqrt(jnp.mean(o * o, axis=-1, keepdims=True) + NORM_EPS) * gw
            g = g_ref[bb, pl.ds(r0, c), cols]
            o_ref[bb, pl.ds(r0, c), cols] = (o * (g * _sigmoid(g))).astype(o_ref.dtype)
        return carry

    lax.fori_loop(0, n_chunks, chunk, 0)


def _hgrn_call(bproj, lbs, gw, batch, seq, ts):
    t = bproj.shape[0]
    d = HG_D
    nh = HG_HEADS
    bp3 = bproj.reshape(batch, seq, bproj.shape[1])
    spec = lambda off: pl.BlockSpec((batch, ts, nh * d), lambda i: (0, i, off))
    out = pl.pallas_call(
        functools.partial(_hgrn_kernel, n_chunks=ts // HG_CHUNK),
        grid=(seq // ts,),
        in_specs=[spec(0), spec(1), spec(2), spec(3),
                  pl.BlockSpec((nh, 2, d), lambda i: (0, 0, 0)),
                  pl.BlockSpec((1, d), lambda i: (0, 0))],
        out_specs=pl.BlockSpec((batch, ts, nh * d), lambda i: (0, i, 0)),
        out_shape=jax.ShapeDtypeStruct((batch, seq, nh * d), BF16),
        scratch_shapes=[pltpu.VMEM((batch * nh, d, d), F32),
                        pltpu.VMEM((HG_CHUNK, d), F32),
                        pltpu.VMEM((HG_CHUNK, d), F32),
                        pltpu.VMEM((HG_CHUNK, d), F32),
                        pltpu.VMEM((batch * nh, HG_CHUNK, d), F32)],
        compiler_params=_cparams(("arbitrary",)),
        name="hgrn2",
    )(bp3, bp3, bp3, bp3, lbs, gw)
    return out.reshape(t, nh * d)


def _mlstm_kernel(uq_ref, uk_ref, v_ref, op_ref, gi_ref, gf_ref, cwq_ref, cwk_ref,
                  cbq_ref, cbk_ref, nw_ref, hg_ref, wg_ref, wb_ref, o_ref, pg_ref, pb_ref,
                  xq_ref, xk_ref, c_ref, m_ref, bs_ref):
    L = ML_CHUNK
    n_gate = wg_ref.shape[1]
    n_hg = wb_ref.shape[1]

    def host(w_ref, out_ref, lo, hi):
        for bb in range(hg_ref.shape[0]):
            out_ref[bb, :, lo:hi] = _dot(hg_ref[bb], w_ref[:, lo:hi]).astype(out_ref.dtype)

    def gate_proj(part):
        host(wg_ref, pg_ref, part * n_gate // 3, (part + 1) * n_gate // 3)
        if part:
            host(wb_ref, pb_ref, (part - 1) * n_hg // 2, part * n_hg // 2)

    d = ML_D
    nh = ML_HEADS
    nb = uq_ref.shape[0]
    i = pl.program_id(0)
    rr = lax.broadcasted_iota(jnp.int32, (L, L), 0)
    cc = lax.broadcasted_iota(jnp.int32, (L, L), 1)

    @pl.when(i == 0)
    def _():
        c_ref[...] = jnp.zeros(c_ref.shape, F32)
        m_ref[...] = jnp.zeros(m_ref.shape, F32)
        xq_ref[:, 0:8, :] = jnp.zeros((nb, 8, nh * d), F32)
        xk_ref[:, 0:8, :] = jnp.zeros((nb, 8, nh * d), F32)
        upper = jnp.where(rr <= cc, 1.0, 0.0).astype(BF16)
        for ch in range(nb * nh):
            hi, mid, lo = _split3(_log_sigmoid(gf_ref[ch // nh, ch % nh]))
            bs_ref[ch] = _dot(hi, upper) + _dot(mid, upper) + _dot(lo, upper)

    def conv_silu(bb, u_ref, x_ref, w_ref, b_ref):
        x_ref[bb, 8:8 + L, :] = u_ref[bb]
        y = b_ref[...] + w_ref[ML_CONV - 1:ML_CONV, :] * x_ref[bb, 8:8 + L, :]
        for j in range(ML_CONV - 1):
            y = y + w_ref[j:j + 1, :] * x_ref[bb, 5 + j:5 + j + L, :]
        x_ref[bb, 0:8, :] = x_ref[bb, L:L + 8, :]
        return y * _sigmoid(y)

    gate_proj(0)
    q_all = [conv_silu(bb, uq_ref, xq_ref, cwq_ref, cbq_ref) for bb in range(nb)]
    k_all = [conv_silu(bb, uk_ref, xk_ref, cwk_ref, cbk_ref) * (d ** -0.5) for bb in range(nb)]
    lane = lax.broadcasted_iota(jnp.int32, (L, d), 1)
    ones_col = jnp.where(lane == 0, 1.0, 0.0)
    stage1 = []
    for ch in range(nb * nh):
        bb, hh = ch // nh, ch % nh
        cols = slice(hh * d, (hh + 1) * d)
        q = q_all[bb][:, cols].astype(BF16)
        kt = k_all[bb][:, cols].T
        v_aug = jnp.concatenate([v_ref[bb, :, cols], ones_col], axis=1).astype(BF16)
        c_aug = c_ref[ch]
        qk = _dot(q, kt.astype(BF16))
        qc = _dot(q, c_aug.astype(BF16))

        b_row = bs_ref[ch, pl.ds(i, 1), :]
        ig_row = gi_ref[bb, hh, pl.ds(i, 1), :]
        m_prev = m_ref[ch]
        g = b_row[:, L - 1:L]
        log_w = g - b_row + ig_row
        m_new = jnp.maximum(g + m_prev, jnp.max(log_w, axis=1, keepdims=True))
        w_s = jnp.exp(log_w - m_new)
        c_ref[ch] = jnp.exp(g + m_prev - m_new) * c_aug + _dot((kt * w_s).astype(BF16), v_aug)
        m_ref[ch] = m_new
        stage1.append((qk, qc, v_aug, b_row, ig_row, m_prev))

    gate_proj(1)
    stage2 = []
    for qk, qc, v_aug, b_row, ig_row, m_prev in stage1:
        b_col = jnp.sum(jnp.where(rr == cc, b_row, 0.0), axis=1, keepdims=True)
        log_d = jnp.where(cc <= rr, b_col + (ig_row - b_row), -jnp.inf)
        log_inter = b_col + m_prev
        m_t = jnp.maximum(log_inter, jnp.max(log_d, axis=1, keepdims=True))
        w_intra = jnp.exp(log_d - m_t) * qk
        tot = jnp.exp(log_inter - m_t) * qc + _dot(w_intra.astype(BF16), v_aug)
        stage2.append((tot, m_t))

    gate_proj(2)
    for ch, (tot, m_t) in enumerate(stage2):
        bb, hh = ch // nh, ch % nh
        cols = slice(hh * d, (hh + 1) * d)
        denom = jnp.maximum(jnp.abs(tot[:, d:d + 1]), jnp.exp(-m_t))
        hout = tot[:, :d] / denom
        hout = hout * lax.rsqrt(jnp.mean(hout * hout, axis=-1, keepdims=True) + NORM_EPS) * nw_ref[...]
        o_ref[bb, :, cols] = (hout * _sigmoid(op_ref[bb, :, cols])).astype(o_ref.dtype)


def _mlstm_call(cproj, gi, gf, conv_w, conv_b, nw, h, w_gate, w_hg, batch, seq):
    t = cproj.shape[0]
    d = ML_D
    nh = ML_HEADS
    L = ML_CHUNK
    nc = seq // L
    dm, n_gate = w_gate.shape
    n_hg = w_hg.shape[1]
    cp3 = cproj.reshape(batch, seq, cproj.shape[1])
    spec = lambda off: pl.BlockSpec((batch, L, nh * d), lambda i: (0, i, off))
    gspec = pl.BlockSpec((batch, nh, nc, L), lambda i: (0, 0, 0, 0))
    out, pg, pb = pl.pallas_call(
        _mlstm_kernel,
        grid=(nc,),
        in_specs=[spec(0), spec(1), spec(2), spec(3), gspec, gspec,
                  pl.BlockSpec((ML_CONV, nh * d), lambda i: (0, 0)),
                  pl.BlockSpec((ML_CONV, nh * d), lambda i: (0, 1)),
                  pl.BlockSpec((1, nh * d), lambda i: (0, 0)),
                  pl.BlockSpec((1, nh * d), lambda i: (0, 1)),
                  pl.BlockSpec((1, d), lambda i: (0, 0)),
                  pl.BlockSpec((batch, L, dm), lambda i: (0, i, 0)),
                  pl.BlockSpec((dm, n_gate), lambda i: (0, 0)),
                  pl.BlockSpec((dm, n_hg), lambda i: (0, 0))],
        out_specs=[pl.BlockSpec((batch, L, nh * d), lambda i: (0, i, 0)),
                   pl.BlockSpec((batch, L, n_gate), lambda i: (0, i, 0)),
                   pl.BlockSpec((batch, L, n_hg), lambda i: (0, i, 0))],
        out_shape=[jax.ShapeDtypeStruct((batch, seq, nh * d), BF16),
                   jax.ShapeDtypeStruct((batch, seq, n_gate), BF16),
                   jax.ShapeDtypeStruct((batch, seq, n_hg), F32)],
        scratch_shapes=[pltpu.VMEM((batch, L + 8, nh * d), F32),
                        pltpu.VMEM((batch, L + 8, nh * d), F32),
                        pltpu.VMEM((batch * nh, d, 2 * d), F32),
                        pltpu.VMEM((batch * nh, 1, 1), F32),
                        pltpu.VMEM((batch * nh, nc, L), F32)],
        compiler_params=_cparams(("arbitrary",)),
        name="mlstm",
    )(cp3, cp3, cp3, cp3, gi, gf, conv_w, conv_w, conv_b, conv_b, nw,
      h.reshape(batch, seq, dm), w_gate, w_hg)
    return out.reshape(t, nh * d), pg.reshape(t, n_gate), pb.reshape(t, n_hg)


def _top2_combine(logits):
    lane = lax.broadcasted_iota(jnp.int32, logits.shape, 1)
    lg = jnp.where(lane < N_EXPERTS, logits, -jnp.inf)
    ex = jnp.exp(lg - jnp.max(lg, axis=1, keepdims=True))
    probs = ex / jnp.sum(ex, axis=1, keepdims=True)
    p1 = jnp.max(probs, axis=1, keepdims=True)
    i1 = jnp.min(jnp.where(probs == p1, lane, LANES), axis=1, keepdims=True)
    rest = jnp.where(lane == i1, -1.0, probs)
    p2 = jnp.max(rest, axis=1, keepdims=True)
    i2 = jnp.min(jnp.where(rest == p2, lane, LANES), axis=1, keepdims=True)
    comb = jnp.where(lane == i1, p1, 0.0) + jnp.where(lane == i2, p2, 0.0)
    return comb / (p1 + p2)


def _merge_kernel(*refs, route):
    if route:
        (ya_ref, yb_ref, yc_ref, gp_ref, wb_ref, wo_ref, x_ref, mod_ref, nw_ref,
         rw_ref, rb_ref, xo_ref, h_ref, cmb_ref) = refs
    else:
        (ya_ref, yb_ref, yc_ref, gp_ref, wb_ref, wo_ref, x_ref, mod_ref, nw_ref,
         xo_ref, h_ref) = refs
    d = x_ref.shape[1]
    merged = None
    for n, y_ref in enumerate((ya_ref, yb_ref, yc_ref)):
        gate = _sigmoid(gp_ref[:, n * d:(n + 1) * d].astype(F32))
        term = gate * _dot(y_ref[...], wb_ref[n])
        merged = term if merged is None else merged + term
    m = mod_ref[0]
    xn = x_ref[...] + m[2:3] * _dot(merged.astype(BF16), wo_ref[...])
    xo_ref[...] = xn
    h2 = _norm_mod(xn, nw_ref[...], m[4:5], m[3:4])
    h_ref[...] = h2.astype(BF16)
    if route:
        h_hi, h_mid, _ = _split3(h2)
        r_hi, r_mid, _ = _split3(rw_ref[...])
        logits = _dot(h_hi, r_hi) + _dot(h_mid, r_hi) + _dot(h_hi, r_mid) + rb_ref[...]
        cmb_ref[...] = _top2_combine(logits)


def _merge_call(ya, yb, yc, gp, wb, wo, x, mod, nw, router, seq, tm):
    t, d = x.shape
    bw = ya.shape[1]
    per_b = seq // tm
    route = router is not None
    tok = lambda w: pl.BlockSpec((tm, w), lambda i: (i, 0))
    const2 = lambda s: pl.BlockSpec(s, lambda i: (0, 0))
    in_specs = [tok(bw), tok(bw), tok(bw), tok(3 * d),
                pl.BlockSpec((3, bw, d), lambda i: (0, 0, 0)), const2((d, d)), tok(d),
                pl.BlockSpec((1, 6, d), lambda i: (i // per_b, 0, 0)), const2((1, d))]
    out_specs = [tok(d), tok(d)]
    out_shape = [jax.ShapeDtypeStruct((t, d), F32), jax.ShapeDtypeStruct((t, d), BF16)]
    args = [ya, yb, yc, gp, wb, wo, x, mod, nw]
    if route:
        in_specs += [const2((d, LANES)), const2((1, LANES))]
        out_specs.append(tok(LANES))
        out_shape.append(jax.ShapeDtypeStruct((t, LANES), F32))
        args += list(router)
    return pl.pallas_call(
        functools.partial(_merge_kernel, route=route),
        grid=(t // tm,),
        in_specs=in_specs, out_specs=out_specs, out_shape=out_shape,
        compiler_params=_cparams(("arbitrary",)),
        name="merge_out",
    )(*args)


def _finish(acc, x_ref, mod_ref, xo_ref, nxt):
    xn = x_ref[...] + mod_ref[0][5:6] * acc
    xo_ref[...] = xn
    if nxt is not None:
        nw_ref, modn_ref, hn_ref = nxt
        mn = modn_ref[0]
        hn_ref[...] = _norm_mod(xn, nw_ref[...], mn[1:2], mn[0:1]).astype(BF16)


def _ffn_kernel(*refs, has_next):
    if has_next:
        h_ref, w1_ref, w3_ref, w2_ref, x_ref, mod_ref, nw_ref, modn_ref, xo_ref, hn_ref, acc_ref = refs
        nxt = (nw_ref, modn_ref, hn_ref)
    else:
        h_ref, w1_ref, w3_ref, w2_ref, x_ref, mod_ref, xo_ref, acc_ref = refs
        nxt = None
    f = pl.program_id(1)

    @pl.when(f == 0)
    def _():
        acc_ref[...] = jnp.zeros(acc_ref.shape, F32)

    h = h_ref[...]
    a = _dot(h, w1_ref[...])
    act = a * _sigmoid(a) * _dot(h, w3_ref[...])
    acc_ref[...] += _dot(act.astype(BF16), w2_ref[...])

    @pl.when(f == pl.num_programs(1) - 1)
    def _():
        _finish(acc_ref[...], x_ref, mod_ref, xo_ref, nxt)


def _ffn_call(h, w1, w3, w2, x, mod, nxt, seq, tm, tf):
    t, d = x.shape
    ff = w1.shape[1]
    per_b = seq // tm
    tok = lambda: pl.BlockSpec((tm, d), lambda i, f: (i, 0))
    modspec = lambda: pl.BlockSpec((1, 6, d), lambda i, f: (i // per_b, 0, 0))
    in_specs = [tok(), pl.BlockSpec((d, tf), lambda i, f: (0, f)),
                pl.BlockSpec((d, tf), lambda i, f: (0, f)),
                pl.BlockSpec((tf, d), lambda i, f: (f, 0)), tok(), modspec()]
    out_specs = [tok()]
    out_shape = [jax.ShapeDtypeStruct((t, d), F32)]
    args = [h, w1, w3, w2, x, mod]
    if nxt is not None:
        in_specs += [pl.BlockSpec((1, d), lambda i, f: (0, 0)), modspec()]
        out_specs.append(tok())
        out_shape.append(jax.ShapeDtypeStruct((t, d), BF16))
        args += list(nxt)
    return pl.pallas_call(
        functools.partial(_ffn_kernel, has_next=nxt is not None),
        grid=(t // tm, ff // tf),
        in_specs=in_specs, out_specs=out_specs, out_shape=out_shape,
        scratch_shapes=[pltpu.VMEM((tm, d), F32)],
        compiler_params=_cparams(("arbitrary", "arbitrary")),
        name="ffn_swiglu",
    )(*args)


def _moe_kernel(h_ref, cmb_ref, w13_ref, w2_ref, y_ref,
                acc_ref, xg_ref, ya_ref, rk_ref, rkt_ref, cnt_ref):
    e = pl.program_id(1)
    f = pl.program_id(2)
    tm = h_ref.shape[0]

    @pl.when((e == 0) & (f == 0))
    def _():
        r = lax.broadcasted_iota(jnp.int32, (tm, tm), 0)
        c = lax.broadcasted_iota(jnp.int32, (tm, tm), 1)
        before = jnp.where(c < r, 1.0, 0.0).astype(BF16)
        sel = cmb_ref[...] > 0.0
        rank = _dot(before, jnp.where(sel, 1.0, 0.0).astype(BF16))
        rk = jnp.where(sel, rank, -1.0)
        rk_ref[...] = rk
        rkt_ref[...] = rk.T
        cnt_ref[...] = jnp.sum(jnp.where(sel, 1.0, 0.0), axis=0, keepdims=True)
        acc_ref[...] = jnp.zeros(acc_ref.shape, F32)

    lane1 = lax.broadcasted_iota(jnp.int32, (1, LANES), 1)
    n_e = jnp.sum(jnp.where(lane1 == e, cnt_ref[...], 0.0)).astype(jnp.int32)

    def loop(n_blocks, body):
        if isinstance(n_blocks, int):
            for sb in range(n_blocks):
                body(sb, 0)
        else:
            lax.fori_loop(0, n_blocks, body, 0)

    def run(cap, n_blocks):
        cap_pad = -(-cap // LANES) * LANES

        def row0(sb):
            return sb * cap if isinstance(sb, int) else pl.multiple_of(sb * cap, cap)

        @pl.when(f == 0)
        def _():
            rank_row = rkt_ref[pl.ds(e, 1), :]

            def gather(sb, carry):
                r0 = row0(sb)
                slot = (r0 + lax.broadcasted_iota(jnp.int32, (cap, tm), 0)).astype(F32)
                onehot = jnp.where(rank_row == slot, 1.0, 0.0).astype(BF16)
                xg_ref[pl.ds(r0, cap), :] = _dot(onehot, h_ref[...]).astype(BF16)
                return carry

            loop(n_blocks, gather)

        def expert(sb, carry):
            r0 = row0(sb)
            xs = xg_ref[pl.ds(r0, cap), :]
            ab = _dot(xs, w13_ref[0])
            tf = w2_ref.shape[1]
            a = ab[:, :tf]
            act = a * _sigmoid(a) * ab[:, tf:]
            part = _dot(act.astype(BF16), w2_ref[0])

            @pl.when(f == 0)
            def _():
                ya_ref[pl.ds(r0, cap), :] = part

            @pl.when(f > 0)
            def _():
                ya_ref[pl.ds(r0, cap), :] += part

            return carry

        loop(n_blocks, expert)

        @pl.when(f == pl.num_programs(2) - 1)
        def _():
            lane = lax.broadcasted_iota(jnp.int32, (tm, LANES), 1)
            rank_col = jnp.sum(jnp.where(lane == e, rk_ref[...], 0.0), axis=1, keepdims=True)
            w_col = jnp.sum(jnp.where(lane == e, cmb_ref[...], 0.0), axis=1, keepdims=True)
            col = lax.broadcasted_iota(jnp.int32, (tm, cap_pad), 1)

            def scatter(sb, carry):
                r0 = row0(sb)
                ys = ya_ref[pl.ds(r0, cap), :].astype(BF16)
                if cap_pad > cap:
                    ys = jnp.concatenate([ys, jnp.zeros((cap_pad - cap, ys.shape[1]), BF16)], axis=0)
                hit = (rank_col == (r0 + col).astype(F32)) & (col < cap)
                acc_ref[...] += w_col * _dot(jnp.where(hit, 1.0, 0.0).astype(BF16), ys)
                return carry

            loop(n_blocks, scatter)

    for idx, cap in enumerate(MOE_CAPS):
        lower = MOE_CAPS[idx - 1] if idx else 0
        if idx == len(MOE_CAPS) - 1:
            pl.when(n_e > lower)(functools.partial(run, cap, (n_e + cap - 1) // cap))
        else:
            pl.when((n_e > lower) & (n_e <= cap))(functools.partial(run, cap, 1))

    @pl.when((e == pl.num_programs(1) - 1) & (f == pl.num_programs(2) - 1))
    def _():
        y_ref[...] = acc_ref[...].astype(y_ref.dtype)


def _moe_pack_w13(w1, w3):
    ff = w1.shape[-1]
    tf = _pick(ff, (MOE_TF,))
    parts = []
    for f in range(ff // tf):
        parts += [w1[:, :, f * tf:(f + 1) * tf].astype(BF16), w3[:, :, f * tf:(f + 1) * tf].astype(BF16)]
    return jnp.concatenate(parts, axis=-1)


def _moe_call(h, cmb, w13, w2, tm):
    t, d = h.shape
    ne, ff, _ = w2.shape
    tf = _pick(ff, (MOE_TF,))
    rows = -(-tm // MOE_CAPS[-1]) * MOE_CAPS[-1]
    return pl.pallas_call(
        _moe_kernel,
        grid=(t // tm, ne, ff // tf),
        in_specs=[pl.BlockSpec((tm, d), lambda i, e, f: (i, 0)),
                  pl.BlockSpec((tm, LANES), lambda i, e, f: (i, 0)),
                  pl.BlockSpec((1, d, 2 * tf), lambda i, e, f: (e, 0, f)),
                  pl.BlockSpec((1, tf, d), lambda i, e, f: (e, f, 0))],
        out_specs=pl.BlockSpec((tm, d), lambda i, e, f: (i, 0)),
        out_shape=jax.ShapeDtypeStruct((t, d), BF16),
        scratch_shapes=[pltpu.VMEM((tm, d), F32),
                        pltpu.VMEM((rows, d), BF16),
                        pltpu.VMEM((rows, d), F32),
                        pltpu.VMEM((tm, LANES), F32),
                        pltpu.VMEM((LANES, tm), F32),
                        pltpu.VMEM((1, LANES), F32)],
        compiler_params=_cparams(("arbitrary", "arbitrary", "arbitrary")),
        name="moe_top2",
    )(h, cmb, w13, w2)


def _resid_kernel(*refs, has_next):
    if has_next:
        y_ref, x_ref, mod_ref, nw_ref, modn_ref, xo_ref, hn_ref = refs
        nxt = (nw_ref, modn_ref, hn_ref)
    else:
        y_ref, x_ref, mod_ref, xo_ref = refs
        nxt = None
    _finish(y_ref[...].astype(F32), x_ref, mod_ref, xo_ref, nxt)


def _resid_call(y, x, mod, nxt, seq, tm):
    t, d = x.shape
    per_b = seq // tm
    tok = lambda: pl.BlockSpec((tm, d), lambda i: (i, 0))
    modspec = lambda: pl.BlockSpec((1, 6, d), lambda i: (i // per_b, 0, 0))
    in_specs = [tok(), tok(), modspec()]
    out_specs = [tok()]
    out_shape = [jax.ShapeDtypeStruct((t, d), F32)]
    args = [y, x, mod]
    if nxt is not None:
        in_specs += [pl.BlockSpec((1, d), lambda i: (0, 0)), modspec()]
        out_specs.append(tok())
        out_shape.append(jax.ShapeDtypeStruct((t, d), BF16))
        args += list(nxt)
    return pl.pallas_call(
        functools.partial(_resid_kernel, has_next=nxt is not None),
        grid=(t // tm,),
        in_specs=in_specs, out_specs=out_specs, out_shape=out_shape,
        compiler_params=_cparams(("arbitrary",)),
        name="moe_residual",
    )(*args)


def _attn_col_perm():
    hq = DA_HEADS * DA_DK
    idx = []
    for base in (0, 2 * hq):
        for h in range(DA_HEADS):
            idx += list(range(base + h * DA_DK, base + (h + 1) * DA_DK))
            idx += list(range(base + hq + h * DA_DK, base + hq + (h + 1) * DA_DK))
    idx += list(range(4 * hq, 4 * hq + DA_HEADS * DA_DV))
    return np.asarray(idx, np.int32)


def _pick(n, pref):
    for c in pref:
        if n % c == 0:
            return c
    return n


def kernel(x, c, w_mod, b_mod, norm1_w, norm2_w, w_in, c_conv_w, c_conv_b, a_qnorm_w, a_knorm_w,
           a_lambda_q1, a_lambda_k1, a_lambda_q2, a_lambda_k2, a_subln_w, b_lb_logits, b_gnorm_w,
           c_igate_b, c_fgate_b, c_norm_w, w_branch, w_out, ffn_w1, ffn_w3, ffn_w2,
           moe_router_w, moe_router_b, moe_w1, moe_w3, moe_w2):
    batch, seq, d = x.shape
    depth = w_in.shape[0]
    t = batch * seq
    n_a = 4 * DA_HEADS * DA_DK + DA_HEADS * DA_DV
    n_b = 4 * HG_HEADS * HG_D
    n_c = 4 * ML_HEADS * ML_D + 2 * ML_HEADS
    n_c_pad = -(-n_c // LANES) * LANES
    tm = _pick(seq, (1024, 512, 256))
    tq = _pick(seq, (ATT_TQ,))
    ts_h = _pick(seq, (512, 256, 128))

    c8 = jnp.zeros((8, d), F32).at[:batch].set(c)
    mod = _mod_call(c8, w_mod, b_mod.reshape(depth, 1, 6 * d))[:, :batch].reshape(depth, batch, 6, d)

    lb_all = jnp.cumsum(jax.nn.softmax(b_lb_logits.astype(F32), axis=0), axis=0)
    lb_all = lb_all - lb_all[:1]
    slopes = jnp.asarray(_attn_slopes(), F32)

    xf = x.reshape(t, d)
    h = _normmod_call(xf, norm1_w[0].reshape(1, d), mod[0], seq, tm)
    perm_a = _attn_col_perm()
    w_bf = lax.optimization_barrier(w_in.astype(BF16))
    w_a_all = w_bf[:, :, :n_a][:, :, perm_a]
    w_b_all = w_bf[:, :, n_a:n_a + n_b]
    w_c_all = jnp.pad(w_bf[:, :, n_a + n_b:n_a + n_b + n_c], ((0, 0), (0, 0), (0, n_c_pad - n_c)))
    w_g_all = w_bf[:, :, n_a + n_b + n_c:]
    for l in range(depth):
        lam_init = 0.8 - 0.6 * math.exp(-0.3 * l)
        w_a, w_b, w_c, w_g = w_a_all[l], w_b_all[l], w_c_all[l], w_g_all[l]
        bias_c = jnp.zeros((1, n_c_pad), F32)
        bias_c = bias_c.at[0, n_c - 2 * ML_HEADS:n_c - ML_HEADS].set(c_igate_b[l])
        bias_c = bias_c.at[0, n_c - ML_HEADS:n_c].set(c_fgate_b[l])
        pc, pc_gates = _matmul_tail(h, w_c, bias_c, tm, n_c_pad - (n_c - 2 * ML_HEADS))

        qw = (jnp.tile(a_qnorm_w[l], 2 * DA_HEADS) * (DA_DK ** -0.5 * LOG2E)).reshape(1, -1)
        kw = jnp.tile(a_knorm_w[l], 2 * DA_HEADS).reshape(1, -1)
        kn, qt, vt = _attn_prep_call(h, w_a, qw, kw, batch, seq, _pick(seq, (512, 256)))
        lam = (jnp.exp(jnp.sum(a_lambda_q1[l] * a_lambda_k1[l]))
               - jnp.exp(jnp.sum(a_lambda_q2[l] * a_lambda_k2[l])) + lam_init)
        scal = jnp.concatenate([slopes, jnp.stack([lam, jnp.asarray(1.0 - lam_init, F32)])]).astype(F32)
        y_a = _attn_call(scal, qt, kn, vt, a_subln_w[l].reshape(1, -1), batch, seq, tq)

        gates = pc_gates[:, :2 * ML_HEADS].reshape(batch, seq, 2, ML_HEADS)
        gates = gates.transpose(2, 0, 3, 1).reshape(2, batch, ML_HEADS, seq // ML_CHUNK, ML_CHUNK)
        y_c, pg, pb = _mlstm_call(pc, gates[0], gates[1], c_conv_w[l], c_conv_b[l].reshape(1, -1),
                                  c_norm_w[l].reshape(1, -1), h, w_g, w_b, batch, seq)

        lb = lb_all[l].reshape(HG_HEADS, 1, HG_D)
        lbs = jnp.concatenate([jnp.log(lb), jnp.log1p(-lb)], axis=1)
        y_b = _hgrn_call(pb, lbs, b_gnorm_w[l].reshape(1, -1), batch, seq, ts_h)

        dense = l % 2 == 0
        router = None
        if not dense:
            rw = jnp.pad(moe_router_w[l // 2], ((0, 0), (0, LANES - N_EXPERTS)))
            rb = jnp.pad(moe_router_b[l // 2], (0, LANES - N_EXPERTS)).reshape(1, LANES)
            router = (rw, rb)
        outs = _merge_call(y_a, y_b, y_c, pg, w_branch[l].astype(BF16), w_out[l].astype(BF16), xf,
                           mod[l], norm2_w[l].reshape(1, d), router, seq, 512)
        xf, h2 = outs[0], outs[1]
        nxt = None if l == depth - 1 else (norm1_w[l + 1].reshape(1, d), mod[l + 1])
        if dense:
            res = _ffn_call(h2, ffn_w1[l // 2].astype(BF16), ffn_w3[l // 2].astype(BF16),
                            ffn_w2[l // 2].astype(BF16), xf, mod[l], nxt, seq, tm, FFN_TF)
        else:
            y = _moe_call(h2, outs[2], _moe_pack_w13(moe_w1[l // 2], moe_w3[l // 2]),
                          moe_w2[l // 2].astype(BF16), tm)
            res = _resid_call(y, xf, mod[l], nxt, seq, tm)
        xf = res[0]
        if nxt is not None:
            h = res[1]
    return xf.reshape(batch, seq, d)
```

```python
import functools
import math

import numpy as np
import jax
import jax.numpy as jnp
from jax import lax
from jax.experimental import pallas as pl
from jax.experimental.pallas import tpu as pltpu

F32 = jnp.float32
BF16 = jnp.bfloat16
NORM_EPS = 1e-6

CHUNK = 64
DA_HEADS, DA_DK, DA_DV = 4, 64, 128
HG_HEADS, HG_D = 4, 128
ML_HEADS, ML_D, ML_CONV = 4, 128, 4
N_EXPERTS = 8

LANES = 128
VMEM_LIMIT = 56 * 1024 * 1024

ATT_TQ = 256
ATT_HEADS_PER_STEP = 2
ATT_PAD_ROWS = 16
LOG2E = math.log2(math.e)
ATT_BOUND_MARGIN = 1.01
ATT_BOUND_EPS = 1e-3
ATT_BOUND_MAX = 40.0
ATT_ZERO_EXP = 160.0
HG_CHUNK = 128
HG_SUB = 32
HG_SAFE_DECAY = 80.0
ML_CHUNK = 256
MOE_CAPS = (256, 288, 320, 352, 384)
MOE_TF = 1408
FFN_TF = 256


def _cparams(sem, vmem=VMEM_LIMIT):
    return pltpu.CompilerParams(dimension_semantics=sem, vmem_limit_bytes=vmem)


def _dot(a, b):
    return jnp.dot(a, b, preferred_element_type=F32)


def _dot_nt(a, b):
    return lax.dot_general(a, b, (((1,), (1,)), ((), ())), preferred_element_type=F32)


def _split3(x):
    hi = x.astype(BF16)
    r1 = x - hi.astype(F32)
    mid = r1.astype(BF16)
    lo = (r1 - mid.astype(F32)).astype(BF16)
    return hi, mid, lo


def _norm_mod(x, w, scale, shift):
    y = x * lax.rsqrt(jnp.mean(x * x, axis=-1, keepdims=True) + NORM_EPS) * w
    return y * (1.0 + scale) + shift


def _sigmoid(x):
    return 0.5 * jnp.tanh(0.5 * x) + 0.5


def _log1p_exp_neg(z):
    return jnp.log(1.0 + jnp.exp(-z))


def _log_sigmoid(x):
    return jnp.minimum(x, 0.0) - _log1p_exp_neg(jnp.abs(x))


def _mod_kernel(c_ref, w_ref, b_ref, o_ref):
    cnd = c_ref[...]
    cnd = cnd * _sigmoid(cnd)
    o_ref[0] = _dot(cnd.astype(BF16), w_ref[0].astype(BF16)) + b_ref[0]


def _mod_call(c8, w_mod, b_mod):
    depth, d, n = w_mod.shape
    tn = 1536 if n % 1536 == 0 else n
    return pl.pallas_call(
        _mod_kernel,
        grid=(depth, n // tn),
        in_specs=[pl.BlockSpec((8, d), lambda l, j: (0, 0)),
                  pl.BlockSpec((1, d, tn), lambda l, j: (l, 0, j)),
                  pl.BlockSpec((1, 1, tn), lambda l, j: (l, 0, j))],
        out_specs=pl.BlockSpec((1, 8, tn), lambda l, j: (l, 0, j)),
        out_shape=jax.ShapeDtypeStruct((depth, 8, n), F32),
        compiler_params=_cparams(("arbitrary", "arbitrary")),
        name="adaln_mod",
    )(c8, w_mod, b_mod)


def _normmod_kernel(x_ref, nw_ref, mod_ref, h_ref):
    m = mod_ref[0]
    h_ref[...] = _norm_mod(x_ref[...], nw_ref[...], m[1:2], m[0:1]).astype(BF16)


def _normmod_call(x, nw, mod, seq, tm):
    t, d = x.shape
    per_b = seq // tm
    return pl.pallas_call(
        _normmod_kernel,
        grid=(t // tm,),
        in_specs=[pl.BlockSpec((tm, d), lambda i: (i, 0)),
                  pl.BlockSpec((1, d), lambda i: (0, 0)),
                  pl.BlockSpec((1, 6, d), lambda i: (i // per_b, 0, 0))],
        out_specs=pl.BlockSpec((tm, d), lambda i: (i, 0)),
        out_shape=jax.ShapeDtypeStruct((t, d), BF16),
        compiler_params=_cparams(("arbitrary",)),
        name="prenorm_mod",
    )(x, nw, mod)


def _mm_tail_kernel(x_ref, w_ref, b_ref, o_ref, tail_ref):
    res = _dot(x_ref[...], w_ref[...]) + b_ref[...]
    main = o_ref.shape[1]
    o_ref[...] = res[:, :main]
    tail_ref[...] = res[:, main:]


def _matmul_tail(x, w, bias, tm, tail):
    t, k = x.shape
    n = w.shape[1]
    return pl.pallas_call(
        _mm_tail_kernel,
        grid=(t // tm,),
        in_specs=[pl.BlockSpec((tm, k), lambda i: (i, 0)),
                  pl.BlockSpec((k, n), lambda i: (0, 0)),
                  pl.BlockSpec((1, n), lambda i: (0, 0))],
        out_specs=[pl.BlockSpec((tm, n - tail), lambda i: (i, 0)),
                   pl.BlockSpec((tm, tail), lambda i: (i, 0))],
        out_shape=[jax.ShapeDtypeStruct((t, n - tail), F32),
                   jax.ShapeDtypeStruct((t, tail), F32)],
        compiler_params=_cparams(("arbitrary",)),
        name="in_proj_c",
    )(x, w, bias)


def _attn_prep_kernel(x_ref, w_ref, qw_ref, kw_ref, kn_ref, qt_ref, vt_ref):
    hw = DA_HEADS * 2 * DA_DK
    a = _dot(x_ref[...], w_ref[...])
    r = lax.broadcasted_iota(jnp.int32, (hw, hw), 0) // DA_DK
    c = lax.broadcasted_iota(jnp.int32, (hw, hw), 1) // DA_DK
    group = jnp.where(r == c, 1.0, 0.0).astype(BF16)

    def qk_norm(z, w):
        ms = _dot((z * z).astype(BF16), group) * (1.0 / DA_DK)
        return z * lax.rsqrt(ms + NORM_EPS) * w

    qn = qk_norm(a[:, :hw], qw_ref[...])
    kn = qk_norm(a[:, hw:2 * hw], kw_ref[...])
    kn_ref[...] = kn.astype(BF16)
    qt_ref[0] = qn.T.astype(BF16)
    vt_ref[0] = a[:, 2 * hw:].T.astype(BF16)


def _attn_prep_call(x, w, qw, kw, batch, seq, ts):
    t, d = x.shape
    hw = DA_HEADS * 2 * DA_DK
    hv = DA_HEADS * DA_DV
    per_b = seq // ts
    return pl.pallas_call(
        _attn_prep_kernel,
        grid=(t // ts,),
        in_specs=[pl.BlockSpec((ts, d), lambda i: (i, 0)),
                  pl.BlockSpec((d, 2 * hw + hv), lambda i: (0, 0)),
                  pl.BlockSpec((1, hw), lambda i: (0, 0)),
                  pl.BlockSpec((1, hw), lambda i: (0, 0))],
        out_specs=[pl.BlockSpec((ts, hw), lambda i: (i, 0)),
                   pl.BlockSpec((1, hw, ts), lambda i: (i // per_b, 0, i % per_b)),
                   pl.BlockSpec((1, hv, ts), lambda i: (i // per_b, 0, i % per_b))],
        out_shape=[jax.ShapeDtypeStruct((t, hw), BF16),
                   jax.ShapeDtypeStruct((batch, hw, seq), BF16),
                   jax.ShapeDtypeStruct((batch, hv, seq), BF16)],
        compiler_params=_cparams(("arbitrary",)),
        name="attn_prep",
    )(x, w, qw, kw)


def _attn_slopes():
    return [LOG2E * 2.0 ** (-8.0 * (h + 1) / DA_HEADS) for h in range(DA_HEADS)]


def _attn_window(head, tq):
    return int((ATT_ZERO_EXP / _attn_slopes()[head] - 1.0) // tq) + 1


def _attn_kernel(sc_ref, qt_ref, k_ref, vt_ref, sw_ref, o_ref,
                 qz_ref, m_ref, acc_ref, vs_ref, kmax_ref, s_ref, s2_ref, *, tq, seq):
    hp = ATT_HEADS_PER_STEP
    dv = DA_DV
    g = pl.program_id(1)
    i = pl.program_id(2)
    lam = sc_ref[DA_HEADS]
    out_scale = sc_ref[DA_HEADS + 1]
    slopes = [sc_ref[hp * g + hh] for hh in range(hp)]
    row16 = lax.broadcasted_iota(jnp.int32, (ATT_PAD_ROWS, tq), 0)
    ones_rows = jnp.where(row16 == 0, 1.0, 0.0).astype(BF16)

    half = lax.broadcasted_iota(jnp.int32, (1, 2 * DA_DK), 1) < DA_DK

    @pl.when(i == 0)
    def _():
        pos = lax.broadcasted_iota(jnp.int32, (1, seq), 1) & (tq - 1)
        rel = (pos - (tq - 1)).astype(F32)
        sub16 = lax.broadcasted_iota(jnp.int32, (ATT_PAD_ROWS, seq), 0)
        gr = lax.broadcasted_iota(jnp.int32, (2 * DA_DK, 2 * DA_DK), 0) // DA_DK
        gc = lax.broadcasted_iota(jnp.int32, (2 * DA_DK, 2 * DA_DK), 1) // DA_DK
        group = jnp.where(gr == gc, 1.0, 0.0).astype(BF16)
        for hh in range(hp):
            w = jnp.exp2(slopes[hh] * rel)
            vs_ref[hh, 0:dv, :] = (vt_ref[0, hh * dv:(hh + 1) * dv, :].astype(F32) * w).astype(BF16)
            vs_ref[hh, dv:dv + ATT_PAD_ROWS, :] = jnp.where(sub16 == 0, w, 0.0).astype(BF16)

            def knorm(n, best):
                kc = k_ref[pl.ds(pl.multiple_of(n * tq, tq), tq),
                           hh * 2 * DA_DK:(hh + 1) * 2 * DA_DK].astype(F32)
                return jnp.maximum(best, jnp.max(_dot((kc * kc).astype(BF16), group), axis=0, keepdims=True))

            k2 = lax.fori_loop(0, seq // tq, knorm, jnp.zeros((1, 2 * DA_DK), F32))
            kmax_ref[2 * hh] = jnp.max(jnp.where(half, k2, 0.0), axis=1, keepdims=True)
            kmax_ref[2 * hh + 1] = jnp.max(jnp.where(half, 0.0, k2), axis=1, keepdims=True)

    row = lax.broadcasted_iota(jnp.int32, (2 * DA_DK, tq), 0)
    qq_row = lax.broadcasted_iota(jnp.int32, (1, tq), 1).astype(F32)
    bound_max = jnp.zeros((), F32)
    firsts = []

    def scores(hh, blk):
        start = pl.multiple_of(blk * tq, tq)
        return _dot(k_ref[pl.ds(start, tq), hh * 2 * DA_DK:(hh + 1) * 2 * DA_DK], qz_ref[hh])

    for hh in range(hp):
        qt = qt_ref[0, hh * 2 * DA_DK:(hh + 1) * 2 * DA_DK, :]
        zero = jnp.zeros_like(qt)
        q1 = jnp.where(row < DA_DK, qt, zero)
        q2 = jnp.where(row >= DA_DK, qt, zero)
        qz_ref[hh] = jnp.concatenate([q1, q2], axis=1)
        window = jnp.int32(_attn_window(hh, tq))
        for grp in range(1, DA_HEADS // hp):
            window = jnp.where(g == grp, _attn_window(grp * hp + hh, tq), window)
        firsts.append((jnp.maximum(i - window, 0) // 2) * 2)
        s_ref[hh] = scores(hh, firsts[hh])
        s2_ref[hh] = scores(hh, jnp.minimum(firsts[hh] + 1, i))
        qn = jnp.concatenate(
            [jnp.sum(jnp.square(q1.astype(F32)), axis=0, keepdims=True) * kmax_ref[2 * hh],
             jnp.sum(jnp.square(q2.astype(F32)), axis=0, keepdims=True) * kmax_ref[2 * hh + 1]], axis=1)
        bound = jnp.sqrt(qn) * ATT_BOUND_MARGIN + ATT_BOUND_EPS
        bound_max = jnp.maximum(bound_max, jnp.max(bound))
        m_ref[hh] = bound + slopes[hh] * jnp.concatenate([qq_row, qq_row], axis=1)
    acc_ref[...] = jnp.zeros(acc_ref.shape, F32)

    kk = lax.broadcasted_iota(jnp.int32, (tq, 2 * tq), 0)
    cc = lax.broadcasted_iota(jnp.int32, (tq, 2 * tq), 1)
    qq = jnp.where(cc >= tq, cc - tq, cc)
    dist = (qq - jnp.abs(qq - kk)).astype(F32)
    visible = (kk // CHUNK) <= (qq // CHUNK)

    def diag_values(hh):
        start = pl.multiple_of(i * tq, tq)
        return jnp.concatenate([vt_ref[0, hh * dv:(hh + 1) * dv, pl.ds(start, tq)], ones_rows], axis=0)

    @pl.when(bound_max <= ATT_BOUND_MAX)
    def _():
        def probs(hh, blk, s):
            last_key = ((blk + 1 - i) * tq - 1).astype(F32)
            return jnp.exp2(s - (m_ref[hh] - slopes[hh] * last_key)).astype(BF16)

        def body(heads, jj, carry):
            blk = 2 * jj
            start = pl.multiple_of(blk * tq, 2 * tq)
            for hh in heads:
                p = jnp.concatenate([probs(hh, blk, s_ref[hh]), probs(hh, blk + 1, s2_ref[hh])], axis=0)
                s_ref[hh] = scores(hh, jnp.minimum(blk + 2, i))
                s2_ref[hh] = scores(hh, jnp.minimum(blk + 3, i))
                acc_ref[hh] += _dot(vs_ref[hh, :, pl.ds(start, 2 * tq)], p)
            return carry

        for hh in range(hp - 1, 0, -1):
            lax.fori_loop(firsts[hh] // 2, firsts[hh - 1] // 2,
                          functools.partial(body, tuple(range(hh, hp))), 0)
        lax.fori_loop(firsts[0] // 2, i // 2, functools.partial(body, tuple(range(hp))), 0)

        @pl.when(i % 2 == 1)
        def _():
            start = pl.multiple_of((i - 1) * tq, tq)
            for hh in range(hp):
                acc_ref[hh] += _dot(vs_ref[hh, :, pl.ds(start, tq)], probs(hh, i - 1, s_ref[hh]))
                s_ref[hh] = s2_ref[hh]

        for hh in range(hp):
            s = jnp.where(visible, s_ref[hh] + slopes[hh] * dist, -jnp.inf)
            acc_ref[hh] += _dot(diag_values(hh), jnp.exp2(s - m_ref[hh]).astype(BF16))

    @pl.when(bound_max > ATT_BOUND_MAX)
    def _():
        m_ref[...] = jnp.full(m_ref.shape, -jnp.inf, F32)

        def update(hh, s, bound_shift, v_aug):
            m_old = m_ref[hh]
            m_new = jnp.maximum(m_old, jnp.max(s, axis=0, keepdims=True) + bound_shift)
            p = jnp.exp2(s - (m_new - bound_shift))
            acc_ref[hh] = jnp.exp2(m_old - m_new) * acc_ref[hh] + _dot(v_aug, p.astype(BF16))
            m_ref[hh] = m_new

        def body(j, carry):
            start = pl.multiple_of(j * tq, tq)
            last_key = ((j + 1 - i) * tq - 1).astype(F32)
            for hh in range(hp):
                update(hh, scores(hh, j), slopes[hh] * last_key, vs_ref[hh, :, pl.ds(start, tq)])
            return carry

        lax.fori_loop(0, i, body, 0)
        for hh in range(hp):
            s = jnp.where(visible, scores(hh, i) + slopes[hh] * dist, -jnp.inf)
            update(hh, s, jnp.zeros((), F32), diag_values(hh))

    outs = []
    for hh in range(hp):
        acc = acc_ref[hh]
        o2 = acc[:dv, :] * (1.0 / acc[dv:dv + 1, :])
        o = (o2[:, :tq] - lam * o2[:, tq:]).T
        o = o * lax.rsqrt(jnp.mean(o * o, axis=-1, keepdims=True) + NORM_EPS) * sw_ref[...]
        outs.append(o * out_scale)
    o_ref[...] = jnp.concatenate(outs, axis=1).astype(o_ref.dtype)


def _attn_call(scal, qt, kn, vt, sw, batch, seq, tq):
    t = kn.shape[0]
    nq = seq // tq
    hp = ATT_HEADS_PER_STEP
    return pl.pallas_call(
        functools.partial(_attn_kernel, tq=tq, seq=seq),
        grid=(batch, DA_HEADS // hp, nq),
        in_specs=[pl.BlockSpec(memory_space=pltpu.SMEM),
                  pl.BlockSpec((1, hp * 2 * DA_DK, tq), lambda b, g, i: (b, g, i)),
                  pl.BlockSpec((seq, hp * 2 * DA_DK), lambda b, g, i: (b, g)),
                  pl.BlockSpec((1, hp * DA_DV, seq), lambda b, g, i: (b, g, 0)),
                  pl.BlockSpec((1, DA_DV), lambda b, g, i: (0, 0))],
        out_specs=pl.BlockSpec((tq, hp * DA_DV), lambda b, g, i: (b * nq + i, g)),
        out_shape=jax.ShapeDtypeStruct((t, DA_HEADS * DA_DV), BF16),
        scratch_shapes=[pltpu.VMEM((hp, 2 * DA_DK, 2 * tq), BF16),
                        pltpu.VMEM((hp, 1, 2 * tq), F32),
                        pltpu.VMEM((hp, DA_DV + ATT_PAD_ROWS, 2 * tq), F32),
                        pltpu.VMEM((hp, DA_DV + ATT_PAD_ROWS, seq), BF16),
                        pltpu.VMEM((2 * hp, 1, 1), F32),
                        pltpu.VMEM((hp, tq, 2 * tq), F32),
                        pltpu.VMEM((hp, tq, 2 * tq), F32)],
        compiler_params=_cparams(("arbitrary", "arbitrary", "arbitrary")),
        name="diff_attn",
    )(scal, qt, kn, vt, sw)


def _hgrn_kernel(q_ref, f_ref, i_ref, g_ref, lb_ref, gw_ref, o_ref,
                 st_ref, b_ref, qs_ref, ks_ref, oi_ref, *, n_chunks):
    c = HG_CHUNK
    sub = HG_SUB
    d = HG_D
    nh = HG_HEADS
    chains = [(bb, hh) for bb in range(q_ref.shape[0]) for hh in range(nh)]

    @pl.when(pl.program_id(0) == 0)
    def _():
        st_ref[...] = jnp.zeros(st_ref.shape, F32)

    gw = gw_ref[...]
    rr = lax.broadcasted_iota(jnp.int32, (c, c), 0)
    cc = lax.broadcasted_iota(jnp.int32, (c, c), 1)
    causal = cc <= rr
    tril = jnp.where(causal, 1.0, 0.0).astype(BF16)

    def chunk(n, carry):
        r0 = pl.multiple_of(n * c, c)
        gates = []
        for ch, (bb, hh) in enumerate(chains):
            cols = slice(hh * d, (hh + 1) * d)
            log_lb = lb_ref[hh, 0:1, :]
            q = q_ref[bb, pl.ds(r0, c), cols]
            q = q * _sigmoid(q)
            a = lb_ref[hh, 1:2, :] + _log_sigmoid(f_ref[bb, pl.ds(r0, c), cols])
            logf = jnp.maximum(log_lb, a) + _log1p_exp_neg(jnp.abs(log_lb - a))
            k = 1.0 - jnp.exp(logf)
            v = i_ref[bb, pl.ds(r0, c), cols]
            hi, mid, lo = _split3(logf)
            b = _dot(tril, hi) + _dot(tril, mid) + _dot(tril, lo)
            gates.append((q, k, v, b))

        heads = []
        decay = jnp.zeros((), F32)
        for ch, (q, k, v, b) in enumerate(gates):
            b_last = b[c - 1:c, :]
            st = st_ref[ch]
            o_inter = _dot_nt((q * jnp.exp(b)).astype(BF16), st.astype(BF16))
            k_hat = k * jnp.exp(b_last - b)
            st_ref[ch] = st * jnp.exp(b_last) + _dot(v.T.astype(BF16), k_hat.astype(BF16))
            betas = []
            for blk in range(c // sub):
                beta = jnp.zeros((1, d), F32) if blk == 0 else b[blk * sub - 1:blk * sub, :]
                betas.append(beta)
                b_end = b[(blk + 1) * sub - 1:(blk + 1) * sub, :]
                decay = jnp.maximum(decay, jnp.max(beta - b_end))
            heads.append((q, k, v, b, betas, o_inter))

        @pl.when(decay < HG_SAFE_DECAY)
        def _():
            atts = []
            for q, k, v, b, betas, _ in heads:
                rows = []
                for blk in range(c // sub):
                    beta = betas[blk]
                    q_t = q[blk * sub:(blk + 1) * sub, :] * jnp.exp(b[blk * sub:(blk + 1) * sub, :] - beta)
                    k_t = k * jnp.exp(jnp.minimum(beta - b, HG_SAFE_DECAY))
                    rows.append(_dot_nt(q_t.astype(BF16), k_t.astype(BF16)))
                atts.append(jnp.where(causal, jnp.concatenate(rows, axis=0), 0.0).astype(BF16))
            for ch, att in enumerate(atts):
                oi_ref[ch] = _dot(att, heads[ch][2].astype(BF16))

        @pl.when(decay >= HG_SAFE_DECAY)
        def _():
            ridx = lax.broadcasted_iota(jnp.int32, (c, d), 0)
            for ch, (q, k, v, b, _, _) in enumerate(heads):
                b_ref[...] = b
                qs_ref[...] = q
                ks_ref[...] = k

                def row(t, carry2):
                    bt = b_ref[pl.ds(t, 1), :]
                    e = jnp.exp(jnp.where(ridx <= t, bt - b_ref[...], -jnp.inf))
                    w = jnp.sum(qs_ref[pl.ds(t, 1), :] * ks_ref[...] * e, axis=1, keepdims=True)
                    oi_ref[ch, pl.ds(t, 1), :] = jnp.sum(w * v, axis=0, keepdims=True)
                    return carry2

                lax.fori_loop(0, c, row, 0)

        for ch, (bb, hh) in enumerate(chains):
            cols = slice(hh * d, (hh + 1) * d)
            o = heads[ch][5] + oi_ref[ch]
            o = o * lax.rsqrt(jnp.mean(o * o, axis=-1, keepdims=True) + NORM_EPS) * gw
            g = g_ref[bb, pl.ds(r0, c), cols]
            o_ref[bb, pl.ds(r0, c), cols] = (o * (g * _sigmoid(g))).astype(o_ref.dtype)
        return carry

    lax.fori_loop(0, n_chunks, chunk, 0)


def _hgrn_call(bproj, lbs, gw, batch, seq, ts):
    t = bproj.shape[0]
    d = HG_D
    nh = HG_HEADS
    bp3 = bproj.reshape(batch, seq, bproj.shape[1])
    spec = lambda off: pl.BlockSpec((batch, ts, nh * d), lambda i: (0, i, off))
    out = pl.pallas_call(
        functools.partial(_hgrn_kernel, n_chunks=ts // HG_CHUNK),
        grid=(seq // ts,),
        in_specs=[spec(0), spec(1), spec(2), spec(3),
                  pl.BlockSpec((nh, 2, d), lambda i: (0, 0, 0)),
                  pl.BlockSpec((1, d), lambda i: (0, 0))],
        out_specs=pl.BlockSpec((batch, ts, nh * d), lambda i: (0, i, 0)),
        out_shape=jax.ShapeDtypeStruct((batch, seq, nh * d), BF16),
        scratch_shapes=[pltpu.VMEM((batch * nh, d, d), F32),
                        pltpu.VMEM((HG_CHUNK, d), F32),
                        pltpu.VMEM((HG_CHUNK, d), F32),
                        pltpu.VMEM((HG_CHUNK, d), F32),
                        pltpu.VMEM((batch * nh, HG_CHUNK, d), F32)],
        compiler_params=_cparams(("arbitrary",)),
        name="hgrn2",
    )(bp3, bp3, bp3, bp3, lbs, gw)
    return out.reshape(t, nh * d)


def _mlstm_kernel(uq_ref, uk_ref, v_ref, op_ref, gi_ref, gf_ref, cwq_ref, cwk_ref,
                  cbq_ref, cbk_ref, nw_ref, hg_ref, wg_ref, wb_ref, o_ref, pg_ref, pb_ref,
                  xq_ref, xk_ref, c_ref, m_ref, bs_ref):
    L = ML_CHUNK
    n_gate = wg_ref.shape[1]
    n_hg = wb_ref.shape[1]

    def host(w_ref, out_ref, lo, hi):
        for bb in range(hg_ref.shape[0]):
            out_ref[bb, :, lo:hi] = _dot(hg_ref[bb], w_ref[:, lo:hi]).astype(out_ref.dtype)

    def gate_proj(part):
        host(wg_ref, pg_ref, part * n_gate // 3, (part + 1) * n_gate // 3)
        if part:
            host(wb_ref, pb_ref, (part - 1) * n_hg // 2, part * n_hg // 2)

    d = ML_D
    nh = ML_HEADS
    nb = uq_ref.shape[0]
    i = pl.program_id(0)
    rr = lax.broadcasted_iota(jnp.int32, (L, L), 0)
    cc = lax.broadcasted_iota(jnp.int32, (L, L), 1)

    @pl.when(i == 0)
    def _():
        c_ref[...] = jnp.zeros(c_ref.shape, F32)
        m_ref[...] = jnp.zeros(m_ref.shape, F32)
        xq_ref[:, 0:8, :] = jnp.zeros((nb, 8, nh * d), F32)
        xk_ref[:, 0:8, :] = jnp.zeros((nb, 8, nh * d), F32)
        upper = jnp.where(rr <= cc, 1.0, 0.0).astype(BF16)
        for ch in range(nb * nh):
            hi, mid, lo = _split3(_log_sigmoid(gf_ref[ch // nh, ch % nh]))
            bs_ref[ch] = _dot(hi, upper) + _dot(mid, upper) + _dot(lo, upper)

    def conv_silu(bb, u_ref, x_ref, w_ref, b_ref):
        x_ref[bb, 8:8 + L, :] = u_ref[bb]
        y = b_ref[...] + w_ref[ML_CONV - 1:ML_CONV, :] * x_ref[bb, 8:8 + L, :]
        for j in range(ML_CONV - 1):
            y = y + w_ref[j:j + 1, :] * x_ref[bb, 5 + j:5 + j + L, :]
        x_ref[bb, 0:8, :] = x_ref[bb, L:L + 8, :]
        return y * _sigmoid(y)

    gate_proj(0)
    q_all = [conv_silu(bb, uq_ref, xq_ref, cwq_ref, cbq_ref) for bb in range(nb)]
    k_all = [conv_silu(bb, uk_ref, xk_ref, cwk_ref, cbk_ref) * (d ** -0.5) for bb in range(nb)]
    lane = lax.broadcasted_iota(jnp.int32, (L, d), 1)
    ones_col = jnp.where(lane == 0, 1.0, 0.0)
    stage1 = []
    for ch in range(nb * nh):
        bb, hh = ch // nh, ch % nh
        cols = slice(hh * d, (hh + 1) * d)
        q = q_all[bb][:, cols].astype(BF16)
        kt = k_all[bb][:, cols].T
        v_aug = jnp.concatenate([v_ref[bb, :, cols], ones_col], axis=1).astype(BF16)
        c_aug = c_ref[ch]
        qk = _dot(q, kt.astype(BF16))
        qc = _dot(q, c_aug.astype(BF16))

        b_row = bs_ref[ch, pl.ds(i, 1), :]
        ig_row = gi_ref[bb, hh, pl.ds(i, 1), :]
        m_prev = m_ref[ch]
        g = b_row[:, L - 1:L]
        log_w = g - b_row + ig_row
        m_new = jnp.maximum(g + m_prev, jnp.max(log_w, axis=1, keepdims=True))
        w_s = jnp.exp(log_w - m_new)
        c_ref[ch] = jnp.exp(g + m_prev - m_new) * c_aug + _dot((kt * w_s).astype(BF16), v_aug)
        m_ref[ch] = m_new
        stage1.append((qk, qc, v_aug, b_row, ig_row, m_prev))

    gate_proj(1)
    stage2 = []
    for qk, qc, v_aug, b_row, ig_row, m_prev in stage1:
        b_col = jnp.sum(jnp.where(rr == cc, b_row, 0.0), axis=1, keepdims=True)
        log_d = jnp.where(cc <= rr, b_col + (ig_row - b_row), -jnp.inf)
        log_inter = b_col + m_prev
        m_t = jnp.maximum(log_inter, jnp.max(log_d, axis=1, keepdims=True))
        w_intra = jnp.exp(log_d - m_t) * qk
        tot = jnp.exp(log_inter - m_t) * qc + _dot(w_intra.astype(BF16), v_aug)
        stage2.append((tot, m_t))

    gate_proj(2)
    for ch, (tot, m_t) in enumerate(stage2):
        bb, hh = ch // nh, ch % nh
        cols = slice(hh * d, (hh + 1) * d)
        denom = jnp.maximum(jnp.abs(tot[:, d:d + 1]), jnp.exp(-m_t))
        hout = tot[:, :d] / denom
        hout = hout * lax.rsqrt(jnp.mean(hout * hout, axis=-1, keepdims=True) + NORM_EPS) * nw_ref[...]
        o_ref[bb, :, cols] = (hout * _sigmoid(op_ref[bb, :, cols])).astype(o_ref.dtype)


def _mlstm_call(cproj, gi, gf, conv_w, conv_b, nw, h, w_gate, w_hg, batch, seq):
    t = cproj.shape[0]
    d = ML_D
    nh = ML_HEADS
    L = ML_CHUNK
    nc = seq // L
    dm, n_gate = w_gate.shape
    n_hg = w_hg.shape[1]
    cp3 = cproj.reshape(batch, seq, cproj.shape[1])
    spec = lambda off: pl.BlockSpec((batch, L, nh * d), lambda i: (0, i, off))
    gspec = pl.BlockSpec((batch, nh, nc, L), lambda i: (0, 0, 0, 0))
    out, pg, pb = pl.pallas_call(
        _mlstm_kernel,
        grid=(nc,),
        in_specs=[spec(0), spec(1), spec(2), spec(3), gspec, gspec,
                  pl.BlockSpec((ML_CONV, nh * d), lambda i: (0, 0)),
                  pl.BlockSpec((ML_CONV, nh * d), lambda i: (0, 1)),
                  pl.BlockSpec((1, nh * d), lambda i: (0, 0)),
                  pl.BlockSpec((1, nh * d), lambda i: (0, 1)),
                  pl.BlockSpec((1, d), lambda i: (0, 0)),
                  pl.BlockSpec((batch, L, dm), lambda i: (0, i, 0)),
                  pl.BlockSpec((dm, n_gate), lambda i: (0, 0)),
                  pl.BlockSpec((dm, n_hg), lambda i: (0, 0))],
        out_specs=[pl.BlockSpec((batch, L, nh * d), lambda i: (0, i, 0)),
                   pl.BlockSpec((batch, L, n_gate), lambda i: (0, i, 0)),
                   pl.BlockSpec((batch, L, n_hg), lambda i: (0, i, 0))],
        out_shape=[jax.ShapeDtypeStruct((batch, seq, nh * d), BF16),
                   jax.ShapeDtypeStruct((batch, seq, n_gate), BF16),
                   jax.ShapeDtypeStruct((batch, seq, n_hg), F32)],
        scratch_shapes=[pltpu.VMEM((batch, L + 8, nh * d), F32),
                        pltpu.VMEM((batch, L + 8, nh * d), F32),
                        pltpu.VMEM((batch * nh, d, 2 * d), F32),
                        pltpu.VMEM((batch * nh, 1, 1), F32),
                        pltpu.VMEM((batch * nh, nc, L), F32)],
        compiler_params=_cparams(("arbitrary",)),
        name="mlstm",
    )(cp3, cp3, cp3, cp3, gi, gf, conv_w, conv_w, conv_b, conv_b, nw,
      h.reshape(batch, seq, dm), w_gate, w_hg)
    return out.reshape(t, nh * d), pg.reshape(t, n_gate), pb.reshape(t, n_hg)


def _top2_combine(logits):
    lane = lax.broadcasted_iota(jnp.int32, logits.shape, 1)
    lg = jnp.where(lane < N_EXPERTS, logits, -jnp.inf)
    ex = jnp.exp(lg - jnp.max(lg, axis=1, keepdims=True))
    probs = ex / jnp.sum(ex, axis=1, keepdims=True)
    p1 = jnp.max(probs, axis=1, keepdims=True)
    i1 = jnp.min(jnp.where(probs == p1, lane, LANES), axis=1, keepdims=True)
    rest = jnp.where(lane == i1, -1.0, probs)
    p2 = jnp.max(rest, axis=1, keepdims=True)
    i2 = jnp.min(jnp.where(rest == p2, lane, LANES), axis=1, keepdims=True)
    comb = jnp.where(lane == i1, p1, 0.0) + jnp.where(lane == i2, p2, 0.0)
    return comb / (p1 + p2)


def _merge_kernel(*refs, route):
    if route:
        (ya_ref, yb_ref, yc_ref, gp_ref, wb_ref, wo_ref, x_ref, mod_ref, nw_ref,
         rw_ref, rb_ref, xo_ref, h_ref, cmb_ref) = refs
    else:
        (ya_ref, yb_ref, yc_ref, gp_ref, wb_ref, wo_ref, x_ref, mod_ref, nw_ref,
         xo_ref, h_ref) = refs
    d = x_ref.shape[1]
    merged = None
    for n, y_ref in enumerate((ya_ref, yb_ref, yc_ref)):
        gate = _sigmoid(gp_ref[:, n * d:(n + 1) * d].astype(F32))
        term = gate * _dot(y_ref[...], wb_ref[n])
        merged = term if merged is None else merged + term
    m = mod_ref[0]
    xn = x_ref[...] + m[2:3] * _dot(merged.astype(BF16), wo_ref[...])
    xo_ref[...] = xn
    h2 = _norm_mod(xn, nw_ref[...], m[4:5], m[3:4])
    h_ref[...] = h2.astype(BF16)
    if route:
        h_hi, h_mid, _ = _split3(h2)
        r_hi, r_mid, _ = _split3(rw_ref[...])
        logits = _dot(h_hi, r_hi) + _dot(h_mid, r_hi) + _dot(h_hi, r_mid) + rb_ref[...]
        cmb_ref[...] = _top2_combine(logits)


def _merge_call(ya, yb, yc, gp, wb, wo, x, mod, nw, router, seq, tm):
    t, d = x.shape
    bw = ya.shape[1]
    per_b = seq // tm
    route = router is not None
    tok = lambda w: pl.BlockSpec((tm, w), lambda i: (i, 0))
    const2 = lambda s: pl.BlockSpec(s, lambda i: (0, 0))
    in_specs = [tok(bw), tok(bw), tok(bw), tok(3 * d),
                pl.BlockSpec((3, bw, d), lambda i: (0, 0, 0)), const2((d, d)), tok(d),
                pl.BlockSpec((1, 6, d), lambda i: (i // per_b, 0, 0)), const2((1, d))]
    out_specs = [tok(d), tok(d)]
    out_shape = [jax.ShapeDtypeStruct((t, d), F32), jax.ShapeDtypeStruct((t, d), BF16)]
    args = [ya, yb, yc, gp, wb, wo, x, mod, nw]
    if route:
        in_specs += [const2((d, LANES)), const2((1, LANES))]
        out_specs.append(tok(LANES))
        out_shape.append(jax.ShapeDtypeStruct((t, LANES), F32))
        args += list(router)
    return pl.pallas_call(
        functools.partial(_merge_kernel, route=route),
        grid=(t // tm,),
        in_specs=in_specs, out_specs=out_specs, out_shape=out_shape,
        compiler_params=_cparams(("arbitrary",)),
        name="merge_out",
    )(*args)


def _finish(acc, x_ref, mod_ref, xo_ref, nxt):
    xn = x_ref[...] + mod_ref[0][5:6] * acc
    xo_ref[...] = xn
    if nxt is not None:
        nw_ref, modn_ref, hn_ref = nxt
        mn = modn_ref[0]
        hn_ref[...] = _norm_mod(xn, nw_ref[...], mn[1:2], mn[0:1]).astype(BF16)


def _ffn_kernel(*refs, has_next):
    if has_next:
        h_ref, w1_ref, w3_ref, w2_ref, x_ref, mod_ref, nw_ref, modn_ref, xo_ref, hn_ref, acc_ref = refs
        nxt = (nw_ref, modn_ref, hn_ref)
    else:
        h_ref, w1_ref, w3_ref, w2_ref, x_ref, mod_ref, xo_ref, acc_ref = refs
        nxt = None
    f = pl.program_id(1)

    @pl.when(f == 0)
    def _():
        acc_ref[...] = jnp.zeros(acc_ref.shape, F32)

    h = h_ref[...]
    a = _dot(h, w1_ref[...])
    act = a * _sigmoid(a) * _dot(h, w3_ref[...])
    acc_ref[...] += _dot(act.astype(BF16), w2_ref[...])

    @pl.when(f == pl.num_programs(1) - 1)
    def _():
        _finish(acc_ref[...], x_ref, mod_ref, xo_ref, nxt)


def _ffn_call(h, w1, w3, w2, x, mod, nxt, seq, tm, tf):
    t, d = x.shape
    ff = w1.shape[1]
    per_b = seq // tm
    tok = lambda: pl.BlockSpec((tm, d), lambda i, f: (i, 0))
    modspec = lambda: pl.BlockSpec((1, 6, d), lambda i, f: (i // per_b, 0, 0))
    in_specs = [tok(), pl.BlockSpec((d, tf), lambda i, f: (0, f)),
                pl.BlockSpec((d, tf), lambda i, f: (0, f)),
                pl.BlockSpec((tf, d), lambda i, f: (f, 0)), tok(), modspec()]
    out_specs = [tok()]
    out_shape = [jax.ShapeDtypeStruct((t, d), F32)]
    args = [h, w1, w3, w2, x, mod]
    if nxt is not None:
        in_specs += [pl.BlockSpec((1, d), lambda i, f: (0, 0)), modspec()]
        out_specs.append(tok())
        out_shape.append(jax.ShapeDtypeStruct((t, d), BF16))
        args += list(nxt)
    return pl.pallas_call(
        functools.partial(_ffn_kernel, has_next=nxt is not None),
        grid=(t // tm, ff // tf),
        in_specs=in_specs, out_specs=out_specs, out_shape=out_shape,
        scratch_shapes=[pltpu.VMEM((tm, d), F32)],
        compiler_params=_cparams(("arbitrary", "arbitrary")),
        name="ffn_swiglu",
    )(*args)


def _moe_kernel(h_ref, cmb_ref, w13_ref, w2_ref, y_ref,
                acc_ref, xg_ref, ya_ref, rk_ref, rkt_ref, cnt_ref):
    e = pl.program_id(1)
    f = pl.program_id(2)
    tm = h_ref.shape[0]

    @pl.when((e == 0) & (f == 0))
    def _():
        r = lax.broadcasted_iota(jnp.int32, (tm, tm), 0)
        c = lax.broadcasted_iota(jnp.int32, (tm, tm), 1)
        before = jnp.where(c < r, 1.0, 0.0).astype(BF16)
        sel = cmb_ref[...] > 0.0
        rank = _dot(before, jnp.where(sel, 1.0, 0.0).astype(BF16))
        rk = jnp.where(sel, rank, -1.0)
        rk_ref[...] = rk
        rkt_ref[...] = rk.T
        cnt_ref[...] = jnp.sum(jnp.where(sel, 1.0, 0.0), axis=0, keepdims=True)
        acc_ref[...] = jnp.zeros(acc_ref.shape, F32)

    lane1 = lax.broadcasted_iota(jnp.int32, (1, LANES), 1)
    n_e = jnp.sum(jnp.where(lane1 == e, cnt_ref[...], 0.0)).astype(jnp.int32)

    def loop(n_blocks, body):
        if isinstance(n_blocks, int):
            for sb in range(n_blocks):
                body(sb, 0)
        else:
            lax.fori_loop(0, n_blocks, body, 0)

    def run(cap, n_blocks):
        cap_pad = -(-cap // LANES) * LANES

        def row0(sb):
            return sb * cap if isinstance(sb, int) else pl.multiple_of(sb * cap, cap)

        @pl.when(f == 0)
        def _():
            rank_row = rkt_ref[pl.ds(e, 1), :]

            def gather(sb, carry):
                r0 = row0(sb)
                slot = (r0 + lax.broadcasted_iota(jnp.int32, (cap, tm), 0)).astype(F32)
                onehot = jnp.where(rank_row == slot, 1.0, 0.0).astype(BF16)
                xg_ref[pl.ds(r0, cap), :] = _dot(onehot, h_ref[...]).astype(BF16)
                return carry

            loop(n_blocks, gather)

        def expert(sb, carry):
            r0 = row0(sb)
            xs = xg_ref[pl.ds(r0, cap), :]
            ab = _dot(xs, w13_ref[0])
            tf = w2_ref.shape[1]
            a = ab[:, :tf]
            act = a * _sigmoid(a) * ab[:, tf:]
            part = _dot(act.astype(BF16), w2_ref[0])

            @pl.when(f == 0)
            def _():
                ya_ref[pl.ds(r0, cap), :] = part

            @pl.when(f > 0)
            def _():
                ya_ref[pl.ds(r0, cap), :] += part

            return carry

        loop(n_blocks, expert)

        @pl.when(f == pl.num_programs(2) - 1)
        def _():
            lane = lax.broadcasted_iota(jnp.int32, (tm, LANES), 1)
            rank_col = jnp.sum(jnp.where(lane == e, rk_ref[...], 0.0), axis=1, keepdims=True)
            w_col = jnp.sum(jnp.where(lane == e, cmb_ref[...], 0.0), axis=1, keepdims=True)
            col = lax.broadcasted_iota(jnp.int32, (tm, cap_pad), 1)

            def scatter(sb, carry):
                r0 = row0(sb)
                ys = ya_ref[pl.ds(r0, cap), :].astype(BF16)
                if cap_pad > cap:
                    ys = jnp.concatenate([ys, jnp.zeros((cap_pad - cap, ys.shape[1]), BF16)], axis=0)
                hit = (rank_col == (r0 + col).astype(F32)) & (col < cap)
                acc_ref[...] += w_col * _dot(jnp.where(hit, 1.0, 0.0).astype(BF16), ys)
                return carry

            loop(n_blocks, scatter)

    for idx, cap in enumerate(MOE_CAPS):
        lower = MOE_CAPS[idx - 1] if idx else 0
        if idx == len(MOE_CAPS) - 1:
            pl.when(n_e > lower)(functools.partial(run, cap, (n_e + cap - 1) // cap))
        else:
            pl.when((n_e > lower) & (n_e <= cap))(functools.partial(run, cap, 1))

    @pl.when((e == pl.num_programs(1) - 1) & (f == pl.num_programs(2) - 1))
    def _():
        y_ref[...] = acc_ref[...].astype(y_ref.dtype)


def _moe_pack_w13(w1, w3):
    ff = w1.shape[-1]
    tf = _pick(ff, (MOE_TF,))
    parts = []
    for f in range(ff // tf):
        parts += [w1[:, :, f * tf:(f + 1) * tf].astype(BF16), w3[:, :, f * tf:(f + 1) * tf].astype(BF16)]
    return jnp.concatenate(parts, axis=-1)


def _moe_call(h, cmb, w13, w2, tm):
    t, d = h.shape
    ne, ff, _ = w2.shape
    tf = _pick(ff, (MOE_TF,))
    rows = -(-tm // MOE_CAPS[-1]) * MOE_CAPS[-1]
    return pl.pallas_call(
        _moe_kernel,
        grid=(t // tm, ne, ff // tf),
        in_specs=[pl.BlockSpec((tm, d), lambda i, e, f: (i, 0)),
                  pl.BlockSpec((tm, LANES), lambda i, e, f: (i, 0)),
                  pl.BlockSpec((1, d, 2 * tf), lambda i, e, f: (e, 0, f)),
                  pl.BlockSpec((1, tf, d), lambda i, e, f: (e, f, 0))],
        out_specs=pl.BlockSpec((tm, d), lambda i, e, f: (i, 0)),
        out_shape=jax.ShapeDtypeStruct((t, d), BF16),
        scratch_shapes=[pltpu.VMEM((tm, d), F32),
                        pltpu.VMEM((rows, d), BF16),
                        pltpu.VMEM((rows, d), F32),
                        pltpu.VMEM((tm, LANES), F32),
                        pltpu.VMEM((LANES, tm), F32),
                        pltpu.VMEM((1, LANES), F32)],
        compiler_params=_cparams(("arbitrary", "arbitrary", "arbitrary")),
        name="moe_top2",
    )(h, cmb, w13, w2)


def _resid_kernel(*refs, has_next):
    if has_next:
        y_ref, x_ref, mod_ref, nw_ref, modn_ref, xo_ref, hn_ref = refs
        nxt = (nw_ref, modn_ref, hn_ref)
    else:
        y_ref, x_ref, mod_ref, xo_ref = refs
        nxt = None
    _finish(y_ref[...].astype(F32), x_ref, mod_ref, xo_ref, nxt)


def _resid_call(y, x, mod, nxt, seq, tm):
    t, d = x.shape
    per_b = seq // tm
    tok = lambda: pl.BlockSpec((tm, d), lambda i: (i, 0))
    modspec = lambda: pl.BlockSpec((1, 6, d), lambda i: (i // per_b, 0, 0))
    in_specs = [tok(), tok(), modspec()]
    out_specs = [tok()]
    out_shape = [jax.ShapeDtypeStruct((t, d), F32)]
    args = [y, x, mod]
    if nxt is not None:
        in_specs += [pl.BlockSpec((1, d), lambda i: (0, 0)), modspec()]
        out_specs.append(tok())
        out_shape.append(jax.ShapeDtypeStruct((t, d), BF16))
        args += list(nxt)
    return pl.pallas_call(
        functools.partial(_resid_kernel, has_next=nxt is not None),
        grid=(t // tm,),
        in_specs=in_specs, out_specs=out_specs, out_shape=out_shape,
        compiler_params=_cparams(("arbitrary",)),
        name="moe_residual",
    )(*args)


def _attn_col_perm():
    hq = DA_HEADS * DA_DK
    idx = []
    for base in (0, 2 * hq):
        for h in range(DA_HEADS):
            idx += list(range(base + h * DA_DK, base + (h + 1) * DA_DK))
            idx += list(range(base + hq + h * DA_DK, base + hq + (h + 1) * DA_DK))
    idx += list(range(4 * hq, 4 * hq + DA_HEADS * DA_DV))
    return np.asarray(idx, np.int32)


def _pick(n, pref):
    for c in pref:
        if n % c == 0:
            return c
    return n


def kernel(x, c, w_mod, b_mod, norm1_w, norm2_w, w_in, c_conv_w, c_conv_b, a_qnorm_w, a_knorm_w,
           a_lambda_q1, a_lambda_k1, a_lambda_q2, a_lambda_k2, a_subln_w, b_lb_logits, b_gnorm_w,
           c_igate_b, c_fgate_b, c_norm_w, w_branch, w_out, ffn_w1, ffn_w3, ffn_w2,
           moe_router_w, moe_router_b, moe_w1, moe_w3, moe_w2):
    batch, seq, d = x.shape
    depth = w_in.shape[0]
    t = batch * seq
    n_a = 4 * DA_HEADS * DA_DK + DA_HEADS * DA_DV
    n_b = 4 * HG_HEADS * HG_D
    n_c = 4 * ML_HEADS * ML_D + 2 * ML_HEADS
    n_c_pad = -(-n_c // LANES) * LANES
    tm = _pick(seq, (1024, 512, 256))
    tq = _pick(seq, (ATT_TQ,))
    ts_h = _pick(seq, (512, 256, 128))

    c8 = jnp.zeros((8, d), F32).at[:batch].set(c)
    mod = _mod_call(c8, w_mod, b_mod.reshape(depth, 1, 6 * d))[:, :batch].reshape(depth, batch, 6, d)

    lb_all = jnp.cumsum(jax.nn.softmax(b_lb_logits.astype(F32), axis=0), axis=0)
    lb_all = lb_all - lb_all[:1]
    slopes = jnp.asarray(_attn_slopes(), F32)

    xf = x.reshape(t, d)
    h = _normmod_call(xf, norm1_w[0].reshape(1, d), mod[0], seq, tm)
    perm_a = _attn_col_perm()
    w_bf = lax.optimization_barrier(w_in.astype(BF16))
    w_a_all = w_bf[:, :, :n_a][:, :, perm_a]
    w_b_all = w_bf[:, :, n_a:n_a + n_b]
    w_c_all = jnp.pad(w_bf[:, :, n_a + n_b:n_a + n_b + n_c], ((0, 0), (0, 0), (0, n_c_pad - n_c)))
    w_g_all = w_bf[:, :, n_a + n_b + n_c:]
    for l in range(depth):
        lam_init = 0.8 - 0.6 * math.exp(-0.3 * l)
        w_a, w_b, w_c, w_g = w_a_all[l], w_b_all[l], w_c_all[l], w_g_all[l]
        bias_c = jnp.zeros((1, n_c_pad), F32)
        bias_c = bias_c.at[0, n_c - 2 * ML_HEADS:n_c - ML_HEADS].set(c_igate_b[l])
        bias_c = bias_c.at[0, n_c - ML_HEADS:n_c].set(c_fgate_b[l])
        pc, pc_gates = _matmul_tail(h, w_c, bias_c, tm, n_c_pad - (n_c - 2 * ML_HEADS))

        qw = (jnp.tile(a_qnorm_w[l], 2 * DA_HEADS) * (DA_DK ** -0.5 * LOG2E)).reshape(1, -1)
        kw = jnp.tile(a_knorm_w[l], 2 * DA_HEADS).reshape(1, -1)
        kn, qt, vt = _attn_prep_call(h, w_a, qw, kw, batch, seq, _pick(seq, (512, 256)))
        lam = (jnp.exp(jnp.sum(a_lambda_q1[l] * a_lambda_k1[l]))
               - jnp.exp(jnp.sum(a_lambda_q2[l] * a_lambda_k2[l])) + lam_init)
        scal = jnp.concatenate([slopes, jnp.stack([lam, jnp.asarray(1.0 - lam_init, F32)])]).astype(F32)
        y_a = _attn_call(scal, qt, kn, vt, a_subln_w[l].reshape(1, -1), batch, seq, tq)

        gates = pc_gates[:, :2 * ML_HEADS].reshape(batch, seq, 2, ML_HEADS)
        gates = gates.transpose(2, 0, 3, 1).reshape(2, batch, ML_HEADS, seq // ML_CHUNK, ML_CHUNK)
        y_c, pg, pb = _mlstm_call(pc, gates[0], gates[1], c_conv_w[l], c_conv_b[l].reshape(1, -1),
                                  c_norm_w[l].reshape(1, -1), h, w_g, w_b, batch, seq)

        lb = lb_all[l].reshape(HG_HEADS, 1, HG_D)
        lbs = jnp.concatenate([jnp.log(lb), jnp.log1p(-lb)], axis=1)
        y_b = _hgrn_call(pb, lbs, b_gnorm_w[l].reshape(1, -1), batch, seq, ts_h)

        dense = l % 2 == 0
        router = None
        if not dense:
            rw = jnp.pad(moe_router_w[l // 2], ((0, 0), (0, LANES - N_EXPERTS)))
            rb = jnp.pad(moe_router_b[l // 2], (0, LANES - N_EXPERTS)).reshape(1, LANES)
            router = (rw, rb)
        outs = _merge_call(y_a, y_b, y_c, pg, w_branch[l].astype(BF16), w_out[l].astype(BF16), xf,
                           mod[l], norm2_w[l].reshape(1, d), router, seq, 512)
        xf, h2 = outs[0], outs[1]
        nxt = None if l == depth - 1 else (norm1_w[l + 1].reshape(1, d), mod[l + 1])
        if dense:
            res = _ffn_call(h2, ffn_w1[l // 2].astype(BF16), ffn_w3[l // 2].astype(BF16),
                            ffn_w2[l // 2].astype(BF16), xf, mod[l], nxt, seq, tm, FFN_TF)
        else:
            y = _moe_call(h2, outs[2], _moe_pack_w13(moe_w1[l // 2], moe_w3[l // 2]),
                          moe_w2[l // 2].astype(BF16), tm)
            res = _resid_call(y, xf, mod[l], nxt, seq, tm)
        xf = res[0]
        if nxt is not None:
            h = res[1]
    return xf.reshape(batch, seq, d)
```

```python
import functools
import math

import numpy as np
import jax
import jax.numpy as jnp
from jax import lax
from jax.experimental import pallas as pl
from jax.experimental.pallas import tpu as pltpu

F32 = jnp.float32
BF16 = jnp.bfloat16
NORM_EPS = 1e-6

CHUNK = 64
DA_HEADS, DA_DK, DA_DV = 4, 64, 128
HG_HEADS, HG_D = 4, 128
ML_HEADS, ML_D, ML_CONV = 4, 128, 4
N_EXPERTS = 8

LANES = 128
VMEM_LIMIT = 56 * 1024 * 1024

ATT_TQ = 256
ATT_HEADS_PER_STEP = 2
ATT_PAD_ROWS = 16
LOG2E = math.log2(math.e)
ATT_BOUND_MARGIN = 1.01
ATT_BOUND_EPS = 1e-3
ATT_BOUND_MAX = 40.0
ATT_ZERO_EXP = 160.0
HG_CHUNK = 128
HG_SUB = 32
HG_SAFE_DECAY = 80.0
ML_CHUNK = 256
MOE_CAPS = (256, 288, 320, 352, 384)
MOE_TF = 1408
FFN_TF = 256
MERGE_ROW_BLOCKS = 2


def _cparams(sem, vmem=VMEM_LIMIT):
    return pltpu.CompilerParams(dimension_semantics=sem, vmem_limit_bytes=vmem)


def _dot(a, b):
    return jnp.dot(a, b, preferred_element_type=F32)


def _dot_nt(a, b):
    return lax.dot_general(a, b, (((1,), (1,)), ((), ())), preferred_element_type=F32)


def _split3(x):
    hi = x.astype(BF16)
    r1 = x - hi.astype(F32)
    mid = r1.astype(BF16)
    lo = (r1 - mid.astype(F32)).astype(BF16)
    return hi, mid, lo


def _norm_mod(x, w, scale, shift):
    y = x * lax.rsqrt(jnp.mean(x * x, axis=-1, keepdims=True) + NORM_EPS) * w
    return y * (1.0 + scale) + shift


def _sigmoid(x):
    return 0.5 * jnp.tanh(0.5 * x) + 0.5


def _log1p_exp_neg(z):
    return jnp.log(1.0 + jnp.exp(-z))


def _log_sigmoid(x):
    return jnp.minimum(x, 0.0) - _log1p_exp_neg(jnp.abs(x))


def _mod_kernel(c_ref, w_ref, b_ref, o_ref):
    cnd = c_ref[...]
    cnd = cnd * _sigmoid(cnd)
    o_ref[0] = _dot(cnd.astype(BF16), w_ref[0].astype(BF16)) + b_ref[0]


def _mod_call(c8, w_mod, b_mod):
    depth, d, n = w_mod.shape
    tn = 1536 if n % 1536 == 0 else n
    return pl.pallas_call(
        _mod_kernel,
        grid=(depth, n // tn),
        in_specs=[pl.BlockSpec((8, d), lambda l, j: (0, 0)),
                  pl.BlockSpec((1, d, tn), lambda l, j: (l, 0, j)),
                  pl.BlockSpec((1, 1, tn), lambda l, j: (l, 0, j))],
        out_specs=pl.BlockSpec((1, 8, tn), lambda l, j: (l, 0, j)),
        out_shape=jax.ShapeDtypeStruct((depth, 8, n), F32),
        compiler_params=_cparams(("arbitrary", "arbitrary")),
        name="adaln_mod",
    )(c8, w_mod, b_mod)


def _normmod_kernel(x_ref, nw_ref, mod_ref, h_ref):
    m = mod_ref[0]
    h_ref[...] = _norm_mod(x_ref[...], nw_ref[...], m[1:2], m[0:1]).astype(BF16)


def _normmod_call(x, nw, mod, seq, tm):
    t, d = x.shape
    per_b = seq // tm
    return pl.pallas_call(
        _normmod_kernel,
        grid=(t // tm,),
        in_specs=[pl.BlockSpec((tm, d), lambda i: (i, 0)),
                  pl.BlockSpec((1, d), lambda i: (0, 0)),
                  pl.BlockSpec((1, 6, d), lambda i: (i // per_b, 0, 0))],
        out_specs=pl.BlockSpec((tm, d), lambda i: (i, 0)),
        out_shape=jax.ShapeDtypeStruct((t, d), BF16),
        compiler_params=_cparams(("arbitrary",)),
        name="prenorm_mod",
    )(x, nw, mod)


def _mm_tail_kernel(x_ref, w_ref, b_ref, o_ref, tail_ref):
    res = _dot(x_ref[...], w_ref[...]) + b_ref[...]
    main = o_ref.shape[1]
    o_ref[...] = res[:, :main]
    tail_ref[...] = res[:, main:]


def _matmul_tail(x, w, bias, tm, tail):
    t, k = x.shape
    n = w.shape[1]
    return pl.pallas_call(
        _mm_tail_kernel,
        grid=(t // tm,),
        in_specs=[pl.BlockSpec((tm, k), lambda i: (i, 0)),
                  pl.BlockSpec((k, n), lambda i: (0, 0)),
                  pl.BlockSpec((1, n), lambda i: (0, 0))],
        out_specs=[pl.BlockSpec((tm, n - tail), lambda i: (i, 0)),
                   pl.BlockSpec((tm, tail), lambda i: (i, 0))],
        out_shape=[jax.ShapeDtypeStruct((t, n - tail), F32),
                   jax.ShapeDtypeStruct((t, tail), F32)],
        compiler_params=_cparams(("arbitrary",)),
        name="in_proj_c",
    )(x, w, bias)


def _attn_prep_kernel(x_ref, w_ref, qw_ref, kw_ref, kn_ref, qt_ref, vt_ref):
    hw = DA_HEADS * 2 * DA_DK
    a = _dot(x_ref[...], w_ref[...])
    r = lax.broadcasted_iota(jnp.int32, (hw, hw), 0) // DA_DK
    c = lax.broadcasted_iota(jnp.int32, (hw, hw), 1) // DA_DK
    group = jnp.where(r == c, 1.0, 0.0).astype(BF16)

    def qk_norm(z, w):
        ms = _dot((z * z).astype(BF16), group) * (1.0 / DA_DK)
        return z * lax.rsqrt(ms + NORM_EPS) * w

    qn = qk_norm(a[:, :hw], qw_ref[...])
    kn = qk_norm(a[:, hw:2 * hw], kw_ref[...])
    kn_ref[...] = kn.astype(BF16)
    qt_ref[0] = qn.T.astype(BF16)
    vt_ref[0] = a[:, 2 * hw:].T.astype(BF16)


def _attn_prep_call(x, w, qw, kw, batch, seq, ts):
    t, d = x.shape
    hw = DA_HEADS * 2 * DA_DK
    hv = DA_HEADS * DA_DV
    per_b = seq // ts
    return pl.pallas_call(
        _attn_prep_kernel,
        grid=(t // ts,),
        in_specs=[pl.BlockSpec((ts, d), lambda i: (i, 0)),
                  pl.BlockSpec((d, 2 * hw + hv), lambda i: (0, 0)),
                  pl.BlockSpec((1, hw), lambda i: (0, 0)),
                  pl.BlockSpec((1, hw), lambda i: (0, 0))],
        out_specs=[pl.BlockSpec((ts, hw), lambda i: (i, 0)),
                   pl.BlockSpec((1, hw, ts), lambda i: (i // per_b, 0, i % per_b)),
                   pl.BlockSpec((1, hv, ts), lambda i: (i // per_b, 0, i % per_b))],
        out_shape=[jax.ShapeDtypeStruct((t, hw), BF16),
                   jax.ShapeDtypeStruct((batch, hw, seq), BF16),
                   jax.ShapeDtypeStruct((batch, hv, seq), BF16)],
        compiler_params=_cparams(("arbitrary",)),
        name="attn_prep",
    )(x, w, qw, kw)


def _attn_slopes():
    return [LOG2E * 2.0 ** (-8.0 * (h + 1) / DA_HEADS) for h in range(DA_HEADS)]


def _attn_window(head, tq):
    return int((ATT_ZERO_EXP / _attn_slopes()[head] - 1.0) // tq) + 1


def _attn_kernel(sc_ref, qt_ref, k_ref, vt_ref, sw_ref, o_ref,
                 qz_ref, m_ref, acc_ref, vs_ref, kmax_ref, s_ref, s2_ref, *, tq, seq):
    hp = ATT_HEADS_PER_STEP
    dv = DA_DV
    g = pl.program_id(1)
    i = pl.program_id(2)
    lam = sc_ref[DA_HEADS]
    out_scale = sc_ref[DA_HEADS + 1]
    slopes = [sc_ref[hp * g + hh] for hh in range(hp)]
    row16 = lax.broadcasted_iota(jnp.int32, (ATT_PAD_ROWS, tq), 0)
    ones_rows = jnp.where(row16 == 0, 1.0, 0.0).astype(BF16)

    half = lax.broadcasted_iota(jnp.int32, (1, 2 * DA_DK), 1) < DA_DK

    @pl.when(i == 0)
    def _():
        pos = lax.broadcasted_iota(jnp.int32, (1, seq), 1) & (tq - 1)
        rel = (pos - (tq - 1)).astype(F32)
        sub16 = lax.broadcasted_iota(jnp.int32, (ATT_PAD_ROWS, seq), 0)
        gr = lax.broadcasted_iota(jnp.int32, (2 * DA_DK, 2 * DA_DK), 0) // DA_DK
        gc = lax.broadcasted_iota(jnp.int32, (2 * DA_DK, 2 * DA_DK), 1) // DA_DK
        group = jnp.where(gr == gc, 1.0, 0.0).astype(BF16)
        for hh in range(hp):
            w = jnp.exp2(slopes[hh] * rel)
            vs_ref[hh, 0:dv, :] = (vt_ref[0, hh * dv:(hh + 1) * dv, :].astype(F32) * w).astype(BF16)
            vs_ref[hh, dv:dv + ATT_PAD_ROWS, :] = jnp.where(sub16 == 0, w, 0.0).astype(BF16)

            def knorm(n, best):
                kc = k_ref[pl.ds(pl.multiple_of(n * tq, tq), tq),
                           hh * 2 * DA_DK:(hh + 1) * 2 * DA_DK].astype(F32)
                return jnp.maximum(best, jnp.max(_dot((kc * kc).astype(BF16), group), axis=0, keepdims=True))

            k2 = lax.fori_loop(0, seq // tq, knorm, jnp.zeros((1, 2 * DA_DK), F32))
            kmax_ref[2 * hh] = jnp.max(jnp.where(half, k2, 0.0), axis=1, keepdims=True)
            kmax_ref[2 * hh + 1] = jnp.max(jnp.where(half, 0.0, k2), axis=1, keepdims=True)

    row = lax.broadcasted_iota(jnp.int32, (2 * DA_DK, tq), 0)
    qq_row = lax.broadcasted_iota(jnp.int32, (1, tq), 1).astype(F32)
    bound_max = jnp.zeros((), F32)
    firsts = []

    def scores(hh, blk):
        start = pl.multiple_of(blk * tq, tq)
        return _dot(k_ref[pl.ds(start, tq), hh * 2 * DA_DK:(hh + 1) * 2 * DA_DK], qz_ref[hh])

    for hh in range(hp):
        qt = qt_ref[0, hh * 2 * DA_DK:(hh + 1) * 2 * DA_DK, :]
        zero = jnp.zeros_like(qt)
        q1 = jnp.where(row < DA_DK, qt, zero)
        q2 = jnp.where(row >= DA_DK, qt, zero)
        qz_ref[hh] = jnp.concatenate([q1, q2], axis=1)
        window = jnp.int32(_attn_window(hh, tq))
        for grp in range(1, DA_HEADS // hp):
            window = jnp.where(g == grp, _attn_window(grp * hp + hh, tq), window)
        firsts.append((jnp.maximum(i - window, 0) // 2) * 2)
        s_ref[hh] = scores(hh, firsts[hh])
        s2_ref[hh] = scores(hh, jnp.minimum(firsts[hh] + 1, i))
        qn = jnp.concatenate(
            [jnp.sum(jnp.square(q1.astype(F32)), axis=0, keepdims=True) * kmax_ref[2 * hh],
             jnp.sum(jnp.square(q2.astype(F32)), axis=0, keepdims=True) * kmax_ref[2 * hh + 1]], axis=1)
        bound = jnp.sqrt(qn) * ATT_BOUND_MARGIN + ATT_BOUND_EPS
        bound_max = jnp.maximum(bound_max, jnp.max(bound))
        m_ref[hh] = bound + slopes[hh] * jnp.concatenate([qq_row, qq_row], axis=1)
    acc_ref[...] = jnp.zeros(acc_ref.shape, F32)

    kk = lax.broadcasted_iota(jnp.int32, (tq, 2 * tq), 0)
    cc = lax.broadcasted_iota(jnp.int32, (tq, 2 * tq), 1)
    qq = jnp.where(cc >= tq, cc - tq, cc)
    dist = (qq - jnp.abs(qq - kk)).astype(F32)
    visible = (kk // CHUNK) <= (qq // CHUNK)

    def diag_values(hh):
        start = pl.multiple_of(i * tq, tq)
        return jnp.concatenate([vt_ref[0, hh * dv:(hh + 1) * dv, pl.ds(start, tq)], ones_rows], axis=0)

    @pl.when(bound_max <= ATT_BOUND_MAX)
    def _():
        def probs(hh, blk, s):
            last_key = ((blk + 1 - i) * tq - 1).astype(F32)
            return jnp.exp2(s - (m_ref[hh] - slopes[hh] * last_key)).astype(BF16)

        def body(heads, jj, carry):
            blk = 2 * jj
            start = pl.multiple_of(blk * tq, 2 * tq)
            for hh in heads:
                p = jnp.concatenate([probs(hh, blk, s_ref[hh]), probs(hh, blk + 1, s2_ref[hh])], axis=0)
                s_ref[hh] = scores(hh, jnp.minimum(blk + 2, i))
                s2_ref[hh] = scores(hh, jnp.minimum(blk + 3, i))
                acc_ref[hh] += _dot(vs_ref[hh, :, pl.ds(start, 2 * tq)], p)
            return carry

        for hh in range(hp - 1, 0, -1):
            lax.fori_loop(firsts[hh] // 2, firsts[hh - 1] // 2,
                          functools.partial(body, tuple(range(hh, hp))), 0)
        lax.fori_loop(firsts[0] // 2, i // 2, functools.partial(body, tuple(range(hp))), 0)

        @pl.when(i % 2 == 1)
        def _():
            start = pl.multiple_of((i - 1) * tq, tq)
            for hh in range(hp):
                acc_ref[hh] += _dot(vs_ref[hh, :, pl.ds(start, tq)], probs(hh, i - 1, s_ref[hh]))
                s_ref[hh] = s2_ref[hh]

        for hh in range(hp):
            s = jnp.where(visible, s_ref[hh] + slopes[hh] * dist, -jnp.inf)
            acc_ref[hh] += _dot(diag_values(hh), jnp.exp2(s - m_ref[hh]).astype(BF16))

    @pl.when(bound_max > ATT_BOUND_MAX)
    def _():
        m_ref[...] = jnp.full(m_ref.shape, -jnp.inf, F32)

        def update(hh, s, bound_shift, v_aug):
            m_old = m_ref[hh]
            m_new = jnp.maximum(m_old, jnp.max(s, axis=0, keepdims=True) + bound_shift)
            p = jnp.exp2(s - (m_new - bound_shift))
            acc_ref[hh] = jnp.exp2(m_old - m_new) * acc_ref[hh] + _dot(v_aug, p.astype(BF16))
            m_ref[hh] = m_new

        def body(j, carry):
            start = pl.multiple_of(j * tq, tq)
            last_key = ((j + 1 - i) * tq - 1).astype(F32)
            for hh in range(hp):
                update(hh, scores(hh, j), slopes[hh] * last_key, vs_ref[hh, :, pl.ds(start, tq)])
            return carry

        lax.fori_loop(0, i, body, 0)
        for hh in range(hp):
            s = jnp.where(visible, scores(hh, i) + slopes[hh] * dist, -jnp.inf)
            update(hh, s, jnp.zeros((), F32), diag_values(hh))

    outs = []
    for hh in range(hp):
        acc = acc_ref[hh]
        o2 = acc[:dv, :] * (1.0 / acc[dv:dv + 1, :])
        o = (o2[:, :tq] - lam * o2[:, tq:]).T
        o = o * lax.rsqrt(jnp.mean(o * o, axis=-1, keepdims=True) + NORM_EPS) * sw_ref[...]
        outs.append(o * out_scale)
    o_ref[...] = jnp.concatenate(outs, axis=1).astype(o_ref.dtype)


def _attn_call(scal, qt, kn, vt, sw, batch, seq, tq):
    t = kn.shape[0]
    nq = seq // tq
    hp = ATT_HEADS_PER_STEP
    return pl.pallas_call(
        functools.partial(_attn_kernel, tq=tq, seq=seq),
        grid=(batch, DA_HEADS // hp, nq),
        in_specs=[pl.BlockSpec(memory_space=pltpu.SMEM),
                  pl.BlockSpec((1, hp * 2 * DA_DK, tq), lambda b, g, i: (b, g, i)),
                  pl.BlockSpec((seq, hp * 2 * DA_DK), lambda b, g, i: (b, g)),
                  pl.BlockSpec((1, hp * DA_DV, seq), lambda b, g, i: (b, g, 0)),
                  pl.BlockSpec((1, DA_DV), lambda b, g, i: (0, 0))],
        out_specs=pl.BlockSpec((tq, hp * DA_DV), lambda b, g, i: (b * nq + i, g)),
        out_shape=jax.ShapeDtypeStruct((t, DA_HEADS * DA_DV), BF16),
        scratch_shapes=[pltpu.VMEM((hp, 2 * DA_DK, 2 * tq), BF16),
                        pltpu.VMEM((hp, 1, 2 * tq), F32),
                        pltpu.VMEM((hp, DA_DV + ATT_PAD_ROWS, 2 * tq), F32),
                        pltpu.VMEM((hp, DA_DV + ATT_PAD_ROWS, seq), BF16),
                        pltpu.VMEM((2 * hp, 1, 1), F32),
                        pltpu.VMEM((hp, tq, 2 * tq), F32),
                        pltpu.VMEM((hp, tq, 2 * tq), F32)],
        compiler_params=_cparams(("arbitrary", "arbitrary", "arbitrary")),
        name="diff_attn",
    )(scal, qt, kn, vt, sw)


def _hgrn_kernel(q_ref, f_ref, i_ref, g_ref, lb_ref, gw_ref, o_ref,
                 st_ref, b_ref, qs_ref, ks_ref, oi_ref, *, n_chunks):
    c = HG_CHUNK
    sub = HG_SUB
    d = HG_D
    nh = HG_HEADS
    chains = [(bb, hh) for bb in range(q_ref.shape[0]) for hh in range(nh)]

    @pl.when(pl.program_id(0) == 0)
    def _():
        st_ref[...] = jnp.zeros(st_ref.shape, F32)

    gw = gw_ref[...]
    rr = lax.broadcasted_iota(jnp.int32, (c, c), 0)
    cc = lax.broadcasted_iota(jnp.int32, (c, c), 1)
    causal = cc <= rr
    tril = jnp.where(causal, 1.0, 0.0).astype(BF16)

    def chunk(n, carry):
        r0 = pl.multiple_of(n * c, c)
        gates = []
        for ch, (bb, hh) in enumerate(chains):
            cols = slice(hh * d, (hh + 1) * d)
            log_lb = lb_ref[hh, 0:1, :]
            q = q_ref[bb, pl.ds(r0, c), cols]
            q = q * _sigmoid(q)
            a = lb_ref[hh, 1:2, :] + _log_sigmoid(f_ref[bb, pl.ds(r0, c), cols])
            logf = jnp.maximum(log_lb, a) + _log1p_exp_neg(jnp.abs(log_lb - a))
            k = 1.0 - jnp.exp(logf)
            v = i_ref[bb, pl.ds(r0, c), cols]
            hi, mid, lo = _split3(logf)
            b = _dot(tril, hi) + _dot(tril, mid) + _dot(tril, lo)
            gates.append((q, k, v, b))

        heads = []
        decay = jnp.zeros((), F32)
        for ch, (q, k, v, b) in enumerate(gates):
            b_last = b[c - 1:c, :]
            st = st_ref[ch]
            o_inter = _dot_nt((q * jnp.exp(b)).astype(BF16), st.astype(BF16))
            k_hat = k * jnp.exp(b_last - b)
            st_ref[ch] = st * jnp.exp(b_last) + _dot(v.T.astype(BF16), k_hat.astype(BF16))
            betas = []
            for blk in range(c // sub):
                beta = jnp.zeros((1, d), F32) if blk == 0 else b[blk * sub - 1:blk * sub, :]
                betas.append(beta)
                b_end = b[(blk + 1) * sub - 1:(blk + 1) * sub, :]
                decay = jnp.maximum(decay, jnp.max(beta - b_end))
            heads.append((q, k, v, b, betas, o_inter))

        @pl.when(decay < HG_SAFE_DECAY)
        def _():
            atts = []
            for q, k, v, b, betas, _ in heads:
                rows = []
                for blk in range(c // sub):
                    beta = betas[blk]
                    q_t = q[blk * sub:(blk + 1) * sub, :] * jnp.exp(b[blk * sub:(blk + 1) * sub, :] - beta)
                    k_t = k * jnp.exp(jnp.minimum(beta - b, HG_SAFE_DECAY))
                    rows.append(_dot_nt(q_t.astype(BF16), k_t.astype(BF16)))
                atts.append(jnp.where(causal, jnp.concatenate(rows, axis=0), 0.0).astype(BF16))
            for ch, att in enumerate(atts):
                oi_ref[ch] = _dot(att, heads[ch][2].astype(BF16))

        @pl.when(decay >= HG_SAFE_DECAY)
        def _():
            ridx = lax.broadcasted_iota(jnp.int32, (c, d), 0)
            for ch, (q, k, v, b, _, _) in enumerate(heads):
                b_ref[...] = b
                qs_ref[...] = q
                ks_ref[...] = k

                def row(t, carry2):
                    bt = b_ref[pl.ds(t, 1), :]
                    e = jnp.exp(jnp.where(ridx <= t, bt - b_ref[...], -jnp.inf))
                    w = jnp.sum(qs_ref[pl.ds(t, 1), :] * ks_ref[...] * e, axis=1, keepdims=True)
                    oi_ref[ch, pl.ds(t, 1), :] = jnp.sum(w * v, axis=0, keepdims=True)
                    return carry2

                lax.fori_loop(0, c, row, 0)

        for ch, (bb, hh) in enumerate(chains):
            cols = slice(hh * d, (hh + 1) * d)
            o = heads[ch][5] + oi_ref[ch]
            o = o * lax.rsqrt(jnp.mean(o * o, axis=-1, keepdims=True) + NORM_EPS) * gw
            g = g_ref[bb, pl.ds(r0, c), cols]
            o_ref[bb, pl.ds(r0, c), cols] = (o * (g * _sigmoid(g))).astype(o_ref.dtype)
        return carry

    lax.fori_loop(0, n_chunks, chunk, 0)


def _hgrn_call(bproj, lbs, gw, batch, seq, ts):
    t = bproj.shape[0]
    d = HG_D
    nh = HG_HEADS
    bp3 = bproj.reshape(batch, seq, bproj.shape[1])
    spec = lambda off: pl.BlockSpec((batch, ts, nh * d), lambda i: (0, i, off))
    out = pl.pallas_call(
        functools.partial(_hgrn_kernel, n_chunks=ts // HG_CHUNK),
        grid=(seq // ts,),
        in_specs=[spec(0), spec(1), spec(2), spec(3),
                  pl.BlockSpec((nh, 2, d), lambda i: (0, 0, 0)),
                  pl.BlockSpec((1, d), lambda i: (0, 0))],
        out_specs=pl.BlockSpec((batch, ts, nh * d), lambda i: (0, i, 0)),
        out_shape=jax.ShapeDtypeStruct((batch, seq, nh * d), BF16),
        scratch_shapes=[pltpu.VMEM((batch * nh, d, d), F32),
                        pltpu.VMEM((HG_CHUNK, d), F32),
                        pltpu.VMEM((HG_CHUNK, d), F32),
                        pltpu.VMEM((HG_CHUNK, d), F32),
                        pltpu.VMEM((batch * nh, HG_CHUNK, d), F32)],
        compiler_params=_cparams(("arbitrary",)),
        name="hgrn2",
    )(bp3, bp3, bp3, bp3, lbs, gw)
    return out.reshape(t, nh * d)


def _mlstm_kernel(uq_ref, uk_ref, v_ref, op_ref, gi_ref, gf_ref, cwq_ref, cwk_ref,
                  cbq_ref, cbk_ref, nw_ref, hg_ref, wg_ref, wb_ref, o_ref, pg_ref, pb_ref,
                  xq_ref, xk_ref, c_ref, m_ref, bs_ref):
    L = ML_CHUNK
    n_gate = wg_ref.shape[1]
    n_hg = wb_ref.shape[1]

    def host(w_ref, out_ref, lo, hi):
        for bb in range(hg_ref.shape[0]):
            out_ref[bb, :, lo:hi] = _dot(hg_ref[bb], w_ref[:, lo:hi]).astype(out_ref.dtype)

    def gate_proj(part):
        host(wg_ref, pg_ref, part * n_gate // 3, (part + 1) * n_gate // 3)
        if part:
            host(wb_ref, pb_ref, (part - 1) * n_hg // 2, part * n_hg // 2)

    d = ML_D
    nh = ML_HEADS
    nb = uq_ref.shape[0]
    i = pl.program_id(0)
    rr = lax.broadcasted_iota(jnp.int32, (L, L), 0)
    cc = lax.broadcasted_iota(jnp.int32, (L, L), 1)

    @pl.when(i == 0)
    def _():
        c_ref[...] = jnp.zeros(c_ref.shape, F32)
        m_ref[...] = jnp.zeros(m_ref.shape, F32)
        xq_ref[:, 0:8, :] = jnp.zeros((nb, 8, nh * d), F32)
        xk_ref[:, 0:8, :] = jnp.zeros((nb, 8, nh * d), F32)
        upper = jnp.where(rr <= cc, 1.0, 0.0).astype(BF16)
        for ch in range(nb * nh):
            hi, mid, lo = _split3(_log_sigmoid(gf_ref[ch // nh, ch % nh]))
            bs_ref[ch] = _dot(hi, upper) + _dot(mid, upper) + _dot(lo, upper)

    def conv_silu(bb, u_ref, x_ref, w_ref, b_ref):
        x_ref[bb, 8:8 + L, :] = u_ref[bb]
        y = b_ref[...] + w_ref[ML_CONV - 1:ML_CONV, :] * x_ref[bb, 8:8 + L, :]
        for j in range(ML_CONV - 1):
            y = y + w_ref[j:j + 1, :] * x_ref[bb, 5 + j:5 + j + L, :]
        x_ref[bb, 0:8, :] = x_ref[bb, L:L + 8, :]
        return y * _sigmoid(y)

    gate_proj(0)
    q_all = [conv_silu(bb, uq_ref, xq_ref, cwq_ref, cbq_ref) for bb in range(nb)]
    k_all = [conv_silu(bb, uk_ref, xk_ref, cwk_ref, cbk_ref) * (d ** -0.5) for bb in range(nb)]
    lane = lax.broadcasted_iota(jnp.int32, (L, d), 1)
    ones_col = jnp.where(lane == 0, 1.0, 0.0)
    stage1 = []
    for ch in range(nb * nh):
        bb, hh = ch // nh, ch % nh
        cols = slice(hh * d, (hh + 1) * d)
        q = q_all[bb][:, cols].astype(BF16)
        kt = k_all[bb][:, cols].T
        v_aug = jnp.concatenate([v_ref[bb, :, cols], ones_col], axis=1).astype(BF16)
        c_aug = c_ref[ch]
        qk = _dot(q, kt.astype(BF16))
        qc = _dot(q, c_aug.astype(BF16))

        b_row = bs_ref[ch, pl.ds(i, 1), :]
        ig_row = gi_ref[bb, hh, pl.ds(i, 1), :]
        m_prev = m_ref[ch]
        g = b_row[:, L - 1:L]
        log_w = g - b_row + ig_row
        m_new = jnp.maximum(g + m_prev, jnp.max(log_w, axis=1, keepdims=True))
        w_s = jnp.exp(log_w - m_new)
        c_ref[ch] = jnp.exp(g + m_prev - m_new) * c_aug + _dot((kt * w_s).astype(BF16), v_aug)
        m_ref[ch] = m_new
        stage1.append((qk, qc, v_aug, b_row, ig_row, m_prev))

    gate_proj(1)
    stage2 = []
    for qk, qc, v_aug, b_row, ig_row, m_prev in stage1:
        b_col = jnp.sum(jnp.where(rr == cc, b_row, 0.0), axis=1, keepdims=True)
        log_d = jnp.where(cc <= rr, b_col + (ig_row - b_row), -jnp.inf)
        log_inter = b_col + m_prev
        m_t = jnp.maximum(log_inter, jnp.max(log_d, axis=1, keepdims=True))
        w_intra = jnp.exp(log_d - m_t) * qk
        tot = jnp.exp(log_inter - m_t) * qc + _dot(w_intra.astype(BF16), v_aug)
        stage2.append((tot, m_t))

    gate_proj(2)
    for ch, (tot, m_t) in enumerate(stage2):
        bb, hh = ch // nh, ch % nh
        cols = slice(hh * d, (hh + 1) * d)
        denom = jnp.maximum(jnp.abs(tot[:, d:d + 1]), jnp.exp(-m_t))
        hout = tot[:, :d] / denom
        hout = hout * lax.rsqrt(jnp.mean(hout * hout, axis=-1, keepdims=True) + NORM_EPS) * nw_ref[...]
        o_ref[bb, :, cols] = (hout * _sigmoid(op_ref[bb, :, cols])).astype(o_ref.dtype)


def _mlstm_call(cproj, gi, gf, conv_w, conv_b, nw, h, w_gate, w_hg, batch, seq):
    t = cproj.shape[0]
    d = ML_D
    nh = ML_HEADS
    L = ML_CHUNK
    nc = seq // L
    dm, n_gate = w_gate.shape
    n_hg = w_hg.shape[1]
    cp3 = cproj.reshape(batch, seq, cproj.shape[1])
    spec = lambda off: pl.BlockSpec((batch, L, nh * d), lambda i: (0, i, off))
    gspec = pl.BlockSpec((batch, nh, nc, L), lambda i: (0, 0, 0, 0))
    out, pg, pb = pl.pallas_call(
        _mlstm_kernel,
        grid=(nc,),
        in_specs=[spec(0), spec(1), spec(2), spec(3), gspec, gspec,
                  pl.BlockSpec((ML_CONV, nh * d), lambda i: (0, 0)),
                  pl.BlockSpec((ML_CONV, nh * d), lambda i: (0, 1)),
                  pl.BlockSpec((1, nh * d), lambda i: (0, 0)),
                  pl.BlockSpec((1, nh * d), lambda i: (0, 1)),
                  pl.BlockSpec((1, d), lambda i: (0, 0)),
                  pl.BlockSpec((batch, L, dm), lambda i: (0, i, 0)),
                  pl.BlockSpec((dm, n_gate), lambda i: (0, 0)),
                  pl.BlockSpec((dm, n_hg), lambda i: (0, 0))],
        out_specs=[pl.BlockSpec((batch, L, nh * d), lambda i: (0, i, 0)),
                   pl.BlockSpec((batch, L, n_gate), lambda i: (0, i, 0)),
                   pl.BlockSpec((batch, L, n_hg), lambda i: (0, i, 0))],
        out_shape=[jax.ShapeDtypeStruct((batch, seq, nh * d), BF16),
                   jax.ShapeDtypeStruct((batch, seq, n_gate), BF16),
                   jax.ShapeDtypeStruct((batch, seq, n_hg), F32)],
        scratch_shapes=[pltpu.VMEM((batch, L + 8, nh * d), F32),
                        pltpu.VMEM((batch, L + 8, nh * d), F32),
                        pltpu.VMEM((batch * nh, d, 2 * d), F32),
                        pltpu.VMEM((batch * nh, 1, 1), F32),
                        pltpu.VMEM((batch * nh, nc, L), F32)],
        compiler_params=_cparams(("arbitrary",)),
        name="mlstm",
    )(cp3, cp3, cp3, cp3, gi, gf, conv_w, conv_w, conv_b, conv_b, nw,
      h.reshape(batch, seq, dm), w_gate, w_hg)
    return out.reshape(t, nh * d), pg.reshape(t, n_gate), pb.reshape(t, n_hg)


def _top2_combine(logits):
    lane = lax.broadcasted_iota(jnp.int32, logits.shape, 1)
    lg = jnp.where(lane < N_EXPERTS, logits, -jnp.inf)
    ex = jnp.exp(lg - jnp.max(lg, axis=1, keepdims=True))
    probs = ex / jnp.sum(ex, axis=1, keepdims=True)
    p1 = jnp.max(probs, axis=1, keepdims=True)
    i1 = jnp.min(jnp.where(probs == p1, lane, LANES), axis=1, keepdims=True)
    rest = jnp.where(lane == i1, -1.0, probs)
    p2 = jnp.max(rest, axis=1, keepdims=True)
    i2 = jnp.min(jnp.where(rest == p2, lane, LANES), axis=1, keepdims=True)
    comb = jnp.where(lane == i1, p1, 0.0) + jnp.where(lane == i2, p2, 0.0)
    return comb / (p1 + p2)


def _merge_kernel(*refs, route):
    if route:
        (ya_ref, yb_ref, yc_ref, gp_ref, wb_ref, wo_ref, x_ref, mod_ref, nw_ref,
         rw_ref, rb_ref, xo_ref, h_ref, cmb_ref) = refs
    else:
        (ya_ref, yb_ref, yc_ref, gp_ref, wb_ref, wo_ref, x_ref, mod_ref, nw_ref,
         xo_ref, h_ref) = refs
    tm, d = x_ref.shape
    rb_rows = tm // MERGE_ROW_BLOCKS
    blocks = [slice(r * rb_rows, (r + 1) * rb_rows) for r in range(MERGE_ROW_BLOCKS)]
    m = mod_ref[0]
    branch = [[_dot(y_ref[rows, :], wb_ref[n]) for n, y_ref in enumerate((ya_ref, yb_ref, yc_ref))]
              for rows in blocks]
    merged = []
    for rows, terms in zip(blocks, branch):
        acc = None
        for n, term in enumerate(terms):
            gated = _sigmoid(gp_ref[rows, n * d:(n + 1) * d].astype(F32)) * term
            acc = gated if acc is None else acc + gated
        merged.append(acc.astype(BF16))
    outs = [_dot(mg, wo_ref[...]) for mg in merged]
    h2s = []
    for rows, out in zip(blocks, outs):
        xn = x_ref[rows, :] + m[2:3] * out
        xo_ref[rows, :] = xn
        h2 = _norm_mod(xn, nw_ref[...], m[4:5], m[3:4])
        h_ref[rows, :] = h2.astype(BF16)
        h2s.append(h2)
    if route:
        r_hi, r_mid, _ = _split3(rw_ref[...])
        for rows, h2 in zip(blocks, h2s):
            h_hi, h_mid, _ = _split3(h2)
            logits = _dot(h_hi, r_hi) + _dot(h_mid, r_hi) + _dot(h_hi, r_mid) + rb_ref[...]
            cmb_ref[rows, :] = _top2_combine(logits)


def _merge_call(ya, yb, yc, gp, wb, wo, x, mod, nw, router, seq, tm):
    t, d = x.shape
    bw = ya.shape[1]
    per_b = seq // tm
    route = router is not None
    tok = lambda w: pl.BlockSpec((tm, w), lambda i: (i, 0))
    const2 = lambda s: pl.BlockSpec(s, lambda i: (0, 0))
    in_specs = [tok(bw), tok(bw), tok(bw), tok(3 * d),
                pl.BlockSpec((3, bw, d), lambda i: (0, 0, 0)), const2((d, d)), tok(d),
                pl.BlockSpec((1, 6, d), lambda i: (i // per_b, 0, 0)), const2((1, d))]
    out_specs = [tok(d), tok(d)]
    out_shape = [jax.ShapeDtypeStruct((t, d), F32), jax.ShapeDtypeStruct((t, d), BF16)]
    args = [ya, yb, yc, gp, wb, wo, x, mod, nw]
    if route:
        in_specs += [const2((d, LANES)), const2((1, LANES))]
        out_specs.append(tok(LANES))
        out_shape.append(jax.ShapeDtypeStruct((t, LANES), F32))
        args += list(router)
    return pl.pallas_call(
        functools.partial(_merge_kernel, route=route),
        grid=(t // tm,),
        in_specs=in_specs, out_specs=out_specs, out_shape=out_shape,
        compiler_params=_cparams(("arbitrary",)),
        name="merge_out",
    )(*args)


def _finish(acc, x_ref, mod_ref, xo_ref, nxt):
    xn = x_ref[...] + mod_ref[0][5:6] * acc
    xo_ref[...] = xn
    if nxt is not None:
        nw_ref, modn_ref, hn_ref = nxt
        mn = modn_ref[0]
        hn_ref[...] = _norm_mod(xn, nw_ref[...], mn[1:2], mn[0:1]).astype(BF16)


def _ffn_kernel(*refs, has_next):
    if has_next:
        h_ref, w1_ref, w3_ref, w2_ref, x_ref, mod_ref, nw_ref, modn_ref, xo_ref, hn_ref, acc_ref = refs
        nxt = (nw_ref, modn_ref, hn_ref)
    else:
        h_ref, w1_ref, w3_ref, w2_ref, x_ref, mod_ref, xo_ref, acc_ref = refs
        nxt = None
    f = pl.program_id(1)

    @pl.when(f == 0)
    def _():
        acc_ref[...] = jnp.zeros(acc_ref.shape, F32)

    h = h_ref[...]
    a = _dot(h, w1_ref[...])
    act = a * _sigmoid(a) * _dot(h, w3_ref[...])
    acc_ref[...] += _dot(act.astype(BF16), w2_ref[...])

    @pl.when(f == pl.num_programs(1) - 1)
    def _():
        _finish(acc_ref[...], x_ref, mod_ref, xo_ref, nxt)


def _ffn_call(h, w1, w3, w2, x, mod, nxt, seq, tm, tf):
    t, d = x.shape
    ff = w1.shape[1]
    per_b = seq // tm
    tok = lambda: pl.BlockSpec((tm, d), lambda i, f: (i, 0))
    modspec = lambda: pl.BlockSpec((1, 6, d), lambda i, f: (i // per_b, 0, 0))
    in_specs = [tok(), pl.BlockSpec((d, tf), lambda i, f: (0, f)),
                pl.BlockSpec((d, tf), lambda i, f: (0, f)),
                pl.BlockSpec((tf, d), lambda i, f: (f, 0)), tok(), modspec()]
    out_specs = [tok()]
    out_shape = [jax.ShapeDtypeStruct((t, d), F32)]
    args = [h, w1, w3, w2, x, mod]
    if nxt is not None:
        in_specs += [pl.BlockSpec((1, d), lambda i, f: (0, 0)), modspec()]
        out_specs.append(tok())
        out_shape.append(jax.ShapeDtypeStruct((t, d), BF16))
        args += list(nxt)
    return pl.pallas_call(
        functools.partial(_ffn_kernel, has_next=nxt is not None),
        grid=(t // tm, ff // tf),
        in_specs=in_specs, out_specs=out_specs, out_shape=out_shape,
        scratch_shapes=[pltpu.VMEM((tm, d), F32)],
        compiler_params=_cparams(("arbitrary", "arbitrary")),
        name="ffn_swiglu",
    )(*args)


def _moe_kernel(h_ref, cmb_ref, w13_ref, w2_ref, y_ref,
                acc_ref, xg_ref, ya_ref, rk_ref, rkt_ref, cnt_ref):
    e = pl.program_id(1)
    f = pl.program_id(2)
    tm = h_ref.shape[0]

    @pl.when((e == 0) & (f == 0))
    def _():
        r = lax.broadcasted_iota(jnp.int32, (tm, tm), 0)
        c = lax.broadcasted_iota(jnp.int32, (tm, tm), 1)
        before = jnp.where(c < r, 1.0, 0.0).astype(BF16)
        sel = cmb_ref[...] > 0.0
        rank = _dot(before, jnp.where(sel, 1.0, 0.0).astype(BF16))
        rk = jnp.where(sel, rank, -1.0)
        rk_ref[...] = rk
        rkt_ref[...] = rk.T
        cnt_ref[...] = jnp.sum(jnp.where(sel, 1.0, 0.0), axis=0, keepdims=True)
        acc_ref[...] = jnp.zeros(acc_ref.shape, F32)

    lane1 = lax.broadcasted_iota(jnp.int32, (1, LANES), 1)
    n_e = jnp.sum(jnp.where(lane1 == e, cnt_ref[...], 0.0)).astype(jnp.int32)

    def loop(n_blocks, body):
        if isinstance(n_blocks, int):
            for sb in range(n_blocks):
                body(sb, 0)
        else:
            lax.fori_loop(0, n_blocks, body, 0)

    def run(cap, n_blocks):
        cap_pad = -(-cap // LANES) * LANES

        def row0(sb):
            return sb * cap if isinstance(sb, int) else pl.multiple_of(sb * cap, cap)

        @pl.when(f == 0)
        def _():
            rank_row = rkt_ref[pl.ds(e, 1), :]

            def gather(sb, carry):
                r0 = row0(sb)
                slot = (r0 + lax.broadcasted_iota(jnp.int32, (cap, tm), 0)).astype(F32)
                onehot = jnp.where(rank_row == slot, 1.0, 0.0).astype(BF16)
                xg_ref[pl.ds(r0, cap), :] = _dot(onehot, h_ref[...]).astype(BF16)
                return carry

            loop(n_blocks, gather)

        def expert(sb, carry):
            r0 = row0(sb)
            xs = xg_ref[pl.ds(r0, cap), :]
            ab = _dot(xs, w13_ref[0])
            tf = w2_ref.shape[1]
            a = ab[:, :tf]
            act = a * _sigmoid(a) * ab[:, tf:]
            part = _dot(act.astype(BF16), w2_ref[0])

            @pl.when(f == 0)
            def _():
                ya_ref[pl.ds(r0, cap), :] = part

            @pl.when(f > 0)
            def _():
                ya_ref[pl.ds(r0, cap), :] += part

            return carry

        loop(n_blocks, expert)

        @pl.when(f == pl.num_programs(2) - 1)
        def _():
            lane = lax.broadcasted_iota(jnp.int32, (tm, LANES), 1)
            rank_col = jnp.sum(jnp.where(lane == e, rk_ref[...], 0.0), axis=1, keepdims=True)
            w_col = jnp.sum(jnp.where(lane == e, cmb_ref[...], 0.0), axis=1, keepdims=True)
            col = lax.broadcasted_iota(jnp.int32, (tm, cap_pad), 1)

            def scatter(sb, carry):
                r0 = row0(sb)
                ys = ya_ref[pl.ds(r0, cap), :].astype(BF16)
                if cap_pad > cap:
                    ys = jnp.concatenate([ys, jnp.zeros((cap_pad - cap, ys.shape[1]), BF16)], axis=0)
                hit = (rank_col == (r0 + col).astype(F32)) & (col < cap)
                acc_ref[...] += w_col * _dot(jnp.where(hit, 1.0, 0.0).astype(BF16), ys)
                return carry

            loop(n_blocks, scatter)

    for idx, cap in enumerate(MOE_CAPS):
        lower = MOE_CAPS[idx - 1] if idx else 0
        if idx == len(MOE_CAPS) - 1:
            pl.when(n_e > lower)(functools.partial(run, cap, (n_e + cap - 1) // cap))
        else:
            pl.when((n_e > lower) & (n_e <= cap))(functools.partial(run, cap, 1))

    @pl.when((e == pl.num_programs(1) - 1) & (f == pl.num_programs(2) - 1))
    def _():
        y_ref[...] = acc_ref[...].astype(y_ref.dtype)


def _moe_pack_w13(w1, w3):
    ff = w1.shape[-1]
    tf = _pick(ff, (MOE_TF,))
    parts = []
    for f in range(ff // tf):
        parts += [w1[:, :, f * tf:(f + 1) * tf].astype(BF16), w3[:, :, f * tf:(f + 1) * tf].astype(BF16)]
    return jnp.concatenate(parts, axis=-1)


def _moe_call(h, cmb, w13, w2, tm):
    t, d = h.shape
    ne, ff, _ = w2.shape
    tf = _pick(ff, (MOE_TF,))
    rows = -(-tm // MOE_CAPS[-1]) * MOE_CAPS[-1]
    return pl.pallas_call(
        _moe_kernel,
        grid=(t // tm, ne, ff // tf),
        in_specs=[pl.BlockSpec((tm, d), lambda i, e, f: (i, 0)),
                  pl.BlockSpec((tm, LANES), lambda i, e, f: (i, 0)),
                  pl.BlockSpec((1, d, 2 * tf), lambda i, e, f: (e, 0, f)),
                  pl.BlockSpec((1, tf, d), lambda i, e, f: (e, f, 0))],
        out_specs=pl.BlockSpec((tm, d), lambda i, e, f: (i, 0)),
        out_shape=jax.ShapeDtypeStruct((t, d), BF16),
        scratch_shapes=[pltpu.VMEM((tm, d), F32),
                        pltpu.VMEM((rows, d), BF16),
                        pltpu.VMEM((rows, d), F32),
                        pltpu.VMEM((tm, LANES), F32),
                        pltpu.VMEM((LANES, tm), F32),
                        pltpu.VMEM((1, LANES), F32)],
        compiler_params=_cparams(("arbitrary", "arbitrary", "arbitrary")),
        name="moe_top2",
    )(h, cmb, w13, w2)


def _resid_kernel(*refs, has_next):
    if has_next:
        y_ref, x_ref, mod_ref, nw_ref, modn_ref, xo_ref, hn_ref = refs
        nxt = (nw_ref, modn_ref, hn_ref)
    else:
        y_ref, x_ref, mod_ref, xo_ref = refs
        nxt = None
    _finish(y_ref[...].astype(F32), x_ref, mod_ref, xo_ref, nxt)


def _resid_call(y, x, mod, nxt, seq, tm):
    t, d = x.shape
    per_b = seq // tm
    tok = lambda: pl.BlockSpec((tm, d), lambda i: (i, 0))
    modspec = lambda: pl.BlockSpec((1, 6, d), lambda i: (i // per_b, 0, 0))
    in_specs = [tok(), tok(), modspec()]
    out_specs = [tok()]
    out_shape = [jax.ShapeDtypeStruct((t, d), F32)]
    args = [y, x, mod]
    if nxt is not None:
        in_specs += [pl.BlockSpec((1, d), lambda i: (0, 0)), modspec()]
        out_specs.append(tok())
        out_shape.append(jax.ShapeDtypeStruct((t, d), BF16))
        args += list(nxt)
    return pl.pallas_call(
        functools.partial(_resid_kernel, has_next=nxt is not None),
        grid=(t // tm,),
        in_specs=in_specs, out_specs=out_specs, out_shape=out_shape,
        compiler_params=_cparams(("arbitrary",)),
        name="moe_residual",
    )(*args)


def _attn_col_perm():
    hq = DA_HEADS * DA_DK
    idx = []
    for base in (0, 2 * hq):
        for h in range(DA_HEADS):
            idx += list(range(base + h * DA_DK, base + (h + 1) * DA_DK))
            idx += list(range(base + hq + h * DA_DK, base + hq + (h + 1) * DA_DK))
    idx += list(range(4 * hq, 4 * hq + DA_HEADS * DA_DV))
    return np.asarray(idx, np.int32)


def _pick(n, pref):
    for c in pref:
        if n % c == 0:
            return c
    return n


def kernel(x, c, w_mod, b_mod, norm1_w, norm2_w, w_in, c_conv_w, c_conv_b, a_qnorm_w, a_knorm_w,
           a_lambda_q1, a_lambda_k1, a_lambda_q2, a_lambda_k2, a_subln_w, b_lb_logits, b_gnorm_w,
           c_igate_b, c_fgate_b, c_norm_w, w_branch, w_out, ffn_w1, ffn_w3, ffn_w2,
           moe_router_w, moe_router_b, moe_w1, moe_w3, moe_w2):
    batch, seq, d = x.shape
    depth = w_in.shape[0]
    t = batch * seq
    n_a = 4 * DA_HEADS * DA_DK + DA_HEADS * DA_DV
    n_b = 4 * HG_HEADS * HG_D
    n_c = 4 * ML_HEADS * ML_D + 2 * ML_HEADS
    n_c_pad = -(-n_c // LANES) * LANES
    tm = _pick(seq, (1024, 512, 256))
    tq = _pick(seq, (ATT_TQ,))
    ts_h = _pick(seq, (512, 256, 128))

    c8 = jnp.zeros((8, d), F32).at[:batch].set(c)
    mod = _mod_call(c8, w_mod, b_mod.reshape(depth, 1, 6 * d))[:, :batch].reshape(depth, batch, 6, d)

    lb_all = jnp.cumsum(jax.nn.softmax(b_lb_logits.astype(F32), axis=0), axis=0)
    lb_all = lb_all - lb_all[:1]
    slopes = jnp.asarray(_attn_slopes(), F32)

    xf = x.reshape(t, d)
    h = _normmod_call(xf, norm1_w[0].reshape(1, d), mod[0], seq, tm)
    perm_a = _attn_col_perm()
    w_bf = lax.optimization_barrier(w_in.astype(BF16))
    w_a_all = w_bf[:, :, :n_a][:, :, perm_a]
    w_b_all = w_bf[:, :, n_a:n_a + n_b]
    w_c_all = jnp.pad(w_bf[:, :, n_a + n_b:n_a + n_b + n_c], ((0, 0), (0, 0), (0, n_c_pad - n_c)))
    w_g_all = w_bf[:, :, n_a + n_b + n_c:]
    for l in range(depth):
        lam_init = 0.8 - 0.6 * math.exp(-0.3 * l)
        w_a, w_b, w_c, w_g = w_a_all[l], w_b_all[l], w_c_all[l], w_g_all[l]
        bias_c = jnp.zeros((1, n_c_pad), F32)
        bias_c = bias_c.at[0, n_c - 2 * ML_HEADS:n_c - ML_HEADS].set(c_igate_b[l])
        bias_c = bias_c.at[0, n_c - ML_HEADS:n_c].set(c_fgate_b[l])
        pc, pc_gates = _matmul_tail(h, w_c, bias_c, tm, n_c_pad - (n_c - 2 * ML_HEADS))

        qw = (jnp.tile(a_qnorm_w[l], 2 * DA_HEADS) * (DA_DK ** -0.5 * LOG2E)).reshape(1, -1)
        kw = jnp.tile(a_knorm_w[l], 2 * DA_HEADS).reshape(1, -1)
        kn, qt, vt = _attn_prep_call(h, w_a, qw, kw, batch, seq, _pick(seq, (512, 256)))
        lam = (jnp.exp(jnp.sum(a_lambda_q1[l] * a_lambda_k1[l]))
               - jnp.exp(jnp.sum(a_lambda_q2[l] * a_lambda_k2[l])) + lam_init)
        scal = jnp.concatenate([slopes, jnp.stack([lam, jnp.asarray(1.0 - lam_init, F32)])]).astype(F32)
        y_a = _attn_call(scal, qt, kn, vt, a_subln_w[l].reshape(1, -1), batch, seq, tq)

        gates = pc_gates[:, :2 * ML_HEADS].reshape(batch, seq, 2, ML_HEADS)
        gates = gates.transpose(2, 0, 3, 1).reshape(2, batch, ML_HEADS, seq // ML_CHUNK, ML_CHUNK)
        y_c, pg, pb = _mlstm_call(pc, gates[0], gates[1], c_conv_w[l], c_conv_b[l].reshape(1, -1),
                                  c_norm_w[l].reshape(1, -1), h, w_g, w_b, batch, seq)

        lb = lb_all[l].reshape(HG_HEADS, 1, HG_D)
        lbs = jnp.concatenate([jnp.log(lb), jnp.log1p(-lb)], axis=1)
        y_b = _hgrn_call(pb, lbs, b_gnorm_w[l].reshape(1, -1), batch, seq, ts_h)

        dense = l % 2 == 0
        router = None
        if not dense:
            rw = jnp.pad(moe_router_w[l // 2], ((0, 0), (0, LANES - N_EXPERTS)))
            rb = jnp.pad(moe_router_b[l // 2], (0, LANES - N_EXPERTS)).reshape(1, LANES)
            router = (rw, rb)
        outs = _merge_call(y_a, y_b, y_c, pg, w_branch[l].astype(BF16), w_out[l].astype(BF16), xf,
                           mod[l], norm2_w[l].reshape(1, d), router, seq, 512)
        xf, h2 = outs[0], outs[1]
        nxt = None if l == depth - 1 else (norm1_w[l + 1].reshape(1, d), mod[l + 1])
        if dense:
            res = _ffn_call(h2, ffn_w1[l // 2].astype(BF16), ffn_w3[l // 2].astype(BF16),
                            ffn_w2[l // 2].astype(BF16), xf, mod[l], nxt, seq, tm, FFN_TF)
        else:
            y = _moe_call(h2, outs[2], _moe_pack_w13(moe_w1[l // 2], moe_w3[l // 2]),
                          moe_w2[l // 2].astype(BF16), tm)
            res = _resid_call(y, xf, mod[l], nxt, seq, tm)
        xf = res[0]
        if nxt is not None:
            h = res[1]
    return xf.reshape(batch, seq, d)
```
